```python
import math
import jax, jax.numpy as jnp
from jax import lax
import numpy as np

D_MODEL = 1024
BATCH = 8
SEQ = 4096
DEPTH = 2

HEAD_DIM = 64
A_GROUPS = ((128, 1), (512, 4), (2048, 16))
A_HEADS_PER_GROUP = 4
A_N_GROUPS = len(A_GROUPS)
A_WIDTH = A_N_GROUPS * A_HEADS_PER_GROUP * HEAD_DIM
A_OUT = A_HEADS_PER_GROUP * HEAD_DIM
BLK = 128
B_CHUNK = 128
B_GROUPS = 4
B_WIDTH = 512
B_GROUP_DIM = B_WIDTH // B_GROUPS
C_WINDOWS = (2, 4, 8, 16)
C_GROUPS = len(C_WINDOWS)
C_WIDTH = 512
C_GROUP_DIM = C_WIDTH // C_GROUPS
N_BRANCH = 3
D_FF = 2816
EPS = 1e-6

Q_OFF = 0
K_OFF = Q_OFF + A_WIDTH
V_OFF = K_OFF + A_WIDTH
UB_OFF = V_OFF + A_WIDTH
VB_OFF = UB_OFF + B_WIDTH
C_OFF = VB_OFF + B_WIDTH
G_OFF = C_OFF + C_WIDTH
IN_WIDTH = G_OFF + N_BRANCH * D_MODEL

kernel_name = "hybrid_dilated_gmlp_pool_macaron"


def rms_norm(x, g):
    xf = x.astype(jnp.float32)
    y = xf * lax.rsqrt(jnp.mean(xf * xf, axis=-1, keepdims=True) + EPS)
    return (y * g.astype(jnp.float32)).astype(x.dtype)


def layer_norm(x, g, b):
    xf = x.astype(jnp.float32)
    mu = jnp.mean(xf, axis=-1, keepdims=True)
    xc = xf - mu
    y = xc * lax.rsqrt(jnp.mean(xc * xc, axis=-1, keepdims=True) + EPS)
    return (y * g.astype(jnp.float32) + b.astype(jnp.float32)).astype(x.dtype)


def swiglu(h, w_gate, w_up, w_down):
    return (jax.nn.silu(h @ w_gate) * (h @ w_up)) @ w_down


def dilated_window_attention(q, k, v, window, dil):
    Bn, S, H, Dh = q.shape
    span = window // dil
    unit = dil * BLK
    S_pad = -(-S // unit) * unit
    L = S_pad // dil
    nb = L // BLK

    def to_strided(t):
        t = jnp.pad(t, ((0, 0), (0, S_pad - S), (0, 0), (0, 0)))
        t = t.reshape(Bn, L, dil, H, Dh).transpose(0, 2, 3, 1, 4)
        return t.reshape(Bn, dil, H, nb, BLK, Dh)

    qs, ks, vs = to_strided(q), to_strided(k), to_strided(v)
    prev = lambda t: jnp.pad(t, ((0, 0), (0, 0), (0, 0), (1, 0), (0, 0), (0, 0)))[:, :, :, :-1]
    kb = jnp.concatenate([prev(ks), ks], axis=4)
    vb = jnp.concatenate([prev(vs), vs], axis=4)

    qi = jnp.arange(BLK)[:, None] + BLK
    kj = jnp.arange(2 * BLK)[None, :]
    dist = qi - kj
    band = (dist >= 0) & (dist <= span)
    blk = jnp.arange(nb)[:, None, None]
    valid = band[None] & ~((blk == 0) & (kj[None] < BLK))

    scale = 1.0 / math.sqrt(Dh)
    s = jnp.einsum('brhnqc,brhnkc->brhnqk', qs, kb,
                   preferred_element_type=jnp.float32) * scale
    s = jnp.where(valid, s, -1e30)
    m = jnp.max(s, axis=-1, keepdims=True)
    p = jnp.exp(s - m)
    l = jnp.sum(p, axis=-1, keepdims=True)
    o = jnp.einsum('brhnqk,brhnkc->brhnqc', p, vb.astype(jnp.float32)) / l
    lse = (m + jnp.log(l))[..., 0]

    o = o.reshape(Bn, dil, H, L, Dh).transpose(0, 3, 1, 2, 4).reshape(Bn, S_pad, H, Dh)[:, :S]
    lse = lse.reshape(Bn, dil, H, L).transpose(0, 3, 1, 2).reshape(Bn, S_pad, H)[:, :S]
    return o, lse


def dilated_attention_mixture(q, k, v):
    outs, lses = [], []
    for g, (window, dil) in enumerate(A_GROUPS):
        o, lse = dilated_window_attention(q[:, :, g], k[:, :, g], v[:, :, g], window, dil)
        outs.append(o)
        lses.append(lse)
    w = jax.nn.softmax(jnp.stack(lses, axis=0), axis=0)
    o = jnp.sum(w[..., None] * jnp.stack(outs, axis=0), axis=0)
    Bn, S = o.shape[:2]
    return o.reshape(Bn, S, A_OUT).astype(q.dtype)


def spatial_gating(u, v, w_s, b_s, ln_g, ln_b):
    v = layer_norm(v, ln_g, ln_b)
    Bn, S, _ = v.shape
    nc = S // B_CHUNK
    vc = v.reshape(Bn, nc, B_CHUNK, B_GROUPS, B_GROUP_DIM)
    causal = jnp.tril(jnp.ones((B_CHUNK, B_CHUNK), dtype=bool))
    w = jnp.where(causal[None], w_s, 0.0).astype(v.dtype)
    mixed = jnp.einsum('gts,bnsgc->bntgc', w, vc) + b_s.T[None, None, :, :, None]
    return u * mixed.reshape(Bn, S, B_WIDTH)


def multiscale_pool(xc, w_c, scale_c):
    Bn, S, _ = xc.shape
    xg = xc.reshape(Bn, S, C_GROUPS, C_GROUP_DIM).astype(jnp.float32)
    cs = jnp.cumsum(xg, axis=1)
    t1 = jnp.arange(1, S + 1, dtype=jnp.float32)
    pooled = []
    for gi, w in enumerate(C_WINDOWS):
        c = cs[:, :, gi]
        shifted = jnp.pad(c, ((0, 0), (w, 0), (0, 0)))[:, :S]
        cnt = jnp.minimum(t1, float(w))[None, :, None]
        pooled.append((c - shifted) / cnt - xg[:, :, gi])
    pooled = jnp.stack(pooled, axis=2)
    y = jnp.einsum('bsgc,gcd->bsgd', pooled, w_c.astype(jnp.float32)) * scale_c.astype(jnp.float32)
    return y.reshape(Bn, S, C_WIDTH).astype(xc.dtype)


def _fwd_setup_inputs(seed: int = 0) -> dict:
    key = jax.random.key(seed)
    ks = jax.random.split(key, 32)
    f32 = jnp.float32
    nrm = lambda k, shape, fan_in: jax.random.normal(k, shape, f32) * (fan_in ** -0.5)
    gain = lambda k, shape: 1.0 + 0.05 * jax.random.normal(k, shape, f32)
    small = lambda k, shape: 0.01 * jax.random.normal(k, shape, f32)
    L = DEPTH
    return {
        "x": jax.random.normal(ks[0], (BATCH, SEQ, D_MODEL), f32),
        "ffn1_norm": gain(ks[1], (L, D_MODEL)),
        "ffn1_w_gate": nrm(ks[2], (L, D_MODEL, D_FF), D_MODEL),
        "ffn1_w_up": nrm(ks[3], (L, D_MODEL, D_FF), D_MODEL),
        "ffn1_w_down": nrm(ks[4], (L, D_FF, D_MODEL), D_FF),
        "mix_norm": gain(ks[5], (L, D_MODEL)),
        "w_in": nrm(ks[6], (L, D_MODEL, IN_WIDTH), D_MODEL),
        "b_gate": small(ks[7], (L, N_BRANCH * D_MODEL)),
        "b_ln_g": gain(ks[8], (L, B_WIDTH)),
        "b_ln_b": small(ks[9], (L, B_WIDTH)),
        "b_w_s": nrm(ks[10], (L, B_GROUPS, B_CHUNK, B_CHUNK), B_CHUNK),
        "b_b_s": gain(ks[11], (L, B_GROUPS, B_CHUNK)),
        "c_w": nrm(ks[12], (L, C_GROUPS, C_GROUP_DIM, C_GROUP_DIM), C_GROUP_DIM),
        "c_scale": gain(ks[13], (L, C_GROUPS, C_GROUP_DIM)),
        "w_proj_a": nrm(ks[14], (L, A_OUT, D_MODEL), A_OUT),
        "w_proj_b": nrm(ks[15], (L, B_WIDTH, D_MODEL), B_WIDTH),
        "w_proj_c": nrm(ks[16], (L, C_WIDTH, D_MODEL), C_WIDTH),
        "w_out": nrm(ks[17], (L, D_MODEL, D_MODEL), D_MODEL),
        "ffn2_norm": gain(ks[18], (L, D_MODEL)),
        "ffn2_w_gate": nrm(ks[19], (L, D_MODEL, D_FF), D_MODEL),
        "ffn2_w_up": nrm(ks[20], (L, D_MODEL, D_FF), D_MODEL),
        "ffn2_w_down": nrm(ks[21], (L, D_FF, D_MODEL), D_FF),
        "final_norm": gain(ks[22], (D_MODEL,)),
    }


def _fwd_reference(x, ffn1_norm, ffn1_w_gate, ffn1_w_up, ffn1_w_down, mix_norm, w_in, b_gate,
              b_ln_g, b_ln_b, b_w_s, b_b_s, c_w, c_scale, w_proj_a, w_proj_b, w_proj_c,
              w_out, ffn2_norm, ffn2_w_gate, ffn2_w_up, ffn2_w_down, final_norm):
    Bn, S, D = x.shape
    for i in range(DEPTH):
        x = x + 0.5 * swiglu(rms_norm(x, ffn1_norm[i]), ffn1_w_gate[i], ffn1_w_up[i], ffn1_w_down[i])

        h = rms_norm(x, mix_norm[i])
        z = h @ w_in[i]
        qkv_shape = (Bn, S, A_N_GROUPS, A_HEADS_PER_GROUP, HEAD_DIM)
        q = z[..., Q_OFF:K_OFF].reshape(qkv_shape)
        k = z[..., K_OFF:V_OFF].reshape(qkv_shape)
        v = z[..., V_OFF:UB_OFF].reshape(qkv_shape)
        y_a = dilated_attention_mixture(q, k, v)

        uv = jax.nn.gelu(z[..., UB_OFF:C_OFF], approximate=False)
        y_b = spatial_gating(uv[..., :B_WIDTH], uv[..., B_WIDTH:], b_w_s[i], b_b_s[i],
                             b_ln_g[i], b_ln_b[i])

        y_c = multiscale_pool(z[..., C_OFF:G_OFF], c_w[i], c_scale[i])

        gates = jax.nn.sigmoid(z[..., G_OFF:] + b_gate[i]).reshape(Bn, S, N_BRANCH, D)
        merged = (gates[:, :, 0] * (y_a @ w_proj_a[i])
                  + gates[:, :, 1] * (y_b @ w_proj_b[i])
                  + gates[:, :, 2] * (y_c @ w_proj_c[i]))
        x = x + merged @ w_out[i]

        x = x + 0.5 * swiglu(rms_norm(x, ffn2_norm[i]), ffn2_w_gate[i], ffn2_w_up[i], ffn2_w_down[i])
    return rms_norm(x, final_norm)


import jax as _jax
import jax.numpy as _jnp

TWIN_FORMAT = 'train_step'
FWD_PARAMS = ['x', 'ffn1_norm', 'ffn1_w_gate', 'ffn1_w_up', 'ffn1_w_down', 'mix_norm', 'w_in', 'b_gate', 'b_ln_g', 'b_ln_b', 'b_w_s', 'b_b_s', 'c_w', 'c_scale', 'w_proj_a', 'w_proj_b', 'w_proj_c', 'w_out', 'ffn2_norm', 'ffn2_w_gate', 'ffn2_w_up', 'ffn2_w_down', 'final_norm']
TWIN_WEIGHTS = ['ffn1_norm', 'ffn1_w_gate', 'ffn1_w_up', 'ffn1_w_down', 'mix_norm', 'w_in', 'b_gate', 'b_ln_g', 'b_ln_b', 'b_w_s', 'b_b_s', 'c_w', 'c_scale', 'w_proj_a', 'w_proj_b', 'w_proj_c', 'w_out', 'ffn2_norm', 'ffn2_w_gate', 'ffn2_w_up', 'ffn2_w_down', 'final_norm']
TWIN_DIFF_INPUT = 'x'
TWIN_INPUTS = ['x', 'ffn1_norm', 'ffn1_w_gate', 'ffn1_w_up', 'ffn1_w_down', 'mix_norm', 'w_in', 'b_gate', 'b_ln_g', 'b_ln_b', 'b_w_s', 'b_b_s', 'c_w', 'c_scale', 'w_proj_a', 'w_proj_b', 'w_proj_c', 'w_out', 'ffn2_norm', 'ffn2_w_gate', 'ffn2_w_up', 'ffn2_w_down', 'final_norm', 'loss_target', 'm_ffn1_norm', 'm_ffn1_w_gate', 'm_ffn1_w_up', 'm_ffn1_w_down', 'm_mix_norm', 'm_w_in', 'm_b_gate', 'm_b_ln_g', 'm_b_ln_b', 'm_b_w_s', 'm_b_b_s', 'm_c_w', 'm_c_scale', 'm_w_proj_a', 'm_w_proj_b', 'm_w_proj_c', 'm_w_out', 'm_ffn2_norm', 'm_ffn2_w_gate', 'm_ffn2_w_up', 'm_ffn2_w_down', 'm_final_norm', 'v_ffn1_norm', 'v_ffn1_w_gate', 'v_ffn1_w_up', 'v_ffn1_w_down', 'v_mix_norm', 'v_w_in', 'v_b_gate', 'v_b_ln_g', 'v_b_ln_b', 'v_b_w_s', 'v_b_b_s', 'v_c_w', 'v_c_scale', 'v_w_proj_a', 'v_w_proj_b', 'v_w_proj_c', 'v_w_out', 'v_ffn2_norm', 'v_ffn2_w_gate', 'v_ffn2_w_up', 'v_ffn2_w_down', 'v_final_norm']
TWIN_OUTPUTS = ['loss', 'grad_x', 'grad_ffn1_norm', 'grad_ffn1_w_gate', 'grad_ffn1_w_up', 'grad_ffn1_w_down', 'grad_mix_norm', 'grad_w_in', 'grad_b_gate', 'grad_b_ln_g', 'grad_b_ln_b', 'grad_b_w_s', 'grad_b_b_s', 'grad_c_w', 'grad_c_scale', 'grad_w_proj_a', 'grad_w_proj_b', 'grad_w_proj_c', 'grad_w_out', 'grad_ffn2_norm', 'grad_ffn2_w_gate', 'grad_ffn2_w_up', 'grad_ffn2_w_down', 'grad_final_norm', 'delta_ffn1_norm', 'delta_ffn1_w_gate', 'delta_ffn1_w_up', 'delta_ffn1_w_down', 'delta_mix_norm', 'delta_w_in', 'delta_b_gate', 'delta_b_ln_g', 'delta_b_ln_b', 'delta_b_w_s', 'delta_b_b_s', 'delta_c_w', 'delta_c_scale', 'delta_w_proj_a', 'delta_w_proj_b', 'delta_w_proj_c', 'delta_w_out', 'delta_ffn2_norm', 'delta_ffn2_w_gate', 'delta_ffn2_w_up', 'delta_ffn2_w_down', 'delta_final_norm', 'new_m_ffn1_norm', 'new_m_ffn1_w_gate', 'new_m_ffn1_w_up', 'new_m_ffn1_w_down', 'new_m_mix_norm', 'new_m_w_in', 'new_m_b_gate', 'new_m_b_ln_g', 'new_m_b_ln_b', 'new_m_b_w_s', 'new_m_b_b_s', 'new_m_c_w', 'new_m_c_scale', 'new_m_w_proj_a', 'new_m_w_proj_b', 'new_m_w_proj_c', 'new_m_w_out', 'new_m_ffn2_norm', 'new_m_ffn2_w_gate', 'new_m_ffn2_w_up', 'new_m_ffn2_w_down', 'new_m_final_norm', 'new_v_ffn1_norm', 'new_v_ffn1_w_gate', 'new_v_ffn1_w_up', 'new_v_ffn1_w_down', 'new_v_mix_norm', 'new_v_w_in', 'new_v_b_gate', 'new_v_b_ln_g', 'new_v_b_ln_b', 'new_v_b_w_s', 'new_v_b_b_s', 'new_v_c_w', 'new_v_c_scale', 'new_v_w_proj_a', 'new_v_w_proj_b', 'new_v_w_proj_c', 'new_v_w_out', 'new_v_ffn2_norm', 'new_v_ffn2_w_gate', 'new_v_ffn2_w_up', 'new_v_ffn2_w_down', 'new_v_final_norm']
TWIN_LEAF_KINDS = {'loss': 'loss', 'grad_x': 'grad_x', 'grad_ffn1_norm': 'grad_w', 'grad_ffn1_w_gate': 'grad_w', 'grad_ffn1_w_up': 'grad_w', 'grad_ffn1_w_down': 'grad_w', 'grad_mix_norm': 'grad_w', 'grad_w_in': 'grad_w', 'grad_b_gate': 'grad_w', 'grad_b_ln_g': 'grad_w', 'grad_b_ln_b': 'grad_w', 'grad_b_w_s': 'grad_w', 'grad_b_b_s': 'grad_w', 'grad_c_w': 'grad_w', 'grad_c_scale': 'grad_w', 'grad_w_proj_a': 'grad_w', 'grad_w_proj_b': 'grad_w', 'grad_w_proj_c': 'grad_w', 'grad_w_out': 'grad_w', 'grad_ffn2_norm': 'grad_w', 'grad_ffn2_w_gate': 'grad_w', 'grad_ffn2_w_up': 'grad_w', 'grad_ffn2_w_down': 'grad_w', 'grad_final_norm': 'grad_w', 'delta_ffn1_norm': 'delta_w', 'delta_ffn1_w_gate': 'delta_w', 'delta_ffn1_w_up': 'delta_w', 'delta_ffn1_w_down': 'delta_w', 'delta_mix_norm': 'delta_w', 'delta_w_in': 'delta_w', 'delta_b_gate': 'delta_w', 'delta_b_ln_g': 'delta_w', 'delta_b_ln_b': 'delta_w', 'delta_b_w_s': 'delta_w', 'delta_b_b_s': 'delta_w', 'delta_c_w': 'delta_w', 'delta_c_scale': 'delta_w', 'delta_w_proj_a': 'delta_w', 'delta_w_proj_b': 'delta_w', 'delta_w_proj_c': 'delta_w', 'delta_w_out': 'delta_w', 'delta_ffn2_norm': 'delta_w', 'delta_ffn2_w_gate': 'delta_w', 'delta_ffn2_w_up': 'delta_w', 'delta_ffn2_w_down': 'delta_w', 'delta_final_norm': 'delta_w', 'new_m_ffn1_norm': 'new_m', 'new_m_ffn1_w_gate': 'new_m', 'new_m_ffn1_w_up': 'new_m', 'new_m_ffn1_w_down': 'new_m', 'new_m_mix_norm': 'new_m', 'new_m_w_in': 'new_m', 'new_m_b_gate': 'new_m', 'new_m_b_ln_g': 'new_m', 'new_m_b_ln_b': 'new_m', 'new_m_b_w_s': 'new_m', 'new_m_b_b_s': 'new_m', 'new_m_c_w': 'new_m', 'new_m_c_scale': 'new_m', 'new_m_w_proj_a': 'new_m', 'new_m_w_proj_b': 'new_m', 'new_m_w_proj_c': 'new_m', 'new_m_w_out': 'new_m', 'new_m_ffn2_norm': 'new_m', 'new_m_ffn2_w_gate': 'new_m', 'new_m_ffn2_w_up': 'new_m', 'new_m_ffn2_w_down': 'new_m', 'new_m_final_norm': 'new_m', 'new_v_ffn1_norm': 'new_v', 'new_v_ffn1_w_gate': 'new_v', 'new_v_ffn1_w_up': 'new_v', 'new_v_ffn1_w_down': 'new_v', 'new_v_mix_norm': 'new_v', 'new_v_w_in': 'new_v', 'new_v_b_gate': 'new_v', 'new_v_b_ln_g': 'new_v', 'new_v_b_ln_b': 'new_v', 'new_v_b_w_s': 'new_v', 'new_v_b_b_s': 'new_v', 'new_v_c_w': 'new_v', 'new_v_c_scale': 'new_v', 'new_v_w_proj_a': 'new_v', 'new_v_w_proj_b': 'new_v', 'new_v_w_proj_c': 'new_v', 'new_v_w_out': 'new_v', 'new_v_ffn2_norm': 'new_v', 'new_v_ffn2_w_gate': 'new_v', 'new_v_ffn2_w_up': 'new_v', 'new_v_ffn2_w_down': 'new_v', 'new_v_final_norm': 'new_v'}


def _forward(args):
    return _fwd_reference(*[args[k] for k in FWD_PARAMS])


def _output_shape():
    out = _jax.eval_shape(lambda: _forward(_fwd_setup_inputs(0)))
    return out.shape, out.dtype

N_MICROBATCH = 1
ADAM_LR = 0.001
ADAM_B1 = 0.9
ADAM_B2 = 0.999
ADAM_EPS = 1e-08
ADAM_WD = 0.01
ADAM_STEP = 10
PER_EXAMPLE_BATCH_AXIS = {'x': 0, 'loss_target': 0}
SHARED_INPUTS = []
_WEIGHT_DTYPES = {'ffn1_norm': _jnp.float32, 'ffn1_w_gate': _jnp.float32, 'ffn1_w_up': _jnp.float32, 'ffn1_w_down': _jnp.float32, 'mix_norm': _jnp.float32, 'w_in': _jnp.float32, 'b_gate': _jnp.float32, 'b_ln_g': _jnp.float32, 'b_ln_b': _jnp.float32, 'b_w_s': _jnp.float32, 'b_b_s': _jnp.float32, 'c_w': _jnp.float32, 'c_scale': _jnp.float32, 'w_proj_a': _jnp.float32, 'w_proj_b': _jnp.float32, 'w_proj_c': _jnp.float32, 'w_out': _jnp.float32, 'ffn2_norm': _jnp.float32, 'ffn2_w_gate': _jnp.float32, 'ffn2_w_up': _jnp.float32, 'ffn2_w_down': _jnp.float32, 'final_norm': _jnp.float32}
MOMENT_SCALE = {'ffn1_norm': 7.997616e-02, 'ffn1_w_gate': 3.288925e-02, 'ffn1_w_up': 3.187587e-02, 'ffn1_w_down': 5.279451e-02, 'mix_norm': 1.125319e-01, 'w_in': 4.330438e-02, 'b_gate': 2.277821e-02, 'b_ln_g': 5.171367e-02, 'b_ln_b': 5.163875e-02, 'b_w_s': 5.089355e-02, 'b_b_s': 7.270754e-02, 'c_w': 9.994329e-02, 'c_scale': 1.077368e-01, 'w_proj_a': 1.573593e-02, 'w_proj_b': 6.754013e-02, 'w_proj_c': 7.111158e-02, 'w_out': 9.836444e-02, 'ffn2_norm': 6.379215e-02, 'ffn2_w_gate': 2.575399e-02, 'ffn2_w_up': 2.516467e-02, 'ffn2_w_down': 4.168947e-02, 'final_norm': 3.201890e+01}


def _to_microbatches(a, axis):
    t = _jnp.moveaxis(a, axis, 0)
    t = t.reshape((N_MICROBATCH, t.shape[0] // N_MICROBATCH) + t.shape[1:])
    return _jnp.moveaxis(t, 1, axis + 1)


def setup_inputs(seed: int = 0) -> dict:
    inp = _fwd_setup_inputs(seed)
    key = _jax.random.fold_in(_jax.random.key(seed), 7919)
    shape, _ = _output_shape()
    out = dict(inp)
    out["loss_target"] = _jax.random.normal(_jax.random.fold_in(key, 0), shape, _jnp.float32)
    for i, name in enumerate(TWIN_WEIGHTS):
        w = inp[name].astype(_jnp.float32)
        if MOMENT_SCALE is None:
            s = _jnp.sqrt(_jnp.mean(_jnp.square(w)) + 1e-30)
        else:
            s = MOMENT_SCALE[name]
        km, kv = _jax.random.split(_jax.random.fold_in(key, i + 1))
        out[name] = w
        out["m_" + name] = s * _jax.random.normal(km, w.shape, _jnp.float32)
        out["v_" + name] = (s * s) * _jax.random.uniform(kv, w.shape, _jnp.float32, 0.5, 1.5)
    if N_MICROBATCH > 1:
        for name, axis in PER_EXAMPLE_BATCH_AXIS.items():
            out[name] = _to_microbatches(out[name], axis)
    return {'x': out['x'], 'ffn1_norm': out['ffn1_norm'], 'ffn1_w_gate': out['ffn1_w_gate'], 'ffn1_w_up': out['ffn1_w_up'], 'ffn1_w_down': out['ffn1_w_down'], 'mix_norm': out['mix_norm'], 'w_in': out['w_in'], 'b_gate': out['b_gate'], 'b_ln_g': out['b_ln_g'], 'b_ln_b': out['b_ln_b'], 'b_w_s': out['b_w_s'], 'b_b_s': out['b_b_s'], 'c_w': out['c_w'], 'c_scale': out['c_scale'], 'w_proj_a': out['w_proj_a'], 'w_proj_b': out['w_proj_b'], 'w_proj_c': out['w_proj_c'], 'w_out': out['w_out'], 'ffn2_norm': out['ffn2_norm'], 'ffn2_w_gate': out['ffn2_w_gate'], 'ffn2_w_up': out['ffn2_w_up'], 'ffn2_w_down': out['ffn2_w_down'], 'final_norm': out['final_norm'], 'loss_target': out['loss_target'], 'm_ffn1_norm': out['m_ffn1_norm'], 'm_ffn1_w_gate': out['m_ffn1_w_gate'], 'm_ffn1_w_up': out['m_ffn1_w_up'], 'm_ffn1_w_down': out['m_ffn1_w_down'], 'm_mix_norm': out['m_mix_norm'], 'm_w_in': out['m_w_in'], 'm_b_gate': out['m_b_gate'], 'm_b_ln_g': out['m_b_ln_g'], 'm_b_ln_b': out['m_b_ln_b'], 'm_b_w_s': out['m_b_w_s'], 'm_b_b_s': out['m_b_b_s'], 'm_c_w': out['m_c_w'], 'm_c_scale': out['m_c_scale'], 'm_w_proj_a': out['m_w_proj_a'], 'm_w_proj_b': out['m_w_proj_b'], 'm_w_proj_c': out['m_w_proj_c'], 'm_w_out': out['m_w_out'], 'm_ffn2_norm': out['m_ffn2_norm'], 'm_ffn2_w_gate': out['m_ffn2_w_gate'], 'm_ffn2_w_up': out['m_ffn2_w_up'], 'm_ffn2_w_down': out['m_ffn2_w_down'], 'm_final_norm': out['m_final_norm'], 'v_ffn1_norm': out['v_ffn1_norm'], 'v_ffn1_w_gate': out['v_ffn1_w_gate'], 'v_ffn1_w_up': out['v_ffn1_w_up'], 'v_ffn1_w_down': out['v_ffn1_w_down'], 'v_mix_norm': out['v_mix_norm'], 'v_w_in': out['v_w_in'], 'v_b_gate': out['v_b_gate'], 'v_b_ln_g': out['v_b_ln_g'], 'v_b_ln_b': out['v_b_ln_b'], 'v_b_w_s': out['v_b_w_s'], 'v_b_b_s': out['v_b_b_s'], 'v_c_w': out['v_c_w'], 'v_c_scale': out['v_c_scale'], 'v_w_proj_a': out['v_w_proj_a'], 'v_w_proj_b': out['v_w_proj_b'], 'v_w_proj_c': out['v_w_proj_c'], 'v_w_out': out['v_w_out'], 'v_ffn2_norm': out['v_ffn2_norm'], 'v_ffn2_w_gate': out['v_ffn2_w_gate'], 'v_ffn2_w_up': out['v_ffn2_w_up'], 'v_ffn2_w_down': out['v_ffn2_w_down'], 'v_final_norm': out['v_final_norm']}


def _loss(weights, diff, rest, loss_target):
    with _jax.named_scope("forward"):
        args = {**rest, TWIN_DIFF_INPUT: diff, **{k: w.astype(_WEIGHT_DTYPES[k]) for k, w in weights.items()}}
        y = _forward(args)
    with _jax.named_scope("loss_head"):
        err = _jnp.square(y.astype(_jnp.float32) - loss_target)
        return 0.5 * _jnp.sum(_jnp.mean(err, axis=-1)) if err.ndim else 0.5 * err


def _adamw(w, g, m, v):
    m = ADAM_B1 * m + (1.0 - ADAM_B1) * g
    v = ADAM_B2 * v + (1.0 - ADAM_B2) * _jnp.square(g)
    m_hat = m / (1.0 - ADAM_B1 ** ADAM_STEP)
    v_hat = v / (1.0 - ADAM_B2 ** ADAM_STEP)
    delta = -ADAM_LR * (m_hat / (_jnp.sqrt(v_hat) + ADAM_EPS) + ADAM_WD * w)
    return delta, m, v


def reference(x, ffn1_norm, ffn1_w_gate, ffn1_w_up, ffn1_w_down, mix_norm, w_in, b_gate, b_ln_g, b_ln_b, b_w_s, b_b_s, c_w, c_scale, w_proj_a, w_proj_b, w_proj_c, w_out, ffn2_norm, ffn2_w_gate, ffn2_w_up, ffn2_w_down, final_norm, loss_target, m_ffn1_norm, m_ffn1_w_gate, m_ffn1_w_up, m_ffn1_w_down, m_mix_norm, m_w_in, m_b_gate, m_b_ln_g, m_b_ln_b, m_b_w_s, m_b_b_s, m_c_w, m_c_scale, m_w_proj_a, m_w_proj_b, m_w_proj_c, m_w_out, m_ffn2_norm, m_ffn2_w_gate, m_ffn2_w_up, m_ffn2_w_down, m_final_norm, v_ffn1_norm, v_ffn1_w_gate, v_ffn1_w_up, v_ffn1_w_down, v_mix_norm, v_w_in, v_b_gate, v_b_ln_g, v_b_ln_b, v_b_w_s, v_b_b_s, v_c_w, v_c_scale, v_w_proj_a, v_w_proj_b, v_w_proj_c, v_w_out, v_ffn2_norm, v_ffn2_w_gate, v_ffn2_w_up, v_ffn2_w_down, v_final_norm):
    given = dict(x=x, ffn1_norm=ffn1_norm, ffn1_w_gate=ffn1_w_gate, ffn1_w_up=ffn1_w_up, ffn1_w_down=ffn1_w_down, mix_norm=mix_norm, w_in=w_in, b_gate=b_gate, b_ln_g=b_ln_g, b_ln_b=b_ln_b, b_w_s=b_w_s, b_b_s=b_b_s, c_w=c_w, c_scale=c_scale, w_proj_a=w_proj_a, w_proj_b=w_proj_b, w_proj_c=w_proj_c, w_out=w_out, ffn2_norm=ffn2_norm, ffn2_w_gate=ffn2_w_gate, ffn2_w_up=ffn2_w_up, ffn2_w_down=ffn2_w_down, final_norm=final_norm, loss_target=loss_target, m_ffn1_norm=m_ffn1_norm, m_ffn1_w_gate=m_ffn1_w_gate, m_ffn1_w_up=m_ffn1_w_up, m_ffn1_w_down=m_ffn1_w_down, m_mix_norm=m_mix_norm, m_w_in=m_w_in, m_b_gate=m_b_gate, m_b_ln_g=m_b_ln_g, m_b_ln_b=m_b_ln_b, m_b_w_s=m_b_w_s, m_b_b_s=m_b_b_s, m_c_w=m_c_w, m_c_scale=m_c_scale, m_w_proj_a=m_w_proj_a, m_w_proj_b=m_w_proj_b, m_w_proj_c=m_w_proj_c, m_w_out=m_w_out, m_ffn2_norm=m_ffn2_norm, m_ffn2_w_gate=m_ffn2_w_gate, m_ffn2_w_up=m_ffn2_w_up, m_ffn2_w_down=m_ffn2_w_down, m_final_norm=m_final_norm, v_ffn1_norm=v_ffn1_norm, v_ffn1_w_gate=v_ffn1_w_gate, v_ffn1_w_up=v_ffn1_w_up, v_ffn1_w_down=v_ffn1_w_down, v_mix_norm=v_mix_norm, v_w_in=v_w_in, v_b_gate=v_b_gate, v_b_ln_g=v_b_ln_g, v_b_ln_b=v_b_ln_b, v_b_w_s=v_b_w_s, v_b_b_s=v_b_b_s, v_c_w=v_c_w, v_c_scale=v_c_scale, v_w_proj_a=v_w_proj_a, v_w_proj_b=v_w_proj_b, v_w_proj_c=v_w_proj_c, v_w_out=v_w_out, v_ffn2_norm=v_ffn2_norm, v_ffn2_w_gate=v_ffn2_w_gate, v_ffn2_w_up=v_ffn2_w_up, v_ffn2_w_down=v_ffn2_w_down, v_final_norm=v_final_norm)
    weights = {n: given[n] for n in TWIN_WEIGHTS}
    shared = {n: given[n] for n in SHARED_INPUTS}
    per_example = {n: given[n] for n in ['x']}
    grad_fn = _jax.value_and_grad(_loss, argnums=(0, 1))

    def one_microbatch(ex, loss_target):
        ex = dict(ex)
        diff = ex.pop(TWIN_DIFF_INPUT)
        return grad_fn(weights, diff, {**shared, **ex}, loss_target)

    if N_MICROBATCH == 1:
        loss, (grad_w, grad_x) = one_microbatch(per_example, given["loss_target"])
    else:
        def body(carry, xs):
            loss_sum, grad_sum = carry
            l_k, (gw_k, gx_k) = one_microbatch(xs[0], xs[1])
            with _jax.named_scope("update"):
                return (loss_sum + l_k, _jax.tree.map(_jnp.add, grad_sum, gw_k)), gx_k

        init = (_jnp.zeros((), _jnp.float32), _jax.tree.map(_jnp.zeros_like, weights))
        (loss, grad_w), grad_x = _jax.lax.scan(body, init, (per_example, given["loss_target"]))
    with _jax.named_scope("update"):
        delta_w, new_m, new_v = {}, {}, {}
        for n in TWIN_WEIGHTS:
            delta_w[n], new_m[n], new_v[n] = _adamw(weights[n], grad_w[n], given["m_" + n], given["v_" + n])
    return (loss, grad_x, *[grad_w[n] for n in TWIN_WEIGHTS], *[delta_w[n] for n in TWIN_WEIGHTS],
            *[new_m[n] for n in TWIN_WEIGHTS], *[new_v[n] for n in TWIN_WEIGHTS])
```

```python
import math

import jax
import jax.numpy as jnp
from jax import lax
from jax.experimental import pallas as pl
from jax.experimental.pallas import tpu as pltpu

F32 = jnp.float32
BF16 = jnp.bfloat16
MESH = pl.DeviceIdType.MESH

NDEV = 8
DEPTH = 2
D = 1024
FF = 2816
FFS = FF // NDEV
INW = 6912
INS = INW // NDEV
DS = D // NDEV
BLK = 128
NH = 4
HD = 64
GW = NH * HD
DILS = (1, 4, 16)
QK_SCALE = 1.0 / math.sqrt(HD)
ZB = INW // GW
Q_B, K_B, V_B = 0, 3, 6
UV_B = 9
C_B128 = 26
G_B = 15
BW = 512
EPS = 1e-6
NEG = -1e30
VMEM_CAP = 60 * 1024 * 1024

ADAM_LR, ADAM_B1, ADAM_B2, ADAM_EPS, ADAM_WD, ADAM_STEP = 0.001, 0.9, 0.999, 1e-08, 0.01, 10

NT = (((1,), (1,)), ((), ()))
TN = (((0,), (0,)), ((), ()))


def _bs(shape, imap):
    return pl.BlockSpec(shape, imap)


def _pc(body, *, name, grid, in_specs, out_specs, out_shape, scratch=(), alias=None, vmem=None):
    params = {}
    if vmem is not None:
        params["vmem_limit_bytes"] = min(int(vmem), VMEM_CAP)
    return pl.pallas_call(
        body, name=name, grid=grid, in_specs=in_specs, out_specs=out_specs, out_shape=out_shape,
        scratch_shapes=list(scratch), input_output_aliases=alias or {},
        compiler_params=pltpu.CompilerParams(**params), interpret=False)


def _sds(shape, dtype):
    return jax.ShapeDtypeStruct(shape, dtype)


def _row_tile(rows, align, cap):
    best = None
    for cand in range(align, min(rows, cap) + 1, align):
        if rows % cand == 0:
            best = cand
    assert best is not None, (rows, align, cap)
    return best


def _mm(a, b, *, mode, tm, tn, tk, n, out_dtype, name, b_off=0, scale=None, res=None):
    m, k_dim = a.shape
    nk = k_dim // tk
    assert m % tm == 0 and n % tn == 0 and k_dim % tk == 0
    grid = (n // tn, m // tm, nk)
    in_specs = [_bs((tm, tk), lambda j, i, k: (i, k))]
    if mode == "nn":
        in_specs.append(_bs((tk, tn), lambda j, i, k: (b_off + k, j)))
    else:
        in_specs.append(_bs((tn, tk), lambda j, i, k: (b_off + j, k)))
    args = [a, b]
    if res is not None:
        in_specs.append(_bs((tm, tn), lambda j, i, k: (i, j)))
        args.append(res)

    def body(*refs):
        a_ref, b_ref = refs[0], refs[1]
        r_ref = refs[2] if res is not None else None
        o_ref = refs[3] if res is not None else refs[2]
        av = a_ref[...].astype(BF16)
        bv = b_ref[...].astype(BF16)
        if mode == "nn":
            p = jnp.dot(av, bv, preferred_element_type=F32)
        else:
            p = lax.dot_general(av, bv, NT, preferred_element_type=F32)

        def fin(v):
            if scale is not None:
                v = v * scale
            if r_ref is not None:
                v = r_ref[...] + v
            o_ref[...] = v.astype(out_dtype)

        if nk == 1:
            fin(p)
        else:
            acc_ref = refs[-1]
            kk = pl.program_id(2)

            @pl.when(kk == 0)
            def _():
                acc_ref[...] = p

            @pl.when(kk > 0)
            def _():
                acc_ref[...] += p

            @pl.when(kk == nk - 1)
            def _():
                fin(acc_ref[...])

    scratch = [pltpu.VMEM((tm, tn), F32)] if nk > 1 else []
    est = 2 * (tm * tk * a.dtype.itemsize + tk * tn * b.dtype.itemsize + tm * tn * jnp.dtype(out_dtype).itemsize)
    est += 3 * tm * tn * 4 + (2 * tm * tn * 4 if res is not None else 0) + (tm * tk + tk * tn) * 2
    return _pc(body, name=name, grid=grid, in_specs=in_specs, out_specs=_bs((tm, tn), lambda j, i, k: (i, j)),
               out_shape=_sds((m, n), out_dtype), scratch=scratch, vmem=est + (8 << 20))(*args)


def _wgrad(a, b, buf, *, piece0, rows, tm, tn, tk, name, scale=None):
    t_dim, m = a.shape
    n = b.shape[1]
    db = tm // rows
    tpp = NDEV // db
    nk = t_dim // tk
    assert tm % rows == 0 and NDEV % db == 0 and m % tm == 0 and n % tn == 0 and t_dim % tk == 0
    grid = (m // tm, n // tn, nk)

    def body(a_ref, b_ref, buf_ref, o_ref, acc_ref):
        del buf_ref
        kk = pl.program_id(2)
        p = lax.dot_general(a_ref[...].astype(BF16), b_ref[...].astype(BF16), TN, preferred_element_type=F32)

        @pl.when(kk == 0)
        def _():
            acc_ref[...] = p

        @pl.when(kk > 0)
        def _():
            acc_ref[...] += p

        @pl.when(kk == nk - 1)
        def _():
            for d in range(db):
                v = acc_ref[d * rows:(d + 1) * rows, :]
                if scale is not None:
                    v = v * scale
                o_ref[0, d] = v.astype(BF16)

    est = 2 * (tk * tm * a.dtype.itemsize + tk * tn * b.dtype.itemsize + tm * tn * 2) + 3 * tm * tn * 4 + tk * tm * 4
    return _pc(
        body, name=name, grid=grid,
        in_specs=[_bs((tk, tm), lambda i, j, k: (k, i)), _bs((tk, tn), lambda i, j, k: (k, j)),
                  pl.BlockSpec(memory_space=pl.ANY)],
        out_specs=_bs((1, db, rows, tn), lambda i, j, k: (piece0 + i // tpp, i % tpp, 0, j)),
        out_shape=_sds(buf.shape, buf.dtype), scratch=[pltpu.VMEM((tm, tn), F32)], alias={2: 0},
        vmem=est + (8 << 20))(a, b, buf)


def _rms_fwd(x, gain, name, tm=512):
    t_dim = x.shape[0]

    def body(x_ref, g_ref, o_ref):
        xv = x_ref[...]
        r = lax.rsqrt(jnp.mean(xv * xv, axis=-1, keepdims=True) + EPS)
        o_ref[...] = (xv * r * g_ref[...]).astype(BF16)

    return _pc(body, name=name, grid=(t_dim // tm,),
               in_specs=[_bs((tm, D), lambda i: (i, 0)), _bs((1, D), lambda i: (0, 0))],
               out_specs=_bs((tm, D), lambda i: (i, 0)), out_shape=_sds((t_dim, D), BF16))(x, gain)


def _rms_bwd(x, gain, dh, dres, name, tm=512):
    t_dim = x.shape[0]

    def body(x_ref, g_ref, dh_ref, dr_ref, dx_ref, dg_ref):
        xv = x_ref[...]
        r = lax.rsqrt(jnp.mean(xv * xv, axis=-1, keepdims=True) + EPS)
        dh_v = dh_ref[...].astype(F32)
        dyg = dh_v * g_ref[...]
        mq = jnp.mean(dyg * xv, axis=-1, keepdims=True)
        dx_ref[...] = dr_ref[...] + r * (dyg - xv * (r * r * mq))

        @pl.when(pl.program_id(0) == 0)
        def _():
            dg_ref[...] = jnp.zeros_like(dg_ref)

        dg_ref[...] += jnp.sum(dh_v * (xv * r), axis=0, keepdims=True)

    return _pc(body, name=name, grid=(t_dim // tm,),
               in_specs=[_bs((tm, D), lambda i: (i, 0)), _bs((1, D), lambda i: (0, 0)),
                         _bs((tm, D), lambda i: (i, 0)), _bs((tm, D), lambda i: (i, 0))],
               out_specs=[_bs((tm, D), lambda i: (i, 0)), _bs((1, D), lambda i: (0, 0))],
               out_shape=[_sds((t_dim, D), F32), _sds((1, D), F32)])(x, gain, dh, dres)


def _sigmoid(v):
    return 1.0 / (1.0 + jnp.exp(-v))


def _swiglu_fwd(gu, name, tm=256):
    t_dim = gu.shape[0]

    def body(gu_ref, o_ref):
        g = gu_ref[:, :FF].astype(F32)
        u = gu_ref[:, FF:].astype(F32)
        o_ref[...] = (g * _sigmoid(g) * u).astype(BF16)

    return _pc(body, name=name, grid=(t_dim // tm,), in_specs=[_bs((tm, 2 * FF), lambda i: (i, 0))],
               out_specs=_bs((tm, FF), lambda i: (i, 0)), out_shape=_sds((t_dim, FF), BF16))(gu)


def _swiglu_bwd(gu, da, name, tm=256):
    t_dim = gu.shape[0]

    def body(gu_ref, da_ref, o_ref):
        g = gu_ref[:, :FF].astype(F32)
        u = gu_ref[:, FF:].astype(F32)
        dav = da_ref[...].astype(F32)
        s = _sigmoid(g)
        o_ref[:, :FF] = (dav * u * (s * (1.0 + g * (1.0 - s)))).astype(BF16)
        o_ref[:, FF:] = (dav * (g * s)).astype(BF16)

    return _pc(body, name=name, grid=(t_dim // tm,),
               in_specs=[_bs((tm, 2 * FF), lambda i: (i, 0)), _bs((tm, FF), lambda i: (i, 0))],
               out_specs=_bs((tm, 2 * FF), lambda i: (i, 0)), out_shape=_sds((t_dim, 2 * FF), BF16))(gu, da)


def _final_loss(x, gain, tgt, name, tm=512):
    t_dim = x.shape[0]

    def body(x_ref, g_ref, t_ref, loss_ref, dx_ref, dg_ref):
        xv = x_ref[...]
        gv = g_ref[...]
        r = lax.rsqrt(jnp.mean(xv * xv, axis=-1, keepdims=True) + EPS)
        xn = xv * r
        err = xn * gv - t_ref[...]
        dy = err * (1.0 / D)
        dyg = dy * gv
        mq = jnp.mean(dyg * xv, axis=-1, keepdims=True)
        dx_ref[...] = r * (dyg - xv * (r * r * mq))

        @pl.when(pl.program_id(0) == 0)
        def _():
            dg_ref[...] = jnp.zeros_like(dg_ref)
            loss_ref[...] = jnp.zeros_like(loss_ref)

        dg_ref[...] += jnp.sum(dy * xn, axis=0, keepdims=True)
        part = 0.5 * jnp.sum(jnp.mean(err * err, axis=-1, keepdims=True), axis=0, keepdims=True)
        loss_ref[...] += jnp.broadcast_to(part, loss_ref.shape)

    return _pc(body, name=name, grid=(t_dim // tm,),
               in_specs=[_bs((tm, D), lambda i: (i, 0)), _bs((1, D), lambda i: (0, 0)), _bs((tm, D), lambda i: (i, 0))],
               out_specs=[_bs((8, 128), lambda i: (0, 0)), _bs((tm, D), lambda i: (i, 0)), _bs((1, D), lambda i: (0, 0))],
               out_shape=[_sds((8, 128), F32), _sds((t_dim, D), F32), _sds((1, D), F32)])(x, gain, tgt)


def _band_mask(n):
    ri = lax.broadcasted_iota(jnp.int32, (BLK, 2 * BLK), 0)
    cj = lax.broadcasted_iota(jnp.int32, (BLK, 2 * BLK), 1)
    dist = ri + BLK - cj
    return (dist >= 0) & (dist <= BLK) & ((cj >= BLK) | (n > 0))


def _attn_fwd(z, acc, ml, *, gi, first, final, name):
    dil = DILS[gi]
    t_dim = z.shape[0]
    ln = t_dim // dil
    nb = ln // BLK
    zv = z.reshape(ln, dil * INW)

    def own(off):
        return _bs((BLK, GW), lambda r, n: (n, r * ZB + off + gi))

    def prev(off):
        return _bs((BLK, GW), lambda r, n: (jnp.maximum(n - 1, 0), r * ZB + off + gi))

    st_acc = _bs((BLK, GW), lambda r, n: (n, r))
    st_ml = _bs((BLK, 128), lambda r, n: (n, r))
    in_specs = [own(Q_B), own(K_B), prev(K_B), own(V_B), prev(V_B)]
    args = [zv, zv, zv, zv, zv]
    if not first:
        in_specs += [st_acc, st_ml]
        args += [acc.reshape(ln, dil * GW), ml.reshape(ln, dil * 128)]

    def body(*refs):
        q_ref, ko_ref, kp_ref, vo_ref, vp_ref = refs[:5]
        o_acc_ref, o_ml_ref = refs[-2], refs[-1]
        n = pl.program_id(1)
        q = q_ref[...] * QK_SCALE
        kb = jnp.concatenate([kp_ref[...], ko_ref[...]], axis=0).astype(BF16)
        vb = jnp.concatenate([vp_ref[...], vo_ref[...]], axis=0).astype(BF16)
        valid = _band_mask(n)
        lane = lax.broadcasted_iota(jnp.int32, (BLK, GW), 1)
        lane_s = lax.broadcasted_iota(jnp.int32, (BLK, 128), 1)
        if first:
            acc_v = jnp.zeros((BLK, GW), F32)
            ml_v = jnp.where(lane_s < 64, NEG, 0.0).astype(F32)
        else:
            acc_v = refs[5][...]
            ml_v = refs[6][...]
        acc_o, ml_o = acc_v, ml_v
        l_b = jnp.ones((BLK, GW), F32)
        for h in range(NH):
            hm = (lane >= h * HD) & (lane < (h + 1) * HD)
            qh = jnp.where(hm, q, 0.0).astype(BF16)
            s = lax.dot_general(qh, kb, NT, preferred_element_type=F32)
            s = jnp.where(valid, s, NEG)
            m_old = ml_v[:, 16 * h:16 * h + 1]
            l_old = ml_v[:, 64 + 16 * h:64 + 16 * h + 1]
            m_new = jnp.maximum(m_old, jnp.max(s, axis=-1, keepdims=True))
            alpha = jnp.exp(m_old - m_new)
            p = jnp.exp(s - m_new)
            l_new = alpha * l_old + jnp.sum(p, axis=-1, keepdims=True)
            pv = jnp.dot(p.astype(BF16), vb, preferred_element_type=F32)
            acc_o = jnp.where(hm, acc_v * alpha + pv, acc_o)
            l_b = jnp.where(hm, l_new, l_b)
            if final:
                ml_o = jnp.where((lane_s >= 16 * h) & (lane_s < 16 * h + 16), m_new + jnp.log(l_new), ml_o)
            else:
                ml_o = jnp.where((lane_s >= 16 * h) & (lane_s < 16 * h + 16), m_new, ml_o)
                ml_o = jnp.where((lane_s >= 64 + 16 * h) & (lane_s < 64 + 16 * h + 16), l_new, ml_o)
        o_acc_ref[...] = acc_o / l_b if final else acc_o
        o_ml_ref[...] = ml_o

    out = _pc(body, name=name, grid=(dil, nb), in_specs=in_specs, out_specs=[st_acc, st_ml],
              out_shape=[_sds((ln, dil * GW), F32), _sds((ln, dil * 128), F32)])(*args)
    return out[0].reshape(t_dim, GW), out[1].reshape(t_dim, 128)


def _attn_bwd(z, dy, y, lse, *, gi, name):
    dil = DILS[gi]
    t_dim = z.shape[0]
    ln = t_dim // dil
    nb = ln // BLK
    zv = z.reshape(ln, dil * INW)

    def cur(n):
        return jnp.minimum(n, nb - 1)

    def own(off):
        return _bs((BLK, GW), lambda r, n: (cur(n), r * ZB + off + gi))

    def prev(off):
        return _bs((BLK, GW), lambda r, n: (jnp.maximum(cur(n) - 1, 0), r * ZB + off + gi))

    tok = _bs((BLK, GW), lambda r, n: (cur(n), r))
    tok_s = _bs((BLK, 128), lambda r, n: (cur(n), r))
    kv_out = _bs((BLK, GW), lambda r, n: (jnp.maximum(n - 1, 0), r))

    def body(q_ref, ko_ref, kp_ref, vo_ref, vp_ref, dy_ref, y_ref, lse_ref, dq_ref, dk_ref, dv_ref, dkc, dvc):
        n = pl.program_id(1)

        @pl.when(n < nb)
        def _():
            q = q_ref[...] * QK_SCALE
            kb = jnp.concatenate([kp_ref[...], ko_ref[...]], axis=0).astype(BF16)
            vb = jnp.concatenate([vp_ref[...], vo_ref[...]], axis=0).astype(BF16)
            valid = _band_mask(n)
            lane = lax.broadcasted_iota(jnp.int32, (BLK, GW), 1)
            dyv = dy_ref[...]
            yv = y_ref[...]
            lse_v = lse_ref[...]
            dq = jnp.zeros((BLK, GW), F32)
            dkb = jnp.zeros((2 * BLK, GW), F32)
            dvb = jnp.zeros((2 * BLK, GW), F32)
            for h in range(NH):
                hm = (lane >= h * HD) & (lane < (h + 1) * HD)
                qh = jnp.where(hm, q, 0.0).astype(BF16)
                s = lax.dot_general(qh, kb, NT, preferred_element_type=F32)
                p = jnp.where(valid, jnp.exp(s - lse_v[:, 16 * h:16 * h + 1]), 0.0)
                dyh = jnp.where(hm, dyv, 0.0)
                delta = jnp.sum(dyh * yv, axis=-1, keepdims=True)
                dyh16 = dyh.astype(BF16)
                dp = lax.dot_general(dyh16, vb, NT, preferred_element_type=F32)
                ds = (p * (dp - delta)).astype(BF16)
                dq = dq + jnp.where(hm, jnp.dot(ds, kb, preferred_element_type=F32), 0.0)
                dkb = dkb + lax.dot_general(ds, qh, TN, preferred_element_type=F32)
                dvb = dvb + lax.dot_general(p.astype(BF16), dyh16, TN, preferred_element_type=F32)
            dq_ref[...] = (dq * QK_SCALE).astype(BF16)

            @pl.when(n > 0)
            def _():
                dk_ref[...] = (dkc[...] + dkb[:BLK]).astype(BF16)
                dv_ref[...] = (dvc[...] + dvb[:BLK]).astype(BF16)

            dkc[...] = dkb[BLK:]
            dvc[...] = dvb[BLK:]

        @pl.when(n == nb)
        def _():
            dk_ref[...] = dkc[...].astype(BF16)
            dv_ref[...] = dvc[...].astype(BF16)

    outs = _pc(body, name=name, grid=(dil, nb + 1),
               in_specs=[own(Q_B), own(K_B), prev(K_B), own(V_B), prev(V_B), tok, tok, tok_s],
               out_specs=[tok, kv_out, kv_out],
               out_shape=[_sds((ln, dil * GW), BF16)] * 3,
               scratch=[pltpu.VMEM((BLK, GW), F32), pltpu.VMEM((BLK, GW), F32)])(
                   zv, zv, zv, zv, zv, dy.reshape(ln, dil * GW), y.reshape(ln, dil * GW), lse.reshape(ln, dil * 128))
    return [o.reshape(t_dim, GW) for o in outs]


_INV_SQRT2 = 1.0 / math.sqrt(2.0)
_INV_SQRT2PI = 1.0 / math.sqrt(2.0 * math.pi)


def _gelu(v):
    return 0.5 * v * (1.0 + lax.erf(v * _INV_SQRT2))


def _gelu_grad(v):
    return 0.5 * (1.0 + lax.erf(v * _INV_SQRT2)) + v * (_INV_SQRT2PI * jnp.exp(-0.5 * v * v))


def _gmlp_core(z_refs, lng, lnb, ws_ref, bs_ref):
    u_pre = jnp.concatenate([z_refs[0][...], z_refs[1][...]], axis=1)
    v_pre = jnp.concatenate([z_refs[2][...], z_refs[3][...]], axis=1)
    u = _gelu(u_pre)
    v = _gelu(v_pre)
    xc = v - jnp.mean(v, axis=-1, keepdims=True)
    rstd = lax.rsqrt(jnp.mean(xc * xc, axis=-1, keepdims=True) + EPS)
    xhat = xc * rstd
    vn = xhat * lng + lnb
    ti = lax.broadcasted_iota(jnp.int32, (BLK, BLK), 0)
    si = lax.broadcasted_iota(jnp.int32, (BLK, BLK), 1)
    causal = ti >= si
    vn16 = vn.astype(BF16)
    ws = [jnp.where(causal, ws_ref[g], 0.0).astype(BF16) for g in range(4)]
    mixed = jnp.concatenate(
        [jnp.dot(ws[g], vn16[:, g * BLK:(g + 1) * BLK], preferred_element_type=F32) + bs_ref[g] for g in range(4)], axis=1)
    return u_pre, v_pre, u, xhat, rstd, vn16, ws, causal, mixed


def _gmlp_specs():
    z_specs = [_bs((BLK, GW), (lambda i, c=c: (i, UV_B + c))) for c in range(4)]
    par_specs = [_bs((1, BW), lambda i: (0, 0)), _bs((1, BW), lambda i: (0, 0)),
                 _bs((4, BLK, BLK), lambda i: (0, 0, 0)), _bs((4, BLK, 1), lambda i: (0, 0, 0))]
    return z_specs, par_specs


def _gmlp_fwd(z, lng, lnb, ws, bs, name):
    t_dim = z.shape[0]
    z_specs, par_specs = _gmlp_specs()

    def body(z0, z1, z2, z3, lng_ref, lnb_ref, ws_ref, bs_ref, o_ref):
        core = _gmlp_core((z0, z1, z2, z3), lng_ref[...], lnb_ref[...], ws_ref, bs_ref)
        o_ref[...] = (core[2] * core[8]).astype(BF16)

    return _pc(body, name=name, grid=(t_dim // BLK,), in_specs=z_specs + par_specs,
               out_specs=_bs((BLK, BW), lambda i: (i, 0)), out_shape=_sds((t_dim, BW), BF16))(z, z, z, z, lng, lnb, ws, bs)


def _gmlp_bwd(z, dy, lng, lnb, ws, bs, name):
    t_dim = z.shape[0]
    z_specs, par_specs = _gmlp_specs()

    def body(z0, z1, z2, z3, lng_ref, lnb_ref, ws_ref, bs_ref, dy_ref, duv_ref, dws_ref, dbs_ref, dlng_ref, dlnb_ref):
        lng_v = lng_ref[...]
        u_pre, v_pre, u, xhat, rstd, vn16, wsm, causal, mixed = _gmlp_core((z0, z1, z2, z3), lng_v, lnb_ref[...], ws_ref, bs_ref)
        dyv = dy_ref[...].astype(F32)
        du = dyv * mixed
        dmixed = dyv * u

        @pl.when(pl.program_id(0) == 0)
        def _():
            dws_ref[...] = jnp.zeros_like(dws_ref)
            dbs_ref[...] = jnp.zeros_like(dbs_ref)
            dlng_ref[...] = jnp.zeros_like(dlng_ref)
            dlnb_ref[...] = jnp.zeros_like(dlnb_ref)

        dvn_parts = []
        for g in range(4):
            dm = dmixed[:, g * BLK:(g + 1) * BLK]
            dm16 = dm.astype(BF16)
            dw = lax.dot_general(dm16, vn16[:, g * BLK:(g + 1) * BLK], NT, preferred_element_type=F32)
            dws_ref[g] += jnp.where(causal, dw, 0.0)
            dbs_ref[g] += jnp.sum(dm, axis=-1, keepdims=True)
            dvn_parts.append(lax.dot_general(wsm[g], dm16, TN, preferred_element_type=F32))
        dvn = jnp.concatenate(dvn_parts, axis=1)
        dlng_ref[...] += jnp.sum(dvn * xhat, axis=0, keepdims=True)
        dlnb_ref[...] += jnp.sum(dvn, axis=0, keepdims=True)
        dxh = dvn * lng_v
        dv = rstd * (dxh - jnp.mean(dxh, axis=-1, keepdims=True) - xhat * jnp.mean(dxh * xhat, axis=-1, keepdims=True))
        duv_ref[:, :BW] = (du * _gelu_grad(u_pre)).astype(BF16)
        duv_ref[:, BW:] = (dv * _gelu_grad(v_pre)).astype(BF16)

    return _pc(body, name=name, grid=(t_dim // BLK,),
               in_specs=z_specs + par_specs + [_bs((BLK, BW), lambda i: (i, 0))],
               out_specs=[_bs((BLK, 2 * BW), lambda i: (i, 0)), _bs((4, BLK, BLK), lambda i: (0, 0, 0)),
                          _bs((4, BLK, 1), lambda i: (0, 0, 0)), _bs((1, BW), lambda i: (0, 0)), _bs((1, BW), lambda i: (0, 0))],
               out_shape=[_sds((t_dim, 2 * BW), BF16), _sds((4, BLK, BLK), F32), _sds((4, BLK, 1), F32),
                          _sds((1, BW), F32), _sds((1, BW), F32)])(z, z, z, z, lng, lnb, ws, bs, dy)


def _pool_core(x, gi):
    t_dim = x.shape[0]
    w = jnp.left_shift(2, gi)
    row = lax.broadcasted_iota(jnp.int32, x.shape, 0)
    s = x
    for k in (1, 2, 4, 8):
        sh = jnp.where(row >= k, pltpu.roll(s, k, 0), 0.0)
        s = jnp.where(k < w, s + sh, s)
    cnt = jnp.minimum(row + 1, w).astype(F32)
    return s / cnt - x, cnt, row, w, t_dim


def _pool_fwd(z, cw, cs, name):
    t_dim = z.shape[0]

    def body(z_ref, cw_ref, cs_ref, o_ref):
        pooled = _pool_core(z_ref[...], pl.program_id(0))[0]
        y = jnp.dot(pooled.astype(BF16), cw_ref[0].astype(BF16), preferred_element_type=F32) * cs_ref[0]
        o_ref[...] = y.astype(BF16)

    return _pc(body, name=name, grid=(4,),
               in_specs=[_bs((t_dim, BLK), lambda g: (0, C_B128 + g)), _bs((1, BLK, BLK), lambda g: (g, 0, 0)),
                         _bs((1, 1, BLK), lambda g: (g, 0, 0))],
               out_specs=_bs((t_dim, BLK), lambda g: (0, g)), out_shape=_sds((t_dim, BW), BF16),
               vmem=40 << 20)(z, cw, cs)


def _pool_bwd(z, dy, cw, cs, name):
    t_dim = z.shape[0]

    def body(z_ref, dy_ref, cw_ref, cs_ref, dx_ref, dcw_ref, dcs_ref):
        pooled, cnt, row, w, _ = _pool_core(z_ref[...], pl.program_id(0))
        p16 = pooled.astype(BF16)
        cw16 = cw_ref[0].astype(BF16)
        dyv = dy_ref[...].astype(F32)
        lin = jnp.dot(p16, cw16, preferred_element_type=F32)
        dcs_ref[0] = jnp.sum(dyv * lin, axis=0, keepdims=True)
        dys = (dyv * cs_ref[0]).astype(BF16)
        dcw_ref[0] = lax.dot_general(p16, dys, TN, preferred_element_type=F32)
        dpool = lax.dot_general(dys, cw16, NT, preferred_element_type=F32)
        sb = dpool / cnt
        for k in (1, 2, 4, 8):
            sh = jnp.where(row < t_dim - k, pltpu.roll(sb, t_dim - k, 0), 0.0)
            sb = jnp.where(k < w, sb + sh, sb)
        dx_ref[...] = (sb - dpool).astype(BF16)

    return _pc(body, name=name, grid=(4,),
               in_specs=[_bs((t_dim, BLK), lambda g: (0, C_B128 + g)), _bs((t_dim, BLK), lambda g: (0, g)),
                         _bs((1, BLK, BLK), lambda g: (g, 0, 0)), _bs((1, 1, BLK), lambda g: (g, 0, 0))],
               out_specs=[_bs((t_dim, BLK), lambda g: (0, g)), _bs((1, BLK, BLK), lambda g: (g, 0, 0)),
                          _bs((1, 1, BLK), lambda g: (g, 0, 0))],
               out_shape=[_sds((t_dim, BW), BF16), _sds((4, BLK, BLK), F32), _sds((4, 1, BLK), F32)],
               vmem=48 << 20)(z, dy, cw, cs)


def _gates_specs(tm):
    z_specs = [_bs((tm, GW), (lambda j, i, b=b: (i, G_B + 4 * b + j))) for b in range(3)]
    bg_specs = [_bs((1, GW), (lambda j, i, b=b: (0, 4 * b + j))) for b in range(3)]
    tile = _bs((tm, GW), lambda j, i: (i, j))
    return z_specs, bg_specs, tile


def _gates_fwd(z, bg, pa, pb, pc, name, tm=512):
    t_dim = z.shape[0]
    z_specs, bg_specs, tile = _gates_specs(tm)

    def body(za, zb, zc, ba, bb, bc, pa_ref, pb_ref, pc_ref, o_ref):
        o_ref[...] = (_sigmoid(za[...] + ba[...]) * pa_ref[...] + _sigmoid(zb[...] + bb[...]) * pb_ref[...]
                      + _sigmoid(zc[...] + bc[...]) * pc_ref[...]).astype(BF16)

    return _pc(body, name=name, grid=(D // GW, t_dim // tm), in_specs=z_specs + bg_specs + [tile] * 3,
               out_specs=tile, out_shape=_sds((t_dim, D), BF16))(z, z, z, bg, bg, bg, pa, pb, pc)


def _gates_bwd(z, bg, pa, pb, pc, dm, name, tm=512):
    t_dim = z.shape[0]
    z_specs, bg_specs, tile = _gates_specs(tm)
    bias_out = _bs((1, GW), lambda j, i: (0, j))

    def body(za, zb, zc, ba, bb, bc, pa_ref, pb_ref, pc_ref, dm_ref, dpa, dpb, dpc, dza, dzb, dzc, dba, dbb, dbc):
        dmv = dm_ref[...]
        first = pl.program_id(1) == 0
        for z_ref, b_ref, p_ref, dp_ref, dz_ref, db_ref in ((za, ba, pa_ref, dpa, dza, dba), (zb, bb, pb_ref, dpb, dzb, dbb),
                                                            (zc, bc, pc_ref, dpc, dzc, dbc)):
            gate = _sigmoid(z_ref[...] + b_ref[...])
            dp_ref[...] = (gate * dmv).astype(BF16)
            dz = dmv * p_ref[...] * (gate * (1.0 - gate))
            dz_ref[...] = dz.astype(BF16)

            @pl.when(first)
            def _():
                db_ref[...] = jnp.zeros_like(db_ref)

            db_ref[...] += jnp.sum(dz, axis=0, keepdims=True)

    return _pc(body, name=name, grid=(D // GW, t_dim // tm), in_specs=z_specs + bg_specs + [tile] * 4,
               out_specs=[tile] * 6 + [bias_out] * 3,
               out_shape=[_sds((t_dim, D), BF16)] * 6 + [_sds((1, D), F32)] * 3)(z, z, z, bg, bg, bg, pa, pb, pc, dm)


def _coords():
    return lax.axis_index("x"), lax.axis_index("y"), lax.axis_index("c")


def _allgather(shards, name):
    na = len(shards)

    def body(*refs):
        ins, outs = refs[:na], refs[na:2 * na]
        send, recv, lsem = refs[2 * na:]
        x, y, c = _coords()
        me = 4 * x + 2 * y + c
        sib = (x, y, 1 - c)
        chips = [(1 - x, y), (x, 1 - y), (1 - x, 1 - y)]

        def ident(px, py, pc):
            return 4 * px + 2 * py + pc

        def cp(a, k, blk, to, src=None):
            dst = outs[a].at[:, blk]
            return pltpu.make_async_remote_copy(src_ref=dst if src is None else src, dst_ref=dst, send_sem=send.at[a * 7 + k],
                                                recv_sem=recv.at[a * 7 + k], device_id=to, device_id_type=MESH)

        mine = [pltpu.make_async_copy(ins[a], outs[a].at[:, me], lsem.at[a]) for a in range(na)]
        for m in mine:
            m.start()
        first = []
        for a in range(na):
            first.append(cp(a, 0, me, sib, src=ins[a]))
            first += [cp(a, 1 + j, me, (*chip, c), src=ins[a]) for j, chip in enumerate(chips)]
        for f in first:
            f.start()
        passed = []
        for j, chip in enumerate(chips):
            for a in range(na):
                cp(a, 1 + j, ident(*chip, c), (x, y, c)).wait_recv()
                fwd = cp(a, 4 + j, ident(*chip, c), sib)
                fwd.start()
                passed.append(fwd)
        for a in range(na):
            cp(a, 0, ident(x, y, 1 - c), (x, y, c)).wait_recv()
            for j, chip in enumerate(chips):
                cp(a, 4 + j, ident(*chip, 1 - c), (x, y, c)).wait_recv()
        for f in first + passed:
            f.wait_send()
        for m in mine:
            m.wait()

    any_spec = pl.BlockSpec(memory_space=pl.ANY)
    return pl.pallas_call(
        body, name=name, in_specs=[any_spec] * na, out_specs=[any_spec] * na,
        out_shape=[_sds((s.shape[0], NDEV) + s.shape[1:], s.dtype) for s in shards],
        scratch_shapes=[pltpu.SemaphoreType.DMA((7 * na,)), pltpu.SemaphoreType.DMA((7 * na,)), pltpu.SemaphoreType.DMA((na,))],
        interpret=False)(*shards)


def _alltoall(bufs, name):
    na = len(bufs)

    def body(*refs):
        ins, outs = refs[:na], refs[na:2 * na]
        send, recv, lsem = refs[2 * na:]
        x, y, c = _coords()
        me = 4 * x + 2 * y + c
        mine = [pltpu.make_async_copy(ins[a].at[:, me], outs[a].at[me], lsem.at[a]) for a in range(na)]
        for m in mine:
            m.start()
        copies = []
        for k in range(1, NDEV):
            px = 1 - x if k & 4 else x
            py = 1 - y if k & 2 else y
            pc = 1 - c if k & 1 else c
            peer = 4 * px + 2 * py + pc
            for a in range(na):
                copies.append(pltpu.make_async_remote_copy(
                    src_ref=ins[a].at[:, peer], dst_ref=outs[a].at[me], send_sem=send.at[a * 7 + k - 1],
                    recv_sem=recv.at[a * 7 + k - 1], device_id=(px, py, pc), device_id_type=MESH))
        for cpy in copies:
            cpy.start()
        for k in range(1, NDEV):
            px = 1 - x if k & 4 else x
            py = 1 - y if k & 2 else y
            pc = 1 - c if k & 1 else c
            peer = 4 * px + 2 * py + pc
            for a in range(na):
                pltpu.make_async_remote_copy(
                    src_ref=ins[a].at[:, peer], dst_ref=outs[a].at[peer], send_sem=send.at[a * 7 + k - 1],
                    recv_sem=recv.at[a * 7 + k - 1], device_id=(px, py, pc), device_id_type=MESH).wait_recv()
        for cpy in copies:
            cpy.wait_send()
        for m in mine:
            m.wait()

    any_spec = pl.BlockSpec(memory_space=pl.ANY)
    return pl.pallas_call(
        body, name=name, in_specs=[any_spec] * na, out_specs=[any_spec] * na,
        out_shape=[_sds((NDEV, b.shape[0]) + b.shape[2:], b.dtype) for b in bufs],
        scratch_shapes=[pltpu.SemaphoreType.DMA((7 * na,)), pltpu.SemaphoreType.DMA((7 * na,)), pltpu.SemaphoreType.DMA((na,))],
        interpret=False)(*bufs)


def _sum_slots(buf, name, slot_axis):
    if slot_axis == 0:
        _, p, rows, cols = buf.shape
    else:
        p, _, rows, cols = buf.shape
    tr = _row_tile(rows, 32 // buf.dtype.itemsize, 1024 if cols <= 128 else 512)
    if slot_axis == 0:
        in_spec = _bs((NDEV, 1, tr, cols), lambda i, j: (0, i, j, 0))
    else:
        in_spec = _bs((1, NDEV, tr, cols), lambda i, j: (i, 0, j, 0))

    def body(b_ref, o_ref):
        def slot(d):
            return (b_ref[d, 0] if slot_axis == 0 else b_ref[0, d]).astype(F32)
        s = slot(0)
        for d in range(1, NDEV):
            s = s + slot(d)
        o_ref[0] = s

    return _pc(body, name=name, grid=(p, rows // tr), in_specs=[in_spec], out_specs=_bs((1, tr, cols), lambda i, j: (i, j, 0)),
               out_shape=_sds((p, rows, cols), F32))(buf)


def _adamw(w, g, m, v, name):
    r, c = w.shape
    tr = _row_tile(r, 8, 1024 if c <= 512 else 512)

    def body(w_ref, g_ref, m_ref, v_ref, d_ref, nm_ref, nv_ref):
        gv = g_ref[...]
        m2 = ADAM_B1 * m_ref[...] + (1.0 - ADAM_B1) * gv
        v2 = ADAM_B2 * v_ref[...] + (1.0 - ADAM_B2) * jnp.square(gv)
        m_hat = m2 / (1.0 - ADAM_B1 ** ADAM_STEP)
        v_hat = v2 / (1.0 - ADAM_B2 ** ADAM_STEP)
        d_ref[...] = -ADAM_LR * (m_hat / (jnp.sqrt(v_hat) + ADAM_EPS) + ADAM_WD * w_ref[...])
        nm_ref[...] = m2
        nv_ref[...] = v2

    spec = _bs((tr, c), lambda i: (i, 0))
    return _pc(body, name=name, grid=(r // tr,), in_specs=[spec] * 4, out_specs=[spec] * 3,
               out_shape=[_sds((r, c), F32)] * 3)(w, g, m, v)


PG1, PD1, PG2, PD2 = 0, 2, 3, 5


def _ffn_fwd(x, gain, wf, pg, pd, tag):
    h = _rms_fwd(x, gain, f"rms_{tag}")
    gu = _mm(h, wf, mode="nt", tm=512, tn=FF // 2, tk=D, n=2 * FF, b_off=2 * pg, out_dtype=BF16, name=f"gu_{tag}")
    act = _swiglu_fwd(gu, f"swiglu_{tag}")
    out = _mm(act, wf, mode="nn", tm=512, tn=512, tk=FF, n=D, b_off=pd, out_dtype=F32, scale=0.5, res=x, name=f"down_{tag}")
    return out, (x, h, gu, act)


def _ffn_bwd(dout, saved, gain, wf, grf, pg, pd, tag):
    x, h, gu, act = saved
    da = _mm(dout, wf, mode="nt", tm=512, tn=FF // 2, tk=D, n=FF, b_off=2 * pd, out_dtype=BF16, scale=0.5, name=f"dact_{tag}")
    grf = _wgrad(act, dout, grf, piece0=pd, rows=FFS, tm=FF, tn=512, tk=512, scale=0.5, name=f"wg_down_{tag}")
    dgu = _swiglu_bwd(gu, da, f"dswiglu_{tag}")
    dh = _mm(dgu, wf, mode="nn", tm=512, tn=512, tk=FF, n=D, b_off=pg, out_dtype=F32, name=f"dh_{tag}")
    grf = _wgrad(dgu, h, grf, piece0=pg, rows=FFS, tm=FF, tn=512, tk=512, name=f"wg_gu_{tag}")
    dx, dgain = _rms_bwd(x, gain, dh, dout, f"drms_{tag}")
    return dx, dgain, grf


def _mixer_fwd(x, sp, ww, wo, wa, wbc, tag):
    h = _rms_fwd(x, sp["mix_norm"], f"rms_mix_{tag}")
    z = _mm(h, ww, mode="nt", tm=512, tn=1152, tk=D, n=INW, out_dtype=F32, name=f"z_{tag}")
    acc, ml = _attn_fwd(z, None, None, gi=0, first=True, final=False, name=f"attn0_{tag}")
    acc, ml = _attn_fwd(z, acc, ml, gi=1, first=False, final=False, name=f"attn1_{tag}")
    ya, lse = _attn_fwd(z, acc, ml, gi=2, first=False, final=True, name=f"attn2_{tag}")
    yb = _gmlp_fwd(z, sp["b_ln_g"], sp["b_ln_b"], sp["b_w_s"], sp["b_b_s"], f"gmlp_{tag}")
    yc = _pool_fwd(z, sp["c_w"], sp["c_scale"], f"pool_{tag}")
    pa = _mm(ya, wa, mode="nt", tm=1024, tn=512, tk=GW, n=D, out_dtype=F32, name=f"proj_a_{tag}")
    pb = _mm(yb, wbc, mode="nt", tm=1024, tn=512, tk=BW, n=D, b_off=0, out_dtype=F32, name=f"proj_b_{tag}")
    pc = _mm(yc, wbc, mode="nt", tm=1024, tn=512, tk=BW, n=D, b_off=2, out_dtype=F32, name=f"proj_c_{tag}")
    merged = _gates_fwd(z, sp["b_gate"], pa, pb, pc, f"gates_{tag}")
    out = _mm(merged, wo, mode="nn", tm=512, tn=512, tk=D, n=D, out_dtype=F32, res=x, name=f"out_{tag}")
    return out, (x, h, z, ya, lse, yb, yc, pa, pb, pc, merged)


def _mixer_bwd(dout, saved, sp, ww, wo, wa, wbc, grw, gro, gra, grbc, tag):
    x, h, z, ya, lse, yb, yc, pa, pb, pc, merged = saved
    dm = _mm(dout, wo, mode="nt", tm=512, tn=512, tk=D, n=D, out_dtype=F32, name=f"dmerged_{tag}")
    gro = _wgrad(merged, dout, gro, piece0=0, rows=DS, tm=D, tn=D, tk=512, name=f"wg_out_{tag}")
    dpa, dpb, dpc, dza, dzb, dzc, dba, dbb, dbc = _gates_bwd(z, sp["b_gate"], pa, pb, pc, dm, f"dgates_{tag}")
    dya = _mm(dpa, wa, mode="nn", tm=1024, tn=GW, tk=D, n=GW, out_dtype=F32, name=f"dya_{tag}")
    dyb = _mm(dpb, wbc, mode="nn", tm=1024, tn=BW, tk=D, n=BW, b_off=0, out_dtype=F32, name=f"dyb_{tag}")
    dyc = _mm(dpc, wbc, mode="nn", tm=1024, tn=BW, tk=D, n=BW, b_off=1, out_dtype=F32, name=f"dyc_{tag}")
    gra = _wgrad(dpa, ya, gra, piece0=0, rows=DS, tm=D, tn=GW, tk=512, name=f"wg_pa_{tag}")
    grbc = _wgrad(dpb, yb, grbc, piece0=0, rows=DS, tm=D, tn=BW, tk=512, name=f"wg_pb_{tag}")
    grbc = _wgrad(dpc, yc, grbc, piece0=1, rows=DS, tm=D, tn=BW, tk=512, name=f"wg_pc_{tag}")
    dqkv = [_attn_bwd(z, dya, ya, lse, gi=gi, name=f"dattn{gi}_{tag}") for gi in range(3)]
    duv, dws, dbs, dlng, dlnb = _gmlp_bwd(z, dyb, sp["b_ln_g"], sp["b_ln_b"], sp["b_w_s"], sp["b_b_s"], f"dgmlp_{tag}")
    dxc, dcw, dcs = _pool_bwd(z, dyc, sp["c_w"], sp["c_scale"], f"dpool_{tag}")
    dz = jnp.concatenate([dqkv[gi][part] for part in range(3) for gi in range(3)] + [duv, dxc, dza, dzb, dzc], axis=1)
    dh = _mm(dz, ww, mode="nn", tm=512, tn=512, tk=INW // 2, n=D, out_dtype=F32, name=f"dh_mix_{tag}")
    grw = _wgrad(dz, h, grw, piece0=0, rows=INS, tm=INW // 2, tn=512, tk=512, name=f"wg_in_{tag}")
    dx, dgain = _rms_bwd(x, sp["mix_norm"], dh, dout, f"drms_mix_{tag}")
    small = {"mix_norm": dgain, "b_gate": jnp.concatenate([dba, dbb, dbc], axis=1), "b_ln_g": dlng, "b_ln_b": dlnb,
             "b_w_s": dws, "b_b_s": dbs, "c_w": dcw, "c_scale": dcs}
    return dx, small, grw, gro, gra, grbc


def _forward(xs, tgt, views, small, final_gain):
    saved = []
    cur = xs
    for l in range(DEPTH):
        wf, ww, wo, wa, wbc = views[l]
        cur, s1 = _ffn_fwd(cur, small[l]["ffn1_norm"], wf, PG1, PD1, f"f1l{l}")
        cur, s2 = _mixer_fwd(cur, small[l], ww, wo, wa, wbc, f"l{l}")
        cur, s3 = _ffn_fwd(cur, small[l]["ffn2_norm"], wf, PG2, PD2, f"f2l{l}")
        saved.append((s1, s2, s3))
    loss_part, dcur, dfinal = _final_loss(cur, final_gain, tgt, "final_loss")
    return loss_part, dcur, dfinal, saved


def _backward_layer(dcur, saved_l, small_l, views_l, l):
    wf, ww, wo, wa, wbc = views_l
    s1, s2, s3 = saved_l
    grf = jnp.zeros((6, NDEV, FFS, D), BF16)
    grw = jnp.zeros((1, NDEV, INS, D), BF16)
    gro = jnp.zeros((1, NDEV, DS, D), BF16)
    gra = jnp.zeros((1, NDEV, DS, GW), BF16)
    grbc = jnp.zeros((2, NDEV, DS, BW), BF16)
    dcur, dn2, grf = _ffn_bwd(dcur, s3, small_l["ffn2_norm"], wf, grf, PG2, PD2, f"f2l{l}")
    dcur, sg, grw, gro, gra, grbc = _mixer_bwd(dcur, s2, small_l, ww, wo, wa, wbc, grw, gro, gra, grbc, f"l{l}")
    dcur, dn1, grf = _ffn_bwd(dcur, s1, small_l["ffn1_norm"], wf, grf, PG1, PD1, f"f1l{l}")
    sg["ffn1_norm"] = dn1
    sg["ffn2_norm"] = dn2
    return dcur, sg, [grf, grw, gro, gra, grbc]


SMALL_LAYER = ("ffn1_norm", "mix_norm", "b_gate", "b_ln_g", "b_ln_b", "b_w_s", "b_b_s", "c_w", "c_scale", "ffn2_norm")
SMALL_ALL = SMALL_LAYER + ("final_norm",)
BIG = ("ffn1_w_gate", "ffn1_w_up", "ffn1_w_down", "w_in", "w_proj_a", "w_proj_b", "w_proj_c", "w_out",
       "ffn2_w_gate", "ffn2_w_up", "ffn2_w_down")
ALL_WEIGHTS = ("ffn1_norm", "ffn1_w_gate", "ffn1_w_up", "ffn1_w_down", "mix_norm", "w_in", "b_gate", "b_ln_g", "b_ln_b",
               "b_w_s", "b_b_s", "c_w", "c_scale", "w_proj_a", "w_proj_b", "w_proj_c", "w_out", "ffn2_norm", "ffn2_w_gate",
               "ffn2_w_up", "ffn2_w_down", "final_norm")


def _layer_shards(w, l):
    t16 = lambda a: a.T.astype(BF16)
    sf = jnp.stack([t16(w["ffn1_w_gate"][l]), t16(w["ffn1_w_up"][l]), w["ffn1_w_down"][l].astype(BF16),
                    t16(w["ffn2_w_gate"][l]), t16(w["ffn2_w_up"][l]), w["ffn2_w_down"][l].astype(BF16)])
    sw = t16(w["w_in"][l])[None]
    so = w["w_out"][l].astype(BF16)[None]
    sa = t16(w["w_proj_a"][l])[None]
    sbc = jnp.stack([t16(w["w_proj_b"][l]), t16(w["w_proj_c"][l])])
    return [sf, sw, so, sa, sbc]


def _layer_small(w, l):
    r = lambda a, shape: a.reshape(shape)
    return {"ffn1_norm": r(w["ffn1_norm"][l], (1, D)), "mix_norm": r(w["mix_norm"][l], (1, D)),
            "b_gate": r(w["b_gate"][l], (1, 3 * D)), "b_ln_g": r(w["b_ln_g"][l], (1, BW)), "b_ln_b": r(w["b_ln_b"][l], (1, BW)),
            "b_w_s": w["b_w_s"][l], "b_b_s": r(w["b_b_s"][l], (4, BLK, 1)), "c_w": w["c_w"][l],
            "c_scale": r(w["c_scale"][l], (4, 1, BLK)), "ffn2_norm": r(w["ffn2_norm"][l], (1, D))}


def _pack_small(per_layer, final):
    flat = [per_layer[l][k].reshape(-1) for l in range(DEPTH) for k in SMALL_LAYER] + [final.reshape(-1)]
    return jnp.concatenate(flat).reshape(-1, 128)


def _unpack_small(packed, shapes):
    flat = packed.reshape(-1)
    per = {k: [] for k in SMALL_LAYER}
    off = 0
    for _ in range(DEPTH):
        for k in SMALL_LAYER:
            size = math.prod(shapes[k][1:])
            per[k].append(flat[off:off + size].reshape(shapes[k][1:]))
            off += size
    out = {k: jnp.stack(v) for k, v in per.items()}
    out["final_norm"] = flat[off:off + D]
    return out


def _unshard_grads(gf, gw, go, ga, gbc):
    return {"ffn1_w_gate": gf[0].T, "ffn1_w_up": gf[1].T, "ffn1_w_down": gf[2], "ffn2_w_gate": gf[3].T, "ffn2_w_up": gf[4].T,
            "ffn2_w_down": gf[5], "w_in": gw[0].T, "w_out": go[0], "w_proj_a": ga[0].T, "w_proj_b": gbc[0].T, "w_proj_c": gbc[1].T}


def kernel(x, ffn1_norm, ffn1_w_gate, ffn1_w_up, ffn1_w_down, mix_norm, w_in, b_gate, b_ln_g, b_ln_b, b_w_s, b_b_s, c_w, c_scale, w_proj_a, w_proj_b, w_proj_c, w_out, ffn2_norm, ffn2_w_gate, ffn2_w_up, ffn2_w_down, final_norm, loss_target, m_ffn1_norm, m_ffn1_w_gate, m_ffn1_w_up, m_ffn1_w_down, m_mix_norm, m_w_in, m_b_gate, m_b_ln_g, m_b_ln_b, m_b_w_s, m_b_b_s, m_c_w, m_c_scale, m_w_proj_a, m_w_proj_b, m_w_proj_c, m_w_out, m_ffn2_norm, m_ffn2_w_gate, m_ffn2_w_up, m_ffn2_w_down, m_final_norm, v_ffn1_norm, v_ffn1_w_gate, v_ffn1_w_up, v_ffn1_w_down, v_mix_norm, v_w_in, v_b_gate, v_b_ln_g, v_b_ln_b, v_b_w_s, v_b_b_s, v_c_w, v_c_scale, v_w_proj_a, v_w_proj_b, v_w_proj_c, v_w_out, v_ffn2_norm, v_ffn2_w_gate, v_ffn2_w_up, v_ffn2_w_down, v_final_norm):
    w = dict(ffn1_norm=ffn1_norm, ffn1_w_gate=ffn1_w_gate, ffn1_w_up=ffn1_w_up, ffn1_w_down=ffn1_w_down, mix_norm=mix_norm,
             w_in=w_in, b_gate=b_gate, b_ln_g=b_ln_g, b_ln_b=b_ln_b, b_w_s=b_w_s, b_b_s=b_b_s, c_w=c_w, c_scale=c_scale,
             w_proj_a=w_proj_a, w_proj_b=w_proj_b, w_proj_c=w_proj_c, w_out=w_out, ffn2_norm=ffn2_norm, ffn2_w_gate=ffn2_w_gate,
             ffn2_w_up=ffn2_w_up, ffn2_w_down=ffn2_w_down, final_norm=final_norm)
    mom = dict(ffn1_norm=m_ffn1_norm, ffn1_w_gate=m_ffn1_w_gate, ffn1_w_up=m_ffn1_w_up, ffn1_w_down=m_ffn1_w_down,
               mix_norm=m_mix_norm, w_in=m_w_in, b_gate=m_b_gate, b_ln_g=m_b_ln_g, b_ln_b=m_b_ln_b, b_w_s=m_b_w_s, b_b_s=m_b_b_s,
               c_w=m_c_w, c_scale=m_c_scale, w_proj_a=m_w_proj_a, w_proj_b=m_w_proj_b, w_proj_c=m_w_proj_c, w_out=m_w_out,
               ffn2_norm=m_ffn2_norm, ffn2_w_gate=m_ffn2_w_gate, ffn2_w_up=m_ffn2_w_up, ffn2_w_down=m_ffn2_w_down,
               final_norm=m_final_norm)
    var = dict(ffn1_norm=v_ffn1_norm, ffn1_w_gate=v_ffn1_w_gate, ffn1_w_up=v_ffn1_w_up, ffn1_w_down=v_ffn1_w_down,
               mix_norm=v_mix_norm, w_in=v_w_in, b_gate=v_b_gate, b_ln_g=v_b_ln_g, b_ln_b=v_b_ln_b, b_w_s=v_b_w_s, b_b_s=v_b_b_s,
               c_w=v_c_w, c_scale=v_c_scale, w_proj_a=v_w_proj_a, w_proj_b=v_w_proj_b, w_proj_c=v_w_proj_c, w_out=v_w_out,
               ffn2_norm=v_ffn2_norm, ffn2_w_gate=v_ffn2_w_gate, ffn2_w_up=v_ffn2_w_up, ffn2_w_down=v_ffn2_w_down,
               final_norm=v_final_norm)

    t_dim = x.shape[1]
    xs = x.reshape(t_dim, D)
    tgt = loss_target.reshape(t_dim, D)

    full = [_allgather(_layer_shards(w, l), f"gather_weights_l{l}") for l in range(DEPTH)]
    views = []
    for l in range(DEPTH):
        pf, pw, po, pa, pbc = full[l]
        views.append((pf.reshape(6 * FF, D), pw.reshape(INW, D), po.reshape(D, D), pa.reshape(D, GW), pbc.reshape(2 * D, BW)))
    small = [_layer_small(w, l) for l in range(DEPTH)]
    loss_part, dcur, dfinal, saved = _forward(xs, tgt, views, small, w["final_norm"].reshape(1, D))
    loss = lax.psum(loss_part[0, 0], ("x", "y", "c"))

    small_grads = [None] * DEPTH
    big_grads = [None] * DEPTH
    for l in reversed(range(DEPTH)):
        dcur, small_grads[l], full_grads = _backward_layer(dcur, saved[l], small[l], views[l], l)
        landed = _alltoall(full_grads, f"scatter_grads_l{l}")
        sums = [_sum_slots(b, f"sum_grads_l{l}_{i}", 0) for i, b in enumerate(landed)]
        big_grads[l] = _unshard_grads(*sums)
    grad_x = dcur.reshape(x.shape)

    packed = _pack_small(small_grads, dfinal)
    gathered = _allgather([packed[None]], "gather_small_grads")[0]
    g_small = _sum_slots(gathered, "sum_small_grads", 1)[0]

    grads, delta, new_m, new_v = {}, {}, {}, {}
    shapes = {k: w[k].shape for k in SMALL_ALL}
    w_small = _pack_small([{k: w[k][l] for k in SMALL_LAYER} for l in range(DEPTH)], w["final_norm"])
    m_small = _pack_small([{k: mom[k][l] for k in SMALL_LAYER} for l in range(DEPTH)], mom["final_norm"])
    v_small = _pack_small([{k: var[k][l] for k in SMALL_LAYER} for l in range(DEPTH)], var["final_norm"])
    d_s, nm_s, nv_s = _adamw(w_small, g_small, m_small, v_small, "adamw_small")
    for dst, src in ((grads, g_small), (delta, d_s), (new_m, nm_s), (new_v, nv_s)):
        dst.update(_unpack_small(src, shapes))
    for k in BIG:
        g = jnp.stack([big_grads[l][k] for l in range(DEPTH)])
        shp = g.shape
        flat = lambda a: a.reshape(shp[0] * shp[1], shp[2])
        d_k, nm_k, nv_k = _adamw(flat(w[k]), flat(g), flat(mom[k]), flat(var[k]), f"adamw_{k}")
        grads[k], delta[k], new_m[k], new_v[k] = g, d_k.reshape(shp), nm_k.reshape(shp), nv_k.reshape(shp)

    return (loss, grad_x, *[grads[k] for k in ALL_WEIGHTS], *[delta[k] for k in ALL_WEIGHTS],
            *[new_m[k] for k in ALL_WEIGHTS], *[new_v[k] for k in ALL_WEIGHTS])
```

```python
import math

import jax
import jax.numpy as jnp
from jax import lax
from jax.experimental import pallas as pl
from jax.experimental.pallas import tpu as pltpu

F32 = jnp.float32
BF16 = jnp.bfloat16
MESH = pl.DeviceIdType.MESH

NDEV = 8
DEPTH = 2
D = 1024
FF = 2816
FFS = FF // NDEV
INW = 6912
INS = INW // NDEV
DS = D // NDEV
BLK = 128
NH = 4
HD = 64
GW = NH * HD
DILS = (1, 4, 16)
QK_SCALE = 1.0 / math.sqrt(HD)
ZB = INW // GW
Q_B, K_B, V_B = 0, 3, 6
UV_B = 9
C_B128 = 26
G_B = 15
BW = 512
EPS = 1e-6
NEG = -1e30
VMEM_CAP = 60 * 1024 * 1024

ADAM_LR, ADAM_B1, ADAM_B2, ADAM_EPS, ADAM_WD, ADAM_STEP = 0.001, 0.9, 0.999, 1e-08, 0.01, 10

NT = (((1,), (1,)), ((), ()))
TN = (((0,), (0,)), ((), ()))


def _bs(shape, imap):
    return pl.BlockSpec(shape, imap)


def _pc(body, *, name, grid, in_specs, out_specs, out_shape, scratch=(), alias=None, vmem=None):
    params = {}
    if vmem is not None:
        params["vmem_limit_bytes"] = min(int(vmem), VMEM_CAP)
    return pl.pallas_call(
        body, name=name, grid=grid, in_specs=in_specs, out_specs=out_specs, out_shape=out_shape,
        scratch_shapes=list(scratch), input_output_aliases=alias or {},
        compiler_params=pltpu.CompilerParams(**params), interpret=False)


def _sds(shape, dtype):
    return jax.ShapeDtypeStruct(shape, dtype)


def _row_tile(rows, align, cap):
    best = None
    for cand in range(align, min(rows, cap) + 1, align):
        if rows % cand == 0:
            best = cand
    assert best is not None, (rows, align, cap)
    return best


def _mm(a, b, *, mode, tm, tn, tk, n, out_dtype, name, b_off=0, scale=None, res=None):
    m, k_dim = a.shape
    nk = k_dim // tk
    assert m % tm == 0 and n % tn == 0 and k_dim % tk == 0
    grid = (n // tn, m // tm, nk)
    in_specs = [_bs((tm, tk), lambda j, i, k: (i, k))]
    if mode == "nn":
        in_specs.append(_bs((tk, tn), lambda j, i, k: (b_off + k, j)))
    else:
        in_specs.append(_bs((tn, tk), lambda j, i, k: (b_off + j, k)))
    args = [a, b]
    if res is not None:
        in_specs.append(_bs((tm, tn), lambda j, i, k: (i, j)))
        args.append(res)

    def body(*refs):
        a_ref, b_ref = refs[0], refs[1]
        r_ref = refs[2] if res is not None else None
        o_ref = refs[3] if res is not None else refs[2]
        av = a_ref[...].astype(BF16)
        bv = b_ref[...].astype(BF16)
        if mode == "nn":
            p = jnp.dot(av, bv, preferred_element_type=F32)
        else:
            p = lax.dot_general(av, bv, NT, preferred_element_type=F32)

        def fin(v):
            if scale is not None:
                v = v * scale
            if r_ref is not None:
                v = r_ref[...] + v
            o_ref[...] = v.astype(out_dtype)

        if nk == 1:
            fin(p)
        else:
            acc_ref = refs[-1]
            kk = pl.program_id(2)

            @pl.when(kk == 0)
            def _():
                acc_ref[...] = p

            @pl.when(kk > 0)
            def _():
                acc_ref[...] += p

            @pl.when(kk == nk - 1)
            def _():
                fin(acc_ref[...])

    scratch = [pltpu.VMEM((tm, tn), F32)] if nk > 1 else []
    est = 2 * (tm * tk * a.dtype.itemsize + tk * tn * b.dtype.itemsize + tm * tn * jnp.dtype(out_dtype).itemsize)
    est += 3 * tm * tn * 4 + (2 * tm * tn * 4 if res is not None else 0) + (tm * tk + tk * tn) * 2
    return _pc(body, name=name, grid=grid, in_specs=in_specs, out_specs=_bs((tm, tn), lambda j, i, k: (i, j)),
               out_shape=_sds((m, n), out_dtype), scratch=scratch, vmem=est + (8 << 20))(*args)


def _wgrad(a, b, buf, *, piece0, rows, tm, tn, tk, name, scale=None):
    t_dim, m = a.shape
    n = b.shape[1]
    db = tm // rows
    tpp = NDEV // db
    nk = t_dim // tk
    assert tm % rows == 0 and NDEV % db == 0 and m % tm == 0 and n % tn == 0 and t_dim % tk == 0
    grid = (m // tm, n // tn, nk)

    def body(a_ref, b_ref, buf_ref, o_ref, acc_ref):
        del buf_ref
        kk = pl.program_id(2)
        p = lax.dot_general(a_ref[...].astype(BF16), b_ref[...].astype(BF16), TN, preferred_element_type=F32)

        @pl.when(kk == 0)
        def _():
            acc_ref[...] = p

        @pl.when(kk > 0)
        def _():
            acc_ref[...] += p

        @pl.when(kk == nk - 1)
        def _():
            for d in range(db):
                v = acc_ref[d * rows:(d + 1) * rows, :]
                if scale is not None:
                    v = v * scale
                o_ref[0, d] = v.astype(BF16)

    est = 2 * (tk * tm * a.dtype.itemsize + tk * tn * b.dtype.itemsize + tm * tn * 2) + 3 * tm * tn * 4 + tk * tm * 4
    return _pc(
        body, name=name, grid=grid,
        in_specs=[_bs((tk, tm), lambda i, j, k: (k, i)), _bs((tk, tn), lambda i, j, k: (k, j)),
                  pl.BlockSpec(memory_space=pl.ANY)],
        out_specs=_bs((1, db, rows, tn), lambda i, j, k: (piece0 + i // tpp, i % tpp, 0, j)),
        out_shape=_sds(buf.shape, buf.dtype), scratch=[pltpu.VMEM((tm, tn), F32)], alias={2: 0},
        vmem=est + (8 << 20))(a, b, buf)


def _rms_fwd(x, gain, name, tm=512):
    t_dim = x.shape[0]

    def body(x_ref, g_ref, o_ref):
        xv = x_ref[...]
        r = lax.rsqrt(jnp.mean(xv * xv, axis=-1, keepdims=True) + EPS)
        o_ref[...] = (xv * r * g_ref[...]).astype(BF16)

    return _pc(body, name=name, grid=(t_dim // tm,),
               in_specs=[_bs((tm, D), lambda i: (i, 0)), _bs((1, D), lambda i: (0, 0))],
               out_specs=_bs((tm, D), lambda i: (i, 0)), out_shape=_sds((t_dim, D), BF16))(x, gain)


def _rms_bwd(x, gain, dh, dres, name, tm=512):
    t_dim = x.shape[0]

    def body(x_ref, g_ref, dh_ref, dr_ref, dx_ref, dg_ref):
        xv = x_ref[...]
        r = lax.rsqrt(jnp.mean(xv * xv, axis=-1, keepdims=True) + EPS)
        dh_v = dh_ref[...].astype(F32)
        dyg = dh_v * g_ref[...]
        mq = jnp.mean(dyg * xv, axis=-1, keepdims=True)
        dx_ref[...] = dr_ref[...] + r * (dyg - xv * (r * r * mq))

        @pl.when(pl.program_id(0) == 0)
        def _():
            dg_ref[...] = jnp.zeros_like(dg_ref)

        dg_ref[...] += jnp.sum(dh_v * (xv * r), axis=0, keepdims=True)

    return _pc(body, name=name, grid=(t_dim // tm,),
               in_specs=[_bs((tm, D), lambda i: (i, 0)), _bs((1, D), lambda i: (0, 0)),
                         _bs((tm, D), lambda i: (i, 0)), _bs((tm, D), lambda i: (i, 0))],
               out_specs=[_bs((tm, D), lambda i: (i, 0)), _bs((1, D), lambda i: (0, 0))],
               out_shape=[_sds((t_dim, D), F32), _sds((1, D), F32)])(x, gain, dh, dres)


def _sigmoid(v):
    return 1.0 / (1.0 + jnp.exp(-v))


def _swiglu_fwd(gu, name, tm=256):
    t_dim = gu.shape[0]

    def body(gu_ref, o_ref):
        g = gu_ref[:, :FF].astype(F32)
        u = gu_ref[:, FF:].astype(F32)
        o_ref[...] = (g * _sigmoid(g) * u).astype(BF16)

    return _pc(body, name=name, grid=(t_dim // tm,), in_specs=[_bs((tm, 2 * FF), lambda i: (i, 0))],
               out_specs=_bs((tm, FF), lambda i: (i, 0)), out_shape=_sds((t_dim, FF), BF16))(gu)


def _swiglu_bwd(gu, da, name, tm=256):
    t_dim = gu.shape[0]

    def body(gu_ref, da_ref, o_ref):
        g = gu_ref[:, :FF].astype(F32)
        u = gu_ref[:, FF:].astype(F32)
        dav = da_ref[...].astype(F32)
        s = _sigmoid(g)
        o_ref[:, :FF] = (dav * u * (s * (1.0 + g * (1.0 - s)))).astype(BF16)
        o_ref[:, FF:] = (dav * (g * s)).astype(BF16)

    return _pc(body, name=name, grid=(t_dim // tm,),
               in_specs=[_bs((tm, 2 * FF), lambda i: (i, 0)), _bs((tm, FF), lambda i: (i, 0))],
               out_specs=_bs((tm, 2 * FF), lambda i: (i, 0)), out_shape=_sds((t_dim, 2 * FF), BF16))(gu, da)


def _final_loss(x, gain, tgt, name, tm=512):
    t_dim = x.shape[0]

    def body(x_ref, g_ref, t_ref, loss_ref, dx_ref, dg_ref):
        xv = x_ref[...]
        gv = g_ref[...]
        r = lax.rsqrt(jnp.mean(xv * xv, axis=-1, keepdims=True) + EPS)
        xn = xv * r
        err = xn * gv - t_ref[...]
        dy = err * (1.0 / D)
        dyg = dy * gv
        mq = jnp.mean(dyg * xv, axis=-1, keepdims=True)
        dx_ref[...] = r * (dyg - xv * (r * r * mq))

        @pl.when(pl.program_id(0) == 0)
        def _():
            dg_ref[...] = jnp.zeros_like(dg_ref)
            loss_ref[...] = jnp.zeros_like(loss_ref)

        dg_ref[...] += jnp.sum(dy * xn, axis=0, keepdims=True)
        part = 0.5 * jnp.sum(jnp.mean(err * err, axis=-1, keepdims=True), axis=0, keepdims=True)
        loss_ref[...] += jnp.broadcast_to(part, loss_ref.shape)

    return _pc(body, name=name, grid=(t_dim // tm,),
               in_specs=[_bs((tm, D), lambda i: (i, 0)), _bs((1, D), lambda i: (0, 0)), _bs((tm, D), lambda i: (i, 0))],
               out_specs=[_bs((8, 128), lambda i: (0, 0)), _bs((tm, D), lambda i: (i, 0)), _bs((1, D), lambda i: (0, 0))],
               out_shape=[_sds((8, 128), F32), _sds((t_dim, D), F32), _sds((1, D), F32)])(x, gain, tgt)


def _band_mask(n):
    ri = lax.broadcasted_iota(jnp.int32, (BLK, 2 * BLK), 0)
    cj = lax.broadcasted_iota(jnp.int32, (BLK, 2 * BLK), 1)
    dist = ri + BLK - cj
    return (dist >= 0) & (dist <= BLK) & ((cj >= BLK) | (n > 0))


def _attn_fwd(z, acc, ml, *, gi, first, final, name):
    dil = DILS[gi]
    t_dim = z.shape[0]
    ln = t_dim // dil
    nb = ln // BLK
    zv = z.reshape(ln, dil * INW)

    def own(off):
        return _bs((BLK, GW), lambda r, n: (n, r * ZB + off + gi))

    def prev(off):
        return _bs((BLK, GW), lambda r, n: (jnp.maximum(n - 1, 0), r * ZB + off + gi))

    st_acc = _bs((BLK, GW), lambda r, n: (n, r))
    st_ml = _bs((BLK, 128), lambda r, n: (n, r))
    in_specs = [own(Q_B), own(K_B), prev(K_B), own(V_B), prev(V_B)]
    args = [zv, zv, zv, zv, zv]
    if not first:
        in_specs += [st_acc, st_ml]
        args += [acc.reshape(ln, dil * GW), ml.reshape(ln, dil * 128)]

    def body(*refs):
        q_ref, ko_ref, kp_ref, vo_ref, vp_ref = refs[:5]
        o_acc_ref, o_ml_ref = refs[-2], refs[-1]
        n = pl.program_id(1)
        q = q_ref[...] * QK_SCALE
        kb = jnp.concatenate([kp_ref[...], ko_ref[...]], axis=0).astype(BF16)
        vb = jnp.concatenate([vp_ref[...], vo_ref[...]], axis=0).astype(BF16)
        valid = _band_mask(n)
        lane = lax.broadcasted_iota(jnp.int32, (BLK, GW), 1)
        lane_s = lax.broadcasted_iota(jnp.int32, (BLK, 128), 1)
        if first:
            acc_v = jnp.zeros((BLK, GW), F32)
            ml_v = jnp.where(lane_s < 64, NEG, 0.0).astype(F32)
        else:
            acc_v = refs[5][...]
            ml_v = refs[6][...]
        acc_o, ml_o = acc_v, ml_v
        l_b = jnp.ones((BLK, GW), F32)
        for h in range(NH):
            hm = (lane >= h * HD) & (lane < (h + 1) * HD)
            qh = jnp.where(hm, q, 0.0).astype(BF16)
            s = lax.dot_general(qh, kb, NT, preferred_element_type=F32)
            s = jnp.where(valid, s, NEG)
            m_old = ml_v[:, 16 * h:16 * h + 1]
            l_old = ml_v[:, 64 + 16 * h:64 + 16 * h + 1]
            m_new = jnp.maximum(m_old, jnp.max(s, axis=-1, keepdims=True))
            alpha = jnp.exp(m_old - m_new)
            p = jnp.exp(s - m_new)
            l_new = alpha * l_old + jnp.sum(p, axis=-1, keepdims=True)
            pv = jnp.dot(p.astype(BF16), vb, preferred_element_type=F32)
            acc_o = jnp.where(hm, acc_v * alpha + pv, acc_o)
            l_b = jnp.where(hm, l_new, l_b)
            if final:
                ml_o = jnp.where((lane_s >= 16 * h) & (lane_s < 16 * h + 16), m_new + jnp.log(l_new), ml_o)
            else:
                ml_o = jnp.where((lane_s >= 16 * h) & (lane_s < 16 * h + 16), m_new, ml_o)
                ml_o = jnp.where((lane_s >= 64 + 16 * h) & (lane_s < 64 + 16 * h + 16), l_new, ml_o)
        o_acc_ref[...] = acc_o / l_b if final else acc_o
        o_ml_ref[...] = ml_o

    out = _pc(body, name=name, grid=(dil, nb), in_specs=in_specs, out_specs=[st_acc, st_ml],
              out_shape=[_sds((ln, dil * GW), F32), _sds((ln, dil * 128), F32)])(*args)
    return out[0].reshape(t_dim, GW), out[1].reshape(t_dim, 128)


def _attn_bwd(z, dy, y, lse, *, gi, name):
    dil = DILS[gi]
    t_dim = z.shape[0]
    ln = t_dim // dil
    nb = ln // BLK
    zv = z.reshape(ln, dil * INW)

    def cur(n):
        return jnp.minimum(n, nb - 1)

    def own(off):
        return _bs((BLK, GW), lambda r, n: (cur(n), r * ZB + off + gi))

    def prev(off):
        return _bs((BLK, GW), lambda r, n: (jnp.maximum(cur(n) - 1, 0), r * ZB + off + gi))

    tok = _bs((BLK, GW), lambda r, n: (cur(n), r))
    tok_s = _bs((BLK, 128), lambda r, n: (cur(n), r))
    kv_out = _bs((BLK, GW), lambda r, n: (jnp.maximum(n - 1, 0), r))

    def body(q_ref, ko_ref, kp_ref, vo_ref, vp_ref, dy_ref, y_ref, lse_ref, dq_ref, dk_ref, dv_ref, dkc, dvc):
        n = pl.program_id(1)

        @pl.when(n < nb)
        def _():
            q = q_ref[...] * QK_SCALE
            kb = jnp.concatenate([kp_ref[...], ko_ref[...]], axis=0).astype(BF16)
            vb = jnp.concatenate([vp_ref[...], vo_ref[...]], axis=0).astype(BF16)
            valid = _band_mask(n)
            lane = lax.broadcasted_iota(jnp.int32, (BLK, GW), 1)
            dyv = dy_ref[...]
            yv = y_ref[...]
            lse_v = lse_ref[...]
            dq = jnp.zeros((BLK, GW), F32)
            dkb = jnp.zeros((2 * BLK, GW), F32)
            dvb = jnp.zeros((2 * BLK, GW), F32)
            for h in range(NH):
                hm = (lane >= h * HD) & (lane < (h + 1) * HD)
                qh = jnp.where(hm, q, 0.0).astype(BF16)
                s = lax.dot_general(qh, kb, NT, preferred_element_type=F32)
                p = jnp.where(valid, jnp.exp(s - lse_v[:, 16 * h:16 * h + 1]), 0.0)
                dyh = jnp.where(hm, dyv, 0.0)
                delta = jnp.sum(dyh * yv, axis=-1, keepdims=True)
                dyh16 = dyh.astype(BF16)
                dp = lax.dot_general(dyh16, vb, NT, preferred_element_type=F32)
                ds = (p * (dp - delta)).astype(BF16)
                dq = dq + jnp.where(hm, jnp.dot(ds, kb, preferred_element_type=F32), 0.0)
                dkb = dkb + lax.dot_general(ds, qh, TN, preferred_element_type=F32)
                dvb = dvb + lax.dot_general(p.astype(BF16), dyh16, TN, preferred_element_type=F32)
            dq_ref[...] = (dq * QK_SCALE).astype(BF16)

            @pl.when(n > 0)
            def _():
                dk_ref[...] = (dkc[...] + dkb[:BLK]).astype(BF16)
                dv_ref[...] = (dvc[...] + dvb[:BLK]).astype(BF16)

            dkc[...] = dkb[BLK:]
            dvc[...] = dvb[BLK:]

        @pl.when(n == nb)
        def _():
            dk_ref[...] = dkc[...].astype(BF16)
            dv_ref[...] = dvc[...].astype(BF16)

    outs = _pc(body, name=name, grid=(dil, nb + 1),
               in_specs=[own(Q_B), own(K_B), prev(K_B), own(V_B), prev(V_B), tok, tok, tok_s],
               out_specs=[tok, kv_out, kv_out],
               out_shape=[_sds((ln, dil * GW), BF16)] * 3,
               scratch=[pltpu.VMEM((BLK, GW), F32), pltpu.VMEM((BLK, GW), F32)])(
                   zv, zv, zv, zv, zv, dy.reshape(ln, dil * GW), y.reshape(ln, dil * GW), lse.reshape(ln, dil * 128))
    return [o.reshape(t_dim, GW) for o in outs]


_INV_SQRT2 = 1.0 / math.sqrt(2.0)
_INV_SQRT2PI = 1.0 / math.sqrt(2.0 * math.pi)


def _gelu(v):
    return 0.5 * v * (1.0 + lax.erf(v * _INV_SQRT2))


def _gelu_grad(v):
    return 0.5 * (1.0 + lax.erf(v * _INV_SQRT2)) + v * (_INV_SQRT2PI * jnp.exp(-0.5 * v * v))


def _gmlp_core(z_refs, lng, lnb, ws_ref, bs_ref):
    u_pre = jnp.concatenate([z_refs[0][...], z_refs[1][...]], axis=1)
    v_pre = jnp.concatenate([z_refs[2][...], z_refs[3][...]], axis=1)
    u = _gelu(u_pre)
    v = _gelu(v_pre)
    xc = v - jnp.mean(v, axis=-1, keepdims=True)
    rstd = lax.rsqrt(jnp.mean(xc * xc, axis=-1, keepdims=True) + EPS)
    xhat = xc * rstd
    vn = xhat * lng + lnb
    ti = lax.broadcasted_iota(jnp.int32, (BLK, BLK), 0)
    si = lax.broadcasted_iota(jnp.int32, (BLK, BLK), 1)
    causal = ti >= si
    vn16 = vn.astype(BF16)
    ws = [jnp.where(causal, ws_ref[g], 0.0).astype(BF16) for g in range(4)]
    mixed = jnp.concatenate(
        [jnp.dot(ws[g], vn16[:, g * BLK:(g + 1) * BLK], preferred_element_type=F32) + bs_ref[g] for g in range(4)], axis=1)
    return u_pre, v_pre, u, xhat, rstd, vn16, ws, causal, mixed


def _gmlp_specs():
    z_specs = [_bs((BLK, GW), (lambda i, c=c: (i, UV_B + c))) for c in range(4)]
    par_specs = [_bs((1, BW), lambda i: (0, 0)), _bs((1, BW), lambda i: (0, 0)),
                 _bs((4, BLK, BLK), lambda i: (0, 0, 0)), _bs((4, BLK, 1), lambda i: (0, 0, 0))]
    return z_specs, par_specs


def _gmlp_fwd(z, lng, lnb, ws, bs, name):
    t_dim = z.shape[0]
    z_specs, par_specs = _gmlp_specs()

    def body(z0, z1, z2, z3, lng_ref, lnb_ref, ws_ref, bs_ref, o_ref):
        core = _gmlp_core((z0, z1, z2, z3), lng_ref[...], lnb_ref[...], ws_ref, bs_ref)
        o_ref[...] = (core[2] * core[8]).astype(BF16)

    return _pc(body, name=name, grid=(t_dim // BLK,), in_specs=z_specs + par_specs,
               out_specs=_bs((BLK, BW), lambda i: (i, 0)), out_shape=_sds((t_dim, BW), BF16))(z, z, z, z, lng, lnb, ws, bs)


def _gmlp_bwd(z, dy, lng, lnb, ws, bs, name):
    t_dim = z.shape[0]
    z_specs, par_specs = _gmlp_specs()

    def body(z0, z1, z2, z3, lng_ref, lnb_ref, ws_ref, bs_ref, dy_ref, duv_ref, dws_ref, dbs_ref, dlng_ref, dlnb_ref):
        lng_v = lng_ref[...]
        u_pre, v_pre, u, xhat, rstd, vn16, wsm, causal, mixed = _gmlp_core((z0, z1, z2, z3), lng_v, lnb_ref[...], ws_ref, bs_ref)
        dyv = dy_ref[...].astype(F32)
        du = dyv * mixed
        dmixed = dyv * u

        @pl.when(pl.program_id(0) == 0)
        def _():
            dws_ref[...] = jnp.zeros_like(dws_ref)
            dbs_ref[...] = jnp.zeros_like(dbs_ref)
            dlng_ref[...] = jnp.zeros_like(dlng_ref)
            dlnb_ref[...] = jnp.zeros_like(dlnb_ref)

        dvn_parts = []
        for g in range(4):
            dm = dmixed[:, g * BLK:(g + 1) * BLK]
            dm16 = dm.astype(BF16)
            dw = lax.dot_general(dm16, vn16[:, g * BLK:(g + 1) * BLK], NT, preferred_element_type=F32)
            dws_ref[g] += jnp.where(causal, dw, 0.0)
            dbs_ref[g] += jnp.sum(dm, axis=-1, keepdims=True)
            dvn_parts.append(lax.dot_general(wsm[g], dm16, TN, preferred_element_type=F32))
        dvn = jnp.concatenate(dvn_parts, axis=1)
        dlng_ref[...] += jnp.sum(dvn * xhat, axis=0, keepdims=True)
        dlnb_ref[...] += jnp.sum(dvn, axis=0, keepdims=True)
        dxh = dvn * lng_v
        dv = rstd * (dxh - jnp.mean(dxh, axis=-1, keepdims=True) - xhat * jnp.mean(dxh * xhat, axis=-1, keepdims=True))
        duv_ref[:, :BW] = (du * _gelu_grad(u_pre)).astype(BF16)
        duv_ref[:, BW:] = (dv * _gelu_grad(v_pre)).astype(BF16)

    return _pc(body, name=name, grid=(t_dim // BLK,),
               in_specs=z_specs + par_specs + [_bs((BLK, BW), lambda i: (i, 0))],
               out_specs=[_bs((BLK, 2 * BW), lambda i: (i, 0)), _bs((4, BLK, BLK), lambda i: (0, 0, 0)),
                          _bs((4, BLK, 1), lambda i: (0, 0, 0)), _bs((1, BW), lambda i: (0, 0)), _bs((1, BW), lambda i: (0, 0))],
               out_shape=[_sds((t_dim, 2 * BW), BF16), _sds((4, BLK, BLK), F32), _sds((4, BLK, 1), F32),
                          _sds((1, BW), F32), _sds((1, BW), F32)])(z, z, z, z, lng, lnb, ws, bs, dy)


def _pool_core(x, gi):
    t_dim = x.shape[0]
    w = jnp.left_shift(2, gi)
    row = lax.broadcasted_iota(jnp.int32, x.shape, 0)
    s = x
    for k in (1, 2, 4, 8):
        sh = jnp.where(row >= k, pltpu.roll(s, k, 0), 0.0)
        s = jnp.where(k < w, s + sh, s)
    cnt = jnp.minimum(row + 1, w).astype(F32)
    return s / cnt - x, cnt, row, w, t_dim


def _pool_fwd(z, cw, cs, name):
    t_dim = z.shape[0]

    def body(z_ref, cw_ref, cs_ref, o_ref):
        pooled = _pool_core(z_ref[...], pl.program_id(0))[0]
        y = jnp.dot(pooled.astype(BF16), cw_ref[0].astype(BF16), preferred_element_type=F32) * cs_ref[0]
        o_ref[...] = y.astype(BF16)

    return _pc(body, name=name, grid=(4,),
               in_specs=[_bs((t_dim, BLK), lambda g: (0, C_B128 + g)), _bs((1, BLK, BLK), lambda g: (g, 0, 0)),
                         _bs((1, 1, BLK), lambda g: (g, 0, 0))],
               out_specs=_bs((t_dim, BLK), lambda g: (0, g)), out_shape=_sds((t_dim, BW), BF16),
               vmem=40 << 20)(z, cw, cs)


def _pool_bwd(z, dy, cw, cs, name):
    t_dim = z.shape[0]

    def body(z_ref, dy_ref, cw_ref, cs_ref, dx_ref, dcw_ref, dcs_ref):
        pooled, cnt, row, w, _ = _pool_core(z_ref[...], pl.program_id(0))
        p16 = pooled.astype(BF16)
        cw16 = cw_ref[0].astype(BF16)
        dyv = dy_ref[...].astype(F32)
        lin = jnp.dot(p16, cw16, preferred_element_type=F32)
        dcs_ref[0] = jnp.sum(dyv * lin, axis=0, keepdims=True)
        dys = (dyv * cs_ref[0]).astype(BF16)
        dcw_ref[0] = lax.dot_general(p16, dys, TN, preferred_element_type=F32)
        dpool = lax.dot_general(dys, cw16, NT, preferred_element_type=F32)
        sb = dpool / cnt
        for k in (1, 2, 4, 8):
            sh = jnp.where(row < t_dim - k, pltpu.roll(sb, t_dim - k, 0), 0.0)
            sb = jnp.where(k < w, sb + sh, sb)
        dx_ref[...] = (sb - dpool).astype(BF16)

    return _pc(body, name=name, grid=(4,),
               in_specs=[_bs((t_dim, BLK), lambda g: (0, C_B128 + g)), _bs((t_dim, BLK), lambda g: (0, g)),
                         _bs((1, BLK, BLK), lambda g: (g, 0, 0)), _bs((1, 1, BLK), lambda g: (g, 0, 0))],
               out_specs=[_bs((t_dim, BLK), lambda g: (0, g)), _bs((1, BLK, BLK), lambda g: (g, 0, 0)),
                          _bs((1, 1, BLK), lambda g: (g, 0, 0))],
               out_shape=[_sds((t_dim, BW), BF16), _sds((4, BLK, BLK), F32), _sds((4, 1, BLK), F32)],
               vmem=48 << 20)(z, dy, cw, cs)


def _gates_specs(tm):
    z_specs = [_bs((tm, GW), (lambda j, i, b=b: (i, G_B + 4 * b + j))) for b in range(3)]
    bg_specs = [_bs((1, GW), (lambda j, i, b=b: (0, 4 * b + j))) for b in range(3)]
    tile = _bs((tm, GW), lambda j, i: (i, j))
    return z_specs, bg_specs, tile


def _gates_fwd(z, bg, pa, pb, pc, name, tm=512):
    t_dim = z.shape[0]
    z_specs, bg_specs, tile = _gates_specs(tm)

    def body(za, zb, zc, ba, bb, bc, pa_ref, pb_ref, pc_ref, o_ref):
        o_ref[...] = (_sigmoid(za[...] + ba[...]) * pa_ref[...] + _sigmoid(zb[...] + bb[...]) * pb_ref[...]
                      + _sigmoid(zc[...] + bc[...]) * pc_ref[...]).astype(BF16)

    return _pc(body, name=name, grid=(D // GW, t_dim // tm), in_specs=z_specs + bg_specs + [tile] * 3,
               out_specs=tile, out_shape=_sds((t_dim, D), BF16))(z, z, z, bg, bg, bg, pa, pb, pc)


def _gates_bwd(z, bg, pa, pb, pc, dm, name, tm=512):
    t_dim = z.shape[0]
    z_specs, bg_specs, tile = _gates_specs(tm)
    bias_out = _bs((1, GW), lambda j, i: (0, j))

    def body(za, zb, zc, ba, bb, bc, pa_ref, pb_ref, pc_ref, dm_ref, dpa, dpb, dpc, dza, dzb, dzc, dba, dbb, dbc):
        dmv = dm_ref[...]
        first = pl.program_id(1) == 0
        for z_ref, b_ref, p_ref, dp_ref, dz_ref, db_ref in ((za, ba, pa_ref, dpa, dza, dba), (zb, bb, pb_ref, dpb, dzb, dbb),
                                                            (zc, bc, pc_ref, dpc, dzc, dbc)):
            gate = _sigmoid(z_ref[...] + b_ref[...])
            dp_ref[...] = (gate * dmv).astype(BF16)
            dz = dmv * p_ref[...] * (gate * (1.0 - gate))
            dz_ref[...] = dz.astype(BF16)

            @pl.when(first)
            def _():
                db_ref[...] = jnp.zeros_like(db_ref)

            db_ref[...] += jnp.sum(dz, axis=0, keepdims=True)

    return _pc(body, name=name, grid=(D // GW, t_dim // tm), in_specs=z_specs + bg_specs + [tile] * 4,
               out_specs=[tile] * 6 + [bias_out] * 3,
               out_shape=[_sds((t_dim, D), BF16)] * 6 + [_sds((1, D), F32)] * 3)(z, z, z, bg, bg, bg, pa, pb, pc, dm)


def _coords():
    return lax.axis_index("x"), lax.axis_index("y"), lax.axis_index("c")


def _allgather(shards, name):
    na = len(shards)

    def body(*refs):
        ins, outs = refs[:na], refs[na:2 * na]
        send, recv, lsem = refs[2 * na:]
        x, y, c = _coords()
        me = 4 * x + 2 * y + c
        sib = (x, y, 1 - c)
        chips = [(1 - x, y), (x, 1 - y), (1 - x, 1 - y)]

        def ident(px, py, pc):
            return 4 * px + 2 * py + pc

        def cp(a, k, blk, to, src=None):
            dst = outs[a].at[:, blk]
            return pltpu.make_async_remote_copy(src_ref=dst if src is None else src, dst_ref=dst, send_sem=send.at[a * 7 + k],
                                                recv_sem=recv.at[a * 7 + k], device_id=to, device_id_type=MESH)

        mine = [pltpu.make_async_copy(ins[a], outs[a].at[:, me], lsem.at[a]) for a in range(na)]
        for m in mine:
            m.start()
        first = []
        for a in range(na):
            first.append(cp(a, 0, me, sib, src=ins[a]))
            first += [cp(a, 1 + j, me, (*chip, c), src=ins[a]) for j, chip in enumerate(chips)]
        for f in first:
            f.start()
        passed = []
        for j, chip in enumerate(chips):
            for a in range(na):
                cp(a, 1 + j, ident(*chip, c), (x, y, c)).wait_recv()
                fwd = cp(a, 4 + j, ident(*chip, c), sib)
                fwd.start()
                passed.append(fwd)
        for a in range(na):
            cp(a, 0, ident(x, y, 1 - c), (x, y, c)).wait_recv()
            for j, chip in enumerate(chips):
                cp(a, 4 + j, ident(*chip, 1 - c), (x, y, c)).wait_recv()
        for f in first + passed:
            f.wait_send()
        for m in mine:
            m.wait()

    any_spec = pl.BlockSpec(memory_space=pl.ANY)
    return pl.pallas_call(
        body, name=name, in_specs=[any_spec] * na, out_specs=[any_spec] * na,
        out_shape=[_sds((s.shape[0], NDEV) + s.shape[1:], s.dtype) for s in shards],
        scratch_shapes=[pltpu.SemaphoreType.DMA((7 * na,)), pltpu.SemaphoreType.DMA((7 * na,)), pltpu.SemaphoreType.DMA((na,))],
        interpret=False)(*shards)


def _alltoall(bufs, name):
    na = len(bufs)

    def body(*refs):
        ins, outs = refs[:na], refs[na:2 * na]
        send, recv, lsem = refs[2 * na:]
        x, y, c = _coords()
        me = 4 * x + 2 * y + c
        mine = [pltpu.make_async_copy(ins[a].at[:, me], outs[a].at[me], lsem.at[a]) for a in range(na)]
        for m in mine:
            m.start()
        copies = []
        for k in range(1, NDEV):
            px = 1 - x if k & 4 else x
            py = 1 - y if k & 2 else y
            pc = 1 - c if k & 1 else c
            peer = 4 * px + 2 * py + pc
            for a in range(na):
                copies.append(pltpu.make_async_remote_copy(
                    src_ref=ins[a].at[:, peer], dst_ref=outs[a].at[me], send_sem=send.at[a * 7 + k - 1],
                    recv_sem=recv.at[a * 7 + k - 1], device_id=(px, py, pc), device_id_type=MESH))
        for cpy in copies:
            cpy.start()
        for k in range(1, NDEV):
            px = 1 - x if k & 4 else x
            py = 1 - y if k & 2 else y
            pc = 1 - c if k & 1 else c
            peer = 4 * px + 2 * py + pc
            for a in range(na):
                pltpu.make_async_remote_copy(
                    src_ref=ins[a].at[:, peer], dst_ref=outs[a].at[peer], send_sem=send.at[a * 7 + k - 1],
                    recv_sem=recv.at[a * 7 + k - 1], device_id=(px, py, pc), device_id_type=MESH).wait_recv()
        for cpy in copies:
            cpy.wait_send()
        for m in mine:
            m.wait()

    any_spec = pl.BlockSpec(memory_space=pl.ANY)
    return pl.pallas_call(
        body, name=name, in_specs=[any_spec] * na, out_specs=[any_spec] * na,
        out_shape=[_sds((NDEV, b.shape[0]) + b.shape[2:], b.dtype) for b in bufs],
        scratch_shapes=[pltpu.SemaphoreType.DMA((7 * na,)), pltpu.SemaphoreType.DMA((7 * na,)), pltpu.SemaphoreType.DMA((na,))],
        interpret=False)(*bufs)


def _peer(k, x, y, c):
    px = 1 - x if k & 4 else x
    py = 1 - y if k & 2 else y
    pc = 1 - c if k & 1 else c
    return (px, py, pc), 4 * px + 2 * py + pc


def _exchange_copies(kind, srcs, lands, send, recv):
    x, y, c = _coords()
    me = 4 * x + 2 * y + c
    out = []
    for k in range(1, NDEV):
        to, peer = _peer(k, x, y, c)
        for a in range(len(srcs)):
            sems = dict(send_sem=send.at[a * 7 + k - 1], recv_sem=recv.at[a * 7 + k - 1], device_id=to, device_id_type=MESH)
            if kind == "gather":
                mine = pltpu.make_async_remote_copy(src_ref=srcs[a], dst_ref=lands[a].at[:, me], **sems)
                theirs = pltpu.make_async_remote_copy(src_ref=srcs[a], dst_ref=lands[a].at[:, peer], **sems)
            else:
                mine = pltpu.make_async_remote_copy(src_ref=srcs[a].at[:, peer], dst_ref=lands[a].at[me], **sems)
                theirs = pltpu.make_async_remote_copy(src_ref=srcs[a].at[:, peer], dst_ref=lands[a].at[peer], **sems)
            out.append((mine, theirs))
    return out


def _land_shape(kind, s):
    if kind == "gather":
        return (s.shape[0], NDEV) + s.shape[1:]
    return (NDEV, s.shape[0]) + s.shape[2:]


_HBM = pl.BlockSpec(memory_space=pltpu.HBM)
_SEM = pl.BlockSpec(memory_space=pltpu.SEMAPHORE)
_EFFECT = pltpu.SideEffectType.DATAFLOW_SIDE_EFFECTING


def _exchange_start(kind, srcs, after, name):
    na = len(srcs)

    def body(*refs):
        src, land = refs[:na], refs[na:2 * na]
        send, recv = refs[2 * na + 1], refs[2 * na + 2]
        token = refs[-1]
        for mine, _ in _exchange_copies(kind, src, land, send, recv):
            mine.start()
        token[...] = jnp.zeros_like(token)

    lands = [pltpu.with_memory_space_constraint(lax.empty(_land_shape(kind, s), s.dtype), pltpu.HBM) for s in srcs]
    srcs = [pltpu.with_memory_space_constraint(s, pltpu.HBM) for s in srcs]
    outs = pl.pallas_call(
        body, name=name,
        out_shape=(pltpu.SemaphoreType.DMA((7 * na,)), pltpu.SemaphoreType.DMA((7 * na,)),
                   *[pltpu.HBM(s.shape, s.dtype) for s in srcs], *[pltpu.HBM(b.shape, b.dtype) for b in lands],
                   _sds((8, 128), F32)),
        in_specs=[_HBM] * (2 * na) + [pl.BlockSpec(memory_space=pl.ANY)],
        out_specs=(_SEM, _SEM, *([_HBM] * (2 * na)), pl.BlockSpec(memory_space=pltpu.VMEM)),
        input_output_aliases={i: 2 + i for i in range(2 * na)},
        compiler_params=pltpu.CompilerParams(has_side_effects=_EFFECT), interpret=False)(*srcs, *lands, after)
    return outs[0], outs[1], list(outs[2:2 + na]), list(outs[2 + na:2 + 2 * na]), outs[-1]


def _exchange_wait(kind, started, after, name):
    send, recv, srcs, lands, _ = started
    na = len(srcs)

    def body(*refs):
        src, land = refs[:na], refs[na:2 * na]
        send_ref, recv_ref = refs[2 * na], refs[2 * na + 1]
        lsem = refs[-1]
        x, y, c = _coords()
        me = 4 * x + 2 * y + c
        own = []
        for a in range(na):
            if kind == "gather":
                own.append(pltpu.make_async_copy(src[a], land[a].at[:, me], lsem.at[a]))
            else:
                own.append(pltpu.make_async_copy(src[a].at[:, me], land[a].at[me], lsem.at[a]))
            own[-1].start()
        for mine, theirs in _exchange_copies(kind, src, land, send_ref, recv_ref):
            mine.wait_send()
            theirs.wait_recv()
        for o in own:
            o.wait()

    outs = pl.pallas_call(
        body, name=name,
        out_shape=tuple(pltpu.HBM(s.shape, s.dtype) for s in srcs + lands),
        in_specs=[_HBM] * (2 * na) + [_SEM, _SEM, pl.BlockSpec(memory_space=pl.ANY)],
        out_specs=tuple([_HBM] * (2 * na)),
        input_output_aliases={i: i for i in range(2 * na)},
        scratch_shapes=[pltpu.SemaphoreType.DMA((na,))],
        compiler_params=pltpu.CompilerParams(has_side_effects=_EFFECT), interpret=False)(*srcs, *lands, send, recv, after)
    return list(outs[na:])


def _sum_slots(buf, name, slot_axis):
    if slot_axis == 0:
        _, p, rows, cols = buf.shape
    else:
        p, _, rows, cols = buf.shape
    tr = _row_tile(rows, 32 // buf.dtype.itemsize, 1024 if cols <= 128 else 512)
    if slot_axis == 0:
        in_spec = _bs((NDEV, 1, tr, cols), lambda i, j: (0, i, j, 0))
    else:
        in_spec = _bs((1, NDEV, tr, cols), lambda i, j: (i, 0, j, 0))

    def body(b_ref, o_ref):
        def slot(d):
            return (b_ref[d, 0] if slot_axis == 0 else b_ref[0, d]).astype(F32)
        s = slot(0)
        for d in range(1, NDEV):
            s = s + slot(d)
        o_ref[0] = s

    return _pc(body, name=name, grid=(p, rows // tr), in_specs=[in_spec], out_specs=_bs((1, tr, cols), lambda i, j: (i, j, 0)),
               out_shape=_sds((p, rows, cols), F32))(buf)


def _adamw(w, g, m, v, name):
    r, c = w.shape
    tr = _row_tile(r, 8, 1024 if c <= 512 else 512)

    def body(w_ref, g_ref, m_ref, v_ref, d_ref, nm_ref, nv_ref):
        gv = g_ref[...]
        m2 = ADAM_B1 * m_ref[...] + (1.0 - ADAM_B1) * gv
        v2 = ADAM_B2 * v_ref[...] + (1.0 - ADAM_B2) * jnp.square(gv)
        m_hat = m2 / (1.0 - ADAM_B1 ** ADAM_STEP)
        v_hat = v2 / (1.0 - ADAM_B2 ** ADAM_STEP)
        d_ref[...] = -ADAM_LR * (m_hat / (jnp.sqrt(v_hat) + ADAM_EPS) + ADAM_WD * w_ref[...])
        nm_ref[...] = m2
        nv_ref[...] = v2

    spec = _bs((tr, c), lambda i: (i, 0))
    return _pc(body, name=name, grid=(r // tr,), in_specs=[spec] * 4, out_specs=[spec] * 3,
               out_shape=[_sds((r, c), F32)] * 3)(w, g, m, v)


PG, PD = 0, 2


def _ffn_fwd(x, gain, wf, tag):
    h = _rms_fwd(x, gain, f"rms_{tag}")
    gu = _mm(h, wf, mode="nt", tm=512, tn=FF // 2, tk=D, n=2 * FF, b_off=2 * PG, out_dtype=BF16, name=f"gu_{tag}")
    act = _swiglu_fwd(gu, f"swiglu_{tag}")
    out = _mm(act, wf, mode="nn", tm=512, tn=512, tk=FF, n=D, b_off=PD, out_dtype=F32, scale=0.5, res=x, name=f"down_{tag}")
    return out, (x, h, gu, act)


def _ffn_bwd(dout, saved, gain, wf, tag):
    x, h, gu, act = saved
    grf = jnp.zeros((3, NDEV, FFS, D), BF16)
    da = _mm(dout, wf, mode="nt", tm=512, tn=FF // 2, tk=D, n=FF, b_off=2 * PD, out_dtype=BF16, scale=0.5, name=f"dact_{tag}")
    grf = _wgrad(act, dout, grf, piece0=PD, rows=FFS, tm=FF, tn=512, tk=512, scale=0.5, name=f"wg_down_{tag}")
    dgu = _swiglu_bwd(gu, da, f"dswiglu_{tag}")
    dh = _mm(dgu, wf, mode="nn", tm=512, tn=512, tk=FF, n=D, b_off=PG, out_dtype=F32, name=f"dh_{tag}")
    grf = _wgrad(dgu, h, grf, piece0=PG, rows=FFS, tm=FF, tn=512, tk=512, name=f"wg_gu_{tag}")
    dx, dgain = _rms_bwd(x, gain, dh, dout, f"drms_{tag}")
    return dx, dgain, grf


def _mixer_fwd(x, sp, ww, wo, wa, wbc, tag):
    h = _rms_fwd(x, sp["mix_norm"], f"rms_mix_{tag}")
    z = _mm(h, ww, mode="nt", tm=512, tn=1152, tk=D, n=INW, out_dtype=F32, name=f"z_{tag}")
    acc, ml = _attn_fwd(z, None, None, gi=0, first=True, final=False, name=f"attn0_{tag}")
    acc, ml = _attn_fwd(z, acc, ml, gi=1, first=False, final=False, name=f"attn1_{tag}")
    ya, lse = _attn_fwd(z, acc, ml, gi=2, first=False, final=True, name=f"attn2_{tag}")
    yb = _gmlp_fwd(z, sp["b_ln_g"], sp["b_ln_b"], sp["b_w_s"], sp["b_b_s"], f"gmlp_{tag}")
    yc = _pool_fwd(z, sp["c_w"], sp["c_scale"], f"pool_{tag}")
    pa = _mm(ya, wa, mode="nt", tm=1024, tn=512, tk=GW, n=D, out_dtype=F32, name=f"proj_a_{tag}")
    pb = _mm(yb, wbc, mode="nt", tm=1024, tn=512, tk=BW, n=D, b_off=0, out_dtype=F32, name=f"proj_b_{tag}")
    pc = _mm(yc, wbc, mode="nt", tm=1024, tn=512, tk=BW, n=D, b_off=2, out_dtype=F32, name=f"proj_c_{tag}")
    merged = _gates_fwd(z, sp["b_gate"], pa, pb, pc, f"gates_{tag}")
    out = _mm(merged, wo, mode="nn", tm=512, tn=512, tk=D, n=D, out_dtype=F32, res=x, name=f"out_{tag}")
    return out, (x, h, z, ya, lse, yb, yc, pa, pb, pc, merged)


def _mixer_bwd(dout, saved, sp, ww, wo, wa, wbc, tag):
    x, h, z, ya, lse, yb, yc, pa, pb, pc, merged = saved
    grw = jnp.zeros((1, NDEV, INS, D), BF16)
    gro = jnp.zeros((1, NDEV, DS, D), BF16)
    gra = jnp.zeros((1, NDEV, DS, GW), BF16)
    grbc = jnp.zeros((2, NDEV, DS, BW), BF16)
    dm = _mm(dout, wo, mode="nt", tm=512, tn=512, tk=D, n=D, out_dtype=F32, name=f"dmerged_{tag}")
    gro = _wgrad(merged, dout, gro, piece0=0, rows=DS, tm=D, tn=D, tk=512, name=f"wg_out_{tag}")
    dpa, dpb, dpc, dza, dzb, dzc, dba, dbb, dbc = _gates_bwd(z, sp["b_gate"], pa, pb, pc, dm, f"dgates_{tag}")
    dya = _mm(dpa, wa, mode="nn", tm=1024, tn=GW, tk=D, n=GW, out_dtype=F32, name=f"dya_{tag}")
    dyb = _mm(dpb, wbc, mode="nn", tm=1024, tn=BW, tk=D, n=BW, b_off=0, out_dtype=F32, name=f"dyb_{tag}")
    dyc = _mm(dpc, wbc, mode="nn", tm=1024, tn=BW, tk=D, n=BW, b_off=1, out_dtype=F32, name=f"dyc_{tag}")
    gra = _wgrad(dpa, ya, gra, piece0=0, rows=DS, tm=D, tn=GW, tk=512, name=f"wg_pa_{tag}")
    grbc = _wgrad(dpb, yb, grbc, piece0=0, rows=DS, tm=D, tn=BW, tk=512, name=f"wg_pb_{tag}")
    grbc = _wgrad(dpc, yc, grbc, piece0=1, rows=DS, tm=D, tn=BW, tk=512, name=f"wg_pc_{tag}")
    dqkv = [_attn_bwd(z, dya, ya, lse, gi=gi, name=f"dattn{gi}_{tag}") for gi in range(3)]
    duv, dws, dbs, dlng, dlnb = _gmlp_bwd(z, dyb, sp["b_ln_g"], sp["b_ln_b"], sp["b_w_s"], sp["b_b_s"], f"dgmlp_{tag}")
    dxc, dcw, dcs = _pool_bwd(z, dyc, sp["c_w"], sp["c_scale"], f"dpool_{tag}")
    dz = jnp.concatenate([dqkv[gi][part] for part in range(3) for gi in range(3)] + [duv, dxc, dza, dzb, dzc], axis=1)
    dh = _mm(dz, ww, mode="nn", tm=512, tn=512, tk=INW // 2, n=D, out_dtype=F32, name=f"dh_mix_{tag}")
    grw = _wgrad(dz, h, grw, piece0=0, rows=INS, tm=INW // 2, tn=512, tk=512, name=f"wg_in_{tag}")
    dx, dgain = _rms_bwd(x, sp["mix_norm"], dh, dout, f"drms_mix_{tag}")
    small = {"mix_norm": dgain, "b_gate": jnp.concatenate([dba, dbb, dbc], axis=1), "b_ln_g": dlng, "b_ln_b": dlnb,
             "b_w_s": dws, "b_b_s": dbs, "c_w": dcw, "c_scale": dcs}
    return dx, small, [grw, gro, gra, grbc]


PARTS = ("f1", "mix", "f2")


def _forward(xs, tgt, weights_of, small, final_gain):
    saved, views = [], []
    cur = xs
    for l in range(DEPTH):
        (wf1,) = weights_of(l, "f1", cur)
        cur, s1 = _ffn_fwd(cur, small[l]["ffn1_norm"], wf1, f"f1l{l}")
        wmix = weights_of(l, "mix", cur)
        cur, s2 = _mixer_fwd(cur, small[l], *wmix, f"l{l}")
        (wf2,) = weights_of(l, "f2", cur)
        cur, s3 = _ffn_fwd(cur, small[l]["ffn2_norm"], wf2, f"f2l{l}")
        saved.append((s1, s2, s3))
        views.append((wf1, wmix, wf2))
    loss_part, dcur, dfinal = _final_loss(cur, final_gain, tgt, "final_loss")
    return loss_part, dcur, dfinal, saved, views


def _backward_layer(dcur, saved_l, small_l, views_l, l, emit):
    wf1, wmix, wf2 = views_l
    s1, s2, s3 = saved_l
    dcur, dn2, grf2 = _ffn_bwd(dcur, s3, small_l["ffn2_norm"], wf2, f"f2l{l}")
    emit("f2", [grf2])
    dcur, sg, gmix = _mixer_bwd(dcur, s2, small_l, *wmix, f"l{l}")
    emit("mix", gmix)
    dcur, dn1, grf1 = _ffn_bwd(dcur, s1, small_l["ffn1_norm"], wf1, f"f1l{l}")
    emit("f1", [grf1])
    sg["ffn1_norm"] = dn1
    sg["ffn2_norm"] = dn2
    return dcur, sg


SMALL_LAYER = ("ffn1_norm", "mix_norm", "b_gate", "b_ln_g", "b_ln_b", "b_w_s", "b_b_s", "c_w", "c_scale", "ffn2_norm")
SMALL_ALL = SMALL_LAYER + ("final_norm",)
BIG = ("ffn1_w_gate", "ffn1_w_up", "ffn1_w_down", "w_in", "w_proj_a", "w_proj_b", "w_proj_c", "w_out",
       "ffn2_w_gate", "ffn2_w_up", "ffn2_w_down")
ALL_WEIGHTS = ("ffn1_norm", "ffn1_w_gate", "ffn1_w_up", "ffn1_w_down", "mix_norm", "w_in", "b_gate", "b_ln_g", "b_ln_b",
               "b_w_s", "b_b_s", "c_w", "c_scale", "w_proj_a", "w_proj_b", "w_proj_c", "w_out", "ffn2_norm", "ffn2_w_gate",
               "ffn2_w_up", "ffn2_w_down", "final_norm")


def _part_shards(w, l, part):
    t16 = lambda a: a.T.astype(BF16)
    if part == "mix":
        return [t16(w["w_in"][l])[None], w["w_out"][l].astype(BF16)[None], t16(w["w_proj_a"][l])[None],
                jnp.stack([t16(w["w_proj_b"][l]), t16(w["w_proj_c"][l])])]
    f = "ffn1" if part == "f1" else "ffn2"
    return [jnp.stack([t16(w[f + "_w_gate"][l]), t16(w[f + "_w_up"][l]), w[f + "_w_down"][l].astype(BF16)])]


def _part_views(part, gathered):
    if part == "mix":
        pw, po, pa, pbc = gathered
        return [pw.reshape(INW, D), po.reshape(D, D), pa.reshape(D, GW), pbc.reshape(2 * D, BW)]
    return [gathered[0].reshape(3 * FF, D)]


def _part_grads(part, sums):
    if part == "mix":
        gw, go, ga, gbc = sums
        return {"w_in": gw[0].T, "w_out": go[0], "w_proj_a": ga[0].T, "w_proj_b": gbc[0].T, "w_proj_c": gbc[1].T}
    f = "ffn1" if part == "f1" else "ffn2"
    return {f + "_w_gate": sums[0][0].T, f + "_w_up": sums[0][1].T, f + "_w_down": sums[0][2]}


def _layer_small(w, l):
    r = lambda a, shape: a.reshape(shape)
    return {"ffn1_norm": r(w["ffn1_norm"][l], (1, D)), "mix_norm": r(w["mix_norm"][l], (1, D)),
            "b_gate": r(w["b_gate"][l], (1, 3 * D)), "b_ln_g": r(w["b_ln_g"][l], (1, BW)), "b_ln_b": r(w["b_ln_b"][l], (1, BW)),
            "b_w_s": w["b_w_s"][l], "b_b_s": r(w["b_b_s"][l], (4, BLK, 1)), "c_w": w["c_w"][l],
            "c_scale": r(w["c_scale"][l], (4, 1, BLK)), "ffn2_norm": r(w["ffn2_norm"][l], (1, D))}


def _pack_small(per_layer, final):
    flat = [per_layer[l][k].reshape(-1) for l in range(DEPTH) for k in SMALL_LAYER] + [final.reshape(-1)]
    return jnp.concatenate(flat).reshape(-1, 128)


def _unpack_small(packed, shapes):
    flat = packed.reshape(-1)
    per = {k: [] for k in SMALL_LAYER}
    off = 0
    for _ in range(DEPTH):
        for k in SMALL_LAYER:
            size = math.prod(shapes[k][1:])
            per[k].append(flat[off:off + size].reshape(shapes[k][1:]))
            off += size
    out = {k: jnp.stack(v) for k, v in per.items()}
    out["final_norm"] = flat[off:off + D]
    return out


def kernel(x, ffn1_norm, ffn1_w_gate, ffn1_w_up, ffn1_w_down, mix_norm, w_in, b_gate, b_ln_g, b_ln_b, b_w_s, b_b_s, c_w, c_scale, w_proj_a, w_proj_b, w_proj_c, w_out, ffn2_norm, ffn2_w_gate, ffn2_w_up, ffn2_w_down, final_norm, loss_target, m_ffn1_norm, m_ffn1_w_gate, m_ffn1_w_up, m_ffn1_w_down, m_mix_norm, m_w_in, m_b_gate, m_b_ln_g, m_b_ln_b, m_b_w_s, m_b_b_s, m_c_w, m_c_scale, m_w_proj_a, m_w_proj_b, m_w_proj_c, m_w_out, m_ffn2_norm, m_ffn2_w_gate, m_ffn2_w_up, m_ffn2_w_down, m_final_norm, v_ffn1_norm, v_ffn1_w_gate, v_ffn1_w_up, v_ffn1_w_down, v_mix_norm, v_w_in, v_b_gate, v_b_ln_g, v_b_ln_b, v_b_w_s, v_b_b_s, v_c_w, v_c_scale, v_w_proj_a, v_w_proj_b, v_w_proj_c, v_w_out, v_ffn2_norm, v_ffn2_w_gate, v_ffn2_w_up, v_ffn2_w_down, v_final_norm):
    w = dict(ffn1_norm=ffn1_norm, ffn1_w_gate=ffn1_w_gate, ffn1_w_up=ffn1_w_up, ffn1_w_down=ffn1_w_down, mix_norm=mix_norm,
             w_in=w_in, b_gate=b_gate, b_ln_g=b_ln_g, b_ln_b=b_ln_b, b_w_s=b_w_s, b_b_s=b_b_s, c_w=c_w, c_scale=c_scale,
             w_proj_a=w_proj_a, w_proj_b=w_proj_b, w_proj_c=w_proj_c, w_out=w_out, ffn2_norm=ffn2_norm, ffn2_w_gate=ffn2_w_gate,
             ffn2_w_up=ffn2_w_up, ffn2_w_down=ffn2_w_down, final_norm=final_norm)
    mom = dict(ffn1_norm=m_ffn1_norm, ffn1_w_gate=m_ffn1_w_gate, ffn1_w_up=m_ffn1_w_up, ffn1_w_down=m_ffn1_w_down,
               mix_norm=m_mix_norm, w_in=m_w_in, b_gate=m_b_gate, b_ln_g=m_b_ln_g, b_ln_b=m_b_ln_b, b_w_s=m_b_w_s, b_b_s=m_b_b_s,
               c_w=m_c_w, c_scale=m_c_scale, w_proj_a=m_w_proj_a, w_proj_b=m_w_proj_b, w_proj_c=m_w_proj_c, w_out=m_w_out,
               ffn2_norm=m_ffn2_norm, ffn2_w_gate=m_ffn2_w_gate, ffn2_w_up=m_ffn2_w_up, ffn2_w_down=m_ffn2_w_down,
               final_norm=m_final_norm)
    var = dict(ffn1_norm=v_ffn1_norm, ffn1_w_gate=v_ffn1_w_gate, ffn1_w_up=v_ffn1_w_up, ffn1_w_down=v_ffn1_w_down,
               mix_norm=v_mix_norm, w_in=v_w_in, b_gate=v_b_gate, b_ln_g=v_b_ln_g, b_ln_b=v_b_ln_b, b_w_s=v_b_w_s, b_b_s=v_b_b_s,
               c_w=v_c_w, c_scale=v_c_scale, w_proj_a=v_w_proj_a, w_proj_b=v_w_proj_b, w_proj_c=v_w_proj_c, w_out=v_w_out,
               ffn2_norm=v_ffn2_norm, ffn2_w_gate=v_ffn2_w_gate, ffn2_w_up=v_ffn2_w_up, ffn2_w_down=v_ffn2_w_down,
               final_norm=v_final_norm)

    t_dim = x.shape[1]
    xs = x.reshape(t_dim, D)
    tgt = loss_target.reshape(t_dim, D)

    gathers = {}
    token = jnp.zeros((8, 128), F32)
    for l in range(DEPTH):
        for part in PARTS:
            gathers[l, part] = _exchange_start("gather", _part_shards(w, l, part), token, f"gather_start_{part}_l{l}")
            token = gathers[l, part][-1]

    def weights_of(l, part, after):
        behind = token if (l, part) == (0, "f1") else after
        return _part_views(part, _exchange_wait("gather", gathers[l, part], behind, f"gather_wait_{part}_l{l}"))

    small = [_layer_small(w, l) for l in range(DEPTH)]
    loss_part, dcur, dfinal, saved, views = _forward(xs, tgt, weights_of, small, w["final_norm"].reshape(1, D))
    loss = lax.psum(loss_part[0, 0], ("x", "y", "c"))

    small_grads = [None] * DEPTH
    scatters = []
    for l in reversed(range(DEPTH)):
        def emit(part, grads, l=l):
            scatters.append((l, part, _exchange_start("scatter", grads, grads[0], f"scatter_start_{part}_l{l}")))
        dcur, small_grads[l] = _backward_layer(dcur, saved[l], small[l], views[l], l, emit)
    grad_x = dcur.reshape(x.shape)

    big_grads = {}
    behind = dcur
    for l, part, started in scatters:
        landed = _exchange_wait("scatter", started, behind, f"scatter_wait_{part}_l{l}")
        sums = [_sum_slots(b, f"sum_grads_{part}_l{l}_{i}", 0) for i, b in enumerate(landed)]
        behind = sums[0]
        for k, g in _part_grads(part, sums).items():
            big_grads[l, k] = g

    packed = _pack_small(small_grads, dfinal)
    gathered = _allgather([packed[None]], "gather_small_grads")[0]
    g_small = _sum_slots(gathered, "sum_small_grads", 1)[0]

    grads, delta, new_m, new_v = {}, {}, {}, {}
    shapes = {k: w[k].shape for k in SMALL_ALL}
    w_small = _pack_small([{k: w[k][l] for k in SMALL_LAYER} for l in range(DEPTH)], w["final_norm"])
    m_small = _pack_small([{k: mom[k][l] for k in SMALL_LAYER} for l in range(DEPTH)], mom["final_norm"])
    v_small = _pack_small([{k: var[k][l] for k in SMALL_LAYER} for l in range(DEPTH)], var["final_norm"])
    d_s, nm_s, nv_s = _adamw(w_small, g_small, m_small, v_small, "adamw_small")
    for dst, src in ((grads, g_small), (delta, d_s), (new_m, nm_s), (new_v, nv_s)):
        dst.update(_unpack_small(src, shapes))
    for k in BIG:
        g = jnp.stack([big_grads[l, k] for l in range(DEPTH)])
        shp = g.shape
        flat = lambda a: a.reshape(shp[0] * shp[1], shp[2])
        d_k, nm_k, nv_k = _adamw(flat(w[k]), flat(g), flat(mom[k]), flat(var[k]), f"adamw_{k}")
        grads[k], delta[k], new_m[k], new_v[k] = g, d_k.reshape(shp), nm_k.reshape(shp), nv_k.reshape(shp)

    return (loss, grad_x, *[grads[k] for k in ALL_WEIGHTS], *[delta[k] for k in ALL_WEIGHTS],
            *[new_m[k] for k in ALL_WEIGHTS], *[new_v[k] for k in ALL_WEIGHTS])
```

```python
import math

import jax
import jax.numpy as jnp
from jax import lax
from jax.experimental import pallas as pl
from jax.experimental.pallas import tpu as pltpu

F32 = jnp.float32
BF16 = jnp.bfloat16
MESH = pl.DeviceIdType.MESH

NDEV = 8
DEPTH = 2
D = 1024
FF = 2816
FFS = FF // NDEV
INW = 6912
INS = INW // NDEV
DS = D // NDEV
BLK = 128
NH = 4
HD = 64
GW = NH * HD
DILS = (1, 4, 16)
QK_SCALE = 1.0 / math.sqrt(HD)
ZB = INW // GW
Q_B, K_B, V_B = 0, 3, 6
UV_B = 9
C_B128 = 26
G_B = 15
BW = 512
EPS = 1e-6
NEG = -1e30
VMEM_CAP = 60 * 1024 * 1024

ADAM_LR, ADAM_B1, ADAM_B2, ADAM_EPS, ADAM_WD, ADAM_STEP = 0.001, 0.9, 0.999, 1e-08, 0.01, 10

NT = (((1,), (1,)), ((), ()))
TN = (((0,), (0,)), ((), ()))


def _bs(shape, imap):
    return pl.BlockSpec(shape, imap)


def _pc(body, *, name, grid, in_specs, out_specs, out_shape, scratch=(), alias=None, vmem=None):
    params = {}
    if vmem is not None:
        params["vmem_limit_bytes"] = min(int(vmem), VMEM_CAP)
    pin = lambda s: pltpu.HBM(s.shape, s.dtype) if math.prod(s.shape) * jnp.dtype(s.dtype).itemsize >= (1 << 20) else s
    out_shape = [pin(s) for s in out_shape] if isinstance(out_shape, (list, tuple)) else pin(out_shape)
    return pl.pallas_call(
        body, name=name, grid=grid, in_specs=in_specs, out_specs=out_specs, out_shape=out_shape,
        scratch_shapes=list(scratch), input_output_aliases=alias or {},
        compiler_params=pltpu.CompilerParams(**params), interpret=False)


def _sds(shape, dtype):
    return jax.ShapeDtypeStruct(shape, dtype)


def _row_tile(rows, align, cap):
    best = None
    for cand in range(align, min(rows, cap) + 1, align):
        if rows % cand == 0:
            best = cand
    assert best is not None, (rows, align, cap)
    return best


def _mm(a, b, *, mode, tm, tn, tk, n, out_dtype, name, b_off=0, scale=None, res=None, after=None):
    m, k_dim = a.shape
    nk = k_dim // tk
    assert m % tm == 0 and n % tn == 0 and k_dim % tk == 0
    grid = (n // tn, m // tm, nk)
    in_specs = [_bs((tm, tk), lambda j, i, k: (i, k))]
    if mode == "nn":
        in_specs.append(_bs((tk, tn), lambda j, i, k: (b_off + k, j)))
    else:
        in_specs.append(_bs((tn, tk), lambda j, i, k: (b_off + j, k)))
    args = [a, b]
    if res is not None:
        in_specs.append(_bs((tm, tn), lambda j, i, k: (i, j)))
        args.append(res)
    if after is not None:
        in_specs.append(pl.BlockSpec(memory_space=pl.ANY))
        args.append(after)
    n_in = len(args)

    def body(*refs):
        a_ref, b_ref = refs[0], refs[1]
        r_ref = refs[2] if res is not None else None
        o_ref = refs[n_in]
        av = a_ref[...].astype(BF16)
        bv = b_ref[...].astype(BF16)
        if mode == "nn":
            p = jnp.dot(av, bv, preferred_element_type=F32)
        else:
            p = lax.dot_general(av, bv, NT, preferred_element_type=F32)

        def fin(v):
            if scale is not None:
                v = v * scale
            if r_ref is not None:
                v = r_ref[...] + v
            o_ref[...] = v.astype(out_dtype)

        if nk == 1:
            fin(p)
        else:
            acc_ref = refs[-1]
            kk = pl.program_id(2)

            @pl.when(kk == 0)
            def _():
                acc_ref[...] = p

            @pl.when(kk > 0)
            def _():
                acc_ref[...] += p

            @pl.when(kk == nk - 1)
            def _():
                fin(acc_ref[...])

    scratch = [pltpu.VMEM((tm, tn), F32)] if nk > 1 else []
    est = 2 * (tm * tk * a.dtype.itemsize + tk * tn * b.dtype.itemsize + tm * tn * jnp.dtype(out_dtype).itemsize)
    est += 3 * tm * tn * 4 + (2 * tm * tn * 4 if res is not None else 0) + (tm * tk + tk * tn) * 2
    return _pc(body, name=name, grid=grid, in_specs=in_specs, out_specs=_bs((tm, tn), lambda j, i, k: (i, j)),
               out_shape=_sds((m, n), out_dtype), scratch=scratch, vmem=est + (8 << 20))(*args)


def _wgrad(a, b, buf, *, piece0, rows, tm, tn, tk, name, scale=None):
    t_dim, m = a.shape
    n = b.shape[1]
    db = tm // rows
    tpp = NDEV // db
    nk = t_dim // tk
    assert tm % rows == 0 and NDEV % db == 0 and m % tm == 0 and n % tn == 0 and t_dim % tk == 0
    grid = (m // tm, n // tn, nk)

    def body(a_ref, b_ref, buf_ref, o_ref, acc_ref):
        del buf_ref
        kk = pl.program_id(2)
        p = lax.dot_general(a_ref[...].astype(BF16), b_ref[...].astype(BF16), TN, preferred_element_type=F32)

        @pl.when(kk == 0)
        def _():
            acc_ref[...] = p

        @pl.when(kk > 0)
        def _():
            acc_ref[...] += p

        @pl.when(kk == nk - 1)
        def _():
            for d in range(db):
                v = acc_ref[d * rows:(d + 1) * rows, :]
                if scale is not None:
                    v = v * scale
                o_ref[0, d] = v.astype(BF16)

    est = 2 * (tk * tm * a.dtype.itemsize + tk * tn * b.dtype.itemsize + tm * tn * 2) + 3 * tm * tn * 4 + tk * tm * 4
    return _pc(
        body, name=name, grid=grid,
        in_specs=[_bs((tk, tm), lambda i, j, k: (k, i)), _bs((tk, tn), lambda i, j, k: (k, j)),
                  pl.BlockSpec(memory_space=pl.ANY)],
        out_specs=_bs((1, db, rows, tn), lambda i, j, k: (piece0 + i // tpp, i % tpp, 0, j)),
        out_shape=_sds(buf.shape, buf.dtype), scratch=[pltpu.VMEM((tm, tn), F32)], alias={2: 0},
        vmem=est + (8 << 20))(a, b, buf)


def _rms_fwd(x, gain, name, tm=512):
    t_dim = x.shape[0]

    def body(x_ref, g_ref, o_ref):
        xv = x_ref[...]
        r = lax.rsqrt(jnp.mean(xv * xv, axis=-1, keepdims=True) + EPS)
        o_ref[...] = (xv * r * g_ref[...]).astype(BF16)

    return _pc(body, name=name, grid=(t_dim // tm,),
               in_specs=[_bs((tm, D), lambda i: (i, 0)), _bs((1, D), lambda i: (0, 0))],
               out_specs=_bs((tm, D), lambda i: (i, 0)), out_shape=_sds((t_dim, D), BF16))(x, gain)


def _rms_bwd(x, gain, dh, dres, name, tm=512):
    t_dim = x.shape[0]

    def body(x_ref, g_ref, dh_ref, dr_ref, dx_ref, dg_ref):
        xv = x_ref[...]
        r = lax.rsqrt(jnp.mean(xv * xv, axis=-1, keepdims=True) + EPS)
        dh_v = dh_ref[...].astype(F32)
        dyg = dh_v * g_ref[...]
        mq = jnp.mean(dyg * xv, axis=-1, keepdims=True)
        dx_ref[...] = dr_ref[...] + r * (dyg - xv * (r * r * mq))

        @pl.when(pl.program_id(0) == 0)
        def _():
            dg_ref[...] = jnp.zeros_like(dg_ref)

        dg_ref[...] += jnp.sum(dh_v * (xv * r), axis=0, keepdims=True)

    return _pc(body, name=name, grid=(t_dim // tm,),
               in_specs=[_bs((tm, D), lambda i: (i, 0)), _bs((1, D), lambda i: (0, 0)),
                         _bs((tm, D), lambda i: (i, 0)), _bs((tm, D), lambda i: (i, 0))],
               out_specs=[_bs((tm, D), lambda i: (i, 0)), _bs((1, D), lambda i: (0, 0))],
               out_shape=[_sds((t_dim, D), F32), _sds((1, D), F32)])(x, gain, dh, dres)


def _sigmoid(v):
    return 1.0 / (1.0 + jnp.exp(-v))


def _swiglu_fwd(gu, name, tm=256):
    t_dim = gu.shape[0]

    def body(gu_ref, o_ref):
        g = gu_ref[:, :FF].astype(F32)
        u = gu_ref[:, FF:].astype(F32)
        o_ref[...] = (g * _sigmoid(g) * u).astype(BF16)

    return _pc(body, name=name, grid=(t_dim // tm,), in_specs=[_bs((tm, 2 * FF), lambda i: (i, 0))],
               out_specs=_bs((tm, FF), lambda i: (i, 0)), out_shape=_sds((t_dim, FF), BF16))(gu)


def _swiglu_bwd(gu, da, name, tm=256):
    t_dim = gu.shape[0]

    def body(gu_ref, da_ref, o_ref):
        g = gu_ref[:, :FF].astype(F32)
        u = gu_ref[:, FF:].astype(F32)
        dav = da_ref[...].astype(F32)
        s = _sigmoid(g)
        o_ref[:, :FF] = (dav * u * (s * (1.0 + g * (1.0 - s)))).astype(BF16)
        o_ref[:, FF:] = (dav * (g * s)).astype(BF16)

    return _pc(body, name=name, grid=(t_dim // tm,),
               in_specs=[_bs((tm, 2 * FF), lambda i: (i, 0)), _bs((tm, FF), lambda i: (i, 0))],
               out_specs=_bs((tm, 2 * FF), lambda i: (i, 0)), out_shape=_sds((t_dim, 2 * FF), BF16))(gu, da)


def _final_loss(x, gain, tgt, name, tm=512):
    t_dim = x.shape[0]

    def body(x_ref, g_ref, t_ref, loss_ref, dx_ref, dg_ref):
        xv = x_ref[...]
        gv = g_ref[...]
        r = lax.rsqrt(jnp.mean(xv * xv, axis=-1, keepdims=True) + EPS)
        xn = xv * r
        err = xn * gv - t_ref[...]
        dy = err * (1.0 / D)
        dyg = dy * gv
        mq = jnp.mean(dyg * xv, axis=-1, keepdims=True)
        dx_ref[...] = r * (dyg - xv * (r * r * mq))

        @pl.when(pl.program_id(0) == 0)
        def _():
            dg_ref[...] = jnp.zeros_like(dg_ref)
            loss_ref[...] = jnp.zeros_like(loss_ref)

        dg_ref[...] += jnp.sum(dy * xn, axis=0, keepdims=True)
        part = 0.5 * jnp.sum(jnp.mean(err * err, axis=-1, keepdims=True), axis=0, keepdims=True)
        loss_ref[...] += jnp.broadcast_to(part, loss_ref.shape)

    return _pc(body, name=name, grid=(t_dim // tm,),
               in_specs=[_bs((tm, D), lambda i: (i, 0)), _bs((1, D), lambda i: (0, 0)), _bs((tm, D), lambda i: (i, 0))],
               out_specs=[_bs((8, 128), lambda i: (0, 0)), _bs((tm, D), lambda i: (i, 0)), _bs((1, D), lambda i: (0, 0))],
               out_shape=[_sds((8, 128), F32), _sds((t_dim, D), F32), _sds((1, D), F32)])(x, gain, tgt)


def _band_mask(n):
    ri = lax.broadcasted_iota(jnp.int32, (BLK, 2 * BLK), 0)
    cj = lax.broadcasted_iota(jnp.int32, (BLK, 2 * BLK), 1)
    dist = ri + BLK - cj
    return (dist >= 0) & (dist <= BLK) & ((cj >= BLK) | (n > 0))


def _halves(unit, imap_rows, col):
    return [_bs((unit, 128), (lambda u, j=j: (imap_rows(u), 2 * col + j))) for j in range(2)]


def _get_rows(pair, r, dil):
    return jnp.concatenate([h[pl.ds(r, BLK, stride=dil), :] for h in pair], axis=1)


def _put_rows(stage, r, dil, val):
    for j in range(2):
        stage[j, pl.ds(r, BLK, stride=dil), :] = val[:, 128 * j:128 * (j + 1)]


def _unstage(stage, out_ref):
    out_ref[:, :128] = stage[0]
    out_ref[:, 128:] = stage[1]


def _attn_fwd(z, acc, ml, *, gi, first, final, name):
    dil = DILS[gi]
    t_dim = z.shape[0]
    unit = BLK * dil
    nu = t_dim // unit
    own = lambda u: u
    prev = lambda u: jnp.maximum(u - 1, 0)
    in_specs = (_halves(unit, own, Q_B + gi) + _halves(unit, own, K_B + gi) + _halves(unit, prev, K_B + gi)
                + _halves(unit, own, V_B + gi) + _halves(unit, prev, V_B + gi))
    args = [z] * 10
    st_ml = _bs((unit, 128), lambda u: (u, 0))
    if not first:
        in_specs += _halves(unit, own, 0) + [st_ml]
        args += [acc, acc, ml]

    def body(*refs):
        q2, ko2, kp2, vo2, vp2 = (refs[2 * i:2 * i + 2] for i in range(5))
        o_acc_ref, o_ml_ref, stage = refs[-3], refs[-2], refs[-1]
        valid = _band_mask(pl.program_id(0))
        lane = lax.broadcasted_iota(jnp.int32, (BLK, GW), 1)
        lane_s = lax.broadcasted_iota(jnp.int32, (BLK, 128), 1)

        def band(r, carry):
            q = _get_rows(q2, r, dil) * QK_SCALE
            kb = jnp.concatenate([_get_rows(kp2, r, dil), _get_rows(ko2, r, dil)], axis=0).astype(BF16)
            vb = jnp.concatenate([_get_rows(vp2, r, dil), _get_rows(vo2, r, dil)], axis=0).astype(BF16)
            if first:
                acc_v = jnp.zeros((BLK, GW), F32)
                ml_v = jnp.where(lane_s < 64, NEG, 0.0).astype(F32)
            else:
                acc_v = _get_rows(refs[10:12], r, dil)
                ml_v = refs[12][pl.ds(r, BLK, stride=dil), :]
            acc_o, ml_o = acc_v, ml_v
            l_b = jnp.ones((BLK, GW), F32)
            for h in range(NH):
                hm = (lane >= h * HD) & (lane < (h + 1) * HD)
                qh = jnp.where(hm, q, 0.0).astype(BF16)
                s = lax.dot_general(qh, kb, NT, preferred_element_type=F32)
                s = jnp.where(valid, s, NEG)
                m_old = ml_v[:, 16 * h:16 * h + 1]
                l_old = ml_v[:, 64 + 16 * h:64 + 16 * h + 1]
                m_new = jnp.maximum(m_old, jnp.max(s, axis=-1, keepdims=True))
                alpha = jnp.exp(m_old - m_new)
                p = jnp.exp(s - m_new)
                l_new = alpha * l_old + jnp.sum(p, axis=-1, keepdims=True)
                pv = jnp.dot(p.astype(BF16), vb, preferred_element_type=F32)
                acc_o = jnp.where(hm, acc_v * alpha + pv, acc_o)
                l_b = jnp.where(hm, l_new, l_b)
                if final:
                    ml_o = jnp.where((lane_s >= 16 * h) & (lane_s < 16 * h + 16), m_new + jnp.log(l_new), ml_o)
                else:
                    ml_o = jnp.where((lane_s >= 16 * h) & (lane_s < 16 * h + 16), m_new, ml_o)
                    ml_o = jnp.where((lane_s >= 64 + 16 * h) & (lane_s < 64 + 16 * h + 16), l_new, ml_o)
            _put_rows(stage, r, dil, acc_o / l_b if final else acc_o)
            o_ml_ref[pl.ds(r, BLK, stride=dil), :] = ml_o
            return carry

        lax.fori_loop(0, dil, band, 0)
        _unstage(stage, o_acc_ref)

    blocks = (len(in_specs) + 6) * unit * 128 * 4
    return _pc(body, name=name, grid=(nu,), in_specs=in_specs, out_specs=[_bs((unit, GW), lambda u: (u, 0)), st_ml],
               out_shape=[_sds((t_dim, GW), F32), _sds((t_dim, 128), F32)],
               scratch=[pltpu.VMEM((2, unit, 128), F32)], vmem=2 * blocks + (12 << 20))(*args)


def _attn_bwd(z, dy, y, lse, *, gi, name):
    dil = DILS[gi]
    t_dim = z.shape[0]
    unit = BLK * dil
    nu = t_dim // unit

    own = lambda u: jnp.minimum(u, nu - 1)
    prev = lambda u: jnp.maximum(jnp.minimum(u, nu - 1) - 1, 0)
    in_specs = (_halves(unit, own, Q_B + gi) + _halves(unit, own, K_B + gi) + _halves(unit, prev, K_B + gi)
                + _halves(unit, own, V_B + gi) + _halves(unit, prev, V_B + gi)
                + _halves(unit, own, 0) + _halves(unit, own, 0) + [_bs((unit, 128), lambda u: (own(u), 0))])
    tok = _bs((unit, GW), lambda u: (own(u), 0))
    kv_out = _bs((unit, GW), lambda u: (jnp.maximum(u - 1, 0), 0))

    def body(*refs):
        q2, ko2, kp2, vo2, vp2, dy2, y2 = (refs[2 * i:2 * i + 2] for i in range(7))
        lse_ref = refs[14]
        dq_ref, dk_ref, dv_ref, dq_st, dk_st, dv_st, dkc, dvc = refs[15:]
        u = pl.program_id(0)

        @pl.when(u == 0)
        def _():
            dkc[...] = jnp.zeros_like(dkc)
            dvc[...] = jnp.zeros_like(dvc)

        @pl.when(u < nu)
        def _():
            valid = _band_mask(u)
            lane = lax.broadcasted_iota(jnp.int32, (BLK, GW), 1)

            def band(r, carry):
                q = _get_rows(q2, r, dil) * QK_SCALE
                kb = jnp.concatenate([_get_rows(kp2, r, dil), _get_rows(ko2, r, dil)], axis=0).astype(BF16)
                vb = jnp.concatenate([_get_rows(vp2, r, dil), _get_rows(vo2, r, dil)], axis=0).astype(BF16)
                dyv = _get_rows(dy2, r, dil)
                yv = _get_rows(y2, r, dil)
                lse_v = lse_ref[pl.ds(r, BLK, stride=dil), :]
                dq = jnp.zeros((BLK, GW), F32)
                dkb = jnp.zeros((2 * BLK, GW), F32)
                dvb = jnp.zeros((2 * BLK, GW), F32)
                for h in range(NH):
                    hm = (lane >= h * HD) & (lane < (h + 1) * HD)
                    qh = jnp.where(hm, q, 0.0).astype(BF16)
                    s = lax.dot_general(qh, kb, NT, preferred_element_type=F32)
                    p = jnp.where(valid, jnp.exp(s - lse_v[:, 16 * h:16 * h + 1]), 0.0)
                    dyh = jnp.where(hm, dyv, 0.0)
                    delta = jnp.sum(dyh * yv, axis=-1, keepdims=True)
                    dyh16 = dyh.astype(BF16)
                    dp = lax.dot_general(dyh16, vb, NT, preferred_element_type=F32)
                    ds = (p * (dp - delta)).astype(BF16)
                    dq = dq + jnp.where(hm, jnp.dot(ds, kb, preferred_element_type=F32), 0.0)
                    dkb = dkb + lax.dot_general(ds, qh, TN, preferred_element_type=F32)
                    dvb = dvb + lax.dot_general(p.astype(BF16), dyh16, TN, preferred_element_type=F32)
                _put_rows(dq_st, r, dil, dq * QK_SCALE)
                _put_rows(dk_st, r, dil, _get_rows((dkc.at[0], dkc.at[1]), r, dil) + dkb[:BLK])
                _put_rows(dv_st, r, dil, _get_rows((dvc.at[0], dvc.at[1]), r, dil) + dvb[:BLK])
                _put_rows(dkc, r, dil, dkb[BLK:])
                _put_rows(dvc, r, dil, dvb[BLK:])
                return carry

            lax.fori_loop(0, dil, band, 0)
            _unstage(dq_st, dq_ref)

            @pl.when(u > 0)
            def _():
                _unstage(dk_st, dk_ref)
                _unstage(dv_st, dv_ref)

        @pl.when(u == nu)
        def _():
            _unstage(dkc, dk_ref)
            _unstage(dvc, dv_ref)

    blocks = (15 + 6 + 5) * unit * 128 * 4 * 2
    return _pc(body, name=name, grid=(nu + 1,), in_specs=in_specs, out_specs=[tok, kv_out, kv_out],
               out_shape=[_sds((t_dim, GW), F32)] * 3, scratch=[pltpu.VMEM((2, unit, 128), F32)] * 5,
               vmem=blocks + (12 << 20))(*([z] * 10), dy, dy, y, y, lse)


_INV_SQRT2 = 1.0 / math.sqrt(2.0)
_INV_SQRT2PI = 1.0 / math.sqrt(2.0 * math.pi)


def _gelu(v):
    return 0.5 * v * (1.0 + lax.erf(v * _INV_SQRT2))


def _gelu_grad(v):
    return 0.5 * (1.0 + lax.erf(v * _INV_SQRT2)) + v * (_INV_SQRT2PI * jnp.exp(-0.5 * v * v))


def _gmlp_core(z_refs, lng, lnb, ws_ref, bs_ref):
    u_pre = jnp.concatenate([z_refs[0][...], z_refs[1][...]], axis=1)
    v_pre = jnp.concatenate([z_refs[2][...], z_refs[3][...]], axis=1)
    u = _gelu(u_pre)
    v = _gelu(v_pre)
    xc = v - jnp.mean(v, axis=-1, keepdims=True)
    rstd = lax.rsqrt(jnp.mean(xc * xc, axis=-1, keepdims=True) + EPS)
    xhat = xc * rstd
    vn = xhat * lng + lnb
    ti = lax.broadcasted_iota(jnp.int32, (BLK, BLK), 0)
    si = lax.broadcasted_iota(jnp.int32, (BLK, BLK), 1)
    causal = ti >= si
    vn16 = vn.astype(BF16)
    ws = [jnp.where(causal, ws_ref[g], 0.0).astype(BF16) for g in range(4)]
    mixed = jnp.concatenate(
        [jnp.dot(ws[g], vn16[:, g * BLK:(g + 1) * BLK], preferred_element_type=F32) + bs_ref[g] for g in range(4)], axis=1)
    return u_pre, v_pre, u, xhat, rstd, vn16, ws, causal, mixed


def _gmlp_specs():
    z_specs = [_bs((BLK, GW), (lambda i, c=c: (i, UV_B + c))) for c in range(4)]
    par_specs = [_bs((1, BW), lambda i: (0, 0)), _bs((1, BW), lambda i: (0, 0)),
                 _bs((4, BLK, BLK), lambda i: (0, 0, 0)), _bs((4, BLK, 1), lambda i: (0, 0, 0))]
    return z_specs, par_specs


def _gmlp_fwd(z, lng, lnb, ws, bs, name):
    t_dim = z.shape[0]
    z_specs, par_specs = _gmlp_specs()

    def body(z0, z1, z2, z3, lng_ref, lnb_ref, ws_ref, bs_ref, o_ref):
        core = _gmlp_core((z0, z1, z2, z3), lng_ref[...], lnb_ref[...], ws_ref, bs_ref)
        o_ref[...] = (core[2] * core[8]).astype(BF16)

    return _pc(body, name=name, grid=(t_dim // BLK,), in_specs=z_specs + par_specs,
               out_specs=_bs((BLK, BW), lambda i: (i, 0)), out_shape=_sds((t_dim, BW), BF16))(z, z, z, z, lng, lnb, ws, bs)


def _gmlp_bwd(z, dy, lng, lnb, ws, bs, name):
    t_dim = z.shape[0]
    z_specs, par_specs = _gmlp_specs()

    def body(z0, z1, z2, z3, lng_ref, lnb_ref, ws_ref, bs_ref, dy_ref, duv_ref, dws_ref, dbs_ref, dlng_ref, dlnb_ref):
        lng_v = lng_ref[...]
        u_pre, v_pre, u, xhat, rstd, vn16, wsm, causal, mixed = _gmlp_core((z0, z1, z2, z3), lng_v, lnb_ref[...], ws_ref, bs_ref)
        dyv = dy_ref[...].astype(F32)
        du = dyv * mixed
        dmixed = dyv * u

        @pl.when(pl.program_id(0) == 0)
        def _():
            dws_ref[...] = jnp.zeros_like(dws_ref)
            dbs_ref[...] = jnp.zeros_like(dbs_ref)
            dlng_ref[...] = jnp.zeros_like(dlng_ref)
            dlnb_ref[...] = jnp.zeros_like(dlnb_ref)

        dvn_parts = []
        for g in range(4):
            dm = dmixed[:, g * BLK:(g + 1) * BLK]
            dm16 = dm.astype(BF16)
            dw = lax.dot_general(dm16, vn16[:, g * BLK:(g + 1) * BLK], NT, preferred_element_type=F32)
            dws_ref[g] += jnp.where(causal, dw, 0.0)
            dbs_ref[g] += jnp.sum(dm.T, axis=0, keepdims=True)
            dvn_parts.append(lax.dot_general(wsm[g], dm16, TN, preferred_element_type=F32))
        dvn = jnp.concatenate(dvn_parts, axis=1)
        dlng_ref[...] += jnp.sum(dvn * xhat, axis=0, keepdims=True)
        dlnb_ref[...] += jnp.sum(dvn, axis=0, keepdims=True)
        dxh = dvn * lng_v
        dv = rstd * (dxh - jnp.mean(dxh, axis=-1, keepdims=True) - xhat * jnp.mean(dxh * xhat, axis=-1, keepdims=True))
        duv_ref[:, :BW] = (du * _gelu_grad(u_pre)).astype(BF16)
        duv_ref[:, BW:] = (dv * _gelu_grad(v_pre)).astype(BF16)

    return _pc(body, name=name, grid=(t_dim // BLK,),
               in_specs=z_specs + par_specs + [_bs((BLK, BW), lambda i: (i, 0))],
               out_specs=[_bs((BLK, 2 * BW), lambda i: (i, 0)), _bs((4, BLK, BLK), lambda i: (0, 0, 0)),
                          _bs((4, 1, BLK), lambda i: (0, 0, 0)), _bs((1, BW), lambda i: (0, 0)), _bs((1, BW), lambda i: (0, 0))],
               out_shape=[_sds((t_dim, 2 * BW), BF16), _sds((4, BLK, BLK), F32), _sds((4, 1, BLK), F32),
                          _sds((1, BW), F32), _sds((1, BW), F32)])(z, z, z, z, lng, lnb, ws, bs, dy)


def _pool_core(x, gi):
    t_dim = x.shape[0]
    w = jnp.left_shift(2, gi)
    row = lax.broadcasted_iota(jnp.int32, x.shape, 0)
    s = x
    for k in (1, 2, 4, 8):
        sh = jnp.where(row >= k, pltpu.roll(s, k, 0), 0.0)
        s = jnp.where(k < w, s + sh, s)
    cnt = jnp.minimum(row + 1, w).astype(F32)
    return s / cnt - x, cnt, row, w, t_dim


def _pool_fwd(z, cw, cs, name):
    t_dim = z.shape[0]

    def body(z_ref, cw_ref, cs_ref, o_ref):
        pooled = _pool_core(z_ref[...], pl.program_id(0))[0]
        y = jnp.dot(pooled.astype(BF16), cw_ref[0].astype(BF16), preferred_element_type=F32) * cs_ref[0]
        o_ref[...] = y.astype(BF16)

    return _pc(body, name=name, grid=(4,),
               in_specs=[_bs((t_dim, BLK), lambda g: (0, C_B128 + g)), _bs((1, BLK, BLK), lambda g: (g, 0, 0)),
                         _bs((1, 1, BLK), lambda g: (g, 0, 0))],
               out_specs=_bs((t_dim, BLK), lambda g: (0, g)), out_shape=_sds((t_dim, BW), BF16),
               vmem=40 << 20)(z, cw, cs)


def _pool_bwd(z, dy, cw, cs, name):
    t_dim = z.shape[0]

    def body(z_ref, dy_ref, cw_ref, cs_ref, dx_ref, dcw_ref, dcs_ref):
        pooled, cnt, row, w, _ = _pool_core(z_ref[...], pl.program_id(0))
        p16 = pooled.astype(BF16)
        cw16 = cw_ref[0].astype(BF16)
        dyv = dy_ref[...].astype(F32)
        lin = jnp.dot(p16, cw16, preferred_element_type=F32)
        dcs_ref[0] = jnp.sum(dyv * lin, axis=0, keepdims=True)
        dys = (dyv * cs_ref[0]).astype(BF16)
        dcw_ref[0] = lax.dot_general(p16, dys, TN, preferred_element_type=F32)
        dpool = lax.dot_general(dys, cw16, NT, preferred_element_type=F32)
        sb = dpool / cnt
        for k in (1, 2, 4, 8):
            sh = jnp.where(row < t_dim - k, pltpu.roll(sb, t_dim - k, 0), 0.0)
            sb = jnp.where(k < w, sb + sh, sb)
        dx_ref[...] = (sb - dpool).astype(BF16)

    return _pc(body, name=name, grid=(4,),
               in_specs=[_bs((t_dim, BLK), lambda g: (0, C_B128 + g)), _bs((t_dim, BLK), lambda g: (0, g)),
                         _bs((1, BLK, BLK), lambda g: (g, 0, 0)), _bs((1, 1, BLK), lambda g: (g, 0, 0))],
               out_specs=[_bs((t_dim, BLK), lambda g: (0, g)), _bs((1, BLK, BLK), lambda g: (g, 0, 0)),
                          _bs((1, 1, BLK), lambda g: (g, 0, 0))],
               out_shape=[_sds((t_dim, BW), BF16), _sds((4, BLK, BLK), F32), _sds((4, 1, BLK), F32)],
               vmem=48 << 20)(z, dy, cw, cs)


def _gates_specs(tm):
    z_specs = [_bs((tm, GW), (lambda j, i, b=b: (i, G_B + 4 * b + j))) for b in range(3)]
    bg_specs = [_bs((1, GW), (lambda j, i, b=b: (0, 4 * b + j))) for b in range(3)]
    tile = _bs((tm, GW), lambda j, i: (i, j))
    return z_specs, bg_specs, tile


def _gates_fwd(z, bg, pa, pb, pc, name, tm=512):
    t_dim = z.shape[0]
    z_specs, bg_specs, tile = _gates_specs(tm)

    def body(za, zb, zc, ba, bb, bc, pa_ref, pb_ref, pc_ref, o_ref):
        o_ref[...] = (_sigmoid(za[...] + ba[...]) * pa_ref[...] + _sigmoid(zb[...] + bb[...]) * pb_ref[...]
                      + _sigmoid(zc[...] + bc[...]) * pc_ref[...]).astype(BF16)

    return _pc(body, name=name, grid=(D // GW, t_dim // tm), in_specs=z_specs + bg_specs + [tile] * 3,
               out_specs=tile, out_shape=_sds((t_dim, D), BF16))(z, z, z, bg, bg, bg, pa, pb, pc)


def _gates_bwd(z, bg, pa, pb, pc, dm, name, tm=512):
    t_dim = z.shape[0]
    z_specs, bg_specs, tile = _gates_specs(tm)
    bias_out = _bs((1, GW), lambda j, i: (0, j))

    def body(za, zb, zc, ba, bb, bc, pa_ref, pb_ref, pc_ref, dm_ref, dpa, dpb, dpc, dza, dzb, dzc, dba, dbb, dbc):
        dmv = dm_ref[...]
        first = pl.program_id(1) == 0
        for z_ref, b_ref, p_ref, dp_ref, dz_ref, db_ref in ((za, ba, pa_ref, dpa, dza, dba), (zb, bb, pb_ref, dpb, dzb, dbb),
                                                            (zc, bc, pc_ref, dpc, dzc, dbc)):
            gate = _sigmoid(z_ref[...] + b_ref[...])
            dp_ref[...] = (gate * dmv).astype(BF16)
            dz = dmv * p_ref[...] * (gate * (1.0 - gate))
            dz_ref[...] = dz.astype(BF16)

            @pl.when(first)
            def _():
                db_ref[...] = jnp.zeros_like(db_ref)

            db_ref[...] += jnp.sum(dz, axis=0, keepdims=True)

    return _pc(body, name=name, grid=(D // GW, t_dim // tm), in_specs=z_specs + bg_specs + [tile] * 4,
               out_specs=[tile] * 6 + [bias_out] * 3,
               out_shape=[_sds((t_dim, D), BF16)] * 6 + [_sds((1, D), F32)] * 3)(z, z, z, bg, bg, bg, pa, pb, pc, dm)


def _coords():
    return lax.axis_index("x"), lax.axis_index("y"), lax.axis_index("c")


def _peer(k, x, y, c):
    px = 1 - x if k & 4 else x
    py = 1 - y if k & 2 else y
    pc = 1 - c if k & 1 else c
    return (px, py, pc), 4 * px + 2 * py + pc


def _exchange_copies(kind, bufs, send, recv):
    x, y, c = _coords()
    me = 4 * x + 2 * y + c
    na = len(bufs) if kind == "gather" else len(bufs) // 2
    out = []
    for k in range(1, NDEV):
        to, peer = _peer(k, x, y, c)
        for a in range(na):
            sems = dict(send_sem=send.at[a * 7 + k - 1], recv_sem=recv.at[a * 7 + k - 1], device_id=to, device_id_type=MESH)
            if kind == "gather":
                mine = pltpu.make_async_remote_copy(src_ref=bufs[a].at[:, me], dst_ref=bufs[a].at[:, me], **sems)
                theirs = pltpu.make_async_remote_copy(src_ref=bufs[a].at[:, me], dst_ref=bufs[a].at[:, peer], **sems)
            else:
                src, land = bufs[a], bufs[na + a]
                mine = pltpu.make_async_remote_copy(src_ref=src.at[:, peer], dst_ref=land.at[me], **sems)
                theirs = pltpu.make_async_remote_copy(src_ref=src.at[:, peer], dst_ref=land.at[peer], **sems)
            out.append((mine, theirs))
    return out


_HBM = pl.BlockSpec(memory_space=pltpu.HBM)
_SEM = pl.BlockSpec(memory_space=pltpu.SEMAPHORE)
_EFFECT = pltpu.SideEffectType.DATAFLOW_SIDE_EFFECTING


def _exchange_start(kind, arrays, after, name):
    na = len(arrays)
    if kind == "scatter":
        arrays = list(arrays) + [lax.empty((NDEV, s.shape[0]) + s.shape[2:], s.dtype) for s in arrays]
    nb = len(arrays)

    def body(*refs):
        bufs = refs[:nb]
        send, recv = refs[nb + 1], refs[nb + 2]
        token = refs[-1]
        for mine, _ in _exchange_copies(kind, bufs, send, recv):
            mine.start()
        token[...] = jnp.zeros_like(token)

    arrays = [pltpu.with_memory_space_constraint(b, pltpu.HBM) for b in arrays]
    outs = pl.pallas_call(
        body, name=name,
        out_shape=(pltpu.SemaphoreType.DMA((7 * na,)), pltpu.SemaphoreType.DMA((7 * na,)),
                   *[pltpu.HBM(b.shape, b.dtype) for b in arrays], _sds((8, 128), F32)),
        in_specs=[_HBM] * nb + [pl.BlockSpec(memory_space=pl.ANY)],
        out_specs=(_SEM, _SEM, *([_HBM] * nb), pl.BlockSpec(memory_space=pltpu.VMEM)),
        input_output_aliases={i: 2 + i for i in range(nb)},
        compiler_params=pltpu.CompilerParams(has_side_effects=_EFFECT), interpret=False)(*arrays, after)
    return outs[0], outs[1], list(outs[2:2 + nb]), outs[-1]


def _exchange_wait(kind, started, after, name):
    send, recv, arrays, _ = started
    nb = len(arrays)

    def body(*refs):
        for mine, theirs in _exchange_copies(kind, refs[:nb], refs[nb], refs[nb + 1]):
            mine.wait_send()
            theirs.wait_recv()

    outs = pl.pallas_call(
        body, name=name,
        out_shape=tuple(pltpu.HBM(b.shape, b.dtype) for b in arrays),
        in_specs=[_HBM] * nb + [_SEM, _SEM, pl.BlockSpec(memory_space=pl.ANY)],
        out_specs=tuple([_HBM] * nb),
        input_output_aliases={i: i for i in range(nb)},
        compiler_params=pltpu.CompilerParams(has_side_effects=_EFFECT), interpret=False)(*arrays, send, recv, after)
    return list(outs)


def _sum_landed(landed, own, me, name):
    _, p, rows, cols = landed.shape
    tr = _row_tile(rows, 32 // landed.dtype.itemsize, 512)

    def body(me_ref, b_ref, own_ref, o_ref):
        mine = me_ref[0]
        s = None
        for d in range(NDEV):
            other = jnp.where(mine == d, (d + 1) % NDEV, d)
            v = jnp.where(mine == d, own_ref[0, 0], b_ref[other, 0]).astype(F32)
            s = v if s is None else s + v
        o_ref[0] = s

    grid_spec = pltpu.PrefetchScalarGridSpec(
        num_scalar_prefetch=1, grid=(p, rows // tr),
        in_specs=[_bs((NDEV, 1, tr, cols), lambda i, j, me_ref: (0, i, j, 0)),
                  _bs((1, 1, tr, cols), lambda i, j, me_ref: (i, me_ref[0], j, 0))],
        out_specs=_bs((1, tr, cols), lambda i, j, me_ref: (i, j, 0)))
    return pl.pallas_call(body, name=name, grid_spec=grid_spec, out_shape=_sds((p, rows, cols), F32),
                          interpret=False)(me, landed, own)


def _adam_math(w, g, m, v):
    m2 = ADAM_B1 * m + (1.0 - ADAM_B1) * g
    v2 = ADAM_B2 * v + (1.0 - ADAM_B2) * jnp.square(g)
    m_hat = m2 / (1.0 - ADAM_B1 ** ADAM_STEP)
    v_hat = v2 / (1.0 - ADAM_B2 ** ADAM_STEP)
    return -ADAM_LR * (m_hat / (jnp.sqrt(v_hat) + ADAM_EPS) + ADAM_WD * w), m2, v2


def _adamw(w, g, m, v, name):
    r, c = w.shape
    tr = _row_tile(r, 8, 1024 if c <= 512 else 512)

    def body(w_ref, g_ref, m_ref, v_ref, d_ref, nm_ref, nv_ref):
        d_ref[...], nm_ref[...], nv_ref[...] = _adam_math(w_ref[...], g_ref[...], m_ref[...], v_ref[...])

    spec = _bs((tr, c), lambda i: (i, 0))
    return _pc(body, name=name, grid=(r // tr,), in_specs=[spec] * 4, out_specs=[spec] * 3,
               out_shape=[_sds((r, c), F32)] * 3)(w, g, m, v)


SMALL_LAYER = ("ffn1_norm", "mix_norm", "b_gate", "b_ln_g", "b_ln_b", "b_w_s", "b_b_s", "c_w", "c_scale", "ffn2_norm")
SMALL_ROWS = {"ffn1_norm": 8, "mix_norm": 8, "b_gate": 24, "b_ln_g": 4, "b_ln_b": 4, "b_w_s": 512, "b_b_s": 4, "c_w": 512,
              "c_scale": 4, "ffn2_norm": 8, "final_norm": 8}
LAYER_ROWS = sum(SMALL_ROWS[k] for k in SMALL_LAYER)
PACK_ROWS = DEPTH * LAYER_ROWS + SMALL_ROWS["final_norm"]


def _small_offset(l, name):
    if name == "final_norm":
        return DEPTH * LAYER_ROWS
    return l * LAYER_ROWS + sum(SMALL_ROWS[k] for k in SMALL_LAYER[:SMALL_LAYER.index(name)])


def _pack_small_grads(per_layer, final, me, name):
    flat, where = [], []
    for l in range(DEPTH):
        for k in SMALL_LAYER:
            parts = per_layer[l][k] if k == "b_gate" else [per_layer[l][k]]
            for i, part in enumerate(parts):
                flat.append(part)
                where.append(_small_offset(l, k) + i * (D // 128))
    flat.append(final)
    where.append(_small_offset(0, "final_norm"))

    def body(me_ref, *refs):
        del me_ref
        o_ref = refs[-1]
        for ref, off in zip(refs[:-1], where):
            if len(ref.shape) == 2:
                for j in range(ref.shape[1] // 128):
                    o_ref[0, 0, off + j:off + j + 1, :] = ref[:, 128 * j:128 * (j + 1)]
            else:
                rows = ref.shape[1]
                for g in range(4):
                    o_ref[0, 0, off + g * rows:off + (g + 1) * rows, :] = ref[g]

    grid_spec = pltpu.PrefetchScalarGridSpec(
        num_scalar_prefetch=1, grid=(1,),
        in_specs=[pl.BlockSpec(a.shape, (lambda i, me_ref, nd=a.ndim: (0,) * nd)) for a in flat],
        out_specs=_bs((1, 1, PACK_ROWS, 128), lambda i, me_ref: (0, me_ref[0], 0, 0)))
    return pl.pallas_call(body, name=name, grid_spec=grid_spec, out_shape=_sds((1, NDEV, PACK_ROWS, 128), F32),
                          interpret=False)(me, *flat)


def _adamw_small(gathered, w, mom, var, name):
    names = SMALL_LAYER + ("final_norm",)
    ins = [gathered] + [t[k] for k in names for t in (w, mom, var)]

    def body(*refs):
        land = refs[0]
        outs = refs[len(ins):]

        def rows(off, n):
            s = land[0, 0, off:off + n, :]
            for d in range(1, NDEV):
                s = s + land[0, d, off:off + n, :]
            return s

        def as_row(off, n):
            s = rows(off, n)
            return jnp.concatenate([s[j:j + 1, :] for j in range(n)], axis=1)

        def update(i, idx, g):
            wr, mr, vr = (refs[1 + 3 * i + t] for t in range(3))
            d, m2, v2 = _adam_math(wr[idx], g, mr[idx], vr[idx])
            for o, val in zip(outs[4 * i:4 * i + 4], (g, d, m2, v2)):
                o[idx] = val

        for i, k in enumerate(names):
            n = SMALL_ROWS[k]
            for l in range(1 if k == "final_norm" else DEPTH):
                off = _small_offset(l, k)
                if k in ("b_w_s", "c_w"):
                    for g in range(4):
                        update(i, (l, g), rows(off + g * BLK, BLK))
                elif k in ("b_b_s", "c_scale"):
                    update(i, (l,), rows(off, n))
                else:
                    update(i, (slice(l, l + 1),), as_row(off, n))

    out_shape = [_sds(w[k].shape, F32) for k in names for _ in range(4)]
    outs = pl.pallas_call(body, name=name, out_shape=out_shape,
                          compiler_params=pltpu.CompilerParams(vmem_limit_bytes=48 << 20), interpret=False)(*ins)
    return tuple({k: outs[4 * i + t] for i, k in enumerate(names)} for t in range(4))


PG, PD = 0, 2


def _ffn_fwd(x, gain, wf, tag):
    h = _rms_fwd(x, gain, f"rms_{tag}")
    gu = _mm(h, wf, mode="nt", tm=512, tn=FF // 2, tk=D, n=2 * FF, b_off=2 * PG, out_dtype=BF16, name=f"gu_{tag}")
    act = _swiglu_fwd(gu, f"swiglu_{tag}")
    out = _mm(act, wf, mode="nn", tm=512, tn=512, tk=FF, n=D, b_off=PD, out_dtype=F32, scale=0.5, res=x, name=f"down_{tag}")
    return out, (x, h, gu, act)


def _ffn_bwd(dout, saved, gain, wf, tag, after, emit):
    x, h, gu, act = saved
    grf = lax.empty((3, NDEV, FFS, D), BF16)
    da = _mm(dout, wf, mode="nt", tm=512, tn=FF // 2, tk=D, n=FF, b_off=2 * PD, out_dtype=BF16, scale=0.5, after=after,
             name=f"dact_{tag}")
    grf = _wgrad(act, dout, grf, piece0=PD, rows=FFS, tm=FF, tn=512, tk=512, scale=0.5, name=f"wg_down_{tag}")
    dgu = _swiglu_bwd(gu, da, f"dswiglu_{tag}")
    grf = _wgrad(dgu, h, grf, piece0=PG, rows=FFS, tm=FF, tn=512, tk=512, name=f"wg_gu_{tag}")
    token = emit([grf])
    dh = _mm(dgu, wf, mode="nn", tm=512, tn=512, tk=FF, n=D, b_off=PG, out_dtype=F32, after=token, name=f"dh_{tag}")
    dx, dgain = _rms_bwd(x, gain, dh, dout, f"drms_{tag}")
    return dx, dgain, token


def _mixer_fwd(x, sp, ww, wo, wa, wbc, tag):
    h = _rms_fwd(x, sp["mix_norm"], f"rms_mix_{tag}")
    z = _mm(h, ww, mode="nt", tm=512, tn=1152, tk=D, n=INW, out_dtype=F32, name=f"z_{tag}")
    acc, ml = _attn_fwd(z, None, None, gi=0, first=True, final=False, name=f"attn0_{tag}")
    acc, ml = _attn_fwd(z, acc, ml, gi=1, first=False, final=False, name=f"attn1_{tag}")
    ya, lse = _attn_fwd(z, acc, ml, gi=2, first=False, final=True, name=f"attn2_{tag}")
    yb = _gmlp_fwd(z, sp["b_ln_g"], sp["b_ln_b"], sp["b_w_s"], sp["b_b_s"], f"gmlp_{tag}")
    yc = _pool_fwd(z, sp["c_w"], sp["c_scale"], f"pool_{tag}")
    pa = _mm(ya, wa, mode="nt", tm=1024, tn=512, tk=GW, n=D, out_dtype=F32, name=f"proj_a_{tag}")
    pb = _mm(yb, wbc, mode="nt", tm=1024, tn=512, tk=BW, n=D, b_off=0, out_dtype=F32, name=f"proj_b_{tag}")
    pc = _mm(yc, wbc, mode="nt", tm=1024, tn=512, tk=BW, n=D, b_off=2, out_dtype=F32, name=f"proj_c_{tag}")
    merged = _gates_fwd(z, sp["b_gate"], pa, pb, pc, f"gates_{tag}")
    out = _mm(merged, wo, mode="nn", tm=512, tn=512, tk=D, n=D, out_dtype=F32, res=x, name=f"out_{tag}")
    return out, (x, h, z, ya, lse, yb, yc, pa, pb, pc, merged)


def _mixer_bwd(dout, saved, sp, ww, wo, wa, wbc, tag, after, emit):
    x, h, z, ya, lse, yb, yc, pa, pb, pc, merged = saved
    grw = lax.empty((1, NDEV, INS, D), BF16)
    gro = lax.empty((1, NDEV, DS, D), BF16)
    gra = lax.empty((1, NDEV, DS, GW), BF16)
    grbc = lax.empty((2, NDEV, DS, BW), BF16)
    dm = _mm(dout, wo, mode="nt", tm=512, tn=512, tk=D, n=D, out_dtype=F32, after=after, name=f"dmerged_{tag}")
    gro = _wgrad(merged, dout, gro, piece0=0, rows=DS, tm=D, tn=D, tk=512, name=f"wg_out_{tag}")
    dpa, dpb, dpc, dza, dzb, dzc, dba, dbb, dbc = _gates_bwd(z, sp["b_gate"], pa, pb, pc, dm, f"dgates_{tag}")
    dya = _mm(dpa, wa, mode="nn", tm=1024, tn=GW, tk=D, n=GW, out_dtype=F32, name=f"dya_{tag}")
    dyb = _mm(dpb, wbc, mode="nn", tm=1024, tn=BW, tk=D, n=BW, b_off=0, out_dtype=F32, name=f"dyb_{tag}")
    dyc = _mm(dpc, wbc, mode="nn", tm=1024, tn=BW, tk=D, n=BW, b_off=1, out_dtype=F32, name=f"dyc_{tag}")
    gra = _wgrad(dpa, ya, gra, piece0=0, rows=DS, tm=D, tn=GW, tk=512, name=f"wg_pa_{tag}")
    grbc = _wgrad(dpb, yb, grbc, piece0=0, rows=DS, tm=D, tn=BW, tk=512, name=f"wg_pb_{tag}")
    grbc = _wgrad(dpc, yc, grbc, piece0=1, rows=DS, tm=D, tn=BW, tk=512, name=f"wg_pc_{tag}")
    dqkv = [_attn_bwd(z, dya, ya, lse, gi=gi, name=f"dattn{gi}_{tag}") for gi in range(3)]
    duv, dws, dbs, dlng, dlnb = _gmlp_bwd(z, dyb, sp["b_ln_g"], sp["b_ln_b"], sp["b_w_s"], sp["b_b_s"], f"dgmlp_{tag}")
    dxc, dcw, dcs = _pool_bwd(z, dyc, sp["c_w"], sp["c_scale"], f"dpool_{tag}")
    dz = jnp.concatenate([dqkv[gi][part].astype(BF16) for part in range(3) for gi in range(3)] + [duv, dxc, dza, dzb, dzc],
                         axis=1)
    grw = _wgrad(dz, h, grw, piece0=0, rows=INS, tm=INW // 2, tn=512, tk=512, name=f"wg_in_{tag}")
    token = emit([grw, gro, gra, grbc])
    dh = _mm(dz, ww, mode="nn", tm=512, tn=512, tk=INW // 2, n=D, out_dtype=F32, after=token, name=f"dh_mix_{tag}")
    dx, dgain = _rms_bwd(x, sp["mix_norm"], dh, dout, f"drms_mix_{tag}")
    small = {"mix_norm": dgain, "b_gate": [dba, dbb, dbc], "b_ln_g": dlng, "b_ln_b": dlnb,
             "b_w_s": dws, "b_b_s": dbs, "c_w": dcw, "c_scale": dcs}
    return dx, small, token


PARTS = ("f1", "mix", "f2")


def _forward(xs, tgt, weights_of, small, final_gain):
    saved, views = [], []
    cur = xs
    for l in range(DEPTH):
        (wf1,) = weights_of(l, "f1", cur)
        cur, s1 = _ffn_fwd(cur, small[l]["ffn1_norm"], wf1, f"f1l{l}")
        wmix = weights_of(l, "mix", cur)
        cur, s2 = _mixer_fwd(cur, small[l], *wmix, f"l{l}")
        (wf2,) = weights_of(l, "f2", cur)
        cur, s3 = _ffn_fwd(cur, small[l]["ffn2_norm"], wf2, f"f2l{l}")
        saved.append((s1, s2, s3))
        views.append((wf1, wmix, wf2))
    loss_part, dcur, dfinal = _final_loss(cur, final_gain, tgt, "final_loss")
    return loss_part, dcur, dfinal, saved, views


def _backward_layer(dcur, saved_l, small_l, views_l, l, emit, after):
    wf1, wmix, wf2 = views_l
    s1, s2, s3 = saved_l
    dcur, dn2, token = _ffn_bwd(dcur, s3, small_l["ffn2_norm"], wf2, f"f2l{l}", after, lambda g: emit("f2", g))
    dcur, sg, token = _mixer_bwd(dcur, s2, small_l, *wmix, f"l{l}", token, lambda g: emit("mix", g))
    dcur, dn1, token = _ffn_bwd(dcur, s1, small_l["ffn1_norm"], wf1, f"f1l{l}", token, lambda g: emit("f1", g))
    sg["ffn1_norm"] = dn1
    sg["ffn2_norm"] = dn2
    return dcur, sg, token


BIG = ("ffn1_w_gate", "ffn1_w_up", "ffn1_w_down", "w_in", "w_proj_a", "w_proj_b", "w_proj_c", "w_out",
       "ffn2_w_gate", "ffn2_w_up", "ffn2_w_down")
ALL_WEIGHTS = ("ffn1_norm", "ffn1_w_gate", "ffn1_w_up", "ffn1_w_down", "mix_norm", "w_in", "b_gate", "b_ln_g", "b_ln_b",
               "b_w_s", "b_b_s", "c_w", "c_scale", "w_proj_a", "w_proj_b", "w_proj_c", "w_out", "ffn2_norm", "ffn2_w_gate",
               "ffn2_w_up", "ffn2_w_down", "final_norm")


def _part_shards(w, l, part):
    t16 = lambda a: a.T.astype(BF16)
    if part == "mix":
        return [t16(w["w_in"][l])[None], w["w_out"][l].astype(BF16)[None], t16(w["w_proj_a"][l])[None],
                jnp.stack([t16(w["w_proj_b"][l]), t16(w["w_proj_c"][l])])]
    f = "ffn1" if part == "f1" else "ffn2"
    return [jnp.stack([t16(w[f + "_w_gate"][l]), t16(w[f + "_w_up"][l]), w[f + "_w_down"][l].astype(BF16)])]


def _part_views(part, gathered):
    if part == "mix":
        pw, po, pa, pbc = gathered
        return [pw.reshape(INW, D), po.reshape(D, D), pa.reshape(D, GW), pbc.reshape(2 * D, BW)]
    return [gathered[0].reshape(3 * FF, D)]


def _part_grads(part, sums):
    if part == "mix":
        gw, go, ga, gbc = sums
        return {"w_in": gw[0].T, "w_out": go[0], "w_proj_a": ga[0].T, "w_proj_b": gbc[0].T, "w_proj_c": gbc[1].T}
    f = "ffn1" if part == "f1" else "ffn2"
    return {f + "_w_gate": sums[0][0].T, f + "_w_up": sums[0][1].T, f + "_w_down": sums[0][2]}


def _layer_small(w, l):
    r = lambda a, shape: a.reshape(shape)
    return {"ffn1_norm": r(w["ffn1_norm"][l], (1, D)), "mix_norm": r(w["mix_norm"][l], (1, D)),
            "b_gate": r(w["b_gate"][l], (1, 3 * D)), "b_ln_g": r(w["b_ln_g"][l], (1, BW)), "b_ln_b": r(w["b_ln_b"][l], (1, BW)),
            "b_w_s": w["b_w_s"][l], "b_b_s": r(w["b_b_s"][l], (4, BLK, 1)), "c_w": w["c_w"][l],
            "c_scale": r(w["c_scale"][l], (4, 1, BLK)), "ffn2_norm": r(w["ffn2_norm"][l], (1, D))}


def kernel(x, ffn1_norm, ffn1_w_gate, ffn1_w_up, ffn1_w_down, mix_norm, w_in, b_gate, b_ln_g, b_ln_b, b_w_s, b_b_s, c_w, c_scale, w_proj_a, w_proj_b, w_proj_c, w_out, ffn2_norm, ffn2_w_gate, ffn2_w_up, ffn2_w_down, final_norm, loss_target, m_ffn1_norm, m_ffn1_w_gate, m_ffn1_w_up, m_ffn1_w_down, m_mix_norm, m_w_in, m_b_gate, m_b_ln_g, m_b_ln_b, m_b_w_s, m_b_b_s, m_c_w, m_c_scale, m_w_proj_a, m_w_proj_b, m_w_proj_c, m_w_out, m_ffn2_norm, m_ffn2_w_gate, m_ffn2_w_up, m_ffn2_w_down, m_final_norm, v_ffn1_norm, v_ffn1_w_gate, v_ffn1_w_up, v_ffn1_w_down, v_mix_norm, v_w_in, v_b_gate, v_b_ln_g, v_b_ln_b, v_b_w_s, v_b_b_s, v_c_w, v_c_scale, v_w_proj_a, v_w_proj_b, v_w_proj_c, v_w_out, v_ffn2_norm, v_ffn2_w_gate, v_ffn2_w_up, v_ffn2_w_down, v_final_norm):
    w = dict(ffn1_norm=ffn1_norm, ffn1_w_gate=ffn1_w_gate, ffn1_w_up=ffn1_w_up, ffn1_w_down=ffn1_w_down, mix_norm=mix_norm,
             w_in=w_in, b_gate=b_gate, b_ln_g=b_ln_g, b_ln_b=b_ln_b, b_w_s=b_w_s, b_b_s=b_b_s, c_w=c_w, c_scale=c_scale,
             w_proj_a=w_proj_a, w_proj_b=w_proj_b, w_proj_c=w_proj_c, w_out=w_out, ffn2_norm=ffn2_norm, ffn2_w_gate=ffn2_w_gate,
             ffn2_w_up=ffn2_w_up, ffn2_w_down=ffn2_w_down, final_norm=final_norm)
    mom = dict(ffn1_norm=m_ffn1_norm, ffn1_w_gate=m_ffn1_w_gate, ffn1_w_up=m_ffn1_w_up, ffn1_w_down=m_ffn1_w_down,
               mix_norm=m_mix_norm, w_in=m_w_in, b_gate=m_b_gate, b_ln_g=m_b_ln_g, b_ln_b=m_b_ln_b, b_w_s=m_b_w_s, b_b_s=m_b_b_s,
               c_w=m_c_w, c_scale=m_c_scale, w_proj_a=m_w_proj_a, w_proj_b=m_w_proj_b, w_proj_c=m_w_proj_c, w_out=m_w_out,
               ffn2_norm=m_ffn2_norm, ffn2_w_gate=m_ffn2_w_gate, ffn2_w_up=m_ffn2_w_up, ffn2_w_down=m_ffn2_w_down,
               final_norm=m_final_norm)
    var = dict(ffn1_norm=v_ffn1_norm, ffn1_w_gate=v_ffn1_w_gate, ffn1_w_up=v_ffn1_w_up, ffn1_w_down=v_ffn1_w_down,
               mix_norm=v_mix_norm, w_in=v_w_in, b_gate=v_b_gate, b_ln_g=v_b_ln_g, b_ln_b=v_b_ln_b, b_w_s=v_b_w_s, b_b_s=v_b_b_s,
               c_w=v_c_w, c_scale=v_c_scale, w_proj_a=v_w_proj_a, w_proj_b=v_w_proj_b, w_proj_c=v_w_proj_c, w_out=v_w_out,
               ffn2_norm=v_ffn2_norm, ffn2_w_gate=v_ffn2_w_gate, ffn2_w_up=v_ffn2_w_up, ffn2_w_down=v_ffn2_w_down,
               final_norm=v_final_norm)

    t_dim = x.shape[1]
    xs = x.reshape(t_dim, D)
    tgt = loss_target.reshape(t_dim, D)

    me = 4 * lax.axis_index("x") + 2 * lax.axis_index("y") + lax.axis_index("c")
    gathers = {}
    token = jnp.zeros((8, 128), F32)
    for l in range(DEPTH):
        for part in PARTS:
            lands = [lax.dynamic_update_slice(lax.empty((s.shape[0], NDEV) + s.shape[1:], BF16), s[:, None], (0, me, 0, 0))
                     for s in _part_shards(w, l, part)]
            gathers[l, part] = _exchange_start("gather", lands, token, f"gather_start_{part}_l{l}")
            token = gathers[l, part][-1]

    def weights_of(l, part, after):
        behind = token if (l, part) == (0, "f1") else after
        return _part_views(part, _exchange_wait("gather", gathers[l, part], behind, f"gather_wait_{part}_l{l}"))

    small = [_layer_small(w, l) for l in range(DEPTH)]
    loss_part, dcur, dfinal, saved, views = _forward(xs, tgt, weights_of, small, w["final_norm"].reshape(1, D))
    loss = lax.psum(loss_part[0, 0], ("x", "y", "c"))

    small_grads = [None] * DEPTH
    scatters = []
    token = None
    for l in reversed(range(DEPTH)):
        def emit(part, grads, l=l):
            scatters.append((l, part, _exchange_start("scatter", grads, grads[0], f"scatter_start_{part}_l{l}")))
            return scatters[-1][2][-1]
        dcur, small_grads[l], token = _backward_layer(dcur, saved[l], small[l], views[l], l, emit, token)
    grad_x = dcur.reshape(x.shape)

    me_arr = me.astype(jnp.int32).reshape(1)
    packed = _pack_small_grads(small_grads, dfinal, me_arr, "pack_small_grads")
    small_gather = _exchange_start("gather", [packed], token, "gather_start_small")

    big_grads = {}
    behind = small_gather[-1]
    for l, part, started in scatters:
        arrs = _exchange_wait("scatter", started, behind, f"scatter_wait_{part}_l{l}")
        na = len(arrs) // 2
        sums = [_sum_landed(arrs[na + i], arrs[i], me_arr, f"sum_grads_{part}_l{l}_{i}") for i in range(na)]
        behind = sums[0]
        for k, g in _part_grads(part, sums).items():
            big_grads[l, k] = g

    grads, delta, new_m, new_v = {}, {}, {}, {}
    for k in BIG:
        g = jnp.stack([big_grads[l, k] for l in range(DEPTH)])
        shp = g.shape
        flat = lambda a: a.reshape(shp[0] * shp[1], shp[2])
        d_k, nm_k, nv_k = _adamw(flat(w[k]), flat(g), flat(mom[k]), flat(var[k]), f"adamw_{k}")
        grads[k], delta[k], new_m[k], new_v[k] = g, d_k.reshape(shp), nm_k.reshape(shp), nv_k.reshape(shp)
        behind = d_k
    gathered = _exchange_wait("gather", small_gather, behind, "gather_wait_small")[0]
    as_row = lambda t: dict(t, final_norm=t["final_norm"].reshape(1, D))
    for dst, src in zip((grads, delta, new_m, new_v), _adamw_small(gathered, as_row(w), as_row(mom), as_row(var), "adamw_small")):
        dst.update(src)
        dst["final_norm"] = src["final_norm"].reshape(D)

    return (loss, grad_x, *[grads[k] for k in ALL_WEIGHTS], *[delta[k] for k in ALL_WEIGHTS],
            *[new_m[k] for k in ALL_WEIGHTS], *[new_v[k] for k in ALL_WEIGHTS])
```

```python
import math

import jax
import jax.numpy as jnp
from jax import lax
from jax.experimental import pallas as pl
from jax.experimental.pallas import tpu as pltpu

F32 = jnp.float32
BF16 = jnp.bfloat16
MESH = pl.DeviceIdType.MESH

NDEV = 8
DEPTH = 2
D = 1024
FF = 2816
FFS = FF // NDEV
INW = 6912
INS = INW // NDEV
DS = D // NDEV
BLK = 128
NH = 4
HD = 64
GW = NH * HD
DILS = (1, 4, 16)
QK_SCALE = 1.0 / math.sqrt(HD)
ZB = INW // GW
Q_B, K_B, V_B = 0, 3, 6
UV_B = 9
C_B128 = 26
G_B = 15
BW = 512
EPS = 1e-6
NEG = -1e30
VMEM_CAP = 60 * 1024 * 1024

ADAM_LR, ADAM_B1, ADAM_B2, ADAM_EPS, ADAM_WD, ADAM_STEP = 0.001, 0.9, 0.999, 1e-08, 0.01, 10

NT = (((1,), (1,)), ((), ()))
TN = (((0,), (0,)), ((), ()))


def _bs(shape, imap):
    return pl.BlockSpec(shape, imap)


def _pc(body, *, name, grid, in_specs, out_specs, out_shape, scratch=(), alias=None, vmem=None):
    params = {}
    if vmem is not None:
        params["vmem_limit_bytes"] = min(int(vmem), VMEM_CAP)
    big = lambda s: math.prod(s.shape) * jnp.dtype(s.dtype).itemsize >= (1 << 20)
    pin = lambda s: pltpu.HBM(s.shape, s.dtype) if big(s) else s
    out_shape = [pin(s) for s in out_shape] if isinstance(out_shape, (list, tuple)) else pin(out_shape)
    call = pl.pallas_call(
        body, name=name, grid=grid, in_specs=in_specs, out_specs=out_specs, out_shape=out_shape,
        scratch_shapes=list(scratch), input_output_aliases=alias or {},
        compiler_params=pltpu.CompilerParams(**params), interpret=False)
    return lambda *args: call(*[pltpu.with_memory_space_constraint(a, pltpu.HBM) if big(a) else a for a in args])


def _sds(shape, dtype):
    return jax.ShapeDtypeStruct(shape, dtype)


def _row_tile(rows, align, cap):
    best = None
    for cand in range(align, min(rows, cap) + 1, align):
        if rows % cand == 0:
            best = cand
    assert best is not None, (rows, align, cap)
    return best


def _mm(a, b, *, mode, tm, tn, tk, n, out_dtype, name, b_row0=0, scale=None, res=None, after=None):
    m, k_dim = a.shape
    nk = k_dim // tk
    assert m % tm == 0 and n % tn == 0 and k_dim % tk == 0
    b_off, rem = divmod(b_row0, tk if mode == "nn" else tn)
    assert rem == 0
    grid = (m // tm, n // tn, nk)
    in_specs = [_bs((tm, tk), lambda i, j, k: (i, k))]
    if mode == "nn":
        in_specs.append(_bs((tk, tn), lambda i, j, k: (b_off + k, j)))
    else:
        in_specs.append(_bs((tn, tk), lambda i, j, k: (b_off + j, k)))
    args = [a, b]
    if res is not None:
        in_specs.append(_bs((tm, tn), lambda i, j, k: (i, j)))
        args.append(res)
    if after is not None:
        in_specs.append(pl.BlockSpec(memory_space=pl.ANY))
        args.append(after)
    n_in = len(args)

    def body(*refs):
        a_ref, b_ref = refs[0], refs[1]
        r_ref = refs[2] if res is not None else None
        o_ref = refs[n_in]
        av = a_ref[...].astype(BF16)
        bv = b_ref[...].astype(BF16)
        if mode == "nn":
            p = jnp.dot(av, bv, preferred_element_type=F32)
        else:
            p = lax.dot_general(av, bv, NT, preferred_element_type=F32)

        def fin(v):
            if scale is not None:
                v = v * scale
            if r_ref is not None:
                v = r_ref[...] + v
            o_ref[...] = v.astype(out_dtype)

        if nk == 1:
            fin(p)
        else:
            acc_ref = refs[-1]
            kk = pl.program_id(2)

            @pl.when(kk == 0)
            def _():
                acc_ref[...] = p

            @pl.when(kk > 0)
            def _():
                acc_ref[...] += p

            @pl.when(kk == nk - 1)
            def _():
                fin(acc_ref[...])

    scratch = [pltpu.VMEM((tm, tn), F32)] if nk > 1 else []
    est = 2 * (tm * tk * a.dtype.itemsize + tk * tn * b.dtype.itemsize + tm * tn * jnp.dtype(out_dtype).itemsize)
    est += (2 + (nk > 1)) * tm * tn * 4 + (2 * tm * tn * 4 if res is not None else 0)
    est += (tm * tk * 2 if a.dtype != BF16 else 0) + (tk * tn * 2 if b.dtype != BF16 else 0)
    return _pc(body, name=name, grid=grid, in_specs=in_specs, out_specs=_bs((tm, tn), lambda i, j, k: (i, j)),
               out_shape=_sds((m, n), out_dtype), scratch=scratch, vmem=est + (8 << 20))(*args)


def _wgrad(a, b, buf, *, piece0, rows, tm, tn, tk, name, scale=None):
    t_dim, m = a.shape
    n = b.shape[1]
    db = tm // rows
    tpp = NDEV // db
    nk = t_dim // tk
    assert tm % rows == 0 and NDEV % db == 0 and m % tm == 0 and n % tn == 0 and t_dim % tk == 0
    grid = (m // tm, n // tn, nk)

    def body(a_ref, b_ref, buf_ref, o_ref, acc_ref):
        del buf_ref
        kk = pl.program_id(2)
        p = lax.dot_general(a_ref[...].astype(BF16), b_ref[...].astype(BF16), TN, preferred_element_type=F32)

        @pl.when(kk == 0)
        def _():
            acc_ref[...] = p

        @pl.when(kk > 0)
        def _():
            acc_ref[...] += p

        @pl.when(kk == nk - 1)
        def _():
            for d in range(db):
                v = acc_ref[d * rows:(d + 1) * rows, :]
                if scale is not None:
                    v = v * scale
                o_ref[0, d] = v.astype(BF16)

    est = 2 * (tk * tm * a.dtype.itemsize + tk * tn * b.dtype.itemsize + tm * tn * 2) + 3 * tm * tn * 4 + tk * tm * 4
    return _pc(
        body, name=name, grid=grid,
        in_specs=[_bs((tk, tm), lambda i, j, k: (k, i)), _bs((tk, tn), lambda i, j, k: (k, j)),
                  pl.BlockSpec(memory_space=pl.ANY)],
        out_specs=_bs((1, db, rows, tn), lambda i, j, k: (piece0 + i // tpp, i % tpp, 0, j)),
        out_shape=_sds(buf.shape, buf.dtype), scratch=[pltpu.VMEM((tm, tn), F32)], alias={2: 0},
        vmem=est + (8 << 20))(a, b, buf)


def _rms_fwd(x, gain, name, tm=512):
    t_dim = x.shape[0]

    def body(x_ref, g_ref, o_ref):
        xv = x_ref[...]
        r = lax.rsqrt(jnp.mean(xv * xv, axis=-1, keepdims=True) + EPS)
        o_ref[...] = (xv * r * g_ref[...]).astype(BF16)

    return _pc(body, name=name, grid=(t_dim // tm,),
               in_specs=[_bs((tm, D), lambda i: (i, 0)), _bs((1, D), lambda i: (0, 0))],
               out_specs=_bs((tm, D), lambda i: (i, 0)), out_shape=_sds((t_dim, D), BF16))(x, gain)


def _rms_bwd(x, gain, dh, dres, name, tm=512):
    t_dim = x.shape[0]

    def body(x_ref, g_ref, dh_ref, dr_ref, dx_ref, dg_ref):
        xv = x_ref[...]
        r = lax.rsqrt(jnp.mean(xv * xv, axis=-1, keepdims=True) + EPS)
        dh_v = dh_ref[...].astype(F32)
        dyg = dh_v * g_ref[...]
        mq = jnp.mean(dyg * xv, axis=-1, keepdims=True)
        dx_ref[...] = dr_ref[...] + r * (dyg - xv * (r * r * mq))

        @pl.when(pl.program_id(0) == 0)
        def _():
            dg_ref[...] = jnp.zeros_like(dg_ref)

        dg_ref[...] += jnp.sum(dh_v * (xv * r), axis=0, keepdims=True)

    return _pc(body, name=name, grid=(t_dim // tm,),
               in_specs=[_bs((tm, D), lambda i: (i, 0)), _bs((1, D), lambda i: (0, 0)),
                         _bs((tm, D), lambda i: (i, 0)), _bs((tm, D), lambda i: (i, 0))],
               out_specs=[_bs((tm, D), lambda i: (i, 0)), _bs((1, D), lambda i: (0, 0))],
               out_shape=[_sds((t_dim, D), F32), _sds((1, D), F32)])(x, gain, dh, dres)


def _sigmoid(v):
    return 1.0 / (1.0 + jnp.exp(-v))


def _swiglu_fwd(gu, name, tm=256):
    t_dim = gu.shape[0]

    def body(gu_ref, o_ref):
        g = gu_ref[:, :FF].astype(F32)
        u = gu_ref[:, FF:].astype(F32)
        o_ref[...] = (g * _sigmoid(g) * u).astype(BF16)

    return _pc(body, name=name, grid=(t_dim // tm,), in_specs=[_bs((tm, 2 * FF), lambda i: (i, 0))],
               out_specs=_bs((tm, FF), lambda i: (i, 0)), out_shape=_sds((t_dim, FF), BF16))(gu)


def _swiglu_bwd(gu, da, name, tm=256):
    t_dim = gu.shape[0]

    def body(gu_ref, da_ref, o_ref):
        g = gu_ref[:, :FF].astype(F32)
        u = gu_ref[:, FF:].astype(F32)
        dav = da_ref[...].astype(F32)
        s = _sigmoid(g)
        o_ref[:, :FF] = (dav * u * (s * (1.0 + g * (1.0 - s)))).astype(BF16)
        o_ref[:, FF:] = (dav * (g * s)).astype(BF16)

    return _pc(body, name=name, grid=(t_dim // tm,),
               in_specs=[_bs((tm, 2 * FF), lambda i: (i, 0)), _bs((tm, FF), lambda i: (i, 0))],
               out_specs=_bs((tm, 2 * FF), lambda i: (i, 0)), out_shape=_sds((t_dim, 2 * FF), BF16))(gu, da)


def _final_loss(x, gain, tgt, name, tm=512):
    t_dim = x.shape[0]

    def body(x_ref, g_ref, t_ref, loss_ref, dx_ref, dg_ref):
        xv = x_ref[...]
        gv = g_ref[...]
        r = lax.rsqrt(jnp.mean(xv * xv, axis=-1, keepdims=True) + EPS)
        xn = xv * r
        err = xn * gv - t_ref[...]
        dy = err * (1.0 / D)
        dyg = dy * gv
        mq = jnp.mean(dyg * xv, axis=-1, keepdims=True)
        dx_ref[...] = r * (dyg - xv * (r * r * mq))

        @pl.when(pl.program_id(0) == 0)
        def _():
            dg_ref[...] = jnp.zeros_like(dg_ref)
            loss_ref[...] = jnp.zeros_like(loss_ref)

        dg_ref[...] += jnp.sum(dy * xn, axis=0, keepdims=True)
        part = 0.5 * jnp.sum(jnp.mean(err * err, axis=-1, keepdims=True), axis=0, keepdims=True)
        loss_ref[...] += jnp.broadcast_to(part, loss_ref.shape)

    return _pc(body, name=name, grid=(t_dim // tm,),
               in_specs=[_bs((tm, D), lambda i: (i, 0)), _bs((1, D), lambda i: (0, 0)), _bs((tm, D), lambda i: (i, 0))],
               out_specs=[_bs((8, 128), lambda i: (0, 0)), _bs((tm, D), lambda i: (i, 0)), _bs((1, D), lambda i: (0, 0))],
               out_shape=[_sds((8, 128), F32), _sds((t_dim, D), F32), _sds((1, D), F32)])(x, gain, tgt)


def _band_mask(n):
    ri = lax.broadcasted_iota(jnp.int32, (BLK, 2 * BLK), 0)
    cj = lax.broadcasted_iota(jnp.int32, (BLK, 2 * BLK), 1)
    dist = ri + BLK - cj
    return (dist >= 0) & (dist <= BLK) & ((cj >= BLK) | (n > 0))


def _halves(unit, imap_rows, col):
    return [_bs((unit, 128), (lambda u, j=j: (imap_rows(u), 2 * col + j))) for j in range(2)]


def _get_rows(pair, r, dil):
    return jnp.concatenate([h[pl.ds(r, BLK, stride=dil), :] for h in pair], axis=1)


def _put_rows(stage, r, dil, val):
    for j in range(2):
        stage[j, pl.ds(r, BLK, stride=dil), :] = val[:, 128 * j:128 * (j + 1)]


def _unstage(stage, out_ref):
    out_ref[:, :128] = stage[0]
    out_ref[:, 128:] = stage[1]


def _attn_fwd(z, acc, ml, *, gi, first, final, name):
    dil = DILS[gi]
    t_dim = z.shape[0]
    unit = BLK * dil
    nu = t_dim // unit
    own = lambda u: u
    prev = lambda u: jnp.maximum(u - 1, 0)
    in_specs = (_halves(unit, own, Q_B + gi) + _halves(unit, own, K_B + gi) + _halves(unit, prev, K_B + gi)
                + _halves(unit, own, V_B + gi) + _halves(unit, prev, V_B + gi))
    args = [z] * 10
    st_ml = _bs((unit, 128), lambda u: (u, 0))
    if not first:
        in_specs += _halves(unit, own, 0) + [st_ml]
        args += [acc, acc, ml]

    def body(*refs):
        q2, ko2, kp2, vo2, vp2 = (refs[2 * i:2 * i + 2] for i in range(5))
        o_acc_ref, o_ml_ref, stage = refs[-3], refs[-2], refs[-1]
        valid = _band_mask(pl.program_id(0))
        lane = lax.broadcasted_iota(jnp.int32, (BLK, GW), 1)
        lane_s = lax.broadcasted_iota(jnp.int32, (BLK, 128), 1)

        def band(r, carry):
            q = _get_rows(q2, r, dil) * QK_SCALE
            kb = jnp.concatenate([_get_rows(kp2, r, dil), _get_rows(ko2, r, dil)], axis=0).astype(BF16)
            vb = jnp.concatenate([_get_rows(vp2, r, dil), _get_rows(vo2, r, dil)], axis=0).astype(BF16)
            if first:
                acc_v = jnp.zeros((BLK, GW), F32)
                ml_v = jnp.where(lane_s < 64, NEG, 0.0).astype(F32)
            else:
                acc_v = _get_rows(refs[10:12], r, dil)
                ml_v = refs[12][pl.ds(r, BLK, stride=dil), :]
            acc_o, ml_o = acc_v, ml_v
            l_b = jnp.ones((BLK, GW), F32)
            for h in range(NH):
                hm = (lane >= h * HD) & (lane < (h + 1) * HD)
                qh = jnp.where(hm, q, 0.0).astype(BF16)
                s = lax.dot_general(qh, kb, NT, preferred_element_type=F32)
                s = jnp.where(valid, s, NEG)
                m_old = ml_v[:, 16 * h:16 * h + 1]
                l_old = ml_v[:, 64 + 16 * h:64 + 16 * h + 1]
                m_new = jnp.maximum(m_old, jnp.max(s, axis=-1, keepdims=True))
                alpha = jnp.exp(m_old - m_new)
                p = jnp.exp(s - m_new)
                l_new = alpha * l_old + jnp.sum(p, axis=-1, keepdims=True)
                pv = jnp.dot(p.astype(BF16), vb, preferred_element_type=F32)
                acc_o = jnp.where(hm, acc_v * alpha + pv, acc_o)
                l_b = jnp.where(hm, l_new, l_b)
                if final:
                    ml_o = jnp.where((lane_s >= 16 * h) & (lane_s < 16 * h + 16), m_new + jnp.log(l_new), ml_o)
                else:
                    ml_o = jnp.where((lane_s >= 16 * h) & (lane_s < 16 * h + 16), m_new, ml_o)
                    ml_o = jnp.where((lane_s >= 64 + 16 * h) & (lane_s < 64 + 16 * h + 16), l_new, ml_o)
            _put_rows(stage, r, dil, acc_o / l_b if final else acc_o)
            o_ml_ref[pl.ds(r, BLK, stride=dil), :] = ml_o
            return carry

        lax.fori_loop(0, dil, band, 0)
        _unstage(stage, o_acc_ref)

    blocks = (len(in_specs) + 6) * unit * 128 * 4
    return _pc(body, name=name, grid=(nu,), in_specs=in_specs, out_specs=[_bs((unit, GW), lambda u: (u, 0)), st_ml],
               out_shape=[_sds((t_dim, GW), F32), _sds((t_dim, 128), F32)],
               scratch=[pltpu.VMEM((2, unit, 128), F32)], vmem=2 * blocks + (12 << 20))(*args)


def _attn_bwd(z, dy, y, lse, *, gi, name):
    dil = DILS[gi]
    t_dim = z.shape[0]
    unit = BLK * dil
    nu = t_dim // unit

    own = lambda u: jnp.minimum(u, nu - 1)
    prev = lambda u: jnp.maximum(jnp.minimum(u, nu - 1) - 1, 0)
    in_specs = (_halves(unit, own, Q_B + gi) + _halves(unit, own, K_B + gi) + _halves(unit, prev, K_B + gi)
                + _halves(unit, own, V_B + gi) + _halves(unit, prev, V_B + gi)
                + _halves(unit, own, 0) + _halves(unit, own, 0) + [_bs((unit, 128), lambda u: (own(u), 0))])
    tok = _bs((unit, GW), lambda u: (own(u), 0))
    kv_out = _bs((unit, GW), lambda u: (jnp.maximum(u - 1, 0), 0))

    def body(*refs):
        q2, ko2, kp2, vo2, vp2, dy2, y2 = (refs[2 * i:2 * i + 2] for i in range(7))
        lse_ref = refs[14]
        dq_ref, dk_ref, dv_ref, dq_st, dk_st, dv_st, dkc, dvc = refs[15:]
        u = pl.program_id(0)

        @pl.when(u == 0)
        def _():
            dkc[...] = jnp.zeros_like(dkc)
            dvc[...] = jnp.zeros_like(dvc)

        @pl.when(u < nu)
        def _():
            valid = _band_mask(u)
            lane = lax.broadcasted_iota(jnp.int32, (BLK, GW), 1)

            def band(r, carry):
                q = _get_rows(q2, r, dil) * QK_SCALE
                kb = jnp.concatenate([_get_rows(kp2, r, dil), _get_rows(ko2, r, dil)], axis=0).astype(BF16)
                vb = jnp.concatenate([_get_rows(vp2, r, dil), _get_rows(vo2, r, dil)], axis=0).astype(BF16)
                dyv = _get_rows(dy2, r, dil)
                yv = _get_rows(y2, r, dil)
                lse_v = lse_ref[pl.ds(r, BLK, stride=dil), :]
                dq = jnp.zeros((BLK, GW), F32)
                dkb = jnp.zeros((2 * BLK, GW), F32)
                dvb = jnp.zeros((2 * BLK, GW), F32)
                for h in range(NH):
                    hm = (lane >= h * HD) & (lane < (h + 1) * HD)
                    qh = jnp.where(hm, q, 0.0).astype(BF16)
                    s = lax.dot_general(qh, kb, NT, preferred_element_type=F32)
                    p = jnp.where(valid, jnp.exp(s - lse_v[:, 16 * h:16 * h + 1]), 0.0)
                    dyh = jnp.where(hm, dyv, 0.0)
                    delta = jnp.sum(dyh * yv, axis=-1, keepdims=True)
                    dyh16 = dyh.astype(BF16)
                    dp = lax.dot_general(dyh16, vb, NT, preferred_element_type=F32)
                    ds = (p * (dp - delta)).astype(BF16)
                    dq = dq + jnp.where(hm, jnp.dot(ds, kb, preferred_element_type=F32), 0.0)
                    dkb = dkb + lax.dot_general(ds, qh, TN, preferred_element_type=F32)
                    dvb = dvb + lax.dot_general(p.astype(BF16), dyh16, TN, preferred_element_type=F32)
                _put_rows(dq_st, r, dil, dq * QK_SCALE)
                _put_rows(dk_st, r, dil, _get_rows((dkc.at[0], dkc.at[1]), r, dil) + dkb[:BLK])
                _put_rows(dv_st, r, dil, _get_rows((dvc.at[0], dvc.at[1]), r, dil) + dvb[:BLK])
                _put_rows(dkc, r, dil, dkb[BLK:])
                _put_rows(dvc, r, dil, dvb[BLK:])
                return carry

            lax.fori_loop(0, dil, band, 0)
            _unstage(dq_st, dq_ref)

            @pl.when(u > 0)
            def _():
                _unstage(dk_st, dk_ref)
                _unstage(dv_st, dv_ref)

        @pl.when(u == nu)
        def _():
            _unstage(dkc, dk_ref)
            _unstage(dvc, dv_ref)

    blocks = (15 + 6 + 5) * unit * 128 * 4 * 2
    return _pc(body, name=name, grid=(nu + 1,), in_specs=in_specs, out_specs=[tok, kv_out, kv_out],
               out_shape=[_sds((t_dim, GW), F32)] * 3, scratch=[pltpu.VMEM((2, unit, 128), F32)] * 5,
               vmem=blocks + (12 << 20))(*([z] * 10), dy, dy, y, y, lse)


_INV_SQRT2 = 1.0 / math.sqrt(2.0)
_INV_SQRT2PI = 1.0 / math.sqrt(2.0 * math.pi)


def _gelu(v):
    return 0.5 * v * (1.0 + lax.erf(v * _INV_SQRT2))


def _gelu_grad(v):
    return 0.5 * (1.0 + lax.erf(v * _INV_SQRT2)) + v * (_INV_SQRT2PI * jnp.exp(-0.5 * v * v))


def _gmlp_core(z_refs, lng, lnb, ws_ref, bs_ref):
    u_pre = jnp.concatenate([z_refs[0][...], z_refs[1][...]], axis=1)
    v_pre = jnp.concatenate([z_refs[2][...], z_refs[3][...]], axis=1)
    u = _gelu(u_pre)
    v = _gelu(v_pre)
    xc = v - jnp.mean(v, axis=-1, keepdims=True)
    rstd = lax.rsqrt(jnp.mean(xc * xc, axis=-1, keepdims=True) + EPS)
    xhat = xc * rstd
    vn = xhat * lng + lnb
    ti = lax.broadcasted_iota(jnp.int32, (BLK, BLK), 0)
    si = lax.broadcasted_iota(jnp.int32, (BLK, BLK), 1)
    causal = ti >= si
    vn16 = vn.astype(BF16)
    ws = [jnp.where(causal, ws_ref[g], 0.0).astype(BF16) for g in range(4)]
    mixed = jnp.concatenate(
        [jnp.dot(ws[g], vn16[:, g * BLK:(g + 1) * BLK], preferred_element_type=F32) + bs_ref[g] for g in range(4)], axis=1)
    return u_pre, v_pre, u, xhat, rstd, vn16, ws, causal, mixed


def _gmlp_specs():
    z_specs = [_bs((BLK, GW), (lambda i, c=c: (i, UV_B + c))) for c in range(4)]
    par_specs = [_bs((1, BW), lambda i: (0, 0)), _bs((1, BW), lambda i: (0, 0)),
                 _bs((4, BLK, BLK), lambda i: (0, 0, 0)), _bs((4, BLK, 1), lambda i: (0, 0, 0))]
    return z_specs, par_specs


def _gmlp_fwd(z, lng, lnb, ws, bs, name):
    t_dim = z.shape[0]
    z_specs, par_specs = _gmlp_specs()

    def body(z0, z1, z2, z3, lng_ref, lnb_ref, ws_ref, bs_ref, o_ref):
        core = _gmlp_core((z0, z1, z2, z3), lng_ref[...], lnb_ref[...], ws_ref, bs_ref)
        o_ref[...] = (core[2] * core[8]).astype(BF16)

    return _pc(body, name=name, grid=(t_dim // BLK,), in_specs=z_specs + par_specs,
               out_specs=_bs((BLK, BW), lambda i: (i, 0)), out_shape=_sds((t_dim, BW), BF16))(z, z, z, z, lng, lnb, ws, bs)


def _gmlp_bwd(z, dy, lng, lnb, ws, bs, name):
    t_dim = z.shape[0]
    z_specs, par_specs = _gmlp_specs()

    def body(z0, z1, z2, z3, lng_ref, lnb_ref, ws_ref, bs_ref, dy_ref, duv_ref, dws_ref, dbs_ref, dlng_ref, dlnb_ref):
        lng_v = lng_ref[...]
        u_pre, v_pre, u, xhat, rstd, vn16, wsm, causal, mixed = _gmlp_core((z0, z1, z2, z3), lng_v, lnb_ref[...], ws_ref, bs_ref)
        dyv = dy_ref[...].astype(F32)
        du = dyv * mixed
        dmixed = dyv * u

        @pl.when(pl.program_id(0) == 0)
        def _():
            dws_ref[...] = jnp.zeros_like(dws_ref)
            dbs_ref[...] = jnp.zeros_like(dbs_ref)
            dlng_ref[...] = jnp.zeros_like(dlng_ref)
            dlnb_ref[...] = jnp.zeros_like(dlnb_ref)

        dvn_parts = []
        for g in range(4):
            dm = dmixed[:, g * BLK:(g + 1) * BLK]
            dm16 = dm.astype(BF16)
            dw = lax.dot_general(dm16, vn16[:, g * BLK:(g + 1) * BLK], NT, preferred_element_type=F32)
            dws_ref[g] += jnp.where(causal, dw, 0.0)
            dbs_ref[g] += jnp.sum(dm.T, axis=0, keepdims=True)
            dvn_parts.append(lax.dot_general(wsm[g], dm16, TN, preferred_element_type=F32))
        dvn = jnp.concatenate(dvn_parts, axis=1)
        dlng_ref[...] += jnp.sum(dvn * xhat, axis=0, keepdims=True)
        dlnb_ref[...] += jnp.sum(dvn, axis=0, keepdims=True)
        dxh = dvn * lng_v
        dv = rstd * (dxh - jnp.mean(dxh, axis=-1, keepdims=True) - xhat * jnp.mean(dxh * xhat, axis=-1, keepdims=True))
        duv_ref[:, :BW] = (du * _gelu_grad(u_pre)).astype(BF16)
        duv_ref[:, BW:] = (dv * _gelu_grad(v_pre)).astype(BF16)

    return _pc(body, name=name, grid=(t_dim // BLK,),
               in_specs=z_specs + par_specs + [_bs((BLK, BW), lambda i: (i, 0))],
               out_specs=[_bs((BLK, 2 * BW), lambda i: (i, 0)), _bs((4, BLK, BLK), lambda i: (0, 0, 0)),
                          _bs((4, 1, BLK), lambda i: (0, 0, 0)), _bs((1, BW), lambda i: (0, 0)), _bs((1, BW), lambda i: (0, 0))],
               out_shape=[_sds((t_dim, 2 * BW), BF16), _sds((4, BLK, BLK), F32), _sds((4, 1, BLK), F32),
                          _sds((1, BW), F32), _sds((1, BW), F32)])(z, z, z, z, lng, lnb, ws, bs, dy)


def _pool_core(x, gi):
    t_dim = x.shape[0]
    w = jnp.left_shift(2, gi)
    row = lax.broadcasted_iota(jnp.int32, x.shape, 0)
    s = x
    for k in (1, 2, 4, 8):
        sh = jnp.where(row >= k, pltpu.roll(s, k, 0), 0.0)
        s = jnp.where(k < w, s + sh, s)
    cnt = jnp.minimum(row + 1, w).astype(F32)
    return s / cnt - x, cnt, row, w, t_dim


def _pool_fwd(z, cw, cs, name):
    t_dim = z.shape[0]

    def body(z_ref, cw_ref, cs_ref, o_ref):
        pooled = _pool_core(z_ref[...], pl.program_id(0))[0]
        y = jnp.dot(pooled.astype(BF16), cw_ref[0].astype(BF16), preferred_element_type=F32) * cs_ref[0]
        o_ref[...] = y.astype(BF16)

    return _pc(body, name=name, grid=(4,),
               in_specs=[_bs((t_dim, BLK), lambda g: (0, C_B128 + g)), _bs((1, BLK, BLK), lambda g: (g, 0, 0)),
                         _bs((1, 1, BLK), lambda g: (g, 0, 0))],
               out_specs=_bs((t_dim, BLK), lambda g: (0, g)), out_shape=_sds((t_dim, BW), BF16),
               vmem=40 << 20)(z, cw, cs)


def _pool_bwd(z, dy, cw, cs, name):
    t_dim = z.shape[0]

    def body(z_ref, dy_ref, cw_ref, cs_ref, dx_ref, dcw_ref, dcs_ref):
        pooled, cnt, row, w, _ = _pool_core(z_ref[...], pl.program_id(0))
        p16 = pooled.astype(BF16)
        cw16 = cw_ref[0].astype(BF16)
        dyv = dy_ref[...].astype(F32)
        lin = jnp.dot(p16, cw16, preferred_element_type=F32)
        dcs_ref[0] = jnp.sum(dyv * lin, axis=0, keepdims=True)
        dys = (dyv * cs_ref[0]).astype(BF16)
        dcw_ref[0] = lax.dot_general(p16, dys, TN, preferred_element_type=F32)
        dpool = lax.dot_general(dys, cw16, NT, preferred_element_type=F32)
        sb = dpool / cnt
        for k in (1, 2, 4, 8):
            sh = jnp.where(row < t_dim - k, pltpu.roll(sb, t_dim - k, 0), 0.0)
            sb = jnp.where(k < w, sb + sh, sb)
        dx_ref[...] = (sb - dpool).astype(BF16)

    return _pc(body, name=name, grid=(4,),
               in_specs=[_bs((t_dim, BLK), lambda g: (0, C_B128 + g)), _bs((t_dim, BLK), lambda g: (0, g)),
                         _bs((1, BLK, BLK), lambda g: (g, 0, 0)), _bs((1, 1, BLK), lambda g: (g, 0, 0))],
               out_specs=[_bs((t_dim, BLK), lambda g: (0, g)), _bs((1, BLK, BLK), lambda g: (g, 0, 0)),
                          _bs((1, 1, BLK), lambda g: (g, 0, 0))],
               out_shape=[_sds((t_dim, BW), BF16), _sds((4, BLK, BLK), F32), _sds((4, 1, BLK), F32)],
               vmem=48 << 20)(z, dy, cw, cs)


def _gates_specs(tm):
    z_specs = [_bs((tm, GW), (lambda j, i, b=b: (i, G_B + 4 * b + j))) for b in range(3)]
    bg_specs = [_bs((1, GW), (lambda j, i, b=b: (0, 4 * b + j))) for b in range(3)]
    tile = _bs((tm, GW), lambda j, i: (i, j))
    return z_specs, bg_specs, tile


def _gates_fwd(z, bg, pa, pb, pc, name, tm=512):
    t_dim = z.shape[0]
    z_specs, bg_specs, tile = _gates_specs(tm)

    def body(za, zb, zc, ba, bb, bc, pa_ref, pb_ref, pc_ref, o_ref):
        o_ref[...] = (_sigmoid(za[...] + ba[...]) * pa_ref[...] + _sigmoid(zb[...] + bb[...]) * pb_ref[...]
                      + _sigmoid(zc[...] + bc[...]) * pc_ref[...]).astype(BF16)

    return _pc(body, name=name, grid=(D // GW, t_dim // tm), in_specs=z_specs + bg_specs + [tile] * 3,
               out_specs=tile, out_shape=_sds((t_dim, D), BF16))(z, z, z, bg, bg, bg, pa, pb, pc)


def _gates_bwd(z, bg, pa, pb, pc, dm, name, tm=512):
    t_dim = z.shape[0]
    z_specs, bg_specs, tile = _gates_specs(tm)
    bias_out = _bs((1, GW), lambda j, i: (0, j))

    def body(za, zb, zc, ba, bb, bc, pa_ref, pb_ref, pc_ref, dm_ref, dpa, dpb, dpc, dza, dzb, dzc, dba, dbb, dbc):
        dmv = dm_ref[...]
        first = pl.program_id(1) == 0
        for z_ref, b_ref, p_ref, dp_ref, dz_ref, db_ref in ((za, ba, pa_ref, dpa, dza, dba), (zb, bb, pb_ref, dpb, dzb, dbb),
                                                            (zc, bc, pc_ref, dpc, dzc, dbc)):
            gate = _sigmoid(z_ref[...] + b_ref[...])
            dp_ref[...] = (gate * dmv).astype(BF16)
            dz = dmv * p_ref[...] * (gate * (1.0 - gate))
            dz_ref[...] = dz.astype(BF16)

            @pl.when(first)
            def _():
                db_ref[...] = jnp.zeros_like(db_ref)

            db_ref[...] += jnp.sum(dz, axis=0, keepdims=True)

    return _pc(body, name=name, grid=(D // GW, t_dim // tm), in_specs=z_specs + bg_specs + [tile] * 4,
               out_specs=[tile] * 6 + [bias_out] * 3,
               out_shape=[_sds((t_dim, D), BF16)] * 6 + [_sds((1, D), F32)] * 3)(z, z, z, bg, bg, bg, pa, pb, pc, dm)


def _coords():
    return lax.axis_index("x"), lax.axis_index("y"), lax.axis_index("c")


def _peer(k, x, y, c):
    px = 1 - x if k & 4 else x
    py = 1 - y if k & 2 else y
    pc = 1 - c if k & 1 else c
    return (px, py, pc), 4 * px + 2 * py + pc


def _exchange_copies(kind, bufs, send, recv):
    x, y, c = _coords()
    me = 4 * x + 2 * y + c
    na = len(bufs) if kind == "gather" else len(bufs) // 2
    out = []
    for k in range(1, NDEV):
        to, peer = _peer(k, x, y, c)
        for a in range(na):
            sems = dict(send_sem=send.at[a * 7 + k - 1], recv_sem=recv.at[a * 7 + k - 1], device_id=to, device_id_type=MESH)
            if kind == "gather":
                mine = pltpu.make_async_remote_copy(src_ref=bufs[a].at[:, me], dst_ref=bufs[a].at[:, me], **sems)
                theirs = pltpu.make_async_remote_copy(src_ref=bufs[a].at[:, me], dst_ref=bufs[a].at[:, peer], **sems)
            else:
                src, land = bufs[a], bufs[na + a]
                mine = pltpu.make_async_remote_copy(src_ref=src.at[:, peer], dst_ref=land.at[me], **sems)
                theirs = pltpu.make_async_remote_copy(src_ref=src.at[:, peer], dst_ref=land.at[peer], **sems)
            out.append((mine, theirs))
    return out


_HBM = pl.BlockSpec(memory_space=pltpu.HBM)
_SEM = pl.BlockSpec(memory_space=pltpu.SEMAPHORE)
_EFFECT = pltpu.SideEffectType.DATAFLOW_SIDE_EFFECTING


def _exchange_start(kind, arrays, after, name):
    na = len(arrays)
    if kind == "scatter":
        arrays = list(arrays) + [lax.empty((NDEV, s.shape[0]) + s.shape[2:], s.dtype) for s in arrays]
    nb = len(arrays)

    def body(*refs):
        bufs = refs[:nb]
        send, recv = refs[nb + 1], refs[nb + 2]
        token = refs[-1]
        for mine, _ in _exchange_copies(kind, bufs, send, recv):
            mine.start()
        token[...] = jnp.zeros_like(token)

    arrays = [pltpu.with_memory_space_constraint(b, pltpu.HBM) for b in arrays]
    outs = pl.pallas_call(
        body, name=name,
        out_shape=(pltpu.SemaphoreType.DMA((7 * na,)), pltpu.SemaphoreType.DMA((7 * na,)),
                   *[pltpu.HBM(b.shape, b.dtype) for b in arrays], _sds((8, 128), F32)),
        in_specs=[_HBM] * nb + [pl.BlockSpec(memory_space=pl.ANY)],
        out_specs=(_SEM, _SEM, *([_HBM] * nb), pl.BlockSpec(memory_space=pltpu.VMEM)),
        input_output_aliases={i: 2 + i for i in range(nb)},
        compiler_params=pltpu.CompilerParams(has_side_effects=_EFFECT), interpret=False)(*arrays, after)
    return outs[0], outs[1], list(outs[2:2 + nb]), outs[-1]


def _exchange_wait(kind, started, after, name):
    send, recv, arrays, _ = started
    nb = len(arrays)

    def body(*refs):
        for mine, theirs in _exchange_copies(kind, refs[:nb], refs[nb], refs[nb + 1]):
            mine.wait_send()
            theirs.wait_recv()

    outs = pl.pallas_call(
        body, name=name,
        out_shape=tuple(pltpu.HBM(b.shape, b.dtype) for b in arrays),
        in_specs=[_HBM] * nb + [_SEM, _SEM, pl.BlockSpec(memory_space=pl.ANY)],
        out_specs=tuple([_HBM] * nb),
        input_output_aliases={i: i for i in range(nb)},
        compiler_params=pltpu.CompilerParams(has_side_effects=_EFFECT), interpret=False)(*arrays, send, recv, after)
    return list(outs)


def _adam_math(w, g, m, v):
    m2 = ADAM_B1 * m + (1.0 - ADAM_B1) * g
    v2 = ADAM_B2 * v + (1.0 - ADAM_B2) * jnp.square(g)
    m_hat = m2 / (1.0 - ADAM_B1 ** ADAM_STEP)
    v_hat = v2 / (1.0 - ADAM_B2 ** ADAM_STEP)
    return -ADAM_LR * (m_hat / (jnp.sqrt(v_hat) + ADAM_EPS) + ADAM_WD * w), m2, v2


def _reduce_update(landed, own, me, piece, w, m, v, layer, prev, transposed, name):
    rows, cols = landed.shape[2], landed.shape[3]
    tc = 256
    assert cols % tc == 0 and w.shape[1:] == ((cols, rows) if transposed else (rows, cols))
    lane_chunks = [(c0, min(128, rows - c0)) for c0 in range(0, rows, 128)]

    def body(me_ref, b_ref, own_ref, w_ref, m_ref, v_ref, *rest):
        outs = rest[-4:]
        mine = me_ref[0]
        s = None
        for d in range(NDEV):
            other = jnp.where(mine == d, (d + 1) % NDEV, d)
            part = jnp.where(mine == d, own_ref[0, 0], b_ref[other, 0]).astype(F32)
            s = part if s is None else s + part
        if not transposed:
            for o, val in zip(outs, (s,) + _adam_math(w_ref[0], s, m_ref[0], v_ref[0])):
                o[0] = val
            return
        for c0, wd in lane_chunks:
            chunk = s[c0:c0 + wd]
            if wd < 128:
                chunk = jnp.concatenate([chunk, jnp.zeros((128 - wd, tc), F32)], axis=0)
            g = chunk.T[:, :wd]
            sl = (0, slice(None), slice(c0, c0 + wd))
            for o, val in zip(outs, (g,) + _adam_math(w_ref[sl], g, m_ref[sl], v_ref[sl])):
                o[sl] = val

    if transposed:
        nat = _bs((1, tc, rows), lambda j, me_ref: (layer, j, 0))
    else:
        nat = _bs((1, rows, tc), lambda j, me_ref: (layer, 0, j))
    in_specs = [_bs((NDEV, 1, rows, tc), lambda j, me_ref: (0, piece, 0, j)),
                _bs((1, 1, rows, tc), lambda j, me_ref: (piece, me_ref[0], 0, j)), nat, nat, nat]
    args = [me, landed, own, w, m, v]
    alias = {}
    if prev is not None:
        in_specs += [pl.BlockSpec(memory_space=pl.ANY)] * 4
        args += list(prev)
        alias = {6 + i: i for i in range(4)}
    grid_spec = pltpu.PrefetchScalarGridSpec(num_scalar_prefetch=1, grid=(cols // tc,), in_specs=in_specs, out_specs=[nat] * 4)
    return pl.pallas_call(body, name=name, grid_spec=grid_spec, out_shape=[_sds(w.shape, F32)] * 4,
                          input_output_aliases=alias, interpret=False)(*args)


SMALL_LAYER = ("ffn1_norm", "mix_norm", "b_gate", "b_ln_g", "b_ln_b", "b_w_s", "b_b_s", "c_w", "c_scale", "ffn2_norm")
SMALL_ROWS = {"ffn1_norm": 8, "mix_norm": 8, "b_gate": 24, "b_ln_g": 4, "b_ln_b": 4, "b_w_s": 512, "b_b_s": 4, "c_w": 512,
              "c_scale": 4, "ffn2_norm": 8, "final_norm": 8}
LAYER_ROWS = sum(SMALL_ROWS[k] for k in SMALL_LAYER)
PACK_ROWS = DEPTH * LAYER_ROWS + SMALL_ROWS["final_norm"]


def _small_offset(l, name):
    if name == "final_norm":
        return DEPTH * LAYER_ROWS
    return l * LAYER_ROWS + sum(SMALL_ROWS[k] for k in SMALL_LAYER[:SMALL_LAYER.index(name)])


def _pack_small_grads(per_layer, final, me, name):
    flat, where = [], []
    for l in range(DEPTH):
        for k in SMALL_LAYER:
            parts = per_layer[l][k] if k == "b_gate" else [per_layer[l][k]]
            for i, part in enumerate(parts):
                flat.append(part)
                where.append(_small_offset(l, k) + i * (D // 128))
    flat.append(final)
    where.append(_small_offset(0, "final_norm"))

    def body(me_ref, *refs):
        del me_ref
        o_ref = refs[-1]
        for ref, off in zip(refs[:-1], where):
            if len(ref.shape) == 2:
                for j in range(ref.shape[1] // 128):
                    o_ref[0, 0, off + j:off + j + 1, :] = ref[:, 128 * j:128 * (j + 1)]
            else:
                rows = ref.shape[1]
                for g in range(4):
                    o_ref[0, 0, off + g * rows:off + (g + 1) * rows, :] = ref[g]

    grid_spec = pltpu.PrefetchScalarGridSpec(
        num_scalar_prefetch=1, grid=(1,),
        in_specs=[pl.BlockSpec(a.shape, (lambda i, me_ref, nd=a.ndim: (0,) * nd)) for a in flat],
        out_specs=_bs((1, 1, PACK_ROWS, 128), lambda i, me_ref: (0, me_ref[0], 0, 0)))
    return pl.pallas_call(body, name=name, grid_spec=grid_spec, out_shape=_sds((1, NDEV, PACK_ROWS, 128), F32),
                          interpret=False)(me, *flat)


def _adamw_small(gathered, w, mom, var, name):
    names = SMALL_LAYER + ("final_norm",)
    ins = [gathered] + [t[k] for k in names for t in (w, mom, var)]

    def body(*refs):
        land = refs[0]
        outs = refs[len(ins):]

        def rows(off, n):
            s = land[0, 0, off:off + n, :]
            for d in range(1, NDEV):
                s = s + land[0, d, off:off + n, :]
            return s

        def as_row(off, n):
            s = rows(off, n)
            return jnp.concatenate([s[j:j + 1, :] for j in range(n)], axis=1)

        def update(i, idx, g):
            wr, mr, vr = (refs[1 + 3 * i + t] for t in range(3))
            d, m2, v2 = _adam_math(wr[idx], g, mr[idx], vr[idx])
            for o, val in zip(outs[4 * i:4 * i + 4], (g, d, m2, v2)):
                o[idx] = val

        for i, k in enumerate(names):
            n = SMALL_ROWS[k]
            for l in range(1 if k == "final_norm" else DEPTH):
                off = _small_offset(l, k)
                if k in ("b_w_s", "c_w"):
                    for g in range(4):
                        update(i, (l, g), rows(off + g * BLK, BLK))
                elif k in ("b_b_s", "c_scale"):
                    update(i, (l,), rows(off, n))
                else:
                    update(i, (slice(l, l + 1),), as_row(off, n))

    out_shape = [_sds(w[k].shape, F32) for k in names for _ in range(4)]
    outs = pl.pallas_call(body, name=name, out_shape=out_shape,
                          compiler_params=pltpu.CompilerParams(vmem_limit_bytes=48 << 20), interpret=False)(*ins)
    return tuple({k: outs[4 * i + t] for i, k in enumerate(names)} for t in range(4))


PG, PD = 0, 2


def _ffn_fwd(x, gain, wf, tag):
    h = _rms_fwd(x, gain, f"rms_{tag}")
    gu = _mm(h, wf, mode="nt", tm=2048, tn=512, tk=D, n=2 * FF, b_row0=PG * FF, out_dtype=BF16, name=f"gu_{tag}")
    act = _swiglu_fwd(gu, f"swiglu_{tag}")
    out = _mm(act, wf, mode="nn", tm=512, tn=D, tk=FF, n=D, b_row0=PD * FF, out_dtype=F32, scale=0.5, res=x, name=f"down_{tag}")
    return out, (x, h, gu, act)


def _ffn_bwd(dout, saved, gain, wf, tag, after, emit):
    x, h, gu, act = saved
    grf = lax.empty((3, NDEV, FFS, D), BF16)
    da = _mm(dout, wf, mode="nt", tm=512, tn=FF, tk=D, n=FF, b_row0=PD * FF, out_dtype=BF16, scale=0.5, after=after,
             name=f"dact_{tag}")
    grf = _wgrad(act, dout, grf, piece0=PD, rows=FFS, tm=FF // 2, tn=D, tk=1024, scale=0.5, name=f"wg_down_{tag}")
    dgu = _swiglu_bwd(gu, da, f"dswiglu_{tag}")
    grf = _wgrad(dgu, h, grf, piece0=PG, rows=FFS, tm=FF // 2, tn=D, tk=2048, name=f"wg_gu_{tag}")
    token = emit([grf])
    dh = _mm(dgu, wf, mode="nn", tm=256, tn=D, tk=2 * FF, n=D, b_row0=PG * FF, out_dtype=F32, after=token, name=f"dh_{tag}")
    dx, dgain = _rms_bwd(x, gain, dh, dout, f"drms_{tag}")
    return dx, dgain, token


def _mixer_fwd(x, sp, ww, wo, wa, wbc, tag):
    h = _rms_fwd(x, sp["mix_norm"], f"rms_mix_{tag}")
    z = _mm(h, ww, mode="nt", tm=1024, tn=1152, tk=D, n=INW, out_dtype=F32, name=f"z_{tag}")
    acc, ml = _attn_fwd(z, None, None, gi=0, first=True, final=False, name=f"attn0_{tag}")
    acc, ml = _attn_fwd(z, acc, ml, gi=1, first=False, final=False, name=f"attn1_{tag}")
    ya, lse = _attn_fwd(z, acc, ml, gi=2, first=False, final=True, name=f"attn2_{tag}")
    yb = _gmlp_fwd(z, sp["b_ln_g"], sp["b_ln_b"], sp["b_w_s"], sp["b_b_s"], f"gmlp_{tag}")
    yc = _pool_fwd(z, sp["c_w"], sp["c_scale"], f"pool_{tag}")
    pa = _mm(ya, wa, mode="nt", tm=1024, tn=D, tk=GW, n=D, out_dtype=F32, name=f"proj_a_{tag}")
    pb = _mm(yb, wbc, mode="nt", tm=1024, tn=D, tk=BW, n=D, b_row0=0, out_dtype=F32, name=f"proj_b_{tag}")
    pc = _mm(yc, wbc, mode="nt", tm=1024, tn=D, tk=BW, n=D, b_row0=D, out_dtype=F32, name=f"proj_c_{tag}")
    merged = _gates_fwd(z, sp["b_gate"], pa, pb, pc, f"gates_{tag}")
    out = _mm(merged, wo, mode="nn", tm=1024, tn=D, tk=D, n=D, out_dtype=F32, res=x, name=f"out_{tag}")
    return out, (x, h, z, ya, lse, yb, yc, pa, pb, pc, merged)


def _mixer_bwd(dout, saved, sp, ww, wo, wa, wbc, tag, after, emit):
    x, h, z, ya, lse, yb, yc, pa, pb, pc, merged = saved
    grw = lax.empty((1, NDEV, INS, D), BF16)
    gro = lax.empty((1, NDEV, DS, D), BF16)
    gra = lax.empty((1, NDEV, DS, GW), BF16)
    grbc = lax.empty((2, NDEV, DS, BW), BF16)
    dm = _mm(dout, wo, mode="nt", tm=1024, tn=D, tk=D, n=D, out_dtype=F32, after=after, name=f"dmerged_{tag}")
    gro = _wgrad(merged, dout, gro, piece0=0, rows=DS, tm=D, tn=D, tk=1024, name=f"wg_out_{tag}")
    dpa, dpb, dpc, dza, dzb, dzc, dba, dbb, dbc = _gates_bwd(z, sp["b_gate"], pa, pb, pc, dm, f"dgates_{tag}")
    dya = _mm(dpa, wa, mode="nn", tm=2048, tn=GW, tk=D, n=GW, out_dtype=F32, name=f"dya_{tag}")
    dyb = _mm(dpb, wbc, mode="nn", tm=2048, tn=BW, tk=D, n=BW, b_row0=0, out_dtype=F32, name=f"dyb_{tag}")
    dyc = _mm(dpc, wbc, mode="nn", tm=2048, tn=BW, tk=D, n=BW, b_row0=D, out_dtype=F32, name=f"dyc_{tag}")
    gra = _wgrad(dpa, ya, gra, piece0=0, rows=DS, tm=D, tn=GW, tk=1024, name=f"wg_pa_{tag}")
    grbc = _wgrad(dpb, yb, grbc, piece0=0, rows=DS, tm=D, tn=BW, tk=1024, name=f"wg_pb_{tag}")
    grbc = _wgrad(dpc, yc, grbc, piece0=1, rows=DS, tm=D, tn=BW, tk=1024, name=f"wg_pc_{tag}")
    dqkv = [_attn_bwd(z, dya, ya, lse, gi=gi, name=f"dattn{gi}_{tag}") for gi in range(3)]
    duv, dws, dbs, dlng, dlnb = _gmlp_bwd(z, dyb, sp["b_ln_g"], sp["b_ln_b"], sp["b_w_s"], sp["b_b_s"], f"dgmlp_{tag}")
    dxc, dcw, dcs = _pool_bwd(z, dyc, sp["c_w"], sp["c_scale"], f"dpool_{tag}")
    dz = jnp.concatenate([dqkv[gi][part].astype(BF16) for part in range(3) for gi in range(3)] + [duv, dxc, dza, dzb, dzc],
                         axis=1)
    grw = _wgrad(dz, h, grw, piece0=0, rows=INS, tm=INW // 2, tn=512, tk=1024, name=f"wg_in_{tag}")
    token = emit([grw, gro, gra, grbc])
    dh = _mm(dz, ww, mode="nn", tm=256, tn=D, tk=INW, n=D, out_dtype=F32, after=token, name=f"dh_mix_{tag}")
    dx, dgain = _rms_bwd(x, sp["mix_norm"], dh, dout, f"drms_mix_{tag}")
    small = {"mix_norm": dgain, "b_gate": [dba, dbb, dbc], "b_ln_g": dlng, "b_ln_b": dlnb,
             "b_w_s": dws, "b_b_s": dbs, "c_w": dcw, "c_scale": dcs}
    return dx, small, token


PARTS = ("f1", "mix", "f2")


def _forward(xs, tgt, weights_of, small, final_gain):
    saved, views = [], []
    cur = xs
    for l in range(DEPTH):
        (wf1,) = weights_of(l, "f1", cur)
        cur, s1 = _ffn_fwd(cur, small[l]["ffn1_norm"], wf1, f"f1l{l}")
        wmix = weights_of(l, "mix", cur)
        cur, s2 = _mixer_fwd(cur, small[l], *wmix, f"l{l}")
        (wf2,) = weights_of(l, "f2", cur)
        cur, s3 = _ffn_fwd(cur, small[l]["ffn2_norm"], wf2, f"f2l{l}")
        saved.append((s1, s2, s3))
        views.append((wf1, wmix, wf2))
    loss_part, dcur, dfinal = _final_loss(cur, final_gain, tgt, "final_loss")
    return loss_part, dcur, dfinal, saved, views


def _backward_layer(dcur, saved_l, small_l, views_l, l, emit, after):
    wf1, wmix, wf2 = views_l
    s1, s2, s3 = saved_l
    dcur, dn2, token = _ffn_bwd(dcur, s3, small_l["ffn2_norm"], wf2, f"f2l{l}", after, lambda g: emit("f2", g))
    dcur, sg, token = _mixer_bwd(dcur, s2, small_l, *wmix, f"l{l}", token, lambda g: emit("mix", g))
    dcur, dn1, token = _ffn_bwd(dcur, s1, small_l["ffn1_norm"], wf1, f"f1l{l}", token, lambda g: emit("f1", g))
    sg["ffn1_norm"] = dn1
    sg["ffn2_norm"] = dn2
    return dcur, sg, token


BIG = ("ffn1_w_gate", "ffn1_w_up", "ffn1_w_down", "w_in", "w_proj_a", "w_proj_b", "w_proj_c", "w_out",
       "ffn2_w_gate", "ffn2_w_up", "ffn2_w_down")
ALL_WEIGHTS = ("ffn1_norm", "ffn1_w_gate", "ffn1_w_up", "ffn1_w_down", "mix_norm", "w_in", "b_gate", "b_ln_g", "b_ln_b",
               "b_w_s", "b_b_s", "c_w", "c_scale", "w_proj_a", "w_proj_b", "w_proj_c", "w_out", "ffn2_norm", "ffn2_w_gate",
               "ffn2_w_up", "ffn2_w_down", "final_norm")


def _part_shards(w, l, part):
    t16 = lambda a: a.T.astype(BF16)
    if part == "mix":
        return [t16(w["w_in"][l])[None], w["w_out"][l].astype(BF16)[None], t16(w["w_proj_a"][l])[None],
                jnp.stack([t16(w["w_proj_b"][l]), t16(w["w_proj_c"][l])])]
    f = "ffn1" if part == "f1" else "ffn2"
    return [jnp.stack([t16(w[f + "_w_gate"][l]), t16(w[f + "_w_up"][l]), w[f + "_w_down"][l].astype(BF16)])]


def _part_views(part, gathered):
    if part == "mix":
        pw, po, pa, pbc = gathered
        return [pw.reshape(INW, D), po.reshape(D, D), pa.reshape(D, GW), pbc.reshape(2 * D, BW)]
    return [gathered[0].reshape(3 * FF, D)]


def _part_pieces(part):
    if part == "mix":
        return [(0, 0, "w_in", True), (1, 0, "w_out", False), (2, 0, "w_proj_a", True), (3, 0, "w_proj_b", True),
                (3, 1, "w_proj_c", True)]
    f = "ffn1" if part == "f1" else "ffn2"
    return [(0, 0, f + "_w_gate", True), (0, 1, f + "_w_up", True), (0, 2, f + "_w_down", False)]


def _layer_small(w, l):
    r = lambda a, shape: a.reshape(shape)
    return {"ffn1_norm": r(w["ffn1_norm"][l], (1, D)), "mix_norm": r(w["mix_norm"][l], (1, D)),
            "b_gate": r(w["b_gate"][l], (1, 3 * D)), "b_ln_g": r(w["b_ln_g"][l], (1, BW)), "b_ln_b": r(w["b_ln_b"][l], (1, BW)),
            "b_w_s": w["b_w_s"][l], "b_b_s": r(w["b_b_s"][l], (4, BLK, 1)), "c_w": w["c_w"][l],
            "c_scale": r(w["c_scale"][l], (4, 1, BLK)), "ffn2_norm": r(w["ffn2_norm"][l], (1, D))}


def kernel(x, ffn1_norm, ffn1_w_gate, ffn1_w_up, ffn1_w_down, mix_norm, w_in, b_gate, b_ln_g, b_ln_b, b_w_s, b_b_s, c_w, c_scale, w_proj_a, w_proj_b, w_proj_c, w_out, ffn2_norm, ffn2_w_gate, ffn2_w_up, ffn2_w_down, final_norm, loss_target, m_ffn1_norm, m_ffn1_w_gate, m_ffn1_w_up, m_ffn1_w_down, m_mix_norm, m_w_in, m_b_gate, m_b_ln_g, m_b_ln_b, m_b_w_s, m_b_b_s, m_c_w, m_c_scale, m_w_proj_a, m_w_proj_b, m_w_proj_c, m_w_out, m_ffn2_norm, m_ffn2_w_gate, m_ffn2_w_up, m_ffn2_w_down, m_final_norm, v_ffn1_norm, v_ffn1_w_gate, v_ffn1_w_up, v_ffn1_w_down, v_mix_norm, v_w_in, v_b_gate, v_b_ln_g, v_b_ln_b, v_b_w_s, v_b_b_s, v_c_w, v_c_scale, v_w_proj_a, v_w_proj_b, v_w_proj_c, v_w_out, v_ffn2_norm, v_ffn2_w_gate, v_ffn2_w_up, v_ffn2_w_down, v_final_norm):
    w = dict(ffn1_norm=ffn1_norm, ffn1_w_gate=ffn1_w_gate, ffn1_w_up=ffn1_w_up, ffn1_w_down=ffn1_w_down, mix_norm=mix_norm,
             w_in=w_in, b_gate=b_gate, b_ln_g=b_ln_g, b_ln_b=b_ln_b, b_w_s=b_w_s, b_b_s=b_b_s, c_w=c_w, c_scale=c_scale,
             w_proj_a=w_proj_a, w_proj_b=w_proj_b, w_proj_c=w_proj_c, w_out=w_out, ffn2_norm=ffn2_norm, ffn2_w_gate=ffn2_w_gate,
             ffn2_w_up=ffn2_w_up, ffn2_w_down=ffn2_w_down, final_norm=final_norm)
    mom = dict(ffn1_norm=m_ffn1_norm, ffn1_w_gate=m_ffn1_w_gate, ffn1_w_up=m_ffn1_w_up, ffn1_w_down=m_ffn1_w_down,
               mix_norm=m_mix_norm, w_in=m_w_in, b_gate=m_b_gate, b_ln_g=m_b_ln_g, b_ln_b=m_b_ln_b, b_w_s=m_b_w_s, b_b_s=m_b_b_s,
               c_w=m_c_w, c_scale=m_c_scale, w_proj_a=m_w_proj_a, w_proj_b=m_w_proj_b, w_proj_c=m_w_proj_c, w_out=m_w_out,
               ffn2_norm=m_ffn2_norm, ffn2_w_gate=m_ffn2_w_gate, ffn2_w_up=m_ffn2_w_up, ffn2_w_down=m_ffn2_w_down,
               final_norm=m_final_norm)
    var = dict(ffn1_norm=v_ffn1_norm, ffn1_w_gate=v_ffn1_w_gate, ffn1_w_up=v_ffn1_w_up, ffn1_w_down=v_ffn1_w_down,
               mix_norm=v_mix_norm, w_in=v_w_in, b_gate=v_b_gate, b_ln_g=v_b_ln_g, b_ln_b=v_b_ln_b, b_w_s=v_b_w_s, b_b_s=v_b_b_s,
               c_w=v_c_w, c_scale=v_c_scale, w_proj_a=v_w_proj_a, w_proj_b=v_w_proj_b, w_proj_c=v_w_proj_c, w_out=v_w_out,
               ffn2_norm=v_ffn2_norm, ffn2_w_gate=v_ffn2_w_gate, ffn2_w_up=v_ffn2_w_up, ffn2_w_down=v_ffn2_w_down,
               final_norm=v_final_norm)

    t_dim = x.shape[1]
    xs = x.reshape(t_dim, D)
    tgt = loss_target.reshape(t_dim, D)

    me = 4 * lax.axis_index("x") + 2 * lax.axis_index("y") + lax.axis_index("c")
    gathers = {}
    token = jnp.zeros((8, 128), F32)
    for l in range(DEPTH):
        for part in PARTS:
            lands = [lax.dynamic_update_slice(lax.empty((s.shape[0], NDEV) + s.shape[1:], BF16), s[:, None], (0, me, 0, 0))
                     for s in _part_shards(w, l, part)]
            gathers[l, part] = _exchange_start("gather", lands, token, f"gather_start_{part}_l{l}")
            token = gathers[l, part][-1]

    def weights_of(l, part, after):
        behind = token if (l, part) == (0, "f1") else after
        return _part_views(part, _exchange_wait("gather", gathers[l, part], behind, f"gather_wait_{part}_l{l}"))

    small = [_layer_small(w, l) for l in range(DEPTH)]
    loss_part, dcur, dfinal, saved, views = _forward(xs, tgt, weights_of, small, w["final_norm"].reshape(1, D))
    loss = lax.psum(loss_part[0, 0], ("x", "y", "c"))

    small_grads = [None] * DEPTH
    scatters = []
    token = None
    for l in reversed(range(DEPTH)):
        def emit(part, grads, l=l):
            scatters.append((l, part, _exchange_start("scatter", grads, grads[0], f"scatter_start_{part}_l{l}")))
            return scatters[-1][2][-1]
        dcur, small_grads[l], token = _backward_layer(dcur, saved[l], small[l], views[l], l, emit, token)
    grad_x = dcur.reshape(x.shape)

    me_arr = me.astype(jnp.int32).reshape(1)
    packed = _pack_small_grads(small_grads, dfinal, me_arr, "pack_small_grads")
    small_gather = _exchange_start("gather", [packed], token, "gather_start_small")

    done = {}
    behind = small_gather[-1]
    for l, part, started in scatters:
        arrs = _exchange_wait("scatter", started, behind, f"scatter_wait_{part}_l{l}")
        na = len(arrs) // 2
        for a, piece, k, transposed in _part_pieces(part):
            done[k] = _reduce_update(arrs[na + a], arrs[a], me_arr, piece, w[k], mom[k], var[k], l, done.get(k), transposed,
                                     f"update_{k}_l{l}")
            behind = done[k][1]
    grads, delta, new_m, new_v = ({k: done[k][t] for k in BIG} for t in range(4))
    gathered = _exchange_wait("gather", small_gather, behind, "gather_wait_small")[0]
    as_row = lambda t: dict(t, final_norm=t["final_norm"].reshape(1, D))
    for dst, src in zip((grads, delta, new_m, new_v), _adamw_small(gathered, as_row(w), as_row(mom), as_row(var), "adamw_small")):
        dst.update(src)
        dst["final_norm"] = src["final_norm"].reshape(D)

    return (loss, grad_x, *[grads[k] for k in ALL_WEIGHTS], *[delta[k] for k in ALL_WEIGHTS],
            *[new_m[k] for k in ALL_WEIGHTS], *[new_v[k] for k in ALL_WEIGHTS])
```

```python
import math

import jax
import jax.numpy as jnp
from jax import lax
from jax.experimental import pallas as pl
from jax.experimental.pallas import tpu as pltpu

F32 = jnp.float32
BF16 = jnp.bfloat16
MESH = pl.DeviceIdType.MESH

NDEV = 8
DEPTH = 2
D = 1024
FF = 2816
FFS = FF // NDEV
INW = 6912
INS = INW // NDEV
DS = D // NDEV
BLK = 128
NH = 4
HD = 64
GW = NH * HD
DILS = (1, 4, 16)
QK_SCALE = 1.0 / math.sqrt(HD)
ZB = INW // GW
Q_B, K_B, V_B = 0, 3, 6
UV_B = 9
C_B128 = 26
G_B = 15
BW = 512
EPS = 1e-6
NEG = -1e30
VMEM_CAP = 60 * 1024 * 1024

ADAM_LR, ADAM_B1, ADAM_B2, ADAM_EPS, ADAM_WD, ADAM_STEP = 0.001, 0.9, 0.999, 1e-08, 0.01, 10

NT = (((1,), (1,)), ((), ()))
TN = (((0,), (0,)), ((), ()))


def _bs(shape, imap):
    return pl.BlockSpec(shape, imap)


def _pc(body, *, name, grid, in_specs, out_specs, out_shape, scratch=(), alias=None, vmem=None):
    params = {}
    if vmem is not None:
        params["vmem_limit_bytes"] = min(int(vmem), VMEM_CAP)
    big = lambda s: math.prod(s.shape) * jnp.dtype(s.dtype).itemsize >= (1 << 20)
    pin = lambda s: pltpu.HBM(s.shape, s.dtype) if big(s) else s
    out_shape = [pin(s) for s in out_shape] if isinstance(out_shape, (list, tuple)) else pin(out_shape)
    call = pl.pallas_call(
        body, name=name, grid=grid, in_specs=in_specs, out_specs=out_specs, out_shape=out_shape,
        scratch_shapes=list(scratch), input_output_aliases=alias or {},
        compiler_params=pltpu.CompilerParams(**params), interpret=False)
    return lambda *args: call(*[pltpu.with_memory_space_constraint(a, pltpu.HBM) if big(a) else a for a in args])


def _sds(shape, dtype):
    return jax.ShapeDtypeStruct(shape, dtype)


def _row_tile(rows, align, cap):
    best = None
    for cand in range(align, min(rows, cap) + 1, align):
        if rows % cand == 0:
            best = cand
    assert best is not None, (rows, align, cap)
    return best


def _mm(a, b, *, mode, tm, tn, tk, n, out_dtype, name, b_row0=0, scale=None, res=None, after=None):
    m, k_dim = a.shape
    nk = k_dim // tk
    assert m % tm == 0 and n % tn == 0 and k_dim % tk == 0
    b_off, rem = divmod(b_row0, tk if mode == "nn" else tn)
    assert rem == 0
    grid = (m // tm, n // tn, nk)
    in_specs = [_bs((tm, tk), lambda i, j, k: (i, k))]
    if mode == "nn":
        in_specs.append(_bs((tk, tn), lambda i, j, k: (b_off + k, j)))
    else:
        in_specs.append(_bs((tn, tk), lambda i, j, k: (b_off + j, k)))
    args = [a, b]
    if res is not None:
        in_specs.append(_bs((tm, tn), lambda i, j, k: (i, j)))
        args.append(res)
    if after is not None:
        in_specs.append(pl.BlockSpec(memory_space=pl.ANY))
        args.append(after)
    n_in = len(args)

    def body(*refs):
        a_ref, b_ref = refs[0], refs[1]
        r_ref = refs[2] if res is not None else None
        o_ref = refs[n_in]
        av = a_ref[...].astype(BF16)
        bv = b_ref[...].astype(BF16)
        if mode == "nn":
            p = jnp.dot(av, bv, preferred_element_type=F32)
        else:
            p = lax.dot_general(av, bv, NT, preferred_element_type=F32)

        def fin(v):
            if scale is not None:
                v = v * scale
            if r_ref is not None:
                v = r_ref[...] + v
            o_ref[...] = v.astype(out_dtype)

        if nk == 1:
            fin(p)
        else:
            acc_ref = refs[-1]
            kk = pl.program_id(2)

            @pl.when(kk == 0)
            def _():
                acc_ref[...] = p

            @pl.when(kk > 0)
            def _():
                acc_ref[...] += p

            @pl.when(kk == nk - 1)
            def _():
                fin(acc_ref[...])

    scratch = [pltpu.VMEM((tm, tn), F32)] if nk > 1 else []
    est = 2 * (tm * tk * a.dtype.itemsize + tk * tn * b.dtype.itemsize + tm * tn * jnp.dtype(out_dtype).itemsize)
    est += (2 + (nk > 1)) * tm * tn * 4 + (2 * tm * tn * 4 if res is not None else 0)
    est += (tm * tk * 2 if a.dtype != BF16 else 0) + (tk * tn * 2 if b.dtype != BF16 else 0)
    return _pc(body, name=name, grid=grid, in_specs=in_specs, out_specs=_bs((tm, tn), lambda i, j, k: (i, j)),
               out_shape=_sds((m, n), out_dtype), scratch=scratch, vmem=est + (8 << 20))(*args)


def _wgrad(a, b, buf, *, piece0, rows, tm, tn, tk, name, scale=None):
    t_dim, m = a.shape
    n = b.shape[1]
    db = tm // rows
    tpp = NDEV // db
    nk = t_dim // tk
    assert tm % rows == 0 and NDEV % db == 0 and m % tm == 0 and n % tn == 0 and t_dim % tk == 0
    grid = (m // tm, n // tn, nk)

    def body(a_ref, b_ref, buf_ref, o_ref, acc_ref):
        del buf_ref
        kk = pl.program_id(2)
        p = lax.dot_general(a_ref[...].astype(BF16), b_ref[...].astype(BF16), TN, preferred_element_type=F32)

        @pl.when(kk == 0)
        def _():
            acc_ref[...] = p

        @pl.when(kk > 0)
        def _():
            acc_ref[...] += p

        @pl.when(kk == nk - 1)
        def _():
            for d in range(db):
                v = acc_ref[d * rows:(d + 1) * rows, :]
                if scale is not None:
                    v = v * scale
                o_ref[0, d] = v.astype(BF16)

    est = 2 * (tk * tm * a.dtype.itemsize + tk * tn * b.dtype.itemsize + tm * tn * 2) + 3 * tm * tn * 4 + tk * tm * 4
    return _pc(
        body, name=name, grid=grid,
        in_specs=[_bs((tk, tm), lambda i, j, k: (k, i)), _bs((tk, tn), lambda i, j, k: (k, j)),
                  pl.BlockSpec(memory_space=pl.ANY)],
        out_specs=_bs((1, db, rows, tn), lambda i, j, k: (piece0 + i // tpp, i % tpp, 0, j)),
        out_shape=_sds(buf.shape, buf.dtype), scratch=[pltpu.VMEM((tm, tn), F32)], alias={2: 0},
        vmem=est + (8 << 20))(a, b, buf)


def _mm_drms(a, b, x, gain, dres, *, tm, b_row0, after, name):
    m, k_dim = a.shape
    assert m % tm == 0 and b_row0 % k_dim == 0 and b.shape[1] == D
    k_blk = b_row0 // k_dim
    after = jnp.zeros((8, 128), F32) if after is None else after

    def body(a_ref, b_ref, x_ref, g_ref, dr_ref, after_ref, dx_ref, dg_ref):
        del after_ref
        dh_v = jnp.dot(a_ref[...].astype(BF16), b_ref[...].astype(BF16), preferred_element_type=F32)
        xv = x_ref[...]
        r = lax.rsqrt(jnp.mean(xv * xv, axis=-1, keepdims=True) + EPS)
        dyg = dh_v * g_ref[...]
        mq = jnp.mean(dyg * xv, axis=-1, keepdims=True)
        dx_ref[...] = dr_ref[...] + r * (dyg - xv * (r * r * mq))

        @pl.when(pl.program_id(0) == 0)
        def _():
            dg_ref[...] = jnp.zeros_like(dg_ref)

        dg_ref[...] += jnp.sum(dh_v * (xv * r), axis=0, keepdims=True)

    row = _bs((tm, D), lambda i: (i, 0))
    est = 2 * (tm * k_dim * a.dtype.itemsize + k_dim * D * b.dtype.itemsize + 3 * tm * D * 4) + 4 * tm * D * 4
    return _pc(body, name=name, grid=(m // tm,),
               in_specs=[_bs((tm, k_dim), lambda i: (i, 0)), _bs((k_dim, D), lambda i: (k_blk, 0)), row,
                         _bs((1, D), lambda i: (0, 0)), row, pl.BlockSpec(memory_space=pl.ANY)],
               out_specs=[row, _bs((1, D), lambda i: (0, 0))], out_shape=[_sds((m, D), F32), _sds((1, D), F32)],
               vmem=est + (8 << 20))(a, b, x, gain, dres, after)


def _dact_dswiglu(dout, b, gu, *, tm, b_row0, after, name):
    t_dim = dout.shape[0]
    assert t_dim % tm == 0 and b_row0 % FF == 0
    k_blk = b_row0 // FF
    after = jnp.zeros((8, 128), F32) if after is None else after

    def body(d_ref, b_ref, gu_ref, after_ref, o_ref):
        del after_ref
        dav = 0.5 * lax.dot_general(d_ref[...].astype(BF16), b_ref[...].astype(BF16), NT, preferred_element_type=F32)
        g = gu_ref[:, :FF].astype(F32)
        u = gu_ref[:, FF:].astype(F32)
        s = _sigmoid(g)
        o_ref[:, :FF] = (dav * u * (s * (1.0 + g * (1.0 - s)))).astype(BF16)
        o_ref[:, FF:] = (dav * (g * s)).astype(BF16)

    est = 2 * (tm * D * 4 + FF * D * 2 + 2 * tm * 2 * FF * 2) + 5 * tm * FF * 4
    return _pc(body, name=name, grid=(t_dim // tm,),
               in_specs=[_bs((tm, D), lambda i: (i, 0)), _bs((FF, D), lambda i: (k_blk, 0)),
                         _bs((tm, 2 * FF), lambda i: (i, 0)), pl.BlockSpec(memory_space=pl.ANY)],
               out_specs=_bs((tm, 2 * FF), lambda i: (i, 0)), out_shape=_sds((t_dim, 2 * FF), BF16),
               vmem=est + (8 << 20))(dout, b, gu, after)


def _rms_fwd(x, gain, name, tm=512):
    t_dim = x.shape[0]

    def body(x_ref, g_ref, o_ref):
        xv = x_ref[...]
        r = lax.rsqrt(jnp.mean(xv * xv, axis=-1, keepdims=True) + EPS)
        o_ref[...] = (xv * r * g_ref[...]).astype(BF16)

    return _pc(body, name=name, grid=(t_dim // tm,),
               in_specs=[_bs((tm, D), lambda i: (i, 0)), _bs((1, D), lambda i: (0, 0))],
               out_specs=_bs((tm, D), lambda i: (i, 0)), out_shape=_sds((t_dim, D), BF16))(x, gain)


def _sigmoid(v):
    return 1.0 / (1.0 + jnp.exp(-v))


def _swiglu_fwd(gu, name, tm=256):
    t_dim = gu.shape[0]

    def body(gu_ref, o_ref):
        g = gu_ref[:, :FF].astype(F32)
        u = gu_ref[:, FF:].astype(F32)
        o_ref[...] = (g * _sigmoid(g) * u).astype(BF16)

    return _pc(body, name=name, grid=(t_dim // tm,), in_specs=[_bs((tm, 2 * FF), lambda i: (i, 0))],
               out_specs=_bs((tm, FF), lambda i: (i, 0)), out_shape=_sds((t_dim, FF), BF16))(gu)


def _final_loss(x, gain, tgt, name, tm=512):
    t_dim = x.shape[0]

    def body(x_ref, g_ref, t_ref, loss_ref, dx_ref, dg_ref):
        xv = x_ref[...]
        gv = g_ref[...]
        r = lax.rsqrt(jnp.mean(xv * xv, axis=-1, keepdims=True) + EPS)
        xn = xv * r
        err = xn * gv - t_ref[...]
        dy = err * (1.0 / D)
        dyg = dy * gv
        mq = jnp.mean(dyg * xv, axis=-1, keepdims=True)
        dx_ref[...] = r * (dyg - xv * (r * r * mq))

        @pl.when(pl.program_id(0) == 0)
        def _():
            dg_ref[...] = jnp.zeros_like(dg_ref)
            loss_ref[...] = jnp.zeros_like(loss_ref)

        dg_ref[...] += jnp.sum(dy * xn, axis=0, keepdims=True)
        part = 0.5 * jnp.sum(jnp.mean(err * err, axis=-1, keepdims=True), axis=0, keepdims=True)
        loss_ref[...] += jnp.broadcast_to(part, loss_ref.shape)

    return _pc(body, name=name, grid=(t_dim // tm,),
               in_specs=[_bs((tm, D), lambda i: (i, 0)), _bs((1, D), lambda i: (0, 0)), _bs((tm, D), lambda i: (i, 0))],
               out_specs=[_bs((8, 128), lambda i: (0, 0)), _bs((tm, D), lambda i: (i, 0)), _bs((1, D), lambda i: (0, 0))],
               out_shape=[_sds((8, 128), F32), _sds((t_dim, D), F32), _sds((1, D), F32)])(x, gain, tgt)


def _band_mask(n):
    ri = lax.broadcasted_iota(jnp.int32, (BLK, 2 * BLK), 0)
    cj = lax.broadcasted_iota(jnp.int32, (BLK, 2 * BLK), 1)
    dist = ri + BLK - cj
    return (dist >= 0) & (dist <= BLK) & ((cj >= BLK) | (n > 0))


def _halves(unit, imap_rows, col):
    return [_bs((unit, 128), (lambda u, j=j: (imap_rows(u), 2 * col + j))) for j in range(2)]


def _get_rows(pair, r, dil):
    return jnp.concatenate([h[pl.ds(r, BLK, stride=dil), :] for h in pair], axis=1)


def _put_rows(stage, r, dil, val):
    for j in range(2):
        stage[j, pl.ds(r, BLK, stride=dil), :] = val[:, 128 * j:128 * (j + 1)]


def _unstage(stage, out_ref):
    out_ref[:, :128] = stage[0]
    out_ref[:, 128:] = stage[1]


def _attn_fwd(z, acc, ml, *, gi, first, final, name):
    dil = DILS[gi]
    t_dim = z.shape[0]
    unit = BLK * dil
    nu = t_dim // unit
    own = lambda u: u
    prev = lambda u: jnp.maximum(u - 1, 0)
    in_specs = (_halves(unit, own, Q_B + gi) + _halves(unit, own, K_B + gi) + _halves(unit, prev, K_B + gi)
                + _halves(unit, own, V_B + gi) + _halves(unit, prev, V_B + gi))
    args = [z] * 10
    st_ml = _bs((unit, 128), lambda u: (u, 0))
    if not first:
        in_specs += _halves(unit, own, 0) + [st_ml]
        args += [acc, acc, ml]

    def body(*refs):
        q2, ko2, kp2, vo2, vp2 = (refs[2 * i:2 * i + 2] for i in range(5))
        o_acc_ref, o_ml_ref, stage = refs[-3], refs[-2], refs[-1]
        valid = _band_mask(pl.program_id(0))
        lane = lax.broadcasted_iota(jnp.int32, (BLK, GW), 1)
        lane_s = lax.broadcasted_iota(jnp.int32, (BLK, 128), 1)

        def band(r, carry):
            q = _get_rows(q2, r, dil) * QK_SCALE
            kb = jnp.concatenate([_get_rows(kp2, r, dil), _get_rows(ko2, r, dil)], axis=0).astype(BF16)
            vb = jnp.concatenate([_get_rows(vp2, r, dil), _get_rows(vo2, r, dil)], axis=0).astype(BF16)
            if first:
                acc_v = jnp.zeros((BLK, GW), F32)
                ml_v = jnp.where(lane_s < 64, NEG, 0.0).astype(F32)
            else:
                acc_v = _get_rows(refs[10:12], r, dil)
                ml_v = refs[12][pl.ds(r, BLK, stride=dil), :]
            acc_o, ml_o = acc_v, ml_v
            l_b = jnp.ones((BLK, GW), F32)
            for h in range(NH):
                hm = (lane >= h * HD) & (lane < (h + 1) * HD)
                qh = jnp.where(hm, q, 0.0).astype(BF16)
                s = lax.dot_general(qh, kb, NT, preferred_element_type=F32)
                s = jnp.where(valid, s, NEG)
                m_old = ml_v[:, 16 * h:16 * h + 1]
                l_old = ml_v[:, 64 + 16 * h:64 + 16 * h + 1]
                m_new = jnp.maximum(m_old, jnp.max(s, axis=-1, keepdims=True))
                alpha = jnp.exp(m_old - m_new)
                p = jnp.exp(s - m_new)
                l_new = alpha * l_old + jnp.sum(p, axis=-1, keepdims=True)
                pv = jnp.dot(p.astype(BF16), vb, preferred_element_type=F32)
                acc_o = jnp.where(hm, acc_v * alpha + pv, acc_o)
                l_b = jnp.where(hm, l_new, l_b)
                if final:
                    ml_o = jnp.where((lane_s >= 16 * h) & (lane_s < 16 * h + 16), m_new + jnp.log(l_new), ml_o)
                else:
                    ml_o = jnp.where((lane_s >= 16 * h) & (lane_s < 16 * h + 16), m_new, ml_o)
                    ml_o = jnp.where((lane_s >= 64 + 16 * h) & (lane_s < 64 + 16 * h + 16), l_new, ml_o)
            _put_rows(stage, r, dil, acc_o / l_b if final else acc_o)
            o_ml_ref[pl.ds(r, BLK, stride=dil), :] = ml_o
            return carry

        lax.fori_loop(0, dil, band, 0)
        _unstage(stage, o_acc_ref)

    blocks = (len(in_specs) + 6) * unit * 128 * 4
    return _pc(body, name=name, grid=(nu,), in_specs=in_specs, out_specs=[_bs((unit, GW), lambda u: (u, 0)), st_ml],
               out_shape=[_sds((t_dim, GW), F32), _sds((t_dim, 128), F32)],
               scratch=[pltpu.VMEM((2, unit, 128), F32)], vmem=2 * blocks + (12 << 20))(*args)


def _attn_bwd(z, dy, y, lse, *, gi, name):
    dil = DILS[gi]
    t_dim = z.shape[0]
    unit = BLK * dil
    nu = t_dim // unit

    own = lambda u: jnp.minimum(u, nu - 1)
    prev = lambda u: jnp.maximum(jnp.minimum(u, nu - 1) - 1, 0)
    in_specs = (_halves(unit, own, Q_B + gi) + _halves(unit, own, K_B + gi) + _halves(unit, prev, K_B + gi)
                + _halves(unit, own, V_B + gi) + _halves(unit, prev, V_B + gi)
                + _halves(unit, own, 0) + _halves(unit, own, 0) + [_bs((unit, 128), lambda u: (own(u), 0))])
    tok = _bs((unit, GW), lambda u: (own(u), 0))
    kv_out = _bs((unit, GW), lambda u: (jnp.maximum(u - 1, 0), 0))

    def body(*refs):
        q2, ko2, kp2, vo2, vp2, dy2, y2 = (refs[2 * i:2 * i + 2] for i in range(7))
        lse_ref = refs[14]
        dq_ref, dk_ref, dv_ref, dq_st, dk_st, dv_st, dkc, dvc = refs[15:]
        u = pl.program_id(0)

        @pl.when(u == 0)
        def _():
            dkc[...] = jnp.zeros_like(dkc)
            dvc[...] = jnp.zeros_like(dvc)

        @pl.when(u < nu)
        def _():
            valid = _band_mask(u)
            lane = lax.broadcasted_iota(jnp.int32, (BLK, GW), 1)

            def band(r, carry):
                q = _get_rows(q2, r, dil) * QK_SCALE
                kb = jnp.concatenate([_get_rows(kp2, r, dil), _get_rows(ko2, r, dil)], axis=0).astype(BF16)
                vb = jnp.concatenate([_get_rows(vp2, r, dil), _get_rows(vo2, r, dil)], axis=0).astype(BF16)
                dyv = _get_rows(dy2, r, dil)
                yv = _get_rows(y2, r, dil)
                lse_v = lse_ref[pl.ds(r, BLK, stride=dil), :]
                dq = jnp.zeros((BLK, GW), F32)
                dkb = jnp.zeros((2 * BLK, GW), F32)
                dvb = jnp.zeros((2 * BLK, GW), F32)
                for h in range(NH):
                    hm = (lane >= h * HD) & (lane < (h + 1) * HD)
                    qh = jnp.where(hm, q, 0.0).astype(BF16)
                    s = lax.dot_general(qh, kb, NT, preferred_element_type=F32)
                    p = jnp.where(valid, jnp.exp(s - lse_v[:, 16 * h:16 * h + 1]), 0.0)
                    dyh = jnp.where(hm, dyv, 0.0)
                    delta = jnp.sum(dyh * yv, axis=-1, keepdims=True)
                    dyh16 = dyh.astype(BF16)
                    dp = lax.dot_general(dyh16, vb, NT, preferred_element_type=F32)
                    ds = (p * (dp - delta)).astype(BF16)
                    dq = dq + jnp.where(hm, jnp.dot(ds, kb, preferred_element_type=F32), 0.0)
                    dkb = dkb + lax.dot_general(ds, qh, TN, preferred_element_type=F32)
                    dvb = dvb + lax.dot_general(p.astype(BF16), dyh16, TN, preferred_element_type=F32)
                _put_rows(dq_st, r, dil, dq * QK_SCALE)
                _put_rows(dk_st, r, dil, _get_rows((dkc.at[0], dkc.at[1]), r, dil) + dkb[:BLK])
                _put_rows(dv_st, r, dil, _get_rows((dvc.at[0], dvc.at[1]), r, dil) + dvb[:BLK])
                _put_rows(dkc, r, dil, dkb[BLK:])
                _put_rows(dvc, r, dil, dvb[BLK:])
                return carry

            lax.fori_loop(0, dil, band, 0)
            _unstage(dq_st, dq_ref)

            @pl.when(u > 0)
            def _():
                _unstage(dk_st, dk_ref)
                _unstage(dv_st, dv_ref)

        @pl.when(u == nu)
        def _():
            _unstage(dkc, dk_ref)
            _unstage(dvc, dv_ref)

    blocks = (15 + 6 + 5) * unit * 128 * 4 * 2
    return _pc(body, name=name, grid=(nu + 1,), in_specs=in_specs, out_specs=[tok, kv_out, kv_out],
               out_shape=[_sds((t_dim, GW), F32)] * 3, scratch=[pltpu.VMEM((2, unit, 128), F32)] * 5,
               vmem=blocks + (12 << 20))(*([z] * 10), dy, dy, y, y, lse)


_INV_SQRT2 = 1.0 / math.sqrt(2.0)
_INV_SQRT2PI = 1.0 / math.sqrt(2.0 * math.pi)


def _gelu(v):
    return 0.5 * v * (1.0 + lax.erf(v * _INV_SQRT2))


def _gelu_grad(v):
    return 0.5 * (1.0 + lax.erf(v * _INV_SQRT2)) + v * (_INV_SQRT2PI * jnp.exp(-0.5 * v * v))


def _gmlp_core(z_refs, lng, lnb, ws_ref, bs_ref):
    u_pre = jnp.concatenate([z_refs[0][...], z_refs[1][...]], axis=1)
    v_pre = jnp.concatenate([z_refs[2][...], z_refs[3][...]], axis=1)
    u = _gelu(u_pre)
    v = _gelu(v_pre)
    xc = v - jnp.mean(v, axis=-1, keepdims=True)
    rstd = lax.rsqrt(jnp.mean(xc * xc, axis=-1, keepdims=True) + EPS)
    xhat = xc * rstd
    vn = xhat * lng + lnb
    ti = lax.broadcasted_iota(jnp.int32, (BLK, BLK), 0)
    si = lax.broadcasted_iota(jnp.int32, (BLK, BLK), 1)
    causal = ti >= si
    vn16 = vn.astype(BF16)
    ws = [jnp.where(causal, ws_ref[g], 0.0).astype(BF16) for g in range(4)]
    mixed = jnp.concatenate(
        [jnp.dot(ws[g], vn16[:, g * BLK:(g + 1) * BLK], preferred_element_type=F32) + bs_ref[g] for g in range(4)], axis=1)
    return u_pre, v_pre, u, xhat, rstd, vn16, ws, causal, mixed


def _gmlp_specs():
    z_specs = [_bs((BLK, GW), (lambda i, c=c: (i, UV_B + c))) for c in range(4)]
    par_specs = [_bs((1, BW), lambda i: (0, 0)), _bs((1, BW), lambda i: (0, 0)),
                 _bs((4, BLK, BLK), lambda i: (0, 0, 0)), _bs((4, BLK, 1), lambda i: (0, 0, 0))]
    return z_specs, par_specs


def _gmlp_fwd(z, lng, lnb, ws, bs, name):
    t_dim = z.shape[0]
    z_specs, par_specs = _gmlp_specs()

    def body(z0, z1, z2, z3, lng_ref, lnb_ref, ws_ref, bs_ref, o_ref):
        core = _gmlp_core((z0, z1, z2, z3), lng_ref[...], lnb_ref[...], ws_ref, bs_ref)
        o_ref[...] = (core[2] * core[8]).astype(BF16)

    return _pc(body, name=name, grid=(t_dim // BLK,), in_specs=z_specs + par_specs,
               out_specs=_bs((BLK, BW), lambda i: (i, 0)), out_shape=_sds((t_dim, BW), BF16))(z, z, z, z, lng, lnb, ws, bs)


def _gmlp_bwd(z, dy, lng, lnb, ws, bs, name):
    t_dim = z.shape[0]
    z_specs, par_specs = _gmlp_specs()

    def body(z0, z1, z2, z3, lng_ref, lnb_ref, ws_ref, bs_ref, dy_ref, duv_ref, dws_ref, dbs_ref, dlng_ref, dlnb_ref):
        lng_v = lng_ref[...]
        u_pre, v_pre, u, xhat, rstd, vn16, wsm, causal, mixed = _gmlp_core((z0, z1, z2, z3), lng_v, lnb_ref[...], ws_ref, bs_ref)
        dyv = dy_ref[...].astype(F32)
        du = dyv * mixed
        dmixed = dyv * u

        @pl.when(pl.program_id(0) == 0)
        def _():
            dws_ref[...] = jnp.zeros_like(dws_ref)
            dbs_ref[...] = jnp.zeros_like(dbs_ref)
            dlng_ref[...] = jnp.zeros_like(dlng_ref)
            dlnb_ref[...] = jnp.zeros_like(dlnb_ref)

        dvn_parts = []
        for g in range(4):
            dm = dmixed[:, g * BLK:(g + 1) * BLK]
            dm16 = dm.astype(BF16)
            dw = lax.dot_general(dm16, vn16[:, g * BLK:(g + 1) * BLK], NT, preferred_element_type=F32)
            dws_ref[g] += jnp.where(causal, dw, 0.0)
            dbs_ref[g] += jnp.sum(dm.T, axis=0, keepdims=True)
            dvn_parts.append(lax.dot_general(wsm[g], dm16, TN, preferred_element_type=F32))
        dvn = jnp.concatenate(dvn_parts, axis=1)
        dlng_ref[...] += jnp.sum(dvn * xhat, axis=0, keepdims=True)
        dlnb_ref[...] += jnp.sum(dvn, axis=0, keepdims=True)
        dxh = dvn * lng_v
        dv = rstd * (dxh - jnp.mean(dxh, axis=-1, keepdims=True) - xhat * jnp.mean(dxh * xhat, axis=-1, keepdims=True))
        duv_ref[:, :BW] = (du * _gelu_grad(u_pre)).astype(BF16)
        duv_ref[:, BW:] = (dv * _gelu_grad(v_pre)).astype(BF16)

    return _pc(body, name=name, grid=(t_dim // BLK,),
               in_specs=z_specs + par_specs + [_bs((BLK, BW), lambda i: (i, 0))],
               out_specs=[_bs((BLK, 2 * BW), lambda i: (i, 0)), _bs((4, BLK, BLK), lambda i: (0, 0, 0)),
                          _bs((4, 1, BLK), lambda i: (0, 0, 0)), _bs((1, BW), lambda i: (0, 0)), _bs((1, BW), lambda i: (0, 0))],
               out_shape=[_sds((t_dim, 2 * BW), BF16), _sds((4, BLK, BLK), F32), _sds((4, 1, BLK), F32),
                          _sds((1, BW), F32), _sds((1, BW), F32)])(z, z, z, z, lng, lnb, ws, bs, dy)


def _pool_core(x, gi):
    t_dim = x.shape[0]
    w = jnp.left_shift(2, gi)
    row = lax.broadcasted_iota(jnp.int32, x.shape, 0)
    s = x
    for k in (1, 2, 4, 8):
        sh = jnp.where(row >= k, pltpu.roll(s, k, 0), 0.0)
        s = jnp.where(k < w, s + sh, s)
    cnt = jnp.minimum(row + 1, w).astype(F32)
    return s / cnt - x, cnt, row, w, t_dim


def _pool_fwd(z, cw, cs, name):
    t_dim = z.shape[0]

    def body(z_ref, cw_ref, cs_ref, o_ref):
        pooled = _pool_core(z_ref[...], pl.program_id(0))[0]
        y = jnp.dot(pooled.astype(BF16), cw_ref[0].astype(BF16), preferred_element_type=F32) * cs_ref[0]
        o_ref[...] = y.astype(BF16)

    return _pc(body, name=name, grid=(4,),
               in_specs=[_bs((t_dim, BLK), lambda g: (0, C_B128 + g)), _bs((1, BLK, BLK), lambda g: (g, 0, 0)),
                         _bs((1, 1, BLK), lambda g: (g, 0, 0))],
               out_specs=_bs((t_dim, BLK), lambda g: (0, g)), out_shape=_sds((t_dim, BW), BF16),
               vmem=40 << 20)(z, cw, cs)


def _pool_bwd(z, dy, cw, cs, name):
    t_dim = z.shape[0]

    def body(z_ref, dy_ref, cw_ref, cs_ref, dx_ref, dcw_ref, dcs_ref):
        pooled, cnt, row, w, _ = _pool_core(z_ref[...], pl.program_id(0))
        p16 = pooled.astype(BF16)
        cw16 = cw_ref[0].astype(BF16)
        dyv = dy_ref[...].astype(F32)
        lin = jnp.dot(p16, cw16, preferred_element_type=F32)
        dcs_ref[0] = jnp.sum(dyv * lin, axis=0, keepdims=True)
        dys = (dyv * cs_ref[0]).astype(BF16)
        dcw_ref[0] = lax.dot_general(p16, dys, TN, preferred_element_type=F32)
        dpool = lax.dot_general(dys, cw16, NT, preferred_element_type=F32)
        sb = dpool / cnt
        for k in (1, 2, 4, 8):
            sh = jnp.where(row < t_dim - k, pltpu.roll(sb, t_dim - k, 0), 0.0)
            sb = jnp.where(k < w, sb + sh, sb)
        dx_ref[...] = (sb - dpool).astype(BF16)

    return _pc(body, name=name, grid=(4,),
               in_specs=[_bs((t_dim, BLK), lambda g: (0, C_B128 + g)), _bs((t_dim, BLK), lambda g: (0, g)),
                         _bs((1, BLK, BLK), lambda g: (g, 0, 0)), _bs((1, 1, BLK), lambda g: (g, 0, 0))],
               out_specs=[_bs((t_dim, BLK), lambda g: (0, g)), _bs((1, BLK, BLK), lambda g: (g, 0, 0)),
                          _bs((1, 1, BLK), lambda g: (g, 0, 0))],
               out_shape=[_sds((t_dim, BW), BF16), _sds((4, BLK, BLK), F32), _sds((4, 1, BLK), F32)],
               vmem=48 << 20)(z, dy, cw, cs)


def _gates_specs(tm):
    z_specs = [_bs((tm, GW), (lambda j, i, b=b: (i, G_B + 4 * b + j))) for b in range(3)]
    bg_specs = [_bs((1, GW), (lambda j, i, b=b: (0, 4 * b + j))) for b in range(3)]
    tile = _bs((tm, GW), lambda j, i: (i, j))
    return z_specs, bg_specs, tile


def _gates_fwd(z, bg, pa, pb, pc, name, tm=512):
    t_dim = z.shape[0]
    z_specs, bg_specs, tile = _gates_specs(tm)

    def body(za, zb, zc, ba, bb, bc, pa_ref, pb_ref, pc_ref, o_ref):
        o_ref[...] = (_sigmoid(za[...] + ba[...]) * pa_ref[...] + _sigmoid(zb[...] + bb[...]) * pb_ref[...]
                      + _sigmoid(zc[...] + bc[...]) * pc_ref[...]).astype(BF16)

    return _pc(body, name=name, grid=(D // GW, t_dim // tm), in_specs=z_specs + bg_specs + [tile] * 3,
               out_specs=tile, out_shape=_sds((t_dim, D), BF16))(z, z, z, bg, bg, bg, pa, pb, pc)


def _gates_bwd(z, bg, pa, pb, pc, dm, name, tm=512):
    t_dim = z.shape[0]
    z_specs, bg_specs, tile = _gates_specs(tm)
    bias_out = _bs((1, GW), lambda j, i: (0, j))

    def body(za, zb, zc, ba, bb, bc, pa_ref, pb_ref, pc_ref, dm_ref, dpa, dpb, dpc, dza, dzb, dzc, dba, dbb, dbc):
        dmv = dm_ref[...]
        first = pl.program_id(1) == 0
        for z_ref, b_ref, p_ref, dp_ref, dz_ref, db_ref in ((za, ba, pa_ref, dpa, dza, dba), (zb, bb, pb_ref, dpb, dzb, dbb),
                                                            (zc, bc, pc_ref, dpc, dzc, dbc)):
            gate = _sigmoid(z_ref[...] + b_ref[...])
            dp_ref[...] = (gate * dmv).astype(BF16)
            dz = dmv * p_ref[...] * (gate * (1.0 - gate))
            dz_ref[...] = dz.astype(BF16)

            @pl.when(first)
            def _():
                db_ref[...] = jnp.zeros_like(db_ref)

            db_ref[...] += jnp.sum(dz, axis=0, keepdims=True)

    return _pc(body, name=name, grid=(D // GW, t_dim // tm), in_specs=z_specs + bg_specs + [tile] * 4,
               out_specs=[tile] * 6 + [bias_out] * 3,
               out_shape=[_sds((t_dim, D), BF16)] * 6 + [_sds((1, D), F32)] * 3)(z, z, z, bg, bg, bg, pa, pb, pc, dm)


def _coords():
    return lax.axis_index("x"), lax.axis_index("y"), lax.axis_index("c")


def _peer(k, x, y, c):
    px = 1 - x if k & 4 else x
    py = 1 - y if k & 2 else y
    pc = 1 - c if k & 1 else c
    return (px, py, pc), 4 * px + 2 * py + pc


def _exchange_copies(kind, bufs, send, recv):
    x, y, c = _coords()
    me = 4 * x + 2 * y + c
    na = len(bufs) if kind == "gather" else len(bufs) // 2
    out = []
    for k in range(1, NDEV):
        to, peer = _peer(k, x, y, c)
        for a in range(na):
            sems = dict(send_sem=send.at[a * 7 + k - 1], recv_sem=recv.at[a * 7 + k - 1], device_id=to, device_id_type=MESH)
            if kind == "gather":
                mine = pltpu.make_async_remote_copy(src_ref=bufs[a].at[:, me], dst_ref=bufs[a].at[:, me], **sems)
                theirs = pltpu.make_async_remote_copy(src_ref=bufs[a].at[:, me], dst_ref=bufs[a].at[:, peer], **sems)
            else:
                src, land = bufs[a], bufs[na + a]
                mine = pltpu.make_async_remote_copy(src_ref=src.at[:, peer], dst_ref=land.at[me], **sems)
                theirs = pltpu.make_async_remote_copy(src_ref=src.at[:, peer], dst_ref=land.at[peer], **sems)
            out.append((mine, theirs))
    return out


_HBM = pl.BlockSpec(memory_space=pltpu.HBM)
_SEM = pl.BlockSpec(memory_space=pltpu.SEMAPHORE)
_EFFECT = pltpu.SideEffectType.DATAFLOW_SIDE_EFFECTING


def _exchange_start(kind, arrays, after, name):
    na = len(arrays)
    if kind == "scatter":
        arrays = list(arrays) + [lax.empty((NDEV, s.shape[0]) + s.shape[2:], s.dtype) for s in arrays]
    nb = len(arrays)

    def body(*refs):
        bufs = refs[:nb]
        send, recv = refs[nb + 1], refs[nb + 2]
        token = refs[-1]
        for mine, _ in _exchange_copies(kind, bufs, send, recv):
            mine.start()
        token[...] = jnp.zeros_like(token)

    arrays = [pltpu.with_memory_space_constraint(b, pltpu.HBM) for b in arrays]
    outs = pl.pallas_call(
        body, name=name,
        out_shape=(pltpu.SemaphoreType.DMA((7 * na,)), pltpu.SemaphoreType.DMA((7 * na,)),
                   *[pltpu.HBM(b.shape, b.dtype) for b in arrays], _sds((8, 128), F32)),
        in_specs=[_HBM] * nb + [pl.BlockSpec(memory_space=pl.ANY)],
        out_specs=(_SEM, _SEM, *([_HBM] * nb), pl.BlockSpec(memory_space=pltpu.VMEM)),
        input_output_aliases={i: 2 + i for i in range(nb)},
        compiler_params=pltpu.CompilerParams(has_side_effects=_EFFECT), interpret=False)(*arrays, after)
    return outs[0], outs[1], list(outs[2:2 + nb]), outs[-1]


def _exchange_wait(kind, started, after, name):
    send, recv, arrays, _ = started
    nb = len(arrays)

    def body(*refs):
        for mine, theirs in _exchange_copies(kind, refs[:nb], refs[nb], refs[nb + 1]):
            mine.wait_send()
            theirs.wait_recv()

    outs = pl.pallas_call(
        body, name=name,
        out_shape=tuple(pltpu.HBM(b.shape, b.dtype) for b in arrays),
        in_specs=[_HBM] * nb + [_SEM, _SEM, pl.BlockSpec(memory_space=pl.ANY)],
        out_specs=tuple([_HBM] * nb),
        input_output_aliases={i: i for i in range(nb)},
        compiler_params=pltpu.CompilerParams(has_side_effects=_EFFECT), interpret=False)(*arrays, send, recv, after)
    return list(outs)


def _adam_math(w, g, m, v):
    m2 = ADAM_B1 * m + (1.0 - ADAM_B1) * g
    v2 = ADAM_B2 * v + (1.0 - ADAM_B2) * jnp.square(g)
    m_hat = m2 / (1.0 - ADAM_B1 ** ADAM_STEP)
    v_hat = v2 / (1.0 - ADAM_B2 ** ADAM_STEP)
    return -ADAM_LR * (m_hat / (jnp.sqrt(v_hat) + ADAM_EPS) + ADAM_WD * w), m2, v2


def _reduce_update(landed, own, me, piece, w, m, v, layer, prev, transposed, name):
    rows, cols = landed.shape[2], landed.shape[3]
    tc = 256
    assert cols % tc == 0 and w.shape[1:] == ((cols, rows) if transposed else (rows, cols))
    lane_chunks = [(c0, min(128, rows - c0)) for c0 in range(0, rows, 128)]

    def body(me_ref, b_ref, own_ref, w_ref, m_ref, v_ref, *rest):
        outs = rest[-4:]
        mine = me_ref[0]
        s = None
        for d in range(NDEV):
            other = jnp.where(mine == d, (d + 1) % NDEV, d)
            part = jnp.where(mine == d, own_ref[0, 0], b_ref[other, 0]).astype(F32)
            s = part if s is None else s + part
        if not transposed:
            for o, val in zip(outs, (s,) + _adam_math(w_ref[0], s, m_ref[0], v_ref[0])):
                o[0] = val
            return
        for c0, wd in lane_chunks:
            chunk = s[c0:c0 + wd]
            if wd < 128:
                chunk = jnp.concatenate([chunk, jnp.zeros((128 - wd, tc), F32)], axis=0)
            g = chunk.T[:, :wd]
            sl = (0, slice(None), slice(c0, c0 + wd))
            for o, val in zip(outs, (g,) + _adam_math(w_ref[sl], g, m_ref[sl], v_ref[sl])):
                o[sl] = val

    if transposed:
        nat = _bs((1, tc, rows), lambda j, me_ref: (layer, j, 0))
    else:
        nat = _bs((1, rows, tc), lambda j, me_ref: (layer, 0, j))
    in_specs = [_bs((NDEV, 1, rows, tc), lambda j, me_ref: (0, piece, 0, j)),
                _bs((1, 1, rows, tc), lambda j, me_ref: (piece, me_ref[0], 0, j)), nat, nat, nat]
    args = [me, landed, own, w, m, v]
    alias = {}
    if prev is not None:
        in_specs += [pl.BlockSpec(memory_space=pl.ANY)] * 4
        args += list(prev)
        alias = {6 + i: i for i in range(4)}
    grid_spec = pltpu.PrefetchScalarGridSpec(num_scalar_prefetch=1, grid=(cols // tc,), in_specs=in_specs, out_specs=[nat] * 4)
    return pl.pallas_call(body, name=name, grid_spec=grid_spec, out_shape=[_sds(w.shape, F32)] * 4,
                          input_output_aliases=alias, interpret=False)(*args)


SMALL_LAYER = ("ffn1_norm", "mix_norm", "b_gate", "b_ln_g", "b_ln_b", "b_w_s", "b_b_s", "c_w", "c_scale", "ffn2_norm")
SMALL_ROWS = {"ffn1_norm": 8, "mix_norm": 8, "b_gate": 24, "b_ln_g": 4, "b_ln_b": 4, "b_w_s": 512, "b_b_s": 4, "c_w": 512,
              "c_scale": 4, "ffn2_norm": 8, "final_norm": 8}
LAYER_ROWS = sum(SMALL_ROWS[k] for k in SMALL_LAYER)
PACK_ROWS = DEPTH * LAYER_ROWS + SMALL_ROWS["final_norm"]


def _small_offset(l, name):
    if name == "final_norm":
        return DEPTH * LAYER_ROWS
    return l * LAYER_ROWS + sum(SMALL_ROWS[k] for k in SMALL_LAYER[:SMALL_LAYER.index(name)])


def _pack_small_grads(per_layer, final, me, name):
    flat, where = [], []
    for l in range(DEPTH):
        for k in SMALL_LAYER:
            parts = per_layer[l][k] if k == "b_gate" else [per_layer[l][k]]
            for i, part in enumerate(parts):
                flat.append(part)
                where.append(_small_offset(l, k) + i * (D // 128))
    flat.append(final)
    where.append(_small_offset(0, "final_norm"))

    def body(me_ref, *refs):
        del me_ref
        o_ref = refs[-1]
        for ref, off in zip(refs[:-1], where):
            if len(ref.shape) == 2:
                for j in range(ref.shape[1] // 128):
                    o_ref[0, 0, off + j:off + j + 1, :] = ref[:, 128 * j:128 * (j + 1)]
            else:
                rows = ref.shape[1]
                for g in range(4):
                    o_ref[0, 0, off + g * rows:off + (g + 1) * rows, :] = ref[g]

    grid_spec = pltpu.PrefetchScalarGridSpec(
        num_scalar_prefetch=1, grid=(1,),
        in_specs=[pl.BlockSpec(a.shape, (lambda i, me_ref, nd=a.ndim: (0,) * nd)) for a in flat],
        out_specs=_bs((1, 1, PACK_ROWS, 128), lambda i, me_ref: (0, me_ref[0], 0, 0)))
    return pl.pallas_call(body, name=name, grid_spec=grid_spec, out_shape=_sds((1, NDEV, PACK_ROWS, 128), F32),
                          interpret=False)(me, *flat)


def _adamw_small(gathered, w, mom, var, name):
    names = SMALL_LAYER + ("final_norm",)
    ins = [gathered] + [t[k] for k in names for t in (w, mom, var)]

    def body(*refs):
        land = refs[0]
        outs = refs[len(ins):]

        def rows(off, n):
            s = land[0, 0, off:off + n, :]
            for d in range(1, NDEV):
                s = s + land[0, d, off:off + n, :]
            return s

        def as_row(off, n):
            s = rows(off, n)
            return jnp.concatenate([s[j:j + 1, :] for j in range(n)], axis=1)

        def update(i, idx, g):
            wr, mr, vr = (refs[1 + 3 * i + t] for t in range(3))
            d, m2, v2 = _adam_math(wr[idx], g, mr[idx], vr[idx])
            for o, val in zip(outs[4 * i:4 * i + 4], (g, d, m2, v2)):
                o[idx] = val

        for i, k in enumerate(names):
            n = SMALL_ROWS[k]
            for l in range(1 if k == "final_norm" else DEPTH):
                off = _small_offset(l, k)
                if k in ("b_w_s", "c_w"):
                    for g in range(4):
                        update(i, (l, g), rows(off + g * BLK, BLK))
                elif k in ("b_b_s", "c_scale"):
                    update(i, (l,), rows(off, n))
                else:
                    update(i, (slice(l, l + 1),), as_row(off, n))

    out_shape = [_sds(w[k].shape, F32) for k in names for _ in range(4)]
    outs = pl.pallas_call(body, name=name, out_shape=out_shape,
                          compiler_params=pltpu.CompilerParams(vmem_limit_bytes=48 << 20), interpret=False)(*ins)
    return tuple({k: outs[4 * i + t] for i, k in enumerate(names)} for t in range(4))


PG, PD = 0, 2


def _ffn_fwd(x, gain, wf, tag):
    h = _rms_fwd(x, gain, f"rms_{tag}")
    gu = _mm(h, wf, mode="nt", tm=2048, tn=512, tk=D, n=2 * FF, b_row0=PG * FF, out_dtype=BF16, name=f"gu_{tag}")
    act = _swiglu_fwd(gu, f"swiglu_{tag}")
    out = _mm(act, wf, mode="nn", tm=512, tn=D, tk=FF, n=D, b_row0=PD * FF, out_dtype=F32, scale=0.5, res=x, name=f"down_{tag}")
    return out, (x, h, gu, act)


def _ffn_bwd(dout, saved, gain, wf, tag, after, emit):
    x, h, gu, act = saved
    grf = lax.empty((3, NDEV, FFS, D), BF16)
    dgu = _dact_dswiglu(dout, wf, gu, tm=256, b_row0=PD * FF, after=after, name=f"dgu_{tag}")
    grf = _wgrad(act, dout, grf, piece0=PD, rows=FFS, tm=FF // 2, tn=D, tk=1024, scale=0.5, name=f"wg_down_{tag}")
    grf = _wgrad(dgu, h, grf, piece0=PG, rows=FFS, tm=FF // 2, tn=D, tk=2048, name=f"wg_gu_{tag}")
    token = emit([grf])
    dx, dgain = _mm_drms(dgu, wf, x, gain, dout, tm=256, b_row0=PG * FF, after=token, name=f"dx_{tag}")
    return dx, dgain, token


def _mixer_fwd(x, sp, ww, wo, wa, wbc, tag):
    h = _rms_fwd(x, sp["mix_norm"], f"rms_mix_{tag}")
    z = _mm(h, ww, mode="nt", tm=1024, tn=1152, tk=D, n=INW, out_dtype=F32, name=f"z_{tag}")
    acc, ml = _attn_fwd(z, None, None, gi=0, first=True, final=False, name=f"attn0_{tag}")
    acc, ml = _attn_fwd(z, acc, ml, gi=1, first=False, final=False, name=f"attn1_{tag}")
    ya, lse = _attn_fwd(z, acc, ml, gi=2, first=False, final=True, name=f"attn2_{tag}")
    yb = _gmlp_fwd(z, sp["b_ln_g"], sp["b_ln_b"], sp["b_w_s"], sp["b_b_s"], f"gmlp_{tag}")
    yc = _pool_fwd(z, sp["c_w"], sp["c_scale"], f"pool_{tag}")
    pa = _mm(ya, wa, mode="nt", tm=1024, tn=D, tk=GW, n=D, out_dtype=F32, name=f"proj_a_{tag}")
    pb = _mm(yb, wbc, mode="nt", tm=1024, tn=D, tk=BW, n=D, b_row0=0, out_dtype=F32, name=f"proj_b_{tag}")
    pc = _mm(yc, wbc, mode="nt", tm=1024, tn=D, tk=BW, n=D, b_row0=D, out_dtype=F32, name=f"proj_c_{tag}")
    merged = _gates_fwd(z, sp["b_gate"], pa, pb, pc, f"gates_{tag}")
    out = _mm(merged, wo, mode="nn", tm=1024, tn=D, tk=D, n=D, out_dtype=F32, res=x, name=f"out_{tag}")
    return out, (x, h, z, ya, lse, yb, yc, pa, pb, pc, merged)


def _mixer_bwd(dout, saved, sp, ww, wo, wa, wbc, tag, after, emit):
    x, h, z, ya, lse, yb, yc, pa, pb, pc, merged = saved
    grw = lax.empty((1, NDEV, INS, D), BF16)
    gro = lax.empty((1, NDEV, DS, D), BF16)
    gra = lax.empty((1, NDEV, DS, GW), BF16)
    grbc = lax.empty((2, NDEV, DS, BW), BF16)
    dm = _mm(dout, wo, mode="nt", tm=1024, tn=D, tk=D, n=D, out_dtype=F32, after=after, name=f"dmerged_{tag}")
    gro = _wgrad(merged, dout, gro, piece0=0, rows=DS, tm=D, tn=D, tk=1024, name=f"wg_out_{tag}")
    dpa, dpb, dpc, dza, dzb, dzc, dba, dbb, dbc = _gates_bwd(z, sp["b_gate"], pa, pb, pc, dm, f"dgates_{tag}")
    dya = _mm(dpa, wa, mode="nn", tm=2048, tn=GW, tk=D, n=GW, out_dtype=F32, name=f"dya_{tag}")
    dyb = _mm(dpb, wbc, mode="nn", tm=2048, tn=BW, tk=D, n=BW, b_row0=0, out_dtype=F32, name=f"dyb_{tag}")
    dyc = _mm(dpc, wbc, mode="nn", tm=2048, tn=BW, tk=D, n=BW, b_row0=D, out_dtype=F32, name=f"dyc_{tag}")
    gra = _wgrad(dpa, ya, gra, piece0=0, rows=DS, tm=D, tn=GW, tk=1024, name=f"wg_pa_{tag}")
    grbc = _wgrad(dpb, yb, grbc, piece0=0, rows=DS, tm=D, tn=BW, tk=1024, name=f"wg_pb_{tag}")
    grbc = _wgrad(dpc, yc, grbc, piece0=1, rows=DS, tm=D, tn=BW, tk=1024, name=f"wg_pc_{tag}")
    dqkv = [_attn_bwd(z, dya, ya, lse, gi=gi, name=f"dattn{gi}_{tag}") for gi in range(3)]
    duv, dws, dbs, dlng, dlnb = _gmlp_bwd(z, dyb, sp["b_ln_g"], sp["b_ln_b"], sp["b_w_s"], sp["b_b_s"], f"dgmlp_{tag}")
    dxc, dcw, dcs = _pool_bwd(z, dyc, sp["c_w"], sp["c_scale"], f"dpool_{tag}")
    dz = jnp.concatenate([dqkv[gi][part].astype(BF16) for part in range(3) for gi in range(3)] + [duv, dxc, dza, dzb, dzc],
                         axis=1)
    grw = _wgrad(dz, h, grw, piece0=0, rows=INS, tm=INW // 2, tn=512, tk=1024, name=f"wg_in_{tag}")
    token = emit([grw, gro, gra, grbc])
    dx, dgain = _mm_drms(dz, ww, x, sp["mix_norm"], dout, tm=256, b_row0=0, after=token, name=f"dx_mix_{tag}")
    small = {"mix_norm": dgain, "b_gate": [dba, dbb, dbc], "b_ln_g": dlng, "b_ln_b": dlnb,
             "b_w_s": dws, "b_b_s": dbs, "c_w": dcw, "c_scale": dcs}
    return dx, small, token


PARTS = ("f1", "mix", "f2")


def _forward(xs, tgt, weights_of, small, final_gain):
    saved, views = [], []
    cur = xs
    for l in range(DEPTH):
        (wf1,) = weights_of(l, "f1", cur)
        cur, s1 = _ffn_fwd(cur, small[l]["ffn1_norm"], wf1, f"f1l{l}")
        wmix = weights_of(l, "mix", cur)
        cur, s2 = _mixer_fwd(cur, small[l], *wmix, f"l{l}")
        (wf2,) = weights_of(l, "f2", cur)
        cur, s3 = _ffn_fwd(cur, small[l]["ffn2_norm"], wf2, f"f2l{l}")
        saved.append((s1, s2, s3))
        views.append((wf1, wmix, wf2))
    loss_part, dcur, dfinal = _final_loss(cur, final_gain, tgt, "final_loss")
    return loss_part, dcur, dfinal, saved, views


def _backward_layer(dcur, saved_l, small_l, views_l, l, emit, after):
    wf1, wmix, wf2 = views_l
    s1, s2, s3 = saved_l
    dcur, dn2, token = _ffn_bwd(dcur, s3, small_l["ffn2_norm"], wf2, f"f2l{l}", after, lambda g: emit("f2", g))
    dcur, sg, token = _mixer_bwd(dcur, s2, small_l, *wmix, f"l{l}", token, lambda g: emit("mix", g))
    dcur, dn1, token = _ffn_bwd(dcur, s1, small_l["ffn1_norm"], wf1, f"f1l{l}", token, lambda g: emit("f1", g))
    sg["ffn1_norm"] = dn1
    sg["ffn2_norm"] = dn2
    return dcur, sg, token


BIG = ("ffn1_w_gate", "ffn1_w_up", "ffn1_w_down", "w_in", "w_proj_a", "w_proj_b", "w_proj_c", "w_out",
       "ffn2_w_gate", "ffn2_w_up", "ffn2_w_down")
ALL_WEIGHTS = ("ffn1_norm", "ffn1_w_gate", "ffn1_w_up", "ffn1_w_down", "mix_norm", "w_in", "b_gate", "b_ln_g", "b_ln_b",
               "b_w_s", "b_b_s", "c_w", "c_scale", "w_proj_a", "w_proj_b", "w_proj_c", "w_out", "ffn2_norm", "ffn2_w_gate",
               "ffn2_w_up", "ffn2_w_down", "final_norm")


def _part_shards(w, l, part):
    t16 = lambda a: a.T.astype(BF16)
    if part == "mix":
        return [t16(w["w_in"][l])[None], w["w_out"][l].astype(BF16)[None], t16(w["w_proj_a"][l])[None],
                jnp.stack([t16(w["w_proj_b"][l]), t16(w["w_proj_c"][l])])]
    f = "ffn1" if part == "f1" else "ffn2"
    return [jnp.stack([t16(w[f + "_w_gate"][l]), t16(w[f + "_w_up"][l]), w[f + "_w_down"][l].astype(BF16)])]


def _part_views(part, gathered):
    if part == "mix":
        pw, po, pa, pbc = gathered
        return [pw.reshape(INW, D), po.reshape(D, D), pa.reshape(D, GW), pbc.reshape(2 * D, BW)]
    return [gathered[0].reshape(3 * FF, D)]


def _part_pieces(part):
    if part == "mix":
        return [(0, 0, "w_in", True), (1, 0, "w_out", False), (2, 0, "w_proj_a", True), (3, 0, "w_proj_b", True),
                (3, 1, "w_proj_c", True)]
    f = "ffn1" if part == "f1" else "ffn2"
    return [(0, 0, f + "_w_gate", True), (0, 1, f + "_w_up", True), (0, 2, f + "_w_down", False)]


def _layer_small(w, l):
    r = lambda a, shape: a.reshape(shape)
    return {"ffn1_norm": r(w["ffn1_norm"][l], (1, D)), "mix_norm": r(w["mix_norm"][l], (1, D)),
            "b_gate": r(w["b_gate"][l], (1, 3 * D)), "b_ln_g": r(w["b_ln_g"][l], (1, BW)), "b_ln_b": r(w["b_ln_b"][l], (1, BW)),
            "b_w_s": w["b_w_s"][l], "b_b_s": r(w["b_b_s"][l], (4, BLK, 1)), "c_w": w["c_w"][l],
            "c_scale": r(w["c_scale"][l], (4, 1, BLK)), "ffn2_norm": r(w["ffn2_norm"][l], (1, D))}


def kernel(x, ffn1_norm, ffn1_w_gate, ffn1_w_up, ffn1_w_down, mix_norm, w_in, b_gate, b_ln_g, b_ln_b, b_w_s, b_b_s, c_w, c_scale, w_proj_a, w_proj_b, w_proj_c, w_out, ffn2_norm, ffn2_w_gate, ffn2_w_up, ffn2_w_down, final_norm, loss_target, m_ffn1_norm, m_ffn1_w_gate, m_ffn1_w_up, m_ffn1_w_down, m_mix_norm, m_w_in, m_b_gate, m_b_ln_g, m_b_ln_b, m_b_w_s, m_b_b_s, m_c_w, m_c_scale, m_w_proj_a, m_w_proj_b, m_w_proj_c, m_w_out, m_ffn2_norm, m_ffn2_w_gate, m_ffn2_w_up, m_ffn2_w_down, m_final_norm, v_ffn1_norm, v_ffn1_w_gate, v_ffn1_w_up, v_ffn1_w_down, v_mix_norm, v_w_in, v_b_gate, v_b_ln_g, v_b_ln_b, v_b_w_s, v_b_b_s, v_c_w, v_c_scale, v_w_proj_a, v_w_proj_b, v_w_proj_c, v_w_out, v_ffn2_norm, v_ffn2_w_gate, v_ffn2_w_up, v_ffn2_w_down, v_final_norm):
    w = dict(ffn1_norm=ffn1_norm, ffn1_w_gate=ffn1_w_gate, ffn1_w_up=ffn1_w_up, ffn1_w_down=ffn1_w_down, mix_norm=mix_norm,
             w_in=w_in, b_gate=b_gate, b_ln_g=b_ln_g, b_ln_b=b_ln_b, b_w_s=b_w_s, b_b_s=b_b_s, c_w=c_w, c_scale=c_scale,
             w_proj_a=w_proj_a, w_proj_b=w_proj_b, w_proj_c=w_proj_c, w_out=w_out, ffn2_norm=ffn2_norm, ffn2_w_gate=ffn2_w_gate,
             ffn2_w_up=ffn2_w_up, ffn2_w_down=ffn2_w_down, final_norm=final_norm)
    mom = dict(ffn1_norm=m_ffn1_norm, ffn1_w_gate=m_ffn1_w_gate, ffn1_w_up=m_ffn1_w_up, ffn1_w_down=m_ffn1_w_down,
               mix_norm=m_mix_norm, w_in=m_w_in, b_gate=m_b_gate, b_ln_g=m_b_ln_g, b_ln_b=m_b_ln_b, b_w_s=m_b_w_s, b_b_s=m_b_b_s,
               c_w=m_c_w, c_scale=m_c_scale, w_proj_a=m_w_proj_a, w_proj_b=m_w_proj_b, w_proj_c=m_w_proj_c, w_out=m_w_out,
               ffn2_norm=m_ffn2_norm, ffn2_w_gate=m_ffn2_w_gate, ffn2_w_up=m_ffn2_w_up, ffn2_w_down=m_ffn2_w_down,
               final_norm=m_final_norm)
    var = dict(ffn1_norm=v_ffn1_norm, ffn1_w_gate=v_ffn1_w_gate, ffn1_w_up=v_ffn1_w_up, ffn1_w_down=v_ffn1_w_down,
               mix_norm=v_mix_norm, w_in=v_w_in, b_gate=v_b_gate, b_ln_g=v_b_ln_g, b_ln_b=v_b_ln_b, b_w_s=v_b_w_s, b_b_s=v_b_b_s,
               c_w=v_c_w, c_scale=v_c_scale, w_proj_a=v_w_proj_a, w_proj_b=v_w_proj_b, w_proj_c=v_w_proj_c, w_out=v_w_out,
               ffn2_norm=v_ffn2_norm, ffn2_w_gate=v_ffn2_w_gate, ffn2_w_up=v_ffn2_w_up, ffn2_w_down=v_ffn2_w_down,
               final_norm=v_final_norm)

    t_dim = x.shape[1]
    xs = x.reshape(t_dim, D)
    tgt = loss_target.reshape(t_dim, D)

    me = 4 * lax.axis_index("x") + 2 * lax.axis_index("y") + lax.axis_index("c")
    gathers = {}
    token = jnp.zeros((8, 128), F32)
    for l in range(DEPTH):
        for part in PARTS:
            lands = [lax.dynamic_update_slice(lax.empty((s.shape[0], NDEV) + s.shape[1:], BF16), s[:, None], (0, me, 0, 0))
                     for s in _part_shards(w, l, part)]
            gathers[l, part] = _exchange_start("gather", lands, token, f"gather_start_{part}_l{l}")
            token = gathers[l, part][-1]

    def weights_of(l, part, after):
        behind = token if (l, part) == (0, "f1") else after
        return _part_views(part, _exchange_wait("gather", gathers[l, part], behind, f"gather_wait_{part}_l{l}"))

    small = [_layer_small(w, l) for l in range(DEPTH)]
    loss_part, dcur, dfinal, saved, views = _forward(xs, tgt, weights_of, small, w["final_norm"].reshape(1, D))
    loss = lax.psum(loss_part[0, 0], ("x", "y", "c"))

    small_grads = [None] * DEPTH
    scatters = []
    token = None
    for l in reversed(range(DEPTH)):
        def emit(part, grads, l=l):
            scatters.append((l, part, _exchange_start("scatter", grads, dfinal, f"scatter_start_{part}_l{l}")))
            return scatters[-1][2][-1]
        dcur, small_grads[l], token = _backward_layer(dcur, saved[l], small[l], views[l], l, emit, token)
    grad_x = dcur.reshape(x.shape)

    me_arr = me.astype(jnp.int32).reshape(1)
    packed = _pack_small_grads(small_grads, dfinal, me_arr, "pack_small_grads")
    small_gather = _exchange_start("gather", [packed], token, "gather_start_small")

    done = {}
    behind = small_gather[-1]
    for l, part, started in scatters:
        arrs = _exchange_wait("scatter", started, behind, f"scatter_wait_{part}_l{l}")
        na = len(arrs) // 2
        for a, piece, k, transposed in _part_pieces(part):
            flip = transposed and w[k].shape[2] % 128 != 0
            view = (lambda t: t.transpose(0, 2, 1)) if flip else (lambda t: t)
            done[k] = _reduce_update(arrs[na + a], arrs[a], me_arr, piece, view(w[k]), view(mom[k]), view(var[k]), l,
                                     done.get(k), transposed and not flip, f"update_{k}_l{l}")
            behind = done[k][1]
    unflip = lambda k, t: t.transpose(0, 2, 1) if t.shape != w[k].shape else t
    grads, delta, new_m, new_v = ({k: unflip(k, done[k][t]) for k in BIG} for t in range(4))
    gathered = _exchange_wait("gather", small_gather, behind, "gather_wait_small")[0]
    as_row = lambda t: dict(t, final_norm=t["final_norm"].reshape(1, D))
    for dst, src in zip((grads, delta, new_m, new_v), _adamw_small(gathered, as_row(w), as_row(mom), as_row(var), "adamw_small")):
        dst.update(src)
        dst["final_norm"] = src["final_norm"].reshape(D)

    return (loss, grad_x, *[grads[k] for k in ALL_WEIGHTS], *[delta[k] for k in ALL_WEIGHTS],
            *[new_m[k] for k in ALL_WEIGHTS], *[new_v[k] for k in ALL_WEIGHTS])
```

```python
import math

import jax
import jax.numpy as jnp
from jax import lax
from jax.experimental import pallas as pl
from jax.experimental.pallas import tpu as pltpu

F32 = jnp.float32
BF16 = jnp.bfloat16
MESH = pl.DeviceIdType.MESH

NDEV = 8
DEPTH = 2
D = 1024
FF = 2816
FFS = FF // NDEV
INW = 6912
INS = INW // NDEV
DS = D // NDEV
BLK = 128
NH = 4
HD = 64
GW = NH * HD
DILS = (1, 4, 16)
QK_SCALE = 1.0 / math.sqrt(HD)
ZB = INW // GW
Q_B, K_B, V_B = 0, 3, 6
UV_B = 9
C_B128 = 26
G_B = 15
BW = 512
EPS = 1e-6
NEG = -1e30
VMEM_CAP = 60 * 1024 * 1024

ADAM_LR, ADAM_B1, ADAM_B2, ADAM_EPS, ADAM_WD, ADAM_STEP = 0.001, 0.9, 0.999, 1e-08, 0.01, 10

NT = (((1,), (1,)), ((), ()))
TN = (((0,), (0,)), ((), ()))


def _bs(shape, imap):
    return pl.BlockSpec(shape, imap)


def _pc(body, *, name, grid, in_specs, out_specs, out_shape, scratch=(), alias=None, vmem=None):
    params = {}
    if vmem is not None:
        params["vmem_limit_bytes"] = min(int(vmem), VMEM_CAP)
    big = lambda s: math.prod(s.shape) * jnp.dtype(s.dtype).itemsize >= (1 << 20)
    pin = lambda s: pltpu.HBM(s.shape, s.dtype) if big(s) else s
    out_shape = [pin(s) for s in out_shape] if isinstance(out_shape, (list, tuple)) else pin(out_shape)
    call = pl.pallas_call(
        body, name=name, grid=grid, in_specs=in_specs, out_specs=out_specs, out_shape=out_shape,
        scratch_shapes=list(scratch), input_output_aliases=alias or {},
        compiler_params=pltpu.CompilerParams(**params), interpret=False)
    return lambda *args: call(*[pltpu.with_memory_space_constraint(a, pltpu.HBM) if big(a) else a for a in args])


def _sds(shape, dtype):
    return jax.ShapeDtypeStruct(shape, dtype)


def _row_tile(rows, align, cap):
    best = None
    for cand in range(align, min(rows, cap) + 1, align):
        if rows % cand == 0:
            best = cand
    assert best is not None, (rows, align, cap)
    return best


def _mm(a, b, *, mode, tm, tn, tk, n, out_dtype, name, b_row0=0, scale=None, res=None, after=None):
    m, k_dim = a.shape
    nk = k_dim // tk
    assert m % tm == 0 and n % tn == 0 and k_dim % tk == 0
    b_off, rem = divmod(b_row0, tk if mode == "nn" else tn)
    assert rem == 0
    grid = (m // tm, n // tn, nk)
    in_specs = [_bs((tm, tk), lambda i, j, k: (i, k))]
    if mode == "nn":
        in_specs.append(_bs((tk, tn), lambda i, j, k: (b_off + k, j)))
    else:
        in_specs.append(_bs((tn, tk), lambda i, j, k: (b_off + j, k)))
    args = [a, b]
    if res is not None:
        in_specs.append(_bs((tm, tn), lambda i, j, k: (i, j)))
        args.append(res)
    if after is not None:
        in_specs.append(pl.BlockSpec(memory_space=pl.ANY))
        args.append(after)
    n_in = len(args)

    def body(*refs):
        a_ref, b_ref = refs[0], refs[1]
        r_ref = refs[2] if res is not None else None
        o_ref = refs[n_in]
        av = a_ref[...].astype(BF16)
        bv = b_ref[...].astype(BF16)
        if mode == "nn":
            p = jnp.dot(av, bv, preferred_element_type=F32)
        else:
            p = lax.dot_general(av, bv, NT, preferred_element_type=F32)

        def fin(v):
            if scale is not None:
                v = v * scale
            if r_ref is not None:
                v = r_ref[...] + v
            o_ref[...] = v.astype(out_dtype)

        if nk == 1:
            fin(p)
        else:
            acc_ref = refs[-1]
            kk = pl.program_id(2)

            @pl.when(kk == 0)
            def _():
                acc_ref[...] = p

            @pl.when(kk > 0)
            def _():
                acc_ref[...] += p

            @pl.when(kk == nk - 1)
            def _():
                fin(acc_ref[...])

    scratch = [pltpu.VMEM((tm, tn), F32)] if nk > 1 else []
    est = 2 * (tm * tk * a.dtype.itemsize + tk * tn * b.dtype.itemsize + tm * tn * jnp.dtype(out_dtype).itemsize)
    est += (2 + (nk > 1)) * tm * tn * 4 + (2 * tm * tn * 4 if res is not None else 0)
    est += (tm * tk * 2 if a.dtype != BF16 else 0) + (tk * tn * 2 if b.dtype != BF16 else 0)
    return _pc(body, name=name, grid=grid, in_specs=in_specs, out_specs=_bs((tm, tn), lambda i, j, k: (i, j)),
               out_shape=_sds((m, n), out_dtype), scratch=scratch, vmem=est + (8 << 20))(*args)


def _wgrad(a, b, buf, *, piece0, rows, tm, tn, tk, name, scale=None):
    t_dim, m = a.shape
    n = b.shape[1]
    db = tm // rows
    tpp = NDEV // db
    nk = t_dim // tk
    assert tm % rows == 0 and NDEV % db == 0 and m % tm == 0 and n % tn == 0 and t_dim % tk == 0
    grid = (m // tm, n // tn, nk)

    def body(a_ref, b_ref, buf_ref, o_ref, acc_ref):
        del buf_ref
        kk = pl.program_id(2)
        p = lax.dot_general(a_ref[...].astype(BF16), b_ref[...].astype(BF16), TN, preferred_element_type=F32)

        @pl.when(kk == 0)
        def _():
            acc_ref[...] = p

        @pl.when(kk > 0)
        def _():
            acc_ref[...] += p

        @pl.when(kk == nk - 1)
        def _():
            for d in range(db):
                v = acc_ref[d * rows:(d + 1) * rows, :]
                if scale is not None:
                    v = v * scale
                o_ref[0, d] = v.astype(BF16)

    est = 2 * (tk * tm * a.dtype.itemsize + tk * tn * b.dtype.itemsize + tm * tn * 2) + 3 * tm * tn * 4 + tk * tm * 4
    return _pc(
        body, name=name, grid=grid,
        in_specs=[_bs((tk, tm), lambda i, j, k: (k, i)), _bs((tk, tn), lambda i, j, k: (k, j)),
                  pl.BlockSpec(memory_space=pl.ANY)],
        out_specs=_bs((1, db, rows, tn), lambda i, j, k: (piece0 + i // tpp, i % tpp, 0, j)),
        out_shape=_sds(buf.shape, buf.dtype), scratch=[pltpu.VMEM((tm, tn), F32)], alias={2: 0},
        vmem=est + (8 << 20))(a, b, buf)


def _mm_drms(a, b, x, gain, dres, *, tm, b_row0, after, name):
    m, k_dim = a.shape
    assert m % tm == 0 and b_row0 % k_dim == 0 and b.shape[1] == D
    k_blk = b_row0 // k_dim
    after = jnp.zeros((8, 128), F32) if after is None else after

    def body(a_ref, b_ref, x_ref, g_ref, dr_ref, after_ref, dx_ref, dg_ref):
        del after_ref
        dh_v = jnp.dot(a_ref[...].astype(BF16), b_ref[...].astype(BF16), preferred_element_type=F32)
        xv = x_ref[...]
        r = lax.rsqrt(jnp.mean(xv * xv, axis=-1, keepdims=True) + EPS)
        dyg = dh_v * g_ref[...]
        mq = jnp.mean(dyg * xv, axis=-1, keepdims=True)
        dx_ref[...] = dr_ref[...] + r * (dyg - xv * (r * r * mq))

        @pl.when(pl.program_id(0) == 0)
        def _():
            dg_ref[...] = jnp.zeros_like(dg_ref)

        dg_ref[...] += jnp.sum(dh_v * (xv * r), axis=0, keepdims=True)

    row = _bs((tm, D), lambda i: (i, 0))
    est = 2 * (tm * k_dim * a.dtype.itemsize + k_dim * D * b.dtype.itemsize + 3 * tm * D * 4) + 4 * tm * D * 4
    return _pc(body, name=name, grid=(m // tm,),
               in_specs=[_bs((tm, k_dim), lambda i: (i, 0)), _bs((k_dim, D), lambda i: (k_blk, 0)), row,
                         _bs((1, D), lambda i: (0, 0)), row, pl.BlockSpec(memory_space=pl.ANY)],
               out_specs=[row, _bs((1, D), lambda i: (0, 0))], out_shape=[_sds((m, D), F32), _sds((1, D), F32)],
               vmem=est + (8 << 20))(a, b, x, gain, dres, after)


def _dact_dswiglu(dout, b, gu, *, tm, b_row0, after, name):
    t_dim = dout.shape[0]
    assert t_dim % tm == 0 and b_row0 % FF == 0
    k_blk = b_row0 // FF
    after = jnp.zeros((8, 128), F32) if after is None else after

    def body(d_ref, b_ref, gu_ref, after_ref, o_ref):
        del after_ref
        dav = 0.5 * lax.dot_general(d_ref[...].astype(BF16), b_ref[...].astype(BF16), NT, preferred_element_type=F32)
        g = gu_ref[:, :FF].astype(F32)
        u = gu_ref[:, FF:].astype(F32)
        s = _sigmoid(g)
        o_ref[:, :FF] = (dav * u * (s * (1.0 + g * (1.0 - s)))).astype(BF16)
        o_ref[:, FF:] = (dav * (g * s)).astype(BF16)

    est = 2 * (tm * D * 4 + FF * D * 2 + 2 * tm * 2 * FF * 2) + 5 * tm * FF * 4
    return _pc(body, name=name, grid=(t_dim // tm,),
               in_specs=[_bs((tm, D), lambda i: (i, 0)), _bs((FF, D), lambda i: (k_blk, 0)),
                         _bs((tm, 2 * FF), lambda i: (i, 0)), pl.BlockSpec(memory_space=pl.ANY)],
               out_specs=_bs((tm, 2 * FF), lambda i: (i, 0)), out_shape=_sds((t_dim, 2 * FF), BF16),
               vmem=est + (8 << 20))(dout, b, gu, after)


def _rms_fwd(x, gain, name, tm=512):
    t_dim = x.shape[0]

    def body(x_ref, g_ref, o_ref):
        xv = x_ref[...]
        r = lax.rsqrt(jnp.mean(xv * xv, axis=-1, keepdims=True) + EPS)
        o_ref[...] = (xv * r * g_ref[...]).astype(BF16)

    return _pc(body, name=name, grid=(t_dim // tm,),
               in_specs=[_bs((tm, D), lambda i: (i, 0)), _bs((1, D), lambda i: (0, 0))],
               out_specs=_bs((tm, D), lambda i: (i, 0)), out_shape=_sds((t_dim, D), BF16))(x, gain)


def _sigmoid(v):
    return 1.0 / (1.0 + jnp.exp(-v))


def _swiglu_fwd(gu, name, tm=256):
    t_dim = gu.shape[0]

    def body(gu_ref, o_ref):
        g = gu_ref[:, :FF].astype(F32)
        u = gu_ref[:, FF:].astype(F32)
        o_ref[...] = (g * _sigmoid(g) * u).astype(BF16)

    return _pc(body, name=name, grid=(t_dim // tm,), in_specs=[_bs((tm, 2 * FF), lambda i: (i, 0))],
               out_specs=_bs((tm, FF), lambda i: (i, 0)), out_shape=_sds((t_dim, FF), BF16))(gu)


def _final_loss(x, gain, tgt, name, tm=512):
    t_dim = x.shape[0]

    def body(x_ref, g_ref, t_ref, loss_ref, dx_ref, dg_ref):
        xv = x_ref[...]
        gv = g_ref[...]
        r = lax.rsqrt(jnp.mean(xv * xv, axis=-1, keepdims=True) + EPS)
        xn = xv * r
        err = xn * gv - t_ref[...]
        dy = err * (1.0 / D)
        dyg = dy * gv
        mq = jnp.mean(dyg * xv, axis=-1, keepdims=True)
        dx_ref[...] = r * (dyg - xv * (r * r * mq))

        @pl.when(pl.program_id(0) == 0)
        def _():
            dg_ref[...] = jnp.zeros_like(dg_ref)
            loss_ref[...] = jnp.zeros_like(loss_ref)

        dg_ref[...] += jnp.sum(dy * xn, axis=0, keepdims=True)
        part = 0.5 * jnp.sum(jnp.mean(err * err, axis=-1, keepdims=True), axis=0, keepdims=True)
        loss_ref[...] += jnp.broadcast_to(part, loss_ref.shape)

    return _pc(body, name=name, grid=(t_dim // tm,),
               in_specs=[_bs((tm, D), lambda i: (i, 0)), _bs((1, D), lambda i: (0, 0)), _bs((tm, D), lambda i: (i, 0))],
               out_specs=[_bs((8, 128), lambda i: (0, 0)), _bs((tm, D), lambda i: (i, 0)), _bs((1, D), lambda i: (0, 0))],
               out_shape=[_sds((8, 128), F32), _sds((t_dim, D), F32), _sds((1, D), F32)])(x, gain, tgt)


def _band_mask(n):
    ri = lax.broadcasted_iota(jnp.int32, (BLK, 2 * BLK), 0)
    cj = lax.broadcasted_iota(jnp.int32, (BLK, 2 * BLK), 1)
    dist = ri + BLK - cj
    return (dist >= 0) & (dist <= BLK) & ((cj >= BLK) | (n > 0))


def _halves(unit, imap_rows, col):
    return [_bs((unit, 128), (lambda u, j=j: (imap_rows(u), 2 * col + j))) for j in range(2)]


def _get_rows(pair, r, dil):
    return jnp.concatenate([h[pl.ds(r, BLK, stride=dil), :] for h in pair], axis=1)


def _put_rows(stage, r, dil, val):
    for j in range(2):
        stage[j, pl.ds(r, BLK, stride=dil), :] = val[:, 128 * j:128 * (j + 1)]


def _unstage(stage, out_ref):
    out_ref[:, :128] = stage[0]
    out_ref[:, 128:] = stage[1]


def _attn_fwd(z, acc, ml, *, gi, first, final, name):
    dil = DILS[gi]
    t_dim = z.shape[0]
    unit = BLK * dil
    nu = t_dim // unit
    own = lambda u: u
    prev = lambda u: jnp.maximum(u - 1, 0)
    in_specs = (_halves(unit, own, Q_B + gi) + _halves(unit, own, K_B + gi) + _halves(unit, prev, K_B + gi)
                + _halves(unit, own, V_B + gi) + _halves(unit, prev, V_B + gi))
    args = [z] * 10
    st_ml = _bs((unit, 128), lambda u: (u, 0))
    if not first:
        in_specs += _halves(unit, own, 0) + [st_ml]
        args += [acc, acc, ml]

    def body(*refs):
        q2, ko2, kp2, vo2, vp2 = (refs[2 * i:2 * i + 2] for i in range(5))
        o_acc_ref, o_ml_ref, stage = refs[-3], refs[-2], refs[-1]
        valid = _band_mask(pl.program_id(0))
        lane = lax.broadcasted_iota(jnp.int32, (BLK, GW), 1)
        lane_s = lax.broadcasted_iota(jnp.int32, (BLK, 128), 1)

        def band(r, carry):
            q = _get_rows(q2, r, dil) * QK_SCALE
            kb = jnp.concatenate([_get_rows(kp2, r, dil), _get_rows(ko2, r, dil)], axis=0).astype(BF16)
            vb = jnp.concatenate([_get_rows(vp2, r, dil), _get_rows(vo2, r, dil)], axis=0).astype(BF16)
            if first:
                acc_v = jnp.zeros((BLK, GW), F32)
                ml_v = jnp.where(lane_s < 64, NEG, 0.0).astype(F32)
            else:
                acc_v = _get_rows(refs[10:12], r, dil)
                ml_v = refs[12][pl.ds(r, BLK, stride=dil), :]
            heads = [(lane >= h * HD) & (lane < (h + 1) * HD) for h in range(NH)]
            rows = lambda v, h: v[h * BLK:(h + 1) * BLK]
            qs = jnp.concatenate([jnp.where(hm, q, 0.0) for hm in heads], axis=0).astype(BF16)
            s = lax.dot_general(qs, kb, NT, preferred_element_type=F32)
            s = jnp.where(jnp.concatenate([valid] * NH, axis=0), s, NEG)
            m_old = jnp.concatenate([ml_v[:, 16 * h:16 * h + 1] for h in range(NH)], axis=0)
            l_old = jnp.concatenate([ml_v[:, 64 + 16 * h:64 + 16 * h + 1] for h in range(NH)], axis=0)
            m_new = jnp.maximum(m_old, jnp.max(s, axis=-1, keepdims=True))
            alpha = jnp.exp(m_old - m_new)
            p = jnp.exp(s - m_new)
            l_new = alpha * l_old + jnp.sum(p, axis=-1, keepdims=True)
            pv = jnp.dot(p.astype(BF16), vb, preferred_element_type=F32)
            stat = m_new + jnp.log(l_new) if final else m_new
            acc_o, ml_o = acc_v, ml_v
            l_b = jnp.ones((BLK, GW), F32)
            for h, hm in enumerate(heads):
                acc_o = jnp.where(hm, acc_v * rows(alpha, h) + rows(pv, h), acc_o)
                l_b = jnp.where(hm, rows(l_new, h), l_b)
                ml_o = jnp.where((lane_s >= 16 * h) & (lane_s < 16 * h + 16), rows(stat, h), ml_o)
                if not final:
                    ml_o = jnp.where((lane_s >= 64 + 16 * h) & (lane_s < 64 + 16 * h + 16), rows(l_new, h), ml_o)
            _put_rows(stage, r, dil, acc_o / l_b if final else acc_o)
            o_ml_ref[pl.ds(r, BLK, stride=dil), :] = ml_o
            return carry

        lax.fori_loop(0, dil, band, 0, unroll=min(dil, 2))
        _unstage(stage, o_acc_ref)

    blocks = (len(in_specs) + 6) * unit * 128 * 4
    return _pc(body, name=name, grid=(nu,), in_specs=in_specs, out_specs=[_bs((unit, GW), lambda u: (u, 0)), st_ml],
               out_shape=[_sds((t_dim, GW), F32), _sds((t_dim, 128), F32)],
               scratch=[pltpu.VMEM((2, unit, 128), F32)], vmem=2 * blocks + (12 << 20))(*args)


def _attn_bwd(z, dy, y, lse, *, gi, name):
    dil = DILS[gi]
    t_dim = z.shape[0]
    unit = BLK * dil
    nu = t_dim // unit

    own = lambda u: jnp.minimum(u, nu - 1)
    prev = lambda u: jnp.maximum(jnp.minimum(u, nu - 1) - 1, 0)
    in_specs = (_halves(unit, own, Q_B + gi) + _halves(unit, own, K_B + gi) + _halves(unit, prev, K_B + gi)
                + _halves(unit, own, V_B + gi) + _halves(unit, prev, V_B + gi)
                + _halves(unit, own, 0) + _halves(unit, own, 0) + [_bs((unit, 128), lambda u: (own(u), 0))])
    tok = _bs((unit, GW), lambda u: (own(u), 0))
    kv_out = _bs((unit, GW), lambda u: (jnp.maximum(u - 1, 0), 0))

    def body(*refs):
        q2, ko2, kp2, vo2, vp2, dy2, y2 = (refs[2 * i:2 * i + 2] for i in range(7))
        lse_ref = refs[14]
        dq_ref, dk_ref, dv_ref, dq_st, dk_st, dv_st, dkc, dvc = refs[15:]
        u = pl.program_id(0)

        @pl.when(u == 0)
        def _():
            dkc[...] = jnp.zeros_like(dkc)
            dvc[...] = jnp.zeros_like(dvc)

        @pl.when(u < nu)
        def _():
            valid = _band_mask(u)
            lane = lax.broadcasted_iota(jnp.int32, (BLK, GW), 1)

            def band(r, carry):
                q = _get_rows(q2, r, dil) * QK_SCALE
                kb = jnp.concatenate([_get_rows(kp2, r, dil), _get_rows(ko2, r, dil)], axis=0).astype(BF16)
                vb = jnp.concatenate([_get_rows(vp2, r, dil), _get_rows(vo2, r, dil)], axis=0).astype(BF16)
                dyv = _get_rows(dy2, r, dil)
                yv = _get_rows(y2, r, dil)
                lse_v = lse_ref[pl.ds(r, BLK, stride=dil), :]
                heads = [(lane >= h * HD) & (lane < (h + 1) * HD) for h in range(NH)]
                qs = jnp.concatenate([jnp.where(hm, q, 0.0) for hm in heads], axis=0).astype(BF16)
                dys = jnp.concatenate([jnp.where(hm, dyv, 0.0) for hm in heads], axis=0)
                lse_s = jnp.concatenate([lse_v[:, 16 * h:16 * h + 1] for h in range(NH)], axis=0)
                s = lax.dot_general(qs, kb, NT, preferred_element_type=F32)
                p = jnp.where(jnp.concatenate([valid] * NH, axis=0), jnp.exp(s - lse_s), 0.0)
                delta = jnp.sum(dys * jnp.concatenate([yv] * NH, axis=0), axis=-1, keepdims=True)
                dys16 = dys.astype(BF16)
                dp = lax.dot_general(dys16, vb, NT, preferred_element_type=F32)
                ds = (p * (dp - delta)).astype(BF16)
                dq_all = jnp.dot(ds, kb, preferred_element_type=F32)
                dq = jnp.zeros((BLK, GW), F32)
                for h, hm in enumerate(heads):
                    dq = jnp.where(hm, dq_all[h * BLK:(h + 1) * BLK], dq)
                dkb = lax.dot_general(ds, qs, TN, preferred_element_type=F32)
                dvb = lax.dot_general(p.astype(BF16), dys16, TN, preferred_element_type=F32)
                _put_rows(dq_st, r, dil, dq * QK_SCALE)
                _put_rows(dk_st, r, dil, _get_rows((dkc.at[0], dkc.at[1]), r, dil) + dkb[:BLK])
                _put_rows(dv_st, r, dil, _get_rows((dvc.at[0], dvc.at[1]), r, dil) + dvb[:BLK])
                _put_rows(dkc, r, dil, dkb[BLK:])
                _put_rows(dvc, r, dil, dvb[BLK:])
                return carry

            lax.fori_loop(0, dil, band, 0, unroll=min(dil, 2))
            _unstage(dq_st, dq_ref)

            @pl.when(u > 0)
            def _():
                _unstage(dk_st, dk_ref)
                _unstage(dv_st, dv_ref)

        @pl.when(u == nu)
        def _():
            _unstage(dkc, dk_ref)
            _unstage(dvc, dv_ref)

    blocks = (15 + 6 + 5) * unit * 128 * 4 * 2
    return _pc(body, name=name, grid=(nu + 1,), in_specs=in_specs, out_specs=[tok, kv_out, kv_out],
               out_shape=[_sds((t_dim, GW), F32)] * 3, scratch=[pltpu.VMEM((2, unit, 128), F32)] * 5,
               vmem=blocks + (12 << 20))(*([z] * 10), dy, dy, y, y, lse)


_INV_SQRT2 = 1.0 / math.sqrt(2.0)
_INV_SQRT2PI = 1.0 / math.sqrt(2.0 * math.pi)


def _gelu(v):
    return 0.5 * v * (1.0 + lax.erf(v * _INV_SQRT2))


def _gelu_grad(v):
    return 0.5 * (1.0 + lax.erf(v * _INV_SQRT2)) + v * (_INV_SQRT2PI * jnp.exp(-0.5 * v * v))


def _gmlp_core(z_refs, lng, lnb, ws_ref, bs_ref):
    u_pre = jnp.concatenate([z_refs[0][...], z_refs[1][...]], axis=1)
    v_pre = jnp.concatenate([z_refs[2][...], z_refs[3][...]], axis=1)
    u = _gelu(u_pre)
    v = _gelu(v_pre)
    xc = v - jnp.mean(v, axis=-1, keepdims=True)
    rstd = lax.rsqrt(jnp.mean(xc * xc, axis=-1, keepdims=True) + EPS)
    xhat = xc * rstd
    vn = xhat * lng + lnb
    ti = lax.broadcasted_iota(jnp.int32, (BLK, BLK), 0)
    si = lax.broadcasted_iota(jnp.int32, (BLK, BLK), 1)
    causal = ti >= si
    vn16 = vn.astype(BF16)
    ws = [jnp.where(causal, ws_ref[g], 0.0).astype(BF16) for g in range(4)]
    mixed = jnp.concatenate(
        [jnp.dot(ws[g], vn16[:, g * BLK:(g + 1) * BLK], preferred_element_type=F32) + bs_ref[g] for g in range(4)], axis=1)
    return u_pre, v_pre, u, xhat, rstd, vn16, ws, causal, mixed


def _gmlp_specs():
    z_specs = [_bs((BLK, GW), (lambda i, c=c: (i, UV_B + c))) for c in range(4)]
    par_specs = [_bs((1, BW), lambda i: (0, 0)), _bs((1, BW), lambda i: (0, 0)),
                 _bs((4, BLK, BLK), lambda i: (0, 0, 0)), _bs((4, BLK, 1), lambda i: (0, 0, 0))]
    return z_specs, par_specs


def _gmlp_fwd(z, lng, lnb, ws, bs, name):
    t_dim = z.shape[0]
    z_specs, par_specs = _gmlp_specs()

    def body(z0, z1, z2, z3, lng_ref, lnb_ref, ws_ref, bs_ref, o_ref):
        core = _gmlp_core((z0, z1, z2, z3), lng_ref[...], lnb_ref[...], ws_ref, bs_ref)
        o_ref[...] = (core[2] * core[8]).astype(BF16)

    return _pc(body, name=name, grid=(t_dim // BLK,), in_specs=z_specs + par_specs,
               out_specs=_bs((BLK, BW), lambda i: (i, 0)), out_shape=_sds((t_dim, BW), BF16))(z, z, z, z, lng, lnb, ws, bs)


def _gmlp_bwd(z, dy, lng, lnb, ws, bs, name):
    t_dim = z.shape[0]
    z_specs, par_specs = _gmlp_specs()

    def body(z0, z1, z2, z3, lng_ref, lnb_ref, ws_ref, bs_ref, dy_ref, duv_ref, dws_ref, dbs_ref, dlng_ref, dlnb_ref):
        lng_v = lng_ref[...]
        u_pre, v_pre, u, xhat, rstd, vn16, wsm, causal, mixed = _gmlp_core((z0, z1, z2, z3), lng_v, lnb_ref[...], ws_ref, bs_ref)
        dyv = dy_ref[...].astype(F32)
        du = dyv * mixed
        dmixed = dyv * u

        @pl.when(pl.program_id(0) == 0)
        def _():
            dws_ref[...] = jnp.zeros_like(dws_ref)
            dbs_ref[...] = jnp.zeros_like(dbs_ref)
            dlng_ref[...] = jnp.zeros_like(dlng_ref)
            dlnb_ref[...] = jnp.zeros_like(dlnb_ref)

        dvn_parts = []
        for g in range(4):
            dm = dmixed[:, g * BLK:(g + 1) * BLK]
            dm16 = dm.astype(BF16)
            dw = lax.dot_general(dm16, vn16[:, g * BLK:(g + 1) * BLK], NT, preferred_element_type=F32)
            dws_ref[g] += jnp.where(causal, dw, 0.0)
            dbs_ref[g] += jnp.sum(dm.T, axis=0, keepdims=True)
            dvn_parts.append(lax.dot_general(wsm[g], dm16, TN, preferred_element_type=F32))
        dvn = jnp.concatenate(dvn_parts, axis=1)
        dlng_ref[...] += jnp.sum(dvn * xhat, axis=0, keepdims=True)
        dlnb_ref[...] += jnp.sum(dvn, axis=0, keepdims=True)
        dxh = dvn * lng_v
        dv = rstd * (dxh - jnp.mean(dxh, axis=-1, keepdims=True) - xhat * jnp.mean(dxh * xhat, axis=-1, keepdims=True))
        duv_ref[:, :BW] = (du * _gelu_grad(u_pre)).astype(BF16)
        duv_ref[:, BW:] = (dv * _gelu_grad(v_pre)).astype(BF16)

    return _pc(body, name=name, grid=(t_dim // BLK,),
               in_specs=z_specs + par_specs + [_bs((BLK, BW), lambda i: (i, 0))],
               out_specs=[_bs((BLK, 2 * BW), lambda i: (i, 0)), _bs((4, BLK, BLK), lambda i: (0, 0, 0)),
                          _bs((4, 1, BLK), lambda i: (0, 0, 0)), _bs((1, BW), lambda i: (0, 0)), _bs((1, BW), lambda i: (0, 0))],
               out_shape=[_sds((t_dim, 2 * BW), BF16), _sds((4, BLK, BLK), F32), _sds((4, 1, BLK), F32),
                          _sds((1, BW), F32), _sds((1, BW), F32)])(z, z, z, z, lng, lnb, ws, bs, dy)


def _pool_core(x, gi):
    t_dim = x.shape[0]
    w = jnp.left_shift(2, gi)
    row = lax.broadcasted_iota(jnp.int32, x.shape, 0)
    s = x
    for k in (1, 2, 4, 8):
        sh = jnp.where(row >= k, pltpu.roll(s, k, 0), 0.0)
        s = jnp.where(k < w, s + sh, s)
    cnt = jnp.minimum(row + 1, w).astype(F32)
    return s / cnt - x, cnt, row, w, t_dim


def _pool_fwd(z, cw, cs, name):
    t_dim = z.shape[0]

    def body(z_ref, cw_ref, cs_ref, o_ref):
        pooled = _pool_core(z_ref[...], pl.program_id(0))[0]
        y = jnp.dot(pooled.astype(BF16), cw_ref[0].astype(BF16), preferred_element_type=F32) * cs_ref[0]
        o_ref[...] = y.astype(BF16)

    return _pc(body, name=name, grid=(4,),
               in_specs=[_bs((t_dim, BLK), lambda g: (0, C_B128 + g)), _bs((1, BLK, BLK), lambda g: (g, 0, 0)),
                         _bs((1, 1, BLK), lambda g: (g, 0, 0))],
               out_specs=_bs((t_dim, BLK), lambda g: (0, g)), out_shape=_sds((t_dim, BW), BF16),
               vmem=40 << 20)(z, cw, cs)


def _pool_bwd(z, dy, cw, cs, name):
    t_dim = z.shape[0]

    def body(z_ref, dy_ref, cw_ref, cs_ref, dx_ref, dcw_ref, dcs_ref):
        pooled, cnt, row, w, _ = _pool_core(z_ref[...], pl.program_id(0))
        p16 = pooled.astype(BF16)
        cw16 = cw_ref[0].astype(BF16)
        dyv = dy_ref[...].astype(F32)
        lin = jnp.dot(p16, cw16, preferred_element_type=F32)
        dcs_ref[0] = jnp.sum(dyv * lin, axis=0, keepdims=True)
        dys = (dyv * cs_ref[0]).astype(BF16)
        dcw_ref[0] = lax.dot_general(p16, dys, TN, preferred_element_type=F32)
        dpool = lax.dot_general(dys, cw16, NT, preferred_element_type=F32)
        sb = dpool / cnt
        for k in (1, 2, 4, 8):
            sh = jnp.where(row < t_dim - k, pltpu.roll(sb, t_dim - k, 0), 0.0)
            sb = jnp.where(k < w, sb + sh, sb)
        dx_ref[...] = (sb - dpool).astype(BF16)

    return _pc(body, name=name, grid=(4,),
               in_specs=[_bs((t_dim, BLK), lambda g: (0, C_B128 + g)), _bs((t_dim, BLK), lambda g: (0, g)),
                         _bs((1, BLK, BLK), lambda g: (g, 0, 0)), _bs((1, 1, BLK), lambda g: (g, 0, 0))],
               out_specs=[_bs((t_dim, BLK), lambda g: (0, g)), _bs((1, BLK, BLK), lambda g: (g, 0, 0)),
                          _bs((1, 1, BLK), lambda g: (g, 0, 0))],
               out_shape=[_sds((t_dim, BW), BF16), _sds((4, BLK, BLK), F32), _sds((4, 1, BLK), F32)],
               vmem=48 << 20)(z, dy, cw, cs)


def _gates_specs(tm):
    z_specs = [_bs((tm, GW), (lambda j, i, b=b: (i, G_B + 4 * b + j))) for b in range(3)]
    bg_specs = [_bs((1, GW), (lambda j, i, b=b: (0, 4 * b + j))) for b in range(3)]
    tile = _bs((tm, GW), lambda j, i: (i, j))
    return z_specs, bg_specs, tile


def _gates_fwd(z, bg, pa, pb, pc, name, tm=512):
    t_dim = z.shape[0]
    z_specs, bg_specs, tile = _gates_specs(tm)

    def body(za, zb, zc, ba, bb, bc, pa_ref, pb_ref, pc_ref, o_ref):
        o_ref[...] = (_sigmoid(za[...] + ba[...]) * pa_ref[...] + _sigmoid(zb[...] + bb[...]) * pb_ref[...]
                      + _sigmoid(zc[...] + bc[...]) * pc_ref[...]).astype(BF16)

    return _pc(body, name=name, grid=(D // GW, t_dim // tm), in_specs=z_specs + bg_specs + [tile] * 3,
               out_specs=tile, out_shape=_sds((t_dim, D), BF16))(z, z, z, bg, bg, bg, pa, pb, pc)


def _gates_bwd(z, bg, pa, pb, pc, dm, name, tm=512):
    t_dim = z.shape[0]
    z_specs, bg_specs, tile = _gates_specs(tm)
    bias_out = _bs((1, GW), lambda j, i: (0, j))

    def body(za, zb, zc, ba, bb, bc, pa_ref, pb_ref, pc_ref, dm_ref, dpa, dpb, dpc, dza, dzb, dzc, dba, dbb, dbc):
        dmv = dm_ref[...]
        first = pl.program_id(1) == 0
        for z_ref, b_ref, p_ref, dp_ref, dz_ref, db_ref in ((za, ba, pa_ref, dpa, dza, dba), (zb, bb, pb_ref, dpb, dzb, dbb),
                                                            (zc, bc, pc_ref, dpc, dzc, dbc)):
            gate = _sigmoid(z_ref[...] + b_ref[...])
            dp_ref[...] = (gate * dmv).astype(BF16)
            dz = dmv * p_ref[...] * (gate * (1.0 - gate))
            dz_ref[...] = dz.astype(BF16)

            @pl.when(first)
            def _():
                db_ref[...] = jnp.zeros_like(db_ref)

            db_ref[...] += jnp.sum(dz, axis=0, keepdims=True)

    return _pc(body, name=name, grid=(D // GW, t_dim // tm), in_specs=z_specs + bg_specs + [tile] * 4,
               out_specs=[tile] * 6 + [bias_out] * 3,
               out_shape=[_sds((t_dim, D), BF16)] * 6 + [_sds((1, D), F32)] * 3)(z, z, z, bg, bg, bg, pa, pb, pc, dm)


def _coords():
    return lax.axis_index("x"), lax.axis_index("y"), lax.axis_index("c")


def _peer(k, x, y, c):
    px = 1 - x if k & 4 else x
    py = 1 - y if k & 2 else y
    pc = 1 - c if k & 1 else c
    return (px, py, pc), 4 * px + 2 * py + pc


def _exchange_copies(kind, bufs, send, recv):
    x, y, c = _coords()
    me = 4 * x + 2 * y + c
    na = len(bufs) if kind == "gather" else len(bufs) // 2
    out = []
    for k in range(1, NDEV):
        to, peer = _peer(k, x, y, c)
        for a in range(na):
            sems = dict(send_sem=send.at[a * 7 + k - 1], recv_sem=recv.at[a * 7 + k - 1], device_id=to, device_id_type=MESH)
            if kind == "gather":
                mine = pltpu.make_async_remote_copy(src_ref=bufs[a].at[:, me], dst_ref=bufs[a].at[:, me], **sems)
                theirs = pltpu.make_async_remote_copy(src_ref=bufs[a].at[:, me], dst_ref=bufs[a].at[:, peer], **sems)
            else:
                src, land = bufs[a], bufs[na + a]
                mine = pltpu.make_async_remote_copy(src_ref=src.at[:, peer], dst_ref=land.at[me], **sems)
                theirs = pltpu.make_async_remote_copy(src_ref=src.at[:, peer], dst_ref=land.at[peer], **sems)
            out.append((mine, theirs))
    return out


_HBM = pl.BlockSpec(memory_space=pltpu.HBM)
_SEM = pl.BlockSpec(memory_space=pltpu.SEMAPHORE)
_EFFECT = pltpu.SideEffectType.DATAFLOW_SIDE_EFFECTING


def _exchange_start(kind, arrays, after, name):
    na = len(arrays)
    if kind == "scatter":
        arrays = list(arrays) + [lax.empty((NDEV, s.shape[0]) + s.shape[2:], s.dtype) for s in arrays]
    nb = len(arrays)

    def body(*refs):
        bufs = refs[:nb]
        send, recv = refs[nb + 1], refs[nb + 2]
        token = refs[-1]
        for mine, _ in _exchange_copies(kind, bufs, send, recv):
            mine.start()
        token[...] = jnp.zeros_like(token)

    arrays = [pltpu.with_memory_space_constraint(b, pltpu.HBM) for b in arrays]
    outs = pl.pallas_call(
        body, name=name,
        out_shape=(pltpu.SemaphoreType.DMA((7 * na,)), pltpu.SemaphoreType.DMA((7 * na,)),
                   *[pltpu.HBM(b.shape, b.dtype) for b in arrays], _sds((8, 128), F32)),
        in_specs=[_HBM] * nb + [pl.BlockSpec(memory_space=pl.ANY)],
        out_specs=(_SEM, _SEM, *([_HBM] * nb), pl.BlockSpec(memory_space=pltpu.VMEM)),
        input_output_aliases={i: 2 + i for i in range(nb)},
        compiler_params=pltpu.CompilerParams(has_side_effects=_EFFECT), interpret=False)(*arrays, after)
    return outs[0], outs[1], list(outs[2:2 + nb]), outs[-1]


def _exchange_wait(kind, started, after, name):
    send, recv, arrays, _ = started
    nb = len(arrays)

    def body(*refs):
        for mine, theirs in _exchange_copies(kind, refs[:nb], refs[nb], refs[nb + 1]):
            mine.wait_send()
            theirs.wait_recv()

    outs = pl.pallas_call(
        body, name=name,
        out_shape=tuple(pltpu.HBM(b.shape, b.dtype) for b in arrays),
        in_specs=[_HBM] * nb + [_SEM, _SEM, pl.BlockSpec(memory_space=pl.ANY)],
        out_specs=tuple([_HBM] * nb),
        input_output_aliases={i: i for i in range(nb)},
        compiler_params=pltpu.CompilerParams(has_side_effects=_EFFECT), interpret=False)(*arrays, send, recv, after)
    return list(outs)


def _adam_math(w, g, m, v):
    m2 = ADAM_B1 * m + (1.0 - ADAM_B1) * g
    v2 = ADAM_B2 * v + (1.0 - ADAM_B2) * jnp.square(g)
    m_hat = m2 / (1.0 - ADAM_B1 ** ADAM_STEP)
    v_hat = v2 / (1.0 - ADAM_B2 ** ADAM_STEP)
    return -ADAM_LR * (m_hat / (jnp.sqrt(v_hat) + ADAM_EPS) + ADAM_WD * w), m2, v2


def _reduce_update(landed, own, me, piece, w, m, v, layer, prev, transposed, name):
    rows, cols = landed.shape[2], landed.shape[3]
    tc = 256
    assert cols % tc == 0 and w.shape[1:] == ((cols, rows) if transposed else (rows, cols))
    lane_chunks = [(c0, min(128, rows - c0)) for c0 in range(0, rows, 128)]

    def body(me_ref, b_ref, own_ref, w_ref, m_ref, v_ref, *rest):
        outs = rest[-4:]
        mine = me_ref[0]
        s = None
        for d in range(NDEV):
            other = jnp.where(mine == d, (d + 1) % NDEV, d)
            part = jnp.where(mine == d, own_ref[0, 0], b_ref[other, 0]).astype(F32)
            s = part if s is None else s + part
        if not transposed:
            for o, val in zip(outs, (s,) + _adam_math(w_ref[0], s, m_ref[0], v_ref[0])):
                o[0] = val
            return
        for c0, wd in lane_chunks:
            chunk = s[c0:c0 + wd]
            if wd < 128:
                chunk = jnp.concatenate([chunk, jnp.zeros((128 - wd, tc), F32)], axis=0)
            g = chunk.T[:, :wd]
            sl = (0, slice(None), slice(c0, c0 + wd))
            for o, val in zip(outs, (g,) + _adam_math(w_ref[sl], g, m_ref[sl], v_ref[sl])):
                o[sl] = val

    if transposed:
        nat = _bs((1, tc, rows), lambda j, me_ref: (layer, j, 0))
    else:
        nat = _bs((1, rows, tc), lambda j, me_ref: (layer, 0, j))
    in_specs = [_bs((NDEV, 1, rows, tc), lambda j, me_ref: (0, piece, 0, j)),
                _bs((1, 1, rows, tc), lambda j, me_ref: (piece, me_ref[0], 0, j)), nat, nat, nat]
    args = [me, landed, own, w, m, v]
    alias = {}
    if prev is not None:
        in_specs += [pl.BlockSpec(memory_space=pl.ANY)] * 4
        args += list(prev)
        alias = {6 + i: i for i in range(4)}
    grid_spec = pltpu.PrefetchScalarGridSpec(num_scalar_prefetch=1, grid=(cols // tc,), in_specs=in_specs, out_specs=[nat] * 4)
    return pl.pallas_call(body, name=name, grid_spec=grid_spec, out_shape=[_sds(w.shape, F32)] * 4,
                          input_output_aliases=alias, interpret=False)(*args)


SMALL_LAYER = ("ffn1_norm", "mix_norm", "b_gate", "b_ln_g", "b_ln_b", "b_w_s", "b_b_s", "c_w", "c_scale", "ffn2_norm")
SMALL_ROWS = {"ffn1_norm": 8, "mix_norm": 8, "b_gate": 24, "b_ln_g": 4, "b_ln_b": 4, "b_w_s": 512, "b_b_s": 4, "c_w": 512,
              "c_scale": 4, "ffn2_norm": 8, "final_norm": 8}
LAYER_ROWS = sum(SMALL_ROWS[k] for k in SMALL_LAYER)


def _small_offset(name):
    if name == "final_norm":
        return LAYER_ROWS
    return sum(SMALL_ROWS[k] for k in SMALL_LAYER[:SMALL_LAYER.index(name)])


def _pack_small_grads(layer_grads, final, me, name):
    flat, where = [], []
    for k in SMALL_LAYER:
        parts = layer_grads[k] if k == "b_gate" else [layer_grads[k]]
        for i, part in enumerate(parts):
            flat.append(part)
            where.append(_small_offset(k) + i * (D // 128))
    if final is not None:
        flat.append(final)
        where.append(_small_offset("final_norm"))
    pack_rows = LAYER_ROWS + (SMALL_ROWS["final_norm"] if final is not None else 0)

    def body(me_ref, *refs):
        del me_ref
        o_ref = refs[-1]
        for ref, off in zip(refs[:-1], where):
            if len(ref.shape) == 2:
                for j in range(ref.shape[1] // 128):
                    o_ref[0, 0, off + j:off + j + 1, :] = ref[:, 128 * j:128 * (j + 1)]
            else:
                rows = ref.shape[1]
                for g in range(4):
                    o_ref[0, 0, off + g * rows:off + (g + 1) * rows, :] = ref[g]

    grid_spec = pltpu.PrefetchScalarGridSpec(
        num_scalar_prefetch=1, grid=(1,),
        in_specs=[pl.BlockSpec(a.shape, (lambda i, me_ref, nd=a.ndim: (0,) * nd)) for a in flat],
        out_specs=_bs((1, 1, pack_rows, 128), lambda i, me_ref: (0, me_ref[0], 0, 0)))
    return pl.pallas_call(body, name=name, grid_spec=grid_spec, out_shape=_sds((1, NDEV, pack_rows, 128), F32),
                          interpret=False)(me, *flat)


def _adamw_small(gathered, w, mom, var, name):
    names = SMALL_LAYER + ("final_norm",)
    ins = list(gathered) + [t[k] for k in names for t in (w, mom, var)]

    def body(*refs):
        outs = refs[len(ins):]

        def rows(l, off, n):
            land = refs[l]
            s = land[0, 0, off:off + n, :]
            for d in range(1, NDEV):
                s = s + land[0, d, off:off + n, :]
            return s

        def as_row(l, off, n):
            s = rows(l, off, n)
            return jnp.concatenate([s[j:j + 1, :] for j in range(n)], axis=1)

        def update(i, idx, g):
            wr, mr, vr = (refs[DEPTH + 3 * i + t] for t in range(3))
            d, m2, v2 = _adam_math(wr[idx], g, mr[idx], vr[idx])
            for o, val in zip(outs[4 * i:4 * i + 4], (g, d, m2, v2)):
                o[idx] = val

        for i, k in enumerate(names):
            n = SMALL_ROWS[k]
            off = _small_offset(k)
            if k == "final_norm":
                update(i, (slice(0, 1),), as_row(DEPTH - 1, off, n))
                continue
            for l in range(DEPTH):
                if k in ("b_w_s", "c_w"):
                    for g in range(4):
                        update(i, (l, g), rows(l, off + g * BLK, BLK))
                elif k in ("b_b_s", "c_scale"):
                    update(i, (l,), rows(l, off, n))
                else:
                    update(i, (slice(l, l + 1),), as_row(l, off, n))

    out_shape = [_sds(w[k].shape, F32) for k in names for _ in range(4)]
    outs = pl.pallas_call(body, name=name, out_shape=out_shape,
                          compiler_params=pltpu.CompilerParams(vmem_limit_bytes=48 << 20), interpret=False)(*ins)
    return tuple({k: outs[4 * i + t] for i, k in enumerate(names)} for t in range(4))


PG, PD = 0, 2


def _ffn_fwd(x, gain, wf, tag):
    h = _rms_fwd(x, gain, f"rms_{tag}")
    gu = _mm(h, wf, mode="nt", tm=2048, tn=512, tk=D, n=2 * FF, b_row0=PG * FF, out_dtype=BF16, name=f"gu_{tag}")
    act = _swiglu_fwd(gu, f"swiglu_{tag}")
    out = _mm(act, wf, mode="nn", tm=512, tn=D, tk=FF, n=D, b_row0=PD * FF, out_dtype=F32, scale=0.5, res=x, name=f"down_{tag}")
    return out, (x, h, gu, act)


def _ffn_bwd(dout, saved, gain, wf, tag, after, emit):
    x, h, gu, act = saved
    grf = lax.empty((3, NDEV, FFS, D), BF16)
    dgu = _dact_dswiglu(dout, wf, gu, tm=256, b_row0=PD * FF, after=after, name=f"dgu_{tag}")
    grf = _wgrad(act, dout, grf, piece0=PD, rows=FFS, tm=FF // 2, tn=D, tk=1024, scale=0.5, name=f"wg_down_{tag}")
    grf = _wgrad(dgu, h, grf, piece0=PG, rows=FFS, tm=FF // 2, tn=D, tk=2048, name=f"wg_gu_{tag}")
    token = emit([grf])
    dx, dgain = _mm_drms(dgu, wf, x, gain, dout, tm=256, b_row0=PG * FF, after=token, name=f"dx_{tag}")
    return dx, dgain, token


def _mixer_fwd(x, sp, ww, wo, wa, wbc, tag):
    h = _rms_fwd(x, sp["mix_norm"], f"rms_mix_{tag}")
    z = _mm(h, ww, mode="nt", tm=1024, tn=1152, tk=D, n=INW, out_dtype=F32, name=f"z_{tag}")
    acc, ml = _attn_fwd(z, None, None, gi=0, first=True, final=False, name=f"attn0_{tag}")
    acc, ml = _attn_fwd(z, acc, ml, gi=1, first=False, final=False, name=f"attn1_{tag}")
    ya, lse = _attn_fwd(z, acc, ml, gi=2, first=False, final=True, name=f"attn2_{tag}")
    yb = _gmlp_fwd(z, sp["b_ln_g"], sp["b_ln_b"], sp["b_w_s"], sp["b_b_s"], f"gmlp_{tag}")
    yc = _pool_fwd(z, sp["c_w"], sp["c_scale"], f"pool_{tag}")
    pa = _mm(ya, wa, mode="nt", tm=1024, tn=D, tk=GW, n=D, out_dtype=F32, name=f"proj_a_{tag}")
    pb = _mm(yb, wbc, mode="nt", tm=1024, tn=D, tk=BW, n=D, b_row0=0, out_dtype=F32, name=f"proj_b_{tag}")
    pc = _mm(yc, wbc, mode="nt", tm=1024, tn=D, tk=BW, n=D, b_row0=D, out_dtype=F32, name=f"proj_c_{tag}")
    merged = _gates_fwd(z, sp["b_gate"], pa, pb, pc, f"gates_{tag}")
    out = _mm(merged, wo, mode="nn", tm=1024, tn=D, tk=D, n=D, out_dtype=F32, res=x, name=f"out_{tag}")
    return out, (x, h, z, ya, lse, yb, yc, pa, pb, pc, merged)


def _mixer_bwd(dout, saved, sp, ww, wo, wa, wbc, tag, after, emit):
    x, h, z, ya, lse, yb, yc, pa, pb, pc, merged = saved
    grw = lax.empty((1, NDEV, INS, D), BF16)
    gro = lax.empty((1, NDEV, DS, D), BF16)
    gra = lax.empty((1, NDEV, DS, GW), BF16)
    grbc = lax.empty((2, NDEV, DS, BW), BF16)
    dm = _mm(dout, wo, mode="nt", tm=1024, tn=D, tk=D, n=D, out_dtype=F32, after=after, name=f"dmerged_{tag}")
    gro = _wgrad(merged, dout, gro, piece0=0, rows=DS, tm=D, tn=D, tk=1024, name=f"wg_out_{tag}")
    dpa, dpb, dpc, dza, dzb, dzc, dba, dbb, dbc = _gates_bwd(z, sp["b_gate"], pa, pb, pc, dm, f"dgates_{tag}")
    dya = _mm(dpa, wa, mode="nn", tm=2048, tn=GW, tk=D, n=GW, out_dtype=F32, name=f"dya_{tag}")
    dyb = _mm(dpb, wbc, mode="nn", tm=2048, tn=BW, tk=D, n=BW, b_row0=0, out_dtype=F32, name=f"dyb_{tag}")
    dyc = _mm(dpc, wbc, mode="nn", tm=2048, tn=BW, tk=D, n=BW, b_row0=D, out_dtype=F32, name=f"dyc_{tag}")
    gra = _wgrad(dpa, ya, gra, piece0=0, rows=DS, tm=D, tn=GW, tk=1024, name=f"wg_pa_{tag}")
    grbc = _wgrad(dpb, yb, grbc, piece0=0, rows=DS, tm=D, tn=BW, tk=1024, name=f"wg_pb_{tag}")
    grbc = _wgrad(dpc, yc, grbc, piece0=1, rows=DS, tm=D, tn=BW, tk=1024, name=f"wg_pc_{tag}")
    dqkv = [_attn_bwd(z, dya, ya, lse, gi=gi, name=f"dattn{gi}_{tag}") for gi in range(3)]
    duv, dws, dbs, dlng, dlnb = _gmlp_bwd(z, dyb, sp["b_ln_g"], sp["b_ln_b"], sp["b_w_s"], sp["b_b_s"], f"dgmlp_{tag}")
    dxc, dcw, dcs = _pool_bwd(z, dyc, sp["c_w"], sp["c_scale"], f"dpool_{tag}")
    dz = jnp.concatenate([dqkv[gi][part].astype(BF16) for part in range(3) for gi in range(3)] + [duv, dxc, dza, dzb, dzc],
                         axis=1)
    grw = _wgrad(dz, h, grw, piece0=0, rows=INS, tm=INW // 2, tn=512, tk=1024, name=f"wg_in_{tag}")
    token = emit([grw, gro, gra, grbc])
    dx, dgain = _mm_drms(dz, ww, x, sp["mix_norm"], dout, tm=256, b_row0=0, after=token, name=f"dx_mix_{tag}")
    small = {"mix_norm": dgain, "b_gate": [dba, dbb, dbc], "b_ln_g": dlng, "b_ln_b": dlnb,
             "b_w_s": dws, "b_b_s": dbs, "c_w": dcw, "c_scale": dcs}
    return dx, small, token


PARTS = ("f1", "mix", "f2")


def _forward(xs, tgt, weights_of, small, final_gain):
    saved, views = [], []
    cur = xs
    for l in range(DEPTH):
        (wf1,) = weights_of(l, "f1", cur)
        cur, s1 = _ffn_fwd(cur, small[l]["ffn1_norm"], wf1, f"f1l{l}")
        wmix = weights_of(l, "mix", cur)
        cur, s2 = _mixer_fwd(cur, small[l], *wmix, f"l{l}")
        (wf2,) = weights_of(l, "f2", cur)
        cur, s3 = _ffn_fwd(cur, small[l]["ffn2_norm"], wf2, f"f2l{l}")
        saved.append((s1, s2, s3))
        views.append((wf1, wmix, wf2))
    loss_part, dcur, dfinal = _final_loss(cur, final_gain, tgt, "final_loss")
    return loss_part, dcur, dfinal, saved, views


def _backward_layer(dcur, saved_l, small_l, views_l, l, emit, after):
    wf1, wmix, wf2 = views_l
    s1, s2, s3 = saved_l
    dcur, dn2, token = _ffn_bwd(dcur, s3, small_l["ffn2_norm"], wf2, f"f2l{l}", after, lambda g: emit("f2", g))
    dcur, sg, token = _mixer_bwd(dcur, s2, small_l, *wmix, f"l{l}", token, lambda g: emit("mix", g))
    dcur, dn1, token = _ffn_bwd(dcur, s1, small_l["ffn1_norm"], wf1, f"f1l{l}", token, lambda g: emit("f1", g))
    sg["ffn1_norm"] = dn1
    sg["ffn2_norm"] = dn2
    return dcur, sg, token


BIG = ("ffn1_w_gate", "ffn1_w_up", "ffn1_w_down", "w_in", "w_proj_a", "w_proj_b", "w_proj_c", "w_out",
       "ffn2_w_gate", "ffn2_w_up", "ffn2_w_down")
ALL_WEIGHTS = ("ffn1_norm", "ffn1_w_gate", "ffn1_w_up", "ffn1_w_down", "mix_norm", "w_in", "b_gate", "b_ln_g", "b_ln_b",
               "b_w_s", "b_b_s", "c_w", "c_scale", "w_proj_a", "w_proj_b", "w_proj_c", "w_out", "ffn2_norm", "ffn2_w_gate",
               "ffn2_w_up", "ffn2_w_down", "final_norm")


def _part_shards(w, l, part):
    t16 = lambda a: a.T.astype(BF16)
    if part == "mix":
        return [t16(w["w_in"][l])[None], w["w_out"][l].astype(BF16)[None], t16(w["w_proj_a"][l])[None],
                jnp.stack([t16(w["w_proj_b"][l]), t16(w["w_proj_c"][l])])]
    f = "ffn1" if part == "f1" else "ffn2"
    return [jnp.stack([t16(w[f + "_w_gate"][l]), t16(w[f + "_w_up"][l]), w[f + "_w_down"][l].astype(BF16)])]


def _part_views(part, gathered):
    if part == "mix":
        pw, po, pa, pbc = gathered
        return [pw.reshape(INW, D), po.reshape(D, D), pa.reshape(D, GW), pbc.reshape(2 * D, BW)]
    return [gathered[0].reshape(3 * FF, D)]


def _part_pieces(part):
    if part == "mix":
        return [(0, 0, "w_in", True), (1, 0, "w_out", False), (2, 0, "w_proj_a", True), (3, 0, "w_proj_b", True),
                (3, 1, "w_proj_c", True)]
    f = "ffn1" if part == "f1" else "ffn2"
    return [(0, 0, f + "_w_gate", True), (0, 1, f + "_w_up", True), (0, 2, f + "_w_down", False)]


def _layer_small(w, l):
    r = lambda a, shape: a.reshape(shape)
    return {"ffn1_norm": r(w["ffn1_norm"][l], (1, D)), "mix_norm": r(w["mix_norm"][l], (1, D)),
            "b_gate": r(w["b_gate"][l], (1, 3 * D)), "b_ln_g": r(w["b_ln_g"][l], (1, BW)), "b_ln_b": r(w["b_ln_b"][l], (1, BW)),
            "b_w_s": w["b_w_s"][l], "b_b_s": r(w["b_b_s"][l], (4, BLK, 1)), "c_w": w["c_w"][l],
            "c_scale": r(w["c_scale"][l], (4, 1, BLK)), "ffn2_norm": r(w["ffn2_norm"][l], (1, D))}


def kernel(x, ffn1_norm, ffn1_w_gate, ffn1_w_up, ffn1_w_down, mix_norm, w_in, b_gate, b_ln_g, b_ln_b, b_w_s, b_b_s, c_w, c_scale, w_proj_a, w_proj_b, w_proj_c, w_out, ffn2_norm, ffn2_w_gate, ffn2_w_up, ffn2_w_down, final_norm, loss_target, m_ffn1_norm, m_ffn1_w_gate, m_ffn1_w_up, m_ffn1_w_down, m_mix_norm, m_w_in, m_b_gate, m_b_ln_g, m_b_ln_b, m_b_w_s, m_b_b_s, m_c_w, m_c_scale, m_w_proj_a, m_w_proj_b, m_w_proj_c, m_w_out, m_ffn2_norm, m_ffn2_w_gate, m_ffn2_w_up, m_ffn2_w_down, m_final_norm, v_ffn1_norm, v_ffn1_w_gate, v_ffn1_w_up, v_ffn1_w_down, v_mix_norm, v_w_in, v_b_gate, v_b_ln_g, v_b_ln_b, v_b_w_s, v_b_b_s, v_c_w, v_c_scale, v_w_proj_a, v_w_proj_b, v_w_proj_c, v_w_out, v_ffn2_norm, v_ffn2_w_gate, v_ffn2_w_up, v_ffn2_w_down, v_final_norm):
    w = dict(ffn1_norm=ffn1_norm, ffn1_w_gate=ffn1_w_gate, ffn1_w_up=ffn1_w_up, ffn1_w_down=ffn1_w_down, mix_norm=mix_norm,
             w_in=w_in, b_gate=b_gate, b_ln_g=b_ln_g, b_ln_b=b_ln_b, b_w_s=b_w_s, b_b_s=b_b_s, c_w=c_w, c_scale=c_scale,
             w_proj_a=w_proj_a, w_proj_b=w_proj_b, w_proj_c=w_proj_c, w_out=w_out, ffn2_norm=ffn2_norm, ffn2_w_gate=ffn2_w_gate,
             ffn2_w_up=ffn2_w_up, ffn2_w_down=ffn2_w_down, final_norm=final_norm)
    mom = dict(ffn1_norm=m_ffn1_norm, ffn1_w_gate=m_ffn1_w_gate, ffn1_w_up=m_ffn1_w_up, ffn1_w_down=m_ffn1_w_down,
               mix_norm=m_mix_norm, w_in=m_w_in, b_gate=m_b_gate, b_ln_g=m_b_ln_g, b_ln_b=m_b_ln_b, b_w_s=m_b_w_s, b_b_s=m_b_b_s,
               c_w=m_c_w, c_scale=m_c_scale, w_proj_a=m_w_proj_a, w_proj_b=m_w_proj_b, w_proj_c=m_w_proj_c, w_out=m_w_out,
               ffn2_norm=m_ffn2_norm, ffn2_w_gate=m_ffn2_w_gate, ffn2_w_up=m_ffn2_w_up, ffn2_w_down=m_ffn2_w_down,
               final_norm=m_final_norm)
    var = dict(ffn1_norm=v_ffn1_norm, ffn1_w_gate=v_ffn1_w_gate, ffn1_w_up=v_ffn1_w_up, ffn1_w_down=v_ffn1_w_down,
               mix_norm=v_mix_norm, w_in=v_w_in, b_gate=v_b_gate, b_ln_g=v_b_ln_g, b_ln_b=v_b_ln_b, b_w_s=v_b_w_s, b_b_s=v_b_b_s,
               c_w=v_c_w, c_scale=v_c_scale, w_proj_a=v_w_proj_a, w_proj_b=v_w_proj_b, w_proj_c=v_w_proj_c, w_out=v_w_out,
               ffn2_norm=v_ffn2_norm, ffn2_w_gate=v_ffn2_w_gate, ffn2_w_up=v_ffn2_w_up, ffn2_w_down=v_ffn2_w_down,
               final_norm=v_final_norm)

    t_dim = x.shape[1]
    xs = x.reshape(t_dim, D)
    tgt = loss_target.reshape(t_dim, D)

    me = 4 * lax.axis_index("x") + 2 * lax.axis_index("y") + lax.axis_index("c")
    gathers = {}
    token = jnp.zeros((8, 128), F32)
    for l in range(DEPTH):
        for part in PARTS:
            lands = [lax.dynamic_update_slice(lax.empty((s.shape[0], NDEV) + s.shape[1:], BF16), s[:, None], (0, me, 0, 0))
                     for s in _part_shards(w, l, part)]
            gathers[l, part] = _exchange_start("gather", lands, token, f"gather_start_{part}_l{l}")
            token = gathers[l, part][-1]

    def weights_of(l, part, after):
        behind = token if (l, part) == (0, "f1") else after
        return _part_views(part, _exchange_wait("gather", gathers[l, part], behind, f"gather_wait_{part}_l{l}"))

    small = [_layer_small(w, l) for l in range(DEPTH)]
    loss_part, dcur, dfinal, saved, views = _forward(xs, tgt, weights_of, small, w["final_norm"].reshape(1, D))
    loss = lax.psum(loss_part[0, 0], ("x", "y", "c"))

    me_arr = me.astype(jnp.int32).reshape(1)
    scatters = []
    small_gathers = [None] * DEPTH
    token = None
    for l in reversed(range(DEPTH)):
        def emit(part, grads, l=l):
            scatters.append((l, part, _exchange_start("scatter", grads, dfinal, f"scatter_start_{part}_l{l}")))
            return scatters[-1][2][-1]
        dcur, small_grads, token = _backward_layer(dcur, saved[l], small[l], views[l], l, emit, token)
        packed = _pack_small_grads(small_grads, dfinal if l == DEPTH - 1 else None, me_arr, f"pack_small_grads_l{l}")
        small_gathers[l] = _exchange_start("gather", [packed], token, f"gather_start_small_l{l}")
        token = small_gathers[l][-1]
    grad_x = dcur.reshape(x.shape)

    done = {}
    behind = token
    for l, part, started in scatters:
        arrs = _exchange_wait("scatter", started, behind, f"scatter_wait_{part}_l{l}")
        na = len(arrs) // 2
        for a, piece, k, transposed in _part_pieces(part):
            flip = transposed and w[k].shape[2] % 128 != 0
            view = (lambda t: t.transpose(0, 2, 1)) if flip else (lambda t: t)
            done[k] = _reduce_update(arrs[na + a], arrs[a], me_arr, piece, view(w[k]), view(mom[k]), view(var[k]), l,
                                     done.get(k), transposed and not flip, f"update_{k}_l{l}")
            behind = done[k][1]
    unflip = lambda k, t: t.transpose(0, 2, 1) if t.shape != w[k].shape else t
    grads, delta, new_m, new_v = ({k: unflip(k, done[k][t]) for k in BIG} for t in range(4))
    gathered = [None] * DEPTH
    for l in reversed(range(DEPTH)):
        gathered[l] = _exchange_wait("gather", small_gathers[l], behind, f"gather_wait_small_l{l}")[0]
        behind = gathered[l]
    as_row = lambda t: dict(t, final_norm=t["final_norm"].reshape(1, D))
    for dst, src in zip((grads, delta, new_m, new_v), _adamw_small(gathered, as_row(w), as_row(mom), as_row(var), "adamw_small")):
        dst.update(src)
        dst["final_norm"] = src["final_norm"].reshape(D)

    return (loss, grad_x, *[grads[k] for k in ALL_WEIGHTS], *[delta[k] for k in ALL_WEIGHTS],
            *[new_m[k] for k in ALL_WEIGHTS], *[new_v[k] for k in ALL_WEIGHTS])
```

```python
import math

import jax
import jax.numpy as jnp
from jax import lax
from jax.experimental import pallas as pl
from jax.experimental.pallas import tpu as pltpu

F32 = jnp.float32
BF16 = jnp.bfloat16
MESH = pl.DeviceIdType.MESH

NDEV = 8
DEPTH = 2
D = 1024
FF = 2816
FFS = FF // NDEV
INW = 6912
INS = INW // NDEV
DS = D // NDEV
BLK = 128
NH = 4
HD = 64
GW = NH * HD
DILS = (1, 4, 16)
QK_SCALE = 1.0 / math.sqrt(HD)
ZB = INW // GW
Q_B, K_B, V_B = 0, 3, 6
UV_B = 9
C_B128 = 26
G_B = 15
BW = 512
EPS = 1e-6
NEG = -1e30
VMEM_CAP = 60 * 1024 * 1024

ADAM_LR, ADAM_B1, ADAM_B2, ADAM_EPS, ADAM_WD, ADAM_STEP = 0.001, 0.9, 0.999, 1e-08, 0.01, 10

NT = (((1,), (1,)), ((), ()))
TN = (((0,), (0,)), ((), ()))


def _bs(shape, imap):
    return pl.BlockSpec(shape, imap)


def _pc(body, *, name, grid, in_specs, out_specs, out_shape, scratch=(), alias=None, vmem=None):
    params = {}
    if vmem is not None:
        params["vmem_limit_bytes"] = min(int(vmem), VMEM_CAP)
    big = lambda s: math.prod(s.shape) * jnp.dtype(s.dtype).itemsize >= (1 << 20)
    pin = lambda s: pltpu.HBM(s.shape, s.dtype) if big(s) else s
    out_shape = [pin(s) for s in out_shape] if isinstance(out_shape, (list, tuple)) else pin(out_shape)
    call = pl.pallas_call(
        body, name=name, grid=grid, in_specs=in_specs, out_specs=out_specs, out_shape=out_shape,
        scratch_shapes=list(scratch), input_output_aliases=alias or {},
        compiler_params=pltpu.CompilerParams(**params), interpret=False)
    return lambda *args: call(*[pltpu.with_memory_space_constraint(a, pltpu.HBM) if big(a) else a for a in args])


def _sds(shape, dtype):
    return jax.ShapeDtypeStruct(shape, dtype)


def _row_tile(rows, align, cap):
    best = None
    for cand in range(align, min(rows, cap) + 1, align):
        if rows % cand == 0:
            best = cand
    assert best is not None, (rows, align, cap)
    return best


def _mm(a, b, *, mode, tm, tn, tk, n, out_dtype, name, b_row0=0, scale=None, res=None, after=None):
    m, k_dim = a.shape
    nk = k_dim // tk
    assert m % tm == 0 and n % tn == 0 and k_dim % tk == 0
    b_off, rem = divmod(b_row0, tk if mode == "nn" else tn)
    assert rem == 0
    grid = (m // tm, n // tn, nk)
    in_specs = [_bs((tm, tk), lambda i, j, k: (i, k))]
    if mode == "nn":
        in_specs.append(_bs((tk, tn), lambda i, j, k: (b_off + k, j)))
    else:
        in_specs.append(_bs((tn, tk), lambda i, j, k: (b_off + j, k)))
    args = [a, b]
    if res is not None:
        in_specs.append(_bs((tm, tn), lambda i, j, k: (i, j)))
        args.append(res)
    if after is not None:
        in_specs.append(pl.BlockSpec(memory_space=pl.ANY))
        args.append(after)
    n_in = len(args)

    def body(*refs):
        a_ref, b_ref = refs[0], refs[1]
        r_ref = refs[2] if res is not None else None
        o_ref = refs[n_in]
        av = a_ref[...].astype(BF16)
        bv = b_ref[...].astype(BF16)
        if mode == "nn":
            p = jnp.dot(av, bv, preferred_element_type=F32)
        else:
            p = lax.dot_general(av, bv, NT, preferred_element_type=F32)

        def fin(v):
            if scale is not None:
                v = v * scale
            if r_ref is not None:
                v = r_ref[...] + v
            o_ref[...] = v.astype(out_dtype)

        if nk == 1:
            fin(p)
        else:
            acc_ref = refs[-1]
            kk = pl.program_id(2)

            @pl.when(kk == 0)
            def _():
                acc_ref[...] = p

            @pl.when(kk > 0)
            def _():
                acc_ref[...] += p

            @pl.when(kk == nk - 1)
            def _():
                fin(acc_ref[...])

    scratch = [pltpu.VMEM((tm, tn), F32)] if nk > 1 else []
    est = 2 * (tm * tk * a.dtype.itemsize + tk * tn * b.dtype.itemsize + tm * tn * jnp.dtype(out_dtype).itemsize)
    est += (2 + (nk > 1)) * tm * tn * 4 + (2 * tm * tn * 4 if res is not None else 0)
    est += (tm * tk * 2 if a.dtype != BF16 else 0) + (tk * tn * 2 if b.dtype != BF16 else 0)
    return _pc(body, name=name, grid=grid, in_specs=in_specs, out_specs=_bs((tm, tn), lambda i, j, k: (i, j)),
               out_shape=_sds((m, n), out_dtype), scratch=scratch, vmem=est + (8 << 20))(*args)


def _wgrad(a, b, buf, *, piece0, rows, tm, tn, tk, name, scale=None):
    t_dim, m = a.shape
    n = b.shape[1]
    db = tm // rows
    tpp = NDEV // db
    nk = t_dim // tk
    assert tm % rows == 0 and NDEV % db == 0 and m % tm == 0 and n % tn == 0 and t_dim % tk == 0
    grid = (m // tm, n // tn, nk)

    def body(a_ref, b_ref, buf_ref, o_ref, acc_ref):
        del buf_ref
        kk = pl.program_id(2)
        p = lax.dot_general(a_ref[...].astype(BF16), b_ref[...].astype(BF16), TN, preferred_element_type=F32)

        @pl.when(kk == 0)
        def _():
            acc_ref[...] = p

        @pl.when(kk > 0)
        def _():
            acc_ref[...] += p

        @pl.when(kk == nk - 1)
        def _():
            for d in range(db):
                v = acc_ref[d * rows:(d + 1) * rows, :]
                if scale is not None:
                    v = v * scale
                o_ref[0, d] = v.astype(BF16)

    est = 2 * (tk * tm * a.dtype.itemsize + tk * tn * b.dtype.itemsize + tm * tn * 2) + 3 * tm * tn * 4 + tk * tm * 4
    return _pc(
        body, name=name, grid=grid,
        in_specs=[_bs((tk, tm), lambda i, j, k: (k, i)), _bs((tk, tn), lambda i, j, k: (k, j)),
                  pl.BlockSpec(memory_space=pl.ANY)],
        out_specs=_bs((1, db, rows, tn), lambda i, j, k: (piece0 + i // tpp, i % tpp, 0, j)),
        out_shape=_sds(buf.shape, buf.dtype), scratch=[pltpu.VMEM((tm, tn), F32)], alias={2: 0},
        vmem=est + (8 << 20))(a, b, buf)


def _mm_drms(a, b, x, gain, dres, *, tm, b_row0, after, name):
    m, k_dim = a.shape
    assert m % tm == 0 and b_row0 % k_dim == 0 and b.shape[1] == D
    k_blk = b_row0 // k_dim
    after = jnp.zeros((8, 128), F32) if after is None else after

    def body(a_ref, b_ref, x_ref, g_ref, dr_ref, after_ref, dx_ref, dg_ref):
        del after_ref
        dh_v = jnp.dot(a_ref[...].astype(BF16), b_ref[...].astype(BF16), preferred_element_type=F32)
        xv = x_ref[...]
        r = lax.rsqrt(jnp.mean(xv * xv, axis=-1, keepdims=True) + EPS)
        dyg = dh_v * g_ref[...]
        mq = jnp.mean(dyg * xv, axis=-1, keepdims=True)
        dx_ref[...] = dr_ref[...] + r * (dyg - xv * (r * r * mq))

        @pl.when(pl.program_id(0) == 0)
        def _():
            dg_ref[...] = jnp.zeros_like(dg_ref)

        dg_ref[...] += jnp.sum(dh_v * (xv * r), axis=0, keepdims=True)

    row = _bs((tm, D), lambda i: (i, 0))
    est = 2 * (tm * k_dim * a.dtype.itemsize + k_dim * D * b.dtype.itemsize + 3 * tm * D * 4) + 4 * tm * D * 4
    return _pc(body, name=name, grid=(m // tm,),
               in_specs=[_bs((tm, k_dim), lambda i: (i, 0)), _bs((k_dim, D), lambda i: (k_blk, 0)), row,
                         _bs((1, D), lambda i: (0, 0)), row, pl.BlockSpec(memory_space=pl.ANY)],
               out_specs=[row, _bs((1, D), lambda i: (0, 0))], out_shape=[_sds((m, D), F32), _sds((1, D), F32)],
               vmem=est + (8 << 20))(a, b, x, gain, dres, after)


def _dact_dswiglu(dout, b, gu, *, tm, b_row0, after, name):
    t_dim = dout.shape[0]
    assert t_dim % tm == 0 and b_row0 % FF == 0
    k_blk = b_row0 // FF
    after = jnp.zeros((8, 128), F32) if after is None else after

    def body(d_ref, b_ref, gu_ref, after_ref, o_ref):
        del after_ref
        dav = 0.5 * lax.dot_general(d_ref[...].astype(BF16), b_ref[...].astype(BF16), NT, preferred_element_type=F32)
        g = gu_ref[:, :FF].astype(F32)
        u = gu_ref[:, FF:].astype(F32)
        s = _sigmoid(g)
        o_ref[:, :FF] = (dav * u * (s * (1.0 + g * (1.0 - s)))).astype(BF16)
        o_ref[:, FF:] = (dav * (g * s)).astype(BF16)

    est = 2 * (tm * D * 4 + FF * D * 2 + 2 * tm * 2 * FF * 2) + 5 * tm * FF * 4
    return _pc(body, name=name, grid=(t_dim // tm,),
               in_specs=[_bs((tm, D), lambda i: (i, 0)), _bs((FF, D), lambda i: (k_blk, 0)),
                         _bs((tm, 2 * FF), lambda i: (i, 0)), pl.BlockSpec(memory_space=pl.ANY)],
               out_specs=_bs((tm, 2 * FF), lambda i: (i, 0)), out_shape=_sds((t_dim, 2 * FF), BF16),
               vmem=est + (8 << 20))(dout, b, gu, after)


def _rms_fwd(x, gain, name, tm=512):
    t_dim = x.shape[0]

    def body(x_ref, g_ref, o_ref):
        xv = x_ref[...]
        r = lax.rsqrt(jnp.mean(xv * xv, axis=-1, keepdims=True) + EPS)
        o_ref[...] = (xv * r * g_ref[...]).astype(BF16)

    return _pc(body, name=name, grid=(t_dim // tm,),
               in_specs=[_bs((tm, D), lambda i: (i, 0)), _bs((1, D), lambda i: (0, 0))],
               out_specs=_bs((tm, D), lambda i: (i, 0)), out_shape=_sds((t_dim, D), BF16))(x, gain)


def _sigmoid(v):
    return 1.0 / (1.0 + jnp.exp(-v))


def _swiglu_fwd(gu, name, tm=256):
    t_dim = gu.shape[0]

    def body(gu_ref, o_ref):
        g = gu_ref[:, :FF].astype(F32)
        u = gu_ref[:, FF:].astype(F32)
        o_ref[...] = (g * _sigmoid(g) * u).astype(BF16)

    return _pc(body, name=name, grid=(t_dim // tm,), in_specs=[_bs((tm, 2 * FF), lambda i: (i, 0))],
               out_specs=_bs((tm, FF), lambda i: (i, 0)), out_shape=_sds((t_dim, FF), BF16))(gu)


def _final_loss(x, gain, tgt, name, tm=512):
    t_dim = x.shape[0]

    def body(x_ref, g_ref, t_ref, loss_ref, dx_ref, dg_ref):
        xv = x_ref[...]
        gv = g_ref[...]
        r = lax.rsqrt(jnp.mean(xv * xv, axis=-1, keepdims=True) + EPS)
        xn = xv * r
        err = xn * gv - t_ref[...]
        dy = err * (1.0 / D)
        dyg = dy * gv
        mq = jnp.mean(dyg * xv, axis=-1, keepdims=True)
        dx_ref[...] = r * (dyg - xv * (r * r * mq))

        @pl.when(pl.program_id(0) == 0)
        def _():
            dg_ref[...] = jnp.zeros_like(dg_ref)
            loss_ref[...] = jnp.zeros_like(loss_ref)

        dg_ref[...] += jnp.sum(dy * xn, axis=0, keepdims=True)
        part = 0.5 * jnp.sum(jnp.mean(err * err, axis=-1, keepdims=True), axis=0, keepdims=True)
        loss_ref[...] += jnp.broadcast_to(part, loss_ref.shape)

    return _pc(body, name=name, grid=(t_dim // tm,),
               in_specs=[_bs((tm, D), lambda i: (i, 0)), _bs((1, D), lambda i: (0, 0)), _bs((tm, D), lambda i: (i, 0))],
               out_specs=[_bs((8, 128), lambda i: (0, 0)), _bs((tm, D), lambda i: (i, 0)), _bs((1, D), lambda i: (0, 0))],
               out_shape=[_sds((8, 128), F32), _sds((t_dim, D), F32), _sds((1, D), F32)])(x, gain, tgt)


def _band_mask(n):
    ri = lax.broadcasted_iota(jnp.int32, (BLK, 2 * BLK), 0)
    cj = lax.broadcasted_iota(jnp.int32, (BLK, 2 * BLK), 1)
    dist = ri + BLK - cj
    return (dist >= 0) & (dist <= BLK) & ((cj >= BLK) | (n > 0))


def _halves(unit, imap_rows, col):
    return [_bs((unit, 128), (lambda u, j=j: (imap_rows(u), 2 * col + j))) for j in range(2)]


def _get_rows(pair, r, dil):
    return jnp.concatenate([h[pl.ds(r, BLK, stride=dil), :] for h in pair], axis=1)


def _put_rows(stage, r, dil, val):
    for j in range(2):
        stage[j, pl.ds(r, BLK, stride=dil), :] = val[:, 128 * j:128 * (j + 1)]


def _unstage(stage, out_ref):
    out_ref[:, :128] = stage[0]
    out_ref[:, 128:] = stage[1]


def _attn_fwd(z, acc, ml, *, gi, first, final, name):
    dil = DILS[gi]
    t_dim = z.shape[0]
    unit = BLK * dil
    nu = t_dim // unit
    own = lambda u: u
    prev = lambda u: jnp.maximum(u - 1, 0)
    in_specs = (_halves(unit, own, Q_B + gi) + _halves(unit, own, K_B + gi) + _halves(unit, prev, K_B + gi)
                + _halves(unit, own, V_B + gi) + _halves(unit, prev, V_B + gi))
    args = [z] * 10
    st_ml = _bs((unit, 128), lambda u: (u, 0))
    if not first:
        in_specs += _halves(unit, own, 0) + [st_ml]
        args += [acc, acc, ml]

    def body(*refs):
        q2, ko2, kp2, vo2, vp2 = (refs[2 * i:2 * i + 2] for i in range(5))
        o_acc_ref, o_ml_ref, stage = refs[-3], refs[-2], refs[-1]
        valid = _band_mask(pl.program_id(0))
        lane = lax.broadcasted_iota(jnp.int32, (BLK, GW), 1)
        lane_s = lax.broadcasted_iota(jnp.int32, (BLK, 128), 1)

        def band(r, carry):
            q = _get_rows(q2, r, dil) * QK_SCALE
            kb = jnp.concatenate([_get_rows(kp2, r, dil), _get_rows(ko2, r, dil)], axis=0).astype(BF16)
            vb = jnp.concatenate([_get_rows(vp2, r, dil), _get_rows(vo2, r, dil)], axis=0).astype(BF16)
            if first:
                acc_v = jnp.zeros((BLK, GW), F32)
                ml_v = jnp.where(lane_s < 64, NEG, 0.0).astype(F32)
            else:
                acc_v = _get_rows(refs[10:12], r, dil)
                ml_v = refs[12][pl.ds(r, BLK, stride=dil), :]
            heads = [(lane >= h * HD) & (lane < (h + 1) * HD) for h in range(NH)]
            rows = lambda v, h: v[h * BLK:(h + 1) * BLK]
            qs = jnp.concatenate([jnp.where(hm, q, 0.0) for hm in heads], axis=0).astype(BF16)
            s = lax.dot_general(qs, kb, NT, preferred_element_type=F32)
            s = jnp.where(jnp.concatenate([valid] * NH, axis=0), s, NEG)
            m_old = jnp.concatenate([ml_v[:, 16 * h:16 * h + 1] for h in range(NH)], axis=0)
            l_old = jnp.concatenate([ml_v[:, 64 + 16 * h:64 + 16 * h + 1] for h in range(NH)], axis=0)
            m_new = jnp.maximum(m_old, jnp.max(s, axis=-1, keepdims=True))
            alpha = jnp.exp(m_old - m_new)
            p = jnp.exp(s - m_new)
            l_new = alpha * l_old + jnp.sum(p, axis=-1, keepdims=True)
            pv = jnp.dot(p.astype(BF16), vb, preferred_element_type=F32)
            stat = m_new + jnp.log(l_new) if final else m_new
            acc_o, ml_o = acc_v, ml_v
            l_b = jnp.ones((BLK, GW), F32)
            for h, hm in enumerate(heads):
                acc_o = jnp.where(hm, acc_v * rows(alpha, h) + rows(pv, h), acc_o)
                l_b = jnp.where(hm, rows(l_new, h), l_b)
                ml_o = jnp.where((lane_s >= 16 * h) & (lane_s < 16 * h + 16), rows(stat, h), ml_o)
                if not final:
                    ml_o = jnp.where((lane_s >= 64 + 16 * h) & (lane_s < 64 + 16 * h + 16), rows(l_new, h), ml_o)
            _put_rows(stage, r, dil, acc_o / l_b if final else acc_o)
            o_ml_ref[pl.ds(r, BLK, stride=dil), :] = ml_o
            return carry

        lax.fori_loop(0, dil, band, 0, unroll=min(dil, 2))
        _unstage(stage, o_acc_ref)

    blocks = (len(in_specs) + 6) * unit * 128 * 4
    return _pc(body, name=name, grid=(nu,), in_specs=in_specs, out_specs=[_bs((unit, GW), lambda u: (u, 0)), st_ml],
               out_shape=[_sds((t_dim, GW), F32), _sds((t_dim, 128), F32)],
               scratch=[pltpu.VMEM((2, unit, 128), F32)], vmem=2 * blocks + (12 << 20))(*args)


def _attn_bwd(z, dy, y, lse, *, gi, name):
    dil = DILS[gi]
    t_dim = z.shape[0]
    unit = BLK * dil
    nu = t_dim // unit

    own = lambda u: jnp.minimum(u, nu - 1)
    prev = lambda u: jnp.maximum(jnp.minimum(u, nu - 1) - 1, 0)
    in_specs = (_halves(unit, own, Q_B + gi) + _halves(unit, own, K_B + gi) + _halves(unit, prev, K_B + gi)
                + _halves(unit, own, V_B + gi) + _halves(unit, prev, V_B + gi)
                + _halves(unit, own, 0) + _halves(unit, own, 0) + [_bs((unit, 128), lambda u: (own(u), 0))])
    tok = _bs((unit, GW), lambda u: (own(u), 0))
    kv_out = _bs((unit, GW), lambda u: (jnp.maximum(u - 1, 0), 0))

    def body(*refs):
        q2, ko2, kp2, vo2, vp2, dy2, y2 = (refs[2 * i:2 * i + 2] for i in range(7))
        lse_ref = refs[14]
        dq_ref, dk_ref, dv_ref, dq_st, dk_st, dv_st, dkc, dvc = refs[15:]
        u = pl.program_id(0)

        @pl.when(u == 0)
        def _():
            dkc[...] = jnp.zeros_like(dkc)
            dvc[...] = jnp.zeros_like(dvc)

        @pl.when(u < nu)
        def _():
            valid = _band_mask(u)
            lane = lax.broadcasted_iota(jnp.int32, (BLK, GW), 1)

            def band(r, carry):
                q = _get_rows(q2, r, dil) * QK_SCALE
                kb = jnp.concatenate([_get_rows(kp2, r, dil), _get_rows(ko2, r, dil)], axis=0).astype(BF16)
                vb = jnp.concatenate([_get_rows(vp2, r, dil), _get_rows(vo2, r, dil)], axis=0).astype(BF16)
                dyv = _get_rows(dy2, r, dil)
                yv = _get_rows(y2, r, dil)
                lse_v = lse_ref[pl.ds(r, BLK, stride=dil), :]
                heads = [(lane >= h * HD) & (lane < (h + 1) * HD) for h in range(NH)]
                qs = jnp.concatenate([jnp.where(hm, q, 0.0) for hm in heads], axis=0).astype(BF16)
                dys = jnp.concatenate([jnp.where(hm, dyv, 0.0) for hm in heads], axis=0)
                lse_s = jnp.concatenate([lse_v[:, 16 * h:16 * h + 1] for h in range(NH)], axis=0)
                s = lax.dot_general(qs, kb, NT, preferred_element_type=F32)
                p = jnp.where(jnp.concatenate([valid] * NH, axis=0), jnp.exp(s - lse_s), 0.0)
                delta = jnp.sum(dys * jnp.concatenate([yv] * NH, axis=0), axis=-1, keepdims=True)
                dys16 = dys.astype(BF16)
                dp = lax.dot_general(dys16, vb, NT, preferred_element_type=F32)
                ds = (p * (dp - delta)).astype(BF16)
                dq_all = jnp.dot(ds, kb, preferred_element_type=F32)
                dq = jnp.zeros((BLK, GW), F32)
                for h, hm in enumerate(heads):
                    dq = jnp.where(hm, dq_all[h * BLK:(h + 1) * BLK], dq)
                dkb = lax.dot_general(ds, qs, TN, preferred_element_type=F32)
                dvb = lax.dot_general(p.astype(BF16), dys16, TN, preferred_element_type=F32)
                _put_rows(dq_st, r, dil, dq * QK_SCALE)
                _put_rows(dk_st, r, dil, _get_rows((dkc.at[0], dkc.at[1]), r, dil) + dkb[:BLK])
                _put_rows(dv_st, r, dil, _get_rows((dvc.at[0], dvc.at[1]), r, dil) + dvb[:BLK])
                _put_rows(dkc, r, dil, dkb[BLK:])
                _put_rows(dvc, r, dil, dvb[BLK:])
                return carry

            lax.fori_loop(0, dil, band, 0, unroll=min(dil, 2))
            _unstage(dq_st, dq_ref)

            @pl.when(u > 0)
            def _():
                _unstage(dk_st, dk_ref)
                _unstage(dv_st, dv_ref)

        @pl.when(u == nu)
        def _():
            _unstage(dkc, dk_ref)
            _unstage(dvc, dv_ref)

    blocks = (15 + 6 + 5) * unit * 128 * 4 * 2
    return _pc(body, name=name, grid=(nu + 1,), in_specs=in_specs, out_specs=[tok, kv_out, kv_out],
               out_shape=[_sds((t_dim, GW), F32)] * 3, scratch=[pltpu.VMEM((2, unit, 128), F32)] * 5,
               vmem=blocks + (12 << 20))(*([z] * 10), dy, dy, y, y, lse)


_INV_SQRT2 = 1.0 / math.sqrt(2.0)
_INV_SQRT2PI = 1.0 / math.sqrt(2.0 * math.pi)


def _gelu(v):
    return 0.5 * v * (1.0 + lax.erf(v * _INV_SQRT2))


def _gelu_grad(v):
    return 0.5 * (1.0 + lax.erf(v * _INV_SQRT2)) + v * (_INV_SQRT2PI * jnp.exp(-0.5 * v * v))


def _gmlp_core(z_refs, lng, lnb, ws_ref, bs_ref):
    u_pre = jnp.concatenate([z_refs[0][...], z_refs[1][...]], axis=1)
    v_pre = jnp.concatenate([z_refs[2][...], z_refs[3][...]], axis=1)
    u = _gelu(u_pre)
    v = _gelu(v_pre)
    xc = v - jnp.mean(v, axis=-1, keepdims=True)
    rstd = lax.rsqrt(jnp.mean(xc * xc, axis=-1, keepdims=True) + EPS)
    xhat = xc * rstd
    vn = xhat * lng + lnb
    ti = lax.broadcasted_iota(jnp.int32, (BLK, BLK), 0)
    si = lax.broadcasted_iota(jnp.int32, (BLK, BLK), 1)
    causal = ti >= si
    vn16 = vn.astype(BF16)
    ws = [jnp.where(causal, ws_ref[g], 0.0).astype(BF16) for g in range(4)]
    mixed = jnp.concatenate(
        [jnp.dot(ws[g], vn16[:, g * BLK:(g + 1) * BLK], preferred_element_type=F32) + bs_ref[g] for g in range(4)], axis=1)
    return u_pre, v_pre, u, xhat, rstd, vn16, ws, causal, mixed


def _gmlp_specs():
    z_specs = [_bs((BLK, GW), (lambda i, c=c: (i, UV_B + c))) for c in range(4)]
    par_specs = [_bs((1, BW), lambda i: (0, 0)), _bs((1, BW), lambda i: (0, 0)),
                 _bs((4, BLK, BLK), lambda i: (0, 0, 0)), _bs((4, BLK, 1), lambda i: (0, 0, 0))]
    return z_specs, par_specs


def _gmlp_fwd(z, lng, lnb, ws, bs, name):
    t_dim = z.shape[0]
    z_specs, par_specs = _gmlp_specs()

    def body(z0, z1, z2, z3, lng_ref, lnb_ref, ws_ref, bs_ref, o_ref):
        core = _gmlp_core((z0, z1, z2, z3), lng_ref[...], lnb_ref[...], ws_ref, bs_ref)
        o_ref[...] = (core[2] * core[8]).astype(BF16)

    return _pc(body, name=name, grid=(t_dim // BLK,), in_specs=z_specs + par_specs,
               out_specs=_bs((BLK, BW), lambda i: (i, 0)), out_shape=_sds((t_dim, BW), BF16))(z, z, z, z, lng, lnb, ws, bs)


def _gmlp_bwd(z, dy, lng, lnb, ws, bs, name):
    t_dim = z.shape[0]
    z_specs, par_specs = _gmlp_specs()

    def body(z0, z1, z2, z3, lng_ref, lnb_ref, ws_ref, bs_ref, dy_ref, duv_ref, dws_ref, dbs_ref, dlng_ref, dlnb_ref):
        lng_v = lng_ref[...]
        u_pre, v_pre, u, xhat, rstd, vn16, wsm, causal, mixed = _gmlp_core((z0, z1, z2, z3), lng_v, lnb_ref[...], ws_ref, bs_ref)
        dyv = dy_ref[...].astype(F32)
        du = dyv * mixed
        dmixed = dyv * u

        @pl.when(pl.program_id(0) == 0)
        def _():
            dws_ref[...] = jnp.zeros_like(dws_ref)
            dbs_ref[...] = jnp.zeros_like(dbs_ref)
            dlng_ref[...] = jnp.zeros_like(dlng_ref)
            dlnb_ref[...] = jnp.zeros_like(dlnb_ref)

        dvn_parts = []
        for g in range(4):
            dm = dmixed[:, g * BLK:(g + 1) * BLK]
            dm16 = dm.astype(BF16)
            dw = lax.dot_general(dm16, vn16[:, g * BLK:(g + 1) * BLK], NT, preferred_element_type=F32)
            dws_ref[g] += jnp.where(causal, dw, 0.0)
            dbs_ref[g] += jnp.sum(dm.T, axis=0, keepdims=True)
            dvn_parts.append(lax.dot_general(wsm[g], dm16, TN, preferred_element_type=F32))
        dvn = jnp.concatenate(dvn_parts, axis=1)
        dlng_ref[...] += jnp.sum(dvn * xhat, axis=0, keepdims=True)
        dlnb_ref[...] += jnp.sum(dvn, axis=0, keepdims=True)
        dxh = dvn * lng_v
        dv = rstd * (dxh - jnp.mean(dxh, axis=-1, keepdims=True) - xhat * jnp.mean(dxh * xhat, axis=-1, keepdims=True))
        duv_ref[:, :BW] = (du * _gelu_grad(u_pre)).astype(BF16)
        duv_ref[:, BW:] = (dv * _gelu_grad(v_pre)).astype(BF16)

    return _pc(body, name=name, grid=(t_dim // BLK,),
               in_specs=z_specs + par_specs + [_bs((BLK, BW), lambda i: (i, 0))],
               out_specs=[_bs((BLK, 2 * BW), lambda i: (i, 0)), _bs((4, BLK, BLK), lambda i: (0, 0, 0)),
                          _bs((4, 1, BLK), lambda i: (0, 0, 0)), _bs((1, BW), lambda i: (0, 0)), _bs((1, BW), lambda i: (0, 0))],
               out_shape=[_sds((t_dim, 2 * BW), BF16), _sds((4, BLK, BLK), F32), _sds((4, 1, BLK), F32),
                          _sds((1, BW), F32), _sds((1, BW), F32)])(z, z, z, z, lng, lnb, ws, bs, dy)


def _pool_core(x, gi):
    t_dim = x.shape[0]
    w = jnp.left_shift(2, gi)
    row = lax.broadcasted_iota(jnp.int32, x.shape, 0)
    s = x
    for k in (1, 2, 4, 8):
        sh = jnp.where(row >= k, pltpu.roll(s, k, 0), 0.0)
        s = jnp.where(k < w, s + sh, s)
    cnt = jnp.minimum(row + 1, w).astype(F32)
    return s / cnt - x, cnt, row, w, t_dim


def _pool_fwd(z, cw, cs, name):
    t_dim = z.shape[0]

    def body(z_ref, cw_ref, cs_ref, o_ref):
        pooled = _pool_core(z_ref[...], pl.program_id(0))[0]
        y = jnp.dot(pooled.astype(BF16), cw_ref[0].astype(BF16), preferred_element_type=F32) * cs_ref[0]
        o_ref[...] = y.astype(BF16)

    return _pc(body, name=name, grid=(4,),
               in_specs=[_bs((t_dim, BLK), lambda g: (0, C_B128 + g)), _bs((1, BLK, BLK), lambda g: (g, 0, 0)),
                         _bs((1, 1, BLK), lambda g: (g, 0, 0))],
               out_specs=_bs((t_dim, BLK), lambda g: (0, g)), out_shape=_sds((t_dim, BW), BF16),
               vmem=40 << 20)(z, cw, cs)


def _pool_bwd(z, dy, cw, cs, name):
    t_dim = z.shape[0]

    def body(z_ref, dy_ref, cw_ref, cs_ref, dx_ref, dcw_ref, dcs_ref):
        pooled, cnt, row, w, _ = _pool_core(z_ref[...], pl.program_id(0))
        p16 = pooled.astype(BF16)
        cw16 = cw_ref[0].astype(BF16)
        dyv = dy_ref[...].astype(F32)
        lin = jnp.dot(p16, cw16, preferred_element_type=F32)
        dcs_ref[0] = jnp.sum(dyv * lin, axis=0, keepdims=True)
        dys = (dyv * cs_ref[0]).astype(BF16)
        dcw_ref[0] = lax.dot_general(p16, dys, TN, preferred_element_type=F32)
        dpool = lax.dot_general(dys, cw16, NT, preferred_element_type=F32)
        sb = dpool / cnt
        for k in (1, 2, 4, 8):
            sh = jnp.where(row < t_dim - k, pltpu.roll(sb, t_dim - k, 0), 0.0)
            sb = jnp.where(k < w, sb + sh, sb)
        dx_ref[...] = (sb - dpool).astype(BF16)

    return _pc(body, name=name, grid=(4,),
               in_specs=[_bs((t_dim, BLK), lambda g: (0, C_B128 + g)), _bs((t_dim, BLK), lambda g: (0, g)),
                         _bs((1, BLK, BLK), lambda g: (g, 0, 0)), _bs((1, 1, BLK), lambda g: (g, 0, 0))],
               out_specs=[_bs((t_dim, BLK), lambda g: (0, g)), _bs((1, BLK, BLK), lambda g: (g, 0, 0)),
                          _bs((1, 1, BLK), lambda g: (g, 0, 0))],
               out_shape=[_sds((t_dim, BW), BF16), _sds((4, BLK, BLK), F32), _sds((4, 1, BLK), F32)],
               vmem=48 << 20)(z, dy, cw, cs)


def _gates_specs(tm):
    z_specs = [_bs((tm, GW), (lambda j, i, b=b: (i, G_B + 4 * b + j))) for b in range(3)]
    bg_specs = [_bs((1, GW), (lambda j, i, b=b: (0, 4 * b + j))) for b in range(3)]
    tile = _bs((tm, GW), lambda j, i: (i, j))
    return z_specs, bg_specs, tile


def _gates_fwd(z, bg, pa, pb, pc, name, tm=512):
    t_dim = z.shape[0]
    z_specs, bg_specs, tile = _gates_specs(tm)

    def body(za, zb, zc, ba, bb, bc, pa_ref, pb_ref, pc_ref, o_ref):
        o_ref[...] = (_sigmoid(za[...] + ba[...]) * pa_ref[...] + _sigmoid(zb[...] + bb[...]) * pb_ref[...]
                      + _sigmoid(zc[...] + bc[...]) * pc_ref[...]).astype(BF16)

    return _pc(body, name=name, grid=(D // GW, t_dim // tm), in_specs=z_specs + bg_specs + [tile] * 3,
               out_specs=tile, out_shape=_sds((t_dim, D), BF16))(z, z, z, bg, bg, bg, pa, pb, pc)


def _gates_bwd(z, bg, pa, pb, pc, dm, name, tm=512):
    t_dim = z.shape[0]
    z_specs, bg_specs, tile = _gates_specs(tm)
    bias_out = _bs((1, GW), lambda j, i: (0, j))

    def body(za, zb, zc, ba, bb, bc, pa_ref, pb_ref, pc_ref, dm_ref, dpa, dpb, dpc, dza, dzb, dzc, dba, dbb, dbc):
        dmv = dm_ref[...]
        first = pl.program_id(1) == 0
        for z_ref, b_ref, p_ref, dp_ref, dz_ref, db_ref in ((za, ba, pa_ref, dpa, dza, dba), (zb, bb, pb_ref, dpb, dzb, dbb),
                                                            (zc, bc, pc_ref, dpc, dzc, dbc)):
            gate = _sigmoid(z_ref[...] + b_ref[...])
            dp_ref[...] = (gate * dmv).astype(BF16)
            dz = dmv * p_ref[...] * (gate * (1.0 - gate))
            dz_ref[...] = dz.astype(BF16)

            @pl.when(first)
            def _():
                db_ref[...] = jnp.zeros_like(db_ref)

            db_ref[...] += jnp.sum(dz, axis=0, keepdims=True)

    return _pc(body, name=name, grid=(D // GW, t_dim // tm), in_specs=z_specs + bg_specs + [tile] * 4,
               out_specs=[tile] * 6 + [bias_out] * 3,
               out_shape=[_sds((t_dim, D), BF16)] * 6 + [_sds((1, D), F32)] * 3)(z, z, z, bg, bg, bg, pa, pb, pc, dm)


def _coords():
    return lax.axis_index("x"), lax.axis_index("y"), lax.axis_index("c")


def _peer(k, x, y, c):
    px = 1 - x if k & 4 else x
    py = 1 - y if k & 2 else y
    pc = 1 - c if k & 1 else c
    return (px, py, pc), 4 * px + 2 * py + pc


def _exchange_copies(kind, bufs, send, recv):
    x, y, c = _coords()
    me = 4 * x + 2 * y + c
    na = len(bufs) if kind == "gather" else len(bufs) // 2
    out = []
    for k in range(1, NDEV):
        to, peer = _peer(k, x, y, c)
        for a in range(na):
            sems = dict(send_sem=send.at[a * 7 + k - 1], recv_sem=recv.at[a * 7 + k - 1], device_id=to, device_id_type=MESH)
            if kind == "gather":
                mine = pltpu.make_async_remote_copy(src_ref=bufs[a].at[:, me], dst_ref=bufs[a].at[:, me], **sems)
                theirs = pltpu.make_async_remote_copy(src_ref=bufs[a].at[:, me], dst_ref=bufs[a].at[:, peer], **sems)
            else:
                src, land = bufs[a], bufs[na + a]
                mine = pltpu.make_async_remote_copy(src_ref=src.at[:, peer], dst_ref=land.at[me], **sems)
                theirs = pltpu.make_async_remote_copy(src_ref=src.at[:, peer], dst_ref=land.at[peer], **sems)
            out.append((mine, theirs))
    return out


_HBM = pl.BlockSpec(memory_space=pltpu.HBM)
_SEM = pl.BlockSpec(memory_space=pltpu.SEMAPHORE)
_EFFECT = pltpu.SideEffectType.DATAFLOW_SIDE_EFFECTING


def _exchange_start(kind, arrays, after, name):
    na = len(arrays)
    if kind == "scatter":
        arrays = list(arrays) + [lax.empty((NDEV, s.shape[0]) + s.shape[2:], s.dtype) for s in arrays]
    nb = len(arrays)

    def body(*refs):
        bufs = refs[:nb]
        send, recv = refs[nb + 1], refs[nb + 2]
        token = refs[-1]
        for mine, _ in _exchange_copies(kind, bufs, send, recv):
            mine.start()
        token[...] = jnp.zeros_like(token)

    arrays = [pltpu.with_memory_space_constraint(b, pltpu.HBM) for b in arrays]
    outs = pl.pallas_call(
        body, name=name,
        out_shape=(pltpu.SemaphoreType.DMA((7 * na,)), pltpu.SemaphoreType.DMA((7 * na,)),
                   *[pltpu.HBM(b.shape, b.dtype) for b in arrays], _sds((8, 128), F32)),
        in_specs=[_HBM] * nb + [pl.BlockSpec(memory_space=pl.ANY)],
        out_specs=(_SEM, _SEM, *([_HBM] * nb), pl.BlockSpec(memory_space=pltpu.VMEM)),
        input_output_aliases={i: 2 + i for i in range(nb)},
        compiler_params=pltpu.CompilerParams(has_side_effects=_EFFECT), interpret=False)(*arrays, after)
    return outs[0], outs[1], list(outs[2:2 + nb]), outs[-1]


def _exchange_wait(kind, started, after, name):
    send, recv, arrays, _ = started
    nb = len(arrays)

    def body(*refs):
        for mine, theirs in _exchange_copies(kind, refs[:nb], refs[nb], refs[nb + 1]):
            mine.wait_send()
            theirs.wait_recv()

    outs = pl.pallas_call(
        body, name=name,
        out_shape=tuple(pltpu.HBM(b.shape, b.dtype) for b in arrays),
        in_specs=[_HBM] * nb + [_SEM, _SEM, pl.BlockSpec(memory_space=pl.ANY)],
        out_specs=tuple([_HBM] * nb),
        input_output_aliases={i: i for i in range(nb)},
        compiler_params=pltpu.CompilerParams(has_side_effects=_EFFECT), interpret=False)(*arrays, send, recv, after)
    return list(outs)


def _adam_math(w, g, m, v):
    m2 = ADAM_B1 * m + (1.0 - ADAM_B1) * g
    v2 = ADAM_B2 * v + (1.0 - ADAM_B2) * jnp.square(g)
    m_hat = m2 / (1.0 - ADAM_B1 ** ADAM_STEP)
    v_hat = v2 / (1.0 - ADAM_B2 ** ADAM_STEP)
    return -ADAM_LR * (m_hat / (jnp.sqrt(v_hat) + ADAM_EPS) + ADAM_WD * w), m2, v2


_DIRECT = (1, 4, 2, 6)


def _gather_copies(bufs, send_d, recv_d, send_f=None, recv_f=None):
    x, y, c = _coords()
    me = 4 * x + 2 * y + c
    sibling, _ = _peer(1, x, y, c)
    direct, forward = [], []
    for a, buf in enumerate(bufs):
        for j, k in enumerate(_DIRECT):
            to, peer = _peer(k, x, y, c)
            sems = dict(send_sem=send_d.at[4 * a + j], recv_sem=recv_d.at[4 * a + j], device_id=to, device_id_type=MESH)
            direct.append((pltpu.make_async_remote_copy(src_ref=buf.at[:, me], dst_ref=buf.at[:, me], **sems),
                           pltpu.make_async_remote_copy(src_ref=buf.at[:, me], dst_ref=buf.at[:, peer], **sems)))
        if send_f is None:
            continue
        for j, k in enumerate(_DIRECT[1:]):
            _, near = _peer(k, x, y, c)
            _, far = _peer(k | 1, x, y, c)
            sems = dict(send_sem=send_f.at[3 * a + j], recv_sem=recv_f.at[3 * a + j], device_id=sibling, device_id_type=MESH)
            forward.append((pltpu.make_async_remote_copy(src_ref=buf.at[:, near], dst_ref=buf.at[:, near], **sems),
                            pltpu.make_async_remote_copy(src_ref=buf.at[:, near], dst_ref=buf.at[:, far], **sems)))
    return direct, forward


def _gather_start(lands, after, name):
    na = len(lands)

    def body(*refs):
        direct, _ = _gather_copies(refs[:na], refs[na + 1], refs[na + 2])
        for mine, _ in direct:
            mine.start()
        refs[-1][...] = jnp.zeros_like(refs[-1])

    lands = [pltpu.with_memory_space_constraint(b, pltpu.HBM) for b in lands]
    outs = pl.pallas_call(
        body, name=name,
        out_shape=(pltpu.SemaphoreType.DMA((4 * na,)), pltpu.SemaphoreType.DMA((4 * na,)),
                   *[pltpu.HBM(b.shape, b.dtype) for b in lands], _sds((8, 128), F32)),
        in_specs=[_HBM] * na + [pl.BlockSpec(memory_space=pl.ANY)],
        out_specs=(_SEM, _SEM, *([_HBM] * na), pl.BlockSpec(memory_space=pltpu.VMEM)),
        input_output_aliases={i: 2 + i for i in range(na)},
        compiler_params=pltpu.CompilerParams(has_side_effects=_EFFECT), interpret=False)(*lands, after)
    return outs[0], outs[1], list(outs[2:2 + na]), outs[-1]


def _gather_forward(started, after, name):
    send_d, recv_d, lands, _ = started
    na = len(lands)

    def body(*refs):
        direct, forward = _gather_copies(refs[:na], refs[na], refs[na + 1], refs[2 * na + 3], refs[2 * na + 4])
        for a in range(na):
            for j in range(3):
                direct[4 * a + 1 + j][1].wait_recv()
                forward[3 * a + j][0].start()
        refs[-1][...] = jnp.zeros_like(refs[-1])

    outs = pl.pallas_call(
        body, name=name,
        out_shape=(*[pltpu.HBM(b.shape, b.dtype) for b in lands], pltpu.SemaphoreType.DMA((3 * na,)),
                   pltpu.SemaphoreType.DMA((3 * na,)), _sds((8, 128), F32)),
        in_specs=[_HBM] * na + [_SEM, _SEM, pl.BlockSpec(memory_space=pl.ANY)],
        out_specs=(*([_HBM] * na), _SEM, _SEM, pl.BlockSpec(memory_space=pltpu.VMEM)),
        input_output_aliases={i: i for i in range(na)},
        compiler_params=pltpu.CompilerParams(has_side_effects=_EFFECT), interpret=False)(*lands, send_d, recv_d, after)
    return send_d, recv_d, outs[na], outs[na + 1], list(outs[:na]), outs[-1]


def _gather_wait(forwarded, after, name):
    send_d, recv_d, send_f, recv_f, lands, _ = forwarded
    na = len(lands)

    def body(*refs):
        direct, forward = _gather_copies(refs[:na], refs[na], refs[na + 1], refs[na + 2], refs[na + 3])
        for a in range(na):
            for j in range(4):
                direct[4 * a + j][0].wait_send()
            direct[4 * a][1].wait_recv()
            for j in range(3):
                forward[3 * a + j][0].wait_send()
                forward[3 * a + j][1].wait_recv()

    outs = pl.pallas_call(
        body, name=name, out_shape=tuple(pltpu.HBM(b.shape, b.dtype) for b in lands),
        in_specs=[_HBM] * na + [_SEM] * 4 + [pl.BlockSpec(memory_space=pl.ANY)], out_specs=tuple([_HBM] * na),
        input_output_aliases={i: i for i in range(na)},
        compiler_params=pltpu.CompilerParams(has_side_effects=_EFFECT), interpret=False)(
            *lands, send_d, recv_d, send_f, recv_f, after)
    return list(outs)


def _reduce_update(landed, own, me, piece, w, m, v, layer, prev, transposed, name):
    rows, cols = landed.shape[2], landed.shape[3]
    tc = 256
    assert cols % tc == 0 and w.shape[1:] == ((cols, rows) if transposed else (rows, cols))
    lane_chunks = [(c0, min(128, rows - c0)) for c0 in range(0, rows, 128)]

    def body(me_ref, b_ref, own_ref, w_ref, m_ref, v_ref, *rest):
        outs = rest[-4:]
        mine = me_ref[0]
        s = None
        for d in range(NDEV):
            other = jnp.where(mine == d, (d + 1) % NDEV, d)
            part = jnp.where(mine == d, own_ref[0, 0], b_ref[other, 0]).astype(F32)
            s = part if s is None else s + part
        if not transposed:
            for o, val in zip(outs, (s,) + _adam_math(w_ref[0], s, m_ref[0], v_ref[0])):
                o[0] = val
            return
        for c0, wd in lane_chunks:
            chunk = s[c0:c0 + wd]
            if wd < 128:
                chunk = jnp.concatenate([chunk, jnp.zeros((128 - wd, tc), F32)], axis=0)
            g = chunk.T[:, :wd]
            sl = (0, slice(None), slice(c0, c0 + wd))
            for o, val in zip(outs, (g,) + _adam_math(w_ref[sl], g, m_ref[sl], v_ref[sl])):
                o[sl] = val

    if transposed:
        nat = _bs((1, tc, rows), lambda j, me_ref: (layer, j, 0))
    else:
        nat = _bs((1, rows, tc), lambda j, me_ref: (layer, 0, j))
    in_specs = [_bs((NDEV, 1, rows, tc), lambda j, me_ref: (0, piece, 0, j)),
                _bs((1, 1, rows, tc), lambda j, me_ref: (piece, me_ref[0], 0, j)), nat, nat, nat]
    args = [me, landed, own, w, m, v]
    alias = {}
    if prev is not None:
        in_specs += [pl.BlockSpec(memory_space=pl.ANY)] * 4
        args += list(prev)
        alias = {6 + i: i for i in range(4)}
    grid_spec = pltpu.PrefetchScalarGridSpec(num_scalar_prefetch=1, grid=(cols // tc,), in_specs=in_specs, out_specs=[nat] * 4)
    return pl.pallas_call(body, name=name, grid_spec=grid_spec, out_shape=[_sds(w.shape, F32)] * 4,
                          input_output_aliases=alias, interpret=False)(*args)


SMALL_LAYER = ("ffn1_norm", "mix_norm", "b_gate", "b_ln_g", "b_ln_b", "b_w_s", "b_b_s", "c_w", "c_scale", "ffn2_norm")
SMALL_ROWS = {"ffn1_norm": 8, "mix_norm": 8, "b_gate": 24, "b_ln_g": 4, "b_ln_b": 4, "b_w_s": 512, "b_b_s": 4, "c_w": 512,
              "c_scale": 4, "ffn2_norm": 8, "final_norm": 8}
LAYER_ROWS = sum(SMALL_ROWS[k] for k in SMALL_LAYER)


def _small_offset(name):
    if name == "final_norm":
        return LAYER_ROWS
    return sum(SMALL_ROWS[k] for k in SMALL_LAYER[:SMALL_LAYER.index(name)])


def _pack_small_grads(layer_grads, final, me, name):
    flat, where = [], []
    for k in SMALL_LAYER:
        parts = layer_grads[k] if k == "b_gate" else [layer_grads[k]]
        for i, part in enumerate(parts):
            flat.append(part)
            where.append(_small_offset(k) + i * (D // 128))
    if final is not None:
        flat.append(final)
        where.append(_small_offset("final_norm"))
    pack_rows = LAYER_ROWS + (SMALL_ROWS["final_norm"] if final is not None else 0)

    def body(me_ref, *refs):
        del me_ref
        o_ref = refs[-1]
        for ref, off in zip(refs[:-1], where):
            if len(ref.shape) == 2:
                for j in range(ref.shape[1] // 128):
                    o_ref[0, 0, off + j:off + j + 1, :] = ref[:, 128 * j:128 * (j + 1)]
            else:
                rows = ref.shape[1]
                for g in range(4):
                    o_ref[0, 0, off + g * rows:off + (g + 1) * rows, :] = ref[g]

    grid_spec = pltpu.PrefetchScalarGridSpec(
        num_scalar_prefetch=1, grid=(1,),
        in_specs=[pl.BlockSpec(a.shape, (lambda i, me_ref, nd=a.ndim: (0,) * nd)) for a in flat],
        out_specs=_bs((1, 1, pack_rows, 128), lambda i, me_ref: (0, me_ref[0], 0, 0)))
    return pl.pallas_call(body, name=name, grid_spec=grid_spec, out_shape=_sds((1, NDEV, pack_rows, 128), F32),
                          interpret=False)(me, *flat)


def _adamw_small(gathered, w, mom, var, name):
    names = SMALL_LAYER + ("final_norm",)
    ins = list(gathered) + [t[k] for k in names for t in (w, mom, var)]

    def body(*refs):
        outs = refs[len(ins):]

        def rows(l, off, n):
            land = refs[l]
            s = land[0, 0, off:off + n, :]
            for d in range(1, NDEV):
                s = s + land[0, d, off:off + n, :]
            return s

        def as_row(l, off, n):
            s = rows(l, off, n)
            return jnp.concatenate([s[j:j + 1, :] for j in range(n)], axis=1)

        def update(i, idx, g):
            wr, mr, vr = (refs[DEPTH + 3 * i + t] for t in range(3))
            d, m2, v2 = _adam_math(wr[idx], g, mr[idx], vr[idx])
            for o, val in zip(outs[4 * i:4 * i + 4], (g, d, m2, v2)):
                o[idx] = val

        for i, k in enumerate(names):
            n = SMALL_ROWS[k]
            off = _small_offset(k)
            if k == "final_norm":
                update(i, (slice(0, 1),), as_row(DEPTH - 1, off, n))
                continue
            for l in range(DEPTH):
                if k in ("b_w_s", "c_w"):
                    for g in range(4):
                        update(i, (l, g), rows(l, off + g * BLK, BLK))
                elif k in ("b_b_s", "c_scale"):
                    update(i, (l,), rows(l, off, n))
                else:
                    update(i, (slice(l, l + 1),), as_row(l, off, n))

    out_shape = [_sds(w[k].shape, F32) for k in names for _ in range(4)]
    outs = pl.pallas_call(body, name=name, out_shape=out_shape,
                          compiler_params=pltpu.CompilerParams(vmem_limit_bytes=48 << 20), interpret=False)(*ins)
    return tuple({k: outs[4 * i + t] for i, k in enumerate(names)} for t in range(4))


PG, PD = 0, 2


def _ffn_fwd(x, gain, wf, tag, mid):
    h = _rms_fwd(x, gain, f"rms_{tag}")
    gu = _mm(h, wf, mode="nt", tm=2048, tn=512, tk=D, n=2 * FF, b_row0=PG * FF, out_dtype=BF16, name=f"gu_{tag}")
    token = mid(gu)
    act = _swiglu_fwd(gu, f"swiglu_{tag}")
    out = _mm(act, wf, mode="nn", tm=512, tn=D, tk=FF, n=D, b_row0=PD * FF, out_dtype=F32, scale=0.5, res=x, after=token,
              name=f"down_{tag}")
    return out, (x, h, gu, act)


def _ffn_bwd(dout, saved, gain, wf, tag, after, emit):
    x, h, gu, act = saved
    grf = lax.empty((3, NDEV, FFS, D), BF16)
    dgu = _dact_dswiglu(dout, wf, gu, tm=256, b_row0=PD * FF, after=after, name=f"dgu_{tag}")
    grf = _wgrad(act, dout, grf, piece0=PD, rows=FFS, tm=FF // 2, tn=D, tk=1024, scale=0.5, name=f"wg_down_{tag}")
    grf = _wgrad(dgu, h, grf, piece0=PG, rows=FFS, tm=FF // 2, tn=D, tk=2048, name=f"wg_gu_{tag}")
    token = emit([grf])
    dx, dgain = _mm_drms(dgu, wf, x, gain, dout, tm=256, b_row0=PG * FF, after=token, name=f"dx_{tag}")
    return dx, dgain, token


def _mixer_fwd(x, sp, ww, wo, wa, wbc, tag, mid):
    h = _rms_fwd(x, sp["mix_norm"], f"rms_mix_{tag}")
    z = _mm(h, ww, mode="nt", tm=1024, tn=1152, tk=D, n=INW, out_dtype=F32, name=f"z_{tag}")
    token = mid(z)
    acc, ml = _attn_fwd(z, None, None, gi=0, first=True, final=False, name=f"attn0_{tag}")
    acc, ml = _attn_fwd(z, acc, ml, gi=1, first=False, final=False, name=f"attn1_{tag}")
    ya, lse = _attn_fwd(z, acc, ml, gi=2, first=False, final=True, name=f"attn2_{tag}")
    yb = _gmlp_fwd(z, sp["b_ln_g"], sp["b_ln_b"], sp["b_w_s"], sp["b_b_s"], f"gmlp_{tag}")
    yc = _pool_fwd(z, sp["c_w"], sp["c_scale"], f"pool_{tag}")
    pa = _mm(ya, wa, mode="nt", tm=1024, tn=D, tk=GW, n=D, out_dtype=F32, after=token, name=f"proj_a_{tag}")
    pb = _mm(yb, wbc, mode="nt", tm=1024, tn=D, tk=BW, n=D, b_row0=0, out_dtype=F32, name=f"proj_b_{tag}")
    pc = _mm(yc, wbc, mode="nt", tm=1024, tn=D, tk=BW, n=D, b_row0=D, out_dtype=F32, name=f"proj_c_{tag}")
    merged = _gates_fwd(z, sp["b_gate"], pa, pb, pc, f"gates_{tag}")
    out = _mm(merged, wo, mode="nn", tm=1024, tn=D, tk=D, n=D, out_dtype=F32, res=x, name=f"out_{tag}")
    return out, (x, h, z, ya, lse, yb, yc, pa, pb, pc, merged)


def _mixer_bwd(dout, saved, sp, ww, wo, wa, wbc, tag, after, emit):
    x, h, z, ya, lse, yb, yc, pa, pb, pc, merged = saved
    grw = lax.empty((1, NDEV, INS, D), BF16)
    gro = lax.empty((1, NDEV, DS, D), BF16)
    gra = lax.empty((1, NDEV, DS, GW), BF16)
    grbc = lax.empty((2, NDEV, DS, BW), BF16)
    dm = _mm(dout, wo, mode="nt", tm=1024, tn=D, tk=D, n=D, out_dtype=F32, after=after, name=f"dmerged_{tag}")
    gro = _wgrad(merged, dout, gro, piece0=0, rows=DS, tm=D, tn=D, tk=1024, name=f"wg_out_{tag}")
    dpa, dpb, dpc, dza, dzb, dzc, dba, dbb, dbc = _gates_bwd(z, sp["b_gate"], pa, pb, pc, dm, f"dgates_{tag}")
    dya = _mm(dpa, wa, mode="nn", tm=2048, tn=GW, tk=D, n=GW, out_dtype=F32, name=f"dya_{tag}")
    dyb = _mm(dpb, wbc, mode="nn", tm=2048, tn=BW, tk=D, n=BW, b_row0=0, out_dtype=F32, name=f"dyb_{tag}")
    dyc = _mm(dpc, wbc, mode="nn", tm=2048, tn=BW, tk=D, n=BW, b_row0=D, out_dtype=F32, name=f"dyc_{tag}")
    gra = _wgrad(dpa, ya, gra, piece0=0, rows=DS, tm=D, tn=GW, tk=1024, name=f"wg_pa_{tag}")
    grbc = _wgrad(dpb, yb, grbc, piece0=0, rows=DS, tm=D, tn=BW, tk=1024, name=f"wg_pb_{tag}")
    grbc = _wgrad(dpc, yc, grbc, piece0=1, rows=DS, tm=D, tn=BW, tk=1024, name=f"wg_pc_{tag}")
    dqkv = [_attn_bwd(z, dya, ya, lse, gi=gi, name=f"dattn{gi}_{tag}") for gi in range(3)]
    duv, dws, dbs, dlng, dlnb = _gmlp_bwd(z, dyb, sp["b_ln_g"], sp["b_ln_b"], sp["b_w_s"], sp["b_b_s"], f"dgmlp_{tag}")
    dxc, dcw, dcs = _pool_bwd(z, dyc, sp["c_w"], sp["c_scale"], f"dpool_{tag}")
    dz = jnp.concatenate([dqkv[gi][part].astype(BF16) for part in range(3) for gi in range(3)] + [duv, dxc, dza, dzb, dzc],
                         axis=1)
    grw = _wgrad(dz, h, grw, piece0=0, rows=INS, tm=INW // 2, tn=512, tk=1024, name=f"wg_in_{tag}")
    token = emit([grw, gro, gra, grbc])
    dx, dgain = _mm_drms(dz, ww, x, sp["mix_norm"], dout, tm=256, b_row0=0, after=token, name=f"dx_mix_{tag}")
    small = {"mix_norm": dgain, "b_gate": [dba, dbb, dbc], "b_ln_g": dlng, "b_ln_b": dlnb,
             "b_w_s": dws, "b_b_s": dbs, "c_w": dcw, "c_scale": dcs}
    return dx, small, token


PARTS = ("f1", "mix", "f2")


def _forward(xs, tgt, weights_of, pass_on, small, final_gain):
    saved, views = [], []
    cur = xs
    for l in range(DEPTH):
        (wf1,) = weights_of(l, "f1", cur)
        cur, s1 = _ffn_fwd(cur, small[l]["ffn1_norm"], wf1, f"f1l{l}", lambda t: pass_on(l, "mix", t))
        wmix = weights_of(l, "mix", cur)
        cur, s2 = _mixer_fwd(cur, small[l], *wmix, f"l{l}", lambda t: pass_on(l, "f2", t))
        (wf2,) = weights_of(l, "f2", cur)
        cur, s3 = _ffn_fwd(cur, small[l]["ffn2_norm"], wf2, f"f2l{l}",
                           lambda t: pass_on(l + 1, "f1", t) if l + 1 < DEPTH else None)
        saved.append((s1, s2, s3))
        views.append((wf1, wmix, wf2))
    loss_part, dcur, dfinal = _final_loss(cur, final_gain, tgt, "final_loss")
    return loss_part, dcur, dfinal, saved, views


def _backward_layer(dcur, saved_l, small_l, views_l, l, emit, after):
    wf1, wmix, wf2 = views_l
    s1, s2, s3 = saved_l
    dcur, dn2, token = _ffn_bwd(dcur, s3, small_l["ffn2_norm"], wf2, f"f2l{l}", after, lambda g: emit("f2", g))
    dcur, sg, token = _mixer_bwd(dcur, s2, small_l, *wmix, f"l{l}", token, lambda g: emit("mix", g))
    dcur, dn1, token = _ffn_bwd(dcur, s1, small_l["ffn1_norm"], wf1, f"f1l{l}", token, lambda g: emit("f1", g))
    sg["ffn1_norm"] = dn1
    sg["ffn2_norm"] = dn2
    return dcur, sg, token


BIG = ("ffn1_w_gate", "ffn1_w_up", "ffn1_w_down", "w_in", "w_proj_a", "w_proj_b", "w_proj_c", "w_out",
       "ffn2_w_gate", "ffn2_w_up", "ffn2_w_down")
ALL_WEIGHTS = ("ffn1_norm", "ffn1_w_gate", "ffn1_w_up", "ffn1_w_down", "mix_norm", "w_in", "b_gate", "b_ln_g", "b_ln_b",
               "b_w_s", "b_b_s", "c_w", "c_scale", "w_proj_a", "w_proj_b", "w_proj_c", "w_out", "ffn2_norm", "ffn2_w_gate",
               "ffn2_w_up", "ffn2_w_down", "final_norm")


def _place_shards(w, l, part, me, after, name):
    tr = lambda a: a.transpose(0, 2, 1)
    if part == "mix":
        srcs = [tr(w["w_in"]), w["w_out"], tr(w["w_proj_a"]), tr(w["w_proj_b"]), tr(w["w_proj_c"])]
        groups = [[0], [1], [2], [3, 4]]
    else:
        f = "ffn1" if part == "f1" else "ffn2"
        srcs = [tr(w[f + "_w_gate"]), tr(w[f + "_w_up"]), w[f + "_w_down"]]
        groups = [[0, 1, 2]]

    def body(me_ref, *refs):
        del me_ref
        outs = refs[len(srcs) + 1:]
        for o_ref, members in zip(outs, groups):
            for p, i in enumerate(members):
                o_ref[p, 0] = refs[i][0].astype(BF16)

    out_shape = [_sds((len(g), NDEV) + srcs[g[0]].shape[1:], BF16) for g in groups]
    grid_spec = pltpu.PrefetchScalarGridSpec(
        num_scalar_prefetch=1, grid=(1,),
        in_specs=[_bs((1,) + s.shape[1:], lambda i, me_ref: (l, 0, 0)) for s in srcs] + [pl.BlockSpec(memory_space=pl.ANY)],
        out_specs=[_bs((len(g), 1) + srcs[g[0]].shape[1:], lambda i, me_ref: (0, me_ref[0], 0, 0)) for g in groups])
    return pl.pallas_call(body, name=name, grid_spec=grid_spec, out_shape=out_shape,
                          compiler_params=pltpu.CompilerParams(vmem_limit_bytes=40 << 20), interpret=False)(me, *srcs, after)


def _part_views(part, gathered):
    if part == "mix":
        pw, po, pa, pbc = gathered
        return [pw.reshape(INW, D), po.reshape(D, D), pa.reshape(D, GW), pbc.reshape(2 * D, BW)]
    return [gathered[0].reshape(3 * FF, D)]


def _part_pieces(part):
    if part == "mix":
        return [(0, 0, "w_in", True), (1, 0, "w_out", False), (2, 0, "w_proj_a", True), (3, 0, "w_proj_b", True),
                (3, 1, "w_proj_c", True)]
    f = "ffn1" if part == "f1" else "ffn2"
    return [(0, 0, f + "_w_gate", True), (0, 1, f + "_w_up", True), (0, 2, f + "_w_down", False)]


def _layer_small(w, l):
    r = lambda a, shape: a.reshape(shape)
    return {"ffn1_norm": r(w["ffn1_norm"][l], (1, D)), "mix_norm": r(w["mix_norm"][l], (1, D)),
            "b_gate": r(w["b_gate"][l], (1, 3 * D)), "b_ln_g": r(w["b_ln_g"][l], (1, BW)), "b_ln_b": r(w["b_ln_b"][l], (1, BW)),
            "b_w_s": w["b_w_s"][l], "b_b_s": r(w["b_b_s"][l], (4, BLK, 1)), "c_w": w["c_w"][l],
            "c_scale": r(w["c_scale"][l], (4, 1, BLK)), "ffn2_norm": r(w["ffn2_norm"][l], (1, D))}


def kernel(x, ffn1_norm, ffn1_w_gate, ffn1_w_up, ffn1_w_down, mix_norm, w_in, b_gate, b_ln_g, b_ln_b, b_w_s, b_b_s, c_w, c_scale, w_proj_a, w_proj_b, w_proj_c, w_out, ffn2_norm, ffn2_w_gate, ffn2_w_up, ffn2_w_down, final_norm, loss_target, m_ffn1_norm, m_ffn1_w_gate, m_ffn1_w_up, m_ffn1_w_down, m_mix_norm, m_w_in, m_b_gate, m_b_ln_g, m_b_ln_b, m_b_w_s, m_b_b_s, m_c_w, m_c_scale, m_w_proj_a, m_w_proj_b, m_w_proj_c, m_w_out, m_ffn2_norm, m_ffn2_w_gate, m_ffn2_w_up, m_ffn2_w_down, m_final_norm, v_ffn1_norm, v_ffn1_w_gate, v_ffn1_w_up, v_ffn1_w_down, v_mix_norm, v_w_in, v_b_gate, v_b_ln_g, v_b_ln_b, v_b_w_s, v_b_b_s, v_c_w, v_c_scale, v_w_proj_a, v_w_proj_b, v_w_proj_c, v_w_out, v_ffn2_norm, v_ffn2_w_gate, v_ffn2_w_up, v_ffn2_w_down, v_final_norm):
    w = dict(ffn1_norm=ffn1_norm, ffn1_w_gate=ffn1_w_gate, ffn1_w_up=ffn1_w_up, ffn1_w_down=ffn1_w_down, mix_norm=mix_norm,
             w_in=w_in, b_gate=b_gate, b_ln_g=b_ln_g, b_ln_b=b_ln_b, b_w_s=b_w_s, b_b_s=b_b_s, c_w=c_w, c_scale=c_scale,
             w_proj_a=w_proj_a, w_proj_b=w_proj_b, w_proj_c=w_proj_c, w_out=w_out, ffn2_norm=ffn2_norm, ffn2_w_gate=ffn2_w_gate,
             ffn2_w_up=ffn2_w_up, ffn2_w_down=ffn2_w_down, final_norm=final_norm)
    mom = dict(ffn1_norm=m_ffn1_norm, ffn1_w_gate=m_ffn1_w_gate, ffn1_w_up=m_ffn1_w_up, ffn1_w_down=m_ffn1_w_down,
               mix_norm=m_mix_norm, w_in=m_w_in, b_gate=m_b_gate, b_ln_g=m_b_ln_g, b_ln_b=m_b_ln_b, b_w_s=m_b_w_s, b_b_s=m_b_b_s,
               c_w=m_c_w, c_scale=m_c_scale, w_proj_a=m_w_proj_a, w_proj_b=m_w_proj_b, w_proj_c=m_w_proj_c, w_out=m_w_out,
               ffn2_norm=m_ffn2_norm, ffn2_w_gate=m_ffn2_w_gate, ffn2_w_up=m_ffn2_w_up, ffn2_w_down=m_ffn2_w_down,
               final_norm=m_final_norm)
    var = dict(ffn1_norm=v_ffn1_norm, ffn1_w_gate=v_ffn1_w_gate, ffn1_w_up=v_ffn1_w_up, ffn1_w_down=v_ffn1_w_down,
               mix_norm=v_mix_norm, w_in=v_w_in, b_gate=v_b_gate, b_ln_g=v_b_ln_g, b_ln_b=v_b_ln_b, b_w_s=v_b_w_s, b_b_s=v_b_b_s,
               c_w=v_c_w, c_scale=v_c_scale, w_proj_a=v_w_proj_a, w_proj_b=v_w_proj_b, w_proj_c=v_w_proj_c, w_out=v_w_out,
               ffn2_norm=v_ffn2_norm, ffn2_w_gate=v_ffn2_w_gate, ffn2_w_up=v_ffn2_w_up, ffn2_w_down=v_ffn2_w_down,
               final_norm=v_final_norm)

    t_dim = x.shape[1]
    xs = x.reshape(t_dim, D)
    tgt = loss_target.reshape(t_dim, D)

    me = 4 * lax.axis_index("x") + 2 * lax.axis_index("y") + lax.axis_index("c")
    me_arr = me.astype(jnp.int32).reshape(1)
    gathers = {}
    token = jnp.zeros((8, 128), F32)
    for l in range(DEPTH):
        for part in PARTS:
            lands = _place_shards(w, l, part, me_arr, token, f"place_{part}_l{l}")
            gathers[l, part] = _gather_start(lands, token, f"gather_start_{part}_l{l}")
            token = gathers[l, part][-1]

    def pass_on(l, part, after):
        gathers[l, part] = _gather_forward(gathers[l, part], after, f"gather_forward_{part}_l{l}")
        return gathers[l, part][-1]

    pass_on(0, "f1", token)

    def weights_of(l, part, after):
        return _part_views(part, _gather_wait(gathers[l, part], after, f"gather_wait_{part}_l{l}"))

    small = [_layer_small(w, l) for l in range(DEPTH)]
    loss_part, dcur, dfinal, saved, views = _forward(xs, tgt, weights_of, pass_on, small, w["final_norm"].reshape(1, D))
    loss = lax.psum(loss_part[0, 0], ("x", "y", "c"))

    scatters = []
    small_gathers = [None] * DEPTH
    token = None
    for l in reversed(range(DEPTH)):
        def emit(part, grads, l=l):
            scatters.append((l, part, _exchange_start("scatter", grads, dfinal, f"scatter_start_{part}_l{l}")))
            return scatters[-1][2][-1]
        dcur, small_grads, token = _backward_layer(dcur, saved[l], small[l], views[l], l, emit, token)
        packed = _pack_small_grads(small_grads, dfinal if l == DEPTH - 1 else None, me_arr, f"pack_small_grads_l{l}")
        small_gathers[l] = _exchange_start("gather", [packed], token, f"gather_start_small_l{l}")
        token = small_gathers[l][-1]
    grad_x = dcur.reshape(x.shape)

    done = {}
    behind = token
    for l, part, started in scatters:
        arrs = _exchange_wait("scatter", started, behind, f"scatter_wait_{part}_l{l}")
        na = len(arrs) // 2
        for a, piece, k, transposed in _part_pieces(part):
            flip = transposed and w[k].shape[2] % 128 != 0
            view = (lambda t: t.transpose(0, 2, 1)) if flip else (lambda t: t)
            done[k] = _reduce_update(arrs[na + a], arrs[a], me_arr, piece, view(w[k]), view(mom[k]), view(var[k]), l,
                                     done.get(k), transposed and not flip, f"update_{k}_l{l}")
            behind = done[k][1]
    unflip = lambda k, t: t.transpose(0, 2, 1) if t.shape != w[k].shape else t
    grads, delta, new_m, new_v = ({k: unflip(k, done[k][t]) for k in BIG} for t in range(4))
    gathered = [None] * DEPTH
    for l in reversed(range(DEPTH)):
        gathered[l] = _exchange_wait("gather", small_gathers[l], behind, f"gather_wait_small_l{l}")[0]
        behind = gathered[l]
    as_row = lambda t: dict(t, final_norm=t["final_norm"].reshape(1, D))
    for dst, src in zip((grads, delta, new_m, new_v), _adamw_small(gathered, as_row(w), as_row(mom), as_row(var), "adamw_small")):
        dst.update(src)
        dst["final_norm"] = src["final_norm"].reshape(D)

    return (loss, grad_x, *[grads[k] for k in ALL_WEIGHTS], *[delta[k] for k in ALL_WEIGHTS],
            *[new_m[k] for k in ALL_WEIGHTS], *[new_v[k] for k in ALL_WEIGHTS])
```

```python
import math

import jax
import jax.numpy as jnp
from jax import lax
from jax.experimental import pallas as pl
from jax.experimental.pallas import tpu as pltpu

F32 = jnp.float32
BF16 = jnp.bfloat16
MESH = pl.DeviceIdType.MESH

NDEV = 8
DEPTH = 2
D = 1024
FF = 2816
FFS = FF // NDEV
INW = 6912
INS = INW // NDEV
DS = D // NDEV
BLK = 128
NH = 4
HD = 64
GW = NH * HD
DILS = (1, 4, 16)
QK_SCALE = 1.0 / math.sqrt(HD)
ZB = INW // GW
Q_B, K_B, V_B = 0, 3, 6
UV_B = 9
C_B128 = 26
G_B = 15
BW = 512
EPS = 1e-6
NEG = -1e30
VMEM_CAP = 60 * 1024 * 1024

ADAM_LR, ADAM_B1, ADAM_B2, ADAM_EPS, ADAM_WD, ADAM_STEP = 0.001, 0.9, 0.999, 1e-08, 0.01, 10

NT = (((1,), (1,)), ((), ()))
TN = (((0,), (0,)), ((), ()))


def _bs(shape, imap):
    return pl.BlockSpec(shape, imap)


def _pc(body, *, name, grid, in_specs, out_specs, out_shape, scratch=(), alias=None, vmem=None):
    params = {}
    if vmem is not None:
        params["vmem_limit_bytes"] = min(int(vmem), VMEM_CAP)
    big = lambda s: math.prod(s.shape) * jnp.dtype(s.dtype).itemsize >= (1 << 20)
    pin = lambda s: pltpu.HBM(s.shape, s.dtype) if big(s) else s
    out_shape = [pin(s) for s in out_shape] if isinstance(out_shape, (list, tuple)) else pin(out_shape)
    call = pl.pallas_call(
        body, name=name, grid=grid, in_specs=in_specs, out_specs=out_specs, out_shape=out_shape,
        scratch_shapes=list(scratch), input_output_aliases=alias or {},
        compiler_params=pltpu.CompilerParams(**params), interpret=False)
    return lambda *args: call(*[pltpu.with_memory_space_constraint(a, pltpu.HBM) if big(a) else a for a in args])


def _sds(shape, dtype):
    return jax.ShapeDtypeStruct(shape, dtype)


def _row_tile(rows, align, cap):
    best = None
    for cand in range(align, min(rows, cap) + 1, align):
        if rows % cand == 0:
            best = cand
    assert best is not None, (rows, align, cap)
    return best


def _mm(a, b, *, mode, tm, tn, tk, n, out_dtype, name, b_row0=0, scale=None, res=None, after=None):
    m, k_dim = a.shape
    nk = k_dim // tk
    assert m % tm == 0 and n % tn == 0 and k_dim % tk == 0
    b_off, rem = divmod(b_row0, tk if mode == "nn" else tn)
    assert rem == 0
    grid = (m // tm, n // tn, nk)
    in_specs = [_bs((tm, tk), lambda i, j, k: (i, k))]
    if mode == "nn":
        in_specs.append(_bs((tk, tn), lambda i, j, k: (b_off + k, j)))
    else:
        in_specs.append(_bs((tn, tk), lambda i, j, k: (b_off + j, k)))
    args = [a, b]
    if res is not None:
        in_specs.append(_bs((tm, tn), lambda i, j, k: (i, j)))
        args.append(res)
    if after is not None:
        in_specs.append(pl.BlockSpec(memory_space=pl.ANY))
        args.append(after)
    n_in = len(args)

    def body(*refs):
        a_ref, b_ref = refs[0], refs[1]
        r_ref = refs[2] if res is not None else None
        o_ref = refs[n_in]
        av = a_ref[...].astype(BF16)
        bv = b_ref[...].astype(BF16)
        if mode == "nn":
            p = jnp.dot(av, bv, preferred_element_type=F32)
        else:
            p = lax.dot_general(av, bv, NT, preferred_element_type=F32)

        def fin(v):
            if scale is not None:
                v = v * scale
            if r_ref is not None:
                v = r_ref[...] + v
            o_ref[...] = v.astype(out_dtype)

        if nk == 1:
            fin(p)
        else:
            acc_ref = refs[-1]
            kk = pl.program_id(2)

            @pl.when(kk == 0)
            def _():
                acc_ref[...] = p

            @pl.when(kk > 0)
            def _():
                acc_ref[...] += p

            @pl.when(kk == nk - 1)
            def _():
                fin(acc_ref[...])

    scratch = [pltpu.VMEM((tm, tn), F32)] if nk > 1 else []
    est = 2 * (tm * tk * a.dtype.itemsize + tk * tn * b.dtype.itemsize + tm * tn * jnp.dtype(out_dtype).itemsize)
    est += (2 + (nk > 1)) * tm * tn * 4 + (2 * tm * tn * 4 if res is not None else 0)
    est += (tm * tk * 2 if a.dtype != BF16 else 0) + (tk * tn * 2 if b.dtype != BF16 else 0)
    return _pc(body, name=name, grid=grid, in_specs=in_specs, out_specs=_bs((tm, tn), lambda i, j, k: (i, j)),
               out_shape=_sds((m, n), out_dtype), scratch=scratch, vmem=est + (8 << 20))(*args)


def _wgrad(a, b, buf, *, piece0, rows, tm, tn, tk, name, scale=None):
    t_dim, m = a.shape
    n = b.shape[1]
    db = tm // rows
    tpp = NDEV // db
    nk = t_dim // tk
    assert tm % rows == 0 and NDEV % db == 0 and m % tm == 0 and n % tn == 0 and t_dim % tk == 0
    grid = (m // tm, n // tn, nk)

    def body(a_ref, b_ref, buf_ref, o_ref, acc_ref):
        del buf_ref
        kk = pl.program_id(2)
        p = lax.dot_general(a_ref[...].astype(BF16), b_ref[...].astype(BF16), TN, preferred_element_type=F32)

        @pl.when(kk == 0)
        def _():
            acc_ref[...] = p

        @pl.when(kk > 0)
        def _():
            acc_ref[...] += p

        @pl.when(kk == nk - 1)
        def _():
            for d in range(db):
                v = acc_ref[d * rows:(d + 1) * rows, :]
                if scale is not None:
                    v = v * scale
                o_ref[0, d] = v.astype(BF16)

    est = 2 * (tk * tm * a.dtype.itemsize + tk * tn * b.dtype.itemsize + tm * tn * 2) + 3 * tm * tn * 4 + tk * tm * 4
    return _pc(
        body, name=name, grid=grid,
        in_specs=[_bs((tk, tm), lambda i, j, k: (k, i)), _bs((tk, tn), lambda i, j, k: (k, j)),
                  pl.BlockSpec(memory_space=pl.ANY)],
        out_specs=_bs((1, db, rows, tn), lambda i, j, k: (piece0 + i // tpp, i % tpp, 0, j)),
        out_shape=_sds(buf.shape, buf.dtype), scratch=[pltpu.VMEM((tm, tn), F32)], alias={2: 0},
        vmem=est + (8 << 20))(a, b, buf)


def _mm_drms(a, b, x, gain, dres, *, tm, b_row0, after, name):
    m, k_dim = a.shape
    assert m % tm == 0 and b_row0 % k_dim == 0 and b.shape[1] == D
    k_blk = b_row0 // k_dim
    after = jnp.zeros((8, 128), F32) if after is None else after

    def body(a_ref, b_ref, x_ref, g_ref, dr_ref, after_ref, dx_ref, dg_ref):
        del after_ref
        dh_v = jnp.dot(a_ref[...].astype(BF16), b_ref[...].astype(BF16), preferred_element_type=F32)
        xv = x_ref[...]
        r = lax.rsqrt(jnp.mean(xv * xv, axis=-1, keepdims=True) + EPS)
        dyg = dh_v * g_ref[...]
        mq = jnp.mean(dyg * xv, axis=-1, keepdims=True)
        dx_ref[...] = dr_ref[...] + r * (dyg - xv * (r * r * mq))

        @pl.when(pl.program_id(0) == 0)
        def _():
            dg_ref[...] = jnp.zeros_like(dg_ref)

        dg_ref[...] += jnp.sum(dh_v * (xv * r), axis=0, keepdims=True)

    row = _bs((tm, D), lambda i: (i, 0))
    est = 2 * (tm * k_dim * a.dtype.itemsize + k_dim * D * b.dtype.itemsize + 3 * tm * D * 4) + 4 * tm * D * 4
    return _pc(body, name=name, grid=(m // tm,),
               in_specs=[_bs((tm, k_dim), lambda i: (i, 0)), _bs((k_dim, D), lambda i: (k_blk, 0)), row,
                         _bs((1, D), lambda i: (0, 0)), row, pl.BlockSpec(memory_space=pl.ANY)],
               out_specs=[row, _bs((1, D), lambda i: (0, 0))], out_shape=[_sds((m, D), F32), _sds((1, D), F32)],
               vmem=est + (8 << 20))(a, b, x, gain, dres, after)


def _dact_dswiglu(dout, b, gu, *, tm, b_row0, after, name):
    t_dim = dout.shape[0]
    assert t_dim % tm == 0 and b_row0 % FF == 0
    k_blk = b_row0 // FF
    after = jnp.zeros((8, 128), F32) if after is None else after

    def body(d_ref, b_ref, gu_ref, after_ref, o_ref):
        del after_ref
        dav = 0.5 * lax.dot_general(d_ref[...].astype(BF16), b_ref[...].astype(BF16), NT, preferred_element_type=F32)
        g = gu_ref[:, :FF].astype(F32)
        u = gu_ref[:, FF:].astype(F32)
        s = _sigmoid(g)
        o_ref[:, :FF] = (dav * u * (s * (1.0 + g * (1.0 - s)))).astype(BF16)
        o_ref[:, FF:] = (dav * (g * s)).astype(BF16)

    est = 2 * (tm * D * 4 + FF * D * 2 + 2 * tm * 2 * FF * 2) + 5 * tm * FF * 4
    return _pc(body, name=name, grid=(t_dim // tm,),
               in_specs=[_bs((tm, D), lambda i: (i, 0)), _bs((FF, D), lambda i: (k_blk, 0)),
                         _bs((tm, 2 * FF), lambda i: (i, 0)), pl.BlockSpec(memory_space=pl.ANY)],
               out_specs=_bs((tm, 2 * FF), lambda i: (i, 0)), out_shape=_sds((t_dim, 2 * FF), BF16),
               vmem=est + (8 << 20))(dout, b, gu, after)


def _rms_fwd(x, gain, name, tm=512):
    t_dim = x.shape[0]

    def body(x_ref, g_ref, o_ref):
        xv = x_ref[...]
        r = lax.rsqrt(jnp.mean(xv * xv, axis=-1, keepdims=True) + EPS)
        o_ref[...] = (xv * r * g_ref[...]).astype(BF16)

    return _pc(body, name=name, grid=(t_dim // tm,),
               in_specs=[_bs((tm, D), lambda i: (i, 0)), _bs((1, D), lambda i: (0, 0))],
               out_specs=_bs((tm, D), lambda i: (i, 0)), out_shape=_sds((t_dim, D), BF16))(x, gain)


def _sigmoid(v):
    return 1.0 / (1.0 + jnp.exp(-v))


def _swiglu_fwd(gu, name, tm=256):
    t_dim = gu.shape[0]

    def body(gu_ref, o_ref):
        g = gu_ref[:, :FF].astype(F32)
        u = gu_ref[:, FF:].astype(F32)
        o_ref[...] = (g * _sigmoid(g) * u).astype(BF16)

    return _pc(body, name=name, grid=(t_dim // tm,), in_specs=[_bs((tm, 2 * FF), lambda i: (i, 0))],
               out_specs=_bs((tm, FF), lambda i: (i, 0)), out_shape=_sds((t_dim, FF), BF16))(gu)


def _final_loss(x, gain, tgt, name, tm=512):
    t_dim = x.shape[0]

    def body(x_ref, g_ref, t_ref, loss_ref, dx_ref, dg_ref):
        xv = x_ref[...]
        gv = g_ref[...]
        r = lax.rsqrt(jnp.mean(xv * xv, axis=-1, keepdims=True) + EPS)
        xn = xv * r
        err = xn * gv - t_ref[...]
        dy = err * (1.0 / D)
        dyg = dy * gv
        mq = jnp.mean(dyg * xv, axis=-1, keepdims=True)
        dx_ref[...] = r * (dyg - xv * (r * r * mq))

        @pl.when(pl.program_id(0) == 0)
        def _():
            dg_ref[...] = jnp.zeros_like(dg_ref)
            loss_ref[...] = jnp.zeros_like(loss_ref)

        dg_ref[...] += jnp.sum(dy * xn, axis=0, keepdims=True)
        part = 0.5 * jnp.sum(jnp.mean(err * err, axis=-1, keepdims=True), axis=0, keepdims=True)
        loss_ref[...] += jnp.broadcast_to(part, loss_ref.shape)

    return _pc(body, name=name, grid=(t_dim // tm,),
               in_specs=[_bs((tm, D), lambda i: (i, 0)), _bs((1, D), lambda i: (0, 0)), _bs((tm, D), lambda i: (i, 0))],
               out_specs=[_bs((8, 128), lambda i: (0, 0)), _bs((tm, D), lambda i: (i, 0)), _bs((1, D), lambda i: (0, 0))],
               out_shape=[_sds((8, 128), F32), _sds((t_dim, D), F32), _sds((1, D), F32)])(x, gain, tgt)


def _band_mask(n):
    ri = lax.broadcasted_iota(jnp.int32, (BLK, 2 * BLK), 0)
    cj = lax.broadcasted_iota(jnp.int32, (BLK, 2 * BLK), 1)
    dist = ri + BLK - cj
    return (dist >= 0) & (dist <= BLK) & ((cj >= BLK) | (n > 0))


def _halves(unit, imap_rows, col):
    return [_bs((unit, 128), (lambda u, j=j: (imap_rows(u), 2 * col + j))) for j in range(2)]


def _get_rows(pair, r, dil):
    return jnp.concatenate([h[pl.ds(r, BLK, stride=dil), :] for h in pair], axis=1)


def _put_rows(stage, r, dil, val):
    for j in range(2):
        stage[j, pl.ds(r, BLK, stride=dil), :] = val[:, 128 * j:128 * (j + 1)]


def _unstage(stage, out_ref):
    out_ref[:, :128] = stage[0]
    out_ref[:, 128:] = stage[1]


def _attn_fwd(z, acc, ml, *, gi, first, final, name):
    dil = DILS[gi]
    t_dim = z.shape[0]
    unit = BLK * dil
    nu = t_dim // unit
    own = lambda u: u
    prev = lambda u: jnp.maximum(u - 1, 0)
    in_specs = (_halves(unit, own, Q_B + gi) + _halves(unit, own, K_B + gi) + _halves(unit, prev, K_B + gi)
                + _halves(unit, own, V_B + gi) + _halves(unit, prev, V_B + gi))
    args = [z] * 10
    st_ml = _bs((unit, 128), lambda u: (u, 0))
    if not first:
        in_specs += _halves(unit, own, 0) + [st_ml]
        args += [acc, acc, ml]

    def body(*refs):
        q2, ko2, kp2, vo2, vp2 = (refs[2 * i:2 * i + 2] for i in range(5))
        o_acc_ref, o_ml_ref, stage = refs[-3], refs[-2], refs[-1]
        valid = _band_mask(pl.program_id(0))
        lane = lax.broadcasted_iota(jnp.int32, (BLK, GW), 1)
        lane_s = lax.broadcasted_iota(jnp.int32, (BLK, 128), 1)

        def band(r, carry):
            q = _get_rows(q2, r, dil) * QK_SCALE
            kb = jnp.concatenate([_get_rows(kp2, r, dil), _get_rows(ko2, r, dil)], axis=0).astype(BF16)
            vb = jnp.concatenate([_get_rows(vp2, r, dil), _get_rows(vo2, r, dil)], axis=0).astype(BF16)
            if first:
                acc_v = jnp.zeros((BLK, GW), F32)
                ml_v = jnp.where(lane_s < 64, NEG, 0.0).astype(F32)
            else:
                acc_v = _get_rows(refs[10:12], r, dil)
                ml_v = refs[12][pl.ds(r, BLK, stride=dil), :]
            heads = [(lane >= h * HD) & (lane < (h + 1) * HD) for h in range(NH)]
            rows = lambda v, h: v[h * BLK:(h + 1) * BLK]
            qs = jnp.concatenate([jnp.where(hm, q, 0.0) for hm in heads], axis=0).astype(BF16)
            s = lax.dot_general(qs, kb, NT, preferred_element_type=F32)
            s = jnp.where(jnp.concatenate([valid] * NH, axis=0), s, NEG)
            m_old = jnp.concatenate([ml_v[:, 16 * h:16 * h + 1] for h in range(NH)], axis=0)
            l_old = jnp.concatenate([ml_v[:, 64 + 16 * h:64 + 16 * h + 1] for h in range(NH)], axis=0)
            m_new = jnp.maximum(m_old, jnp.max(s, axis=-1, keepdims=True))
            alpha = jnp.exp(m_old - m_new)
            p = jnp.exp(s - m_new)
            l_new = alpha * l_old + jnp.sum(p, axis=-1, keepdims=True)
            pv = jnp.dot(p.astype(BF16), vb, preferred_element_type=F32)
            stat = m_new + jnp.log(l_new) if final else m_new
            acc_o, ml_o = acc_v, ml_v
            l_b = jnp.ones((BLK, GW), F32)
            for h, hm in enumerate(heads):
                acc_o = jnp.where(hm, acc_v * rows(alpha, h) + rows(pv, h), acc_o)
                l_b = jnp.where(hm, rows(l_new, h), l_b)
                ml_o = jnp.where((lane_s >= 16 * h) & (lane_s < 16 * h + 16), rows(stat, h), ml_o)
                if not final:
                    ml_o = jnp.where((lane_s >= 64 + 16 * h) & (lane_s < 64 + 16 * h + 16), rows(l_new, h), ml_o)
            _put_rows(stage, r, dil, acc_o / l_b if final else acc_o)
            o_ml_ref[pl.ds(r, BLK, stride=dil), :] = ml_o
            return carry

        lax.fori_loop(0, dil, band, 0, unroll=min(dil, 4))
        _unstage(stage, o_acc_ref)

    blocks = (len(in_specs) + 6) * unit * 128 * 4
    return _pc(body, name=name, grid=(nu,), in_specs=in_specs, out_specs=[_bs((unit, GW), lambda u: (u, 0)), st_ml],
               out_shape=[_sds((t_dim, GW), F32), _sds((t_dim, 128), F32)],
               scratch=[pltpu.VMEM((2, unit, 128), F32)], vmem=2 * blocks + (12 << 20))(*args)


def _attn_bwd(z, dy, y, lse, *, gi, name):
    dil = DILS[gi]
    t_dim = z.shape[0]
    unit = BLK * dil
    nu = t_dim // unit

    own = lambda u: jnp.minimum(u, nu - 1)
    prev = lambda u: jnp.maximum(jnp.minimum(u, nu - 1) - 1, 0)
    in_specs = (_halves(unit, own, Q_B + gi) + _halves(unit, own, K_B + gi) + _halves(unit, prev, K_B + gi)
                + _halves(unit, own, V_B + gi) + _halves(unit, prev, V_B + gi)
                + _halves(unit, own, 0) + _halves(unit, own, 0) + [_bs((unit, 128), lambda u: (own(u), 0))])
    tok = _bs((unit, GW), lambda u: (own(u), 0))
    kv_out = _bs((unit, GW), lambda u: (jnp.maximum(u - 1, 0), 0))

    def body(*refs):
        q2, ko2, kp2, vo2, vp2, dy2, y2 = (refs[2 * i:2 * i + 2] for i in range(7))
        lse_ref = refs[14]
        dq_ref, dk_ref, dv_ref, dq_st, dk_st, dv_st, dkc, dvc = refs[15:]
        u = pl.program_id(0)

        @pl.when(u == 0)
        def _():
            dkc[...] = jnp.zeros_like(dkc)
            dvc[...] = jnp.zeros_like(dvc)

        @pl.when(u < nu)
        def _():
            valid = _band_mask(u)
            lane = lax.broadcasted_iota(jnp.int32, (BLK, GW), 1)

            def band(r, carry):
                q = _get_rows(q2, r, dil) * QK_SCALE
                kb = jnp.concatenate([_get_rows(kp2, r, dil), _get_rows(ko2, r, dil)], axis=0).astype(BF16)
                vb = jnp.concatenate([_get_rows(vp2, r, dil), _get_rows(vo2, r, dil)], axis=0).astype(BF16)
                dyv = _get_rows(dy2, r, dil)
                yv = _get_rows(y2, r, dil)
                lse_v = lse_ref[pl.ds(r, BLK, stride=dil), :]
                heads = [(lane >= h * HD) & (lane < (h + 1) * HD) for h in range(NH)]
                qs = jnp.concatenate([jnp.where(hm, q, 0.0) for hm in heads], axis=0).astype(BF16)
                dys = jnp.concatenate([jnp.where(hm, dyv, 0.0) for hm in heads], axis=0)
                lse_s = jnp.concatenate([lse_v[:, 16 * h:16 * h + 1] for h in range(NH)], axis=0)
                s = lax.dot_general(qs, kb, NT, preferred_element_type=F32)
                p = jnp.where(jnp.concatenate([valid] * NH, axis=0), jnp.exp(s - lse_s), 0.0)
                delta = jnp.sum(dys * jnp.concatenate([yv] * NH, axis=0), axis=-1, keepdims=True)
                dys16 = dys.astype(BF16)
                dp = lax.dot_general(dys16, vb, NT, preferred_element_type=F32)
                ds = (p * (dp - delta)).astype(BF16)
                dq_all = jnp.dot(ds, kb, preferred_element_type=F32)
                dq = jnp.zeros((BLK, GW), F32)
                for h, hm in enumerate(heads):
                    dq = jnp.where(hm, dq_all[h * BLK:(h + 1) * BLK], dq)
                dkb = lax.dot_general(ds, qs, TN, preferred_element_type=F32)
                dvb = lax.dot_general(p.astype(BF16), dys16, TN, preferred_element_type=F32)
                _put_rows(dq_st, r, dil, dq * QK_SCALE)
                _put_rows(dk_st, r, dil, _get_rows((dkc.at[0], dkc.at[1]), r, dil) + dkb[:BLK])
                _put_rows(dv_st, r, dil, _get_rows((dvc.at[0], dvc.at[1]), r, dil) + dvb[:BLK])
                _put_rows(dkc, r, dil, dkb[BLK:])
                _put_rows(dvc, r, dil, dvb[BLK:])
                return carry

            lax.fori_loop(0, dil, band, 0, unroll=min(dil, 4))
            _unstage(dq_st, dq_ref)

            @pl.when(u > 0)
            def _():
                _unstage(dk_st, dk_ref)
                _unstage(dv_st, dv_ref)

        @pl.when(u == nu)
        def _():
            _unstage(dkc, dk_ref)
            _unstage(dvc, dv_ref)

    blocks = (15 + 6 + 5) * unit * 128 * 4 * 2
    return _pc(body, name=name, grid=(nu + 1,), in_specs=in_specs, out_specs=[tok, kv_out, kv_out],
               out_shape=[_sds((t_dim, GW), F32)] * 3, scratch=[pltpu.VMEM((2, unit, 128), F32)] * 5,
               vmem=blocks + (12 << 20))(*([z] * 10), dy, dy, y, y, lse)


_INV_SQRT2 = 1.0 / math.sqrt(2.0)
_INV_SQRT2PI = 1.0 / math.sqrt(2.0 * math.pi)


def _gelu(v):
    return 0.5 * v * (1.0 + lax.erf(v * _INV_SQRT2))


def _gelu_grad(v):
    return 0.5 * (1.0 + lax.erf(v * _INV_SQRT2)) + v * (_INV_SQRT2PI * jnp.exp(-0.5 * v * v))


def _gmlp_core(z_refs, lng, lnb, ws_ref, bs_ref):
    u_pre = jnp.concatenate([z_refs[0][...], z_refs[1][...]], axis=1)
    v_pre = jnp.concatenate([z_refs[2][...], z_refs[3][...]], axis=1)
    u = _gelu(u_pre)
    v = _gelu(v_pre)
    xc = v - jnp.mean(v, axis=-1, keepdims=True)
    rstd = lax.rsqrt(jnp.mean(xc * xc, axis=-1, keepdims=True) + EPS)
    xhat = xc * rstd
    vn = xhat * lng + lnb
    ti = lax.broadcasted_iota(jnp.int32, (BLK, BLK), 0)
    si = lax.broadcasted_iota(jnp.int32, (BLK, BLK), 1)
    causal = ti >= si
    vn16 = vn.astype(BF16)
    ws = [jnp.where(causal, ws_ref[g], 0.0).astype(BF16) for g in range(4)]
    mixed = jnp.concatenate(
        [jnp.dot(ws[g], vn16[:, g * BLK:(g + 1) * BLK], preferred_element_type=F32) + bs_ref[g] for g in range(4)], axis=1)
    return u_pre, v_pre, u, xhat, rstd, vn16, ws, causal, mixed


def _gmlp_specs():
    z_specs = [_bs((BLK, GW), (lambda i, c=c: (i, UV_B + c))) for c in range(4)]
    par_specs = [_bs((1, BW), lambda i: (0, 0)), _bs((1, BW), lambda i: (0, 0)),
                 _bs((4, BLK, BLK), lambda i: (0, 0, 0)), _bs((4, BLK, 1), lambda i: (0, 0, 0))]
    return z_specs, par_specs


def _gmlp_fwd(z, lng, lnb, ws, bs, name):
    t_dim = z.shape[0]
    z_specs, par_specs = _gmlp_specs()

    def body(z0, z1, z2, z3, lng_ref, lnb_ref, ws_ref, bs_ref, o_ref):
        core = _gmlp_core((z0, z1, z2, z3), lng_ref[...], lnb_ref[...], ws_ref, bs_ref)
        o_ref[...] = (core[2] * core[8]).astype(BF16)

    return _pc(body, name=name, grid=(t_dim // BLK,), in_specs=z_specs + par_specs,
               out_specs=_bs((BLK, BW), lambda i: (i, 0)), out_shape=_sds((t_dim, BW), BF16))(z, z, z, z, lng, lnb, ws, bs)


def _gmlp_bwd(z, dy, lng, lnb, ws, bs, name):
    t_dim = z.shape[0]
    z_specs, par_specs = _gmlp_specs()

    def body(z0, z1, z2, z3, lng_ref, lnb_ref, ws_ref, bs_ref, dy_ref, duv_ref, dws_ref, dbs_ref, dlng_ref, dlnb_ref):
        lng_v = lng_ref[...]
        u_pre, v_pre, u, xhat, rstd, vn16, wsm, causal, mixed = _gmlp_core((z0, z1, z2, z3), lng_v, lnb_ref[...], ws_ref, bs_ref)
        dyv = dy_ref[...].astype(F32)
        du = dyv * mixed
        dmixed = dyv * u

        @pl.when(pl.program_id(0) == 0)
        def _():
            dws_ref[...] = jnp.zeros_like(dws_ref)
            dbs_ref[...] = jnp.zeros_like(dbs_ref)
            dlng_ref[...] = jnp.zeros_like(dlng_ref)
            dlnb_ref[...] = jnp.zeros_like(dlnb_ref)

        dvn_parts = []
        for g in range(4):
            dm = dmixed[:, g * BLK:(g + 1) * BLK]
            dm16 = dm.astype(BF16)
            dw = lax.dot_general(dm16, vn16[:, g * BLK:(g + 1) * BLK], NT, preferred_element_type=F32)
            dws_ref[g] += jnp.where(causal, dw, 0.0)
            dbs_ref[g] += jnp.sum(dm.T, axis=0, keepdims=True)
            dvn_parts.append(lax.dot_general(wsm[g], dm16, TN, preferred_element_type=F32))
        dvn = jnp.concatenate(dvn_parts, axis=1)
        dlng_ref[...] += jnp.sum(dvn * xhat, axis=0, keepdims=True)
        dlnb_ref[...] += jnp.sum(dvn, axis=0, keepdims=True)
        dxh = dvn * lng_v
        dv = rstd * (dxh - jnp.mean(dxh, axis=-1, keepdims=True) - xhat * jnp.mean(dxh * xhat, axis=-1, keepdims=True))
        duv_ref[:, :BW] = (du * _gelu_grad(u_pre)).astype(BF16)
        duv_ref[:, BW:] = (dv * _gelu_grad(v_pre)).astype(BF16)

    return _pc(body, name=name, grid=(t_dim // BLK,),
               in_specs=z_specs + par_specs + [_bs((BLK, BW), lambda i: (i, 0))],
               out_specs=[_bs((BLK, 2 * BW), lambda i: (i, 0)), _bs((4, BLK, BLK), lambda i: (0, 0, 0)),
                          _bs((4, 1, BLK), lambda i: (0, 0, 0)), _bs((1, BW), lambda i: (0, 0)), _bs((1, BW), lambda i: (0, 0))],
               out_shape=[_sds((t_dim, 2 * BW), BF16), _sds((4, BLK, BLK), F32), _sds((4, 1, BLK), F32),
                          _sds((1, BW), F32), _sds((1, BW), F32)])(z, z, z, z, lng, lnb, ws, bs, dy)


def _pool_core(x, gi):
    t_dim = x.shape[0]
    w = jnp.left_shift(2, gi)
    row = lax.broadcasted_iota(jnp.int32, x.shape, 0)
    s = x
    for k in (1, 2, 4, 8):
        sh = jnp.where(row >= k, pltpu.roll(s, k, 0), 0.0)
        s = jnp.where(k < w, s + sh, s)
    cnt = jnp.minimum(row + 1, w).astype(F32)
    return s / cnt - x, cnt, row, w, t_dim


def _pool_fwd(z, cw, cs, name):
    t_dim = z.shape[0]

    def body(z_ref, cw_ref, cs_ref, o_ref):
        pooled = _pool_core(z_ref[...], pl.program_id(0))[0]
        y = jnp.dot(pooled.astype(BF16), cw_ref[0].astype(BF16), preferred_element_type=F32) * cs_ref[0]
        o_ref[...] = y.astype(BF16)

    return _pc(body, name=name, grid=(4,),
               in_specs=[_bs((t_dim, BLK), lambda g: (0, C_B128 + g)), _bs((1, BLK, BLK), lambda g: (g, 0, 0)),
                         _bs((1, 1, BLK), lambda g: (g, 0, 0))],
               out_specs=_bs((t_dim, BLK), lambda g: (0, g)), out_shape=_sds((t_dim, BW), BF16),
               vmem=40 << 20)(z, cw, cs)


def _pool_bwd(z, dy, cw, cs, name):
    t_dim = z.shape[0]

    def body(z_ref, dy_ref, cw_ref, cs_ref, dx_ref, dcw_ref, dcs_ref):
        pooled, cnt, row, w, _ = _pool_core(z_ref[...], pl.program_id(0))
        p16 = pooled.astype(BF16)
        cw16 = cw_ref[0].astype(BF16)
        dyv = dy_ref[...].astype(F32)
        lin = jnp.dot(p16, cw16, preferred_element_type=F32)
        dcs_ref[0] = jnp.sum(dyv * lin, axis=0, keepdims=True)
        dys = (dyv * cs_ref[0]).astype(BF16)
        dcw_ref[0] = lax.dot_general(p16, dys, TN, preferred_element_type=F32)
        dpool = lax.dot_general(dys, cw16, NT, preferred_element_type=F32)
        sb = dpool / cnt
        for k in (1, 2, 4, 8):
            sh = jnp.where(row < t_dim - k, pltpu.roll(sb, t_dim - k, 0), 0.0)
            sb = jnp.where(k < w, sb + sh, sb)
        dx_ref[...] = (sb - dpool).astype(BF16)

    return _pc(body, name=name, grid=(4,),
               in_specs=[_bs((t_dim, BLK), lambda g: (0, C_B128 + g)), _bs((t_dim, BLK), lambda g: (0, g)),
                         _bs((1, BLK, BLK), lambda g: (g, 0, 0)), _bs((1, 1, BLK), lambda g: (g, 0, 0))],
               out_specs=[_bs((t_dim, BLK), lambda g: (0, g)), _bs((1, BLK, BLK), lambda g: (g, 0, 0)),
                          _bs((1, 1, BLK), lambda g: (g, 0, 0))],
               out_shape=[_sds((t_dim, BW), BF16), _sds((4, BLK, BLK), F32), _sds((4, 1, BLK), F32)],
               vmem=48 << 20)(z, dy, cw, cs)


def _gates_specs(tm):
    z_specs = [_bs((tm, GW), (lambda j, i, b=b: (i, G_B + 4 * b + j))) for b in range(3)]
    bg_specs = [_bs((1, GW), (lambda j, i, b=b: (0, 4 * b + j))) for b in range(3)]
    tile = _bs((tm, GW), lambda j, i: (i, j))
    return z_specs, bg_specs, tile


def _gates_fwd(z, bg, pa, pb, pc, name, tm=512):
    t_dim = z.shape[0]
    z_specs, bg_specs, tile = _gates_specs(tm)

    def body(za, zb, zc, ba, bb, bc, pa_ref, pb_ref, pc_ref, o_ref):
        o_ref[...] = (_sigmoid(za[...] + ba[...]) * pa_ref[...] + _sigmoid(zb[...] + bb[...]) * pb_ref[...]
                      + _sigmoid(zc[...] + bc[...]) * pc_ref[...]).astype(BF16)

    return _pc(body, name=name, grid=(D // GW, t_dim // tm), in_specs=z_specs + bg_specs + [tile] * 3,
               out_specs=tile, out_shape=_sds((t_dim, D), BF16))(z, z, z, bg, bg, bg, pa, pb, pc)


def _gates_bwd(z, bg, pa, pb, pc, dm, name, tm=512):
    t_dim = z.shape[0]
    z_specs, bg_specs, tile = _gates_specs(tm)
    bias_out = _bs((1, GW), lambda j, i: (0, j))

    def body(za, zb, zc, ba, bb, bc, pa_ref, pb_ref, pc_ref, dm_ref, dpa, dpb, dpc, dza, dzb, dzc, dba, dbb, dbc):
        dmv = dm_ref[...]
        first = pl.program_id(1) == 0
        for z_ref, b_ref, p_ref, dp_ref, dz_ref, db_ref in ((za, ba, pa_ref, dpa, dza, dba), (zb, bb, pb_ref, dpb, dzb, dbb),
                                                            (zc, bc, pc_ref, dpc, dzc, dbc)):
            gate = _sigmoid(z_ref[...] + b_ref[...])
            dp_ref[...] = (gate * dmv).astype(BF16)
            dz = dmv * p_ref[...] * (gate * (1.0 - gate))
            dz_ref[...] = dz.astype(BF16)

            @pl.when(first)
            def _():
                db_ref[...] = jnp.zeros_like(db_ref)

            db_ref[...] += jnp.sum(dz, axis=0, keepdims=True)

    return _pc(body, name=name, grid=(D // GW, t_dim // tm), in_specs=z_specs + bg_specs + [tile] * 4,
               out_specs=[tile] * 6 + [bias_out] * 3,
               out_shape=[_sds((t_dim, D), BF16)] * 6 + [_sds((1, D), F32)] * 3)(z, z, z, bg, bg, bg, pa, pb, pc, dm)


def _concat_bf16(parts, name, tm=512):
    t_dim = parts[0].shape[0]
    offs = [sum(p.shape[1] for p in parts[:i]) for i in range(len(parts) + 1)]

    def body(*refs):
        o_ref = refs[-1]
        for ref, off in zip(refs[:-1], offs):
            o_ref[:, off:off + ref.shape[1]] = ref[...].astype(BF16)

    return _pc(body, name=name, grid=(t_dim // tm,), in_specs=[_bs((tm, p.shape[1]), lambda i: (i, 0)) for p in parts],
               out_specs=_bs((tm, offs[-1]), lambda i: (i, 0)), out_shape=_sds((t_dim, offs[-1]), BF16),
               vmem=48 << 20)(*parts)


def _coords():
    return lax.axis_index("x"), lax.axis_index("y"), lax.axis_index("c")


def _peer(k, x, y, c):
    px = 1 - x if k & 4 else x
    py = 1 - y if k & 2 else y
    pc = 1 - c if k & 1 else c
    return (px, py, pc), 4 * px + 2 * py + pc


def _exchange_copies(kind, bufs, send, recv):
    x, y, c = _coords()
    me = 4 * x + 2 * y + c
    na = len(bufs) if kind == "gather" else len(bufs) // 2
    out = []
    for k in range(1, NDEV):
        to, peer = _peer(k, x, y, c)
        for a in range(na):
            sems = dict(send_sem=send.at[a * 7 + k - 1], recv_sem=recv.at[a * 7 + k - 1], device_id=to, device_id_type=MESH)
            if kind == "gather":
                mine = pltpu.make_async_remote_copy(src_ref=bufs[a].at[:, me], dst_ref=bufs[a].at[:, me], **sems)
                theirs = pltpu.make_async_remote_copy(src_ref=bufs[a].at[:, me], dst_ref=bufs[a].at[:, peer], **sems)
            else:
                src, land = bufs[a], bufs[na + a]
                mine = pltpu.make_async_remote_copy(src_ref=src.at[:, peer], dst_ref=land.at[me], **sems)
                theirs = pltpu.make_async_remote_copy(src_ref=src.at[:, peer], dst_ref=land.at[peer], **sems)
            out.append((mine, theirs))
    return out


_HBM = pl.BlockSpec(memory_space=pltpu.HBM)
_SEM = pl.BlockSpec(memory_space=pltpu.SEMAPHORE)
_EFFECT = pltpu.SideEffectType.DATAFLOW_SIDE_EFFECTING


def _exchange_start(kind, arrays, after, name):
    na = len(arrays)
    if kind == "scatter":
        arrays = list(arrays) + [lax.empty((NDEV, s.shape[0]) + s.shape[2:], s.dtype) for s in arrays]
    nb = len(arrays)

    def body(*refs):
        bufs = refs[:nb]
        send, recv = refs[nb + 1], refs[nb + 2]
        token = refs[-1]
        for mine, _ in _exchange_copies(kind, bufs, send, recv):
            mine.start()
        token[...] = jnp.zeros_like(token)

    arrays = [pltpu.with_memory_space_constraint(b, pltpu.HBM) for b in arrays]
    outs = pl.pallas_call(
        body, name=name,
        out_shape=(pltpu.SemaphoreType.DMA((7 * na,)), pltpu.SemaphoreType.DMA((7 * na,)),
                   *[pltpu.HBM(b.shape, b.dtype) for b in arrays], _sds((8, 128), F32)),
        in_specs=[_HBM] * nb + [pl.BlockSpec(memory_space=pl.ANY)],
        out_specs=(_SEM, _SEM, *([_HBM] * nb), pl.BlockSpec(memory_space=pltpu.VMEM)),
        input_output_aliases={i: 2 + i for i in range(nb)},
        compiler_params=pltpu.CompilerParams(has_side_effects=_EFFECT), interpret=False)(*arrays, after)
    return outs[0], outs[1], list(outs[2:2 + nb]), outs[-1]


def _exchange_wait(kind, started, after, name):
    send, recv, arrays, _ = started
    nb = len(arrays)

    def body(*refs):
        for mine, theirs in _exchange_copies(kind, refs[:nb], refs[nb], refs[nb + 1]):
            mine.wait_send()
            theirs.wait_recv()

    outs = pl.pallas_call(
        body, name=name,
        out_shape=tuple(pltpu.HBM(b.shape, b.dtype) for b in arrays),
        in_specs=[_HBM] * nb + [_SEM, _SEM, pl.BlockSpec(memory_space=pl.ANY)],
        out_specs=tuple([_HBM] * nb),
        input_output_aliases={i: i for i in range(nb)},
        compiler_params=pltpu.CompilerParams(has_side_effects=_EFFECT), interpret=False)(*arrays, send, recv, after)
    return list(outs)


def _adam_math(w, g, m, v):
    m2 = ADAM_B1 * m + (1.0 - ADAM_B1) * g
    v2 = ADAM_B2 * v + (1.0 - ADAM_B2) * jnp.square(g)
    m_hat = m2 / (1.0 - ADAM_B1 ** ADAM_STEP)
    v_hat = v2 / (1.0 - ADAM_B2 ** ADAM_STEP)
    return -ADAM_LR * (m_hat / (jnp.sqrt(v_hat) + ADAM_EPS) + ADAM_WD * w), m2, v2


_DIRECT = (1, 4, 2, 6)


def _gather_copies(bufs, send_d, recv_d, send_f=None, recv_f=None):
    x, y, c = _coords()
    me = 4 * x + 2 * y + c
    sibling, _ = _peer(1, x, y, c)
    direct, forward = [], []
    for a, buf in enumerate(bufs):
        for j, k in enumerate(_DIRECT):
            to, peer = _peer(k, x, y, c)
            sems = dict(send_sem=send_d.at[4 * a + j], recv_sem=recv_d.at[4 * a + j], device_id=to, device_id_type=MESH)
            direct.append((pltpu.make_async_remote_copy(src_ref=buf.at[:, me], dst_ref=buf.at[:, me], **sems),
                           pltpu.make_async_remote_copy(src_ref=buf.at[:, me], dst_ref=buf.at[:, peer], **sems)))
        if send_f is None:
            continue
        for j, k in enumerate(_DIRECT[1:]):
            _, near = _peer(k, x, y, c)
            _, far = _peer(k | 1, x, y, c)
            sems = dict(send_sem=send_f.at[3 * a + j], recv_sem=recv_f.at[3 * a + j], device_id=sibling, device_id_type=MESH)
            forward.append((pltpu.make_async_remote_copy(src_ref=buf.at[:, near], dst_ref=buf.at[:, near], **sems),
                            pltpu.make_async_remote_copy(src_ref=buf.at[:, near], dst_ref=buf.at[:, far], **sems)))
    return direct, forward


def _gather_start(lands, after, name):
    na = len(lands)

    def body(*refs):
        direct, _ = _gather_copies(refs[:na], refs[na + 1], refs[na + 2])
        for mine, _ in direct:
            mine.start()
        refs[-1][...] = jnp.zeros_like(refs[-1])

    lands = [pltpu.with_memory_space_constraint(b, pltpu.HBM) for b in lands]
    outs = pl.pallas_call(
        body, name=name,
        out_shape=(pltpu.SemaphoreType.DMA((4 * na,)), pltpu.SemaphoreType.DMA((4 * na,)),
                   *[pltpu.HBM(b.shape, b.dtype) for b in lands], _sds((8, 128), F32)),
        in_specs=[_HBM] * na + [pl.BlockSpec(memory_space=pl.ANY)],
        out_specs=(_SEM, _SEM, *([_HBM] * na), pl.BlockSpec(memory_space=pltpu.VMEM)),
        input_output_aliases={i: 2 + i for i in range(na)},
        compiler_params=pltpu.CompilerParams(has_side_effects=_EFFECT), interpret=False)(*lands, after)
    return outs[0], outs[1], list(outs[2:2 + na]), outs[-1]


def _gather_forward(started, after, name):
    send_d, recv_d, lands, _ = started
    na = len(lands)

    def body(*refs):
        direct, forward = _gather_copies(refs[:na], refs[na], refs[na + 1], refs[2 * na + 3], refs[2 * na + 4])
        for a in range(na):
            for j in range(3):
                direct[4 * a + 1 + j][1].wait_recv()
                forward[3 * a + j][0].start()
        refs[-1][...] = jnp.zeros_like(refs[-1])

    outs = pl.pallas_call(
        body, name=name,
        out_shape=(*[pltpu.HBM(b.shape, b.dtype) for b in lands], pltpu.SemaphoreType.DMA((3 * na,)),
                   pltpu.SemaphoreType.DMA((3 * na,)), _sds((8, 128), F32)),
        in_specs=[_HBM] * na + [_SEM, _SEM, pl.BlockSpec(memory_space=pl.ANY)],
        out_specs=(*([_HBM] * na), _SEM, _SEM, pl.BlockSpec(memory_space=pltpu.VMEM)),
        input_output_aliases={i: i for i in range(na)},
        compiler_params=pltpu.CompilerParams(has_side_effects=_EFFECT), interpret=False)(*lands, send_d, recv_d, after)
    return send_d, recv_d, outs[na], outs[na + 1], list(outs[:na]), outs[-1]


def _gather_wait(forwarded, after, name):
    send_d, recv_d, send_f, recv_f, lands, _ = forwarded
    na = len(lands)

    def body(*refs):
        direct, forward = _gather_copies(refs[:na], refs[na], refs[na + 1], refs[na + 2], refs[na + 3])
        for a in range(na):
            for j in range(4):
                direct[4 * a + j][0].wait_send()
            direct[4 * a][1].wait_recv()
            for j in range(3):
                forward[3 * a + j][0].wait_send()
                forward[3 * a + j][1].wait_recv()

    outs = pl.pallas_call(
        body, name=name, out_shape=tuple(pltpu.HBM(b.shape, b.dtype) for b in lands),
        in_specs=[_HBM] * na + [_SEM] * 4 + [pl.BlockSpec(memory_space=pl.ANY)], out_specs=tuple([_HBM] * na),
        input_output_aliases={i: i for i in range(na)},
        compiler_params=pltpu.CompilerParams(has_side_effects=_EFFECT), interpret=False)(
            *lands, send_d, recv_d, send_f, recv_f, after)
    return list(outs)


def _reduce_update(landed, own, me, piece, w, m, v, layer, prev, transposed, name):
    rows, cols = landed.shape[2], landed.shape[3]
    tc = 256
    assert cols % tc == 0 and w.shape[1:] == ((cols, rows) if transposed else (rows, cols))
    lane_chunks = [(c0, min(128, rows - c0)) for c0 in range(0, rows, 128)]

    def body(me_ref, b_ref, own_ref, w_ref, m_ref, v_ref, *rest):
        outs = rest[-4:]
        mine = me_ref[0]
        s = None
        for d in range(NDEV):
            other = jnp.where(mine == d, (d + 1) % NDEV, d)
            part = jnp.where(mine == d, own_ref[0, 0], b_ref[other, 0]).astype(F32)
            s = part if s is None else s + part
        if not transposed:
            for o, val in zip(outs, (s,) + _adam_math(w_ref[0], s, m_ref[0], v_ref[0])):
                o[0] = val
            return
        for c0, wd in lane_chunks:
            chunk = s[c0:c0 + wd]
            if wd < 128:
                chunk = jnp.concatenate([chunk, jnp.zeros((128 - wd, tc), F32)], axis=0)
            g = chunk.T[:, :wd]
            sl = (0, slice(None), slice(c0, c0 + wd))
            for o, val in zip(outs, (g,) + _adam_math(w_ref[sl], g, m_ref[sl], v_ref[sl])):
                o[sl] = val

    if transposed:
        nat = _bs((1, tc, rows), lambda j, me_ref: (layer, j, 0))
    else:
        nat = _bs((1, rows, tc), lambda j, me_ref: (layer, 0, j))
    in_specs = [_bs((NDEV, 1, rows, tc), lambda j, me_ref: (0, piece, 0, j)),
                _bs((1, 1, rows, tc), lambda j, me_ref: (piece, me_ref[0], 0, j)), nat, nat, nat]
    args = [me, landed, own, w, m, v]
    alias = {}
    if prev is not None:
        in_specs += [pl.BlockSpec(memory_space=pl.ANY)] * 4
        args += list(prev)
        alias = {6 + i: i for i in range(4)}
    grid_spec = pltpu.PrefetchScalarGridSpec(num_scalar_prefetch=1, grid=(cols // tc,), in_specs=in_specs, out_specs=[nat] * 4)
    return pl.pallas_call(body, name=name, grid_spec=grid_spec, out_shape=[_sds(w.shape, F32)] * 4,
                          input_output_aliases=alias, interpret=False)(*args)


SMALL_LAYER = ("ffn1_norm", "mix_norm", "b_gate", "b_ln_g", "b_ln_b", "b_w_s", "b_b_s", "c_w", "c_scale", "ffn2_norm")
SMALL_ROWS = {"ffn1_norm": 8, "mix_norm": 8, "b_gate": 24, "b_ln_g": 4, "b_ln_b": 4, "b_w_s": 512, "b_b_s": 4, "c_w": 512,
              "c_scale": 4, "ffn2_norm": 8, "final_norm": 8}
LAYER_ROWS = sum(SMALL_ROWS[k] for k in SMALL_LAYER)


def _small_offset(name):
    if name == "final_norm":
        return LAYER_ROWS
    return sum(SMALL_ROWS[k] for k in SMALL_LAYER[:SMALL_LAYER.index(name)])


def _pack_small_grads(layer_grads, final, me, name):
    flat, where = [], []
    for k in SMALL_LAYER:
        parts = layer_grads[k] if k == "b_gate" else [layer_grads[k]]
        for i, part in enumerate(parts):
            flat.append(part)
            where.append(_small_offset(k) + i * (D // 128))
    if final is not None:
        flat.append(final)
        where.append(_small_offset("final_norm"))
    pack_rows = LAYER_ROWS + (SMALL_ROWS["final_norm"] if final is not None else 0)

    def body(me_ref, *refs):
        del me_ref
        o_ref = refs[-1]
        for ref, off in zip(refs[:-1], where):
            if len(ref.shape) == 2:
                for j in range(ref.shape[1] // 128):
                    o_ref[0, 0, off + j:off + j + 1, :] = ref[:, 128 * j:128 * (j + 1)]
            else:
                rows = ref.shape[1]
                for g in range(4):
                    o_ref[0, 0, off + g * rows:off + (g + 1) * rows, :] = ref[g]

    grid_spec = pltpu.PrefetchScalarGridSpec(
        num_scalar_prefetch=1, grid=(1,),
        in_specs=[pl.BlockSpec(a.shape, (lambda i, me_ref, nd=a.ndim: (0,) * nd)) for a in flat],
        out_specs=_bs((1, 1, pack_rows, 128), lambda i, me_ref: (0, me_ref[0], 0, 0)))
    return pl.pallas_call(body, name=name, grid_spec=grid_spec, out_shape=_sds((1, NDEV, pack_rows, 128), F32),
                          interpret=False)(me, *flat)


def _adamw_small(gathered, w, mom, var, name):
    names = SMALL_LAYER + ("final_norm",)
    ins = list(gathered) + [t[k] for k in names for t in (w, mom, var)]

    def body(*refs):
        outs = refs[len(ins):]

        def rows(l, off, n):
            land = refs[l]
            s = land[0, 0, off:off + n, :]
            for d in range(1, NDEV):
                s = s + land[0, d, off:off + n, :]
            return s

        def as_row(l, off, n):
            s = rows(l, off, n)
            return jnp.concatenate([s[j:j + 1, :] for j in range(n)], axis=1)

        def update(i, idx, g):
            wr, mr, vr = (refs[DEPTH + 3 * i + t] for t in range(3))
            d, m2, v2 = _adam_math(wr[idx], g, mr[idx], vr[idx])
            for o, val in zip(outs[4 * i:4 * i + 4], (g, d, m2, v2)):
                o[idx] = val

        for i, k in enumerate(names):
            n = SMALL_ROWS[k]
            off = _small_offset(k)
            if k == "final_norm":
                update(i, (slice(0, 1),), as_row(DEPTH - 1, off, n))
                continue
            for l in range(DEPTH):
                if k in ("b_w_s", "c_w"):
                    for g in range(4):
                        update(i, (l, g), rows(l, off + g * BLK, BLK))
                elif k in ("b_b_s", "c_scale"):
                    update(i, (l,), rows(l, off, n))
                else:
                    update(i, (slice(l, l + 1),), as_row(l, off, n))

    out_shape = [_sds(w[k].shape, F32) for k in names for _ in range(4)]
    outs = pl.pallas_call(body, name=name, out_shape=out_shape,
                          compiler_params=pltpu.CompilerParams(vmem_limit_bytes=48 << 20), interpret=False)(*ins)
    return tuple({k: outs[4 * i + t] for i, k in enumerate(names)} for t in range(4))


PG, PD = 0, 2


def _ffn_fwd(x, gain, wf, tag, mid):
    h = _rms_fwd(x, gain, f"rms_{tag}")
    gu = _mm(h, wf, mode="nt", tm=2048, tn=512, tk=D, n=2 * FF, b_row0=PG * FF, out_dtype=BF16, name=f"gu_{tag}")
    token = mid(gu)
    act = _swiglu_fwd(gu, f"swiglu_{tag}")
    out = _mm(act, wf, mode="nn", tm=512, tn=D, tk=FF, n=D, b_row0=PD * FF, out_dtype=F32, scale=0.5, res=x, after=token,
              name=f"down_{tag}")
    return out, (x, h, gu, act)


def _ffn_bwd(dout, saved, gain, wf, tag, after, emit):
    x, h, gu, act = saved
    grf = lax.empty((3, NDEV, FFS, D), BF16)
    dgu = _dact_dswiglu(dout, wf, gu, tm=256, b_row0=PD * FF, after=after, name=f"dgu_{tag}")
    grf = _wgrad(act, dout, grf, piece0=PD, rows=FFS, tm=FF // 2, tn=D, tk=1024, scale=0.5, name=f"wg_down_{tag}")
    grf = _wgrad(dgu, h, grf, piece0=PG, rows=FFS, tm=FF // 2, tn=D, tk=2048, name=f"wg_gu_{tag}")
    token = emit([grf])
    dx, dgain = _mm_drms(dgu, wf, x, gain, dout, tm=256, b_row0=PG * FF, after=token, name=f"dx_{tag}")
    return dx, dgain, token


def _mixer_fwd(x, sp, ww, wo, wa, wbc, tag, mid):
    h = _rms_fwd(x, sp["mix_norm"], f"rms_mix_{tag}")
    z = _mm(h, ww, mode="nt", tm=1024, tn=1152, tk=D, n=INW, out_dtype=F32, name=f"z_{tag}")
    token = mid(z)
    acc, ml = _attn_fwd(z, None, None, gi=0, first=True, final=False, name=f"attn0_{tag}")
    acc, ml = _attn_fwd(z, acc, ml, gi=1, first=False, final=False, name=f"attn1_{tag}")
    ya, lse = _attn_fwd(z, acc, ml, gi=2, first=False, final=True, name=f"attn2_{tag}")
    yb = _gmlp_fwd(z, sp["b_ln_g"], sp["b_ln_b"], sp["b_w_s"], sp["b_b_s"], f"gmlp_{tag}")
    yc = _pool_fwd(z, sp["c_w"], sp["c_scale"], f"pool_{tag}")
    pa = _mm(ya, wa, mode="nt", tm=1024, tn=D, tk=GW, n=D, out_dtype=F32, after=token, name=f"proj_a_{tag}")
    pb = _mm(yb, wbc, mode="nt", tm=1024, tn=D, tk=BW, n=D, b_row0=0, out_dtype=F32, name=f"proj_b_{tag}")
    pc = _mm(yc, wbc, mode="nt", tm=1024, tn=D, tk=BW, n=D, b_row0=D, out_dtype=F32, name=f"proj_c_{tag}")
    merged = _gates_fwd(z, sp["b_gate"], pa, pb, pc, f"gates_{tag}")
    out = _mm(merged, wo, mode="nn", tm=1024, tn=D, tk=D, n=D, out_dtype=F32, res=x, name=f"out_{tag}")
    return out, (x, h, z, ya, lse, yb, yc, pa, pb, pc, merged)


def _mixer_bwd(dout, saved, sp, ww, wo, wa, wbc, tag, after, emit):
    x, h, z, ya, lse, yb, yc, pa, pb, pc, merged = saved
    grw = lax.empty((1, NDEV, INS, D), BF16)
    gro = lax.empty((1, NDEV, DS, D), BF16)
    gra = lax.empty((1, NDEV, DS, GW), BF16)
    grbc = lax.empty((2, NDEV, DS, BW), BF16)
    dm = _mm(dout, wo, mode="nt", tm=1024, tn=D, tk=D, n=D, out_dtype=F32, after=after, name=f"dmerged_{tag}")
    gro = _wgrad(merged, dout, gro, piece0=0, rows=DS, tm=D, tn=D, tk=1024, name=f"wg_out_{tag}")
    dpa, dpb, dpc, dza, dzb, dzc, dba, dbb, dbc = _gates_bwd(z, sp["b_gate"], pa, pb, pc, dm, f"dgates_{tag}")
    dya = _mm(dpa, wa, mode="nn", tm=2048, tn=GW, tk=D, n=GW, out_dtype=F32, name=f"dya_{tag}")
    dyb = _mm(dpb, wbc, mode="nn", tm=2048, tn=BW, tk=D, n=BW, b_row0=0, out_dtype=F32, name=f"dyb_{tag}")
    dyc = _mm(dpc, wbc, mode="nn", tm=2048, tn=BW, tk=D, n=BW, b_row0=D, out_dtype=F32, name=f"dyc_{tag}")
    gra = _wgrad(dpa, ya, gra, piece0=0, rows=DS, tm=D, tn=GW, tk=1024, name=f"wg_pa_{tag}")
    grbc = _wgrad(dpb, yb, grbc, piece0=0, rows=DS, tm=D, tn=BW, tk=1024, name=f"wg_pb_{tag}")
    grbc = _wgrad(dpc, yc, grbc, piece0=1, rows=DS, tm=D, tn=BW, tk=1024, name=f"wg_pc_{tag}")
    dqkv = [_attn_bwd(z, dya, ya, lse, gi=gi, name=f"dattn{gi}_{tag}") for gi in range(3)]
    duv, dws, dbs, dlng, dlnb = _gmlp_bwd(z, dyb, sp["b_ln_g"], sp["b_ln_b"], sp["b_w_s"], sp["b_b_s"], f"dgmlp_{tag}")
    dxc, dcw, dcs = _pool_bwd(z, dyc, sp["c_w"], sp["c_scale"], f"dpool_{tag}")
    dz = _concat_bf16([dqkv[gi][part] for part in range(3) for gi in range(3)] + [duv, dxc, dza, dzb, dzc], f"dz_{tag}")
    grw = _wgrad(dz, h, grw, piece0=0, rows=INS, tm=INW // 2, tn=512, tk=1024, name=f"wg_in_{tag}")
    token = emit([grw, gro, gra, grbc])
    dx, dgain = _mm_drms(dz, ww, x, sp["mix_norm"], dout, tm=256, b_row0=0, after=token, name=f"dx_mix_{tag}")
    small = {"mix_norm": dgain, "b_gate": [dba, dbb, dbc], "b_ln_g": dlng, "b_ln_b": dlnb,
             "b_w_s": dws, "b_b_s": dbs, "c_w": dcw, "c_scale": dcs}
    return dx, small, token


PARTS = ("f1", "mix", "f2")


def _forward(xs, tgt, weights_of, pass_on, small, final_gain):
    saved, views = [], []
    cur = xs
    for l in range(DEPTH):
        (wf1,) = weights_of(l, "f1", cur)
        cur, s1 = _ffn_fwd(cur, small[l]["ffn1_norm"], wf1, f"f1l{l}", lambda t: pass_on(l, "mix", t))
        wmix = weights_of(l, "mix", cur)
        cur, s2 = _mixer_fwd(cur, small[l], *wmix, f"l{l}", lambda t: pass_on(l, "f2", t))
        (wf2,) = weights_of(l, "f2", cur)
        cur, s3 = _ffn_fwd(cur, small[l]["ffn2_norm"], wf2, f"f2l{l}",
                           lambda t: pass_on(l + 1, "f1", t) if l + 1 < DEPTH else None)
        saved.append((s1, s2, s3))
        views.append((wf1, wmix, wf2))
    loss_part, dcur, dfinal = _final_loss(cur, final_gain, tgt, "final_loss")
    return loss_part, dcur, dfinal, saved, views


def _backward_layer(dcur, saved_l, small_l, views_l, l, emit, after):
    wf1, wmix, wf2 = views_l
    s1, s2, s3 = saved_l
    dcur, dn2, token = _ffn_bwd(dcur, s3, small_l["ffn2_norm"], wf2, f"f2l{l}", after, lambda g: emit("f2", g))
    dcur, sg, token = _mixer_bwd(dcur, s2, small_l, *wmix, f"l{l}", token, lambda g: emit("mix", g))
    dcur, dn1, token = _ffn_bwd(dcur, s1, small_l["ffn1_norm"], wf1, f"f1l{l}", token, lambda g: emit("f1", g))
    sg["ffn1_norm"] = dn1
    sg["ffn2_norm"] = dn2
    return dcur, sg, token


BIG = ("ffn1_w_gate", "ffn1_w_up", "ffn1_w_down", "w_in", "w_proj_a", "w_proj_b", "w_proj_c", "w_out",
       "ffn2_w_gate", "ffn2_w_up", "ffn2_w_down")
ALL_WEIGHTS = ("ffn1_norm", "ffn1_w_gate", "ffn1_w_up", "ffn1_w_down", "mix_norm", "w_in", "b_gate", "b_ln_g", "b_ln_b",
               "b_w_s", "b_b_s", "c_w", "c_scale", "w_proj_a", "w_proj_b", "w_proj_c", "w_out", "ffn2_norm", "ffn2_w_gate",
               "ffn2_w_up", "ffn2_w_down", "final_norm")


def _place_shards(w, l, part, me, after, name):
    tr = lambda a: a.transpose(0, 2, 1)
    if part == "mix":
        srcs = [tr(w["w_in"]), w["w_out"], tr(w["w_proj_a"]), tr(w["w_proj_b"]), tr(w["w_proj_c"])]
        groups = [[0], [1], [2], [3, 4]]
    else:
        f = "ffn1" if part == "f1" else "ffn2"
        srcs = [tr(w[f + "_w_gate"]), tr(w[f + "_w_up"]), w[f + "_w_down"]]
        groups = [[0, 1, 2]]

    def body(me_ref, *refs):
        del me_ref
        outs = refs[len(srcs) + 1:]
        for o_ref, members in zip(outs, groups):
            for p, i in enumerate(members):
                o_ref[p, 0] = refs[i][0].astype(BF16)

    out_shape = [_sds((len(g), NDEV) + srcs[g[0]].shape[1:], BF16) for g in groups]
    grid_spec = pltpu.PrefetchScalarGridSpec(
        num_scalar_prefetch=1, grid=(1,),
        in_specs=[_bs((1,) + s.shape[1:], lambda i, me_ref: (l, 0, 0)) for s in srcs] + [pl.BlockSpec(memory_space=pl.ANY)],
        out_specs=[_bs((len(g), 1) + srcs[g[0]].shape[1:], lambda i, me_ref: (0, me_ref[0], 0, 0)) for g in groups])
    return pl.pallas_call(body, name=name, grid_spec=grid_spec, out_shape=out_shape,
                          compiler_params=pltpu.CompilerParams(vmem_limit_bytes=40 << 20), interpret=False)(me, *srcs, after)


def _part_views(part, gathered):
    if part == "mix":
        pw, po, pa, pbc = gathered
        return [pw.reshape(INW, D), po.reshape(D, D), pa.reshape(D, GW), pbc.reshape(2 * D, BW)]
    return [gathered[0].reshape(3 * FF, D)]


def _part_pieces(part):
    if part == "mix":
        return [(0, 0, "w_in", True), (1, 0, "w_out", False), (2, 0, "w_proj_a", True), (3, 0, "w_proj_b", True),
                (3, 1, "w_proj_c", True)]
    f = "ffn1" if part == "f1" else "ffn2"
    return [(0, 0, f + "_w_gate", True), (0, 1, f + "_w_up", True), (0, 2, f + "_w_down", False)]


def _layer_small(w, l):
    r = lambda a, shape: a.reshape(shape)
    return {"ffn1_norm": r(w["ffn1_norm"][l], (1, D)), "mix_norm": r(w["mix_norm"][l], (1, D)),
            "b_gate": r(w["b_gate"][l], (1, 3 * D)), "b_ln_g": r(w["b_ln_g"][l], (1, BW)), "b_ln_b": r(w["b_ln_b"][l], (1, BW)),
            "b_w_s": w["b_w_s"][l], "b_b_s": r(w["b_b_s"][l], (4, BLK, 1)), "c_w": w["c_w"][l],
            "c_scale": r(w["c_scale"][l], (4, 1, BLK)), "ffn2_norm": r(w["ffn2_norm"][l], (1, D))}


def kernel(x, ffn1_norm, ffn1_w_gate, ffn1_w_up, ffn1_w_down, mix_norm, w_in, b_gate, b_ln_g, b_ln_b, b_w_s, b_b_s, c_w, c_scale, w_proj_a, w_proj_b, w_proj_c, w_out, ffn2_norm, ffn2_w_gate, ffn2_w_up, ffn2_w_down, final_norm, loss_target, m_ffn1_norm, m_ffn1_w_gate, m_ffn1_w_up, m_ffn1_w_down, m_mix_norm, m_w_in, m_b_gate, m_b_ln_g, m_b_ln_b, m_b_w_s, m_b_b_s, m_c_w, m_c_scale, m_w_proj_a, m_w_proj_b, m_w_proj_c, m_w_out, m_ffn2_norm, m_ffn2_w_gate, m_ffn2_w_up, m_ffn2_w_down, m_final_norm, v_ffn1_norm, v_ffn1_w_gate, v_ffn1_w_up, v_ffn1_w_down, v_mix_norm, v_w_in, v_b_gate, v_b_ln_g, v_b_ln_b, v_b_w_s, v_b_b_s, v_c_w, v_c_scale, v_w_proj_a, v_w_proj_b, v_w_proj_c, v_w_out, v_ffn2_norm, v_ffn2_w_gate, v_ffn2_w_up, v_ffn2_w_down, v_final_norm):
    w = dict(ffn1_norm=ffn1_norm, ffn1_w_gate=ffn1_w_gate, ffn1_w_up=ffn1_w_up, ffn1_w_down=ffn1_w_down, mix_norm=mix_norm,
             w_in=w_in, b_gate=b_gate, b_ln_g=b_ln_g, b_ln_b=b_ln_b, b_w_s=b_w_s, b_b_s=b_b_s, c_w=c_w, c_scale=c_scale,
             w_proj_a=w_proj_a, w_proj_b=w_proj_b, w_proj_c=w_proj_c, w_out=w_out, ffn2_norm=ffn2_norm, ffn2_w_gate=ffn2_w_gate,
             ffn2_w_up=ffn2_w_up, ffn2_w_down=ffn2_w_down, final_norm=final_norm)
    mom = dict(ffn1_norm=m_ffn1_norm, ffn1_w_gate=m_ffn1_w_gate, ffn1_w_up=m_ffn1_w_up, ffn1_w_down=m_ffn1_w_down,
               mix_norm=m_mix_norm, w_in=m_w_in, b_gate=m_b_gate, b_ln_g=m_b_ln_g, b_ln_b=m_b_ln_b, b_w_s=m_b_w_s, b_b_s=m_b_b_s,
               c_w=m_c_w, c_scale=m_c_scale, w_proj_a=m_w_proj_a, w_proj_b=m_w_proj_b, w_proj_c=m_w_proj_c, w_out=m_w_out,
               ffn2_norm=m_ffn2_norm, ffn2_w_gate=m_ffn2_w_gate, ffn2_w_up=m_ffn2_w_up, ffn2_w_down=m_ffn2_w_down,
               final_norm=m_final_norm)
    var = dict(ffn1_norm=v_ffn1_norm, ffn1_w_gate=v_ffn1_w_gate, ffn1_w_up=v_ffn1_w_up, ffn1_w_down=v_ffn1_w_down,
               mix_norm=v_mix_norm, w_in=v_w_in, b_gate=v_b_gate, b_ln_g=v_b_ln_g, b_ln_b=v_b_ln_b, b_w_s=v_b_w_s, b_b_s=v_b_b_s,
               c_w=v_c_w, c_scale=v_c_scale, w_proj_a=v_w_proj_a, w_proj_b=v_w_proj_b, w_proj_c=v_w_proj_c, w_out=v_w_out,
               ffn2_norm=v_ffn2_norm, ffn2_w_gate=v_ffn2_w_gate, ffn2_w_up=v_ffn2_w_up, ffn2_w_down=v_ffn2_w_down,
               final_norm=v_final_norm)

    t_dim = x.shape[1]
    xs = x.reshape(t_dim, D)
    tgt = loss_target.reshape(t_dim, D)

    me = 4 * lax.axis_index("x") + 2 * lax.axis_index("y") + lax.axis_index("c")
    me_arr = me.astype(jnp.int32).reshape(1)
    gathers = {}
    token = jnp.zeros((8, 128), F32)
    for l in range(DEPTH):
        for part in PARTS:
            lands = _place_shards(w, l, part, me_arr, token, f"place_{part}_l{l}")
            gathers[l, part] = _gather_start(lands, token, f"gather_start_{part}_l{l}")
            token = gathers[l, part][-1]

    def pass_on(l, part, after):
        gathers[l, part] = _gather_forward(gathers[l, part], after, f"gather_forward_{part}_l{l}")
        return gathers[l, part][-1]

    pass_on(0, "f1", token)

    def weights_of(l, part, after):
        return _part_views(part, _gather_wait(gathers[l, part], after, f"gather_wait_{part}_l{l}"))

    small = [_layer_small(w, l) for l in range(DEPTH)]
    loss_part, dcur, dfinal, saved, views = _forward(xs, tgt, weights_of, pass_on, small, w["final_norm"].reshape(1, D))
    loss = lax.psum(loss_part[0, 0], ("x", "y", "c"))

    scatters = []
    small_gathers = [None] * DEPTH
    token = None
    for l in reversed(range(DEPTH)):
        def emit(part, grads, l=l):
            scatters.append((l, part, _exchange_start("scatter", grads, dfinal, f"scatter_start_{part}_l{l}")))
            return scatters[-1][2][-1]
        dcur, small_grads, token = _backward_layer(dcur, saved[l], small[l], views[l], l, emit, token)
        packed = _pack_small_grads(small_grads, dfinal if l == DEPTH - 1 else None, me_arr, f"pack_small_grads_l{l}")
        small_gathers[l] = _exchange_start("gather", [packed], token, f"gather_start_small_l{l}")
        token = small_gathers[l][-1]
    grad_x = dcur.reshape(x.shape)

    done = {}
    behind = token
    for l, part, started in scatters:
        arrs = _exchange_wait("scatter", started, behind, f"scatter_wait_{part}_l{l}")
        na = len(arrs) // 2
        for a, piece, k, transposed in _part_pieces(part):
            flip = transposed and w[k].shape[2] % 128 != 0
            view = (lambda t: t.transpose(0, 2, 1)) if flip else (lambda t: t)
            done[k] = _reduce_update(arrs[na + a], arrs[a], me_arr, piece, view(w[k]), view(mom[k]), view(var[k]), l,
                                     done.get(k), transposed and not flip, f"update_{k}_l{l}")
            behind = done[k][1]
    unflip = lambda k, t: t.transpose(0, 2, 1) if t.shape != w[k].shape else t
    grads, delta, new_m, new_v = ({k: unflip(k, done[k][t]) for k in BIG} for t in range(4))
    gathered = [None] * DEPTH
    for l in reversed(range(DEPTH)):
        gathered[l] = _exchange_wait("gather", small_gathers[l], behind, f"gather_wait_small_l{l}")[0]
        behind = gathered[l]
    as_row = lambda t: dict(t, final_norm=t["final_norm"].reshape(1, D))
    for dst, src in zip((grads, delta, new_m, new_v), _adamw_small(gathered, as_row(w), as_row(mom), as_row(var), "adamw_small")):
        dst.update(src)
        dst["final_norm"] = src["final_norm"].reshape(D)

    return (loss, grad_x, *[grads[k] for k in ALL_WEIGHTS], *[delta[k] for k in ALL_WEIGHTS],
            *[new_m[k] for k in ALL_WEIGHTS], *[new_v[k] for k in ALL_WEIGHTS])
```

```python
import math

import jax
import jax.numpy as jnp
from jax import lax
from jax.experimental import pallas as pl
from jax.experimental.pallas import tpu as pltpu

F32 = jnp.float32
BF16 = jnp.bfloat16
MESH = pl.DeviceIdType.MESH

NDEV = 8
DEPTH = 2
D = 1024
FF = 2816
FFS = FF // NDEV
INW = 6912
INS = INW // NDEV
DS = D // NDEV
BLK = 128
NH = 4
HD = 64
GW = NH * HD
DILS = (1, 4, 16)
QK_SCALE = 1.0 / math.sqrt(HD)
ZB = INW // GW
Q_B, K_B, V_B = 0, 3, 6
UV_B = 9
C_B128 = 26
G_B = 15
BW = 512
EPS = 1e-6
NEG = -1e30
VMEM_CAP = 60 * 1024 * 1024

ADAM_LR, ADAM_B1, ADAM_B2, ADAM_EPS, ADAM_WD, ADAM_STEP = 0.001, 0.9, 0.999, 1e-08, 0.01, 10

NT = (((1,), (1,)), ((), ()))
TN = (((0,), (0,)), ((), ()))


def _bs(shape, imap):
    return pl.BlockSpec(shape, imap)


def _pc(body, *, name, grid, in_specs, out_specs, out_shape, scratch=(), alias=None, vmem=None):
    params = {}
    if vmem is not None:
        params["vmem_limit_bytes"] = min(int(vmem), VMEM_CAP)
    big = lambda s: math.prod(s.shape) * jnp.dtype(s.dtype).itemsize >= (1 << 20)
    pin = lambda s: pltpu.HBM(s.shape, s.dtype) if big(s) else s
    out_shape = [pin(s) for s in out_shape] if isinstance(out_shape, (list, tuple)) else pin(out_shape)
    call = pl.pallas_call(
        body, name=name, grid=grid, in_specs=in_specs, out_specs=out_specs, out_shape=out_shape,
        scratch_shapes=list(scratch), input_output_aliases=alias or {},
        compiler_params=pltpu.CompilerParams(**params), interpret=False)
    return lambda *args: call(*[pltpu.with_memory_space_constraint(a, pltpu.HBM) if big(a) else a for a in args])


def _sds(shape, dtype):
    return jax.ShapeDtypeStruct(shape, dtype)


def _row_tile(rows, align, cap):
    best = None
    for cand in range(align, min(rows, cap) + 1, align):
        if rows % cand == 0:
            best = cand
    assert best is not None, (rows, align, cap)
    return best


def _mm(a, b, *, mode, tm, tn, tk, n, out_dtype, name, b_row0=0, scale=None, res=None, after=None):
    m, k_dim = a.shape
    nk = k_dim // tk
    assert m % tm == 0 and n % tn == 0 and k_dim % tk == 0
    b_off, rem = divmod(b_row0, tk if mode == "nn" else tn)
    assert rem == 0
    grid = (m // tm, n // tn, nk)
    in_specs = [_bs((tm, tk), lambda i, j, k: (i, k))]
    if mode == "nn":
        in_specs.append(_bs((tk, tn), lambda i, j, k: (b_off + k, j)))
    else:
        in_specs.append(_bs((tn, tk), lambda i, j, k: (b_off + j, k)))
    args = [a, b]
    if res is not None:
        in_specs.append(_bs((tm, tn), lambda i, j, k: (i, j)))
        args.append(res)
    if after is not None:
        in_specs.append(pl.BlockSpec(memory_space=pl.ANY))
        args.append(after)
    n_in = len(args)

    def body(*refs):
        a_ref, b_ref = refs[0], refs[1]
        r_ref = refs[2] if res is not None else None
        o_ref = refs[n_in]
        av = a_ref[...].astype(BF16)
        bv = b_ref[...].astype(BF16)
        if mode == "nn":
            p = jnp.dot(av, bv, preferred_element_type=F32)
        else:
            p = lax.dot_general(av, bv, NT, preferred_element_type=F32)

        def fin(v):
            if scale is not None:
                v = v * scale
            if r_ref is not None:
                v = r_ref[...] + v
            o_ref[...] = v.astype(out_dtype)

        if nk == 1:
            fin(p)
        else:
            acc_ref = refs[-1]
            kk = pl.program_id(2)

            @pl.when(kk == 0)
            def _():
                acc_ref[...] = p

            @pl.when(kk > 0)
            def _():
                acc_ref[...] += p

            @pl.when(kk == nk - 1)
            def _():
                fin(acc_ref[...])

    scratch = [pltpu.VMEM((tm, tn), F32)] if nk > 1 else []
    est = 2 * (tm * tk * a.dtype.itemsize + tk * tn * b.dtype.itemsize + tm * tn * jnp.dtype(out_dtype).itemsize)
    est += (2 + (nk > 1)) * tm * tn * 4 + (2 * tm * tn * 4 if res is not None else 0)
    est += (tm * tk * 2 if a.dtype != BF16 else 0) + (tk * tn * 2 if b.dtype != BF16 else 0)
    return _pc(body, name=name, grid=grid, in_specs=in_specs, out_specs=_bs((tm, tn), lambda i, j, k: (i, j)),
               out_shape=_sds((m, n), out_dtype), scratch=scratch, vmem=est + (8 << 20))(*args)


def _wgrad(a, b, buf, *, piece0, rows, tm, tn, tk, name, scale=None):
    t_dim, m = a.shape
    n = b.shape[1]
    db = tm // rows
    tpp = NDEV // db
    nk = t_dim // tk
    assert tm % rows == 0 and NDEV % db == 0 and m % tm == 0 and n % tn == 0 and t_dim % tk == 0
    grid = (m // tm, n // tn, nk)

    def body(a_ref, b_ref, buf_ref, o_ref, acc_ref):
        del buf_ref
        kk = pl.program_id(2)
        p = lax.dot_general(a_ref[...].astype(BF16), b_ref[...].astype(BF16), TN, preferred_element_type=F32)

        @pl.when(kk == 0)
        def _():
            acc_ref[...] = p

        @pl.when(kk > 0)
        def _():
            acc_ref[...] += p

        @pl.when(kk == nk - 1)
        def _():
            for d in range(db):
                v = acc_ref[d * rows:(d + 1) * rows, :]
                if scale is not None:
                    v = v * scale
                o_ref[0, d] = v.astype(BF16)

    est = 2 * (tk * tm * a.dtype.itemsize + tk * tn * b.dtype.itemsize + tm * tn * 2) + 3 * tm * tn * 4 + tk * tm * 4
    return _pc(
        body, name=name, grid=grid,
        in_specs=[_bs((tk, tm), lambda i, j, k: (k, i)), _bs((tk, tn), lambda i, j, k: (k, j)),
                  pl.BlockSpec(memory_space=pl.ANY)],
        out_specs=_bs((1, db, rows, tn), lambda i, j, k: (piece0 + i // tpp, i % tpp, 0, j)),
        out_shape=_sds(buf.shape, buf.dtype), scratch=[pltpu.VMEM((tm, tn), F32)], alias={2: 0},
        vmem=est + (8 << 20))(a, b, buf)


def _mm_drms(a, b, x, gain, dres, *, tm, b_row0, after, name):
    m, k_dim = a.shape
    assert m % tm == 0 and b_row0 % k_dim == 0 and b.shape[1] == D
    k_blk = b_row0 // k_dim
    after = jnp.zeros((8, 128), F32) if after is None else after

    def body(a_ref, b_ref, x_ref, g_ref, dr_ref, after_ref, dx_ref, dg_ref):
        del after_ref
        dh_v = jnp.dot(a_ref[...].astype(BF16), b_ref[...].astype(BF16), preferred_element_type=F32)
        xv = x_ref[...]
        r = lax.rsqrt(jnp.mean(xv * xv, axis=-1, keepdims=True) + EPS)
        dyg = dh_v * g_ref[...]
        mq = jnp.mean(dyg * xv, axis=-1, keepdims=True)
        dx_ref[...] = dr_ref[...] + r * (dyg - xv * (r * r * mq))

        @pl.when(pl.program_id(0) == 0)
        def _():
            dg_ref[...] = jnp.zeros_like(dg_ref)

        dg_ref[...] += jnp.sum(dh_v * (xv * r), axis=0, keepdims=True)

    row = _bs((tm, D), lambda i: (i, 0))
    est = 2 * (tm * k_dim * a.dtype.itemsize + k_dim * D * b.dtype.itemsize + 3 * tm * D * 4) + 4 * tm * D * 4
    return _pc(body, name=name, grid=(m // tm,),
               in_specs=[_bs((tm, k_dim), lambda i: (i, 0)), _bs((k_dim, D), lambda i: (k_blk, 0)), row,
                         _bs((1, D), lambda i: (0, 0)), row, pl.BlockSpec(memory_space=pl.ANY)],
               out_specs=[row, _bs((1, D), lambda i: (0, 0))], out_shape=[_sds((m, D), F32), _sds((1, D), F32)],
               vmem=est + (8 << 20))(a, b, x, gain, dres, after)


def _dact_dswiglu(dout, b, g_pre, u_pre, *, tm, b_row0, after, name):
    t_dim = dout.shape[0]
    assert t_dim % tm == 0 and b_row0 % FF == 0
    k_blk = b_row0 // FF
    after = jnp.zeros((8, 128), F32) if after is None else after

    def body(d_ref, b_ref, g_ref, u_ref, after_ref, o_ref):
        del after_ref
        dav = 0.5 * lax.dot_general(d_ref[...].astype(BF16), b_ref[...].astype(BF16), NT, preferred_element_type=F32)
        g = g_ref[...].astype(F32)
        u = u_ref[...].astype(F32)
        s = _sigmoid(g)
        o_ref[:, :FF] = (dav * u * (s * (1.0 + g * (1.0 - s)))).astype(BF16)
        o_ref[:, FF:] = (dav * (g * s)).astype(BF16)

    est = 2 * (tm * D * 4 + FF * D * 2 + 2 * tm * 2 * FF * 2) + 5 * tm * FF * 4
    return _pc(body, name=name, grid=(t_dim // tm,),
               in_specs=[_bs((tm, D), lambda i: (i, 0)), _bs((FF, D), lambda i: (k_blk, 0)),
                         _bs((tm, FF), lambda i: (i, 0)), _bs((tm, FF), lambda i: (i, 0)), pl.BlockSpec(memory_space=pl.ANY)],
               out_specs=_bs((tm, 2 * FF), lambda i: (i, 0)), out_shape=_sds((t_dim, 2 * FF), BF16),
               vmem=est + (8 << 20))(dout, b, g_pre, u_pre, after)


def _rms_fwd(x, gain, name, tm=512):
    t_dim = x.shape[0]

    def body(x_ref, g_ref, o_ref):
        xv = x_ref[...]
        r = lax.rsqrt(jnp.mean(xv * xv, axis=-1, keepdims=True) + EPS)
        o_ref[...] = (xv * r * g_ref[...]).astype(BF16)

    return _pc(body, name=name, grid=(t_dim // tm,),
               in_specs=[_bs((tm, D), lambda i: (i, 0)), _bs((1, D), lambda i: (0, 0))],
               out_specs=_bs((tm, D), lambda i: (i, 0)), out_shape=_sds((t_dim, D), BF16))(x, gain)


def _sigmoid(v):
    return 1.0 / (1.0 + jnp.exp(-v))


def _gu_swiglu(h, b, *, tm, tn, b_row0, name):
    t_dim = h.shape[0]
    assert t_dim % tm == 0 and FF % tn == 0 and b_row0 % tn == 0
    j0 = b_row0 // tn

    def body(h_ref, bg_ref, bu_ref, g_ref, u_ref, a_ref):
        hv = h_ref[...]
        g16 = lax.dot_general(hv, bg_ref[...], NT, preferred_element_type=F32).astype(BF16)
        u16 = lax.dot_general(hv, bu_ref[...], NT, preferred_element_type=F32).astype(BF16)
        g = g16.astype(F32)
        g_ref[...] = g16
        u_ref[...] = u16
        a_ref[...] = (g * _sigmoid(g) * u16.astype(F32)).astype(BF16)

    tile = _bs((tm, tn), lambda i, j: (i, j))
    est = 2 * (tm * D * 2 + 2 * tn * D * 2 + 3 * tm * tn * 2) + 5 * tm * tn * 4
    return _pc(body, name=name, grid=(t_dim // tm, FF // tn),
               in_specs=[_bs((tm, D), lambda i, j: (i, 0)), _bs((tn, D), lambda i, j: (j0 + j, 0)),
                         _bs((tn, D), lambda i, j: (j0 + FF // tn + j, 0))],
               out_specs=[tile] * 3, out_shape=[_sds((t_dim, FF), BF16)] * 3, vmem=est + (8 << 20))(h, b, b)


def _final_loss(x, gain, tgt, name, tm=512):
    t_dim = x.shape[0]

    def body(x_ref, g_ref, t_ref, loss_ref, dx_ref, dg_ref):
        xv = x_ref[...]
        gv = g_ref[...]
        r = lax.rsqrt(jnp.mean(xv * xv, axis=-1, keepdims=True) + EPS)
        xn = xv * r
        err = xn * gv - t_ref[...]
        dy = err * (1.0 / D)
        dyg = dy * gv
        mq = jnp.mean(dyg * xv, axis=-1, keepdims=True)
        dx_ref[...] = r * (dyg - xv * (r * r * mq))

        @pl.when(pl.program_id(0) == 0)
        def _():
            dg_ref[...] = jnp.zeros_like(dg_ref)
            loss_ref[...] = jnp.zeros_like(loss_ref)

        dg_ref[...] += jnp.sum(dy * xn, axis=0, keepdims=True)
        part = 0.5 * jnp.sum(jnp.mean(err * err, axis=-1, keepdims=True), axis=0, keepdims=True)
        loss_ref[...] += jnp.broadcast_to(part, loss_ref.shape)

    return _pc(body, name=name, grid=(t_dim // tm,),
               in_specs=[_bs((tm, D), lambda i: (i, 0)), _bs((1, D), lambda i: (0, 0)), _bs((tm, D), lambda i: (i, 0))],
               out_specs=[_bs((8, 128), lambda i: (0, 0)), _bs((tm, D), lambda i: (i, 0)), _bs((1, D), lambda i: (0, 0))],
               out_shape=[_sds((8, 128), F32), _sds((t_dim, D), F32), _sds((1, D), F32)])(x, gain, tgt)


def _band_mask(n):
    ri = lax.broadcasted_iota(jnp.int32, (BLK, 2 * BLK), 0)
    cj = lax.broadcasted_iota(jnp.int32, (BLK, 2 * BLK), 1)
    dist = ri + BLK - cj
    return (dist >= 0) & (dist <= BLK) & ((cj >= BLK) | (n > 0))


def _halves(unit, imap_rows, col):
    return [_bs((unit, 128), (lambda u, j=j: (imap_rows(u), 2 * col + j))) for j in range(2)]


def _get_rows(pair, r, dil):
    return jnp.concatenate([h[pl.ds(r, BLK, stride=dil), :] for h in pair], axis=1)


def _put_rows(stage, r, dil, val):
    for j in range(2):
        stage[j, pl.ds(r, BLK, stride=dil), :] = val[:, 128 * j:128 * (j + 1)]


def _unstage(stage, out_ref):
    out_ref[:, :128] = stage[0]
    out_ref[:, 128:] = stage[1]


def _attn_fwd(z, acc, ml, *, gi, first, final, name):
    dil = DILS[gi]
    t_dim = z.shape[0]
    unit = BLK * dil
    nu = t_dim // unit
    own = lambda u: u
    prev = lambda u: jnp.maximum(u - 1, 0)
    in_specs = (_halves(unit, own, Q_B + gi) + _halves(unit, own, K_B + gi) + _halves(unit, prev, K_B + gi)
                + _halves(unit, own, V_B + gi) + _halves(unit, prev, V_B + gi))
    args = [z] * 10
    st_ml = _bs((unit, 128), lambda u: (u, 0))
    if not first:
        in_specs += _halves(unit, own, 0) + [st_ml]
        args += [acc, acc, ml]

    def body(*refs):
        q2, ko2, kp2, vo2, vp2 = (refs[2 * i:2 * i + 2] for i in range(5))
        o_acc_ref, o_ml_ref, stage = refs[-3], refs[-2], refs[-1]
        valid = _band_mask(pl.program_id(0))
        lane = lax.broadcasted_iota(jnp.int32, (BLK, GW), 1)
        lane_s = lax.broadcasted_iota(jnp.int32, (BLK, 128), 1)

        def band(r, carry):
            q = _get_rows(q2, r, dil) * QK_SCALE
            kb = jnp.concatenate([_get_rows(kp2, r, dil), _get_rows(ko2, r, dil)], axis=0).astype(BF16)
            vb = jnp.concatenate([_get_rows(vp2, r, dil), _get_rows(vo2, r, dil)], axis=0).astype(BF16)
            if first:
                acc_v = jnp.zeros((BLK, GW), F32)
                ml_v = jnp.where(lane_s < 64, NEG, 0.0).astype(F32)
            else:
                acc_v = _get_rows(refs[10:12], r, dil)
                ml_v = refs[12][pl.ds(r, BLK, stride=dil), :]
            heads = [(lane >= h * HD) & (lane < (h + 1) * HD) for h in range(NH)]
            rows = lambda v, h: v[h * BLK:(h + 1) * BLK]
            qs = jnp.concatenate([jnp.where(hm, q, 0.0) for hm in heads], axis=0).astype(BF16)
            s = lax.dot_general(qs, kb, NT, preferred_element_type=F32)
            s = jnp.where(jnp.concatenate([valid] * NH, axis=0), s, NEG)
            m_old = jnp.concatenate([ml_v[:, 16 * h:16 * h + 1] for h in range(NH)], axis=0)
            l_old = jnp.concatenate([ml_v[:, 64 + 16 * h:64 + 16 * h + 1] for h in range(NH)], axis=0)
            m_new = jnp.maximum(m_old, jnp.max(s, axis=-1, keepdims=True))
            alpha = jnp.exp(m_old - m_new)
            p = jnp.exp(s - m_new)
            l_new = alpha * l_old + jnp.sum(p, axis=-1, keepdims=True)
            pv = jnp.dot(p.astype(BF16), vb, preferred_element_type=F32)
            stat = m_new + jnp.log(l_new) if final else m_new
            acc_o, ml_o = acc_v, ml_v
            l_b = jnp.ones((BLK, GW), F32)
            for h, hm in enumerate(heads):
                acc_o = jnp.where(hm, acc_v * rows(alpha, h) + rows(pv, h), acc_o)
                l_b = jnp.where(hm, rows(l_new, h), l_b)
                ml_o = jnp.where((lane_s >= 16 * h) & (lane_s < 16 * h + 16), rows(stat, h), ml_o)
                if not final:
                    ml_o = jnp.where((lane_s >= 64 + 16 * h) & (lane_s < 64 + 16 * h + 16), rows(l_new, h), ml_o)
            _put_rows(stage, r, dil, acc_o / l_b if final else acc_o)
            o_ml_ref[pl.ds(r, BLK, stride=dil), :] = ml_o
            return carry

        lax.fori_loop(0, dil, band, 0, unroll=min(dil, 4))
        _unstage(stage, o_acc_ref)

    blocks = (len(in_specs) + 6) * unit * 128 * 4
    return _pc(body, name=name, grid=(nu,), in_specs=in_specs, out_specs=[_bs((unit, GW), lambda u: (u, 0)), st_ml],
               out_shape=[_sds((t_dim, GW), F32), _sds((t_dim, 128), F32)],
               scratch=[pltpu.VMEM((2, unit, 128), F32)], vmem=2 * blocks + (12 << 20))(*args)


def _attn_bwd(z, dy, y, lse, *, gi, name):
    dil = DILS[gi]
    t_dim = z.shape[0]
    unit = BLK * dil
    nu = t_dim // unit

    own = lambda u: jnp.minimum(u, nu - 1)
    prev = lambda u: jnp.maximum(jnp.minimum(u, nu - 1) - 1, 0)
    in_specs = (_halves(unit, own, Q_B + gi) + _halves(unit, own, K_B + gi) + _halves(unit, prev, K_B + gi)
                + _halves(unit, own, V_B + gi) + _halves(unit, prev, V_B + gi)
                + _halves(unit, own, 0) + _halves(unit, own, 0) + [_bs((unit, 128), lambda u: (own(u), 0))])
    tok = _bs((unit, GW), lambda u: (own(u), 0))
    kv_out = _bs((unit, GW), lambda u: (jnp.maximum(u - 1, 0), 0))

    def body(*refs):
        q2, ko2, kp2, vo2, vp2, dy2, y2 = (refs[2 * i:2 * i + 2] for i in range(7))
        lse_ref = refs[14]
        dq_ref, dk_ref, dv_ref, dq_st, dk_st, dv_st, dkc, dvc = refs[15:]
        u = pl.program_id(0)

        @pl.when(u == 0)
        def _():
            dkc[...] = jnp.zeros_like(dkc)
            dvc[...] = jnp.zeros_like(dvc)

        @pl.when(u < nu)
        def _():
            valid = _band_mask(u)
            lane = lax.broadcasted_iota(jnp.int32, (BLK, GW), 1)

            def band(r, carry):
                q = _get_rows(q2, r, dil) * QK_SCALE
                kb = jnp.concatenate([_get_rows(kp2, r, dil), _get_rows(ko2, r, dil)], axis=0).astype(BF16)
                vb = jnp.concatenate([_get_rows(vp2, r, dil), _get_rows(vo2, r, dil)], axis=0).astype(BF16)
                dyv = _get_rows(dy2, r, dil)
                yv = _get_rows(y2, r, dil)
                lse_v = lse_ref[pl.ds(r, BLK, stride=dil), :]
                heads = [(lane >= h * HD) & (lane < (h + 1) * HD) for h in range(NH)]
                qs = jnp.concatenate([jnp.where(hm, q, 0.0) for hm in heads], axis=0).astype(BF16)
                dys = jnp.concatenate([jnp.where(hm, dyv, 0.0) for hm in heads], axis=0)
                lse_s = jnp.concatenate([lse_v[:, 16 * h:16 * h + 1] for h in range(NH)], axis=0)
                s = lax.dot_general(qs, kb, NT, preferred_element_type=F32)
                p = jnp.where(jnp.concatenate([valid] * NH, axis=0), jnp.exp(s - lse_s), 0.0)
                delta = jnp.sum(dys * jnp.concatenate([yv] * NH, axis=0), axis=-1, keepdims=True)
                dys16 = dys.astype(BF16)
                dp = lax.dot_general(dys16, vb, NT, preferred_element_type=F32)
                ds = (p * (dp - delta)).astype(BF16)
                dq_all = jnp.dot(ds, kb, preferred_element_type=F32)
                dq = jnp.zeros((BLK, GW), F32)
                for h, hm in enumerate(heads):
                    dq = jnp.where(hm, dq_all[h * BLK:(h + 1) * BLK], dq)
                dkb = lax.dot_general(ds, qs, TN, preferred_element_type=F32)
                dvb = lax.dot_general(p.astype(BF16), dys16, TN, preferred_element_type=F32)
                _put_rows(dq_st, r, dil, dq * QK_SCALE)
                _put_rows(dk_st, r, dil, _get_rows((dkc.at[0], dkc.at[1]), r, dil) + dkb[:BLK])
                _put_rows(dv_st, r, dil, _get_rows((dvc.at[0], dvc.at[1]), r, dil) + dvb[:BLK])
                _put_rows(dkc, r, dil, dkb[BLK:])
                _put_rows(dvc, r, dil, dvb[BLK:])
                return carry

            lax.fori_loop(0, dil, band, 0, unroll=min(dil, 4))
            _unstage(dq_st, dq_ref)

            @pl.when(u > 0)
            def _():
                _unstage(dk_st, dk_ref)
                _unstage(dv_st, dv_ref)

        @pl.when(u == nu)
        def _():
            _unstage(dkc, dk_ref)
            _unstage(dvc, dv_ref)

    blocks = (15 + 6 + 5) * unit * 128 * 4 * 2
    return _pc(body, name=name, grid=(nu + 1,), in_specs=in_specs, out_specs=[tok, kv_out, kv_out],
               out_shape=[_sds((t_dim, GW), F32)] * 3, scratch=[pltpu.VMEM((2, unit, 128), F32)] * 5,
               vmem=blocks + (12 << 20))(*([z] * 10), dy, dy, y, y, lse)


_INV_SQRT2 = 1.0 / math.sqrt(2.0)
_INV_SQRT2PI = 1.0 / math.sqrt(2.0 * math.pi)


def _gelu(v):
    return 0.5 * v * (1.0 + lax.erf(v * _INV_SQRT2))


def _gelu_grad(v):
    return 0.5 * (1.0 + lax.erf(v * _INV_SQRT2)) + v * (_INV_SQRT2PI * jnp.exp(-0.5 * v * v))


def _gmlp_core(z_refs, lng, lnb, ws_ref, bs_ref):
    u_pre = jnp.concatenate([z_refs[0][...], z_refs[1][...]], axis=1)
    v_pre = jnp.concatenate([z_refs[2][...], z_refs[3][...]], axis=1)
    u = _gelu(u_pre)
    v = _gelu(v_pre)
    xc = v - jnp.mean(v, axis=-1, keepdims=True)
    rstd = lax.rsqrt(jnp.mean(xc * xc, axis=-1, keepdims=True) + EPS)
    xhat = xc * rstd
    vn = xhat * lng + lnb
    ti = lax.broadcasted_iota(jnp.int32, (BLK, BLK), 0)
    si = lax.broadcasted_iota(jnp.int32, (BLK, BLK), 1)
    causal = ti >= si
    vn16 = vn.astype(BF16)
    ws = [jnp.where(causal, ws_ref[g], 0.0).astype(BF16) for g in range(4)]
    mixed = jnp.concatenate(
        [jnp.dot(ws[g], vn16[:, g * BLK:(g + 1) * BLK], preferred_element_type=F32) + bs_ref[g] for g in range(4)], axis=1)
    return u_pre, v_pre, u, xhat, rstd, vn16, ws, causal, mixed


def _gmlp_specs():
    z_specs = [_bs((BLK, GW), (lambda i, c=c: (i, UV_B + c))) for c in range(4)]
    par_specs = [_bs((1, BW), lambda i: (0, 0)), _bs((1, BW), lambda i: (0, 0)),
                 _bs((4, BLK, BLK), lambda i: (0, 0, 0)), _bs((4, BLK, 1), lambda i: (0, 0, 0))]
    return z_specs, par_specs


def _gmlp_fwd(z, lng, lnb, ws, bs, name):
    t_dim = z.shape[0]
    z_specs, par_specs = _gmlp_specs()

    def body(z0, z1, z2, z3, lng_ref, lnb_ref, ws_ref, bs_ref, o_ref):
        core = _gmlp_core((z0, z1, z2, z3), lng_ref[...], lnb_ref[...], ws_ref, bs_ref)
        o_ref[...] = (core[2] * core[8]).astype(BF16)

    return _pc(body, name=name, grid=(t_dim // BLK,), in_specs=z_specs + par_specs,
               out_specs=_bs((BLK, BW), lambda i: (i, 0)), out_shape=_sds((t_dim, BW), BF16))(z, z, z, z, lng, lnb, ws, bs)


def _gmlp_bwd(z, dy, lng, lnb, ws, bs, name):
    t_dim = z.shape[0]
    z_specs, par_specs = _gmlp_specs()

    def body(z0, z1, z2, z3, lng_ref, lnb_ref, ws_ref, bs_ref, dy_ref, duv_ref, dws_ref, dbs_ref, dlng_ref, dlnb_ref):
        lng_v = lng_ref[...]
        u_pre, v_pre, u, xhat, rstd, vn16, wsm, causal, mixed = _gmlp_core((z0, z1, z2, z3), lng_v, lnb_ref[...], ws_ref, bs_ref)
        dyv = dy_ref[...].astype(F32)
        du = dyv * mixed
        dmixed = dyv * u

        @pl.when(pl.program_id(0) == 0)
        def _():
            dws_ref[...] = jnp.zeros_like(dws_ref)
            dbs_ref[...] = jnp.zeros_like(dbs_ref)
            dlng_ref[...] = jnp.zeros_like(dlng_ref)
            dlnb_ref[...] = jnp.zeros_like(dlnb_ref)

        dvn_parts = []
        for g in range(4):
            dm = dmixed[:, g * BLK:(g + 1) * BLK]
            dm16 = dm.astype(BF16)
            dw = lax.dot_general(dm16, vn16[:, g * BLK:(g + 1) * BLK], NT, preferred_element_type=F32)
            dws_ref[g] += jnp.where(causal, dw, 0.0)
            dbs_ref[g] += jnp.sum(dm.T, axis=0, keepdims=True)
            dvn_parts.append(lax.dot_general(wsm[g], dm16, TN, preferred_element_type=F32))
        dvn = jnp.concatenate(dvn_parts, axis=1)
        dlng_ref[...] += jnp.sum(dvn * xhat, axis=0, keepdims=True)
        dlnb_ref[...] += jnp.sum(dvn, axis=0, keepdims=True)
        dxh = dvn * lng_v
        dv = rstd * (dxh - jnp.mean(dxh, axis=-1, keepdims=True) - xhat * jnp.mean(dxh * xhat, axis=-1, keepdims=True))
        duv_ref[:, :BW] = (du * _gelu_grad(u_pre)).astype(BF16)
        duv_ref[:, BW:] = (dv * _gelu_grad(v_pre)).astype(BF16)

    return _pc(body, name=name, grid=(t_dim // BLK,),
               in_specs=z_specs + par_specs + [_bs((BLK, BW), lambda i: (i, 0))],
               out_specs=[_bs((BLK, 2 * BW), lambda i: (i, 0)), _bs((4, BLK, BLK), lambda i: (0, 0, 0)),
                          _bs((4, 1, BLK), lambda i: (0, 0, 0)), _bs((1, BW), lambda i: (0, 0)), _bs((1, BW), lambda i: (0, 0))],
               out_shape=[_sds((t_dim, 2 * BW), BF16), _sds((4, BLK, BLK), F32), _sds((4, 1, BLK), F32),
                          _sds((1, BW), F32), _sds((1, BW), F32)])(z, z, z, z, lng, lnb, ws, bs, dy)


def _pool_core(x, gi):
    t_dim = x.shape[0]
    w = jnp.left_shift(2, gi)
    row = lax.broadcasted_iota(jnp.int32, x.shape, 0)
    s = x
    for k in (1, 2, 4, 8):
        sh = jnp.where(row >= k, pltpu.roll(s, k, 0), 0.0)
        s = jnp.where(k < w, s + sh, s)
    cnt = jnp.minimum(row + 1, w).astype(F32)
    return s / cnt - x, cnt, row, w, t_dim


def _pool_fwd(z, cw, cs, name):
    t_dim = z.shape[0]

    def body(z_ref, cw_ref, cs_ref, o_ref):
        pooled = _pool_core(z_ref[...], pl.program_id(0))[0]
        y = jnp.dot(pooled.astype(BF16), cw_ref[0].astype(BF16), preferred_element_type=F32) * cs_ref[0]
        o_ref[...] = y.astype(BF16)

    return _pc(body, name=name, grid=(4,),
               in_specs=[_bs((t_dim, BLK), lambda g: (0, C_B128 + g)), _bs((1, BLK, BLK), lambda g: (g, 0, 0)),
                         _bs((1, 1, BLK), lambda g: (g, 0, 0))],
               out_specs=_bs((t_dim, BLK), lambda g: (0, g)), out_shape=_sds((t_dim, BW), BF16),
               vmem=40 << 20)(z, cw, cs)


def _pool_bwd(z, dy, cw, cs, name):
    t_dim = z.shape[0]

    def body(z_ref, dy_ref, cw_ref, cs_ref, dx_ref, dcw_ref, dcs_ref):
        pooled, cnt, row, w, _ = _pool_core(z_ref[...], pl.program_id(0))
        p16 = pooled.astype(BF16)
        cw16 = cw_ref[0].astype(BF16)
        dyv = dy_ref[...].astype(F32)
        lin = jnp.dot(p16, cw16, preferred_element_type=F32)
        dcs_ref[0] = jnp.sum(dyv * lin, axis=0, keepdims=True)
        dys = (dyv * cs_ref[0]).astype(BF16)
        dcw_ref[0] = lax.dot_general(p16, dys, TN, preferred_element_type=F32)
        dpool = lax.dot_general(dys, cw16, NT, preferred_element_type=F32)
        sb = dpool / cnt
        for k in (1, 2, 4, 8):
            sh = jnp.where(row < t_dim - k, pltpu.roll(sb, t_dim - k, 0), 0.0)
            sb = jnp.where(k < w, sb + sh, sb)
        dx_ref[...] = (sb - dpool).astype(BF16)

    return _pc(body, name=name, grid=(4,),
               in_specs=[_bs((t_dim, BLK), lambda g: (0, C_B128 + g)), _bs((t_dim, BLK), lambda g: (0, g)),
                         _bs((1, BLK, BLK), lambda g: (g, 0, 0)), _bs((1, 1, BLK), lambda g: (g, 0, 0))],
               out_specs=[_bs((t_dim, BLK), lambda g: (0, g)), _bs((1, BLK, BLK), lambda g: (g, 0, 0)),
                          _bs((1, 1, BLK), lambda g: (g, 0, 0))],
               out_shape=[_sds((t_dim, BW), BF16), _sds((4, BLK, BLK), F32), _sds((4, 1, BLK), F32)],
               vmem=48 << 20)(z, dy, cw, cs)


def _gates_specs(tm):
    z_specs = [_bs((tm, GW), (lambda j, i, b=b: (i, G_B + 4 * b + j))) for b in range(3)]
    bg_specs = [_bs((1, GW), (lambda j, i, b=b: (0, 4 * b + j))) for b in range(3)]
    tile = _bs((tm, GW), lambda j, i: (i, j))
    return z_specs, bg_specs, tile


def _gates_fwd(z, bg, pa, pb, pc, name, tm=512):
    t_dim = z.shape[0]
    z_specs, bg_specs, tile = _gates_specs(tm)

    def body(za, zb, zc, ba, bb, bc, pa_ref, pb_ref, pc_ref, o_ref):
        o_ref[...] = (_sigmoid(za[...] + ba[...]) * pa_ref[...] + _sigmoid(zb[...] + bb[...]) * pb_ref[...]
                      + _sigmoid(zc[...] + bc[...]) * pc_ref[...]).astype(BF16)

    return _pc(body, name=name, grid=(D // GW, t_dim // tm), in_specs=z_specs + bg_specs + [tile] * 3,
               out_specs=tile, out_shape=_sds((t_dim, D), BF16))(z, z, z, bg, bg, bg, pa, pb, pc)


def _gates_bwd(z, bg, pa, pb, pc, dm, name, tm=512):
    t_dim = z.shape[0]
    z_specs, bg_specs, tile = _gates_specs(tm)
    bias_out = _bs((1, GW), lambda j, i: (0, j))

    def body(za, zb, zc, ba, bb, bc, pa_ref, pb_ref, pc_ref, dm_ref, dpa, dpb, dpc, dza, dzb, dzc, dba, dbb, dbc):
        dmv = dm_ref[...]
        first = pl.program_id(1) == 0
        for z_ref, b_ref, p_ref, dp_ref, dz_ref, db_ref in ((za, ba, pa_ref, dpa, dza, dba), (zb, bb, pb_ref, dpb, dzb, dbb),
                                                            (zc, bc, pc_ref, dpc, dzc, dbc)):
            gate = _sigmoid(z_ref[...] + b_ref[...])
            dp_ref[...] = (gate * dmv).astype(BF16)
            dz = dmv * p_ref[...] * (gate * (1.0 - gate))
            dz_ref[...] = dz.astype(BF16)

            @pl.when(first)
            def _():
                db_ref[...] = jnp.zeros_like(db_ref)

            db_ref[...] += jnp.sum(dz, axis=0, keepdims=True)

    return _pc(body, name=name, grid=(D // GW, t_dim // tm), in_specs=z_specs + bg_specs + [tile] * 4,
               out_specs=[tile] * 6 + [bias_out] * 3,
               out_shape=[_sds((t_dim, D), BF16)] * 6 + [_sds((1, D), F32)] * 3)(z, z, z, bg, bg, bg, pa, pb, pc, dm)


def _concat_bf16(parts, name, tm=512):
    t_dim = parts[0].shape[0]
    offs = [sum(p.shape[1] for p in parts[:i]) for i in range(len(parts) + 1)]

    def body(*refs):
        o_ref = refs[-1]
        for ref, off in zip(refs[:-1], offs):
            o_ref[:, off:off + ref.shape[1]] = ref[...].astype(BF16)

    return _pc(body, name=name, grid=(t_dim // tm,), in_specs=[_bs((tm, p.shape[1]), lambda i: (i, 0)) for p in parts],
               out_specs=_bs((tm, offs[-1]), lambda i: (i, 0)), out_shape=_sds((t_dim, offs[-1]), BF16),
               vmem=48 << 20)(*parts)


def _coords():
    return lax.axis_index("x"), lax.axis_index("y"), lax.axis_index("c")


def _peer(k, x, y, c):
    px = 1 - x if k & 4 else x
    py = 1 - y if k & 2 else y
    pc = 1 - c if k & 1 else c
    return (px, py, pc), 4 * px + 2 * py + pc


def _exchange_copies(kind, bufs, send, recv):
    x, y, c = _coords()
    me = 4 * x + 2 * y + c
    na = len(bufs) if kind == "gather" else len(bufs) // 2
    out = []
    for k in range(1, NDEV):
        to, peer = _peer(k, x, y, c)
        for a in range(na):
            sems = dict(send_sem=send.at[a * 7 + k - 1], recv_sem=recv.at[a * 7 + k - 1], device_id=to, device_id_type=MESH)
            if kind == "gather":
                mine = pltpu.make_async_remote_copy(src_ref=bufs[a].at[:, me], dst_ref=bufs[a].at[:, me], **sems)
                theirs = pltpu.make_async_remote_copy(src_ref=bufs[a].at[:, me], dst_ref=bufs[a].at[:, peer], **sems)
            else:
                src, land = bufs[a], bufs[na + a]
                mine = pltpu.make_async_remote_copy(src_ref=src.at[:, peer], dst_ref=land.at[me], **sems)
                theirs = pltpu.make_async_remote_copy(src_ref=src.at[:, peer], dst_ref=land.at[peer], **sems)
            out.append((mine, theirs))
    return out


_HBM = pl.BlockSpec(memory_space=pltpu.HBM)
_SEM = pl.BlockSpec(memory_space=pltpu.SEMAPHORE)
_EFFECT = pltpu.SideEffectType.DATAFLOW_SIDE_EFFECTING


def _exchange_start(kind, arrays, after, name):
    na = len(arrays)
    if kind == "scatter":
        arrays = list(arrays) + [lax.empty((NDEV, s.shape[0]) + s.shape[2:], s.dtype) for s in arrays]
    nb = len(arrays)

    def body(*refs):
        bufs = refs[:nb]
        send, recv = refs[nb + 1], refs[nb + 2]
        token = refs[-1]
        for mine, _ in _exchange_copies(kind, bufs, send, recv):
            mine.start()
        token[...] = jnp.zeros_like(token)

    arrays = [pltpu.with_memory_space_constraint(b, pltpu.HBM) for b in arrays]
    outs = pl.pallas_call(
        body, name=name,
        out_shape=(pltpu.SemaphoreType.DMA((7 * na,)), pltpu.SemaphoreType.DMA((7 * na,)),
                   *[pltpu.HBM(b.shape, b.dtype) for b in arrays], _sds((8, 128), F32)),
        in_specs=[_HBM] * nb + [pl.BlockSpec(memory_space=pl.ANY)],
        out_specs=(_SEM, _SEM, *([_HBM] * nb), pl.BlockSpec(memory_space=pltpu.VMEM)),
        input_output_aliases={i: 2 + i for i in range(nb)},
        compiler_params=pltpu.CompilerParams(has_side_effects=_EFFECT), interpret=False)(*arrays, after)
    return outs[0], outs[1], list(outs[2:2 + nb]), outs[-1]


def _exchange_wait(kind, started, after, name):
    send, recv, arrays, _ = started
    nb = len(arrays)

    def body(*refs):
        for mine, theirs in _exchange_copies(kind, refs[:nb], refs[nb], refs[nb + 1]):
            mine.wait_send()
            theirs.wait_recv()

    outs = pl.pallas_call(
        body, name=name,
        out_shape=tuple(pltpu.HBM(b.shape, b.dtype) for b in arrays),
        in_specs=[_HBM] * nb + [_SEM, _SEM, pl.BlockSpec(memory_space=pl.ANY)],
        out_specs=tuple([_HBM] * nb),
        input_output_aliases={i: i for i in range(nb)},
        compiler_params=pltpu.CompilerParams(has_side_effects=_EFFECT), interpret=False)(*arrays, send, recv, after)
    return list(outs)


def _adam_math(w, g, m, v):
    m2 = ADAM_B1 * m + (1.0 - ADAM_B1) * g
    v2 = ADAM_B2 * v + (1.0 - ADAM_B2) * jnp.square(g)
    m_hat = m2 / (1.0 - ADAM_B1 ** ADAM_STEP)
    v_hat = v2 / (1.0 - ADAM_B2 ** ADAM_STEP)
    return -ADAM_LR * (m_hat / (jnp.sqrt(v_hat) + ADAM_EPS) + ADAM_WD * w), m2, v2


_DIRECT = (1, 4, 2, 6)


def _gather_copies(bufs, send_d, recv_d, send_f=None, recv_f=None):
    x, y, c = _coords()
    me = 4 * x + 2 * y + c
    sibling, _ = _peer(1, x, y, c)
    direct, forward = [], []
    for a, buf in enumerate(bufs):
        for j, k in enumerate(_DIRECT):
            to, peer = _peer(k, x, y, c)
            sems = dict(send_sem=send_d.at[4 * a + j], recv_sem=recv_d.at[4 * a + j], device_id=to, device_id_type=MESH)
            direct.append((pltpu.make_async_remote_copy(src_ref=buf.at[:, me], dst_ref=buf.at[:, me], **sems),
                           pltpu.make_async_remote_copy(src_ref=buf.at[:, me], dst_ref=buf.at[:, peer], **sems)))
        if send_f is None:
            continue
        for j, k in enumerate(_DIRECT[1:]):
            _, near = _peer(k, x, y, c)
            _, far = _peer(k | 1, x, y, c)
            sems = dict(send_sem=send_f.at[3 * a + j], recv_sem=recv_f.at[3 * a + j], device_id=sibling, device_id_type=MESH)
            forward.append((pltpu.make_async_remote_copy(src_ref=buf.at[:, near], dst_ref=buf.at[:, near], **sems),
                            pltpu.make_async_remote_copy(src_ref=buf.at[:, near], dst_ref=buf.at[:, far], **sems)))
    return direct, forward


def _gather_start(lands, after, name):
    na = len(lands)

    def body(*refs):
        direct, _ = _gather_copies(refs[:na], refs[na + 1], refs[na + 2])
        for mine, _ in direct:
            mine.start()
        refs[-1][...] = jnp.zeros_like(refs[-1])

    lands = [pltpu.with_memory_space_constraint(b, pltpu.HBM) for b in lands]
    outs = pl.pallas_call(
        body, name=name,
        out_shape=(pltpu.SemaphoreType.DMA((4 * na,)), pltpu.SemaphoreType.DMA((4 * na,)),
                   *[pltpu.HBM(b.shape, b.dtype) for b in lands], _sds((8, 128), F32)),
        in_specs=[_HBM] * na + [pl.BlockSpec(memory_space=pl.ANY)],
        out_specs=(_SEM, _SEM, *([_HBM] * na), pl.BlockSpec(memory_space=pltpu.VMEM)),
        input_output_aliases={i: 2 + i for i in range(na)},
        compiler_params=pltpu.CompilerParams(has_side_effects=_EFFECT), interpret=False)(*lands, after)
    return outs[0], outs[1], list(outs[2:2 + na]), outs[-1]


def _gather_forward(started, after, name):
    send_d, recv_d, lands, _ = started
    na = len(lands)

    def body(*refs):
        direct, forward = _gather_copies(refs[:na], refs[na], refs[na + 1], refs[2 * na + 3], refs[2 * na + 4])
        for a in range(na):
            for j in range(3):
                direct[4 * a + 1 + j][1].wait_recv()
                forward[3 * a + j][0].start()
        refs[-1][...] = jnp.zeros_like(refs[-1])

    outs = pl.pallas_call(
        body, name=name,
        out_shape=(*[pltpu.HBM(b.shape, b.dtype) for b in lands], pltpu.SemaphoreType.DMA((3 * na,)),
                   pltpu.SemaphoreType.DMA((3 * na,)), _sds((8, 128), F32)),
        in_specs=[_HBM] * na + [_SEM, _SEM, pl.BlockSpec(memory_space=pl.ANY)],
        out_specs=(*([_HBM] * na), _SEM, _SEM, pl.BlockSpec(memory_space=pltpu.VMEM)),
        input_output_aliases={i: i for i in range(na)},
        compiler_params=pltpu.CompilerParams(has_side_effects=_EFFECT), interpret=False)(*lands, send_d, recv_d, after)
    return send_d, recv_d, outs[na], outs[na + 1], list(outs[:na]), outs[-1]


def _gather_wait(forwarded, after, name):
    send_d, recv_d, send_f, recv_f, lands, _ = forwarded
    na = len(lands)

    def body(*refs):
        direct, forward = _gather_copies(refs[:na], refs[na], refs[na + 1], refs[na + 2], refs[na + 3])
        for a in range(na):
            for j in range(4):
                direct[4 * a + j][0].wait_send()
            direct[4 * a][1].wait_recv()
            for j in range(3):
                forward[3 * a + j][0].wait_send()
                forward[3 * a + j][1].wait_recv()

    outs = pl.pallas_call(
        body, name=name, out_shape=tuple(pltpu.HBM(b.shape, b.dtype) for b in lands),
        in_specs=[_HBM] * na + [_SEM] * 4 + [pl.BlockSpec(memory_space=pl.ANY)], out_specs=tuple([_HBM] * na),
        input_output_aliases={i: i for i in range(na)},
        compiler_params=pltpu.CompilerParams(has_side_effects=_EFFECT), interpret=False)(
            *lands, send_d, recv_d, send_f, recv_f, after)
    return list(outs)


def _reduce_update(landed, own, me, piece, w, m, v, layer, prev, transposed, name):
    rows, cols = landed.shape[2], landed.shape[3]
    tc = 256
    assert cols % tc == 0 and w.shape[1:] == ((cols, rows) if transposed else (rows, cols))
    lane_chunks = [(c0, min(128, rows - c0)) for c0 in range(0, rows, 128)]

    def body(me_ref, b_ref, own_ref, w_ref, m_ref, v_ref, *rest):
        outs = rest[-4:]
        mine = me_ref[0]
        s = None
        for d in range(NDEV):
            other = jnp.where(mine == d, (d + 1) % NDEV, d)
            part = jnp.where(mine == d, own_ref[0, 0], b_ref[other, 0]).astype(F32)
            s = part if s is None else s + part
        if not transposed:
            for o, val in zip(outs, (s,) + _adam_math(w_ref[0], s, m_ref[0], v_ref[0])):
                o[0] = val
            return
        for c0, wd in lane_chunks:
            chunk = s[c0:c0 + wd]
            if wd < 128:
                chunk = jnp.concatenate([chunk, jnp.zeros((128 - wd, tc), F32)], axis=0)
            g = chunk.T[:, :wd]
            sl = (0, slice(None), slice(c0, c0 + wd))
            for o, val in zip(outs, (g,) + _adam_math(w_ref[sl], g, m_ref[sl], v_ref[sl])):
                o[sl] = val

    if transposed:
        nat = _bs((1, tc, rows), lambda j, me_ref: (layer, j, 0))
    else:
        nat = _bs((1, rows, tc), lambda j, me_ref: (layer, 0, j))
    in_specs = [_bs((NDEV, 1, rows, tc), lambda j, me_ref: (0, piece, 0, j)),
                _bs((1, 1, rows, tc), lambda j, me_ref: (piece, me_ref[0], 0, j)), nat, nat, nat]
    args = [me, landed, own, w, m, v]
    alias = {}
    if prev is not None:
        in_specs += [pl.BlockSpec(memory_space=pl.ANY)] * 4
        args += list(prev)
        alias = {6 + i: i for i in range(4)}
    grid_spec = pltpu.PrefetchScalarGridSpec(num_scalar_prefetch=1, grid=(cols // tc,), in_specs=in_specs, out_specs=[nat] * 4)
    return pl.pallas_call(body, name=name, grid_spec=grid_spec, out_shape=[_sds(w.shape, F32)] * 4,
                          input_output_aliases=alias, interpret=False)(*args)


SMALL_LAYER = ("ffn1_norm", "mix_norm", "b_gate", "b_ln_g", "b_ln_b", "b_w_s", "b_b_s", "c_w", "c_scale", "ffn2_norm")
SMALL_ROWS = {"ffn1_norm": 8, "mix_norm": 8, "b_gate": 24, "b_ln_g": 4, "b_ln_b": 4, "b_w_s": 512, "b_b_s": 4, "c_w": 512,
              "c_scale": 4, "ffn2_norm": 8, "final_norm": 8}
LAYER_ROWS = sum(SMALL_ROWS[k] for k in SMALL_LAYER)


def _small_offset(name):
    if name == "final_norm":
        return LAYER_ROWS
    return sum(SMALL_ROWS[k] for k in SMALL_LAYER[:SMALL_LAYER.index(name)])


def _pack_small_grads(layer_grads, final, me, name):
    flat, where = [], []
    for k in SMALL_LAYER:
        parts = layer_grads[k] if k == "b_gate" else [layer_grads[k]]
        for i, part in enumerate(parts):
            flat.append(part)
            where.append(_small_offset(k) + i * (D // 128))
    if final is not None:
        flat.append(final)
        where.append(_small_offset("final_norm"))
    pack_rows = LAYER_ROWS + (SMALL_ROWS["final_norm"] if final is not None else 0)

    def body(me_ref, *refs):
        del me_ref
        o_ref = refs[-1]
        for ref, off in zip(refs[:-1], where):
            if len(ref.shape) == 2:
                for j in range(ref.shape[1] // 128):
                    o_ref[0, 0, off + j:off + j + 1, :] = ref[:, 128 * j:128 * (j + 1)]
            else:
                rows = ref.shape[1]
                for g in range(4):
                    o_ref[0, 0, off + g * rows:off + (g + 1) * rows, :] = ref[g]

    grid_spec = pltpu.PrefetchScalarGridSpec(
        num_scalar_prefetch=1, grid=(1,),
        in_specs=[pl.BlockSpec(a.shape, (lambda i, me_ref, nd=a.ndim: (0,) * nd)) for a in flat],
        out_specs=_bs((1, 1, pack_rows, 128), lambda i, me_ref: (0, me_ref[0], 0, 0)))
    return pl.pallas_call(body, name=name, grid_spec=grid_spec, out_shape=_sds((1, NDEV, pack_rows, 128), F32),
                          interpret=False)(me, *flat)


def _adamw_small(gathered, w, mom, var, name):
    names = SMALL_LAYER + ("final_norm",)
    ins = list(gathered) + [t[k] for k in names for t in (w, mom, var)]

    def body(*refs):
        outs = refs[len(ins):]

        def rows(l, off, n):
            land = refs[l]
            s = land[0, 0, off:off + n, :]
            for d in range(1, NDEV):
                s = s + land[0, d, off:off + n, :]
            return s

        def as_row(l, off, n):
            s = rows(l, off, n)
            return jnp.concatenate([s[j:j + 1, :] for j in range(n)], axis=1)

        def update(i, idx, g):
            wr, mr, vr = (refs[DEPTH + 3 * i + t] for t in range(3))
            d, m2, v2 = _adam_math(wr[idx], g, mr[idx], vr[idx])
            for o, val in zip(outs[4 * i:4 * i + 4], (g, d, m2, v2)):
                o[idx] = val

        for i, k in enumerate(names):
            n = SMALL_ROWS[k]
            off = _small_offset(k)
            if k == "final_norm":
                update(i, (slice(0, 1),), as_row(DEPTH - 1, off, n))
                continue
            for l in range(DEPTH):
                if k in ("b_w_s", "c_w"):
                    for g in range(4):
                        update(i, (l, g), rows(l, off + g * BLK, BLK))
                elif k in ("b_b_s", "c_scale"):
                    update(i, (l,), rows(l, off, n))
                else:
                    update(i, (slice(l, l + 1),), as_row(l, off, n))

    out_shape = [_sds(w[k].shape, F32) for k in names for _ in range(4)]
    outs = pl.pallas_call(body, name=name, out_shape=out_shape,
                          compiler_params=pltpu.CompilerParams(vmem_limit_bytes=48 << 20), interpret=False)(*ins)
    return tuple({k: outs[4 * i + t] for i, k in enumerate(names)} for t in range(4))


PG, PD = 0, 2


def _ffn_fwd(x, gain, wf, tag, mid):
    h = _rms_fwd(x, gain, f"rms_{tag}")
    g_pre, u_pre, act = _gu_swiglu(h, wf, tm=2048, tn=256, b_row0=PG * FF, name=f"gu_{tag}")
    token = mid(act)
    out = _mm(act, wf, mode="nn", tm=512, tn=D, tk=FF, n=D, b_row0=PD * FF, out_dtype=F32, scale=0.5, res=x, after=token,
              name=f"down_{tag}")
    return out, (x, h, g_pre, u_pre, act)


def _ffn_bwd(dout, saved, gain, wf, tag, after, emit):
    x, h, g_pre, u_pre, act = saved
    grf = lax.empty((3, NDEV, FFS, D), BF16)
    dgu = _dact_dswiglu(dout, wf, g_pre, u_pre, tm=256, b_row0=PD * FF, after=after, name=f"dgu_{tag}")
    grf = _wgrad(act, dout, grf, piece0=PD, rows=FFS, tm=FF // 2, tn=D, tk=1024, scale=0.5, name=f"wg_down_{tag}")
    grf = _wgrad(dgu, h, grf, piece0=PG, rows=FFS, tm=FF // 2, tn=D, tk=2048, name=f"wg_gu_{tag}")
    token = emit([grf])
    dx, dgain = _mm_drms(dgu, wf, x, gain, dout, tm=256, b_row0=PG * FF, after=token, name=f"dx_{tag}")
    return dx, dgain, token


def _mixer_fwd(x, sp, ww, wo, wa, wbc, tag, mid):
    h = _rms_fwd(x, sp["mix_norm"], f"rms_mix_{tag}")
    z = _mm(h, ww, mode="nt", tm=1024, tn=1152, tk=D, n=INW, out_dtype=F32, name=f"z_{tag}")
    token = mid(z)
    acc, ml = _attn_fwd(z, None, None, gi=0, first=True, final=False, name=f"attn0_{tag}")
    acc, ml = _attn_fwd(z, acc, ml, gi=1, first=False, final=False, name=f"attn1_{tag}")
    ya, lse = _attn_fwd(z, acc, ml, gi=2, first=False, final=True, name=f"attn2_{tag}")
    yb = _gmlp_fwd(z, sp["b_ln_g"], sp["b_ln_b"], sp["b_w_s"], sp["b_b_s"], f"gmlp_{tag}")
    yc = _pool_fwd(z, sp["c_w"], sp["c_scale"], f"pool_{tag}")
    pa = _mm(ya, wa, mode="nt", tm=1024, tn=D, tk=GW, n=D, out_dtype=F32, after=token, name=f"proj_a_{tag}")
    pb = _mm(yb, wbc, mode="nt", tm=1024, tn=D, tk=BW, n=D, b_row0=0, out_dtype=F32, name=f"proj_b_{tag}")
    pc = _mm(yc, wbc, mode="nt", tm=1024, tn=D, tk=BW, n=D, b_row0=D, out_dtype=F32, name=f"proj_c_{tag}")
    merged = _gates_fwd(z, sp["b_gate"], pa, pb, pc, f"gates_{tag}")
    out = _mm(merged, wo, mode="nn", tm=1024, tn=D, tk=D, n=D, out_dtype=F32, res=x, name=f"out_{tag}")
    return out, (x, h, z, ya, lse, yb, yc, pa, pb, pc, merged)


def _mixer_bwd(dout, saved, sp, ww, wo, wa, wbc, tag, after, emit):
    x, h, z, ya, lse, yb, yc, pa, pb, pc, merged = saved
    grw = lax.empty((1, NDEV, INS, D), BF16)
    gro = lax.empty((1, NDEV, DS, D), BF16)
    gra = lax.empty((1, NDEV, DS, GW), BF16)
    grbc = lax.empty((2, NDEV, DS, BW), BF16)
    dm = _mm(dout, wo, mode="nt", tm=1024, tn=D, tk=D, n=D, out_dtype=F32, after=after, name=f"dmerged_{tag}")
    gro = _wgrad(merged, dout, gro, piece0=0, rows=DS, tm=D, tn=D, tk=1024, name=f"wg_out_{tag}")
    dpa, dpb, dpc, dza, dzb, dzc, dba, dbb, dbc = _gates_bwd(z, sp["b_gate"], pa, pb, pc, dm, f"dgates_{tag}")
    dya = _mm(dpa, wa, mode="nn", tm=2048, tn=GW, tk=D, n=GW, out_dtype=F32, name=f"dya_{tag}")
    dyb = _mm(dpb, wbc, mode="nn", tm=2048, tn=BW, tk=D, n=BW, b_row0=0, out_dtype=F32, name=f"dyb_{tag}")
    dyc = _mm(dpc, wbc, mode="nn", tm=2048, tn=BW, tk=D, n=BW, b_row0=D, out_dtype=F32, name=f"dyc_{tag}")
    gra = _wgrad(dpa, ya, gra, piece0=0, rows=DS, tm=D, tn=GW, tk=1024, name=f"wg_pa_{tag}")
    grbc = _wgrad(dpb, yb, grbc, piece0=0, rows=DS, tm=D, tn=BW, tk=1024, name=f"wg_pb_{tag}")
    grbc = _wgrad(dpc, yc, grbc, piece0=1, rows=DS, tm=D, tn=BW, tk=1024, name=f"wg_pc_{tag}")
    dqkv = [_attn_bwd(z, dya, ya, lse, gi=gi, name=f"dattn{gi}_{tag}") for gi in range(3)]
    duv, dws, dbs, dlng, dlnb = _gmlp_bwd(z, dyb, sp["b_ln_g"], sp["b_ln_b"], sp["b_w_s"], sp["b_b_s"], f"dgmlp_{tag}")
    dxc, dcw, dcs = _pool_bwd(z, dyc, sp["c_w"], sp["c_scale"], f"dpool_{tag}")
    dz = _concat_bf16([dqkv[gi][part] for part in range(3) for gi in range(3)] + [duv, dxc, dza, dzb, dzc], f"dz_{tag}")
    grw = _wgrad(dz, h, grw, piece0=0, rows=INS, tm=INW // 2, tn=512, tk=1024, name=f"wg_in_{tag}")
    token = emit([grw, gro, gra, grbc])
    dx, dgain = _mm_drms(dz, ww, x, sp["mix_norm"], dout, tm=256, b_row0=0, after=token, name=f"dx_mix_{tag}")
    small = {"mix_norm": dgain, "b_gate": [dba, dbb, dbc], "b_ln_g": dlng, "b_ln_b": dlnb,
             "b_w_s": dws, "b_b_s": dbs, "c_w": dcw, "c_scale": dcs}
    return dx, small, token


PARTS = ("f1", "mix", "f2")


def _forward(xs, tgt, weights_of, pass_on, small, final_gain):
    saved, views = [], []
    cur = xs
    for l in range(DEPTH):
        (wf1,) = weights_of(l, "f1", cur)
        cur, s1 = _ffn_fwd(cur, small[l]["ffn1_norm"], wf1, f"f1l{l}", lambda t: pass_on(l, "mix", t))
        wmix = weights_of(l, "mix", cur)
        cur, s2 = _mixer_fwd(cur, small[l], *wmix, f"l{l}", lambda t: pass_on(l, "f2", t))
        (wf2,) = weights_of(l, "f2", cur)
        cur, s3 = _ffn_fwd(cur, small[l]["ffn2_norm"], wf2, f"f2l{l}",
                           lambda t: pass_on(l + 1, "f1", t) if l + 1 < DEPTH else None)
        saved.append((s1, s2, s3))
        views.append((wf1, wmix, wf2))
    loss_part, dcur, dfinal = _final_loss(cur, final_gain, tgt, "final_loss")
    return loss_part, dcur, dfinal, saved, views


def _backward_layer(dcur, saved_l, small_l, views_l, l, emit, after):
    wf1, wmix, wf2 = views_l
    s1, s2, s3 = saved_l
    dcur, dn2, token = _ffn_bwd(dcur, s3, small_l["ffn2_norm"], wf2, f"f2l{l}", after, lambda g: emit("f2", g))
    dcur, sg, token = _mixer_bwd(dcur, s2, small_l, *wmix, f"l{l}", token, lambda g: emit("mix", g))
    dcur, dn1, token = _ffn_bwd(dcur, s1, small_l["ffn1_norm"], wf1, f"f1l{l}", token, lambda g: emit("f1", g))
    sg["ffn1_norm"] = dn1
    sg["ffn2_norm"] = dn2
    return dcur, sg, token


BIG = ("ffn1_w_gate", "ffn1_w_up", "ffn1_w_down", "w_in", "w_proj_a", "w_proj_b", "w_proj_c", "w_out",
       "ffn2_w_gate", "ffn2_w_up", "ffn2_w_down")
ALL_WEIGHTS = ("ffn1_norm", "ffn1_w_gate", "ffn1_w_up", "ffn1_w_down", "mix_norm", "w_in", "b_gate", "b_ln_g", "b_ln_b",
               "b_w_s", "b_b_s", "c_w", "c_scale", "w_proj_a", "w_proj_b", "w_proj_c", "w_out", "ffn2_norm", "ffn2_w_gate",
               "ffn2_w_up", "ffn2_w_down", "final_norm")


def _place_shards(w, l, part, me, after, name):
    tr = lambda a: a.transpose(0, 2, 1)
    if part == "mix":
        srcs = [tr(w["w_in"]), w["w_out"], tr(w["w_proj_a"]), tr(w["w_proj_b"]), tr(w["w_proj_c"])]
        groups = [[0], [1], [2], [3, 4]]
    else:
        f = "ffn1" if part == "f1" else "ffn2"
        srcs = [tr(w[f + "_w_gate"]), tr(w[f + "_w_up"]), w[f + "_w_down"]]
        groups = [[0, 1, 2]]

    def body(me_ref, *refs):
        del me_ref
        outs = refs[len(srcs) + 1:]
        for o_ref, members in zip(outs, groups):
            for p, i in enumerate(members):
                o_ref[p, 0] = refs[i][0].astype(BF16)

    out_shape = [_sds((len(g), NDEV) + srcs[g[0]].shape[1:], BF16) for g in groups]
    grid_spec = pltpu.PrefetchScalarGridSpec(
        num_scalar_prefetch=1, grid=(1,),
        in_specs=[_bs((1,) + s.shape[1:], lambda i, me_ref: (l, 0, 0)) for s in srcs] + [pl.BlockSpec(memory_space=pl.ANY)],
        out_specs=[_bs((len(g), 1) + srcs[g[0]].shape[1:], lambda i, me_ref: (0, me_ref[0], 0, 0)) for g in groups])
    return pl.pallas_call(body, name=name, grid_spec=grid_spec, out_shape=out_shape,
                          compiler_params=pltpu.CompilerParams(vmem_limit_bytes=40 << 20), interpret=False)(me, *srcs, after)


def _part_views(part, gathered):
    if part == "mix":
        pw, po, pa, pbc = gathered
        return [pw.reshape(INW, D), po.reshape(D, D), pa.reshape(D, GW), pbc.reshape(2 * D, BW)]
    return [gathered[0].reshape(3 * FF, D)]


def _part_pieces(part):
    if part == "mix":
        return [(0, 0, "w_in", True), (1, 0, "w_out", False), (2, 0, "w_proj_a", True), (3, 0, "w_proj_b", True),
                (3, 1, "w_proj_c", True)]
    f = "ffn1" if part == "f1" else "ffn2"
    return [(0, 0, f + "_w_gate", True), (0, 1, f + "_w_up", True), (0, 2, f + "_w_down", False)]


def _layer_small(w, l):
    r = lambda a, shape: a.reshape(shape)
    return {"ffn1_norm": r(w["ffn1_norm"][l], (1, D)), "mix_norm": r(w["mix_norm"][l], (1, D)),
            "b_gate": r(w["b_gate"][l], (1, 3 * D)), "b_ln_g": r(w["b_ln_g"][l], (1, BW)), "b_ln_b": r(w["b_ln_b"][l], (1, BW)),
            "b_w_s": w["b_w_s"][l], "b_b_s": r(w["b_b_s"][l], (4, BLK, 1)), "c_w": w["c_w"][l],
            "c_scale": r(w["c_scale"][l], (4, 1, BLK)), "ffn2_norm": r(w["ffn2_norm"][l], (1, D))}


def kernel(x, ffn1_norm, ffn1_w_gate, ffn1_w_up, ffn1_w_down, mix_norm, w_in, b_gate, b_ln_g, b_ln_b, b_w_s, b_b_s, c_w, c_scale, w_proj_a, w_proj_b, w_proj_c, w_out, ffn2_norm, ffn2_w_gate, ffn2_w_up, ffn2_w_down, final_norm, loss_target, m_ffn1_norm, m_ffn1_w_gate, m_ffn1_w_up, m_ffn1_w_down, m_mix_norm, m_w_in, m_b_gate, m_b_ln_g, m_b_ln_b, m_b_w_s, m_b_b_s, m_c_w, m_c_scale, m_w_proj_a, m_w_proj_b, m_w_proj_c, m_w_out, m_ffn2_norm, m_ffn2_w_gate, m_ffn2_w_up, m_ffn2_w_down, m_final_norm, v_ffn1_norm, v_ffn1_w_gate, v_ffn1_w_up, v_ffn1_w_down, v_mix_norm, v_w_in, v_b_gate, v_b_ln_g, v_b_ln_b, v_b_w_s, v_b_b_s, v_c_w, v_c_scale, v_w_proj_a, v_w_proj_b, v_w_proj_c, v_w_out, v_ffn2_norm, v_ffn2_w_gate, v_ffn2_w_up, v_ffn2_w_down, v_final_norm):
    w = dict(ffn1_norm=ffn1_norm, ffn1_w_gate=ffn1_w_gate, ffn1_w_up=ffn1_w_up, ffn1_w_down=ffn1_w_down, mix_norm=mix_norm,
             w_in=w_in, b_gate=b_gate, b_ln_g=b_ln_g, b_ln_b=b_ln_b, b_w_s=b_w_s, b_b_s=b_b_s, c_w=c_w, c_scale=c_scale,
             w_proj_a=w_proj_a, w_proj_b=w_proj_b, w_proj_c=w_proj_c, w_out=w_out, ffn2_norm=ffn2_norm, ffn2_w_gate=ffn2_w_gate,
             ffn2_w_up=ffn2_w_up, ffn2_w_down=ffn2_w_down, final_norm=final_norm)
    mom = dict(ffn1_norm=m_ffn1_norm, ffn1_w_gate=m_ffn1_w_gate, ffn1_w_up=m_ffn1_w_up, ffn1_w_down=m_ffn1_w_down,
               mix_norm=m_mix_norm, w_in=m_w_in, b_gate=m_b_gate, b_ln_g=m_b_ln_g, b_ln_b=m_b_ln_b, b_w_s=m_b_w_s, b_b_s=m_b_b_s,
               c_w=m_c_w, c_scale=m_c_scale, w_proj_a=m_w_proj_a, w_proj_b=m_w_proj_b, w_proj_c=m_w_proj_c, w_out=m_w_out,
               ffn2_norm=m_ffn2_norm, ffn2_w_gate=m_ffn2_w_gate, ffn2_w_up=m_ffn2_w_up, ffn2_w_down=m_ffn2_w_down,
               final_norm=m_final_norm)
    var = dict(ffn1_norm=v_ffn1_norm, ffn1_w_gate=v_ffn1_w_gate, ffn1_w_up=v_ffn1_w_up, ffn1_w_down=v_ffn1_w_down,
               mix_norm=v_mix_norm, w_in=v_w_in, b_gate=v_b_gate, b_ln_g=v_b_ln_g, b_ln_b=v_b_ln_b, b_w_s=v_b_w_s, b_b_s=v_b_b_s,
               c_w=v_c_w, c_scale=v_c_scale, w_proj_a=v_w_proj_a, w_proj_b=v_w_proj_b, w_proj_c=v_w_proj_c, w_out=v_w_out,
               ffn2_norm=v_ffn2_norm, ffn2_w_gate=v_ffn2_w_gate, ffn2_w_up=v_ffn2_w_up, ffn2_w_down=v_ffn2_w_down,
               final_norm=v_final_norm)

    t_dim = x.shape[1]
    xs = x.reshape(t_dim, D)
    tgt = loss_target.reshape(t_dim, D)

    me = 4 * lax.axis_index("x") + 2 * lax.axis_index("y") + lax.axis_index("c")
    me_arr = me.astype(jnp.int32).reshape(1)
    gathers = {}
    token = jnp.zeros((8, 128), F32)
    for l in range(DEPTH):
        for part in PARTS:
            lands = _place_shards(w, l, part, me_arr, token, f"place_{part}_l{l}")
            gathers[l, part] = _gather_start(lands, token, f"gather_start_{part}_l{l}")
            token = gathers[l, part][-1]

    def pass_on(l, part, after):
        gathers[l, part] = _gather_forward(gathers[l, part], after, f"gather_forward_{part}_l{l}")
        return gathers[l, part][-1]

    pass_on(0, "f1", token)

    def weights_of(l, part, after):
        return _part_views(part, _gather_wait(gathers[l, part], after, f"gather_wait_{part}_l{l}"))

    small = [_layer_small(w, l) for l in range(DEPTH)]
    loss_part, dcur, dfinal, saved, views = _forward(xs, tgt, weights_of, pass_on, small, w["final_norm"].reshape(1, D))
    loss = lax.psum(loss_part[0, 0], ("x", "y", "c"))

    scatters = []
    small_gathers = [None] * DEPTH
    token = None
    for l in reversed(range(DEPTH)):
        def emit(part, grads, l=l):
            scatters.append((l, part, _exchange_start("scatter", grads, dfinal, f"scatter_start_{part}_l{l}")))
            return scatters[-1][2][-1]
        dcur, small_grads, token = _backward_layer(dcur, saved[l], small[l], views[l], l, emit, token)
        packed = _pack_small_grads(small_grads, dfinal if l == DEPTH - 1 else None, me_arr, f"pack_small_grads_l{l}")
        small_gathers[l] = _exchange_start("gather", [packed], token, f"gather_start_small_l{l}")
        token = small_gathers[l][-1]
    grad_x = dcur.reshape(x.shape)

    done = {}
    behind = token
    for l, part, started in scatters:
        arrs = _exchange_wait("scatter", started, behind, f"scatter_wait_{part}_l{l}")
        na = len(arrs) // 2
        for a, piece, k, transposed in _part_pieces(part):
            flip = transposed and w[k].shape[2] % 128 != 0
            view = (lambda t: t.transpose(0, 2, 1)) if flip else (lambda t: t)
            done[k] = _reduce_update(arrs[na + a], arrs[a], me_arr, piece, view(w[k]), view(mom[k]), view(var[k]), l,
                                     done.get(k), transposed and not flip, f"update_{k}_l{l}")
            behind = done[k][1]
    unflip = lambda k, t: t.transpose(0, 2, 1) if t.shape != w[k].shape else t
    grads, delta, new_m, new_v = ({k: unflip(k, done[k][t]) for k in BIG} for t in range(4))
    gathered = [None] * DEPTH
    for l in reversed(range(DEPTH)):
        gathered[l] = _exchange_wait("gather", small_gathers[l], behind, f"gather_wait_small_l{l}")[0]
        behind = gathered[l]
    as_row = lambda t: dict(t, final_norm=t["final_norm"].reshape(1, D))
    for dst, src in zip((grads, delta, new_m, new_v), _adamw_small(gathered, as_row(w), as_row(mom), as_row(var), "adamw_small")):
        dst.update(src)
        dst["final_norm"] = src["final_norm"].reshape(D)

    return (loss, grad_x, *[grads[k] for k in ALL_WEIGHTS], *[delta[k] for k in ALL_WEIGHTS],
            *[new_m[k] for k in ALL_WEIGHTS], *[new_v[k] for k in ALL_WEIGHTS])
```

```python
import math

import jax
import jax.numpy as jnp
from jax import lax
from jax.experimental import pallas as pl
from jax.experimental.pallas import tpu as pltpu

F32 = jnp.float32
BF16 = jnp.bfloat16
MESH = pl.DeviceIdType.MESH

NDEV = 8
DEPTH = 2
D = 1024
FF = 2816
FFS = FF // NDEV
INW = 6912
INS = INW // NDEV
DS = D // NDEV
BLK = 128
NH = 4
HD = 64
GW = NH * HD
DILS = (1, 4, 16)
QK_SCALE = 1.0 / math.sqrt(HD)
ZB = INW // GW
Q_B, K_B, V_B = 0, 3, 6
UV_B = 9
C_B128 = 26
G_B = 15
BW = 512
EPS = 1e-6
NEG = -1e30
VMEM_CAP = 60 * 1024 * 1024

ADAM_LR, ADAM_B1, ADAM_B2, ADAM_EPS, ADAM_WD, ADAM_STEP = 0.001, 0.9, 0.999, 1e-08, 0.01, 10

NT = (((1,), (1,)), ((), ()))
TN = (((0,), (0,)), ((), ()))


def _bs(shape, imap):
    return pl.BlockSpec(shape, imap)


def _pc(body, *, name, grid, in_specs, out_specs, out_shape, scratch=(), alias=None, vmem=None):
    params = {}
    if vmem is not None:
        params["vmem_limit_bytes"] = min(int(vmem), VMEM_CAP)
    big = lambda s: math.prod(s.shape) * jnp.dtype(s.dtype).itemsize >= (1 << 20)
    pin = lambda s: pltpu.HBM(s.shape, s.dtype) if big(s) else s
    out_shape = [pin(s) for s in out_shape] if isinstance(out_shape, (list, tuple)) else pin(out_shape)
    call = pl.pallas_call(
        body, name=name, grid=grid, in_specs=in_specs, out_specs=out_specs, out_shape=out_shape,
        scratch_shapes=list(scratch), input_output_aliases=alias or {},
        compiler_params=pltpu.CompilerParams(**params), interpret=False)
    return lambda *args: call(*[pltpu.with_memory_space_constraint(a, pltpu.HBM) if big(a) else a for a in args])


def _sds(shape, dtype):
    return jax.ShapeDtypeStruct(shape, dtype)


def _row_tile(rows, align, cap):
    best = None
    for cand in range(align, min(rows, cap) + 1, align):
        if rows % cand == 0:
            best = cand
    assert best is not None, (rows, align, cap)
    return best


def _mm(a, b, *, mode, tm, tn, tk, n, out_dtype, name, b_row0=0, scale=None, res=None, after=None, norm_gain=None):
    m, k_dim = a.shape
    nk = k_dim // tk
    assert m % tm == 0 and n % tn == 0 and k_dim % tk == 0 and (norm_gain is None or tn == n)
    b_off, rem = divmod(b_row0, tk if mode == "nn" else tn)
    assert rem == 0
    grid = (m // tm, n // tn, nk)
    in_specs = [_bs((tm, tk), lambda i, j, k: (i, k))]
    if mode == "nn":
        in_specs.append(_bs((tk, tn), lambda i, j, k: (b_off + k, j)))
    else:
        in_specs.append(_bs((tn, tk), lambda i, j, k: (b_off + j, k)))
    args = [a, b]
    if res is not None:
        in_specs.append(_bs((tm, tn), lambda i, j, k: (i, j)))
        args.append(res)
    if norm_gain is not None:
        in_specs.append(_bs((1, tn), lambda i, j, k: (0, 0)))
        args.append(norm_gain)
    if after is not None:
        in_specs.append(pl.BlockSpec(memory_space=pl.ANY))
        args.append(after)
    n_in = len(args)

    def body(*refs):
        a_ref, b_ref = refs[0], refs[1]
        r_ref = refs[2] if res is not None else None
        g_ref = refs[2 + (res is not None)] if norm_gain is not None else None
        o_ref = refs[n_in]
        av = a_ref[...].astype(BF16)
        bv = b_ref[...].astype(BF16)
        if mode == "nn":
            p = jnp.dot(av, bv, preferred_element_type=F32)
        else:
            p = lax.dot_general(av, bv, NT, preferred_element_type=F32)

        def fin(v):
            if scale is not None:
                v = v * scale
            if r_ref is not None:
                v = r_ref[...] + v
            o_ref[...] = v.astype(out_dtype)
            if g_ref is not None:
                r = lax.rsqrt(jnp.mean(v * v, axis=-1, keepdims=True) + EPS)
                refs[n_in + 1][...] = (v * r * g_ref[...]).astype(BF16)

        if nk == 1:
            fin(p)
        else:
            acc_ref = refs[-1]
            kk = pl.program_id(2)

            @pl.when(kk == 0)
            def _():
                acc_ref[...] = p

            @pl.when(kk > 0)
            def _():
                acc_ref[...] += p

            @pl.when(kk == nk - 1)
            def _():
                fin(acc_ref[...])

    scratch = [pltpu.VMEM((tm, tn), F32)] if nk > 1 else []
    est = 2 * (tm * tk * a.dtype.itemsize + tk * tn * b.dtype.itemsize + tm * tn * jnp.dtype(out_dtype).itemsize)
    est += (2 + (nk > 1)) * tm * tn * 4 + (2 * tm * tn * 4 if res is not None else 0)
    est += (tm * tk * 2 if a.dtype != BF16 else 0) + (tk * tn * 2 if b.dtype != BF16 else 0)
    tile = _bs((tm, tn), lambda i, j, k: (i, j))
    if norm_gain is None:
        return _pc(body, name=name, grid=grid, in_specs=in_specs, out_specs=tile, out_shape=_sds((m, n), out_dtype),
                   scratch=scratch, vmem=est + (8 << 20))(*args)
    return _pc(body, name=name, grid=grid, in_specs=in_specs, out_specs=[tile, tile],
               out_shape=[_sds((m, n), out_dtype), _sds((m, n), BF16)], scratch=scratch,
               vmem=est + 3 * tm * tn * 4 + (8 << 20))(*args)


def _wgrad(a, b, buf, *, piece0, rows, tm, tn, tk, name, scale=None):
    t_dim, m = a.shape
    n = b.shape[1]
    db = tm // rows
    tpp = NDEV // db
    nk = t_dim // tk
    assert tm % rows == 0 and NDEV % db == 0 and m % tm == 0 and n % tn == 0 and t_dim % tk == 0
    grid = (m // tm, n // tn, nk)

    def body(a_ref, b_ref, buf_ref, o_ref, acc_ref):
        del buf_ref
        kk = pl.program_id(2)
        p = lax.dot_general(a_ref[...].astype(BF16), b_ref[...].astype(BF16), TN, preferred_element_type=F32)

        @pl.when(kk == 0)
        def _():
            acc_ref[...] = p

        @pl.when(kk > 0)
        def _():
            acc_ref[...] += p

        @pl.when(kk == nk - 1)
        def _():
            for d in range(db):
                v = acc_ref[d * rows:(d + 1) * rows, :]
                if scale is not None:
                    v = v * scale
                o_ref[0, d] = v.astype(BF16)

    est = 2 * (tk * tm * a.dtype.itemsize + tk * tn * b.dtype.itemsize + tm * tn * 2) + 3 * tm * tn * 4 + tk * tm * 4
    return _pc(
        body, name=name, grid=grid,
        in_specs=[_bs((tk, tm), lambda i, j, k: (k, i)), _bs((tk, tn), lambda i, j, k: (k, j)),
                  pl.BlockSpec(memory_space=pl.ANY)],
        out_specs=_bs((1, db, rows, tn), lambda i, j, k: (piece0 + i // tpp, i % tpp, 0, j)),
        out_shape=_sds(buf.shape, buf.dtype), scratch=[pltpu.VMEM((tm, tn), F32)], alias={2: 0},
        vmem=est + (8 << 20))(a, b, buf)


def _mm_drms(a, b, x, gain, dres, *, tm, b_row0, after, name):
    m, k_dim = a.shape
    assert m % tm == 0 and b_row0 % k_dim == 0 and b.shape[1] == D
    k_blk = b_row0 // k_dim
    after = jnp.zeros((8, 128), F32) if after is None else after

    def body(a_ref, b_ref, x_ref, g_ref, dr_ref, after_ref, dx_ref, dg_ref):
        del after_ref
        dh_v = jnp.dot(a_ref[...].astype(BF16), b_ref[...].astype(BF16), preferred_element_type=F32)
        xv = x_ref[...]
        r = lax.rsqrt(jnp.mean(xv * xv, axis=-1, keepdims=True) + EPS)
        dyg = dh_v * g_ref[...]
        mq = jnp.mean(dyg * xv, axis=-1, keepdims=True)
        dx_ref[...] = dr_ref[...] + r * (dyg - xv * (r * r * mq))

        @pl.when(pl.program_id(0) == 0)
        def _():
            dg_ref[...] = jnp.zeros_like(dg_ref)

        dg_ref[...] += jnp.sum(dh_v * (xv * r), axis=0, keepdims=True)

    row = _bs((tm, D), lambda i: (i, 0))
    est = 2 * (tm * k_dim * a.dtype.itemsize + k_dim * D * b.dtype.itemsize + 3 * tm * D * 4) + 4 * tm * D * 4
    return _pc(body, name=name, grid=(m // tm,),
               in_specs=[_bs((tm, k_dim), lambda i: (i, 0)), _bs((k_dim, D), lambda i: (k_blk, 0)), row,
                         _bs((1, D), lambda i: (0, 0)), row, pl.BlockSpec(memory_space=pl.ANY)],
               out_specs=[row, _bs((1, D), lambda i: (0, 0))], out_shape=[_sds((m, D), F32), _sds((1, D), F32)],
               vmem=est + (8 << 20))(a, b, x, gain, dres, after)


def _dact_dswiglu(dout, b, g_pre, u_pre, *, tm, b_row0, after, name):
    t_dim = dout.shape[0]
    assert t_dim % tm == 0 and b_row0 % FF == 0
    k_blk = b_row0 // FF
    after = jnp.zeros((8, 128), F32) if after is None else after

    def body(d_ref, b_ref, g_ref, u_ref, after_ref, o_ref):
        del after_ref
        dav = 0.5 * lax.dot_general(d_ref[...].astype(BF16), b_ref[...].astype(BF16), NT, preferred_element_type=F32)
        g = g_ref[...].astype(F32)
        u = u_ref[...].astype(F32)
        s = _sigmoid(g)
        o_ref[:, :FF] = (dav * u * (s * (1.0 + g * (1.0 - s)))).astype(BF16)
        o_ref[:, FF:] = (dav * (g * s)).astype(BF16)

    est = 2 * (tm * D * 4 + FF * D * 2 + 2 * tm * 2 * FF * 2) + 5 * tm * FF * 4
    return _pc(body, name=name, grid=(t_dim // tm,),
               in_specs=[_bs((tm, D), lambda i: (i, 0)), _bs((FF, D), lambda i: (k_blk, 0)),
                         _bs((tm, FF), lambda i: (i, 0)), _bs((tm, FF), lambda i: (i, 0)), pl.BlockSpec(memory_space=pl.ANY)],
               out_specs=_bs((tm, 2 * FF), lambda i: (i, 0)), out_shape=_sds((t_dim, 2 * FF), BF16),
               vmem=est + (8 << 20))(dout, b, g_pre, u_pre, after)


def _rms_fwd(x, gain, name, tm=512):
    t_dim = x.shape[0]

    def body(x_ref, g_ref, o_ref):
        xv = x_ref[...]
        r = lax.rsqrt(jnp.mean(xv * xv, axis=-1, keepdims=True) + EPS)
        o_ref[...] = (xv * r * g_ref[...]).astype(BF16)

    return _pc(body, name=name, grid=(t_dim // tm,),
               in_specs=[_bs((tm, D), lambda i: (i, 0)), _bs((1, D), lambda i: (0, 0))],
               out_specs=_bs((tm, D), lambda i: (i, 0)), out_shape=_sds((t_dim, D), BF16))(x, gain)


def _sigmoid(v):
    return 1.0 / (1.0 + jnp.exp(-v))


def _gu_swiglu(h, b, *, tm, tn, b_row0, name):
    t_dim = h.shape[0]
    assert t_dim % tm == 0 and FF % tn == 0 and b_row0 % tn == 0
    j0 = b_row0 // tn

    def body(h_ref, bg_ref, bu_ref, g_ref, u_ref, a_ref):
        hv = h_ref[...]
        g16 = lax.dot_general(hv, bg_ref[...], NT, preferred_element_type=F32).astype(BF16)
        u16 = lax.dot_general(hv, bu_ref[...], NT, preferred_element_type=F32).astype(BF16)
        g = g16.astype(F32)
        g_ref[...] = g16
        u_ref[...] = u16
        a_ref[...] = (g * _sigmoid(g) * u16.astype(F32)).astype(BF16)

    tile = _bs((tm, tn), lambda i, j: (i, j))
    est = 2 * (tm * D * 2 + 2 * tn * D * 2 + 3 * tm * tn * 2) + 5 * tm * tn * 4
    return _pc(body, name=name, grid=(t_dim // tm, FF // tn),
               in_specs=[_bs((tm, D), lambda i, j: (i, 0)), _bs((tn, D), lambda i, j: (j0 + j, 0)),
                         _bs((tn, D), lambda i, j: (j0 + FF // tn + j, 0))],
               out_specs=[tile] * 3, out_shape=[_sds((t_dim, FF), BF16)] * 3, vmem=est + (8 << 20))(h, b, b)


def _final_loss(x, gain, tgt, name, tm=512):
    t_dim = x.shape[0]

    def body(x_ref, g_ref, t_ref, loss_ref, dx_ref, dg_ref):
        xv = x_ref[...]
        gv = g_ref[...]
        r = lax.rsqrt(jnp.mean(xv * xv, axis=-1, keepdims=True) + EPS)
        xn = xv * r
        err = xn * gv - t_ref[...]
        dy = err * (1.0 / D)
        dyg = dy * gv
        mq = jnp.mean(dyg * xv, axis=-1, keepdims=True)
        dx_ref[...] = r * (dyg - xv * (r * r * mq))

        @pl.when(pl.program_id(0) == 0)
        def _():
            dg_ref[...] = jnp.zeros_like(dg_ref)
            loss_ref[...] = jnp.zeros_like(loss_ref)

        dg_ref[...] += jnp.sum(dy * xn, axis=0, keepdims=True)
        part = 0.5 * jnp.sum(jnp.mean(err * err, axis=-1, keepdims=True), axis=0, keepdims=True)
        loss_ref[...] += jnp.broadcast_to(part, loss_ref.shape)

    return _pc(body, name=name, grid=(t_dim // tm,),
               in_specs=[_bs((tm, D), lambda i: (i, 0)), _bs((1, D), lambda i: (0, 0)), _bs((tm, D), lambda i: (i, 0))],
               out_specs=[_bs((8, 128), lambda i: (0, 0)), _bs((tm, D), lambda i: (i, 0)), _bs((1, D), lambda i: (0, 0))],
               out_shape=[_sds((8, 128), F32), _sds((t_dim, D), F32), _sds((1, D), F32)])(x, gain, tgt)


def _band_mask(n):
    ri = lax.broadcasted_iota(jnp.int32, (BLK, 2 * BLK), 0)
    cj = lax.broadcasted_iota(jnp.int32, (BLK, 2 * BLK), 1)
    dist = ri + BLK - cj
    return (dist >= 0) & (dist <= BLK) & ((cj >= BLK) | (n > 0))


def _halves(unit, imap_rows, col):
    return [_bs((unit, 128), (lambda u, j=j: (imap_rows(u), 2 * col + j))) for j in range(2)]


def _get_rows(pair, r, dil):
    return jnp.concatenate([h[pl.ds(r, BLK, stride=dil), :] for h in pair], axis=1)


def _put_rows(stage, r, dil, val):
    for j in range(2):
        stage[j, pl.ds(r, BLK, stride=dil), :] = val[:, 128 * j:128 * (j + 1)]


def _unstage(stage, out_ref):
    out_ref[:, :128] = stage[0]
    out_ref[:, 128:] = stage[1]


def _attn_fwd(z, acc, ml, *, gi, first, final, name):
    dil = DILS[gi]
    t_dim = z.shape[0]
    unit = BLK * dil
    nu = t_dim // unit
    own = lambda u: u
    prev = lambda u: jnp.maximum(u - 1, 0)
    in_specs = (_halves(unit, own, Q_B + gi) + _halves(unit, own, K_B + gi) + _halves(unit, prev, K_B + gi)
                + _halves(unit, own, V_B + gi) + _halves(unit, prev, V_B + gi))
    args = [z] * 10
    st_ml = _bs((unit, 128), lambda u: (u, 0))
    if not first:
        in_specs += _halves(unit, own, 0) + [st_ml]
        args += [acc, acc, ml]

    def body(*refs):
        q2, ko2, kp2, vo2, vp2 = (refs[2 * i:2 * i + 2] for i in range(5))
        o_acc_ref, o_ml_ref, stage = refs[-3], refs[-2], refs[-1]
        valid = _band_mask(pl.program_id(0))
        lane = lax.broadcasted_iota(jnp.int32, (BLK, GW), 1)
        lane_s = lax.broadcasted_iota(jnp.int32, (BLK, 128), 1)

        def band(r, carry):
            q = _get_rows(q2, r, dil) * QK_SCALE
            kb = jnp.concatenate([_get_rows(kp2, r, dil), _get_rows(ko2, r, dil)], axis=0).astype(BF16)
            vb = jnp.concatenate([_get_rows(vp2, r, dil), _get_rows(vo2, r, dil)], axis=0).astype(BF16)
            if first:
                acc_v = jnp.zeros((BLK, GW), F32)
                ml_v = jnp.where(lane_s < 64, NEG, 0.0).astype(F32)
            else:
                acc_v = _get_rows(refs[10:12], r, dil)
                ml_v = refs[12][pl.ds(r, BLK, stride=dil), :]
            heads = [(lane >= h * HD) & (lane < (h + 1) * HD) for h in range(NH)]
            rows = lambda v, h: v[h * BLK:(h + 1) * BLK]
            qs = jnp.concatenate([jnp.where(hm, q, 0.0) for hm in heads], axis=0).astype(BF16)
            s = lax.dot_general(qs, kb, NT, preferred_element_type=F32)
            s = jnp.where(jnp.concatenate([valid] * NH, axis=0), s, NEG)
            m_old = jnp.concatenate([ml_v[:, 16 * h:16 * h + 1] for h in range(NH)], axis=0)
            l_old = jnp.concatenate([ml_v[:, 64 + 16 * h:64 + 16 * h + 1] for h in range(NH)], axis=0)
            m_new = jnp.maximum(m_old, jnp.max(s, axis=-1, keepdims=True))
            alpha = jnp.exp(m_old - m_new)
            p = jnp.exp(s - m_new)
            l_new = alpha * l_old + jnp.sum(p, axis=-1, keepdims=True)
            pv = jnp.dot(p.astype(BF16), vb, preferred_element_type=F32)
            stat = m_new + jnp.log(l_new) if final else m_new
            acc_o, ml_o = acc_v, ml_v
            l_b = jnp.ones((BLK, GW), F32)
            for h, hm in enumerate(heads):
                acc_o = jnp.where(hm, acc_v * rows(alpha, h) + rows(pv, h), acc_o)
                l_b = jnp.where(hm, rows(l_new, h), l_b)
                ml_o = jnp.where((lane_s >= 16 * h) & (lane_s < 16 * h + 16), rows(stat, h), ml_o)
                if not final:
                    ml_o = jnp.where((lane_s >= 64 + 16 * h) & (lane_s < 64 + 16 * h + 16), rows(l_new, h), ml_o)
            _put_rows(stage, r, dil, acc_o / l_b if final else acc_o)
            o_ml_ref[pl.ds(r, BLK, stride=dil), :] = ml_o
            return carry

        lax.fori_loop(0, dil, band, 0, unroll=min(dil, 4))
        _unstage(stage, o_acc_ref)

    blocks = (len(in_specs) + 6) * unit * 128 * 4
    return _pc(body, name=name, grid=(nu,), in_specs=in_specs, out_specs=[_bs((unit, GW), lambda u: (u, 0)), st_ml],
               out_shape=[_sds((t_dim, GW), F32), _sds((t_dim, 128), F32)],
               scratch=[pltpu.VMEM((2, unit, 128), F32)], vmem=2 * blocks + (12 << 20))(*args)


def _attn_bwd(z, dy, y, lse, *, gi, name):
    dil = DILS[gi]
    t_dim = z.shape[0]
    unit = BLK * dil
    nu = t_dim // unit

    own = lambda u: jnp.minimum(u, nu - 1)
    prev = lambda u: jnp.maximum(jnp.minimum(u, nu - 1) - 1, 0)
    in_specs = (_halves(unit, own, Q_B + gi) + _halves(unit, own, K_B + gi) + _halves(unit, prev, K_B + gi)
                + _halves(unit, own, V_B + gi) + _halves(unit, prev, V_B + gi)
                + _halves(unit, own, 0) + _halves(unit, own, 0) + [_bs((unit, 128), lambda u: (own(u), 0))])
    tok = _bs((unit, GW), lambda u: (own(u), 0))
    kv_out = _bs((unit, GW), lambda u: (jnp.maximum(u - 1, 0), 0))

    def body(*refs):
        q2, ko2, kp2, vo2, vp2, dy2, y2 = (refs[2 * i:2 * i + 2] for i in range(7))
        lse_ref = refs[14]
        dq_ref, dk_ref, dv_ref, dq_st, dk_st, dv_st, dkc, dvc = refs[15:]
        u = pl.program_id(0)

        @pl.when(u == 0)
        def _():
            dkc[...] = jnp.zeros_like(dkc)
            dvc[...] = jnp.zeros_like(dvc)

        @pl.when(u < nu)
        def _():
            valid = _band_mask(u)
            lane = lax.broadcasted_iota(jnp.int32, (BLK, GW), 1)

            def band(r, carry):
                q = _get_rows(q2, r, dil) * QK_SCALE
                kb = jnp.concatenate([_get_rows(kp2, r, dil), _get_rows(ko2, r, dil)], axis=0).astype(BF16)
                vb = jnp.concatenate([_get_rows(vp2, r, dil), _get_rows(vo2, r, dil)], axis=0).astype(BF16)
                dyv = _get_rows(dy2, r, dil)
                yv = _get_rows(y2, r, dil)
                lse_v = lse_ref[pl.ds(r, BLK, stride=dil), :]
                heads = [(lane >= h * HD) & (lane < (h + 1) * HD) for h in range(NH)]
                qs = jnp.concatenate([jnp.where(hm, q, 0.0) for hm in heads], axis=0).astype(BF16)
                dys = jnp.concatenate([jnp.where(hm, dyv, 0.0) for hm in heads], axis=0)
                lse_s = jnp.concatenate([lse_v[:, 16 * h:16 * h + 1] for h in range(NH)], axis=0)
                s = lax.dot_general(qs, kb, NT, preferred_element_type=F32)
                p = jnp.where(jnp.concatenate([valid] * NH, axis=0), jnp.exp(s - lse_s), 0.0)
                delta = jnp.sum(dys * jnp.concatenate([yv] * NH, axis=0), axis=-1, keepdims=True)
                dys16 = dys.astype(BF16)
                dp = lax.dot_general(dys16, vb, NT, preferred_element_type=F32)
                ds = (p * (dp - delta)).astype(BF16)
                dq_all = jnp.dot(ds, kb, preferred_element_type=F32)
                dq = jnp.zeros((BLK, GW), F32)
                for h, hm in enumerate(heads):
                    dq = jnp.where(hm, dq_all[h * BLK:(h + 1) * BLK], dq)
                dkb = lax.dot_general(ds, qs, TN, preferred_element_type=F32)
                dvb = lax.dot_general(p.astype(BF16), dys16, TN, preferred_element_type=F32)
                _put_rows(dq_st, r, dil, dq * QK_SCALE)
                _put_rows(dk_st, r, dil, _get_rows((dkc.at[0], dkc.at[1]), r, dil) + dkb[:BLK])
                _put_rows(dv_st, r, dil, _get_rows((dvc.at[0], dvc.at[1]), r, dil) + dvb[:BLK])
                _put_rows(dkc, r, dil, dkb[BLK:])
                _put_rows(dvc, r, dil, dvb[BLK:])
                return carry

            lax.fori_loop(0, dil, band, 0, unroll=min(dil, 4))
            _unstage(dq_st, dq_ref)

            @pl.when(u > 0)
            def _():
                _unstage(dk_st, dk_ref)
                _unstage(dv_st, dv_ref)

        @pl.when(u == nu)
        def _():
            _unstage(dkc, dk_ref)
            _unstage(dvc, dv_ref)

    blocks = (15 + 6 + 5) * unit * 128 * 4 * 2
    return _pc(body, name=name, grid=(nu + 1,), in_specs=in_specs, out_specs=[tok, kv_out, kv_out],
               out_shape=[_sds((t_dim, GW), F32)] * 3, scratch=[pltpu.VMEM((2, unit, 128), F32)] * 5,
               vmem=blocks + (12 << 20))(*([z] * 10), dy, dy, y, y, lse)


_INV_SQRT2 = 1.0 / math.sqrt(2.0)
_INV_SQRT2PI = 1.0 / math.sqrt(2.0 * math.pi)


def _gelu(v):
    return 0.5 * v * (1.0 + lax.erf(v * _INV_SQRT2))


def _gelu_grad(v):
    return 0.5 * (1.0 + lax.erf(v * _INV_SQRT2)) + v * (_INV_SQRT2PI * jnp.exp(-0.5 * v * v))


def _gmlp_core(z_refs, lng, lnb, ws_ref, bs_ref):
    u_pre = jnp.concatenate([z_refs[0][...], z_refs[1][...]], axis=1)
    v_pre = jnp.concatenate([z_refs[2][...], z_refs[3][...]], axis=1)
    u = _gelu(u_pre)
    v = _gelu(v_pre)
    xc = v - jnp.mean(v, axis=-1, keepdims=True)
    rstd = lax.rsqrt(jnp.mean(xc * xc, axis=-1, keepdims=True) + EPS)
    xhat = xc * rstd
    vn = xhat * lng + lnb
    ti = lax.broadcasted_iota(jnp.int32, (BLK, BLK), 0)
    si = lax.broadcasted_iota(jnp.int32, (BLK, BLK), 1)
    causal = ti >= si
    vn16 = vn.astype(BF16)
    ws = [jnp.where(causal, ws_ref[g], 0.0).astype(BF16) for g in range(4)]
    mixed = jnp.concatenate(
        [jnp.dot(ws[g], vn16[:, g * BLK:(g + 1) * BLK], preferred_element_type=F32) + bs_ref[g] for g in range(4)], axis=1)
    return u_pre, v_pre, u, xhat, rstd, vn16, ws, causal, mixed


def _gmlp_specs():
    z_specs = [_bs((BLK, GW), (lambda i, c=c: (i, UV_B + c))) for c in range(4)]
    par_specs = [_bs((1, BW), lambda i: (0, 0)), _bs((1, BW), lambda i: (0, 0)),
                 _bs((4, BLK, BLK), lambda i: (0, 0, 0)), _bs((4, BLK, 1), lambda i: (0, 0, 0))]
    return z_specs, par_specs


def _gmlp_fwd(z, lng, lnb, ws, bs, name):
    t_dim = z.shape[0]
    z_specs, par_specs = _gmlp_specs()

    def body(z0, z1, z2, z3, lng_ref, lnb_ref, ws_ref, bs_ref, o_ref):
        core = _gmlp_core((z0, z1, z2, z3), lng_ref[...], lnb_ref[...], ws_ref, bs_ref)
        o_ref[...] = (core[2] * core[8]).astype(BF16)

    return _pc(body, name=name, grid=(t_dim // BLK,), in_specs=z_specs + par_specs,
               out_specs=_bs((BLK, BW), lambda i: (i, 0)), out_shape=_sds((t_dim, BW), BF16))(z, z, z, z, lng, lnb, ws, bs)


def _gmlp_bwd(z, dy, lng, lnb, ws, bs, name):
    t_dim = z.shape[0]
    z_specs, par_specs = _gmlp_specs()

    def body(z0, z1, z2, z3, lng_ref, lnb_ref, ws_ref, bs_ref, dy_ref, duv_ref, dws_ref, dbs_ref, dlng_ref, dlnb_ref):
        lng_v = lng_ref[...]
        u_pre, v_pre, u, xhat, rstd, vn16, wsm, causal, mixed = _gmlp_core((z0, z1, z2, z3), lng_v, lnb_ref[...], ws_ref, bs_ref)
        dyv = dy_ref[...].astype(F32)
        du = dyv * mixed
        dmixed = dyv * u

        @pl.when(pl.program_id(0) == 0)
        def _():
            dws_ref[...] = jnp.zeros_like(dws_ref)
            dbs_ref[...] = jnp.zeros_like(dbs_ref)
            dlng_ref[...] = jnp.zeros_like(dlng_ref)
            dlnb_ref[...] = jnp.zeros_like(dlnb_ref)

        dvn_parts = []
        for g in range(4):
            dm = dmixed[:, g * BLK:(g + 1) * BLK]
            dm16 = dm.astype(BF16)
            dw = lax.dot_general(dm16, vn16[:, g * BLK:(g + 1) * BLK], NT, preferred_element_type=F32)
            dws_ref[g] += jnp.where(causal, dw, 0.0)
            dbs_ref[g] += jnp.sum(dm.T, axis=0, keepdims=True)
            dvn_parts.append(lax.dot_general(wsm[g], dm16, TN, preferred_element_type=F32))
        dvn = jnp.concatenate(dvn_parts, axis=1)
        dlng_ref[...] += jnp.sum(dvn * xhat, axis=0, keepdims=True)
        dlnb_ref[...] += jnp.sum(dvn, axis=0, keepdims=True)
        dxh = dvn * lng_v
        dv = rstd * (dxh - jnp.mean(dxh, axis=-1, keepdims=True) - xhat * jnp.mean(dxh * xhat, axis=-1, keepdims=True))
        duv_ref[:, :BW] = (du * _gelu_grad(u_pre)).astype(BF16)
        duv_ref[:, BW:] = (dv * _gelu_grad(v_pre)).astype(BF16)

    return _pc(body, name=name, grid=(t_dim // BLK,),
               in_specs=z_specs + par_specs + [_bs((BLK, BW), lambda i: (i, 0))],
               out_specs=[_bs((BLK, 2 * BW), lambda i: (i, 0)), _bs((4, BLK, BLK), lambda i: (0, 0, 0)),
                          _bs((4, 1, BLK), lambda i: (0, 0, 0)), _bs((1, BW), lambda i: (0, 0)), _bs((1, BW), lambda i: (0, 0))],
               out_shape=[_sds((t_dim, 2 * BW), BF16), _sds((4, BLK, BLK), F32), _sds((4, 1, BLK), F32),
                          _sds((1, BW), F32), _sds((1, BW), F32)])(z, z, z, z, lng, lnb, ws, bs, dy)


def _pool_core(x, gi):
    t_dim = x.shape[0]
    w = jnp.left_shift(2, gi)
    row = lax.broadcasted_iota(jnp.int32, x.shape, 0)
    s = x
    for k in (1, 2, 4, 8):
        sh = jnp.where(row >= k, pltpu.roll(s, k, 0), 0.0)
        s = jnp.where(k < w, s + sh, s)
    cnt = jnp.minimum(row + 1, w).astype(F32)
    return s / cnt - x, cnt, row, w, t_dim


def _pool_fwd(z, cw, cs, name):
    t_dim = z.shape[0]

    def body(z_ref, cw_ref, cs_ref, o_ref):
        pooled = _pool_core(z_ref[...], pl.program_id(0))[0]
        y = jnp.dot(pooled.astype(BF16), cw_ref[0].astype(BF16), preferred_element_type=F32) * cs_ref[0]
        o_ref[...] = y.astype(BF16)

    return _pc(body, name=name, grid=(4,),
               in_specs=[_bs((t_dim, BLK), lambda g: (0, C_B128 + g)), _bs((1, BLK, BLK), lambda g: (g, 0, 0)),
                         _bs((1, 1, BLK), lambda g: (g, 0, 0))],
               out_specs=_bs((t_dim, BLK), lambda g: (0, g)), out_shape=_sds((t_dim, BW), BF16),
               vmem=40 << 20)(z, cw, cs)


def _pool_bwd(z, dy, cw, cs, name):
    t_dim = z.shape[0]

    def body(z_ref, dy_ref, cw_ref, cs_ref, dx_ref, dcw_ref, dcs_ref):
        pooled, cnt, row, w, _ = _pool_core(z_ref[...], pl.program_id(0))
        p16 = pooled.astype(BF16)
        cw16 = cw_ref[0].astype(BF16)
        dyv = dy_ref[...].astype(F32)
        lin = jnp.dot(p16, cw16, preferred_element_type=F32)
        dcs_ref[0] = jnp.sum(dyv * lin, axis=0, keepdims=True)
        dys = (dyv * cs_ref[0]).astype(BF16)
        dcw_ref[0] = lax.dot_general(p16, dys, TN, preferred_element_type=F32)
        dpool = lax.dot_general(dys, cw16, NT, preferred_element_type=F32)
        sb = dpool / cnt
        for k in (1, 2, 4, 8):
            sh = jnp.where(row < t_dim - k, pltpu.roll(sb, t_dim - k, 0), 0.0)
            sb = jnp.where(k < w, sb + sh, sb)
        dx_ref[...] = (sb - dpool).astype(BF16)

    return _pc(body, name=name, grid=(4,),
               in_specs=[_bs((t_dim, BLK), lambda g: (0, C_B128 + g)), _bs((t_dim, BLK), lambda g: (0, g)),
                         _bs((1, BLK, BLK), lambda g: (g, 0, 0)), _bs((1, 1, BLK), lambda g: (g, 0, 0))],
               out_specs=[_bs((t_dim, BLK), lambda g: (0, g)), _bs((1, BLK, BLK), lambda g: (g, 0, 0)),
                          _bs((1, 1, BLK), lambda g: (g, 0, 0))],
               out_shape=[_sds((t_dim, BW), BF16), _sds((4, BLK, BLK), F32), _sds((4, 1, BLK), F32)],
               vmem=48 << 20)(z, dy, cw, cs)


def _gates_specs(tm):
    z_specs = [_bs((tm, GW), (lambda j, i, b=b: (i, G_B + 4 * b + j))) for b in range(3)]
    bg_specs = [_bs((1, GW), (lambda j, i, b=b: (0, 4 * b + j))) for b in range(3)]
    tile = _bs((tm, GW), lambda j, i: (i, j))
    return z_specs, bg_specs, tile


def _gates_fwd(z, bg, pa, pb, pc, name, tm=512):
    t_dim = z.shape[0]
    z_specs, bg_specs, tile = _gates_specs(tm)

    def body(za, zb, zc, ba, bb, bc, pa_ref, pb_ref, pc_ref, o_ref):
        o_ref[...] = (_sigmoid(za[...] + ba[...]) * pa_ref[...] + _sigmoid(zb[...] + bb[...]) * pb_ref[...]
                      + _sigmoid(zc[...] + bc[...]) * pc_ref[...]).astype(BF16)

    return _pc(body, name=name, grid=(D // GW, t_dim // tm), in_specs=z_specs + bg_specs + [tile] * 3,
               out_specs=tile, out_shape=_sds((t_dim, D), BF16))(z, z, z, bg, bg, bg, pa, pb, pc)


def _gates_bwd(z, bg, pa, pb, pc, dm, name, tm=512):
    t_dim = z.shape[0]
    z_specs, bg_specs, tile = _gates_specs(tm)
    bias_out = _bs((1, GW), lambda j, i: (0, j))

    def body(za, zb, zc, ba, bb, bc, pa_ref, pb_ref, pc_ref, dm_ref, dpa, dpb, dpc, dza, dzb, dzc, dba, dbb, dbc):
        dmv = dm_ref[...]
        first = pl.program_id(1) == 0
        for z_ref, b_ref, p_ref, dp_ref, dz_ref, db_ref in ((za, ba, pa_ref, dpa, dza, dba), (zb, bb, pb_ref, dpb, dzb, dbb),
                                                            (zc, bc, pc_ref, dpc, dzc, dbc)):
            gate = _sigmoid(z_ref[...] + b_ref[...])
            dp_ref[...] = (gate * dmv).astype(BF16)
            dz = dmv * p_ref[...] * (gate * (1.0 - gate))
            dz_ref[...] = dz.astype(BF16)

            @pl.when(first)
            def _():
                db_ref[...] = jnp.zeros_like(db_ref)

            db_ref[...] += jnp.sum(dz, axis=0, keepdims=True)

    return _pc(body, name=name, grid=(D // GW, t_dim // tm), in_specs=z_specs + bg_specs + [tile] * 4,
               out_specs=[tile] * 6 + [bias_out] * 3,
               out_shape=[_sds((t_dim, D), BF16)] * 6 + [_sds((1, D), F32)] * 3)(z, z, z, bg, bg, bg, pa, pb, pc, dm)


def _concat_bf16(parts, name, tm=512):
    t_dim = parts[0].shape[0]
    offs = [sum(p.shape[1] for p in parts[:i]) for i in range(len(parts) + 1)]

    def body(*refs):
        o_ref = refs[-1]
        for ref, off in zip(refs[:-1], offs):
            o_ref[:, off:off + ref.shape[1]] = ref[...].astype(BF16)

    return _pc(body, name=name, grid=(t_dim // tm,), in_specs=[_bs((tm, p.shape[1]), lambda i: (i, 0)) for p in parts],
               out_specs=_bs((tm, offs[-1]), lambda i: (i, 0)), out_shape=_sds((t_dim, offs[-1]), BF16),
               vmem=48 << 20)(*parts)


def _coords():
    return lax.axis_index("x"), lax.axis_index("y"), lax.axis_index("c")


def _peer(k, x, y, c):
    px = 1 - x if k & 4 else x
    py = 1 - y if k & 2 else y
    pc = 1 - c if k & 1 else c
    return (px, py, pc), 4 * px + 2 * py + pc


def _exchange_copies(kind, bufs, send, recv):
    x, y, c = _coords()
    me = 4 * x + 2 * y + c
    na = len(bufs) if kind == "gather" else len(bufs) // 2
    out = []
    for k in range(1, NDEV):
        to, peer = _peer(k, x, y, c)
        for a in range(na):
            sems = dict(send_sem=send.at[a * 7 + k - 1], recv_sem=recv.at[a * 7 + k - 1], device_id=to, device_id_type=MESH)
            if kind == "gather":
                mine = pltpu.make_async_remote_copy(src_ref=bufs[a].at[:, me], dst_ref=bufs[a].at[:, me], **sems)
                theirs = pltpu.make_async_remote_copy(src_ref=bufs[a].at[:, me], dst_ref=bufs[a].at[:, peer], **sems)
            else:
                src, land = bufs[a], bufs[na + a]
                mine = pltpu.make_async_remote_copy(src_ref=src.at[:, peer], dst_ref=land.at[me], **sems)
                theirs = pltpu.make_async_remote_copy(src_ref=src.at[:, peer], dst_ref=land.at[peer], **sems)
            out.append((mine, theirs))
    return out


_HBM = pl.BlockSpec(memory_space=pltpu.HBM)
_SEM = pl.BlockSpec(memory_space=pltpu.SEMAPHORE)
_EFFECT = pltpu.SideEffectType.DATAFLOW_SIDE_EFFECTING


def _exchange_start(kind, arrays, after, name):
    na = len(arrays)
    if kind == "scatter":
        arrays = list(arrays) + [lax.empty((NDEV, s.shape[0]) + s.shape[2:], s.dtype) for s in arrays]
    nb = len(arrays)

    def body(*refs):
        bufs = refs[:nb]
        send, recv = refs[nb + 1], refs[nb + 2]
        token = refs[-1]
        for mine, _ in _exchange_copies(kind, bufs, send, recv):
            mine.start()
        token[...] = jnp.zeros_like(token)

    arrays = [pltpu.with_memory_space_constraint(b, pltpu.HBM) for b in arrays]
    outs = pl.pallas_call(
        body, name=name,
        out_shape=(pltpu.SemaphoreType.DMA((7 * na,)), pltpu.SemaphoreType.DMA((7 * na,)),
                   *[pltpu.HBM(b.shape, b.dtype) for b in arrays], _sds((8, 128), F32)),
        in_specs=[_HBM] * nb + [pl.BlockSpec(memory_space=pl.ANY)],
        out_specs=(_SEM, _SEM, *([_HBM] * nb), pl.BlockSpec(memory_space=pltpu.VMEM)),
        input_output_aliases={i: 2 + i for i in range(nb)},
        compiler_params=pltpu.CompilerParams(has_side_effects=_EFFECT), interpret=False)(*arrays, after)
    return outs[0], outs[1], list(outs[2:2 + nb]), outs[-1]


def _exchange_wait(kind, started, after, name):
    send, recv, arrays, _ = started
    nb = len(arrays)

    def body(*refs):
        for mine, theirs in _exchange_copies(kind, refs[:nb], refs[nb], refs[nb + 1]):
            mine.wait_send()
            theirs.wait_recv()

    outs = pl.pallas_call(
        body, name=name,
        out_shape=tuple(pltpu.HBM(b.shape, b.dtype) for b in arrays),
        in_specs=[_HBM] * nb + [_SEM, _SEM, pl.BlockSpec(memory_space=pl.ANY)],
        out_specs=tuple([_HBM] * nb),
        input_output_aliases={i: i for i in range(nb)},
        compiler_params=pltpu.CompilerParams(has_side_effects=_EFFECT), interpret=False)(*arrays, send, recv, after)
    return list(outs)


def _adam_math(w, g, m, v):
    m2 = ADAM_B1 * m + (1.0 - ADAM_B1) * g
    v2 = ADAM_B2 * v + (1.0 - ADAM_B2) * jnp.square(g)
    m_hat = m2 / (1.0 - ADAM_B1 ** ADAM_STEP)
    v_hat = v2 / (1.0 - ADAM_B2 ** ADAM_STEP)
    return -ADAM_LR * (m_hat / (jnp.sqrt(v_hat) + ADAM_EPS) + ADAM_WD * w), m2, v2


_DIRECT = (1, 4, 2, 6)


def _gather_copies(bufs, send_d, recv_d, send_f=None, recv_f=None):
    x, y, c = _coords()
    me = 4 * x + 2 * y + c
    sibling, _ = _peer(1, x, y, c)
    direct, forward = [], []
    for a, buf in enumerate(bufs):
        for j, k in enumerate(_DIRECT):
            to, peer = _peer(k, x, y, c)
            sems = dict(send_sem=send_d.at[4 * a + j], recv_sem=recv_d.at[4 * a + j], device_id=to, device_id_type=MESH)
            direct.append((pltpu.make_async_remote_copy(src_ref=buf.at[:, me], dst_ref=buf.at[:, me], **sems),
                           pltpu.make_async_remote_copy(src_ref=buf.at[:, me], dst_ref=buf.at[:, peer], **sems)))
        if send_f is None:
            continue
        for j, k in enumerate(_DIRECT[1:]):
            _, near = _peer(k, x, y, c)
            _, far = _peer(k | 1, x, y, c)
            sems = dict(send_sem=send_f.at[3 * a + j], recv_sem=recv_f.at[3 * a + j], device_id=sibling, device_id_type=MESH)
            forward.append((pltpu.make_async_remote_copy(src_ref=buf.at[:, near], dst_ref=buf.at[:, near], **sems),
                            pltpu.make_async_remote_copy(src_ref=buf.at[:, near], dst_ref=buf.at[:, far], **sems)))
    return direct, forward


def _gather_start(lands, after, name):
    na = len(lands)

    def body(*refs):
        direct, _ = _gather_copies(refs[:na], refs[na + 1], refs[na + 2])
        for mine, _ in direct:
            mine.start()
        refs[-1][...] = jnp.zeros_like(refs[-1])

    lands = [pltpu.with_memory_space_constraint(b, pltpu.HBM) for b in lands]
    outs = pl.pallas_call(
        body, name=name,
        out_shape=(pltpu.SemaphoreType.DMA((4 * na,)), pltpu.SemaphoreType.DMA((4 * na,)),
                   *[pltpu.HBM(b.shape, b.dtype) for b in lands], _sds((8, 128), F32)),
        in_specs=[_HBM] * na + [pl.BlockSpec(memory_space=pl.ANY)],
        out_specs=(_SEM, _SEM, *([_HBM] * na), pl.BlockSpec(memory_space=pltpu.VMEM)),
        input_output_aliases={i: 2 + i for i in range(na)},
        compiler_params=pltpu.CompilerParams(has_side_effects=_EFFECT), interpret=False)(*lands, after)
    return outs[0], outs[1], list(outs[2:2 + na]), outs[-1]


def _gather_forward(started, after, name):
    send_d, recv_d, lands, _ = started
    na = len(lands)

    def body(*refs):
        direct, forward = _gather_copies(refs[:na], refs[na], refs[na + 1], refs[2 * na + 3], refs[2 * na + 4])
        for a in range(na):
            for j in range(3):
                direct[4 * a + 1 + j][1].wait_recv()
                forward[3 * a + j][0].start()
        refs[-1][...] = jnp.zeros_like(refs[-1])

    outs = pl.pallas_call(
        body, name=name,
        out_shape=(*[pltpu.HBM(b.shape, b.dtype) for b in lands], pltpu.SemaphoreType.DMA((3 * na,)),
                   pltpu.SemaphoreType.DMA((3 * na,)), _sds((8, 128), F32)),
        in_specs=[_HBM] * na + [_SEM, _SEM, pl.BlockSpec(memory_space=pl.ANY)],
        out_specs=(*([_HBM] * na), _SEM, _SEM, pl.BlockSpec(memory_space=pltpu.VMEM)),
        input_output_aliases={i: i for i in range(na)},
        compiler_params=pltpu.CompilerParams(has_side_effects=_EFFECT), interpret=False)(*lands, send_d, recv_d, after)
    return send_d, recv_d, outs[na], outs[na + 1], list(outs[:na]), outs[-1]


def _gather_wait(forwarded, after, name):
    send_d, recv_d, send_f, recv_f, lands, _ = forwarded
    na = len(lands)

    def body(*refs):
        direct, forward = _gather_copies(refs[:na], refs[na], refs[na + 1], refs[na + 2], refs[na + 3])
        for a in range(na):
            for j in range(4):
                direct[4 * a + j][0].wait_send()
            direct[4 * a][1].wait_recv()
            for j in range(3):
                forward[3 * a + j][0].wait_send()
                forward[3 * a + j][1].wait_recv()

    outs = pl.pallas_call(
        body, name=name, out_shape=tuple(pltpu.HBM(b.shape, b.dtype) for b in lands),
        in_specs=[_HBM] * na + [_SEM] * 4 + [pl.BlockSpec(memory_space=pl.ANY)], out_specs=tuple([_HBM] * na),
        input_output_aliases={i: i for i in range(na)},
        compiler_params=pltpu.CompilerParams(has_side_effects=_EFFECT), interpret=False)(
            *lands, send_d, recv_d, send_f, recv_f, after)
    return list(outs)


def _reduce_update(landed, own, me, piece, w, m, v, layer, prev, transposed, name):
    rows, cols = landed.shape[2], landed.shape[3]
    tc = 256
    assert cols % tc == 0 and w.shape[1:] == ((cols, rows) if transposed else (rows, cols))
    lane_chunks = [(c0, min(128, rows - c0)) for c0 in range(0, rows, 128)]

    def body(me_ref, b_ref, own_ref, w_ref, m_ref, v_ref, *rest):
        outs = rest[-4:]
        mine = me_ref[0]
        s = None
        for d in range(NDEV):
            other = jnp.where(mine == d, (d + 1) % NDEV, d)
            part = jnp.where(mine == d, own_ref[0, 0], b_ref[other, 0]).astype(F32)
            s = part if s is None else s + part
        if not transposed:
            for o, val in zip(outs, (s,) + _adam_math(w_ref[0], s, m_ref[0], v_ref[0])):
                o[0] = val
            return
        for c0, wd in lane_chunks:
            chunk = s[c0:c0 + wd]
            if wd < 128:
                chunk = jnp.concatenate([chunk, jnp.zeros((128 - wd, tc), F32)], axis=0)
            g = chunk.T[:, :wd]
            sl = (0, slice(None), slice(c0, c0 + wd))
            for o, val in zip(outs, (g,) + _adam_math(w_ref[sl], g, m_ref[sl], v_ref[sl])):
                o[sl] = val

    if transposed:
        nat = _bs((1, tc, rows), lambda j, me_ref: (layer, j, 0))
    else:
        nat = _bs((1, rows, tc), lambda j, me_ref: (layer, 0, j))
    in_specs = [_bs((NDEV, 1, rows, tc), lambda j, me_ref: (0, piece, 0, j)),
                _bs((1, 1, rows, tc), lambda j, me_ref: (piece, me_ref[0], 0, j)), nat, nat, nat]
    args = [me, landed, own, w, m, v]
    alias = {}
    if prev is not None:
        in_specs += [pl.BlockSpec(memory_space=pl.ANY)] * 4
        args += list(prev)
        alias = {6 + i: i for i in range(4)}
    grid_spec = pltpu.PrefetchScalarGridSpec(num_scalar_prefetch=1, grid=(cols // tc,), in_specs=in_specs, out_specs=[nat] * 4)
    return pl.pallas_call(body, name=name, grid_spec=grid_spec, out_shape=[_sds(w.shape, F32)] * 4,
                          input_output_aliases=alias, interpret=False)(*args)


SMALL_LAYER = ("ffn1_norm", "mix_norm", "b_gate", "b_ln_g", "b_ln_b", "b_w_s", "b_b_s", "c_w", "c_scale", "ffn2_norm")
SMALL_ROWS = {"ffn1_norm": 8, "mix_norm": 8, "b_gate": 24, "b_ln_g": 4, "b_ln_b": 4, "b_w_s": 512, "b_b_s": 4, "c_w": 512,
              "c_scale": 4, "ffn2_norm": 8, "final_norm": 8}
LAYER_ROWS = sum(SMALL_ROWS[k] for k in SMALL_LAYER)


def _small_offset(name):
    if name == "final_norm":
        return LAYER_ROWS
    return sum(SMALL_ROWS[k] for k in SMALL_LAYER[:SMALL_LAYER.index(name)])


def _pack_small_grads(layer_grads, final, me, name):
    flat, where = [], []
    for k in SMALL_LAYER:
        parts = layer_grads[k] if k == "b_gate" else [layer_grads[k]]
        for i, part in enumerate(parts):
            flat.append(part)
            where.append(_small_offset(k) + i * (D // 128))
    if final is not None:
        flat.append(final)
        where.append(_small_offset("final_norm"))
    pack_rows = LAYER_ROWS + (SMALL_ROWS["final_norm"] if final is not None else 0)

    def body(me_ref, *refs):
        del me_ref
        o_ref = refs[-1]
        for ref, off in zip(refs[:-1], where):
            if len(ref.shape) == 2:
                for j in range(ref.shape[1] // 128):
                    o_ref[0, 0, off + j:off + j + 1, :] = ref[:, 128 * j:128 * (j + 1)]
            else:
                rows = ref.shape[1]
                for g in range(4):
                    o_ref[0, 0, off + g * rows:off + (g + 1) * rows, :] = ref[g]

    grid_spec = pltpu.PrefetchScalarGridSpec(
        num_scalar_prefetch=1, grid=(1,),
        in_specs=[pl.BlockSpec(a.shape, (lambda i, me_ref, nd=a.ndim: (0,) * nd)) for a in flat],
        out_specs=_bs((1, 1, pack_rows, 128), lambda i, me_ref: (0, me_ref[0], 0, 0)))
    return pl.pallas_call(body, name=name, grid_spec=grid_spec, out_shape=_sds((1, NDEV, pack_rows, 128), F32),
                          interpret=False)(me, *flat)


def _adamw_small(gathered, w, mom, var, name):
    names = SMALL_LAYER + ("final_norm",)
    ins = list(gathered) + [t[k] for k in names for t in (w, mom, var)]

    def body(*refs):
        outs = refs[len(ins):]

        def rows(l, off, n):
            land = refs[l]
            s = land[0, 0, off:off + n, :]
            for d in range(1, NDEV):
                s = s + land[0, d, off:off + n, :]
            return s

        def as_row(l, off, n):
            s = rows(l, off, n)
            return jnp.concatenate([s[j:j + 1, :] for j in range(n)], axis=1)

        def update(i, idx, g):
            wr, mr, vr = (refs[DEPTH + 3 * i + t] for t in range(3))
            d, m2, v2 = _adam_math(wr[idx], g, mr[idx], vr[idx])
            for o, val in zip(outs[4 * i:4 * i + 4], (g, d, m2, v2)):
                o[idx] = val

        for i, k in enumerate(names):
            n = SMALL_ROWS[k]
            off = _small_offset(k)
            if k == "final_norm":
                update(i, (slice(0, 1),), as_row(DEPTH - 1, off, n))
                continue
            for l in range(DEPTH):
                if k in ("b_w_s", "c_w"):
                    for g in range(4):
                        update(i, (l, g), rows(l, off + g * BLK, BLK))
                elif k in ("b_b_s", "c_scale"):
                    update(i, (l,), rows(l, off, n))
                else:
                    update(i, (slice(l, l + 1),), as_row(l, off, n))

    out_shape = [_sds(w[k].shape, F32) for k in names for _ in range(4)]
    outs = pl.pallas_call(body, name=name, out_shape=out_shape,
                          compiler_params=pltpu.CompilerParams(vmem_limit_bytes=48 << 20), interpret=False)(*ins)
    return tuple({k: outs[4 * i + t] for i, k in enumerate(names)} for t in range(4))


PG, PD = 0, 2


def _ffn_fwd(x, h, wf, tag, mid, next_gain):
    g_pre, u_pre, act = _gu_swiglu(h, wf, tm=2048, tn=256, b_row0=PG * FF, name=f"gu_{tag}")
    token = mid(act)
    out = _mm(act, wf, mode="nn", tm=512, tn=D, tk=FF, n=D, b_row0=PD * FF, out_dtype=F32, scale=0.5, res=x, after=token,
              norm_gain=next_gain, name=f"down_{tag}")
    out, h_next = out if next_gain is not None else (out, None)
    return out, h_next, (x, h, g_pre, u_pre, act)


def _ffn_bwd(dout, saved, gain, wf, tag, after, emit):
    x, h, g_pre, u_pre, act = saved
    grf = lax.empty((3, NDEV, FFS, D), BF16)
    dgu = _dact_dswiglu(dout, wf, g_pre, u_pre, tm=256, b_row0=PD * FF, after=after, name=f"dgu_{tag}")
    grf = _wgrad(act, dout, grf, piece0=PD, rows=FFS, tm=FF // 2, tn=D, tk=1024, scale=0.5, name=f"wg_down_{tag}")
    grf = _wgrad(dgu, h, grf, piece0=PG, rows=FFS, tm=FF // 2, tn=D, tk=2048, name=f"wg_gu_{tag}")
    token = emit([grf])
    dx, dgain = _mm_drms(dgu, wf, x, gain, dout, tm=256, b_row0=PG * FF, after=token, name=f"dx_{tag}")
    return dx, dgain, token


def _mixer_fwd(x, h, sp, ww, wo, wa, wbc, tag, mid, next_gain):
    z = _mm(h, ww, mode="nt", tm=1024, tn=1152, tk=D, n=INW, out_dtype=F32, name=f"z_{tag}")
    token = mid(z)
    acc, ml = _attn_fwd(z, None, None, gi=0, first=True, final=False, name=f"attn0_{tag}")
    acc, ml = _attn_fwd(z, acc, ml, gi=1, first=False, final=False, name=f"attn1_{tag}")
    ya, lse = _attn_fwd(z, acc, ml, gi=2, first=False, final=True, name=f"attn2_{tag}")
    yb = _gmlp_fwd(z, sp["b_ln_g"], sp["b_ln_b"], sp["b_w_s"], sp["b_b_s"], f"gmlp_{tag}")
    yc = _pool_fwd(z, sp["c_w"], sp["c_scale"], f"pool_{tag}")
    pa = _mm(ya, wa, mode="nt", tm=1024, tn=D, tk=GW, n=D, out_dtype=BF16, after=token, name=f"proj_a_{tag}")
    pb = _mm(yb, wbc, mode="nt", tm=1024, tn=D, tk=BW, n=D, b_row0=0, out_dtype=BF16, name=f"proj_b_{tag}")
    pc = _mm(yc, wbc, mode="nt", tm=1024, tn=D, tk=BW, n=D, b_row0=D, out_dtype=BF16, name=f"proj_c_{tag}")
    merged = _gates_fwd(z, sp["b_gate"], pa, pb, pc, f"gates_{tag}")
    out, h_next = _mm(merged, wo, mode="nn", tm=1024, tn=D, tk=D, n=D, out_dtype=F32, res=x, norm_gain=next_gain,
                      name=f"out_{tag}")
    return out, h_next, (x, h, z, ya, lse, yb, yc, pa, pb, pc, merged)


def _mixer_bwd(dout, saved, sp, ww, wo, wa, wbc, tag, after, emit):
    x, h, z, ya, lse, yb, yc, pa, pb, pc, merged = saved
    grw = lax.empty((1, NDEV, INS, D), BF16)
    gro = lax.empty((1, NDEV, DS, D), BF16)
    gra = lax.empty((1, NDEV, DS, GW), BF16)
    grbc = lax.empty((2, NDEV, DS, BW), BF16)
    dm = _mm(dout, wo, mode="nt", tm=1024, tn=D, tk=D, n=D, out_dtype=F32, after=after, name=f"dmerged_{tag}")
    gro = _wgrad(merged, dout, gro, piece0=0, rows=DS, tm=D, tn=D, tk=1024, name=f"wg_out_{tag}")
    dpa, dpb, dpc, dza, dzb, dzc, dba, dbb, dbc = _gates_bwd(z, sp["b_gate"], pa, pb, pc, dm, f"dgates_{tag}")
    dya = _mm(dpa, wa, mode="nn", tm=2048, tn=GW, tk=D, n=GW, out_dtype=F32, name=f"dya_{tag}")
    dyb = _mm(dpb, wbc, mode="nn", tm=2048, tn=BW, tk=D, n=BW, b_row0=0, out_dtype=F32, name=f"dyb_{tag}")
    dyc = _mm(dpc, wbc, mode="nn", tm=2048, tn=BW, tk=D, n=BW, b_row0=D, out_dtype=F32, name=f"dyc_{tag}")
    gra = _wgrad(dpa, ya, gra, piece0=0, rows=DS, tm=D, tn=GW, tk=1024, name=f"wg_pa_{tag}")
    grbc = _wgrad(dpb, yb, grbc, piece0=0, rows=DS, tm=D, tn=BW, tk=1024, name=f"wg_pb_{tag}")
    grbc = _wgrad(dpc, yc, grbc, piece0=1, rows=DS, tm=D, tn=BW, tk=1024, name=f"wg_pc_{tag}")
    dqkv = [_attn_bwd(z, dya, ya, lse, gi=gi, name=f"dattn{gi}_{tag}") for gi in range(3)]
    duv, dws, dbs, dlng, dlnb = _gmlp_bwd(z, dyb, sp["b_ln_g"], sp["b_ln_b"], sp["b_w_s"], sp["b_b_s"], f"dgmlp_{tag}")
    dxc, dcw, dcs = _pool_bwd(z, dyc, sp["c_w"], sp["c_scale"], f"dpool_{tag}")
    dz = _concat_bf16([dqkv[gi][part] for part in range(3) for gi in range(3)] + [duv, dxc, dza, dzb, dzc], f"dz_{tag}")
    grw = _wgrad(dz, h, grw, piece0=0, rows=INS, tm=INW // 2, tn=512, tk=1024, name=f"wg_in_{tag}")
    token = emit([grw, gro, gra, grbc])
    dx, dgain = _mm_drms(dz, ww, x, sp["mix_norm"], dout, tm=256, b_row0=0, after=token, name=f"dx_mix_{tag}")
    small = {"mix_norm": dgain, "b_gate": [dba, dbb, dbc], "b_ln_g": dlng, "b_ln_b": dlnb,
             "b_w_s": dws, "b_b_s": dbs, "c_w": dcw, "c_scale": dcs}
    return dx, small, token


PARTS = ("f1", "mix", "f2")


def _forward(xs, tgt, weights_of, pass_on, small, final_gain):
    saved, views = [], []
    cur = xs
    h = _rms_fwd(xs, small[0]["ffn1_norm"], "rms_first")
    for l in range(DEPTH):
        last = l + 1 == DEPTH
        (wf1,) = weights_of(l, "f1", cur)
        cur, h, s1 = _ffn_fwd(cur, h, wf1, f"f1l{l}", lambda t: pass_on(l, "mix", t), small[l]["mix_norm"])
        wmix = weights_of(l, "mix", cur)
        cur, h, s2 = _mixer_fwd(cur, h, small[l], *wmix, f"l{l}", lambda t: pass_on(l, "f2", t), small[l]["ffn2_norm"])
        (wf2,) = weights_of(l, "f2", cur)
        cur, h, s3 = _ffn_fwd(cur, h, wf2, f"f2l{l}", lambda t: None if last else pass_on(l + 1, "f1", t),
                              None if last else small[l + 1]["ffn1_norm"])
        saved.append((s1, s2, s3))
        views.append((wf1, wmix, wf2))
    loss_part, dcur, dfinal = _final_loss(cur, final_gain, tgt, "final_loss")
    return loss_part, dcur, dfinal, saved, views


def _backward_layer(dcur, saved_l, small_l, views_l, l, emit, after):
    wf1, wmix, wf2 = views_l
    s1, s2, s3 = saved_l
    dcur, dn2, token = _ffn_bwd(dcur, s3, small_l["ffn2_norm"], wf2, f"f2l{l}", after, lambda g: emit("f2", g))
    dcur, sg, token = _mixer_bwd(dcur, s2, small_l, *wmix, f"l{l}", token, lambda g: emit("mix", g))
    dcur, dn1, token = _ffn_bwd(dcur, s1, small_l["ffn1_norm"], wf1, f"f1l{l}", token, lambda g: emit("f1", g))
    sg["ffn1_norm"] = dn1
    sg["ffn2_norm"] = dn2
    return dcur, sg, token


BIG = ("ffn1_w_gate", "ffn1_w_up", "ffn1_w_down", "w_in", "w_proj_a", "w_proj_b", "w_proj_c", "w_out",
       "ffn2_w_gate", "ffn2_w_up", "ffn2_w_down")
ALL_WEIGHTS = ("ffn1_norm", "ffn1_w_gate", "ffn1_w_up", "ffn1_w_down", "mix_norm", "w_in", "b_gate", "b_ln_g", "b_ln_b",
               "b_w_s", "b_b_s", "c_w", "c_scale", "w_proj_a", "w_proj_b", "w_proj_c", "w_out", "ffn2_norm", "ffn2_w_gate",
               "ffn2_w_up", "ffn2_w_down", "final_norm")


def _place_shards(w, l, part, me, after, name):
    tr = lambda a: a.transpose(0, 2, 1)
    if part == "mix":
        srcs = [tr(w["w_in"]), w["w_out"], tr(w["w_proj_a"]), tr(w["w_proj_b"]), tr(w["w_proj_c"])]
        groups = [[0], [1], [2], [3, 4]]
    else:
        f = "ffn1" if part == "f1" else "ffn2"
        srcs = [tr(w[f + "_w_gate"]), tr(w[f + "_w_up"]), w[f + "_w_down"]]
        groups = [[0, 1, 2]]

    def body(me_ref, *refs):
        del me_ref
        outs = refs[len(srcs) + 1:]
        for o_ref, members in zip(outs, groups):
            for p, i in enumerate(members):
                o_ref[p, 0] = refs[i][0].astype(BF16)

    out_shape = [_sds((len(g), NDEV) + srcs[g[0]].shape[1:], BF16) for g in groups]
    grid_spec = pltpu.PrefetchScalarGridSpec(
        num_scalar_prefetch=1, grid=(1,),
        in_specs=[_bs((1,) + s.shape[1:], lambda i, me_ref: (l, 0, 0)) for s in srcs] + [pl.BlockSpec(memory_space=pl.ANY)],
        out_specs=[_bs((len(g), 1) + srcs[g[0]].shape[1:], lambda i, me_ref: (0, me_ref[0], 0, 0)) for g in groups])
    return pl.pallas_call(body, name=name, grid_spec=grid_spec, out_shape=out_shape,
                          compiler_params=pltpu.CompilerParams(vmem_limit_bytes=40 << 20), interpret=False)(me, *srcs, after)


def _part_views(part, gathered):
    if part == "mix":
        pw, po, pa, pbc = gathered
        return [pw.reshape(INW, D), po.reshape(D, D), pa.reshape(D, GW), pbc.reshape(2 * D, BW)]
    return [gathered[0].reshape(3 * FF, D)]


def _part_pieces(part):
    if part == "mix":
        return [(0, 0, "w_in", True), (1, 0, "w_out", False), (2, 0, "w_proj_a", True), (3, 0, "w_proj_b", True),
                (3, 1, "w_proj_c", True)]
    f = "ffn1" if part == "f1" else "ffn2"
    return [(0, 0, f + "_w_gate", True), (0, 1, f + "_w_up", True), (0, 2, f + "_w_down", False)]


def _layer_small(w, l):
    r = lambda a, shape: a.reshape(shape)
    return {"ffn1_norm": r(w["ffn1_norm"][l], (1, D)), "mix_norm": r(w["mix_norm"][l], (1, D)),
            "b_gate": r(w["b_gate"][l], (1, 3 * D)), "b_ln_g": r(w["b_ln_g"][l], (1, BW)), "b_ln_b": r(w["b_ln_b"][l], (1, BW)),
            "b_w_s": w["b_w_s"][l], "b_b_s": r(w["b_b_s"][l], (4, BLK, 1)), "c_w": w["c_w"][l],
            "c_scale": r(w["c_scale"][l], (4, 1, BLK)), "ffn2_norm": r(w["ffn2_norm"][l], (1, D))}


def kernel(x, ffn1_norm, ffn1_w_gate, ffn1_w_up, ffn1_w_down, mix_norm, w_in, b_gate, b_ln_g, b_ln_b, b_w_s, b_b_s, c_w, c_scale, w_proj_a, w_proj_b, w_proj_c, w_out, ffn2_norm, ffn2_w_gate, ffn2_w_up, ffn2_w_down, final_norm, loss_target, m_ffn1_norm, m_ffn1_w_gate, m_ffn1_w_up, m_ffn1_w_down, m_mix_norm, m_w_in, m_b_gate, m_b_ln_g, m_b_ln_b, m_b_w_s, m_b_b_s, m_c_w, m_c_scale, m_w_proj_a, m_w_proj_b, m_w_proj_c, m_w_out, m_ffn2_norm, m_ffn2_w_gate, m_ffn2_w_up, m_ffn2_w_down, m_final_norm, v_ffn1_norm, v_ffn1_w_gate, v_ffn1_w_up, v_ffn1_w_down, v_mix_norm, v_w_in, v_b_gate, v_b_ln_g, v_b_ln_b, v_b_w_s, v_b_b_s, v_c_w, v_c_scale, v_w_proj_a, v_w_proj_b, v_w_proj_c, v_w_out, v_ffn2_norm, v_ffn2_w_gate, v_ffn2_w_up, v_ffn2_w_down, v_final_norm):
    w = dict(ffn1_norm=ffn1_norm, ffn1_w_gate=ffn1_w_gate, ffn1_w_up=ffn1_w_up, ffn1_w_down=ffn1_w_down, mix_norm=mix_norm,
             w_in=w_in, b_gate=b_gate, b_ln_g=b_ln_g, b_ln_b=b_ln_b, b_w_s=b_w_s, b_b_s=b_b_s, c_w=c_w, c_scale=c_scale,
             w_proj_a=w_proj_a, w_proj_b=w_proj_b, w_proj_c=w_proj_c, w_out=w_out, ffn2_norm=ffn2_norm, ffn2_w_gate=ffn2_w_gate,
             ffn2_w_up=ffn2_w_up, ffn2_w_down=ffn2_w_down, final_norm=final_norm)
    mom = dict(ffn1_norm=m_ffn1_norm, ffn1_w_gate=m_ffn1_w_gate, ffn1_w_up=m_ffn1_w_up, ffn1_w_down=m_ffn1_w_down,
               mix_norm=m_mix_norm, w_in=m_w_in, b_gate=m_b_gate, b_ln_g=m_b_ln_g, b_ln_b=m_b_ln_b, b_w_s=m_b_w_s, b_b_s=m_b_b_s,
               c_w=m_c_w, c_scale=m_c_scale, w_proj_a=m_w_proj_a, w_proj_b=m_w_proj_b, w_proj_c=m_w_proj_c, w_out=m_w_out,
               ffn2_norm=m_ffn2_norm, ffn2_w_gate=m_ffn2_w_gate, ffn2_w_up=m_ffn2_w_up, ffn2_w_down=m_ffn2_w_down,
               final_norm=m_final_norm)
    var = dict(ffn1_norm=v_ffn1_norm, ffn1_w_gate=v_ffn1_w_gate, ffn1_w_up=v_ffn1_w_up, ffn1_w_down=v_ffn1_w_down,
               mix_norm=v_mix_norm, w_in=v_w_in, b_gate=v_b_gate, b_ln_g=v_b_ln_g, b_ln_b=v_b_ln_b, b_w_s=v_b_w_s, b_b_s=v_b_b_s,
               c_w=v_c_w, c_scale=v_c_scale, w_proj_a=v_w_proj_a, w_proj_b=v_w_proj_b, w_proj_c=v_w_proj_c, w_out=v_w_out,
               ffn2_norm=v_ffn2_norm, ffn2_w_gate=v_ffn2_w_gate, ffn2_w_up=v_ffn2_w_up, ffn2_w_down=v_ffn2_w_down,
               final_norm=v_final_norm)

    t_dim = x.shape[1]
    xs = x.reshape(t_dim, D)
    tgt = loss_target.reshape(t_dim, D)

    me = 4 * lax.axis_index("x") + 2 * lax.axis_index("y") + lax.axis_index("c")
    me_arr = me.astype(jnp.int32).reshape(1)
    gathers = {}
    token = jnp.zeros((8, 128), F32)
    for l in range(DEPTH):
        for part in PARTS:
            lands = _place_shards(w, l, part, me_arr, token, f"place_{part}_l{l}")
            gathers[l, part] = _gather_start(lands, token, f"gather_start_{part}_l{l}")
            token = gathers[l, part][-1]

    def pass_on(l, part, after):
        gathers[l, part] = _gather_forward(gathers[l, part], after, f"gather_forward_{part}_l{l}")
        return gathers[l, part][-1]

    pass_on(0, "f1", token)

    def weights_of(l, part, after):
        return _part_views(part, _gather_wait(gathers[l, part], after, f"gather_wait_{part}_l{l}"))

    small = [_layer_small(w, l) for l in range(DEPTH)]
    loss_part, dcur, dfinal, saved, views = _forward(xs, tgt, weights_of, pass_on, small, w["final_norm"].reshape(1, D))
    loss = lax.psum(loss_part[0, 0], ("x", "y", "c"))

    scatters = []
    small_gathers = [None] * DEPTH
    token = None
    for l in reversed(range(DEPTH)):
        def emit(part, grads, l=l):
            scatters.append((l, part, _exchange_start("scatter", grads, dfinal, f"scatter_start_{part}_l{l}")))
            return scatters[-1][2][-1]
        dcur, small_grads, token = _backward_layer(dcur, saved[l], small[l], views[l], l, emit, token)
        packed = _pack_small_grads(small_grads, dfinal if l == DEPTH - 1 else None, me_arr, f"pack_small_grads_l{l}")
        small_gathers[l] = _exchange_start("gather", [packed], token, f"gather_start_small_l{l}")
        token = small_gathers[l][-1]
    grad_x = dcur.reshape(x.shape)

    done = {}
    behind = token
    for l, part, started in scatters:
        arrs = _exchange_wait("scatter", started, behind, f"scatter_wait_{part}_l{l}")
        na = len(arrs) // 2
        for a, piece, k, transposed in _part_pieces(part):
            flip = transposed and w[k].shape[2] % 128 != 0
            view = (lambda t: t.transpose(0, 2, 1)) if flip else (lambda t: t)
            done[k] = _reduce_update(arrs[na + a], arrs[a], me_arr, piece, view(w[k]), view(mom[k]), view(var[k]), l,
                                     done.get(k), transposed and not flip, f"update_{k}_l{l}")
            behind = done[k][1]
    unflip = lambda k, t: t.transpose(0, 2, 1) if t.shape != w[k].shape else t
    grads, delta, new_m, new_v = ({k: unflip(k, done[k][t]) for k in BIG} for t in range(4))
    gathered = [None] * DEPTH
    for l in reversed(range(DEPTH)):
        gathered[l] = _exchange_wait("gather", small_gathers[l], behind, f"gather_wait_small_l{l}")[0]
        behind = gathered[l]
    as_row = lambda t: dict(t, final_norm=t["final_norm"].reshape(1, D))
    for dst, src in zip((grads, delta, new_m, new_v), _adamw_small(gathered, as_row(w), as_row(mom), as_row(var), "adamw_small")):
        dst.update(src)
        dst["final_norm"] = src["final_norm"].reshape(D)

    return (loss, grad_x, *[grads[k] for k in ALL_WEIGHTS], *[delta[k] for k in ALL_WEIGHTS],
            *[new_m[k] for k in ALL_WEIGHTS], *[new_v[k] for k in ALL_WEIGHTS])
```

```python
import math

import jax
import jax.numpy as jnp
from jax import lax
from jax.experimental import pallas as pl
from jax.experimental.pallas import tpu as pltpu

F32 = jnp.float32
BF16 = jnp.bfloat16
MESH = pl.DeviceIdType.MESH

NDEV = 8
DEPTH = 2
D = 1024
FF = 2816
FFS = FF // NDEV
INW = 6912
INS = INW // NDEV
DS = D // NDEV
BLK = 128
NH = 4
HD = 64
GW = NH * HD
DILS = (1, 4, 16)
QK_SCALE = 1.0 / math.sqrt(HD)
QKV_W = 3 * len(DILS) * GW
Q_B, K_B, V_B = 0, 3, 6
UV_B = 0
C_B128 = 8
G_B = 6
BW = 512
EPS = 1e-6
NEG = -1e30
VMEM_CAP = 60 * 1024 * 1024

ADAM_LR, ADAM_B1, ADAM_B2, ADAM_EPS, ADAM_WD, ADAM_STEP = 0.001, 0.9, 0.999, 1e-08, 0.01, 10

NT = (((1,), (1,)), ((), ()))
TN = (((0,), (0,)), ((), ()))


def _bs(shape, imap):
    return pl.BlockSpec(shape, imap)


def _pc(body, *, name, grid, in_specs, out_specs, out_shape, scratch=(), alias=None, vmem=None):
    params = {}
    if vmem is not None:
        params["vmem_limit_bytes"] = min(int(vmem), VMEM_CAP)
    big = lambda s: math.prod(s.shape) * jnp.dtype(s.dtype).itemsize >= (1 << 20)
    pin = lambda s: pltpu.HBM(s.shape, s.dtype) if big(s) else s
    out_shape = [pin(s) for s in out_shape] if isinstance(out_shape, (list, tuple)) else pin(out_shape)
    call = pl.pallas_call(
        body, name=name, grid=grid, in_specs=in_specs, out_specs=out_specs, out_shape=out_shape,
        scratch_shapes=list(scratch), input_output_aliases=alias or {},
        compiler_params=pltpu.CompilerParams(**params), interpret=False)
    return lambda *args: call(*[pltpu.with_memory_space_constraint(a, pltpu.HBM) if big(a) else a for a in args])


def _sds(shape, dtype):
    return jax.ShapeDtypeStruct(shape, dtype)


def _row_tile(rows, align, cap):
    best = None
    for cand in range(align, min(rows, cap) + 1, align):
        if rows % cand == 0:
            best = cand
    assert best is not None, (rows, align, cap)
    return best


def _mm(a, b, *, mode, tm, tn, tk, n, out_dtype, name, b_row0=0, scale=None, res=None, after=None, norm_gain=None):
    m, k_dim = a.shape
    nk = k_dim // tk
    assert m % tm == 0 and n % tn == 0 and k_dim % tk == 0 and (norm_gain is None or tn == n)
    b_off, rem = divmod(b_row0, tk if mode == "nn" else tn)
    assert rem == 0
    grid = (m // tm, n // tn, nk)
    in_specs = [_bs((tm, tk), lambda i, j, k: (i, k))]
    if mode == "nn":
        in_specs.append(_bs((tk, tn), lambda i, j, k: (b_off + k, j)))
    else:
        in_specs.append(_bs((tn, tk), lambda i, j, k: (b_off + j, k)))
    args = [a, b]
    if res is not None:
        in_specs.append(_bs((tm, tn), lambda i, j, k: (i, j)))
        args.append(res)
    if norm_gain is not None:
        in_specs.append(_bs((1, tn), lambda i, j, k: (0, 0)))
        args.append(norm_gain)
    if after is not None:
        in_specs.append(pl.BlockSpec(memory_space=pl.ANY))
        args.append(after)
    n_in = len(args)

    def body(*refs):
        a_ref, b_ref = refs[0], refs[1]
        r_ref = refs[2] if res is not None else None
        g_ref = refs[2 + (res is not None)] if norm_gain is not None else None
        o_ref = refs[n_in]
        av = a_ref[...].astype(BF16)
        bv = b_ref[...].astype(BF16)
        if mode == "nn":
            p = jnp.dot(av, bv, preferred_element_type=F32)
        else:
            p = lax.dot_general(av, bv, NT, preferred_element_type=F32)

        def fin(v):
            if scale is not None:
                v = v * scale
            if r_ref is not None:
                v = r_ref[...] + v
            o_ref[...] = v.astype(out_dtype)
            if g_ref is not None:
                r = lax.rsqrt(jnp.mean(v * v, axis=-1, keepdims=True) + EPS)
                refs[n_in + 1][...] = (v * r * g_ref[...]).astype(BF16)

        if nk == 1:
            fin(p)
        else:
            acc_ref = refs[-1]
            kk = pl.program_id(2)

            @pl.when(kk == 0)
            def _():
                acc_ref[...] = p

            @pl.when(kk > 0)
            def _():
                acc_ref[...] += p

            @pl.when(kk == nk - 1)
            def _():
                fin(acc_ref[...])

    scratch = [pltpu.VMEM((tm, tn), F32)] if nk > 1 else []
    est = 2 * (tm * tk * a.dtype.itemsize + tk * tn * b.dtype.itemsize + tm * tn * jnp.dtype(out_dtype).itemsize)
    est += (2 + (nk > 1)) * tm * tn * 4 + (2 * tm * tn * 4 if res is not None else 0)
    est += (tm * tk * 2 if a.dtype != BF16 else 0) + (tk * tn * 2 if b.dtype != BF16 else 0)
    tile = _bs((tm, tn), lambda i, j, k: (i, j))
    if norm_gain is None:
        return _pc(body, name=name, grid=grid, in_specs=in_specs, out_specs=tile, out_shape=_sds((m, n), out_dtype),
                   scratch=scratch, vmem=est + (8 << 20))(*args)
    return _pc(body, name=name, grid=grid, in_specs=in_specs, out_specs=[tile, tile],
               out_shape=[_sds((m, n), out_dtype), _sds((m, n), BF16)], scratch=scratch,
               vmem=est + 3 * tm * tn * 4 + (8 << 20))(*args)


def _wgrad(a, b, buf, *, piece0, rows, tm, tn, tk, name, scale=None):
    t_dim, m = a.shape
    n = b.shape[1]
    db = tm // rows
    tpp = NDEV // db
    nk = t_dim // tk
    assert tm % rows == 0 and NDEV % db == 0 and m % tm == 0 and n % tn == 0 and t_dim % tk == 0
    grid = (m // tm, n // tn, nk)

    def body(a_ref, b_ref, buf_ref, o_ref, acc_ref):
        del buf_ref
        kk = pl.program_id(2)
        p = lax.dot_general(a_ref[...].astype(BF16), b_ref[...].astype(BF16), TN, preferred_element_type=F32)

        @pl.when(kk == 0)
        def _():
            acc_ref[...] = p

        @pl.when(kk > 0)
        def _():
            acc_ref[...] += p

        @pl.when(kk == nk - 1)
        def _():
            for d in range(db):
                v = acc_ref[d * rows:(d + 1) * rows, :]
                if scale is not None:
                    v = v * scale
                o_ref[0, d] = v.astype(BF16)

    est = 2 * (tk * tm * a.dtype.itemsize + tk * tn * b.dtype.itemsize + tm * tn * 2) + 3 * tm * tn * 4 + tk * tm * 4
    return _pc(
        body, name=name, grid=grid,
        in_specs=[_bs((tk, tm), lambda i, j, k: (k, i)), _bs((tk, tn), lambda i, j, k: (k, j)),
                  pl.BlockSpec(memory_space=pl.ANY)],
        out_specs=_bs((1, db, rows, tn), lambda i, j, k: (piece0 + i // tpp, i % tpp, 0, j)),
        out_shape=_sds(buf.shape, buf.dtype), scratch=[pltpu.VMEM((tm, tn), F32)], alias={2: 0},
        vmem=est + (8 << 20))(a, b, buf)


def _mm_drms(a, b, x, gain, dres, *, tm, b_row0, after, name):
    m, k_dim = a.shape
    assert m % tm == 0 and b_row0 % k_dim == 0 and b.shape[1] == D
    k_blk = b_row0 // k_dim
    after = jnp.zeros((8, 128), F32) if after is None else after

    def body(a_ref, b_ref, x_ref, g_ref, dr_ref, after_ref, dx_ref, dg_ref):
        del after_ref
        dh_v = jnp.dot(a_ref[...].astype(BF16), b_ref[...].astype(BF16), preferred_element_type=F32)
        xv = x_ref[...]
        r = lax.rsqrt(jnp.mean(xv * xv, axis=-1, keepdims=True) + EPS)
        dyg = dh_v * g_ref[...]
        mq = jnp.mean(dyg * xv, axis=-1, keepdims=True)
        dx_ref[...] = dr_ref[...] + r * (dyg - xv * (r * r * mq))

        @pl.when(pl.program_id(0) == 0)
        def _():
            dg_ref[...] = jnp.zeros_like(dg_ref)

        dg_ref[...] += jnp.sum(dh_v * (xv * r), axis=0, keepdims=True)

    row = _bs((tm, D), lambda i: (i, 0))
    est = 2 * (tm * k_dim * a.dtype.itemsize + k_dim * D * b.dtype.itemsize + 3 * tm * D * 4) + 4 * tm * D * 4
    return _pc(body, name=name, grid=(m // tm,),
               in_specs=[_bs((tm, k_dim), lambda i: (i, 0)), _bs((k_dim, D), lambda i: (k_blk, 0)), row,
                         _bs((1, D), lambda i: (0, 0)), row, pl.BlockSpec(memory_space=pl.ANY)],
               out_specs=[row, _bs((1, D), lambda i: (0, 0))], out_shape=[_sds((m, D), F32), _sds((1, D), F32)],
               vmem=est + (8 << 20))(a, b, x, gain, dres, after)


def _dact_dswiglu(dout, b, g_pre, u_pre, *, tm, b_row0, after, name):
    t_dim = dout.shape[0]
    assert t_dim % tm == 0 and b_row0 % FF == 0
    k_blk = b_row0 // FF
    after = jnp.zeros((8, 128), F32) if after is None else after

    def body(d_ref, b_ref, g_ref, u_ref, after_ref, o_ref):
        del after_ref
        dav = 0.5 * lax.dot_general(d_ref[...].astype(BF16), b_ref[...].astype(BF16), NT, preferred_element_type=F32)
        g = g_ref[...].astype(F32)
        u = u_ref[...].astype(F32)
        s = _sigmoid(g)
        o_ref[:, :FF] = (dav * u * (s * (1.0 + g * (1.0 - s)))).astype(BF16)
        o_ref[:, FF:] = (dav * (g * s)).astype(BF16)

    est = 2 * (tm * D * 4 + FF * D * 2 + 2 * tm * 2 * FF * 2) + 5 * tm * FF * 4
    return _pc(body, name=name, grid=(t_dim // tm,),
               in_specs=[_bs((tm, D), lambda i: (i, 0)), _bs((FF, D), lambda i: (k_blk, 0)),
                         _bs((tm, FF), lambda i: (i, 0)), _bs((tm, FF), lambda i: (i, 0)), pl.BlockSpec(memory_space=pl.ANY)],
               out_specs=_bs((tm, 2 * FF), lambda i: (i, 0)), out_shape=_sds((t_dim, 2 * FF), BF16),
               vmem=est + (8 << 20))(dout, b, g_pre, u_pre, after)


def _rms_fwd(x, gain, name, tm=512):
    t_dim = x.shape[0]

    def body(x_ref, g_ref, o_ref):
        xv = x_ref[...]
        r = lax.rsqrt(jnp.mean(xv * xv, axis=-1, keepdims=True) + EPS)
        o_ref[...] = (xv * r * g_ref[...]).astype(BF16)

    return _pc(body, name=name, grid=(t_dim // tm,),
               in_specs=[_bs((tm, D), lambda i: (i, 0)), _bs((1, D), lambda i: (0, 0))],
               out_specs=_bs((tm, D), lambda i: (i, 0)), out_shape=_sds((t_dim, D), BF16))(x, gain)


def _sigmoid(v):
    return 1.0 / (1.0 + jnp.exp(-v))


def _gu_swiglu(h, b, *, tm, tn, b_row0, name):
    t_dim = h.shape[0]
    assert t_dim % tm == 0 and FF % tn == 0 and b_row0 % tn == 0
    j0 = b_row0 // tn

    def body(h_ref, bg_ref, bu_ref, g_ref, u_ref, a_ref):
        hv = h_ref[...]
        g16 = lax.dot_general(hv, bg_ref[...], NT, preferred_element_type=F32).astype(BF16)
        u16 = lax.dot_general(hv, bu_ref[...], NT, preferred_element_type=F32).astype(BF16)
        g = g16.astype(F32)
        g_ref[...] = g16
        u_ref[...] = u16
        a_ref[...] = (g * _sigmoid(g) * u16.astype(F32)).astype(BF16)

    tile = _bs((tm, tn), lambda i, j: (i, j))
    est = 2 * (tm * D * 2 + 2 * tn * D * 2 + 3 * tm * tn * 2) + 5 * tm * tn * 4
    return _pc(body, name=name, grid=(t_dim // tm, FF // tn),
               in_specs=[_bs((tm, D), lambda i, j: (i, 0)), _bs((tn, D), lambda i, j: (j0 + j, 0)),
                         _bs((tn, D), lambda i, j: (j0 + FF // tn + j, 0))],
               out_specs=[tile] * 3, out_shape=[_sds((t_dim, FF), BF16)] * 3, vmem=est + (8 << 20))(h, b, b)


def _final_loss(x, gain, tgt, name, tm=512):
    t_dim = x.shape[0]

    def body(x_ref, g_ref, t_ref, loss_ref, dx_ref, dg_ref):
        xv = x_ref[...]
        gv = g_ref[...]
        r = lax.rsqrt(jnp.mean(xv * xv, axis=-1, keepdims=True) + EPS)
        xn = xv * r
        err = xn * gv - t_ref[...]
        dy = err * (1.0 / D)
        dyg = dy * gv
        mq = jnp.mean(dyg * xv, axis=-1, keepdims=True)
        dx_ref[...] = r * (dyg - xv * (r * r * mq))

        @pl.when(pl.program_id(0) == 0)
        def _():
            dg_ref[...] = jnp.zeros_like(dg_ref)
            loss_ref[...] = jnp.zeros_like(loss_ref)

        dg_ref[...] += jnp.sum(dy * xn, axis=0, keepdims=True)
        part = 0.5 * jnp.sum(jnp.mean(err * err, axis=-1, keepdims=True), axis=0, keepdims=True)
        loss_ref[...] += jnp.broadcast_to(part, loss_ref.shape)

    return _pc(body, name=name, grid=(t_dim // tm,),
               in_specs=[_bs((tm, D), lambda i: (i, 0)), _bs((1, D), lambda i: (0, 0)), _bs((tm, D), lambda i: (i, 0))],
               out_specs=[_bs((8, 128), lambda i: (0, 0)), _bs((tm, D), lambda i: (i, 0)), _bs((1, D), lambda i: (0, 0))],
               out_shape=[_sds((8, 128), F32), _sds((t_dim, D), F32), _sds((1, D), F32)])(x, gain, tgt)


def _band_mask(n):
    ri = lax.broadcasted_iota(jnp.int32, (BLK, 2 * BLK), 0)
    cj = lax.broadcasted_iota(jnp.int32, (BLK, 2 * BLK), 1)
    dist = ri + BLK - cj
    return (dist >= 0) & (dist <= BLK) & ((cj >= BLK) | (n > 0))


def _halves(unit, imap_rows, col):
    return [_bs((unit, 128), (lambda u, j=j: (imap_rows(u), 2 * col + j))) for j in range(2)]


def _get_rows(pair, r, dil):
    return jnp.concatenate([h[pl.ds(r, BLK, stride=dil), :] for h in pair], axis=1)


def _put_rows(stage, r, dil, val):
    for j in range(2):
        stage[j, pl.ds(r, BLK, stride=dil), :] = val[:, 128 * j:128 * (j + 1)]


def _unstage(stage, out_ref):
    out_ref[:, :128] = stage[0]
    out_ref[:, 128:] = stage[1]


def _attn_fwd(z, acc, ml, *, gi, first, final, name):
    dil = DILS[gi]
    t_dim = z.shape[0]
    unit = BLK * dil
    nu = t_dim // unit
    own = lambda u: u
    prev = lambda u: jnp.maximum(u - 1, 0)
    in_specs = (_halves(unit, own, Q_B + gi) + _halves(unit, own, K_B + gi) + _halves(unit, prev, K_B + gi)
                + _halves(unit, own, V_B + gi) + _halves(unit, prev, V_B + gi))
    args = [z] * 10
    st_ml = _bs((unit, 128), lambda u: (u, 0))
    if not first:
        in_specs += _halves(unit, own, 0) + [st_ml]
        args += [acc, acc, ml]

    def body(*refs):
        q2, ko2, kp2, vo2, vp2 = (refs[2 * i:2 * i + 2] for i in range(5))
        o_acc_ref, o_ml_ref, stage = refs[-3], refs[-2], refs[-1]
        valid = _band_mask(pl.program_id(0))
        lane = lax.broadcasted_iota(jnp.int32, (BLK, GW), 1)
        lane_s = lax.broadcasted_iota(jnp.int32, (BLK, 128), 1)

        def band(r, carry):
            q = _get_rows(q2, r, dil) * QK_SCALE
            kb = jnp.concatenate([_get_rows(kp2, r, dil), _get_rows(ko2, r, dil)], axis=0).astype(BF16)
            vb = jnp.concatenate([_get_rows(vp2, r, dil), _get_rows(vo2, r, dil)], axis=0).astype(BF16)
            if first:
                acc_v = jnp.zeros((BLK, GW), F32)
                ml_v = jnp.where(lane_s < 64, NEG, 0.0).astype(F32)
            else:
                acc_v = _get_rows(refs[10:12], r, dil)
                ml_v = refs[12][pl.ds(r, BLK, stride=dil), :]
            heads = [(lane >= h * HD) & (lane < (h + 1) * HD) for h in range(NH)]
            rows = lambda v, h: v[h * BLK:(h + 1) * BLK]
            qs = jnp.concatenate([jnp.where(hm, q, 0.0) for hm in heads], axis=0).astype(BF16)
            s = lax.dot_general(qs, kb, NT, preferred_element_type=F32)
            s = jnp.where(jnp.concatenate([valid] * NH, axis=0), s, NEG)
            m_old = jnp.concatenate([ml_v[:, 16 * h:16 * h + 1] for h in range(NH)], axis=0)
            l_old = jnp.concatenate([ml_v[:, 64 + 16 * h:64 + 16 * h + 1] for h in range(NH)], axis=0)
            m_new = jnp.maximum(m_old, jnp.max(s, axis=-1, keepdims=True))
            alpha = jnp.exp(m_old - m_new)
            p = jnp.exp(s - m_new)
            l_new = alpha * l_old + jnp.sum(p, axis=-1, keepdims=True)
            pv = jnp.dot(p.astype(BF16), vb, preferred_element_type=F32)
            stat = m_new + jnp.log(l_new) if final else m_new
            acc_o, ml_o = acc_v, ml_v
            l_b = jnp.ones((BLK, GW), F32)
            for h, hm in enumerate(heads):
                acc_o = jnp.where(hm, acc_v * rows(alpha, h) + rows(pv, h), acc_o)
                l_b = jnp.where(hm, rows(l_new, h), l_b)
                ml_o = jnp.where((lane_s >= 16 * h) & (lane_s < 16 * h + 16), rows(stat, h), ml_o)
                if not final:
                    ml_o = jnp.where((lane_s >= 64 + 16 * h) & (lane_s < 64 + 16 * h + 16), rows(l_new, h), ml_o)
            _put_rows(stage, r, dil, acc_o / l_b if final else acc_o)
            o_ml_ref[pl.ds(r, BLK, stride=dil), :] = ml_o
            return carry

        lax.fori_loop(0, dil, band, 0, unroll=min(dil, 4))
        _unstage(stage, o_acc_ref)

    blocks = (len(in_specs) + 6) * unit * 128 * 4
    return _pc(body, name=name, grid=(nu,), in_specs=in_specs, out_specs=[_bs((unit, GW), lambda u: (u, 0)), st_ml],
               out_shape=[_sds((t_dim, GW), F32), _sds((t_dim, 128), F32)],
               scratch=[pltpu.VMEM((2, unit, 128), F32)], vmem=2 * blocks + (12 << 20))(*args)


def _attn_bwd(z, dy, y, lse, *, gi, name):
    dil = DILS[gi]
    t_dim = z.shape[0]
    unit = BLK * dil
    nu = t_dim // unit

    own = lambda u: jnp.minimum(u, nu - 1)
    prev = lambda u: jnp.maximum(jnp.minimum(u, nu - 1) - 1, 0)
    in_specs = (_halves(unit, own, Q_B + gi) + _halves(unit, own, K_B + gi) + _halves(unit, prev, K_B + gi)
                + _halves(unit, own, V_B + gi) + _halves(unit, prev, V_B + gi)
                + _halves(unit, own, 0) + _halves(unit, own, 0) + [_bs((unit, 128), lambda u: (own(u), 0))])
    tok = _bs((unit, GW), lambda u: (own(u), 0))
    kv_out = _bs((unit, GW), lambda u: (jnp.maximum(u - 1, 0), 0))

    def body(*refs):
        q2, ko2, kp2, vo2, vp2, dy2, y2 = (refs[2 * i:2 * i + 2] for i in range(7))
        lse_ref = refs[14]
        dq_ref, dk_ref, dv_ref, dq_st, dk_st, dv_st, dkc, dvc = refs[15:]
        u = pl.program_id(0)

        @pl.when(u == 0)
        def _():
            dkc[...] = jnp.zeros_like(dkc)
            dvc[...] = jnp.zeros_like(dvc)

        @pl.when(u < nu)
        def _():
            valid = _band_mask(u)
            lane = lax.broadcasted_iota(jnp.int32, (BLK, GW), 1)

            def band(r, carry):
                q = _get_rows(q2, r, dil) * QK_SCALE
                kb = jnp.concatenate([_get_rows(kp2, r, dil), _get_rows(ko2, r, dil)], axis=0).astype(BF16)
                vb = jnp.concatenate([_get_rows(vp2, r, dil), _get_rows(vo2, r, dil)], axis=0).astype(BF16)
                dyv = _get_rows(dy2, r, dil)
                yv = _get_rows(y2, r, dil)
                lse_v = lse_ref[pl.ds(r, BLK, stride=dil), :]
                heads = [(lane >= h * HD) & (lane < (h + 1) * HD) for h in range(NH)]
                qs = jnp.concatenate([jnp.where(hm, q, 0.0) for hm in heads], axis=0).astype(BF16)
                dys = jnp.concatenate([jnp.where(hm, dyv, 0.0) for hm in heads], axis=0)
                lse_s = jnp.concatenate([lse_v[:, 16 * h:16 * h + 1] for h in range(NH)], axis=0)
                s = lax.dot_general(qs, kb, NT, preferred_element_type=F32)
                p = jnp.where(jnp.concatenate([valid] * NH, axis=0), jnp.exp(s - lse_s), 0.0)
                delta = jnp.sum(dys * jnp.concatenate([yv] * NH, axis=0), axis=-1, keepdims=True)
                dys16 = dys.astype(BF16)
                dp = lax.dot_general(dys16, vb, NT, preferred_element_type=F32)
                ds = (p * (dp - delta)).astype(BF16)
                dq_all = jnp.dot(ds, kb, preferred_element_type=F32)
                dq = jnp.zeros((BLK, GW), F32)
                for h, hm in enumerate(heads):
                    dq = jnp.where(hm, dq_all[h * BLK:(h + 1) * BLK], dq)
                dkb = lax.dot_general(ds, qs, TN, preferred_element_type=F32)
                dvb = lax.dot_general(p.astype(BF16), dys16, TN, preferred_element_type=F32)
                _put_rows(dq_st, r, dil, dq * QK_SCALE)
                _put_rows(dk_st, r, dil, _get_rows((dkc.at[0], dkc.at[1]), r, dil) + dkb[:BLK])
                _put_rows(dv_st, r, dil, _get_rows((dvc.at[0], dvc.at[1]), r, dil) + dvb[:BLK])
                _put_rows(dkc, r, dil, dkb[BLK:])
                _put_rows(dvc, r, dil, dvb[BLK:])
                return carry

            lax.fori_loop(0, dil, band, 0, unroll=min(dil, 4))
            _unstage(dq_st, dq_ref)

            @pl.when(u > 0)
            def _():
                _unstage(dk_st, dk_ref)
                _unstage(dv_st, dv_ref)

        @pl.when(u == nu)
        def _():
            _unstage(dkc, dk_ref)
            _unstage(dvc, dv_ref)

    blocks = (15 + 6 + 5) * unit * 128 * 4 * 2
    return _pc(body, name=name, grid=(nu + 1,), in_specs=in_specs, out_specs=[tok, kv_out, kv_out],
               out_shape=[_sds((t_dim, GW), F32)] * 3, scratch=[pltpu.VMEM((2, unit, 128), F32)] * 5,
               vmem=blocks + (12 << 20))(*([z] * 10), dy, dy, y, y, lse)


_INV_SQRT2 = 1.0 / math.sqrt(2.0)
_INV_SQRT2PI = 1.0 / math.sqrt(2.0 * math.pi)


def _gelu(v):
    return 0.5 * v * (1.0 + lax.erf(v * _INV_SQRT2))


def _gelu_grad(v):
    return 0.5 * (1.0 + lax.erf(v * _INV_SQRT2)) + v * (_INV_SQRT2PI * jnp.exp(-0.5 * v * v))


def _gmlp_core(z_refs, lng, lnb, ws_ref, bs_ref):
    u_pre = jnp.concatenate([z_refs[0][...], z_refs[1][...]], axis=1).astype(F32)
    v_pre = jnp.concatenate([z_refs[2][...], z_refs[3][...]], axis=1).astype(F32)
    u = _gelu(u_pre)
    v = _gelu(v_pre)
    xc = v - jnp.mean(v, axis=-1, keepdims=True)
    rstd = lax.rsqrt(jnp.mean(xc * xc, axis=-1, keepdims=True) + EPS)
    xhat = xc * rstd
    vn = xhat * lng + lnb
    ti = lax.broadcasted_iota(jnp.int32, (BLK, BLK), 0)
    si = lax.broadcasted_iota(jnp.int32, (BLK, BLK), 1)
    causal = ti >= si
    vn16 = vn.astype(BF16)
    ws = [jnp.where(causal, ws_ref[g], 0.0).astype(BF16) for g in range(4)]
    mixed = jnp.concatenate(
        [jnp.dot(ws[g], vn16[:, g * BLK:(g + 1) * BLK], preferred_element_type=F32) + bs_ref[g] for g in range(4)], axis=1)
    return u_pre, v_pre, u, xhat, rstd, vn16, ws, causal, mixed


def _gmlp_specs():
    z_specs = [_bs((BLK, GW), (lambda i, c=c: (i, UV_B + c))) for c in range(4)]
    par_specs = [_bs((1, BW), lambda i: (0, 0)), _bs((1, BW), lambda i: (0, 0)),
                 _bs((4, BLK, BLK), lambda i: (0, 0, 0)), _bs((4, BLK, 1), lambda i: (0, 0, 0))]
    return z_specs, par_specs


def _gmlp_fwd(z, lng, lnb, ws, bs, name):
    t_dim = z.shape[0]
    z_specs, par_specs = _gmlp_specs()

    def body(z0, z1, z2, z3, lng_ref, lnb_ref, ws_ref, bs_ref, o_ref):
        core = _gmlp_core((z0, z1, z2, z3), lng_ref[...], lnb_ref[...], ws_ref, bs_ref)
        o_ref[...] = (core[2] * core[8]).astype(BF16)

    return _pc(body, name=name, grid=(t_dim // BLK,), in_specs=z_specs + par_specs,
               out_specs=_bs((BLK, BW), lambda i: (i, 0)), out_shape=_sds((t_dim, BW), BF16))(z, z, z, z, lng, lnb, ws, bs)


def _gmlp_bwd(z, dy, lng, lnb, ws, bs, name):
    t_dim = z.shape[0]
    z_specs, par_specs = _gmlp_specs()

    def body(z0, z1, z2, z3, lng_ref, lnb_ref, ws_ref, bs_ref, dy_ref, duv_ref, dws_ref, dbs_ref, dlng_ref, dlnb_ref):
        lng_v = lng_ref[...]
        u_pre, v_pre, u, xhat, rstd, vn16, wsm, causal, mixed = _gmlp_core((z0, z1, z2, z3), lng_v, lnb_ref[...], ws_ref, bs_ref)
        dyv = dy_ref[...].astype(F32)
        du = dyv * mixed
        dmixed = dyv * u

        @pl.when(pl.program_id(0) == 0)
        def _():
            dws_ref[...] = jnp.zeros_like(dws_ref)
            dbs_ref[...] = jnp.zeros_like(dbs_ref)
            dlng_ref[...] = jnp.zeros_like(dlng_ref)
            dlnb_ref[...] = jnp.zeros_like(dlnb_ref)

        dvn_parts = []
        for g in range(4):
            dm = dmixed[:, g * BLK:(g + 1) * BLK]
            dm16 = dm.astype(BF16)
            dw = lax.dot_general(dm16, vn16[:, g * BLK:(g + 1) * BLK], NT, preferred_element_type=F32)
            dws_ref[g] += jnp.where(causal, dw, 0.0)
            dbs_ref[g] += jnp.sum(dm.T, axis=0, keepdims=True)
            dvn_parts.append(lax.dot_general(wsm[g], dm16, TN, preferred_element_type=F32))
        dvn = jnp.concatenate(dvn_parts, axis=1)
        dlng_ref[...] += jnp.sum(dvn * xhat, axis=0, keepdims=True)
        dlnb_ref[...] += jnp.sum(dvn, axis=0, keepdims=True)
        dxh = dvn * lng_v
        dv = rstd * (dxh - jnp.mean(dxh, axis=-1, keepdims=True) - xhat * jnp.mean(dxh * xhat, axis=-1, keepdims=True))
        duv_ref[:, :BW] = (du * _gelu_grad(u_pre)).astype(BF16)
        duv_ref[:, BW:] = (dv * _gelu_grad(v_pre)).astype(BF16)

    return _pc(body, name=name, grid=(t_dim // BLK,),
               in_specs=z_specs + par_specs + [_bs((BLK, BW), lambda i: (i, 0))],
               out_specs=[_bs((BLK, 2 * BW), lambda i: (i, 0)), _bs((4, BLK, BLK), lambda i: (0, 0, 0)),
                          _bs((4, 1, BLK), lambda i: (0, 0, 0)), _bs((1, BW), lambda i: (0, 0)), _bs((1, BW), lambda i: (0, 0))],
               out_shape=[_sds((t_dim, 2 * BW), BF16), _sds((4, BLK, BLK), F32), _sds((4, 1, BLK), F32),
                          _sds((1, BW), F32), _sds((1, BW), F32)])(z, z, z, z, lng, lnb, ws, bs, dy)


def _pool_core(x, gi):
    t_dim = x.shape[0]
    w = jnp.left_shift(2, gi)
    row = lax.broadcasted_iota(jnp.int32, x.shape, 0)
    s = x
    for k in (1, 2, 4, 8):
        sh = jnp.where(row >= k, pltpu.roll(s, k, 0), 0.0)
        s = jnp.where(k < w, s + sh, s)
    cnt = jnp.minimum(row + 1, w).astype(F32)
    return s / cnt - x, cnt, row, w, t_dim


def _pool_fwd(z, cw, cs, name):
    t_dim = z.shape[0]

    def body(z_ref, cw_ref, cs_ref, o_ref):
        pooled = _pool_core(z_ref[...].astype(F32), pl.program_id(0))[0]
        y = jnp.dot(pooled.astype(BF16), cw_ref[0].astype(BF16), preferred_element_type=F32) * cs_ref[0]
        o_ref[...] = y.astype(BF16)

    return _pc(body, name=name, grid=(4,),
               in_specs=[_bs((t_dim, BLK), lambda g: (0, C_B128 + g)), _bs((1, BLK, BLK), lambda g: (g, 0, 0)),
                         _bs((1, 1, BLK), lambda g: (g, 0, 0))],
               out_specs=_bs((t_dim, BLK), lambda g: (0, g)), out_shape=_sds((t_dim, BW), BF16),
               vmem=40 << 20)(z, cw, cs)


def _pool_bwd(z, dy, cw, cs, name):
    t_dim = z.shape[0]

    def body(z_ref, dy_ref, cw_ref, cs_ref, dx_ref, dcw_ref, dcs_ref):
        pooled, cnt, row, w, _ = _pool_core(z_ref[...].astype(F32), pl.program_id(0))
        p16 = pooled.astype(BF16)
        cw16 = cw_ref[0].astype(BF16)
        dyv = dy_ref[...].astype(F32)
        lin = jnp.dot(p16, cw16, preferred_element_type=F32)
        dcs_ref[0] = jnp.sum(dyv * lin, axis=0, keepdims=True)
        dys = (dyv * cs_ref[0]).astype(BF16)
        dcw_ref[0] = lax.dot_general(p16, dys, TN, preferred_element_type=F32)
        dpool = lax.dot_general(dys, cw16, NT, preferred_element_type=F32)
        sb = dpool / cnt
        for k in (1, 2, 4, 8):
            sh = jnp.where(row < t_dim - k, pltpu.roll(sb, t_dim - k, 0), 0.0)
            sb = jnp.where(k < w, sb + sh, sb)
        dx_ref[...] = (sb - dpool).astype(BF16)

    return _pc(body, name=name, grid=(4,),
               in_specs=[_bs((t_dim, BLK), lambda g: (0, C_B128 + g)), _bs((t_dim, BLK), lambda g: (0, g)),
                         _bs((1, BLK, BLK), lambda g: (g, 0, 0)), _bs((1, 1, BLK), lambda g: (g, 0, 0))],
               out_specs=[_bs((t_dim, BLK), lambda g: (0, g)), _bs((1, BLK, BLK), lambda g: (g, 0, 0)),
                          _bs((1, 1, BLK), lambda g: (g, 0, 0))],
               out_shape=[_sds((t_dim, BW), BF16), _sds((4, BLK, BLK), F32), _sds((4, 1, BLK), F32)],
               vmem=48 << 20)(z, dy, cw, cs)


def _gates_specs(tm):
    z_specs = [_bs((tm, GW), (lambda j, i, b=b: (i, G_B + 4 * b + j))) for b in range(3)]
    bg_specs = [_bs((1, GW), (lambda j, i, b=b: (0, 4 * b + j))) for b in range(3)]
    tile = _bs((tm, GW), lambda j, i: (i, j))
    return z_specs, bg_specs, tile


def _gates_fwd(z, bg, pa, pb, pc, name, tm=512):
    t_dim = z.shape[0]
    z_specs, bg_specs, tile = _gates_specs(tm)

    def body(za, zb, zc, ba, bb, bc, pa_ref, pb_ref, pc_ref, o_ref):
        o_ref[...] = (_sigmoid(za[...] + ba[...]) * pa_ref[...] + _sigmoid(zb[...] + bb[...]) * pb_ref[...]
                      + _sigmoid(zc[...] + bc[...]) * pc_ref[...]).astype(BF16)

    return _pc(body, name=name, grid=(D // GW, t_dim // tm), in_specs=z_specs + bg_specs + [tile] * 3,
               out_specs=tile, out_shape=_sds((t_dim, D), BF16))(z, z, z, bg, bg, bg, pa, pb, pc)


def _gates_bwd(z, bg, pa, pb, pc, dm, name, tm=512):
    t_dim = z.shape[0]
    z_specs, bg_specs, tile = _gates_specs(tm)
    bias_out = _bs((1, GW), lambda j, i: (0, j))

    def body(za, zb, zc, ba, bb, bc, pa_ref, pb_ref, pc_ref, dm_ref, dpa, dpb, dpc, dza, dzb, dzc, dba, dbb, dbc):
        dmv = dm_ref[...]
        first = pl.program_id(1) == 0
        for z_ref, b_ref, p_ref, dp_ref, dz_ref, db_ref in ((za, ba, pa_ref, dpa, dza, dba), (zb, bb, pb_ref, dpb, dzb, dbb),
                                                            (zc, bc, pc_ref, dpc, dzc, dbc)):
            gate = _sigmoid(z_ref[...] + b_ref[...])
            dp_ref[...] = (gate * dmv).astype(BF16)
            dz = dmv * p_ref[...] * (gate * (1.0 - gate))
            dz_ref[...] = dz.astype(BF16)

            @pl.when(first)
            def _():
                db_ref[...] = jnp.zeros_like(db_ref)

            db_ref[...] += jnp.sum(dz, axis=0, keepdims=True)

    return _pc(body, name=name, grid=(D // GW, t_dim // tm), in_specs=z_specs + bg_specs + [tile] * 4,
               out_specs=[tile] * 6 + [bias_out] * 3,
               out_shape=[_sds((t_dim, D), BF16)] * 6 + [_sds((1, D), F32)] * 3)(z, z, z, bg, bg, bg, pa, pb, pc, dm)


def _concat_bf16(parts, name, tm=512):
    t_dim = parts[0].shape[0]
    offs = [sum(p.shape[1] for p in parts[:i]) for i in range(len(parts) + 1)]

    def body(*refs):
        o_ref = refs[-1]
        for ref, off in zip(refs[:-1], offs):
            o_ref[:, off:off + ref.shape[1]] = ref[...].astype(BF16)

    return _pc(body, name=name, grid=(t_dim // tm,), in_specs=[_bs((tm, p.shape[1]), lambda i: (i, 0)) for p in parts],
               out_specs=_bs((tm, offs[-1]), lambda i: (i, 0)), out_shape=_sds((t_dim, offs[-1]), BF16),
               vmem=48 << 20)(*parts)


def _coords():
    return lax.axis_index("x"), lax.axis_index("y"), lax.axis_index("c")


def _peer(k, x, y, c):
    px = 1 - x if k & 4 else x
    py = 1 - y if k & 2 else y
    pc = 1 - c if k & 1 else c
    return (px, py, pc), 4 * px + 2 * py + pc


def _exchange_copies(kind, bufs, send, recv):
    x, y, c = _coords()
    me = 4 * x + 2 * y + c
    na = len(bufs) if kind == "gather" else len(bufs) // 2
    out = []
    for k in range(1, NDEV):
        to, peer = _peer(k, x, y, c)
        for a in range(na):
            sems = dict(send_sem=send.at[a * 7 + k - 1], recv_sem=recv.at[a * 7 + k - 1], device_id=to, device_id_type=MESH)
            if kind == "gather":
                mine = pltpu.make_async_remote_copy(src_ref=bufs[a].at[:, me], dst_ref=bufs[a].at[:, me], **sems)
                theirs = pltpu.make_async_remote_copy(src_ref=bufs[a].at[:, me], dst_ref=bufs[a].at[:, peer], **sems)
            else:
                src, land = bufs[a], bufs[na + a]
                mine = pltpu.make_async_remote_copy(src_ref=src.at[:, peer], dst_ref=land.at[me], **sems)
                theirs = pltpu.make_async_remote_copy(src_ref=src.at[:, peer], dst_ref=land.at[peer], **sems)
            out.append((mine, theirs))
    return out


_HBM = pl.BlockSpec(memory_space=pltpu.HBM)
_SEM = pl.BlockSpec(memory_space=pltpu.SEMAPHORE)
_EFFECT = pltpu.SideEffectType.DATAFLOW_SIDE_EFFECTING


def _exchange_start(kind, arrays, after, name):
    na = len(arrays)
    if kind == "scatter":
        arrays = list(arrays) + [lax.empty((NDEV, s.shape[0]) + s.shape[2:], s.dtype) for s in arrays]
    nb = len(arrays)

    def body(*refs):
        bufs = refs[:nb]
        send, recv = refs[nb + 1], refs[nb + 2]
        token = refs[-1]
        for mine, _ in _exchange_copies(kind, bufs, send, recv):
            mine.start()
        token[...] = jnp.zeros_like(token)

    arrays = [pltpu.with_memory_space_constraint(b, pltpu.HBM) for b in arrays]
    outs = pl.pallas_call(
        body, name=name,
        out_shape=(pltpu.SemaphoreType.DMA((7 * na,)), pltpu.SemaphoreType.DMA((7 * na,)),
                   *[pltpu.HBM(b.shape, b.dtype) for b in arrays], _sds((8, 128), F32)),
        in_specs=[_HBM] * nb + [pl.BlockSpec(memory_space=pl.ANY)],
        out_specs=(_SEM, _SEM, *([_HBM] * nb), pl.BlockSpec(memory_space=pltpu.VMEM)),
        input_output_aliases={i: 2 + i for i in range(nb)},
        compiler_params=pltpu.CompilerParams(has_side_effects=_EFFECT), interpret=False)(*arrays, after)
    return outs[0], outs[1], list(outs[2:2 + nb]), outs[-1]


def _exchange_wait(kind, started, after, name):
    send, recv, arrays, _ = started
    nb = len(arrays)

    def body(*refs):
        for mine, theirs in _exchange_copies(kind, refs[:nb], refs[nb], refs[nb + 1]):
            mine.wait_send()
            theirs.wait_recv()

    outs = pl.pallas_call(
        body, name=name,
        out_shape=tuple(pltpu.HBM(b.shape, b.dtype) for b in arrays),
        in_specs=[_HBM] * nb + [_SEM, _SEM, pl.BlockSpec(memory_space=pl.ANY)],
        out_specs=tuple([_HBM] * nb),
        input_output_aliases={i: i for i in range(nb)},
        compiler_params=pltpu.CompilerParams(has_side_effects=_EFFECT), interpret=False)(*arrays, send, recv, after)
    return list(outs)


def _adam_math(w, g, m, v):
    m2 = ADAM_B1 * m + (1.0 - ADAM_B1) * g
    v2 = ADAM_B2 * v + (1.0 - ADAM_B2) * jnp.square(g)
    m_hat = m2 / (1.0 - ADAM_B1 ** ADAM_STEP)
    v_hat = v2 / (1.0 - ADAM_B2 ** ADAM_STEP)
    return -ADAM_LR * (m_hat / (jnp.sqrt(v_hat) + ADAM_EPS) + ADAM_WD * w), m2, v2


_DIRECT = (1, 4, 2, 6)


def _gather_copies(bufs, send_d, recv_d, send_f=None, recv_f=None):
    x, y, c = _coords()
    me = 4 * x + 2 * y + c
    sibling, _ = _peer(1, x, y, c)
    direct, forward = [], []
    for a, buf in enumerate(bufs):
        for j, k in enumerate(_DIRECT):
            to, peer = _peer(k, x, y, c)
            sems = dict(send_sem=send_d.at[4 * a + j], recv_sem=recv_d.at[4 * a + j], device_id=to, device_id_type=MESH)
            direct.append((pltpu.make_async_remote_copy(src_ref=buf.at[:, me], dst_ref=buf.at[:, me], **sems),
                           pltpu.make_async_remote_copy(src_ref=buf.at[:, me], dst_ref=buf.at[:, peer], **sems)))
        if send_f is None:
            continue
        for j, k in enumerate(_DIRECT[1:]):
            _, near = _peer(k, x, y, c)
            _, far = _peer(k | 1, x, y, c)
            sems = dict(send_sem=send_f.at[3 * a + j], recv_sem=recv_f.at[3 * a + j], device_id=sibling, device_id_type=MESH)
            forward.append((pltpu.make_async_remote_copy(src_ref=buf.at[:, near], dst_ref=buf.at[:, near], **sems),
                            pltpu.make_async_remote_copy(src_ref=buf.at[:, near], dst_ref=buf.at[:, far], **sems)))
    return direct, forward


def _gather_start(lands, after, name):
    na = len(lands)

    def body(*refs):
        direct, _ = _gather_copies(refs[:na], refs[na + 1], refs[na + 2])
        for mine, _ in direct:
            mine.start()
        refs[-1][...] = jnp.zeros_like(refs[-1])

    lands = [pltpu.with_memory_space_constraint(b, pltpu.HBM) for b in lands]
    outs = pl.pallas_call(
        body, name=name,
        out_shape=(pltpu.SemaphoreType.DMA((4 * na,)), pltpu.SemaphoreType.DMA((4 * na,)),
                   *[pltpu.HBM(b.shape, b.dtype) for b in lands], _sds((8, 128), F32)),
        in_specs=[_HBM] * na + [pl.BlockSpec(memory_space=pl.ANY)],
        out_specs=(_SEM, _SEM, *([_HBM] * na), pl.BlockSpec(memory_space=pltpu.VMEM)),
        input_output_aliases={i: 2 + i for i in range(na)},
        compiler_params=pltpu.CompilerParams(has_side_effects=_EFFECT), interpret=False)(*lands, after)
    return outs[0], outs[1], list(outs[2:2 + na]), outs[-1]


def _gather_forward(started, after, name):
    send_d, recv_d, lands, _ = started
    na = len(lands)

    def body(*refs):
        direct, forward = _gather_copies(refs[:na], refs[na], refs[na + 1], refs[2 * na + 3], refs[2 * na + 4])
        for a in range(na):
            for j in range(3):
                direct[4 * a + 1 + j][1].wait_recv()
                forward[3 * a + j][0].start()
        refs[-1][...] = jnp.zeros_like(refs[-1])

    outs = pl.pallas_call(
        body, name=name,
        out_shape=(*[pltpu.HBM(b.shape, b.dtype) for b in lands], pltpu.SemaphoreType.DMA((3 * na,)),
                   pltpu.SemaphoreType.DMA((3 * na,)), _sds((8, 128), F32)),
        in_specs=[_HBM] * na + [_SEM, _SEM, pl.BlockSpec(memory_space=pl.ANY)],
        out_specs=(*([_HBM] * na), _SEM, _SEM, pl.BlockSpec(memory_space=pltpu.VMEM)),
        input_output_aliases={i: i for i in range(na)},
        compiler_params=pltpu.CompilerParams(has_side_effects=_EFFECT), interpret=False)(*lands, send_d, recv_d, after)
    return send_d, recv_d, outs[na], outs[na + 1], list(outs[:na]), outs[-1]


def _gather_wait(forwarded, after, name):
    send_d, recv_d, send_f, recv_f, lands, _ = forwarded
    na = len(lands)

    def body(*refs):
        direct, forward = _gather_copies(refs[:na], refs[na], refs[na + 1], refs[na + 2], refs[na + 3])
        for a in range(na):
            for j in range(4):
                direct[4 * a + j][0].wait_send()
            direct[4 * a][1].wait_recv()
            for j in range(3):
                forward[3 * a + j][0].wait_send()
                forward[3 * a + j][1].wait_recv()

    outs = pl.pallas_call(
        body, name=name, out_shape=tuple(pltpu.HBM(b.shape, b.dtype) for b in lands),
        in_specs=[_HBM] * na + [_SEM] * 4 + [pl.BlockSpec(memory_space=pl.ANY)], out_specs=tuple([_HBM] * na),
        input_output_aliases={i: i for i in range(na)},
        compiler_params=pltpu.CompilerParams(has_side_effects=_EFFECT), interpret=False)(
            *lands, send_d, recv_d, send_f, recv_f, after)
    return list(outs)


def _reduce_update(landed, own, me, piece, w, m, v, layer, prev, transposed, name):
    rows, cols = landed.shape[2], landed.shape[3]
    tc = min(cols, 512 if rows <= FFS else 256)
    assert cols % tc == 0 and w.shape[1:] == ((cols, rows) if transposed else (rows, cols))
    lane_chunks = [(c0, min(128, rows - c0)) for c0 in range(0, rows, 128)]

    def body(me_ref, b_ref, own_ref, w_ref, m_ref, v_ref, *rest):
        outs = rest[-4:]
        mine = me_ref[0]
        s = None
        for d in range(NDEV):
            other = jnp.where(mine == d, (d + 1) % NDEV, d)
            part = jnp.where(mine == d, own_ref[0, 0], b_ref[other, 0]).astype(F32)
            s = part if s is None else s + part
        if not transposed:
            for o, val in zip(outs, (s,) + _adam_math(w_ref[0], s, m_ref[0], v_ref[0])):
                o[0] = val
            return
        for c0, wd in lane_chunks:
            chunk = s[c0:c0 + wd]
            if wd < 128:
                chunk = jnp.concatenate([chunk, jnp.zeros((128 - wd, tc), F32)], axis=0)
            g = chunk.T[:, :wd]
            sl = (0, slice(None), slice(c0, c0 + wd))
            for o, val in zip(outs, (g,) + _adam_math(w_ref[sl], g, m_ref[sl], v_ref[sl])):
                o[sl] = val

    if transposed:
        nat = _bs((1, tc, rows), lambda j, me_ref: (layer, j, 0))
    else:
        nat = _bs((1, rows, tc), lambda j, me_ref: (layer, 0, j))
    in_specs = [_bs((NDEV, 1, rows, tc), lambda j, me_ref: (0, piece, 0, j)),
                _bs((1, 1, rows, tc), lambda j, me_ref: (piece, me_ref[0], 0, j)), nat, nat, nat]
    args = [me, landed, own, w, m, v]
    alias = {}
    if prev is not None:
        in_specs += [pl.BlockSpec(memory_space=pl.ANY)] * 4
        args += list(prev)
        alias = {6 + i: i for i in range(4)}
    grid_spec = pltpu.PrefetchScalarGridSpec(num_scalar_prefetch=1, grid=(cols // tc,), in_specs=in_specs, out_specs=[nat] * 4)
    return pl.pallas_call(body, name=name, grid_spec=grid_spec, out_shape=[_sds(w.shape, F32)] * 4,
                          input_output_aliases=alias, interpret=False)(*args)


SMALL_LAYER = ("ffn1_norm", "mix_norm", "b_gate", "b_ln_g", "b_ln_b", "b_w_s", "b_b_s", "c_w", "c_scale", "ffn2_norm")
SMALL_ROWS = {"ffn1_norm": 8, "mix_norm": 8, "b_gate": 24, "b_ln_g": 4, "b_ln_b": 4, "b_w_s": 512, "b_b_s": 4, "c_w": 512,
              "c_scale": 4, "ffn2_norm": 8, "final_norm": 8}
LAYER_ROWS = sum(SMALL_ROWS[k] for k in SMALL_LAYER)


def _small_offset(name):
    if name == "final_norm":
        return LAYER_ROWS
    return sum(SMALL_ROWS[k] for k in SMALL_LAYER[:SMALL_LAYER.index(name)])


def _pack_small_grads(layer_grads, final, me, name):
    flat, where = [], []
    for k in SMALL_LAYER:
        parts = layer_grads[k] if k == "b_gate" else [layer_grads[k]]
        for i, part in enumerate(parts):
            flat.append(part)
            where.append(_small_offset(k) + i * (D // 128))
    if final is not None:
        flat.append(final)
        where.append(_small_offset("final_norm"))
    pack_rows = LAYER_ROWS + (SMALL_ROWS["final_norm"] if final is not None else 0)

    def body(me_ref, *refs):
        del me_ref
        o_ref = refs[-1]
        for ref, off in zip(refs[:-1], where):
            if len(ref.shape) == 2:
                for j in range(ref.shape[1] // 128):
                    o_ref[0, 0, off + j:off + j + 1, :] = ref[:, 128 * j:128 * (j + 1)]
            else:
                rows = ref.shape[1]
                for g in range(4):
                    o_ref[0, 0, off + g * rows:off + (g + 1) * rows, :] = ref[g]

    grid_spec = pltpu.PrefetchScalarGridSpec(
        num_scalar_prefetch=1, grid=(1,),
        in_specs=[pl.BlockSpec(a.shape, (lambda i, me_ref, nd=a.ndim: (0,) * nd)) for a in flat],
        out_specs=_bs((1, 1, pack_rows, 128), lambda i, me_ref: (0, me_ref[0], 0, 0)))
    return pl.pallas_call(body, name=name, grid_spec=grid_spec, out_shape=_sds((1, NDEV, pack_rows, 128), F32),
                          interpret=False)(me, *flat)


def _adamw_small(gathered, w, mom, var, name):
    names = SMALL_LAYER + ("final_norm",)
    ins = list(gathered) + [t[k] for k in names for t in (w, mom, var)]

    def body(*refs):
        outs = refs[len(ins):]

        def rows(l, off, n):
            land = refs[l]
            s = land[0, 0, off:off + n, :]
            for d in range(1, NDEV):
                s = s + land[0, d, off:off + n, :]
            return s

        def as_row(l, off, n):
            s = rows(l, off, n)
            return jnp.concatenate([s[j:j + 1, :] for j in range(n)], axis=1)

        def update(i, idx, g):
            wr, mr, vr = (refs[DEPTH + 3 * i + t] for t in range(3))
            d, m2, v2 = _adam_math(wr[idx], g, mr[idx], vr[idx])
            for o, val in zip(outs[4 * i:4 * i + 4], (g, d, m2, v2)):
                o[idx] = val

        for i, k in enumerate(names):
            n = SMALL_ROWS[k]
            off = _small_offset(k)
            if k == "final_norm":
                update(i, (slice(0, 1),), as_row(DEPTH - 1, off, n))
                continue
            for l in range(DEPTH):
                if k in ("b_w_s", "c_w"):
                    for g in range(4):
                        update(i, (l, g), rows(l, off + g * BLK, BLK))
                elif k in ("b_b_s", "c_scale"):
                    update(i, (l,), rows(l, off, n))
                else:
                    update(i, (slice(l, l + 1),), as_row(l, off, n))

    out_shape = [_sds(w[k].shape, F32) for k in names for _ in range(4)]
    outs = pl.pallas_call(body, name=name, out_shape=out_shape,
                          compiler_params=pltpu.CompilerParams(vmem_limit_bytes=48 << 20), interpret=False)(*ins)
    return tuple({k: outs[4 * i + t] for i, k in enumerate(names)} for t in range(4))


PG, PD = 0, 2


def _ffn_fwd(x, h, wf, tag, mid, next_gain):
    g_pre, u_pre, act = _gu_swiglu(h, wf, tm=2048, tn=256, b_row0=PG * FF, name=f"gu_{tag}")
    token = mid(act)
    out = _mm(act, wf, mode="nn", tm=512, tn=D, tk=FF, n=D, b_row0=PD * FF, out_dtype=F32, scale=0.5, res=x, after=token,
              norm_gain=next_gain, name=f"down_{tag}")
    out, h_next = out if next_gain is not None else (out, None)
    return out, h_next, (x, h, g_pre, u_pre, act)


def _ffn_bwd(dout, saved, gain, wf, tag, after, emit):
    x, h, g_pre, u_pre, act = saved
    grf = lax.empty((3, NDEV, FFS, D), BF16)
    dgu = _dact_dswiglu(dout, wf, g_pre, u_pre, tm=256, b_row0=PD * FF, after=after, name=f"dgu_{tag}")
    grf = _wgrad(act, dout, grf, piece0=PD, rows=FFS, tm=FF // 2, tn=D, tk=1024, scale=0.5, name=f"wg_down_{tag}")
    grf = _wgrad(dgu, h, grf, piece0=PG, rows=FFS, tm=FF // 2, tn=D, tk=2048, name=f"wg_gu_{tag}")
    token = emit([grf])
    dx, dgain = _mm_drms(dgu, wf, x, gain, dout, tm=256, b_row0=PG * FF, after=token, name=f"dx_{tag}")
    return dx, dgain, token


def _mixer_fwd(x, h, sp, ww, wo, wa, wbc, tag, mid, next_gain):
    zq = _mm(h, ww, mode="nt", tm=1024, tn=1152, tk=D, n=QKV_W, out_dtype=F32, name=f"zq_{tag}")
    z = _mm(h, ww, mode="nt", tm=1024, tn=1152, tk=D, n=INW - QKV_W, b_row0=QKV_W, out_dtype=BF16, name=f"z_{tag}")
    token = mid(z)
    acc, ml = _attn_fwd(zq, None, None, gi=0, first=True, final=False, name=f"attn0_{tag}")
    acc, ml = _attn_fwd(zq, acc, ml, gi=1, first=False, final=False, name=f"attn1_{tag}")
    ya, lse = _attn_fwd(zq, acc, ml, gi=2, first=False, final=True, name=f"attn2_{tag}")
    yb = _gmlp_fwd(z, sp["b_ln_g"], sp["b_ln_b"], sp["b_w_s"], sp["b_b_s"], f"gmlp_{tag}")
    yc = _pool_fwd(z, sp["c_w"], sp["c_scale"], f"pool_{tag}")
    pa = _mm(ya, wa, mode="nt", tm=1024, tn=D, tk=GW, n=D, out_dtype=BF16, after=token, name=f"proj_a_{tag}")
    pb = _mm(yb, wbc, mode="nt", tm=1024, tn=D, tk=BW, n=D, b_row0=0, out_dtype=BF16, name=f"proj_b_{tag}")
    pc = _mm(yc, wbc, mode="nt", tm=1024, tn=D, tk=BW, n=D, b_row0=D, out_dtype=BF16, name=f"proj_c_{tag}")
    merged = _gates_fwd(z, sp["b_gate"], pa, pb, pc, f"gates_{tag}")
    out, h_next = _mm(merged, wo, mode="nn", tm=1024, tn=D, tk=D, n=D, out_dtype=F32, res=x, norm_gain=next_gain,
                      name=f"out_{tag}")
    return out, h_next, (x, h, zq, z, ya, lse, yb, yc, pa, pb, pc, merged)


def _mixer_bwd(dout, saved, sp, ww, wo, wa, wbc, tag, after, emit):
    x, h, zq, z, ya, lse, yb, yc, pa, pb, pc, merged = saved
    grw = lax.empty((1, NDEV, INS, D), BF16)
    gro = lax.empty((1, NDEV, DS, D), BF16)
    gra = lax.empty((1, NDEV, DS, GW), BF16)
    grbc = lax.empty((2, NDEV, DS, BW), BF16)
    dm = _mm(dout, wo, mode="nt", tm=1024, tn=D, tk=D, n=D, out_dtype=F32, after=after, name=f"dmerged_{tag}")
    gro = _wgrad(merged, dout, gro, piece0=0, rows=DS, tm=D, tn=D, tk=1024, name=f"wg_out_{tag}")
    dpa, dpb, dpc, dza, dzb, dzc, dba, dbb, dbc = _gates_bwd(z, sp["b_gate"], pa, pb, pc, dm, f"dgates_{tag}")
    dya = _mm(dpa, wa, mode="nn", tm=2048, tn=GW, tk=D, n=GW, out_dtype=F32, name=f"dya_{tag}")
    dyb = _mm(dpb, wbc, mode="nn", tm=2048, tn=BW, tk=D, n=BW, b_row0=0, out_dtype=F32, name=f"dyb_{tag}")
    dyc = _mm(dpc, wbc, mode="nn", tm=2048, tn=BW, tk=D, n=BW, b_row0=D, out_dtype=F32, name=f"dyc_{tag}")
    gra = _wgrad(dpa, ya, gra, piece0=0, rows=DS, tm=D, tn=GW, tk=1024, name=f"wg_pa_{tag}")
    grbc = _wgrad(dpb, yb, grbc, piece0=0, rows=DS, tm=D, tn=BW, tk=1024, name=f"wg_pb_{tag}")
    grbc = _wgrad(dpc, yc, grbc, piece0=1, rows=DS, tm=D, tn=BW, tk=1024, name=f"wg_pc_{tag}")
    dqkv = [_attn_bwd(zq, dya, ya, lse, gi=gi, name=f"dattn{gi}_{tag}") for gi in range(3)]
    duv, dws, dbs, dlng, dlnb = _gmlp_bwd(z, dyb, sp["b_ln_g"], sp["b_ln_b"], sp["b_w_s"], sp["b_b_s"], f"dgmlp_{tag}")
    dxc, dcw, dcs = _pool_bwd(z, dyc, sp["c_w"], sp["c_scale"], f"dpool_{tag}")
    dz = _concat_bf16([dqkv[gi][part] for part in range(3) for gi in range(3)] + [duv, dxc, dza, dzb, dzc], f"dz_{tag}")
    grw = _wgrad(dz, h, grw, piece0=0, rows=INS, tm=INW // 2, tn=512, tk=1024, name=f"wg_in_{tag}")
    token = emit([grw, gro, gra, grbc])
    dx, dgain = _mm_drms(dz, ww, x, sp["mix_norm"], dout, tm=256, b_row0=0, after=token, name=f"dx_mix_{tag}")
    small = {"mix_norm": dgain, "b_gate": [dba, dbb, dbc], "b_ln_g": dlng, "b_ln_b": dlnb,
             "b_w_s": dws, "b_b_s": dbs, "c_w": dcw, "c_scale": dcs}
    return dx, small, token


PARTS = ("f1", "mix", "f2")


def _forward(xs, tgt, weights_of, pass_on, small, final_gain):
    saved, views = [], []
    cur = xs
    h = _rms_fwd(xs, small[0]["ffn1_norm"], "rms_first")
    for l in range(DEPTH):
        last = l + 1 == DEPTH
        (wf1,) = weights_of(l, "f1", cur)
        cur, h, s1 = _ffn_fwd(cur, h, wf1, f"f1l{l}", lambda t: pass_on(l, "mix", t), small[l]["mix_norm"])
        wmix = weights_of(l, "mix", cur)
        cur, h, s2 = _mixer_fwd(cur, h, small[l], *wmix, f"l{l}", lambda t: pass_on(l, "f2", t), small[l]["ffn2_norm"])
        (wf2,) = weights_of(l, "f2", cur)
        cur, h, s3 = _ffn_fwd(cur, h, wf2, f"f2l{l}", lambda t: None if last else pass_on(l + 1, "f1", t),
                              None if last else small[l + 1]["ffn1_norm"])
        saved.append((s1, s2, s3))
        views.append((wf1, wmix, wf2))
    loss_part, dcur, dfinal = _final_loss(cur, final_gain, tgt, "final_loss")
    return loss_part, dcur, dfinal, saved, views


def _backward_layer(dcur, saved_l, small_l, views_l, l, emit, after):
    wf1, wmix, wf2 = views_l
    s1, s2, s3 = saved_l
    dcur, dn2, token = _ffn_bwd(dcur, s3, small_l["ffn2_norm"], wf2, f"f2l{l}", after, lambda g: emit("f2", g))
    dcur, sg, token = _mixer_bwd(dcur, s2, small_l, *wmix, f"l{l}", token, lambda g: emit("mix", g))
    dcur, dn1, token = _ffn_bwd(dcur, s1, small_l["ffn1_norm"], wf1, f"f1l{l}", token, lambda g: emit("f1", g))
    sg["ffn1_norm"] = dn1
    sg["ffn2_norm"] = dn2
    return dcur, sg, token


BIG = ("ffn1_w_gate", "ffn1_w_up", "ffn1_w_down", "w_in", "w_proj_a", "w_proj_b", "w_proj_c", "w_out",
       "ffn2_w_gate", "ffn2_w_up", "ffn2_w_down")
ALL_WEIGHTS = ("ffn1_norm", "ffn1_w_gate", "ffn1_w_up", "ffn1_w_down", "mix_norm", "w_in", "b_gate", "b_ln_g", "b_ln_b",
               "b_w_s", "b_b_s", "c_w", "c_scale", "w_proj_a", "w_proj_b", "w_proj_c", "w_out", "ffn2_norm", "ffn2_w_gate",
               "ffn2_w_up", "ffn2_w_down", "final_norm")


def _place_shards(w, l, part, me, after, name):
    tr = lambda a: a.transpose(0, 2, 1)
    if part == "mix":
        srcs = [tr(w["w_in"]), w["w_out"], tr(w["w_proj_a"]), tr(w["w_proj_b"]), tr(w["w_proj_c"])]
        groups = [[0], [1], [2], [3, 4]]
    else:
        f = "ffn1" if part == "f1" else "ffn2"
        srcs = [tr(w[f + "_w_gate"]), tr(w[f + "_w_up"]), w[f + "_w_down"]]
        groups = [[0, 1, 2]]

    def body(me_ref, *refs):
        del me_ref
        outs = refs[len(srcs) + 1:]
        for o_ref, members in zip(outs, groups):
            for p, i in enumerate(members):
                o_ref[p, 0] = refs[i][0].astype(BF16)

    out_shape = [_sds((len(g), NDEV) + srcs[g[0]].shape[1:], BF16) for g in groups]
    grid_spec = pltpu.PrefetchScalarGridSpec(
        num_scalar_prefetch=1, grid=(1,),
        in_specs=[_bs((1,) + s.shape[1:], lambda i, me_ref: (l, 0, 0)) for s in srcs] + [pl.BlockSpec(memory_space=pl.ANY)],
        out_specs=[_bs((len(g), 1) + srcs[g[0]].shape[1:], lambda i, me_ref: (0, me_ref[0], 0, 0)) for g in groups])
    return pl.pallas_call(body, name=name, grid_spec=grid_spec, out_shape=out_shape,
                          compiler_params=pltpu.CompilerParams(vmem_limit_bytes=40 << 20), interpret=False)(me, *srcs, after)


def _part_views(part, gathered):
    if part == "mix":
        pw, po, pa, pbc = gathered
        return [pw.reshape(INW, D), po.reshape(D, D), pa.reshape(D, GW), pbc.reshape(2 * D, BW)]
    return [gathered[0].reshape(3 * FF, D)]


def _part_pieces(part):
    if part == "mix":
        return [(0, 0, "w_in", True), (1, 0, "w_out", False), (2, 0, "w_proj_a", True), (3, 0, "w_proj_b", True),
                (3, 1, "w_proj_c", True)]
    f = "ffn1" if part == "f1" else "ffn2"
    return [(0, 0, f + "_w_gate", True), (0, 1, f + "_w_up", True), (0, 2, f + "_w_down", False)]


def _layer_small(w, l):
    r = lambda a, shape: a.reshape(shape)
    return {"ffn1_norm": r(w["ffn1_norm"][l], (1, D)), "mix_norm": r(w["mix_norm"][l], (1, D)),
            "b_gate": r(w["b_gate"][l], (1, 3 * D)), "b_ln_g": r(w["b_ln_g"][l], (1, BW)), "b_ln_b": r(w["b_ln_b"][l], (1, BW)),
            "b_w_s": w["b_w_s"][l], "b_b_s": r(w["b_b_s"][l], (4, BLK, 1)), "c_w": w["c_w"][l],
            "c_scale": r(w["c_scale"][l], (4, 1, BLK)), "ffn2_norm": r(w["ffn2_norm"][l], (1, D))}


def kernel(x, ffn1_norm, ffn1_w_gate, ffn1_w_up, ffn1_w_down, mix_norm, w_in, b_gate, b_ln_g, b_ln_b, b_w_s, b_b_s, c_w, c_scale, w_proj_a, w_proj_b, w_proj_c, w_out, ffn2_norm, ffn2_w_gate, ffn2_w_up, ffn2_w_down, final_norm, loss_target, m_ffn1_norm, m_ffn1_w_gate, m_ffn1_w_up, m_ffn1_w_down, m_mix_norm, m_w_in, m_b_gate, m_b_ln_g, m_b_ln_b, m_b_w_s, m_b_b_s, m_c_w, m_c_scale, m_w_proj_a, m_w_proj_b, m_w_proj_c, m_w_out, m_ffn2_norm, m_ffn2_w_gate, m_ffn2_w_up, m_ffn2_w_down, m_final_norm, v_ffn1_norm, v_ffn1_w_gate, v_ffn1_w_up, v_ffn1_w_down, v_mix_norm, v_w_in, v_b_gate, v_b_ln_g, v_b_ln_b, v_b_w_s, v_b_b_s, v_c_w, v_c_scale, v_w_proj_a, v_w_proj_b, v_w_proj_c, v_w_out, v_ffn2_norm, v_ffn2_w_gate, v_ffn2_w_up, v_ffn2_w_down, v_final_norm):
    w = dict(ffn1_norm=ffn1_norm, ffn1_w_gate=ffn1_w_gate, ffn1_w_up=ffn1_w_up, ffn1_w_down=ffn1_w_down, mix_norm=mix_norm,
             w_in=w_in, b_gate=b_gate, b_ln_g=b_ln_g, b_ln_b=b_ln_b, b_w_s=b_w_s, b_b_s=b_b_s, c_w=c_w, c_scale=c_scale,
             w_proj_a=w_proj_a, w_proj_b=w_proj_b, w_proj_c=w_proj_c, w_out=w_out, ffn2_norm=ffn2_norm, ffn2_w_gate=ffn2_w_gate,
             ffn2_w_up=ffn2_w_up, ffn2_w_down=ffn2_w_down, final_norm=final_norm)
    mom = dict(ffn1_norm=m_ffn1_norm, ffn1_w_gate=m_ffn1_w_gate, ffn1_w_up=m_ffn1_w_up, ffn1_w_down=m_ffn1_w_down,
               mix_norm=m_mix_norm, w_in=m_w_in, b_gate=m_b_gate, b_ln_g=m_b_ln_g, b_ln_b=m_b_ln_b, b_w_s=m_b_w_s, b_b_s=m_b_b_s,
               c_w=m_c_w, c_scale=m_c_scale, w_proj_a=m_w_proj_a, w_proj_b=m_w_proj_b, w_proj_c=m_w_proj_c, w_out=m_w_out,
               ffn2_norm=m_ffn2_norm, ffn2_w_gate=m_ffn2_w_gate, ffn2_w_up=m_ffn2_w_up, ffn2_w_down=m_ffn2_w_down,
               final_norm=m_final_norm)
    var = dict(ffn1_norm=v_ffn1_norm, ffn1_w_gate=v_ffn1_w_gate, ffn1_w_up=v_ffn1_w_up, ffn1_w_down=v_ffn1_w_down,
               mix_norm=v_mix_norm, w_in=v_w_in, b_gate=v_b_gate, b_ln_g=v_b_ln_g, b_ln_b=v_b_ln_b, b_w_s=v_b_w_s, b_b_s=v_b_b_s,
               c_w=v_c_w, c_scale=v_c_scale, w_proj_a=v_w_proj_a, w_proj_b=v_w_proj_b, w_proj_c=v_w_proj_c, w_out=v_w_out,
               ffn2_norm=v_ffn2_norm, ffn2_w_gate=v_ffn2_w_gate, ffn2_w_up=v_ffn2_w_up, ffn2_w_down=v_ffn2_w_down,
               final_norm=v_final_norm)

    t_dim = x.shape[1]
    xs = x.reshape(t_dim, D)
    tgt = loss_target.reshape(t_dim, D)

    me = 4 * lax.axis_index("x") + 2 * lax.axis_index("y") + lax.axis_index("c")
    me_arr = me.astype(jnp.int32).reshape(1)
    gathers = {}
    token = jnp.zeros((8, 128), F32)
    for l in range(DEPTH):
        for part in PARTS:
            lands = _place_shards(w, l, part, me_arr, token, f"place_{part}_l{l}")
            gathers[l, part] = _gather_start(lands, token, f"gather_start_{part}_l{l}")
            token = gathers[l, part][-1]

    def pass_on(l, part, after):
        gathers[l, part] = _gather_forward(gathers[l, part], after, f"gather_forward_{part}_l{l}")
        return gathers[l, part][-1]

    pass_on(0, "f1", token)

    def weights_of(l, part, after):
        return _part_views(part, _gather_wait(gathers[l, part], after, f"gather_wait_{part}_l{l}"))

    small = [_layer_small(w, l) for l in range(DEPTH)]
    loss_part, dcur, dfinal, saved, views = _forward(xs, tgt, weights_of, pass_on, small, w["final_norm"].reshape(1, D))
    loss = lax.psum(loss_part[0, 0], ("x", "y", "c"))

    scatters = []
    small_gathers = [None] * DEPTH
    token = None
    for l in reversed(range(DEPTH)):
        def emit(part, grads, l=l):
            scatters.append((l, part, _exchange_start("scatter", grads, dfinal, f"scatter_start_{part}_l{l}")))
            return scatters[-1][2][-1]
        dcur, small_grads, token = _backward_layer(dcur, saved[l], small[l], views[l], l, emit, token)
        packed = _pack_small_grads(small_grads, dfinal if l == DEPTH - 1 else None, me_arr, f"pack_small_grads_l{l}")
        small_gathers[l] = _exchange_start("gather", [packed], token, f"gather_start_small_l{l}")
        token = small_gathers[l][-1]
    grad_x = dcur.reshape(x.shape)

    done = {}
    behind = token
    for l, part, started in scatters:
        arrs = _exchange_wait("scatter", started, behind, f"scatter_wait_{part}_l{l}")
        na = len(arrs) // 2
        for a, piece, k, transposed in _part_pieces(part):
            flip = transposed and w[k].shape[2] % 128 != 0
            view = (lambda t: t.transpose(0, 2, 1)) if flip else (lambda t: t)
            done[k] = _reduce_update(arrs[na + a], arrs[a], me_arr, piece, view(w[k]), view(mom[k]), view(var[k]), l,
                                     done.get(k), transposed and not flip, f"update_{k}_l{l}")
            behind = done[k][1]
    unflip = lambda k, t: t.transpose(0, 2, 1) if t.shape != w[k].shape else t
    grads, delta, new_m, new_v = ({k: unflip(k, done[k][t]) for k in BIG} for t in range(4))
    gathered = [None] * DEPTH
    for l in reversed(range(DEPTH)):
        gathered[l] = _exchange_wait("gather", small_gathers[l], behind, f"gather_wait_small_l{l}")[0]
        behind = gathered[l]
    as_row = lambda t: dict(t, final_norm=t["final_norm"].reshape(1, D))
    for dst, src in zip((grads, delta, new_m, new_v), _adamw_small(gathered, as_row(w), as_row(mom), as_row(var), "adamw_small")):
        dst.update(src)
        dst["final_norm"] = src["final_norm"].reshape(D)

    return (loss, grad_x, *[grads[k] for k in ALL_WEIGHTS], *[delta[k] for k in ALL_WEIGHTS],
            *[new_m[k] for k in ALL_WEIGHTS], *[new_v[k] for k in ALL_WEIGHTS])
```

```python
import math

import jax
import jax.numpy as jnp
from jax import lax
from jax.experimental import pallas as pl
from jax.experimental.pallas import tpu as pltpu

F32 = jnp.float32
BF16 = jnp.bfloat16
MESH = pl.DeviceIdType.MESH

NDEV = 8
DEPTH = 2
D = 1024
FF = 2816
FFS = FF // NDEV
INW = 6912
INS = INW // NDEV
DS = D // NDEV
BLK = 128
NH = 4
HD = 64
GW = NH * HD
DILS = (1, 4, 16)
QK_SCALE = 1.0 / math.sqrt(HD)
QKV_W = 3 * len(DILS) * GW
Q_B, K_B, V_B = 0, 3, 6
UV_B = 0
C_B128 = 8
G_B = 6
BW = 512
EPS = 1e-6
NEG = -1e30
VMEM_CAP = 60 * 1024 * 1024

ADAM_LR, ADAM_B1, ADAM_B2, ADAM_EPS, ADAM_WD, ADAM_STEP = 0.001, 0.9, 0.999, 1e-08, 0.01, 10

NT = (((1,), (1,)), ((), ()))
TN = (((0,), (0,)), ((), ()))


def _bs(shape, imap):
    return pl.BlockSpec(shape, imap)


def _pc(body, *, name, grid, in_specs, out_specs, out_shape, scratch=(), alias=None, vmem=None):
    params = {}
    if vmem is not None:
        params["vmem_limit_bytes"] = min(int(vmem), VMEM_CAP)
    big = lambda s: math.prod(s.shape) * jnp.dtype(s.dtype).itemsize >= (1 << 20)
    pin = lambda s: pltpu.HBM(s.shape, s.dtype) if big(s) else s
    out_shape = [pin(s) for s in out_shape] if isinstance(out_shape, (list, tuple)) else pin(out_shape)
    call = pl.pallas_call(
        body, name=name, grid=grid, in_specs=in_specs, out_specs=out_specs, out_shape=out_shape,
        scratch_shapes=list(scratch), input_output_aliases=alias or {},
        compiler_params=pltpu.CompilerParams(**params), interpret=False)
    return lambda *args: call(*[pltpu.with_memory_space_constraint(a, pltpu.HBM) if big(a) else a for a in args])


def _sds(shape, dtype):
    return jax.ShapeDtypeStruct(shape, dtype)


def _mm(a, b, *, mode, tm, tn, tk, n, out_dtype, name, b_row0=0, scale=None, res=None, after=None, norm_gain=None):
    m, k_dim = a.shape
    nk = k_dim // tk
    assert m % tm == 0 and n % tn == 0 and k_dim % tk == 0 and (norm_gain is None or tn == n)
    b_off, rem = divmod(b_row0, tk if mode == "nn" else tn)
    assert rem == 0
    grid = (m // tm, n // tn, nk)
    in_specs = [_bs((tm, tk), lambda i, j, k: (i, k))]
    if mode == "nn":
        in_specs.append(_bs((tk, tn), lambda i, j, k: (b_off + k, j)))
    else:
        in_specs.append(_bs((tn, tk), lambda i, j, k: (b_off + j, k)))
    args = [a, b]
    if res is not None:
        in_specs.append(_bs((tm, tn), lambda i, j, k: (i, j)))
        args.append(res)
    if norm_gain is not None:
        in_specs.append(_bs((1, tn), lambda i, j, k: (0, 0)))
        args.append(norm_gain)
    if after is not None:
        in_specs.append(pl.BlockSpec(memory_space=pl.ANY))
        args.append(after)
    n_in = len(args)

    def body(*refs):
        a_ref, b_ref = refs[0], refs[1]
        r_ref = refs[2] if res is not None else None
        g_ref = refs[2 + (res is not None)] if norm_gain is not None else None
        o_ref = refs[n_in]
        av = a_ref[...].astype(BF16)
        bv = b_ref[...].astype(BF16)
        if mode == "nn":
            p = jnp.dot(av, bv, preferred_element_type=F32)
        else:
            p = lax.dot_general(av, bv, NT, preferred_element_type=F32)

        def fin(v):
            if scale is not None:
                v = v * scale
            if r_ref is not None:
                v = r_ref[...] + v
            o_ref[...] = v.astype(out_dtype)
            if g_ref is not None:
                r = lax.rsqrt(jnp.mean(v * v, axis=-1, keepdims=True) + EPS)
                refs[n_in + 1][...] = (v * r * g_ref[...]).astype(BF16)

        if nk == 1:
            fin(p)
        else:
            acc_ref = refs[-1]
            kk = pl.program_id(2)

            @pl.when(kk == 0)
            def _():
                acc_ref[...] = p

            @pl.when(kk > 0)
            def _():
                acc_ref[...] += p

            @pl.when(kk == nk - 1)
            def _():
                fin(acc_ref[...])

    scratch = [pltpu.VMEM((tm, tn), F32)] if nk > 1 else []
    est = 2 * (tm * tk * a.dtype.itemsize + tk * tn * b.dtype.itemsize + tm * tn * jnp.dtype(out_dtype).itemsize)
    est += (2 + (nk > 1)) * tm * tn * 4 + (2 * tm * tn * 4 if res is not None else 0)
    est += (tm * tk * 2 if a.dtype != BF16 else 0) + (tk * tn * 2 if b.dtype != BF16 else 0)
    tile = _bs((tm, tn), lambda i, j, k: (i, j))
    if norm_gain is None:
        return _pc(body, name=name, grid=grid, in_specs=in_specs, out_specs=tile, out_shape=_sds((m, n), out_dtype),
                   scratch=scratch, vmem=est + (8 << 20))(*args)
    return _pc(body, name=name, grid=grid, in_specs=in_specs, out_specs=[tile, tile],
               out_shape=[_sds((m, n), out_dtype), _sds((m, n), BF16)], scratch=scratch,
               vmem=est + 3 * tm * tn * 4 + (8 << 20))(*args)


def _wgrad(a, b, buf, *, piece0, rows, tm, tn, tk, name, scale=None):
    t_dim, m = a.shape
    n = b.shape[1]
    db = tm // rows
    tpp = NDEV // db
    nk = t_dim // tk
    assert tm % rows == 0 and NDEV % db == 0 and m % tm == 0 and n % tn == 0 and t_dim % tk == 0
    grid = (m // tm, n // tn, nk)

    def body(a_ref, b_ref, buf_ref, o_ref, acc_ref):
        del buf_ref
        kk = pl.program_id(2)
        p = lax.dot_general(a_ref[...].astype(BF16), b_ref[...].astype(BF16), TN, preferred_element_type=F32)

        @pl.when(kk == 0)
        def _():
            acc_ref[...] = p

        @pl.when(kk > 0)
        def _():
            acc_ref[...] += p

        @pl.when(kk == nk - 1)
        def _():
            for d in range(db):
                v = acc_ref[d * rows:(d + 1) * rows, :]
                if scale is not None:
                    v = v * scale
                o_ref[0, d] = v.astype(BF16)

    est = 2 * (tk * tm * a.dtype.itemsize + tk * tn * b.dtype.itemsize + tm * tn * 2) + 3 * tm * tn * 4 + tk * tm * 4
    return _pc(
        body, name=name, grid=grid,
        in_specs=[_bs((tk, tm), lambda i, j, k: (k, i)), _bs((tk, tn), lambda i, j, k: (k, j)),
                  pl.BlockSpec(memory_space=pl.ANY)],
        out_specs=_bs((1, db, rows, tn), lambda i, j, k: (piece0 + i // tpp, i % tpp, 0, j)),
        out_shape=_sds(buf.shape, buf.dtype), scratch=[pltpu.VMEM((tm, tn), F32)], alias={2: 0},
        vmem=est + (8 << 20))(a, b, buf)


def _mm_drms(a, b, x, gain, dres, *, tm, b_row0, after, name):
    m, k_dim = a.shape
    assert m % tm == 0 and b_row0 % k_dim == 0 and b.shape[1] == D
    k_blk = b_row0 // k_dim
    after = jnp.zeros((8, 128), F32) if after is None else after

    def body(a_ref, b_ref, x_ref, g_ref, dr_ref, after_ref, dx_ref, dg_ref):
        del after_ref
        dh_v = jnp.dot(a_ref[...].astype(BF16), b_ref[...].astype(BF16), preferred_element_type=F32)
        xv = x_ref[...]
        r = lax.rsqrt(jnp.mean(xv * xv, axis=-1, keepdims=True) + EPS)
        dyg = dh_v * g_ref[...]
        mq = jnp.mean(dyg * xv, axis=-1, keepdims=True)
        dx_ref[...] = dr_ref[...] + r * (dyg - xv * (r * r * mq))

        @pl.when(pl.program_id(0) == 0)
        def _():
            dg_ref[...] = jnp.zeros_like(dg_ref)

        dg_ref[...] += jnp.sum(dh_v * (xv * r), axis=0, keepdims=True)

    row = _bs((tm, D), lambda i: (i, 0))
    est = 2 * (tm * k_dim * a.dtype.itemsize + k_dim * D * b.dtype.itemsize + 3 * tm * D * 4) + 4 * tm * D * 4
    return _pc(body, name=name, grid=(m // tm,),
               in_specs=[_bs((tm, k_dim), lambda i: (i, 0)), _bs((k_dim, D), lambda i: (k_blk, 0)), row,
                         _bs((1, D), lambda i: (0, 0)), row, pl.BlockSpec(memory_space=pl.ANY)],
               out_specs=[row, _bs((1, D), lambda i: (0, 0))], out_shape=[_sds((m, D), F32), _sds((1, D), F32)],
               vmem=est + (8 << 20))(a, b, x, gain, dres, after)


def _dact_dswiglu(dout, b, g_pre, u_pre, *, tm, b_row0, after, name):
    t_dim = dout.shape[0]
    assert t_dim % tm == 0 and b_row0 % FF == 0
    k_blk = b_row0 // FF
    after = jnp.zeros((8, 128), F32) if after is None else after

    def body(d_ref, b_ref, g_ref, u_ref, after_ref, o_ref):
        del after_ref
        dav = 0.5 * lax.dot_general(d_ref[...].astype(BF16), b_ref[...].astype(BF16), NT, preferred_element_type=F32)
        g = g_ref[...].astype(F32)
        u = u_ref[...].astype(F32)
        s = _sigmoid(g)
        o_ref[:, :FF] = (dav * u * (s * (1.0 + g * (1.0 - s)))).astype(BF16)
        o_ref[:, FF:] = (dav * (g * s)).astype(BF16)

    est = 2 * (tm * D * 4 + FF * D * 2 + 2 * tm * 2 * FF * 2) + 5 * tm * FF * 4
    return _pc(body, name=name, grid=(t_dim // tm,),
               in_specs=[_bs((tm, D), lambda i: (i, 0)), _bs((FF, D), lambda i: (k_blk, 0)),
                         _bs((tm, FF), lambda i: (i, 0)), _bs((tm, FF), lambda i: (i, 0)), pl.BlockSpec(memory_space=pl.ANY)],
               out_specs=_bs((tm, 2 * FF), lambda i: (i, 0)), out_shape=_sds((t_dim, 2 * FF), BF16),
               vmem=est + (8 << 20))(dout, b, g_pre, u_pre, after)


def _rms_fwd(x, gain, name, tm=512):
    t_dim = x.shape[0]

    def body(x_ref, g_ref, o_ref):
        xv = x_ref[...]
        r = lax.rsqrt(jnp.mean(xv * xv, axis=-1, keepdims=True) + EPS)
        o_ref[...] = (xv * r * g_ref[...]).astype(BF16)

    return _pc(body, name=name, grid=(t_dim // tm,),
               in_specs=[_bs((tm, D), lambda i: (i, 0)), _bs((1, D), lambda i: (0, 0))],
               out_specs=_bs((tm, D), lambda i: (i, 0)), out_shape=_sds((t_dim, D), BF16))(x, gain)


def _sigmoid(v):
    return 1.0 / (1.0 + jnp.exp(-v))


def _gu_swiglu(h, b, *, tm, tn, b_row0, name):
    t_dim = h.shape[0]
    assert t_dim % tm == 0 and FF % tn == 0 and b_row0 % tn == 0
    j0 = b_row0 // tn

    def body(h_ref, bg_ref, bu_ref, g_ref, u_ref, a_ref):
        hv = h_ref[...]
        g16 = lax.dot_general(hv, bg_ref[...], NT, preferred_element_type=F32).astype(BF16)
        u16 = lax.dot_general(hv, bu_ref[...], NT, preferred_element_type=F32).astype(BF16)
        g = g16.astype(F32)
        g_ref[...] = g16
        u_ref[...] = u16
        a_ref[...] = (g * _sigmoid(g) * u16.astype(F32)).astype(BF16)

    tile = _bs((tm, tn), lambda i, j: (i, j))
    est = 2 * (tm * D * 2 + 2 * tn * D * 2 + 3 * tm * tn * 2) + 5 * tm * tn * 4
    return _pc(body, name=name, grid=(t_dim // tm, FF // tn),
               in_specs=[_bs((tm, D), lambda i, j: (i, 0)), _bs((tn, D), lambda i, j: (j0 + j, 0)),
                         _bs((tn, D), lambda i, j: (j0 + FF // tn + j, 0))],
               out_specs=[tile] * 3, out_shape=[_sds((t_dim, FF), BF16)] * 3, vmem=est + (8 << 20))(h, b, b)


def _final_loss(x, gain, tgt, name, tm=512):
    t_dim = x.shape[0]

    def body(x_ref, g_ref, t_ref, loss_ref, dx_ref, dg_ref):
        xv = x_ref[...]
        gv = g_ref[...]
        r = lax.rsqrt(jnp.mean(xv * xv, axis=-1, keepdims=True) + EPS)
        xn = xv * r
        err = xn * gv - t_ref[...]
        dy = err * (1.0 / D)
        dyg = dy * gv
        mq = jnp.mean(dyg * xv, axis=-1, keepdims=True)
        dx_ref[...] = r * (dyg - xv * (r * r * mq))

        @pl.when(pl.program_id(0) == 0)
        def _():
            dg_ref[...] = jnp.zeros_like(dg_ref)
            loss_ref[...] = jnp.zeros_like(loss_ref)

        dg_ref[...] += jnp.sum(dy * xn, axis=0, keepdims=True)
        part = 0.5 * jnp.sum(jnp.mean(err * err, axis=-1, keepdims=True), axis=0, keepdims=True)
        loss_ref[...] += jnp.broadcast_to(part, loss_ref.shape)

    return _pc(body, name=name, grid=(t_dim // tm,),
               in_specs=[_bs((tm, D), lambda i: (i, 0)), _bs((1, D), lambda i: (0, 0)), _bs((tm, D), lambda i: (i, 0))],
               out_specs=[_bs((8, 128), lambda i: (0, 0)), _bs((tm, D), lambda i: (i, 0)), _bs((1, D), lambda i: (0, 0))],
               out_shape=[_sds((8, 128), F32), _sds((t_dim, D), F32), _sds((1, D), F32)])(x, gain, tgt)


def _band_mask(n):
    ri = lax.broadcasted_iota(jnp.int32, (BLK, 2 * BLK), 0)
    cj = lax.broadcasted_iota(jnp.int32, (BLK, 2 * BLK), 1)
    dist = ri + BLK - cj
    return (dist >= 0) & (dist <= BLK) & ((cj >= BLK) | (n > 0))


def _halves(unit, imap_rows, col):
    return [_bs((unit, 128), (lambda u, j=j: (imap_rows(u), 2 * col + j))) for j in range(2)]


def _get_rows(pair, r, dil):
    return jnp.concatenate([h[pl.ds(r, BLK, stride=dil), :] for h in pair], axis=1)


def _put_rows(stage, r, dil, val):
    for j in range(2):
        stage[j, pl.ds(r, BLK, stride=dil), :] = val[:, 128 * j:128 * (j + 1)]


def _unstage(stage, out_ref):
    out_ref[:, :128] = stage[0]
    out_ref[:, 128:] = stage[1]


def _attn_fwd(z, acc, ml, *, gi, first, final, name):
    dil = DILS[gi]
    t_dim = z.shape[0]
    unit = BLK * dil
    nu = t_dim // unit
    own = lambda u: u
    prev = lambda u: jnp.maximum(u - 1, 0)
    in_specs = (_halves(unit, own, Q_B + gi) + _halves(unit, own, K_B + gi) + _halves(unit, prev, K_B + gi)
                + _halves(unit, own, V_B + gi) + _halves(unit, prev, V_B + gi))
    args = [z] * 10
    st_ml = _bs((unit, 128), lambda u: (u, 0))
    if not first:
        in_specs += _halves(unit, own, 0) + [st_ml]
        args += [acc, acc, ml]

    def body(*refs):
        q2, ko2, kp2, vo2, vp2 = (refs[2 * i:2 * i + 2] for i in range(5))
        o_acc_ref, o_ml_ref, stage = refs[-3], refs[-2], refs[-1]
        valid = _band_mask(pl.program_id(0))
        lane = lax.broadcasted_iota(jnp.int32, (BLK, GW), 1)
        lane_s = lax.broadcasted_iota(jnp.int32, (BLK, 128), 1)

        def band(r, carry):
            q = _get_rows(q2, r, dil) * QK_SCALE
            kb = jnp.concatenate([_get_rows(kp2, r, dil), _get_rows(ko2, r, dil)], axis=0).astype(BF16)
            vb = jnp.concatenate([_get_rows(vp2, r, dil), _get_rows(vo2, r, dil)], axis=0).astype(BF16)
            if first:
                acc_v = jnp.zeros((BLK, GW), F32)
                ml_v = jnp.where(lane_s < 64, NEG, 0.0).astype(F32)
            else:
                acc_v = _get_rows(refs[10:12], r, dil)
                ml_v = refs[12][pl.ds(r, BLK, stride=dil), :]
            heads = [(lane >= h * HD) & (lane < (h + 1) * HD) for h in range(NH)]
            rows = lambda v, h: v[h * BLK:(h + 1) * BLK]
            qs = jnp.concatenate([jnp.where(hm, q, 0.0) for hm in heads], axis=0).astype(BF16)
            s = lax.dot_general(qs, kb, NT, preferred_element_type=F32)
            s = jnp.where(jnp.concatenate([valid] * NH, axis=0), s, NEG)
            m_old = jnp.concatenate([ml_v[:, 16 * h:16 * h + 1] for h in range(NH)], axis=0)
            l_old = jnp.concatenate([ml_v[:, 64 + 16 * h:64 + 16 * h + 1] for h in range(NH)], axis=0)
            m_new = jnp.maximum(m_old, jnp.max(s, axis=-1, keepdims=True))
            alpha = jnp.exp(m_old - m_new)
            p = jnp.exp(s - m_new)
            l_new = alpha * l_old + jnp.sum(p, axis=-1, keepdims=True)
            pv = jnp.dot(p.astype(BF16), vb, preferred_element_type=F32)
            stat = m_new + jnp.log(l_new) if final else m_new
            acc_o, ml_o = acc_v, ml_v
            l_b = jnp.ones((BLK, GW), F32)
            for h, hm in enumerate(heads):
                acc_o = jnp.where(hm, acc_v * rows(alpha, h) + rows(pv, h), acc_o)
                l_b = jnp.where(hm, rows(l_new, h), l_b)
                ml_o = jnp.where((lane_s >= 16 * h) & (lane_s < 16 * h + 16), rows(stat, h), ml_o)
                if not final:
                    ml_o = jnp.where((lane_s >= 64 + 16 * h) & (lane_s < 64 + 16 * h + 16), rows(l_new, h), ml_o)
            _put_rows(stage, r, dil, acc_o / l_b if final else acc_o)
            o_ml_ref[pl.ds(r, BLK, stride=dil), :] = ml_o
            return carry

        lax.fori_loop(0, dil, band, 0, unroll=min(dil, 4))
        _unstage(stage, o_acc_ref)

    blocks = (len(in_specs) + 6) * unit * 128 * 4
    return _pc(body, name=name, grid=(nu,), in_specs=in_specs, out_specs=[_bs((unit, GW), lambda u: (u, 0)), st_ml],
               out_shape=[_sds((t_dim, GW), F32), _sds((t_dim, 128), F32)],
               scratch=[pltpu.VMEM((2, unit, 128), F32)], vmem=2 * blocks + (12 << 20))(*args)


def _attn_bwd(z, dy, y, lse, *, gi, name):
    dil = DILS[gi]
    t_dim = z.shape[0]
    unit = BLK * dil
    nu = t_dim // unit

    own = lambda u: jnp.minimum(u, nu - 1)
    prev = lambda u: jnp.maximum(jnp.minimum(u, nu - 1) - 1, 0)
    in_specs = (_halves(unit, own, Q_B + gi) + _halves(unit, own, K_B + gi) + _halves(unit, prev, K_B + gi)
                + _halves(unit, own, V_B + gi) + _halves(unit, prev, V_B + gi)
                + _halves(unit, own, 0) + _halves(unit, own, 0) + [_bs((unit, 128), lambda u: (own(u), 0))])
    tok = _bs((unit, GW), lambda u: (own(u), 0))
    kv_out = _bs((unit, GW), lambda u: (jnp.maximum(u - 1, 0), 0))

    def body(*refs):
        q2, ko2, kp2, vo2, vp2, dy2, y2 = (refs[2 * i:2 * i + 2] for i in range(7))
        lse_ref = refs[14]
        dq_ref, dk_ref, dv_ref, dq_st, dk_st, dv_st, dkc, dvc = refs[15:]
        u = pl.program_id(0)

        @pl.when(u == 0)
        def _():
            dkc[...] = jnp.zeros_like(dkc)
            dvc[...] = jnp.zeros_like(dvc)

        @pl.when(u < nu)
        def _():
            valid = _band_mask(u)
            lane = lax.broadcasted_iota(jnp.int32, (BLK, GW), 1)

            def band(r, carry):
                q = _get_rows(q2, r, dil) * QK_SCALE
                kb = jnp.concatenate([_get_rows(kp2, r, dil), _get_rows(ko2, r, dil)], axis=0).astype(BF16)
                vb = jnp.concatenate([_get_rows(vp2, r, dil), _get_rows(vo2, r, dil)], axis=0).astype(BF16)
                dyv = _get_rows(dy2, r, dil)
                yv = _get_rows(y2, r, dil)
                lse_v = lse_ref[pl.ds(r, BLK, stride=dil), :]
                heads = [(lane >= h * HD) & (lane < (h + 1) * HD) for h in range(NH)]
                qs = jnp.concatenate([jnp.where(hm, q, 0.0) for hm in heads], axis=0).astype(BF16)
                dys = jnp.concatenate([jnp.where(hm, dyv, 0.0) for hm in heads], axis=0)
                lse_s = jnp.concatenate([lse_v[:, 16 * h:16 * h + 1] for h in range(NH)], axis=0)
                s = lax.dot_general(qs, kb, NT, preferred_element_type=F32)
                p = jnp.where(jnp.concatenate([valid] * NH, axis=0), jnp.exp(s - lse_s), 0.0)
                delta = jnp.sum(dys * jnp.concatenate([yv] * NH, axis=0), axis=-1, keepdims=True)
                dys16 = dys.astype(BF16)
                dp = lax.dot_general(dys16, vb, NT, preferred_element_type=F32)
                ds = (p * (dp - delta)).astype(BF16)
                dq_all = jnp.dot(ds, kb, preferred_element_type=F32)
                dq = jnp.zeros((BLK, GW), F32)
                for h, hm in enumerate(heads):
                    dq = jnp.where(hm, dq_all[h * BLK:(h + 1) * BLK], dq)
                dkb = lax.dot_general(ds, qs, TN, preferred_element_type=F32)
                dvb = lax.dot_general(p.astype(BF16), dys16, TN, preferred_element_type=F32)
                _put_rows(dq_st, r, dil, dq * QK_SCALE)
                _put_rows(dk_st, r, dil, _get_rows((dkc.at[0], dkc.at[1]), r, dil) + dkb[:BLK])
                _put_rows(dv_st, r, dil, _get_rows((dvc.at[0], dvc.at[1]), r, dil) + dvb[:BLK])
                _put_rows(dkc, r, dil, dkb[BLK:])
                _put_rows(dvc, r, dil, dvb[BLK:])
                return carry

            lax.fori_loop(0, dil, band, 0, unroll=min(dil, 4))
            _unstage(dq_st, dq_ref)

            @pl.when(u > 0)
            def _():
                _unstage(dk_st, dk_ref)
                _unstage(dv_st, dv_ref)

        @pl.when(u == nu)
        def _():
            _unstage(dkc, dk_ref)
            _unstage(dvc, dv_ref)

    blocks = (15 + 6 + 5) * unit * 128 * 4 * 2
    return _pc(body, name=name, grid=(nu + 1,), in_specs=in_specs, out_specs=[tok, kv_out, kv_out],
               out_shape=[_sds((t_dim, GW), F32)] * 3, scratch=[pltpu.VMEM((2, unit, 128), F32)] * 5,
               vmem=blocks + (12 << 20))(*([z] * 10), dy, dy, y, y, lse)


_INV_SQRT2 = 1.0 / math.sqrt(2.0)
_INV_SQRT2PI = 1.0 / math.sqrt(2.0 * math.pi)


def _gelu(v):
    return 0.5 * v * (1.0 + lax.erf(v * _INV_SQRT2))


def _gelu_grad(v):
    return 0.5 * (1.0 + lax.erf(v * _INV_SQRT2)) + v * (_INV_SQRT2PI * jnp.exp(-0.5 * v * v))


def _gmlp_core(z_refs, lng, lnb, ws_ref, bs_ref):
    u_pre = jnp.concatenate([z_refs[0][...], z_refs[1][...]], axis=1).astype(F32)
    v_pre = jnp.concatenate([z_refs[2][...], z_refs[3][...]], axis=1).astype(F32)
    u = _gelu(u_pre)
    v = _gelu(v_pre)
    xc = v - jnp.mean(v, axis=-1, keepdims=True)
    rstd = lax.rsqrt(jnp.mean(xc * xc, axis=-1, keepdims=True) + EPS)
    xhat = xc * rstd
    vn = xhat * lng + lnb
    ti = lax.broadcasted_iota(jnp.int32, (BLK, BLK), 0)
    si = lax.broadcasted_iota(jnp.int32, (BLK, BLK), 1)
    causal = ti >= si
    vn16 = vn.astype(BF16)
    ws = [jnp.where(causal, ws_ref[g], 0.0).astype(BF16) for g in range(4)]
    mixed = jnp.concatenate(
        [jnp.dot(ws[g], vn16[:, g * BLK:(g + 1) * BLK], preferred_element_type=F32) + bs_ref[g] for g in range(4)], axis=1)
    return u_pre, v_pre, u, xhat, rstd, vn16, ws, causal, mixed


def _gmlp_specs():
    z_specs = [_bs((BLK, GW), (lambda i, c=c: (i, UV_B + c))) for c in range(4)]
    par_specs = [_bs((1, BW), lambda i: (0, 0)), _bs((1, BW), lambda i: (0, 0)),
                 _bs((4, BLK, BLK), lambda i: (0, 0, 0)), _bs((4, BLK, 1), lambda i: (0, 0, 0))]
    return z_specs, par_specs


def _gmlp_fwd(z, lng, lnb, ws, bs, name):
    t_dim = z.shape[0]
    z_specs, par_specs = _gmlp_specs()

    def body(z0, z1, z2, z3, lng_ref, lnb_ref, ws_ref, bs_ref, o_ref):
        core = _gmlp_core((z0, z1, z2, z3), lng_ref[...], lnb_ref[...], ws_ref, bs_ref)
        o_ref[...] = (core[2] * core[8]).astype(BF16)

    return _pc(body, name=name, grid=(t_dim // BLK,), in_specs=z_specs + par_specs,
               out_specs=_bs((BLK, BW), lambda i: (i, 0)), out_shape=_sds((t_dim, BW), BF16))(z, z, z, z, lng, lnb, ws, bs)


def _gmlp_bwd(z, dy, lng, lnb, ws, bs, name):
    t_dim = z.shape[0]
    z_specs, par_specs = _gmlp_specs()

    def body(z0, z1, z2, z3, lng_ref, lnb_ref, ws_ref, bs_ref, dy_ref, duv_ref, dws_ref, dbs_ref, dlng_ref, dlnb_ref):
        lng_v = lng_ref[...]
        u_pre, v_pre, u, xhat, rstd, vn16, wsm, causal, mixed = _gmlp_core((z0, z1, z2, z3), lng_v, lnb_ref[...], ws_ref, bs_ref)
        dyv = dy_ref[...].astype(F32)
        du = dyv * mixed
        dmixed = dyv * u

        @pl.when(pl.program_id(0) == 0)
        def _():
            dws_ref[...] = jnp.zeros_like(dws_ref)
            dbs_ref[...] = jnp.zeros_like(dbs_ref)
            dlng_ref[...] = jnp.zeros_like(dlng_ref)
            dlnb_ref[...] = jnp.zeros_like(dlnb_ref)

        dvn_parts = []
        for g in range(4):
            dm = dmixed[:, g * BLK:(g + 1) * BLK]
            dm16 = dm.astype(BF16)
            dw = lax.dot_general(dm16, vn16[:, g * BLK:(g + 1) * BLK], NT, preferred_element_type=F32)
            dws_ref[g] += jnp.where(causal, dw, 0.0)
            dbs_ref[g] += jnp.sum(dm.T, axis=0, keepdims=True)
            dvn_parts.append(lax.dot_general(wsm[g], dm16, TN, preferred_element_type=F32))
        dvn = jnp.concatenate(dvn_parts, axis=1)
        dlng_ref[...] += jnp.sum(dvn * xhat, axis=0, keepdims=True)
        dlnb_ref[...] += jnp.sum(dvn, axis=0, keepdims=True)
        dxh = dvn * lng_v
        dv = rstd * (dxh - jnp.mean(dxh, axis=-1, keepdims=True) - xhat * jnp.mean(dxh * xhat, axis=-1, keepdims=True))
        duv_ref[:, :BW] = (du * _gelu_grad(u_pre)).astype(BF16)
        duv_ref[:, BW:] = (dv * _gelu_grad(v_pre)).astype(BF16)

    return _pc(body, name=name, grid=(t_dim // BLK,),
               in_specs=z_specs + par_specs + [_bs((BLK, BW), lambda i: (i, 0))],
               out_specs=[_bs((BLK, 2 * BW), lambda i: (i, 0)), _bs((4, BLK, BLK), lambda i: (0, 0, 0)),
                          _bs((4, 1, BLK), lambda i: (0, 0, 0)), _bs((1, BW), lambda i: (0, 0)), _bs((1, BW), lambda i: (0, 0))],
               out_shape=[_sds((t_dim, 2 * BW), BF16), _sds((4, BLK, BLK), F32), _sds((4, 1, BLK), F32),
                          _sds((1, BW), F32), _sds((1, BW), F32)])(z, z, z, z, lng, lnb, ws, bs, dy)


def _pool_core(x, gi):
    t_dim = x.shape[0]
    w = jnp.left_shift(2, gi)
    row = lax.broadcasted_iota(jnp.int32, x.shape, 0)
    s = x
    for k in (1, 2, 4, 8):
        sh = jnp.where(row >= k, pltpu.roll(s, k, 0), 0.0)
        s = jnp.where(k < w, s + sh, s)
    cnt = jnp.minimum(row + 1, w).astype(F32)
    return s / cnt - x, cnt, row, w, t_dim


def _pool_fwd(z, cw, cs, name):
    t_dim = z.shape[0]

    def body(z_ref, cw_ref, cs_ref, o_ref):
        pooled = _pool_core(z_ref[...].astype(F32), pl.program_id(0))[0]
        y = jnp.dot(pooled.astype(BF16), cw_ref[0].astype(BF16), preferred_element_type=F32) * cs_ref[0]
        o_ref[...] = y.astype(BF16)

    return _pc(body, name=name, grid=(4,),
               in_specs=[_bs((t_dim, BLK), lambda g: (0, C_B128 + g)), _bs((1, BLK, BLK), lambda g: (g, 0, 0)),
                         _bs((1, 1, BLK), lambda g: (g, 0, 0))],
               out_specs=_bs((t_dim, BLK), lambda g: (0, g)), out_shape=_sds((t_dim, BW), BF16),
               vmem=40 << 20)(z, cw, cs)


def _pool_bwd(z, dy, cw, cs, name):
    t_dim = z.shape[0]

    def body(z_ref, dy_ref, cw_ref, cs_ref, dx_ref, dcw_ref, dcs_ref):
        pooled, cnt, row, w, _ = _pool_core(z_ref[...].astype(F32), pl.program_id(0))
        p16 = pooled.astype(BF16)
        cw16 = cw_ref[0].astype(BF16)
        dyv = dy_ref[...].astype(F32)
        lin = jnp.dot(p16, cw16, preferred_element_type=F32)
        dcs_ref[0] = jnp.sum(dyv * lin, axis=0, keepdims=True)
        dys = (dyv * cs_ref[0]).astype(BF16)
        dcw_ref[0] = lax.dot_general(p16, dys, TN, preferred_element_type=F32)
        dpool = lax.dot_general(dys, cw16, NT, preferred_element_type=F32)
        sb = dpool / cnt
        for k in (1, 2, 4, 8):
            sh = jnp.where(row < t_dim - k, pltpu.roll(sb, t_dim - k, 0), 0.0)
            sb = jnp.where(k < w, sb + sh, sb)
        dx_ref[...] = (sb - dpool).astype(BF16)

    return _pc(body, name=name, grid=(4,),
               in_specs=[_bs((t_dim, BLK), lambda g: (0, C_B128 + g)), _bs((t_dim, BLK), lambda g: (0, g)),
                         _bs((1, BLK, BLK), lambda g: (g, 0, 0)), _bs((1, 1, BLK), lambda g: (g, 0, 0))],
               out_specs=[_bs((t_dim, BLK), lambda g: (0, g)), _bs((1, BLK, BLK), lambda g: (g, 0, 0)),
                          _bs((1, 1, BLK), lambda g: (g, 0, 0))],
               out_shape=[_sds((t_dim, BW), BF16), _sds((4, BLK, BLK), F32), _sds((4, 1, BLK), F32)],
               vmem=48 << 20)(z, dy, cw, cs)


GATE_W = 2 * GW


def _gates_specs(tm):
    per_branch = D // GATE_W
    z_specs = [_bs((tm, GATE_W), (lambda j, i, b=b: (i, G_B * GW // GATE_W + per_branch * b + j))) for b in range(3)]
    bg_specs = [_bs((1, GATE_W), (lambda j, i, b=b: (0, per_branch * b + j))) for b in range(3)]
    tile = _bs((tm, GATE_W), lambda j, i: (i, j))
    return z_specs, bg_specs, tile


def _gates_fwd(z, bg, pa, pb, pc, name, tm=512):
    t_dim = z.shape[0]
    z_specs, bg_specs, tile = _gates_specs(tm)

    def body(za, zb, zc, ba, bb, bc, pa_ref, pb_ref, pc_ref, o_ref):
        o_ref[...] = (_sigmoid(za[...] + ba[...]) * pa_ref[...] + _sigmoid(zb[...] + bb[...]) * pb_ref[...]
                      + _sigmoid(zc[...] + bc[...]) * pc_ref[...]).astype(BF16)

    return _pc(body, name=name, grid=(D // GATE_W, t_dim // tm), in_specs=z_specs + bg_specs + [tile] * 3,
               out_specs=tile, out_shape=_sds((t_dim, D), BF16), vmem=40 << 20)(z, z, z, bg, bg, bg, pa, pb, pc)


def _gates_bwd(z, bg, pa, pb, pc, dm, name, tm=512):
    t_dim = z.shape[0]
    z_specs, bg_specs, tile = _gates_specs(tm)
    bias_out = _bs((1, GATE_W), lambda j, i: (0, j))

    def body(za, zb, zc, ba, bb, bc, pa_ref, pb_ref, pc_ref, dm_ref, dpa, dpb, dpc, dza, dzb, dzc, dba, dbb, dbc):
        dmv = dm_ref[...]
        first = pl.program_id(1) == 0
        for z_ref, b_ref, p_ref, dp_ref, dz_ref, db_ref in ((za, ba, pa_ref, dpa, dza, dba), (zb, bb, pb_ref, dpb, dzb, dbb),
                                                            (zc, bc, pc_ref, dpc, dzc, dbc)):
            gate = _sigmoid(z_ref[...] + b_ref[...])
            dp_ref[...] = (gate * dmv).astype(BF16)
            dz = dmv * p_ref[...] * (gate * (1.0 - gate))
            dz_ref[...] = dz.astype(BF16)

            @pl.when(first)
            def _():
                db_ref[...] = jnp.zeros_like(db_ref)

            db_ref[...] += jnp.sum(dz, axis=0, keepdims=True)

    return _pc(body, name=name, grid=(D // GATE_W, t_dim // tm), in_specs=z_specs + bg_specs + [tile] * 4,
               out_specs=[tile] * 6 + [bias_out] * 3,
               out_shape=[_sds((t_dim, D), BF16)] * 6 + [_sds((1, D), F32)] * 3, vmem=48 << 20)(
                   z, z, z, bg, bg, bg, pa, pb, pc, dm)


def _concat_bf16(parts, name, tm=512):
    t_dim = parts[0].shape[0]
    offs = [sum(p.shape[1] for p in parts[:i]) for i in range(len(parts) + 1)]

    def body(*refs):
        o_ref = refs[-1]
        for ref, off in zip(refs[:-1], offs):
            o_ref[:, off:off + ref.shape[1]] = ref[...].astype(BF16)

    return _pc(body, name=name, grid=(t_dim // tm,), in_specs=[_bs((tm, p.shape[1]), lambda i: (i, 0)) for p in parts],
               out_specs=_bs((tm, offs[-1]), lambda i: (i, 0)), out_shape=_sds((t_dim, offs[-1]), BF16),
               vmem=48 << 20)(*parts)


def _coords():
    return lax.axis_index("x"), lax.axis_index("y"), lax.axis_index("c")


def _peer(k, x, y, c):
    px = 1 - x if k & 4 else x
    py = 1 - y if k & 2 else y
    pc = 1 - c if k & 1 else c
    return (px, py, pc), 4 * px + 2 * py + pc


def _exchange_copies(kind, bufs, send, recv):
    x, y, c = _coords()
    me = 4 * x + 2 * y + c
    na = len(bufs) if kind == "gather" else len(bufs) // 2
    out = []
    for k in range(1, NDEV):
        to, peer = _peer(k, x, y, c)
        for a in range(na):
            sems = dict(send_sem=send.at[a * 7 + k - 1], recv_sem=recv.at[a * 7 + k - 1], device_id=to, device_id_type=MESH)
            if kind == "gather":
                mine = pltpu.make_async_remote_copy(src_ref=bufs[a].at[:, me], dst_ref=bufs[a].at[:, me], **sems)
                theirs = pltpu.make_async_remote_copy(src_ref=bufs[a].at[:, me], dst_ref=bufs[a].at[:, peer], **sems)
            else:
                src, land = bufs[a], bufs[na + a]
                mine = pltpu.make_async_remote_copy(src_ref=src.at[:, peer], dst_ref=land.at[me], **sems)
                theirs = pltpu.make_async_remote_copy(src_ref=src.at[:, peer], dst_ref=land.at[peer], **sems)
            out.append((mine, theirs))
    return out


_HBM = pl.BlockSpec(memory_space=pltpu.HBM)
_SEM = pl.BlockSpec(memory_space=pltpu.SEMAPHORE)
_EFFECT = pltpu.SideEffectType.DATAFLOW_SIDE_EFFECTING


def _exchange_start(kind, arrays, after, name):
    na = len(arrays)
    if kind == "scatter":
        arrays = list(arrays) + [lax.empty((NDEV, s.shape[0]) + s.shape[2:], s.dtype) for s in arrays]
    nb = len(arrays)

    def body(*refs):
        bufs = refs[:nb]
        send, recv = refs[nb + 1], refs[nb + 2]
        token = refs[-1]
        for mine, _ in _exchange_copies(kind, bufs, send, recv):
            mine.start()
        token[...] = jnp.zeros_like(token)

    arrays = [pltpu.with_memory_space_constraint(b, pltpu.HBM) for b in arrays]
    outs = pl.pallas_call(
        body, name=name,
        out_shape=(pltpu.SemaphoreType.DMA((7 * na,)), pltpu.SemaphoreType.DMA((7 * na,)),
                   *[pltpu.HBM(b.shape, b.dtype) for b in arrays], _sds((8, 128), F32)),
        in_specs=[_HBM] * nb + [pl.BlockSpec(memory_space=pl.ANY)],
        out_specs=(_SEM, _SEM, *([_HBM] * nb), pl.BlockSpec(memory_space=pltpu.VMEM)),
        input_output_aliases={i: 2 + i for i in range(nb)},
        compiler_params=pltpu.CompilerParams(has_side_effects=_EFFECT), interpret=False)(*arrays, after)
    return outs[0], outs[1], list(outs[2:2 + nb]), outs[-1]


def _exchange_wait(kind, started, after, name):
    send, recv, arrays, _ = started
    nb = len(arrays)

    def body(*refs):
        for mine, theirs in _exchange_copies(kind, refs[:nb], refs[nb], refs[nb + 1]):
            mine.wait_send()
            theirs.wait_recv()

    outs = pl.pallas_call(
        body, name=name,
        out_shape=tuple(pltpu.HBM(b.shape, b.dtype) for b in arrays),
        in_specs=[_HBM] * nb + [_SEM, _SEM, pl.BlockSpec(memory_space=pl.ANY)],
        out_specs=tuple([_HBM] * nb),
        input_output_aliases={i: i for i in range(nb)},
        compiler_params=pltpu.CompilerParams(has_side_effects=_EFFECT), interpret=False)(*arrays, send, recv, after)
    return list(outs)


def _adam_math(w, g, m, v):
    m2 = ADAM_B1 * m + (1.0 - ADAM_B1) * g
    v2 = ADAM_B2 * v + (1.0 - ADAM_B2) * jnp.square(g)
    m_hat = m2 / (1.0 - ADAM_B1 ** ADAM_STEP)
    v_hat = v2 / (1.0 - ADAM_B2 ** ADAM_STEP)
    return -ADAM_LR * (m_hat / (jnp.sqrt(v_hat) + ADAM_EPS) + ADAM_WD * w), m2, v2


_DIRECT = (1, 4, 2, 6)


def _gather_copies(bufs, send_d, recv_d, send_f=None, recv_f=None):
    x, y, c = _coords()
    me = 4 * x + 2 * y + c
    sibling, _ = _peer(1, x, y, c)
    direct, forward = [], []
    for a, buf in enumerate(bufs):
        for j, k in enumerate(_DIRECT):
            to, peer = _peer(k, x, y, c)
            sems = dict(send_sem=send_d.at[4 * a + j], recv_sem=recv_d.at[4 * a + j], device_id=to, device_id_type=MESH)
            direct.append((pltpu.make_async_remote_copy(src_ref=buf.at[:, me], dst_ref=buf.at[:, me], **sems),
                           pltpu.make_async_remote_copy(src_ref=buf.at[:, me], dst_ref=buf.at[:, peer], **sems)))
        if send_f is None:
            continue
        for j, k in enumerate(_DIRECT[1:]):
            _, near = _peer(k, x, y, c)
            _, far = _peer(k | 1, x, y, c)
            sems = dict(send_sem=send_f.at[3 * a + j], recv_sem=recv_f.at[3 * a + j], device_id=sibling, device_id_type=MESH)
            forward.append((pltpu.make_async_remote_copy(src_ref=buf.at[:, near], dst_ref=buf.at[:, near], **sems),
                            pltpu.make_async_remote_copy(src_ref=buf.at[:, near], dst_ref=buf.at[:, far], **sems)))
    return direct, forward


def _gather_start(lands, after, name):
    na = len(lands)

    def body(*refs):
        direct, _ = _gather_copies(refs[:na], refs[na + 1], refs[na + 2])
        for mine, _ in direct:
            mine.start()
        refs[-1][...] = jnp.zeros_like(refs[-1])

    lands = [pltpu.with_memory_space_constraint(b, pltpu.HBM) for b in lands]
    outs = pl.pallas_call(
        body, name=name,
        out_shape=(pltpu.SemaphoreType.DMA((4 * na,)), pltpu.SemaphoreType.DMA((4 * na,)),
                   *[pltpu.HBM(b.shape, b.dtype) for b in lands], _sds((8, 128), F32)),
        in_specs=[_HBM] * na + [pl.BlockSpec(memory_space=pl.ANY)],
        out_specs=(_SEM, _SEM, *([_HBM] * na), pl.BlockSpec(memory_space=pltpu.VMEM)),
        input_output_aliases={i: 2 + i for i in range(na)},
        compiler_params=pltpu.CompilerParams(has_side_effects=_EFFECT), interpret=False)(*lands, after)
    return outs[0], outs[1], list(outs[2:2 + na]), outs[-1]


def _gather_forward(started, after, name):
    send_d, recv_d, lands, _ = started
    na = len(lands)

    def body(*refs):
        direct, forward = _gather_copies(refs[:na], refs[na], refs[na + 1], refs[2 * na + 3], refs[2 * na + 4])
        for a in range(na):
            for j in range(3):
                direct[4 * a + 1 + j][1].wait_recv()
                forward[3 * a + j][0].start()
        refs[-1][...] = jnp.zeros_like(refs[-1])

    outs = pl.pallas_call(
        body, name=name,
        out_shape=(*[pltpu.HBM(b.shape, b.dtype) for b in lands], pltpu.SemaphoreType.DMA((3 * na,)),
                   pltpu.SemaphoreType.DMA((3 * na,)), _sds((8, 128), F32)),
        in_specs=[_HBM] * na + [_SEM, _SEM, pl.BlockSpec(memory_space=pl.ANY)],
        out_specs=(*([_HBM] * na), _SEM, _SEM, pl.BlockSpec(memory_space=pltpu.VMEM)),
        input_output_aliases={i: i for i in range(na)},
        compiler_params=pltpu.CompilerParams(has_side_effects=_EFFECT), interpret=False)(*lands, send_d, recv_d, after)
    return send_d, recv_d, outs[na], outs[na + 1], list(outs[:na]), outs[-1]


def _gather_wait(forwarded, after, name):
    send_d, recv_d, send_f, recv_f, lands, _ = forwarded
    na = len(lands)

    def body(*refs):
        direct, forward = _gather_copies(refs[:na], refs[na], refs[na + 1], refs[na + 2], refs[na + 3])
        for a in range(na):
            for j in range(4):
                direct[4 * a + j][0].wait_send()
            direct[4 * a][1].wait_recv()
            for j in range(3):
                forward[3 * a + j][0].wait_send()
                forward[3 * a + j][1].wait_recv()

    outs = pl.pallas_call(
        body, name=name, out_shape=tuple(pltpu.HBM(b.shape, b.dtype) for b in lands),
        in_specs=[_HBM] * na + [_SEM] * 4 + [pl.BlockSpec(memory_space=pl.ANY)], out_specs=tuple([_HBM] * na),
        input_output_aliases={i: i for i in range(na)},
        compiler_params=pltpu.CompilerParams(has_side_effects=_EFFECT), interpret=False)(
            *lands, send_d, recv_d, send_f, recv_f, after)
    return list(outs)


def _reduce_update(landed, own, me, piece, w, m, v, layer, prev, transposed, name):
    rows, cols = landed.shape[2], landed.shape[3]
    tc = min(cols, 512 if rows <= FFS else 256)
    assert cols % tc == 0 and w.shape[1:] == ((cols, rows) if transposed else (rows, cols))
    lane_chunks = [(c0, min(128, rows - c0)) for c0 in range(0, rows, 128)]

    def body(me_ref, b_ref, own_ref, w_ref, m_ref, v_ref, *rest):
        outs = rest[-4:]
        mine = me_ref[0]
        s = None
        for d in range(NDEV):
            other = jnp.where(mine == d, (d + 1) % NDEV, d)
            part = jnp.where(mine == d, own_ref[0, 0], b_ref[other, 0]).astype(F32)
            s = part if s is None else s + part
        if not transposed:
            for o, val in zip(outs, (s,) + _adam_math(w_ref[0], s, m_ref[0], v_ref[0])):
                o[0] = val
            return
        for c0, wd in lane_chunks:
            chunk = s[c0:c0 + wd]
            if wd < 128:
                chunk = jnp.concatenate([chunk, jnp.zeros((128 - wd, tc), F32)], axis=0)
            g = chunk.T[:, :wd]
            sl = (0, slice(None), slice(c0, c0 + wd))
            for o, val in zip(outs, (g,) + _adam_math(w_ref[sl], g, m_ref[sl], v_ref[sl])):
                o[sl] = val

    if transposed:
        nat = _bs((1, tc, rows), lambda j, me_ref: (layer, j, 0))
    else:
        nat = _bs((1, rows, tc), lambda j, me_ref: (layer, 0, j))
    in_specs = [_bs((NDEV, 1, rows, tc), lambda j, me_ref: (0, piece, 0, j)),
                _bs((1, 1, rows, tc), lambda j, me_ref: (piece, me_ref[0], 0, j)), nat, nat, nat]
    args = [me, landed, own, w, m, v]
    alias = {}
    if prev is not None:
        in_specs += [pl.BlockSpec(memory_space=pl.ANY)] * 4
        args += list(prev)
        alias = {6 + i: i for i in range(4)}
    grid_spec = pltpu.PrefetchScalarGridSpec(num_scalar_prefetch=1, grid=(cols // tc,), in_specs=in_specs, out_specs=[nat] * 4)
    return pl.pallas_call(body, name=name, grid_spec=grid_spec, out_shape=[_sds(w.shape, F32)] * 4,
                          input_output_aliases=alias, interpret=False)(*args)


SMALL_LAYER = ("ffn1_norm", "mix_norm", "b_gate", "b_ln_g", "b_ln_b", "b_w_s", "b_b_s", "c_w", "c_scale", "ffn2_norm")
SMALL_ROWS = {"ffn1_norm": 8, "mix_norm": 8, "b_gate": 24, "b_ln_g": 4, "b_ln_b": 4, "b_w_s": 512, "b_b_s": 4, "c_w": 512,
              "c_scale": 4, "ffn2_norm": 8, "final_norm": 8}
LAYER_ROWS = sum(SMALL_ROWS[k] for k in SMALL_LAYER)


def _small_offset(name):
    if name == "final_norm":
        return LAYER_ROWS
    return sum(SMALL_ROWS[k] for k in SMALL_LAYER[:SMALL_LAYER.index(name)])


def _pack_small_grads(layer_grads, final, me, name):
    flat, where = [], []
    for k in SMALL_LAYER:
        parts = layer_grads[k] if k == "b_gate" else [layer_grads[k]]
        for i, part in enumerate(parts):
            flat.append(part)
            where.append(_small_offset(k) + i * (D // 128))
    if final is not None:
        flat.append(final)
        where.append(_small_offset("final_norm"))
    pack_rows = LAYER_ROWS + (SMALL_ROWS["final_norm"] if final is not None else 0)

    def body(me_ref, *refs):
        del me_ref
        o_ref = refs[-1]
        for ref, off in zip(refs[:-1], where):
            if len(ref.shape) == 2:
                for j in range(ref.shape[1] // 128):
                    o_ref[0, 0, off + j:off + j + 1, :] = ref[:, 128 * j:128 * (j + 1)]
            else:
                rows = ref.shape[1]
                for g in range(4):
                    o_ref[0, 0, off + g * rows:off + (g + 1) * rows, :] = ref[g]

    grid_spec = pltpu.PrefetchScalarGridSpec(
        num_scalar_prefetch=1, grid=(1,),
        in_specs=[pl.BlockSpec(a.shape, (lambda i, me_ref, nd=a.ndim: (0,) * nd)) for a in flat],
        out_specs=_bs((1, 1, pack_rows, 128), lambda i, me_ref: (0, me_ref[0], 0, 0)))
    return pl.pallas_call(body, name=name, grid_spec=grid_spec, out_shape=_sds((1, NDEV, pack_rows, 128), F32),
                          interpret=False)(me, *flat)


def _adamw_small(gathered, w, mom, var, name):
    names = SMALL_LAYER + ("final_norm",)
    ins = list(gathered) + [t[k] for k in names for t in (w, mom, var)]

    def body(*refs):
        outs = refs[len(ins):]

        def rows(l, off, n):
            land = refs[l]
            s = land[0, 0, off:off + n, :]
            for d in range(1, NDEV):
                s = s + land[0, d, off:off + n, :]
            return s

        def as_row(l, off, n):
            s = rows(l, off, n)
            return jnp.concatenate([s[j:j + 1, :] for j in range(n)], axis=1)

        def update(i, idx, g):
            wr, mr, vr = (refs[DEPTH + 3 * i + t] for t in range(3))
            d, m2, v2 = _adam_math(wr[idx], g, mr[idx], vr[idx])
            for o, val in zip(outs[4 * i:4 * i + 4], (g, d, m2, v2)):
                o[idx] = val

        for i, k in enumerate(names):
            n = SMALL_ROWS[k]
            off = _small_offset(k)
            if k == "final_norm":
                update(i, (slice(0, 1),), as_row(DEPTH - 1, off, n))
                continue
            for l in range(DEPTH):
                if k in ("b_w_s", "c_w"):
                    for g in range(4):
                        update(i, (l, g), rows(l, off + g * BLK, BLK))
                elif k in ("b_b_s", "c_scale"):
                    update(i, (l,), rows(l, off, n))
                else:
                    update(i, (slice(l, l + 1),), as_row(l, off, n))

    out_shape = [_sds(w[k].shape, F32) for k in names for _ in range(4)]
    outs = pl.pallas_call(body, name=name, out_shape=out_shape,
                          compiler_params=pltpu.CompilerParams(vmem_limit_bytes=48 << 20), interpret=False)(*ins)
    return tuple({k: outs[4 * i + t] for i, k in enumerate(names)} for t in range(4))


PG, PD = 0, 2


def _ffn_fwd(x, h, wf, tag, mid, next_gain):
    g_pre, u_pre, act = _gu_swiglu(h, wf, tm=2048, tn=256, b_row0=PG * FF, name=f"gu_{tag}")
    token = mid(act)
    out = _mm(act, wf, mode="nn", tm=512, tn=D, tk=FF, n=D, b_row0=PD * FF, out_dtype=F32, scale=0.5, res=x, after=token,
              norm_gain=next_gain, name=f"down_{tag}")
    out, h_next = out if next_gain is not None else (out, None)
    return out, h_next, (x, h, g_pre, u_pre, act)


def _ffn_bwd(dout, saved, gain, wf, tag, after, emit):
    x, h, g_pre, u_pre, act = saved
    grf = lax.empty((3, NDEV, FFS, D), BF16)
    dgu = _dact_dswiglu(dout, wf, g_pre, u_pre, tm=256, b_row0=PD * FF, after=after, name=f"dgu_{tag}")
    grf = _wgrad(act, dout, grf, piece0=PD, rows=FFS, tm=FF // 2, tn=D, tk=1024, scale=0.5, name=f"wg_down_{tag}")
    grf = _wgrad(dgu, h, grf, piece0=PG, rows=FFS, tm=FF // 2, tn=D, tk=2048, name=f"wg_gu_{tag}")
    token = emit([grf])
    dx, dgain = _mm_drms(dgu, wf, x, gain, dout, tm=256, b_row0=PG * FF, after=token, name=f"dx_{tag}")
    return dx, dgain, token


def _mixer_fwd(x, h, sp, ww, wo, wa, wbc, tag, mid, next_gain):
    zq = _mm(h, ww, mode="nt", tm=1024, tn=1152, tk=D, n=QKV_W, out_dtype=F32, name=f"zq_{tag}")
    z = _mm(h, ww, mode="nt", tm=1024, tn=1152, tk=D, n=INW - QKV_W, b_row0=QKV_W, out_dtype=BF16, name=f"z_{tag}")
    token = mid(z)
    acc, ml = _attn_fwd(zq, None, None, gi=0, first=True, final=False, name=f"attn0_{tag}")
    acc, ml = _attn_fwd(zq, acc, ml, gi=1, first=False, final=False, name=f"attn1_{tag}")
    ya, lse = _attn_fwd(zq, acc, ml, gi=2, first=False, final=True, name=f"attn2_{tag}")
    yb = _gmlp_fwd(z, sp["b_ln_g"], sp["b_ln_b"], sp["b_w_s"], sp["b_b_s"], f"gmlp_{tag}")
    yc = _pool_fwd(z, sp["c_w"], sp["c_scale"], f"pool_{tag}")
    pa = _mm(ya, wa, mode="nt", tm=1024, tn=D, tk=GW, n=D, out_dtype=BF16, after=token, name=f"proj_a_{tag}")
    pb = _mm(yb, wbc, mode="nt", tm=1024, tn=D, tk=BW, n=D, b_row0=0, out_dtype=BF16, name=f"proj_b_{tag}")
    pc = _mm(yc, wbc, mode="nt", tm=1024, tn=D, tk=BW, n=D, b_row0=D, out_dtype=BF16, name=f"proj_c_{tag}")
    merged = _gates_fwd(z, sp["b_gate"], pa, pb, pc, f"gates_{tag}")
    out, h_next = _mm(merged, wo, mode="nn", tm=1024, tn=D, tk=D, n=D, out_dtype=F32, res=x, norm_gain=next_gain,
                      name=f"out_{tag}")
    return out, h_next, (x, h, zq, z, ya, lse, yb, yc, pa, pb, pc, merged)


def _mixer_bwd(dout, saved, sp, ww, wo, wa, wbc, tag, after, emit):
    x, h, zq, z, ya, lse, yb, yc, pa, pb, pc, merged = saved
    grw = lax.empty((1, NDEV, INS, D), BF16)
    gro = lax.empty((1, NDEV, DS, D), BF16)
    gra = lax.empty((1, NDEV, DS, GW), BF16)
    grbc = lax.empty((2, NDEV, DS, BW), BF16)
    dm = _mm(dout, wo, mode="nt", tm=1024, tn=D, tk=D, n=D, out_dtype=F32, after=after, name=f"dmerged_{tag}")
    gro = _wgrad(merged, dout, gro, piece0=0, rows=DS, tm=D, tn=D, tk=1024, name=f"wg_out_{tag}")
    dpa, dpb, dpc, dza, dzb, dzc, dba, dbb, dbc = _gates_bwd(z, sp["b_gate"], pa, pb, pc, dm, f"dgates_{tag}")
    dya = _mm(dpa, wa, mode="nn", tm=2048, tn=GW, tk=D, n=GW, out_dtype=F32, name=f"dya_{tag}")
    dyb = _mm(dpb, wbc, mode="nn", tm=2048, tn=BW, tk=D, n=BW, b_row0=0, out_dtype=F32, name=f"dyb_{tag}")
    dyc = _mm(dpc, wbc, mode="nn", tm=2048, tn=BW, tk=D, n=BW, b_row0=D, out_dtype=F32, name=f"dyc_{tag}")
    gra = _wgrad(dpa, ya, gra, piece0=0, rows=DS, tm=D, tn=GW, tk=1024, name=f"wg_pa_{tag}")
    grbc = _wgrad(dpb, yb, grbc, piece0=0, rows=DS, tm=D, tn=BW, tk=1024, name=f"wg_pb_{tag}")
    grbc = _wgrad(dpc, yc, grbc, piece0=1, rows=DS, tm=D, tn=BW, tk=1024, name=f"wg_pc_{tag}")
    dqkv = [_attn_bwd(zq, dya, ya, lse, gi=gi, name=f"dattn{gi}_{tag}") for gi in range(3)]
    duv, dws, dbs, dlng, dlnb = _gmlp_bwd(z, dyb, sp["b_ln_g"], sp["b_ln_b"], sp["b_w_s"], sp["b_b_s"], f"dgmlp_{tag}")
    dxc, dcw, dcs = _pool_bwd(z, dyc, sp["c_w"], sp["c_scale"], f"dpool_{tag}")
    dz = _concat_bf16([dqkv[gi][part] for part in range(3) for gi in range(3)] + [duv, dxc, dza, dzb, dzc], f"dz_{tag}")
    grw = _wgrad(dz, h, grw, piece0=0, rows=INS, tm=INW // 2, tn=512, tk=1024, name=f"wg_in_{tag}")
    token = emit([grw, gro, gra, grbc])
    dx, dgain = _mm_drms(dz, ww, x, sp["mix_norm"], dout, tm=256, b_row0=0, after=token, name=f"dx_mix_{tag}")
    small = {"mix_norm": dgain, "b_gate": [dba, dbb, dbc], "b_ln_g": dlng, "b_ln_b": dlnb,
             "b_w_s": dws, "b_b_s": dbs, "c_w": dcw, "c_scale": dcs}
    return dx, small, token


PARTS = ("f1", "mix", "f2")


def _forward(xs, tgt, weights_of, pass_on, small, final_gain):
    saved, views = [], []
    cur = xs
    h = _rms_fwd(xs, small[0]["ffn1_norm"], "rms_first")
    for l in range(DEPTH):
        last = l + 1 == DEPTH
        (wf1,) = weights_of(l, "f1", cur)
        cur, h, s1 = _ffn_fwd(cur, h, wf1, f"f1l{l}", lambda t: pass_on(l, "mix", t), small[l]["mix_norm"])
        wmix = weights_of(l, "mix", cur)
        cur, h, s2 = _mixer_fwd(cur, h, small[l], *wmix, f"l{l}", lambda t: pass_on(l, "f2", t), small[l]["ffn2_norm"])
        (wf2,) = weights_of(l, "f2", cur)
        cur, h, s3 = _ffn_fwd(cur, h, wf2, f"f2l{l}", lambda t: None if last else pass_on(l + 1, "f1", t),
                              None if last else small[l + 1]["ffn1_norm"])
        saved.append((s1, s2, s3))
        views.append((wf1, wmix, wf2))
    loss_part, dcur, dfinal = _final_loss(cur, final_gain, tgt, "final_loss")
    return loss_part, dcur, dfinal, saved, views


def _backward_layer(dcur, saved_l, small_l, views_l, l, emit, after):
    wf1, wmix, wf2 = views_l
    s1, s2, s3 = saved_l
    dcur, dn2, token = _ffn_bwd(dcur, s3, small_l["ffn2_norm"], wf2, f"f2l{l}", after, lambda g: emit("f2", g))
    dcur, sg, token = _mixer_bwd(dcur, s2, small_l, *wmix, f"l{l}", token, lambda g: emit("mix", g))
    dcur, dn1, token = _ffn_bwd(dcur, s1, small_l["ffn1_norm"], wf1, f"f1l{l}", token, lambda g: emit("f1", g))
    sg["ffn1_norm"] = dn1
    sg["ffn2_norm"] = dn2
    return dcur, sg, token


BIG = ("ffn1_w_gate", "ffn1_w_up", "ffn1_w_down", "w_in", "w_proj_a", "w_proj_b", "w_proj_c", "w_out",
       "ffn2_w_gate", "ffn2_w_up", "ffn2_w_down")
ALL_WEIGHTS = ("ffn1_norm", "ffn1_w_gate", "ffn1_w_up", "ffn1_w_down", "mix_norm", "w_in", "b_gate", "b_ln_g", "b_ln_b",
               "b_w_s", "b_b_s", "c_w", "c_scale", "w_proj_a", "w_proj_b", "w_proj_c", "w_out", "ffn2_norm", "ffn2_w_gate",
               "ffn2_w_up", "ffn2_w_down", "final_norm")


def _place_shards(w, l, part, me, after, name):
    tr = lambda a: a.transpose(0, 2, 1)
    if part == "mix":
        srcs = [tr(w["w_in"]), w["w_out"], tr(w["w_proj_a"]), tr(w["w_proj_b"]), tr(w["w_proj_c"])]
        groups = [[0], [1], [2], [3, 4]]
    else:
        f = "ffn1" if part == "f1" else "ffn2"
        srcs = [tr(w[f + "_w_gate"]), tr(w[f + "_w_up"]), w[f + "_w_down"]]
        groups = [[0, 1, 2]]

    def body(me_ref, *refs):
        del me_ref
        outs = refs[len(srcs) + 1:]
        for o_ref, members in zip(outs, groups):
            for p, i in enumerate(members):
                o_ref[p, 0] = refs[i][0].astype(BF16)

    out_shape = [_sds((len(g), NDEV) + srcs[g[0]].shape[1:], BF16) for g in groups]
    grid_spec = pltpu.PrefetchScalarGridSpec(
        num_scalar_prefetch=1, grid=(1,),
        in_specs=[_bs((1,) + s.shape[1:], lambda i, me_ref: (l, 0, 0)) for s in srcs] + [pl.BlockSpec(memory_space=pl.ANY)],
        out_specs=[_bs((len(g), 1) + srcs[g[0]].shape[1:], lambda i, me_ref: (0, me_ref[0], 0, 0)) for g in groups])
    return pl.pallas_call(body, name=name, grid_spec=grid_spec, out_shape=out_shape,
                          compiler_params=pltpu.CompilerParams(vmem_limit_bytes=40 << 20), interpret=False)(me, *srcs, after)


def _part_views(part, gathered):
    if part == "mix":
        pw, po, pa, pbc = gathered
        return [pw.reshape(INW, D), po.reshape(D, D), pa.reshape(D, GW), pbc.reshape(2 * D, BW)]
    return [gathered[0].reshape(3 * FF, D)]


def _part_pieces(part):
    if part == "mix":
        return [(0, 0, "w_in", True), (1, 0, "w_out", False), (2, 0, "w_proj_a", True), (3, 0, "w_proj_b", True),
                (3, 1, "w_proj_c", True)]
    f = "ffn1" if part == "f1" else "ffn2"
    return [(0, 0, f + "_w_gate", True), (0, 1, f + "_w_up", True), (0, 2, f + "_w_down", False)]


def _layer_small(w, l):
    r = lambda a, shape: a.reshape(shape)
    return {"ffn1_norm": r(w["ffn1_norm"][l], (1, D)), "mix_norm": r(w["mix_norm"][l], (1, D)),
            "b_gate": r(w["b_gate"][l], (1, 3 * D)), "b_ln_g": r(w["b_ln_g"][l], (1, BW)), "b_ln_b": r(w["b_ln_b"][l], (1, BW)),
            "b_w_s": w["b_w_s"][l], "b_b_s": r(w["b_b_s"][l], (4, BLK, 1)), "c_w": w["c_w"][l],
            "c_scale": r(w["c_scale"][l], (4, 1, BLK)), "ffn2_norm": r(w["ffn2_norm"][l], (1, D))}


def kernel(x, ffn1_norm, ffn1_w_gate, ffn1_w_up, ffn1_w_down, mix_norm, w_in, b_gate, b_ln_g, b_ln_b, b_w_s, b_b_s, c_w, c_scale, w_proj_a, w_proj_b, w_proj_c, w_out, ffn2_norm, ffn2_w_gate, ffn2_w_up, ffn2_w_down, final_norm, loss_target, m_ffn1_norm, m_ffn1_w_gate, m_ffn1_w_up, m_ffn1_w_down, m_mix_norm, m_w_in, m_b_gate, m_b_ln_g, m_b_ln_b, m_b_w_s, m_b_b_s, m_c_w, m_c_scale, m_w_proj_a, m_w_proj_b, m_w_proj_c, m_w_out, m_ffn2_norm, m_ffn2_w_gate, m_ffn2_w_up, m_ffn2_w_down, m_final_norm, v_ffn1_norm, v_ffn1_w_gate, v_ffn1_w_up, v_ffn1_w_down, v_mix_norm, v_w_in, v_b_gate, v_b_ln_g, v_b_ln_b, v_b_w_s, v_b_b_s, v_c_w, v_c_scale, v_w_proj_a, v_w_proj_b, v_w_proj_c, v_w_out, v_ffn2_norm, v_ffn2_w_gate, v_ffn2_w_up, v_ffn2_w_down, v_final_norm):
    w = dict(ffn1_norm=ffn1_norm, ffn1_w_gate=ffn1_w_gate, ffn1_w_up=ffn1_w_up, ffn1_w_down=ffn1_w_down, mix_norm=mix_norm,
             w_in=w_in, b_gate=b_gate, b_ln_g=b_ln_g, b_ln_b=b_ln_b, b_w_s=b_w_s, b_b_s=b_b_s, c_w=c_w, c_scale=c_scale,
             w_proj_a=w_proj_a, w_proj_b=w_proj_b, w_proj_c=w_proj_c, w_out=w_out, ffn2_norm=ffn2_norm, ffn2_w_gate=ffn2_w_gate,
             ffn2_w_up=ffn2_w_up, ffn2_w_down=ffn2_w_down, final_norm=final_norm)
    mom = dict(ffn1_norm=m_ffn1_norm, ffn1_w_gate=m_ffn1_w_gate, ffn1_w_up=m_ffn1_w_up, ffn1_w_down=m_ffn1_w_down,
               mix_norm=m_mix_norm, w_in=m_w_in, b_gate=m_b_gate, b_ln_g=m_b_ln_g, b_ln_b=m_b_ln_b, b_w_s=m_b_w_s, b_b_s=m_b_b_s,
               c_w=m_c_w, c_scale=m_c_scale, w_proj_a=m_w_proj_a, w_proj_b=m_w_proj_b, w_proj_c=m_w_proj_c, w_out=m_w_out,
               ffn2_norm=m_ffn2_norm, ffn2_w_gate=m_ffn2_w_gate, ffn2_w_up=m_ffn2_w_up, ffn2_w_down=m_ffn2_w_down,
               final_norm=m_final_norm)
    var = dict(ffn1_norm=v_ffn1_norm, ffn1_w_gate=v_ffn1_w_gate, ffn1_w_up=v_ffn1_w_up, ffn1_w_down=v_ffn1_w_down,
               mix_norm=v_mix_norm, w_in=v_w_in, b_gate=v_b_gate, b_ln_g=v_b_ln_g, b_ln_b=v_b_ln_b, b_w_s=v_b_w_s, b_b_s=v_b_b_s,
               c_w=v_c_w, c_scale=v_c_scale, w_proj_a=v_w_proj_a, w_proj_b=v_w_proj_b, w_proj_c=v_w_proj_c, w_out=v_w_out,
               ffn2_norm=v_ffn2_norm, ffn2_w_gate=v_ffn2_w_gate, ffn2_w_up=v_ffn2_w_up, ffn2_w_down=v_ffn2_w_down,
               final_norm=v_final_norm)

    t_dim = x.shape[1]
    xs = x.reshape(t_dim, D)
    tgt = loss_target.reshape(t_dim, D)

    me = 4 * lax.axis_index("x") + 2 * lax.axis_index("y") + lax.axis_index("c")
    me_arr = me.astype(jnp.int32).reshape(1)
    gathers = {}
    token = jnp.zeros((8, 128), F32)
    for l in range(DEPTH):
        for part in PARTS:
            lands = _place_shards(w, l, part, me_arr, token, f"place_{part}_l{l}")
            gathers[l, part] = _gather_start(lands, token, f"gather_start_{part}_l{l}")
            token = gathers[l, part][-1]

    def pass_on(l, part, after):
        gathers[l, part] = _gather_forward(gathers[l, part], after, f"gather_forward_{part}_l{l}")
        return gathers[l, part][-1]

    pass_on(0, "f1", token)

    def weights_of(l, part, after):
        return _part_views(part, _gather_wait(gathers[l, part], after, f"gather_wait_{part}_l{l}"))

    small = [_layer_small(w, l) for l in range(DEPTH)]
    loss_part, dcur, dfinal, saved, views = _forward(xs, tgt, weights_of, pass_on, small, w["final_norm"].reshape(1, D))
    loss = lax.psum(loss_part[0, 0], ("x", "y", "c"))

    scatters = []
    small_gathers = [None] * DEPTH
    token = None
    for l in reversed(range(DEPTH)):
        def emit(part, grads, l=l):
            scatters.append((l, part, _exchange_start("scatter", grads, dfinal, f"scatter_start_{part}_l{l}")))
            return scatters[-1][2][-1]
        dcur, small_grads, token = _backward_layer(dcur, saved[l], small[l], views[l], l, emit, token)
        packed = _pack_small_grads(small_grads, dfinal if l == DEPTH - 1 else None, me_arr, f"pack_small_grads_l{l}")
        small_gathers[l] = _exchange_start("gather", [packed], token, f"gather_start_small_l{l}")
        token = small_gathers[l][-1]
    grad_x = dcur.reshape(x.shape)

    done = {}
    behind = token
    for l, part, started in scatters:
        arrs = _exchange_wait("scatter", started, behind, f"scatter_wait_{part}_l{l}")
        na = len(arrs) // 2
        for a, piece, k, transposed in _part_pieces(part):
            flip = transposed and w[k].shape[2] % 128 != 0
            view = (lambda t: t.transpose(0, 2, 1)) if flip else (lambda t: t)
            done[k] = _reduce_update(arrs[na + a], arrs[a], me_arr, piece, view(w[k]), view(mom[k]), view(var[k]), l,
                                     done.get(k), transposed and not flip, f"update_{k}_l{l}")
            behind = done[k][1]
    unflip = lambda k, t: t.transpose(0, 2, 1) if t.shape != w[k].shape else t
    grads, delta, new_m, new_v = ({k: unflip(k, done[k][t]) for k in BIG} for t in range(4))
    gathered = [None] * DEPTH
    for l in reversed(range(DEPTH)):
        gathered[l] = _exchange_wait("gather", small_gathers[l], behind, f"gather_wait_small_l{l}")[0]
        behind = gathered[l]
    as_row = lambda t: dict(t, final_norm=t["final_norm"].reshape(1, D))
    for dst, src in zip((grads, delta, new_m, new_v), _adamw_small(gathered, as_row(w), as_row(mom), as_row(var), "adamw_small")):
        dst.update(src)
        dst["final_norm"] = src["final_norm"].reshape(D)

    return (loss, grad_x, *[grads[k] for k in ALL_WEIGHTS], *[delta[k] for k in ALL_WEIGHTS],
            *[new_m[k] for k in ALL_WEIGHTS], *[new_v[k] for k in ALL_WEIGHTS])
```

```python
import math

import jax
import jax.numpy as jnp
from jax import lax
from jax.experimental import pallas as pl
from jax.experimental.pallas import tpu as pltpu

F32 = jnp.float32
BF16 = jnp.bfloat16
MESH = pl.DeviceIdType.MESH

NDEV = 8
DEPTH = 2
D = 1024
FF = 2816
FFS = FF // NDEV
INW = 6912
INS = INW // NDEV
DS = D // NDEV
BLK = 128
NH = 4
HD = 64
GW = NH * HD
DILS = (1, 4, 16)
QK_SCALE = 1.0 / math.sqrt(HD)
QKV_W = 3 * len(DILS) * GW
Q_B, K_B, V_B = 0, 3, 6
UV_B = 0
C_B128 = 8
G_B = 6
BW = 512
EPS = 1e-6
NEG = -1e30
VMEM_CAP = 60 * 1024 * 1024

ADAM_LR, ADAM_B1, ADAM_B2, ADAM_EPS, ADAM_WD, ADAM_STEP = 0.001, 0.9, 0.999, 1e-08, 0.01, 10

NT = (((1,), (1,)), ((), ()))
TN = (((0,), (0,)), ((), ()))


def _bs(shape, imap):
    return pl.BlockSpec(shape, imap)


def _pc(body, *, name, grid, in_specs, out_specs, out_shape, scratch=(), alias=None, vmem=None):
    params = {}
    if vmem is not None:
        params["vmem_limit_bytes"] = min(int(vmem), VMEM_CAP)
    big = lambda s: math.prod(s.shape) * jnp.dtype(s.dtype).itemsize >= (1 << 20)
    pin = lambda s: pltpu.HBM(s.shape, s.dtype) if big(s) else s
    out_shape = [pin(s) for s in out_shape] if isinstance(out_shape, (list, tuple)) else pin(out_shape)
    call = pl.pallas_call(
        body, name=name, grid=grid, in_specs=in_specs, out_specs=out_specs, out_shape=out_shape,
        scratch_shapes=list(scratch), input_output_aliases=alias or {},
        compiler_params=pltpu.CompilerParams(**params), interpret=False)
    return lambda *args: call(*[pltpu.with_memory_space_constraint(a, pltpu.HBM) if big(a) else a for a in args])


def _sds(shape, dtype):
    return jax.ShapeDtypeStruct(shape, dtype)


def _mm(a, b, *, mode, tm, tn, tk, n, out_dtype, name, b_row0=0, scale=None, res=None, after=None, norm_gain=None):
    m, k_dim = a.shape
    nk = k_dim // tk
    assert m % tm == 0 and n % tn == 0 and k_dim % tk == 0 and (norm_gain is None or tn == n)
    b_off, rem = divmod(b_row0, tk if mode == "nn" else tn)
    assert rem == 0
    grid = (m // tm, n // tn, nk)
    in_specs = [_bs((tm, tk), lambda i, j, k: (i, k))]
    if mode == "nn":
        in_specs.append(_bs((tk, tn), lambda i, j, k: (b_off + k, j)))
    else:
        in_specs.append(_bs((tn, tk), lambda i, j, k: (b_off + j, k)))
    args = [a, b]
    if res is not None:
        in_specs.append(_bs((tm, tn), lambda i, j, k: (i, j)))
        args.append(res)
    if norm_gain is not None:
        in_specs.append(_bs((1, tn), lambda i, j, k: (0, 0)))
        args.append(norm_gain)
    if after is not None:
        in_specs.append(pl.BlockSpec(memory_space=pl.ANY))
        args.append(after)
    n_in = len(args)

    def body(*refs):
        a_ref, b_ref = refs[0], refs[1]
        r_ref = refs[2] if res is not None else None
        g_ref = refs[2 + (res is not None)] if norm_gain is not None else None
        o_ref = refs[n_in]
        av = a_ref[...].astype(BF16)
        bv = b_ref[...].astype(BF16)
        if mode == "nn":
            p = jnp.dot(av, bv, preferred_element_type=F32)
        else:
            p = lax.dot_general(av, bv, NT, preferred_element_type=F32)

        def fin(v):
            if scale is not None:
                v = v * scale
            if r_ref is not None:
                v = r_ref[...] + v
            o_ref[...] = v.astype(out_dtype)
            if g_ref is not None:
                r = lax.rsqrt(jnp.mean(v * v, axis=-1, keepdims=True) + EPS)
                refs[n_in + 1][...] = (v * r * g_ref[...]).astype(BF16)

        if nk == 1:
            fin(p)
        else:
            acc_ref = refs[-1]
            kk = pl.program_id(2)

            @pl.when(kk == 0)
            def _():
                acc_ref[...] = p

            @pl.when(kk > 0)
            def _():
                acc_ref[...] += p

            @pl.when(kk == nk - 1)
            def _():
                fin(acc_ref[...])

    scratch = [pltpu.VMEM((tm, tn), F32)] if nk > 1 else []
    est = 2 * (tm * tk * a.dtype.itemsize + tk * tn * b.dtype.itemsize + tm * tn * jnp.dtype(out_dtype).itemsize)
    est += (2 + (nk > 1)) * tm * tn * 4 + (2 * tm * tn * 4 if res is not None else 0)
    est += (tm * tk * 2 if a.dtype != BF16 else 0) + (tk * tn * 2 if b.dtype != BF16 else 0)
    tile = _bs((tm, tn), lambda i, j, k: (i, j))
    if norm_gain is None:
        return _pc(body, name=name, grid=grid, in_specs=in_specs, out_specs=tile, out_shape=_sds((m, n), out_dtype),
                   scratch=scratch, vmem=est + (8 << 20))(*args)
    return _pc(body, name=name, grid=grid, in_specs=in_specs, out_specs=[tile, tile],
               out_shape=[_sds((m, n), out_dtype), _sds((m, n), BF16)], scratch=scratch,
               vmem=est + 3 * tm * tn * 4 + (8 << 20))(*args)


def _wgrad(a, b, buf, *, piece0, rows, tm, tn, tk, name, scale=None):
    t_dim, m = a.shape
    n = b.shape[1]
    db = tm // rows
    tpp = NDEV // db
    nk = t_dim // tk
    assert tm % rows == 0 and NDEV % db == 0 and m % tm == 0 and n % tn == 0 and t_dim % tk == 0
    grid = (m // tm, n // tn, nk)

    def body(a_ref, b_ref, buf_ref, o_ref, acc_ref):
        del buf_ref
        kk = pl.program_id(2)
        p = lax.dot_general(a_ref[...].astype(BF16), b_ref[...].astype(BF16), TN, preferred_element_type=F32)

        @pl.when(kk == 0)
        def _():
            acc_ref[...] = p

        @pl.when(kk > 0)
        def _():
            acc_ref[...] += p

        @pl.when(kk == nk - 1)
        def _():
            for d in range(db):
                v = acc_ref[d * rows:(d + 1) * rows, :]
                if scale is not None:
                    v = v * scale
                o_ref[0, d] = v.astype(BF16)

    est = 2 * (tk * tm * a.dtype.itemsize + tk * tn * b.dtype.itemsize + tm * tn * 2) + 3 * tm * tn * 4 + tk * tm * 4
    return _pc(
        body, name=name, grid=grid,
        in_specs=[_bs((tk, tm), lambda i, j, k: (k, i)), _bs((tk, tn), lambda i, j, k: (k, j)),
                  pl.BlockSpec(memory_space=pl.ANY)],
        out_specs=_bs((1, db, rows, tn), lambda i, j, k: (piece0 + i // tpp, i % tpp, 0, j)),
        out_shape=_sds(buf.shape, buf.dtype), scratch=[pltpu.VMEM((tm, tn), F32)], alias={2: 0},
        vmem=est + (8 << 20))(a, b, buf)


def _mm_drms(a, b, x, gain, dres, *, tm, b_row0, after, name):
    m, k_dim = a.shape
    assert m % tm == 0 and b_row0 % k_dim == 0 and b.shape[1] == D
    k_blk = b_row0 // k_dim
    after = jnp.zeros((8, 128), F32) if after is None else after

    def body(a_ref, b_ref, x_ref, g_ref, dr_ref, after_ref, dx_ref, dg_ref):
        del after_ref
        dh_v = jnp.dot(a_ref[...].astype(BF16), b_ref[...].astype(BF16), preferred_element_type=F32)
        xv = x_ref[...]
        r = lax.rsqrt(jnp.mean(xv * xv, axis=-1, keepdims=True) + EPS)
        dyg = dh_v * g_ref[...]
        mq = jnp.mean(dyg * xv, axis=-1, keepdims=True)
        dx_ref[...] = dr_ref[...] + r * (dyg - xv * (r * r * mq))

        @pl.when(pl.program_id(0) == 0)
        def _():
            dg_ref[...] = jnp.zeros_like(dg_ref)

        dg_ref[...] += jnp.sum(dh_v * (xv * r), axis=0, keepdims=True)

    row = _bs((tm, D), lambda i: (i, 0))
    est = 2 * (tm * k_dim * a.dtype.itemsize + k_dim * D * b.dtype.itemsize + 3 * tm * D * 4) + 4 * tm * D * 4
    return _pc(body, name=name, grid=(m // tm,),
               in_specs=[_bs((tm, k_dim), lambda i: (i, 0)), _bs((k_dim, D), lambda i: (k_blk, 0)), row,
                         _bs((1, D), lambda i: (0, 0)), row, pl.BlockSpec(memory_space=pl.ANY)],
               out_specs=[row, _bs((1, D), lambda i: (0, 0))], out_shape=[_sds((m, D), F32), _sds((1, D), F32)],
               vmem=est + (8 << 20))(a, b, x, gain, dres, after)


def _dact_dswiglu(dout, b, g_pre, u_pre, *, tm, b_row0, after, name):
    t_dim = dout.shape[0]
    assert t_dim % tm == 0 and b_row0 % FF == 0
    k_blk = b_row0 // FF
    after = jnp.zeros((8, 128), F32) if after is None else after

    def body(d_ref, b_ref, g_ref, u_ref, after_ref, o_ref):
        del after_ref
        dav = 0.5 * lax.dot_general(d_ref[...].astype(BF16), b_ref[...].astype(BF16), NT, preferred_element_type=F32)
        g = g_ref[...].astype(F32)
        u = u_ref[...].astype(F32)
        s = _sigmoid(g)
        o_ref[:, :FF] = (dav * u * (s * (1.0 + g * (1.0 - s)))).astype(BF16)
        o_ref[:, FF:] = (dav * (g * s)).astype(BF16)

    est = 2 * (tm * D * 4 + FF * D * 2 + 2 * tm * 2 * FF * 2) + 5 * tm * FF * 4
    return _pc(body, name=name, grid=(t_dim // tm,),
               in_specs=[_bs((tm, D), lambda i: (i, 0)), _bs((FF, D), lambda i: (k_blk, 0)),
                         _bs((tm, FF), lambda i: (i, 0)), _bs((tm, FF), lambda i: (i, 0)), pl.BlockSpec(memory_space=pl.ANY)],
               out_specs=_bs((tm, 2 * FF), lambda i: (i, 0)), out_shape=_sds((t_dim, 2 * FF), BF16),
               vmem=est + (8 << 20))(dout, b, g_pre, u_pre, after)


def _rms_fwd(x, gain, name, tm=512):
    t_dim = x.shape[0]

    def body(x_ref, g_ref, o_ref):
        xv = x_ref[...]
        r = lax.rsqrt(jnp.mean(xv * xv, axis=-1, keepdims=True) + EPS)
        o_ref[...] = (xv * r * g_ref[...]).astype(BF16)

    return _pc(body, name=name, grid=(t_dim // tm,),
               in_specs=[_bs((tm, D), lambda i: (i, 0)), _bs((1, D), lambda i: (0, 0))],
               out_specs=_bs((tm, D), lambda i: (i, 0)), out_shape=_sds((t_dim, D), BF16))(x, gain)


def _sigmoid(v):
    return 1.0 / (1.0 + jnp.exp(-v))


def _gu_swiglu(h, b, *, tm, tn, b_row0, name):
    t_dim = h.shape[0]
    assert t_dim % tm == 0 and FF % tn == 0 and b_row0 % tn == 0
    j0 = b_row0 // tn

    def body(h_ref, bg_ref, bu_ref, g_ref, u_ref, a_ref):
        hv = h_ref[...]
        g16 = lax.dot_general(hv, bg_ref[...], NT, preferred_element_type=F32).astype(BF16)
        u16 = lax.dot_general(hv, bu_ref[...], NT, preferred_element_type=F32).astype(BF16)
        g = g16.astype(F32)
        g_ref[...] = g16
        u_ref[...] = u16
        a_ref[...] = (g * _sigmoid(g) * u16.astype(F32)).astype(BF16)

    tile = _bs((tm, tn), lambda i, j: (i, j))
    est = 2 * (tm * D * 2 + 2 * tn * D * 2 + 3 * tm * tn * 2) + 5 * tm * tn * 4
    return _pc(body, name=name, grid=(t_dim // tm, FF // tn),
               in_specs=[_bs((tm, D), lambda i, j: (i, 0)), _bs((tn, D), lambda i, j: (j0 + j, 0)),
                         _bs((tn, D), lambda i, j: (j0 + FF // tn + j, 0))],
               out_specs=[tile] * 3, out_shape=[_sds((t_dim, FF), BF16)] * 3, vmem=est + (8 << 20))(h, b, b)


def _final_loss(x, gain, tgt, name, tm=512):
    t_dim = x.shape[0]

    def body(x_ref, g_ref, t_ref, loss_ref, dx_ref, dg_ref):
        xv = x_ref[...]
        gv = g_ref[...]
        r = lax.rsqrt(jnp.mean(xv * xv, axis=-1, keepdims=True) + EPS)
        xn = xv * r
        err = xn * gv - t_ref[...]
        dy = err * (1.0 / D)
        dyg = dy * gv
        mq = jnp.mean(dyg * xv, axis=-1, keepdims=True)
        dx_ref[...] = r * (dyg - xv * (r * r * mq))

        @pl.when(pl.program_id(0) == 0)
        def _():
            dg_ref[...] = jnp.zeros_like(dg_ref)
            loss_ref[...] = jnp.zeros_like(loss_ref)

        dg_ref[...] += jnp.sum(dy * xn, axis=0, keepdims=True)
        part = 0.5 * jnp.sum(jnp.mean(err * err, axis=-1, keepdims=True), axis=0, keepdims=True)
        loss_ref[...] += jnp.broadcast_to(part, loss_ref.shape)

    return _pc(body, name=name, grid=(t_dim // tm,),
               in_specs=[_bs((tm, D), lambda i: (i, 0)), _bs((1, D), lambda i: (0, 0)), _bs((tm, D), lambda i: (i, 0))],
               out_specs=[_bs((8, 128), lambda i: (0, 0)), _bs((tm, D), lambda i: (i, 0)), _bs((1, D), lambda i: (0, 0))],
               out_shape=[_sds((8, 128), F32), _sds((t_dim, D), F32), _sds((1, D), F32)])(x, gain, tgt)


def _band_mask(n):
    ri = lax.broadcasted_iota(jnp.int32, (BLK, 2 * BLK), 0)
    cj = lax.broadcasted_iota(jnp.int32, (BLK, 2 * BLK), 1)
    dist = ri + BLK - cj
    return (dist >= 0) & (dist <= BLK) & ((cj >= BLK) | (n > 0))


def _halves(unit, imap_rows, col):
    return [_bs((unit, 128), (lambda u, j=j: (imap_rows(u), 2 * col + j))) for j in range(2)]


def _get_rows(pair, r, dil):
    return jnp.concatenate([h[pl.ds(r, BLK, stride=dil), :] for h in pair], axis=1)


def _put_rows(stage, r, dil, val):
    for j in range(2):
        stage[j, pl.ds(r, BLK, stride=dil), :] = val[:, 128 * j:128 * (j + 1)]


def _unstage(stage, out_ref):
    out_ref[:, :128] = stage[0]
    out_ref[:, 128:] = stage[1]


def _attn_fwd(z, acc, ml, *, gi, first, final, name):
    dil = DILS[gi]
    t_dim = z.shape[0]
    unit = BLK * dil
    nu = t_dim // unit
    own = lambda u: u
    prev = lambda u: jnp.maximum(u - 1, 0)
    in_specs = (_halves(unit, own, Q_B + gi) + _halves(unit, own, K_B + gi) + _halves(unit, prev, K_B + gi)
                + _halves(unit, own, V_B + gi) + _halves(unit, prev, V_B + gi))
    args = [z] * 10
    st_ml = _bs((unit, 128), lambda u: (u, 0))
    if not first:
        in_specs += _halves(unit, own, 0) + [st_ml]
        args += [acc, acc, ml]

    def body(*refs):
        q2, ko2, kp2, vo2, vp2 = (refs[2 * i:2 * i + 2] for i in range(5))
        o_acc_ref, o_ml_ref, stage = refs[-3], refs[-2], refs[-1]
        valid = _band_mask(pl.program_id(0))
        lane = lax.broadcasted_iota(jnp.int32, (BLK, GW), 1)
        lane_s = lax.broadcasted_iota(jnp.int32, (BLK, 128), 1)

        def band(r, carry):
            q = _get_rows(q2, r, dil) * QK_SCALE
            kb = jnp.concatenate([_get_rows(kp2, r, dil), _get_rows(ko2, r, dil)], axis=0).astype(BF16)
            vb = jnp.concatenate([_get_rows(vp2, r, dil), _get_rows(vo2, r, dil)], axis=0).astype(BF16)
            if first:
                acc_v = jnp.zeros((BLK, GW), F32)
                ml_v = jnp.where(lane_s < 64, NEG, 0.0).astype(F32)
            else:
                acc_v = _get_rows(refs[10:12], r, dil)
                ml_v = refs[12][pl.ds(r, BLK, stride=dil), :]
            heads = [(lane >= h * HD) & (lane < (h + 1) * HD) for h in range(NH)]
            rows = lambda v, h: v[h * BLK:(h + 1) * BLK]
            qs = jnp.concatenate([jnp.where(hm, q, 0.0) for hm in heads], axis=0).astype(BF16)
            s = lax.dot_general(qs, kb, NT, preferred_element_type=F32)
            s = jnp.where(jnp.concatenate([valid] * NH, axis=0), s, NEG)
            m_old = jnp.concatenate([ml_v[:, 16 * h:16 * h + 1] for h in range(NH)], axis=0)
            l_old = jnp.concatenate([ml_v[:, 64 + 16 * h:64 + 16 * h + 1] for h in range(NH)], axis=0)
            m_new = jnp.maximum(m_old, jnp.max(s, axis=-1, keepdims=True))
            alpha = jnp.exp(m_old - m_new)
            p = jnp.exp(s - m_new)
            l_new = alpha * l_old + jnp.sum(p, axis=-1, keepdims=True)
            pv = jnp.dot(p.astype(BF16), vb, preferred_element_type=F32)
            stat = m_new + jnp.log(l_new) if final else m_new
            acc_o, ml_o = acc_v, ml_v
            l_b = jnp.ones((BLK, GW), F32)
            for h, hm in enumerate(heads):
                acc_o = jnp.where(hm, acc_v * rows(alpha, h) + rows(pv, h), acc_o)
                l_b = jnp.where(hm, rows(l_new, h), l_b)
                ml_o = jnp.where((lane_s >= 16 * h) & (lane_s < 16 * h + 16), rows(stat, h), ml_o)
                if not final:
                    ml_o = jnp.where((lane_s >= 64 + 16 * h) & (lane_s < 64 + 16 * h + 16), rows(l_new, h), ml_o)
            _put_rows(stage, r, dil, acc_o / l_b if final else acc_o)
            o_ml_ref[pl.ds(r, BLK, stride=dil), :] = ml_o
            return carry

        lax.fori_loop(0, dil, band, 0, unroll=min(dil, 8))
        _unstage(stage, o_acc_ref)

    blocks = (len(in_specs) + 6) * unit * 128 * 4
    return _pc(body, name=name, grid=(nu,), in_specs=in_specs, out_specs=[_bs((unit, GW), lambda u: (u, 0)), st_ml],
               out_shape=[_sds((t_dim, GW), F32), _sds((t_dim, 128), F32)],
               scratch=[pltpu.VMEM((2, unit, 128), F32)], vmem=2 * blocks + (12 << 20))(*args)


def _attn_bwd(z, dy, y, lse, *, gi, name):
    dil = DILS[gi]
    t_dim = z.shape[0]
    unit = BLK * dil
    nu = t_dim // unit

    own = lambda u: jnp.minimum(u, nu - 1)
    prev = lambda u: jnp.maximum(jnp.minimum(u, nu - 1) - 1, 0)
    in_specs = (_halves(unit, own, Q_B + gi) + _halves(unit, own, K_B + gi) + _halves(unit, prev, K_B + gi)
                + _halves(unit, own, V_B + gi) + _halves(unit, prev, V_B + gi)
                + _halves(unit, own, 0) + _halves(unit, own, 0) + [_bs((unit, 128), lambda u: (own(u), 0))])
    tok = _bs((unit, GW), lambda u: (own(u), 0))
    kv_out = _bs((unit, GW), lambda u: (jnp.maximum(u - 1, 0), 0))

    def body(*refs):
        q2, ko2, kp2, vo2, vp2, dy2, y2 = (refs[2 * i:2 * i + 2] for i in range(7))
        lse_ref = refs[14]
        dq_ref, dk_ref, dv_ref, dq_st, dk_st, dv_st, dkc, dvc = refs[15:]
        u = pl.program_id(0)

        @pl.when(u == 0)
        def _():
            dkc[...] = jnp.zeros_like(dkc)
            dvc[...] = jnp.zeros_like(dvc)

        @pl.when(u < nu)
        def _():
            valid = _band_mask(u)
            lane = lax.broadcasted_iota(jnp.int32, (BLK, GW), 1)

            def band(r, carry):
                q = _get_rows(q2, r, dil) * QK_SCALE
                kb = jnp.concatenate([_get_rows(kp2, r, dil), _get_rows(ko2, r, dil)], axis=0).astype(BF16)
                vb = jnp.concatenate([_get_rows(vp2, r, dil), _get_rows(vo2, r, dil)], axis=0).astype(BF16)
                dyv = _get_rows(dy2, r, dil)
                yv = _get_rows(y2, r, dil)
                lse_v = lse_ref[pl.ds(r, BLK, stride=dil), :]
                heads = [(lane >= h * HD) & (lane < (h + 1) * HD) for h in range(NH)]
                qs = jnp.concatenate([jnp.where(hm, q, 0.0) for hm in heads], axis=0).astype(BF16)
                dys = jnp.concatenate([jnp.where(hm, dyv, 0.0) for hm in heads], axis=0)
                lse_s = jnp.concatenate([lse_v[:, 16 * h:16 * h + 1] for h in range(NH)], axis=0)
                s = lax.dot_general(qs, kb, NT, preferred_element_type=F32)
                p = jnp.where(jnp.concatenate([valid] * NH, axis=0), jnp.exp(s - lse_s), 0.0)
                delta = jnp.sum(dys * jnp.concatenate([yv] * NH, axis=0), axis=-1, keepdims=True)
                dys16 = dys.astype(BF16)
                dp = lax.dot_general(dys16, vb, NT, preferred_element_type=F32)
                ds = (p * (dp - delta)).astype(BF16)
                dq_all = jnp.dot(ds, kb, preferred_element_type=F32)
                dq = jnp.zeros((BLK, GW), F32)
                for h, hm in enumerate(heads):
                    dq = jnp.where(hm, dq_all[h * BLK:(h + 1) * BLK], dq)
                dkb = lax.dot_general(ds, qs, TN, preferred_element_type=F32)
                dvb = lax.dot_general(p.astype(BF16), dys16, TN, preferred_element_type=F32)
                _put_rows(dq_st, r, dil, dq * QK_SCALE)
                _put_rows(dk_st, r, dil, _get_rows((dkc.at[0], dkc.at[1]), r, dil) + dkb[:BLK])
                _put_rows(dv_st, r, dil, _get_rows((dvc.at[0], dvc.at[1]), r, dil) + dvb[:BLK])
                _put_rows(dkc, r, dil, dkb[BLK:])
                _put_rows(dvc, r, dil, dvb[BLK:])
                return carry

            lax.fori_loop(0, dil, band, 0, unroll=min(dil, 8))
            _unstage(dq_st, dq_ref)

            @pl.when(u > 0)
            def _():
                _unstage(dk_st, dk_ref)
                _unstage(dv_st, dv_ref)

        @pl.when(u == nu)
        def _():
            _unstage(dkc, dk_ref)
            _unstage(dvc, dv_ref)

    blocks = (15 + 6 + 5) * unit * 128 * 4 * 2
    return _pc(body, name=name, grid=(nu + 1,), in_specs=in_specs, out_specs=[tok, kv_out, kv_out],
               out_shape=[_sds((t_dim, GW), F32)] * 3, scratch=[pltpu.VMEM((2, unit, 128), F32)] * 5,
               vmem=blocks + (12 << 20))(*([z] * 10), dy, dy, y, y, lse)


_INV_SQRT2 = 1.0 / math.sqrt(2.0)
_INV_SQRT2PI = 1.0 / math.sqrt(2.0 * math.pi)


def _gelu(v):
    return 0.5 * v * (1.0 + lax.erf(v * _INV_SQRT2))


def _gelu_grad(v):
    return 0.5 * (1.0 + lax.erf(v * _INV_SQRT2)) + v * (_INV_SQRT2PI * jnp.exp(-0.5 * v * v))


def _gmlp_core(z_refs, lng, lnb, ws_ref, bs_ref):
    u_pre = jnp.concatenate([z_refs[0][...], z_refs[1][...]], axis=1).astype(F32)
    v_pre = jnp.concatenate([z_refs[2][...], z_refs[3][...]], axis=1).astype(F32)
    u = _gelu(u_pre)
    v = _gelu(v_pre)
    xc = v - jnp.mean(v, axis=-1, keepdims=True)
    rstd = lax.rsqrt(jnp.mean(xc * xc, axis=-1, keepdims=True) + EPS)
    xhat = xc * rstd
    vn = xhat * lng + lnb
    ti = lax.broadcasted_iota(jnp.int32, (BLK, BLK), 0)
    si = lax.broadcasted_iota(jnp.int32, (BLK, BLK), 1)
    causal = ti >= si
    vn16 = vn.astype(BF16)
    ws = [jnp.where(causal, ws_ref[g], 0.0).astype(BF16) for g in range(4)]
    mixed = jnp.concatenate(
        [jnp.dot(ws[g], vn16[:, g * BLK:(g + 1) * BLK], preferred_element_type=F32) + bs_ref[g] for g in range(4)], axis=1)
    return u_pre, v_pre, u, xhat, rstd, vn16, ws, causal, mixed


def _gmlp_specs():
    z_specs = [_bs((BLK, GW), (lambda i, c=c: (i, UV_B + c))) for c in range(4)]
    par_specs = [_bs((1, BW), lambda i: (0, 0)), _bs((1, BW), lambda i: (0, 0)),
                 _bs((4, BLK, BLK), lambda i: (0, 0, 0)), _bs((4, BLK, 1), lambda i: (0, 0, 0))]
    return z_specs, par_specs


def _gmlp_fwd(z, lng, lnb, ws, bs, name):
    t_dim = z.shape[0]
    z_specs, par_specs = _gmlp_specs()

    def body(z0, z1, z2, z3, lng_ref, lnb_ref, ws_ref, bs_ref, o_ref):
        core = _gmlp_core((z0, z1, z2, z3), lng_ref[...], lnb_ref[...], ws_ref, bs_ref)
        o_ref[...] = (core[2] * core[8]).astype(BF16)

    return _pc(body, name=name, grid=(t_dim // BLK,), in_specs=z_specs + par_specs,
               out_specs=_bs((BLK, BW), lambda i: (i, 0)), out_shape=_sds((t_dim, BW), BF16))(z, z, z, z, lng, lnb, ws, bs)


def _gmlp_bwd(z, dy, lng, lnb, ws, bs, name):
    t_dim = z.shape[0]
    z_specs, par_specs = _gmlp_specs()

    def body(z0, z1, z2, z3, lng_ref, lnb_ref, ws_ref, bs_ref, dy_ref, duv_ref, dws_ref, dbs_ref, dlng_ref, dlnb_ref):
        lng_v = lng_ref[...]
        u_pre, v_pre, u, xhat, rstd, vn16, wsm, causal, mixed = _gmlp_core((z0, z1, z2, z3), lng_v, lnb_ref[...], ws_ref, bs_ref)
        dyv = dy_ref[...].astype(F32)
        du = dyv * mixed
        dmixed = dyv * u

        @pl.when(pl.program_id(0) == 0)
        def _():
            dws_ref[...] = jnp.zeros_like(dws_ref)
            dbs_ref[...] = jnp.zeros_like(dbs_ref)
            dlng_ref[...] = jnp.zeros_like(dlng_ref)
            dlnb_ref[...] = jnp.zeros_like(dlnb_ref)

        dvn_parts = []
        for g in range(4):
            dm = dmixed[:, g * BLK:(g + 1) * BLK]
            dm16 = dm.astype(BF16)
            dw = lax.dot_general(dm16, vn16[:, g * BLK:(g + 1) * BLK], NT, preferred_element_type=F32)
            dws_ref[g] += jnp.where(causal, dw, 0.0)
            dbs_ref[g] += jnp.sum(dm.T, axis=0, keepdims=True)
            dvn_parts.append(lax.dot_general(wsm[g], dm16, TN, preferred_element_type=F32))
        dvn = jnp.concatenate(dvn_parts, axis=1)
        dlng_ref[...] += jnp.sum(dvn * xhat, axis=0, keepdims=True)
        dlnb_ref[...] += jnp.sum(dvn, axis=0, keepdims=True)
        dxh = dvn * lng_v
        dv = rstd * (dxh - jnp.mean(dxh, axis=-1, keepdims=True) - xhat * jnp.mean(dxh * xhat, axis=-1, keepdims=True))
        duv_ref[:, :BW] = (du * _gelu_grad(u_pre)).astype(BF16)
        duv_ref[:, BW:] = (dv * _gelu_grad(v_pre)).astype(BF16)

    return _pc(body, name=name, grid=(t_dim // BLK,),
               in_specs=z_specs + par_specs + [_bs((BLK, BW), lambda i: (i, 0))],
               out_specs=[_bs((BLK, 2 * BW), lambda i: (i, 0)), _bs((4, BLK, BLK), lambda i: (0, 0, 0)),
                          _bs((4, 1, BLK), lambda i: (0, 0, 0)), _bs((1, BW), lambda i: (0, 0)), _bs((1, BW), lambda i: (0, 0))],
               out_shape=[_sds((t_dim, 2 * BW), BF16), _sds((4, BLK, BLK), F32), _sds((4, 1, BLK), F32),
                          _sds((1, BW), F32), _sds((1, BW), F32)])(z, z, z, z, lng, lnb, ws, bs, dy)


def _pool_core(x, gi):
    t_dim = x.shape[0]
    w = jnp.left_shift(2, gi)
    row = lax.broadcasted_iota(jnp.int32, x.shape, 0)
    s = x
    for k in (1, 2, 4, 8):
        sh = jnp.where(row >= k, pltpu.roll(s, k, 0), 0.0)
        s = jnp.where(k < w, s + sh, s)
    cnt = jnp.minimum(row + 1, w).astype(F32)
    return s / cnt - x, cnt, row, w, t_dim


def _pool_fwd(z, cw, cs, name):
    t_dim = z.shape[0]

    def body(z_ref, cw_ref, cs_ref, o_ref):
        pooled = _pool_core(z_ref[...].astype(F32), pl.program_id(0))[0]
        y = jnp.dot(pooled.astype(BF16), cw_ref[0].astype(BF16), preferred_element_type=F32) * cs_ref[0]
        o_ref[...] = y.astype(BF16)

    return _pc(body, name=name, grid=(4,),
               in_specs=[_bs((t_dim, BLK), lambda g: (0, C_B128 + g)), _bs((1, BLK, BLK), lambda g: (g, 0, 0)),
                         _bs((1, 1, BLK), lambda g: (g, 0, 0))],
               out_specs=_bs((t_dim, BLK), lambda g: (0, g)), out_shape=_sds((t_dim, BW), BF16),
               vmem=40 << 20)(z, cw, cs)


def _pool_bwd(z, dy, cw, cs, name):
    t_dim = z.shape[0]

    def body(z_ref, dy_ref, cw_ref, cs_ref, dx_ref, dcw_ref, dcs_ref):
        pooled, cnt, row, w, _ = _pool_core(z_ref[...].astype(F32), pl.program_id(0))
        p16 = pooled.astype(BF16)
        cw16 = cw_ref[0].astype(BF16)
        dyv = dy_ref[...].astype(F32)
        lin = jnp.dot(p16, cw16, preferred_element_type=F32)
        dcs_ref[0] = jnp.sum(dyv * lin, axis=0, keepdims=True)
        dys = (dyv * cs_ref[0]).astype(BF16)
        dcw_ref[0] = lax.dot_general(p16, dys, TN, preferred_element_type=F32)
        dpool = lax.dot_general(dys, cw16, NT, preferred_element_type=F32)
        sb = dpool / cnt
        for k in (1, 2, 4, 8):
            sh = jnp.where(row < t_dim - k, pltpu.roll(sb, t_dim - k, 0), 0.0)
            sb = jnp.where(k < w, sb + sh, sb)
        dx_ref[...] = (sb - dpool).astype(BF16)

    return _pc(body, name=name, grid=(4,),
               in_specs=[_bs((t_dim, BLK), lambda g: (0, C_B128 + g)), _bs((t_dim, BLK), lambda g: (0, g)),
                         _bs((1, BLK, BLK), lambda g: (g, 0, 0)), _bs((1, 1, BLK), lambda g: (g, 0, 0))],
               out_specs=[_bs((t_dim, BLK), lambda g: (0, g)), _bs((1, BLK, BLK), lambda g: (g, 0, 0)),
                          _bs((1, 1, BLK), lambda g: (g, 0, 0))],
               out_shape=[_sds((t_dim, BW), BF16), _sds((4, BLK, BLK), F32), _sds((4, 1, BLK), F32)],
               vmem=48 << 20)(z, dy, cw, cs)


GATE_W = 2 * GW


def _gates_specs(tm):
    per_branch = D // GATE_W
    z_specs = [_bs((tm, GATE_W), (lambda j, i, b=b: (i, G_B * GW // GATE_W + per_branch * b + j))) for b in range(3)]
    bg_specs = [_bs((1, GATE_W), (lambda j, i, b=b: (0, per_branch * b + j))) for b in range(3)]
    tile = _bs((tm, GATE_W), lambda j, i: (i, j))
    return z_specs, bg_specs, tile


def _gates_fwd(z, bg, pa, pb, pc, name, tm=512):
    t_dim = z.shape[0]
    z_specs, bg_specs, tile = _gates_specs(tm)

    def body(za, zb, zc, ba, bb, bc, pa_ref, pb_ref, pc_ref, o_ref):
        o_ref[...] = (_sigmoid(za[...] + ba[...]) * pa_ref[...] + _sigmoid(zb[...] + bb[...]) * pb_ref[...]
                      + _sigmoid(zc[...] + bc[...]) * pc_ref[...]).astype(BF16)

    return _pc(body, name=name, grid=(D // GATE_W, t_dim // tm), in_specs=z_specs + bg_specs + [tile] * 3,
               out_specs=tile, out_shape=_sds((t_dim, D), BF16), vmem=40 << 20)(z, z, z, bg, bg, bg, pa, pb, pc)


def _gates_bwd(z, bg, pa, pb, pc, dm, name, tm=512):
    t_dim = z.shape[0]
    z_specs, bg_specs, tile = _gates_specs(tm)
    bias_out = _bs((1, GATE_W), lambda j, i: (0, j))

    def body(za, zb, zc, ba, bb, bc, pa_ref, pb_ref, pc_ref, dm_ref, dpa, dpb, dpc, dza, dzb, dzc, dba, dbb, dbc):
        dmv = dm_ref[...]
        first = pl.program_id(1) == 0
        for z_ref, b_ref, p_ref, dp_ref, dz_ref, db_ref in ((za, ba, pa_ref, dpa, dza, dba), (zb, bb, pb_ref, dpb, dzb, dbb),
                                                            (zc, bc, pc_ref, dpc, dzc, dbc)):
            gate = _sigmoid(z_ref[...] + b_ref[...])
            dp_ref[...] = (gate * dmv).astype(BF16)
            dz = dmv * p_ref[...] * (gate * (1.0 - gate))
            dz_ref[...] = dz.astype(BF16)

            @pl.when(first)
            def _():
                db_ref[...] = jnp.zeros_like(db_ref)

            db_ref[...] += jnp.sum(dz, axis=0, keepdims=True)

    return _pc(body, name=name, grid=(D // GATE_W, t_dim // tm), in_specs=z_specs + bg_specs + [tile] * 4,
               out_specs=[tile] * 6 + [bias_out] * 3,
               out_shape=[_sds((t_dim, D), BF16)] * 6 + [_sds((1, D), F32)] * 3, vmem=48 << 20)(
                   z, z, z, bg, bg, bg, pa, pb, pc, dm)


def _concat_bf16(parts, name, tm=512):
    t_dim = parts[0].shape[0]
    offs = [sum(p.shape[1] for p in parts[:i]) for i in range(len(parts) + 1)]

    def body(*refs):
        o_ref = refs[-1]
        for ref, off in zip(refs[:-1], offs):
            o_ref[:, off:off + ref.shape[1]] = ref[...].astype(BF16)

    return _pc(body, name=name, grid=(t_dim // tm,), in_specs=[_bs((tm, p.shape[1]), lambda i: (i, 0)) for p in parts],
               out_specs=_bs((tm, offs[-1]), lambda i: (i, 0)), out_shape=_sds((t_dim, offs[-1]), BF16),
               vmem=48 << 20)(*parts)


def _coords():
    return lax.axis_index("x"), lax.axis_index("y"), lax.axis_index("c")


def _peer(k, x, y, c):
    px = 1 - x if k & 4 else x
    py = 1 - y if k & 2 else y
    pc = 1 - c if k & 1 else c
    return (px, py, pc), 4 * px + 2 * py + pc


def _exchange_copies(kind, bufs, send, recv):
    x, y, c = _coords()
    me = 4 * x + 2 * y + c
    na = len(bufs) if kind == "gather" else len(bufs) // 2
    out = []
    for k in range(1, NDEV):
        to, peer = _peer(k, x, y, c)
        for a in range(na):
            sems = dict(send_sem=send.at[a * 7 + k - 1], recv_sem=recv.at[a * 7 + k - 1], device_id=to, device_id_type=MESH)
            if kind == "gather":
                mine = pltpu.make_async_remote_copy(src_ref=bufs[a].at[:, me], dst_ref=bufs[a].at[:, me], **sems)
                theirs = pltpu.make_async_remote_copy(src_ref=bufs[a].at[:, me], dst_ref=bufs[a].at[:, peer], **sems)
            else:
                src, land = bufs[a], bufs[na + a]
                mine = pltpu.make_async_remote_copy(src_ref=src.at[:, peer], dst_ref=land.at[me], **sems)
                theirs = pltpu.make_async_remote_copy(src_ref=src.at[:, peer], dst_ref=land.at[peer], **sems)
            out.append((mine, theirs))
    return out


_HBM = pl.BlockSpec(memory_space=pltpu.HBM)
_SEM = pl.BlockSpec(memory_space=pltpu.SEMAPHORE)
_EFFECT = pltpu.SideEffectType.DATAFLOW_SIDE_EFFECTING


def _exchange_start(kind, arrays, after, name):
    na = len(arrays)
    if kind == "scatter":
        arrays = list(arrays) + [lax.empty((NDEV, s.shape[0]) + s.shape[2:], s.dtype) for s in arrays]
    nb = len(arrays)

    def body(*refs):
        bufs = refs[:nb]
        send, recv = refs[nb + 1], refs[nb + 2]
        token = refs[-1]
        for mine, _ in _exchange_copies(kind, bufs, send, recv):
            mine.start()
        token[...] = jnp.zeros_like(token)

    arrays = [pltpu.with_memory_space_constraint(b, pltpu.HBM) for b in arrays]
    outs = pl.pallas_call(
        body, name=name,
        out_shape=(pltpu.SemaphoreType.DMA((7 * na,)), pltpu.SemaphoreType.DMA((7 * na,)),
                   *[pltpu.HBM(b.shape, b.dtype) for b in arrays], _sds((8, 128), F32)),
        in_specs=[_HBM] * nb + [pl.BlockSpec(memory_space=pl.ANY)],
        out_specs=(_SEM, _SEM, *([_HBM] * nb), pl.BlockSpec(memory_space=pltpu.VMEM)),
        input_output_aliases={i: 2 + i for i in range(nb)},
        compiler_params=pltpu.CompilerParams(has_side_effects=_EFFECT), interpret=False)(*arrays, after)
    return outs[0], outs[1], list(outs[2:2 + nb]), outs[-1]


def _exchange_wait(kind, started, after, name):
    send, recv, arrays, _ = started
    nb = len(arrays)

    def body(*refs):
        for mine, theirs in _exchange_copies(kind, refs[:nb], refs[nb], refs[nb + 1]):
            mine.wait_send()
            theirs.wait_recv()

    outs = pl.pallas_call(
        body, name=name,
        out_shape=tuple(pltpu.HBM(b.shape, b.dtype) for b in arrays),
        in_specs=[_HBM] * nb + [_SEM, _SEM, pl.BlockSpec(memory_space=pl.ANY)],
        out_specs=tuple([_HBM] * nb),
        input_output_aliases={i: i for i in range(nb)},
        compiler_params=pltpu.CompilerParams(has_side_effects=_EFFECT), interpret=False)(*arrays, send, recv, after)
    return list(outs)


def _adam_math(w, g, m, v):
    m2 = ADAM_B1 * m + (1.0 - ADAM_B1) * g
    v2 = ADAM_B2 * v + (1.0 - ADAM_B2) * jnp.square(g)
    m_hat = m2 / (1.0 - ADAM_B1 ** ADAM_STEP)
    v_hat = v2 / (1.0 - ADAM_B2 ** ADAM_STEP)
    return -ADAM_LR * (m_hat / (jnp.sqrt(v_hat) + ADAM_EPS) + ADAM_WD * w), m2, v2


_DIRECT = (1, 4, 2, 6)


def _gather_copies(bufs, send_d, recv_d, send_f=None, recv_f=None):
    x, y, c = _coords()
    me = 4 * x + 2 * y + c
    sibling, _ = _peer(1, x, y, c)
    direct, forward = [], []
    for a, buf in enumerate(bufs):
        for j, k in enumerate(_DIRECT):
            to, peer = _peer(k, x, y, c)
            sems = dict(send_sem=send_d.at[4 * a + j], recv_sem=recv_d.at[4 * a + j], device_id=to, device_id_type=MESH)
            direct.append((pltpu.make_async_remote_copy(src_ref=buf.at[:, me], dst_ref=buf.at[:, me], **sems),
                           pltpu.make_async_remote_copy(src_ref=buf.at[:, me], dst_ref=buf.at[:, peer], **sems)))
        if send_f is None:
            continue
        for j, k in enumerate(_DIRECT[1:]):
            _, near = _peer(k, x, y, c)
            _, far = _peer(k | 1, x, y, c)
            sems = dict(send_sem=send_f.at[3 * a + j], recv_sem=recv_f.at[3 * a + j], device_id=sibling, device_id_type=MESH)
            forward.append((pltpu.make_async_remote_copy(src_ref=buf.at[:, near], dst_ref=buf.at[:, near], **sems),
                            pltpu.make_async_remote_copy(src_ref=buf.at[:, near], dst_ref=buf.at[:, far], **sems)))
    return direct, forward


def _gather_start(lands, after, name):
    na = len(lands)

    def body(*refs):
        direct, _ = _gather_copies(refs[:na], refs[na + 1], refs[na + 2])
        for mine, _ in direct:
            mine.start()
        refs[-1][...] = jnp.zeros_like(refs[-1])

    lands = [pltpu.with_memory_space_constraint(b, pltpu.HBM) for b in lands]
    outs = pl.pallas_call(
        body, name=name,
        out_shape=(pltpu.SemaphoreType.DMA((4 * na,)), pltpu.SemaphoreType.DMA((4 * na,)),
                   *[pltpu.HBM(b.shape, b.dtype) for b in lands], _sds((8, 128), F32)),
        in_specs=[_HBM] * na + [pl.BlockSpec(memory_space=pl.ANY)],
        out_specs=(_SEM, _SEM, *([_HBM] * na), pl.BlockSpec(memory_space=pltpu.VMEM)),
        input_output_aliases={i: 2 + i for i in range(na)},
        compiler_params=pltpu.CompilerParams(has_side_effects=_EFFECT), interpret=False)(*lands, after)
    return outs[0], outs[1], list(outs[2:2 + na]), outs[-1]


def _gather_forward(started, after, name):
    send_d, recv_d, lands, _ = started
    na = len(lands)

    def body(*refs):
        direct, forward = _gather_copies(refs[:na], refs[na], refs[na + 1], refs[2 * na + 3], refs[2 * na + 4])
        for a in range(na):
            for j in range(3):
                direct[4 * a + 1 + j][1].wait_recv()
                forward[3 * a + j][0].start()
        refs[-1][...] = jnp.zeros_like(refs[-1])

    outs = pl.pallas_call(
        body, name=name,
        out_shape=(*[pltpu.HBM(b.shape, b.dtype) for b in lands], pltpu.SemaphoreType.DMA((3 * na,)),
                   pltpu.SemaphoreType.DMA((3 * na,)), _sds((8, 128), F32)),
        in_specs=[_HBM] * na + [_SEM, _SEM, pl.BlockSpec(memory_space=pl.ANY)],
        out_specs=(*([_HBM] * na), _SEM, _SEM, pl.BlockSpec(memory_space=pltpu.VMEM)),
        input_output_aliases={i: i for i in range(na)},
        compiler_params=pltpu.CompilerParams(has_side_effects=_EFFECT), interpret=False)(*lands, send_d, recv_d, after)
    return send_d, recv_d, outs[na], outs[na + 1], list(outs[:na]), outs[-1]


def _gather_wait(forwarded, after, name):
    send_d, recv_d, send_f, recv_f, lands, _ = forwarded
    na = len(lands)

    def body(*refs):
        direct, forward = _gather_copies(refs[:na], refs[na], refs[na + 1], refs[na + 2], refs[na + 3])
        for a in range(na):
            for j in range(4):
                direct[4 * a + j][0].wait_send()
            direct[4 * a][1].wait_recv()
            for j in range(3):
                forward[3 * a + j][0].wait_send()
                forward[3 * a + j][1].wait_recv()

    outs = pl.pallas_call(
        body, name=name, out_shape=tuple(pltpu.HBM(b.shape, b.dtype) for b in lands),
        in_specs=[_HBM] * na + [_SEM] * 4 + [pl.BlockSpec(memory_space=pl.ANY)], out_specs=tuple([_HBM] * na),
        input_output_aliases={i: i for i in range(na)},
        compiler_params=pltpu.CompilerParams(has_side_effects=_EFFECT), interpret=False)(
            *lands, send_d, recv_d, send_f, recv_f, after)
    return list(outs)


def _reduce_update(landed, own, me, piece, w, m, v, layer, prev, transposed, name):
    rows, cols = landed.shape[2], landed.shape[3]
    tc = min(cols, 512 if rows <= FFS else 256)
    assert cols % tc == 0 and w.shape[1:] == ((cols, rows) if transposed else (rows, cols))
    lane_chunks = [(c0, min(128, rows - c0)) for c0 in range(0, rows, 128)]

    def body(me_ref, b_ref, own_ref, w_ref, m_ref, v_ref, *rest):
        outs = rest[-4:]
        mine = me_ref[0]
        s = None
        for d in range(NDEV):
            other = jnp.where(mine == d, (d + 1) % NDEV, d)
            part = jnp.where(mine == d, own_ref[0, 0], b_ref[other, 0]).astype(F32)
            s = part if s is None else s + part
        if not transposed:
            for o, val in zip(outs, (s,) + _adam_math(w_ref[0], s, m_ref[0], v_ref[0])):
                o[0] = val
            return
        for c0, wd in lane_chunks:
            chunk = s[c0:c0 + wd]
            if wd < 128:
                chunk = jnp.concatenate([chunk, jnp.zeros((128 - wd, tc), F32)], axis=0)
            g = chunk.T[:, :wd]
            sl = (0, slice(None), slice(c0, c0 + wd))
            for o, val in zip(outs, (g,) + _adam_math(w_ref[sl], g, m_ref[sl], v_ref[sl])):
                o[sl] = val

    if transposed:
        nat = _bs((1, tc, rows), lambda j, me_ref: (layer, j, 0))
    else:
        nat = _bs((1, rows, tc), lambda j, me_ref: (layer, 0, j))
    in_specs = [_bs((NDEV, 1, rows, tc), lambda j, me_ref: (0, piece, 0, j)),
                _bs((1, 1, rows, tc), lambda j, me_ref: (piece, me_ref[0], 0, j)), nat, nat, nat]
    args = [me, landed, own, w, m, v]
    alias = {}
    if prev is not None:
        in_specs += [pl.BlockSpec(memory_space=pl.ANY)] * 4
        args += list(prev)
        alias = {6 + i: i for i in range(4)}
    grid_spec = pltpu.PrefetchScalarGridSpec(num_scalar_prefetch=1, grid=(cols // tc,), in_specs=in_specs, out_specs=[nat] * 4)
    return pl.pallas_call(body, name=name, grid_spec=grid_spec, out_shape=[_sds(w.shape, F32)] * 4,
                          input_output_aliases=alias, interpret=False)(*args)


SMALL_LAYER = ("ffn1_norm", "mix_norm", "b_gate", "b_ln_g", "b_ln_b", "b_w_s", "b_b_s", "c_w", "c_scale", "ffn2_norm")
SMALL_ROWS = {"ffn1_norm": 8, "mix_norm": 8, "b_gate": 24, "b_ln_g": 4, "b_ln_b": 4, "b_w_s": 512, "b_b_s": 4, "c_w": 512,
              "c_scale": 4, "ffn2_norm": 8, "final_norm": 8}
LAYER_ROWS = sum(SMALL_ROWS[k] for k in SMALL_LAYER)


def _small_offset(name):
    if name == "final_norm":
        return LAYER_ROWS
    return sum(SMALL_ROWS[k] for k in SMALL_LAYER[:SMALL_LAYER.index(name)])


def _pack_small_grads(layer_grads, final, me, name):
    flat, where = [], []
    for k in SMALL_LAYER:
        parts = layer_grads[k] if k == "b_gate" else [layer_grads[k]]
        for i, part in enumerate(parts):
            flat.append(part)
            where.append(_small_offset(k) + i * (D // 128))
    if final is not None:
        flat.append(final)
        where.append(_small_offset("final_norm"))
    pack_rows = LAYER_ROWS + (SMALL_ROWS["final_norm"] if final is not None else 0)

    def body(me_ref, *refs):
        del me_ref
        o_ref = refs[-1]
        for ref, off in zip(refs[:-1], where):
            if len(ref.shape) == 2:
                for j in range(ref.shape[1] // 128):
                    o_ref[0, 0, off + j:off + j + 1, :] = ref[:, 128 * j:128 * (j + 1)]
            else:
                rows = ref.shape[1]
                for g in range(4):
                    o_ref[0, 0, off + g * rows:off + (g + 1) * rows, :] = ref[g]

    grid_spec = pltpu.PrefetchScalarGridSpec(
        num_scalar_prefetch=1, grid=(1,),
        in_specs=[pl.BlockSpec(a.shape, (lambda i, me_ref, nd=a.ndim: (0,) * nd)) for a in flat],
        out_specs=_bs((1, 1, pack_rows, 128), lambda i, me_ref: (0, me_ref[0], 0, 0)))
    return pl.pallas_call(body, name=name, grid_spec=grid_spec, out_shape=_sds((1, NDEV, pack_rows, 128), F32),
                          interpret=False)(me, *flat)


def _adamw_small(gathered, w, mom, var, name):
    names = SMALL_LAYER + ("final_norm",)
    ins = list(gathered) + [t[k] for k in names for t in (w, mom, var)]

    def body(*refs):
        outs = refs[len(ins):]

        def rows(l, off, n):
            land = refs[l]
            s = land[0, 0, off:off + n, :]
            for d in range(1, NDEV):
                s = s + land[0, d, off:off + n, :]
            return s

        def as_row(l, off, n):
            s = rows(l, off, n)
            return jnp.concatenate([s[j:j + 1, :] for j in range(n)], axis=1)

        def update(i, idx, g):
            wr, mr, vr = (refs[DEPTH + 3 * i + t] for t in range(3))
            d, m2, v2 = _adam_math(wr[idx], g, mr[idx], vr[idx])
            for o, val in zip(outs[4 * i:4 * i + 4], (g, d, m2, v2)):
                o[idx] = val

        for i, k in enumerate(names):
            n = SMALL_ROWS[k]
            off = _small_offset(k)
            if k == "final_norm":
                update(i, (slice(0, 1),), as_row(DEPTH - 1, off, n))
                continue
            for l in range(DEPTH):
                if k in ("b_w_s", "c_w"):
                    for g in range(4):
                        update(i, (l, g), rows(l, off + g * BLK, BLK))
                elif k in ("b_b_s", "c_scale"):
                    update(i, (l,), rows(l, off, n))
                else:
                    update(i, (slice(l, l + 1),), as_row(l, off, n))

    out_shape = [_sds(w[k].shape, F32) for k in names for _ in range(4)]
    outs = pl.pallas_call(body, name=name, out_shape=out_shape,
                          compiler_params=pltpu.CompilerParams(vmem_limit_bytes=48 << 20), interpret=False)(*ins)
    return tuple({k: outs[4 * i + t] for i, k in enumerate(names)} for t in range(4))


PG, PD = 0, 2


def _ffn_fwd(x, h, wf, tag, mid, next_gain):
    g_pre, u_pre, act = _gu_swiglu(h, wf, tm=2048, tn=256, b_row0=PG * FF, name=f"gu_{tag}")
    token = mid(act)
    out = _mm(act, wf, mode="nn", tm=512, tn=D, tk=FF, n=D, b_row0=PD * FF, out_dtype=F32, scale=0.5, res=x, after=token,
              norm_gain=next_gain, name=f"down_{tag}")
    out, h_next = out if next_gain is not None else (out, None)
    return out, h_next, (x, h, g_pre, u_pre, act)


def _ffn_bwd(dout, saved, gain, wf, tag, after, emit):
    x, h, g_pre, u_pre, act = saved
    grf = lax.empty((3, NDEV, FFS, D), BF16)
    dgu = _dact_dswiglu(dout, wf, g_pre, u_pre, tm=256, b_row0=PD * FF, after=after, name=f"dgu_{tag}")
    grf = _wgrad(act, dout, grf, piece0=PD, rows=FFS, tm=FF // 2, tn=D, tk=1024, scale=0.5, name=f"wg_down_{tag}")
    grf = _wgrad(dgu, h, grf, piece0=PG, rows=FFS, tm=FF // 2, tn=D, tk=2048, name=f"wg_gu_{tag}")
    token = emit([grf])
    dx, dgain = _mm_drms(dgu, wf, x, gain, dout, tm=256, b_row0=PG * FF, after=token, name=f"dx_{tag}")
    return dx, dgain, token


def _mixer_fwd(x, h, sp, ww, wo, wa, wbc, tag, mid, next_gain):
    zq = _mm(h, ww, mode="nt", tm=1024, tn=1152, tk=D, n=QKV_W, out_dtype=F32, name=f"zq_{tag}")
    z = _mm(h, ww, mode="nt", tm=1024, tn=1152, tk=D, n=INW - QKV_W, b_row0=QKV_W, out_dtype=BF16, name=f"z_{tag}")
    token = mid(z)
    acc, ml = _attn_fwd(zq, None, None, gi=0, first=True, final=False, name=f"attn0_{tag}")
    acc, ml = _attn_fwd(zq, acc, ml, gi=1, first=False, final=False, name=f"attn1_{tag}")
    ya, lse = _attn_fwd(zq, acc, ml, gi=2, first=False, final=True, name=f"attn2_{tag}")
    yb = _gmlp_fwd(z, sp["b_ln_g"], sp["b_ln_b"], sp["b_w_s"], sp["b_b_s"], f"gmlp_{tag}")
    yc = _pool_fwd(z, sp["c_w"], sp["c_scale"], f"pool_{tag}")
    pa = _mm(ya, wa, mode="nt", tm=1024, tn=D, tk=GW, n=D, out_dtype=BF16, after=token, name=f"proj_a_{tag}")
    pb = _mm(yb, wbc, mode="nt", tm=1024, tn=D, tk=BW, n=D, b_row0=0, out_dtype=BF16, name=f"proj_b_{tag}")
    pc = _mm(yc, wbc, mode="nt", tm=1024, tn=D, tk=BW, n=D, b_row0=D, out_dtype=BF16, name=f"proj_c_{tag}")
    merged = _gates_fwd(z, sp["b_gate"], pa, pb, pc, f"gates_{tag}")
    out, h_next = _mm(merged, wo, mode="nn", tm=1024, tn=D, tk=D, n=D, out_dtype=F32, res=x, norm_gain=next_gain,
                      name=f"out_{tag}")
    return out, h_next, (x, h, zq, z, ya, lse, yb, yc, pa, pb, pc, merged)


def _mixer_bwd(dout, saved, sp, ww, wo, wa, wbc, tag, after, emit):
    x, h, zq, z, ya, lse, yb, yc, pa, pb, pc, merged = saved
    grw = lax.empty((1, NDEV, INS, D), BF16)
    gro = lax.empty((1, NDEV, DS, D), BF16)
    gra = lax.empty((1, NDEV, DS, GW), BF16)
    grbc = lax.empty((2, NDEV, DS, BW), BF16)
    dm = _mm(dout, wo, mode="nt", tm=1024, tn=D, tk=D, n=D, out_dtype=F32, after=after, name=f"dmerged_{tag}")
    gro = _wgrad(merged, dout, gro, piece0=0, rows=DS, tm=D, tn=D, tk=1024, name=f"wg_out_{tag}")
    dpa, dpb, dpc, dza, dzb, dzc, dba, dbb, dbc = _gates_bwd(z, sp["b_gate"], pa, pb, pc, dm, f"dgates_{tag}")
    dya = _mm(dpa, wa, mode="nn", tm=2048, tn=GW, tk=D, n=GW, out_dtype=F32, name=f"dya_{tag}")
    dyb = _mm(dpb, wbc, mode="nn", tm=2048, tn=BW, tk=D, n=BW, b_row0=0, out_dtype=F32, name=f"dyb_{tag}")
    dyc = _mm(dpc, wbc, mode="nn", tm=2048, tn=BW, tk=D, n=BW, b_row0=D, out_dtype=F32, name=f"dyc_{tag}")
    gra = _wgrad(dpa, ya, gra, piece0=0, rows=DS, tm=D, tn=GW, tk=1024, name=f"wg_pa_{tag}")
    grbc = _wgrad(dpb, yb, grbc, piece0=0, rows=DS, tm=D, tn=BW, tk=1024, name=f"wg_pb_{tag}")
    grbc = _wgrad(dpc, yc, grbc, piece0=1, rows=DS, tm=D, tn=BW, tk=1024, name=f"wg_pc_{tag}")
    dqkv = [_attn_bwd(zq, dya, ya, lse, gi=gi, name=f"dattn{gi}_{tag}") for gi in range(3)]
    duv, dws, dbs, dlng, dlnb = _gmlp_bwd(z, dyb, sp["b_ln_g"], sp["b_ln_b"], sp["b_w_s"], sp["b_b_s"], f"dgmlp_{tag}")
    dxc, dcw, dcs = _pool_bwd(z, dyc, sp["c_w"], sp["c_scale"], f"dpool_{tag}")
    dz = _concat_bf16([dqkv[gi][part] for part in range(3) for gi in range(3)] + [duv, dxc, dza, dzb, dzc], f"dz_{tag}")
    grw = _wgrad(dz, h, grw, piece0=0, rows=INS, tm=INW // 2, tn=512, tk=1024, name=f"wg_in_{tag}")
    token = emit([grw, gro, gra, grbc])
    dx, dgain = _mm_drms(dz, ww, x, sp["mix_norm"], dout, tm=256, b_row0=0, after=token, name=f"dx_mix_{tag}")
    small = {"mix_norm": dgain, "b_gate": [dba, dbb, dbc], "b_ln_g": dlng, "b_ln_b": dlnb,
             "b_w_s": dws, "b_b_s": dbs, "c_w": dcw, "c_scale": dcs}
    return dx, small, token


PARTS = ("f1", "mix", "f2")


def _forward(xs, tgt, weights_of, pass_on, small, final_gain):
    saved, views = [], []
    cur = xs
    h = _rms_fwd(xs, small[0]["ffn1_norm"], "rms_first")
    for l in range(DEPTH):
        last = l + 1 == DEPTH
        (wf1,) = weights_of(l, "f1", cur)
        cur, h, s1 = _ffn_fwd(cur, h, wf1, f"f1l{l}", lambda t: pass_on(l, "mix", t), small[l]["mix_norm"])
        wmix = weights_of(l, "mix", cur)
        cur, h, s2 = _mixer_fwd(cur, h, small[l], *wmix, f"l{l}", lambda t: pass_on(l, "f2", t), small[l]["ffn2_norm"])
        (wf2,) = weights_of(l, "f2", cur)
        cur, h, s3 = _ffn_fwd(cur, h, wf2, f"f2l{l}", lambda t: None if last else pass_on(l + 1, "f1", t),
                              None if last else small[l + 1]["ffn1_norm"])
        saved.append((s1, s2, s3))
        views.append((wf1, wmix, wf2))
    loss_part, dcur, dfinal = _final_loss(cur, final_gain, tgt, "final_loss")
    return loss_part, dcur, dfinal, saved, views


def _backward_layer(dcur, saved_l, small_l, views_l, l, emit, after):
    wf1, wmix, wf2 = views_l
    s1, s2, s3 = saved_l
    dcur, dn2, token = _ffn_bwd(dcur, s3, small_l["ffn2_norm"], wf2, f"f2l{l}", after, lambda g: emit("f2", g))
    dcur, sg, token = _mixer_bwd(dcur, s2, small_l, *wmix, f"l{l}", token, lambda g: emit("mix", g))
    dcur, dn1, token = _ffn_bwd(dcur, s1, small_l["ffn1_norm"], wf1, f"f1l{l}", token, lambda g: emit("f1", g))
    sg["ffn1_norm"] = dn1
    sg["ffn2_norm"] = dn2
    return dcur, sg, token


BIG = ("ffn1_w_gate", "ffn1_w_up", "ffn1_w_down", "w_in", "w_proj_a", "w_proj_b", "w_proj_c", "w_out",
       "ffn2_w_gate", "ffn2_w_up", "ffn2_w_down")
ALL_WEIGHTS = ("ffn1_norm", "ffn1_w_gate", "ffn1_w_up", "ffn1_w_down", "mix_norm", "w_in", "b_gate", "b_ln_g", "b_ln_b",
               "b_w_s", "b_b_s", "c_w", "c_scale", "w_proj_a", "w_proj_b", "w_proj_c", "w_out", "ffn2_norm", "ffn2_w_gate",
               "ffn2_w_up", "ffn2_w_down", "final_norm")


def _place_shards(w, l, part, me, after, name):
    tr = lambda a: a.transpose(0, 2, 1)
    if part == "mix":
        srcs = [tr(w["w_in"]), w["w_out"], tr(w["w_proj_a"]), tr(w["w_proj_b"]), tr(w["w_proj_c"])]
        groups = [[0], [1], [2], [3, 4]]
    else:
        f = "ffn1" if part == "f1" else "ffn2"
        srcs = [tr(w[f + "_w_gate"]), tr(w[f + "_w_up"]), w[f + "_w_down"]]
        groups = [[0, 1, 2]]

    def body(me_ref, *refs):
        del me_ref
        outs = refs[len(srcs) + 1:]
        for o_ref, members in zip(outs, groups):
            for p, i in enumerate(members):
                o_ref[p, 0] = refs[i][0].astype(BF16)

    out_shape = [_sds((len(g), NDEV) + srcs[g[0]].shape[1:], BF16) for g in groups]
    grid_spec = pltpu.PrefetchScalarGridSpec(
        num_scalar_prefetch=1, grid=(1,),
        in_specs=[_bs((1,) + s.shape[1:], lambda i, me_ref: (l, 0, 0)) for s in srcs] + [pl.BlockSpec(memory_space=pl.ANY)],
        out_specs=[_bs((len(g), 1) + srcs[g[0]].shape[1:], lambda i, me_ref: (0, me_ref[0], 0, 0)) for g in groups])
    return pl.pallas_call(body, name=name, grid_spec=grid_spec, out_shape=out_shape,
                          compiler_params=pltpu.CompilerParams(vmem_limit_bytes=40 << 20), interpret=False)(me, *srcs, after)


def _part_views(part, gathered):
    if part == "mix":
        pw, po, pa, pbc = gathered
        return [pw.reshape(INW, D), po.reshape(D, D), pa.reshape(D, GW), pbc.reshape(2 * D, BW)]
    return [gathered[0].reshape(3 * FF, D)]


def _part_pieces(part):
    if part == "mix":
        return [(0, 0, "w_in", True), (1, 0, "w_out", False), (2, 0, "w_proj_a", True), (3, 0, "w_proj_b", True),
                (3, 1, "w_proj_c", True)]
    f = "ffn1" if part == "f1" else "ffn2"
    return [(0, 0, f + "_w_gate", True), (0, 1, f + "_w_up", True), (0, 2, f + "_w_down", False)]


def _layer_small(w, l):
    r = lambda a, shape: a.reshape(shape)
    return {"ffn1_norm": r(w["ffn1_norm"][l], (1, D)), "mix_norm": r(w["mix_norm"][l], (1, D)),
            "b_gate": r(w["b_gate"][l], (1, 3 * D)), "b_ln_g": r(w["b_ln_g"][l], (1, BW)), "b_ln_b": r(w["b_ln_b"][l], (1, BW)),
            "b_w_s": w["b_w_s"][l], "b_b_s": r(w["b_b_s"][l], (4, BLK, 1)), "c_w": w["c_w"][l],
            "c_scale": r(w["c_scale"][l], (4, 1, BLK)), "ffn2_norm": r(w["ffn2_norm"][l], (1, D))}


def kernel(x, ffn1_norm, ffn1_w_gate, ffn1_w_up, ffn1_w_down, mix_norm, w_in, b_gate, b_ln_g, b_ln_b, b_w_s, b_b_s, c_w, c_scale, w_proj_a, w_proj_b, w_proj_c, w_out, ffn2_norm, ffn2_w_gate, ffn2_w_up, ffn2_w_down, final_norm, loss_target, m_ffn1_norm, m_ffn1_w_gate, m_ffn1_w_up, m_ffn1_w_down, m_mix_norm, m_w_in, m_b_gate, m_b_ln_g, m_b_ln_b, m_b_w_s, m_b_b_s, m_c_w, m_c_scale, m_w_proj_a, m_w_proj_b, m_w_proj_c, m_w_out, m_ffn2_norm, m_ffn2_w_gate, m_ffn2_w_up, m_ffn2_w_down, m_final_norm, v_ffn1_norm, v_ffn1_w_gate, v_ffn1_w_up, v_ffn1_w_down, v_mix_norm, v_w_in, v_b_gate, v_b_ln_g, v_b_ln_b, v_b_w_s, v_b_b_s, v_c_w, v_c_scale, v_w_proj_a, v_w_proj_b, v_w_proj_c, v_w_out, v_ffn2_norm, v_ffn2_w_gate, v_ffn2_w_up, v_ffn2_w_down, v_final_norm):
    w = dict(ffn1_norm=ffn1_norm, ffn1_w_gate=ffn1_w_gate, ffn1_w_up=ffn1_w_up, ffn1_w_down=ffn1_w_down, mix_norm=mix_norm,
             w_in=w_in, b_gate=b_gate, b_ln_g=b_ln_g, b_ln_b=b_ln_b, b_w_s=b_w_s, b_b_s=b_b_s, c_w=c_w, c_scale=c_scale,
             w_proj_a=w_proj_a, w_proj_b=w_proj_b, w_proj_c=w_proj_c, w_out=w_out, ffn2_norm=ffn2_norm, ffn2_w_gate=ffn2_w_gate,
             ffn2_w_up=ffn2_w_up, ffn2_w_down=ffn2_w_down, final_norm=final_norm)
    mom = dict(ffn1_norm=m_ffn1_norm, ffn1_w_gate=m_ffn1_w_gate, ffn1_w_up=m_ffn1_w_up, ffn1_w_down=m_ffn1_w_down,
               mix_norm=m_mix_norm, w_in=m_w_in, b_gate=m_b_gate, b_ln_g=m_b_ln_g, b_ln_b=m_b_ln_b, b_w_s=m_b_w_s, b_b_s=m_b_b_s,
               c_w=m_c_w, c_scale=m_c_scale, w_proj_a=m_w_proj_a, w_proj_b=m_w_proj_b, w_proj_c=m_w_proj_c, w_out=m_w_out,
               ffn2_norm=m_ffn2_norm, ffn2_w_gate=m_ffn2_w_gate, ffn2_w_up=m_ffn2_w_up, ffn2_w_down=m_ffn2_w_down,
               final_norm=m_final_norm)
    var = dict(ffn1_norm=v_ffn1_norm, ffn1_w_gate=v_ffn1_w_gate, ffn1_w_up=v_ffn1_w_up, ffn1_w_down=v_ffn1_w_down,
               mix_norm=v_mix_norm, w_in=v_w_in, b_gate=v_b_gate, b_ln_g=v_b_ln_g, b_ln_b=v_b_ln_b, b_w_s=v_b_w_s, b_b_s=v_b_b_s,
               c_w=v_c_w, c_scale=v_c_scale, w_proj_a=v_w_proj_a, w_proj_b=v_w_proj_b, w_proj_c=v_w_proj_c, w_out=v_w_out,
               ffn2_norm=v_ffn2_norm, ffn2_w_gate=v_ffn2_w_gate, ffn2_w_up=v_ffn2_w_up, ffn2_w_down=v_ffn2_w_down,
               final_norm=v_final_norm)

    t_dim = x.shape[1]
    xs = x.reshape(t_dim, D)
    tgt = loss_target.reshape(t_dim, D)

    me = 4 * lax.axis_index("x") + 2 * lax.axis_index("y") + lax.axis_index("c")
    me_arr = me.astype(jnp.int32).reshape(1)
    gathers = {}
    token = jnp.zeros((8, 128), F32)
    for l in range(DEPTH):
        for part in PARTS:
            lands = _place_shards(w, l, part, me_arr, token, f"place_{part}_l{l}")
            gathers[l, part] = _gather_start(lands, token, f"gather_start_{part}_l{l}")
            token = gathers[l, part][-1]

    def pass_on(l, part, after):
        gathers[l, part] = _gather_forward(gathers[l, part], after, f"gather_forward_{part}_l{l}")
        return gathers[l, part][-1]

    pass_on(0, "f1", token)

    def weights_of(l, part, after):
        return _part_views(part, _gather_wait(gathers[l, part], after, f"gather_wait_{part}_l{l}"))

    small = [_layer_small(w, l) for l in range(DEPTH)]
    loss_part, dcur, dfinal, saved, views = _forward(xs, tgt, weights_of, pass_on, small, w["final_norm"].reshape(1, D))
    loss = lax.psum(loss_part[0, 0], ("x", "y", "c"))

    scatters = []
    small_gathers = [None] * DEPTH
    token = None
    for l in reversed(range(DEPTH)):
        def emit(part, grads, l=l):
            scatters.append((l, part, _exchange_start("scatter", grads, dfinal, f"scatter_start_{part}_l{l}")))
            return scatters[-1][2][-1]
        dcur, small_grads, token = _backward_layer(dcur, saved[l], small[l], views[l], l, emit, token)
        packed = _pack_small_grads(small_grads, dfinal if l == DEPTH - 1 else None, me_arr, f"pack_small_grads_l{l}")
        small_gathers[l] = _exchange_start("gather", [packed], token, f"gather_start_small_l{l}")
        token = small_gathers[l][-1]
    grad_x = dcur.reshape(x.shape)

    done = {}
    behind = token
    for l, part, started in scatters:
        arrs = _exchange_wait("scatter", started, behind, f"scatter_wait_{part}_l{l}")
        na = len(arrs) // 2
        for a, piece, k, transposed in _part_pieces(part):
            flip = transposed and w[k].shape[2] % 128 != 0
            view = (lambda t: t.transpose(0, 2, 1)) if flip else (lambda t: t)
            done[k] = _reduce_update(arrs[na + a], arrs[a], me_arr, piece, view(w[k]), view(mom[k]), view(var[k]), l,
                                     done.get(k), transposed and not flip, f"update_{k}_l{l}")
            behind = done[k][1]
    unflip = lambda k, t: t.transpose(0, 2, 1) if t.shape != w[k].shape else t
    grads, delta, new_m, new_v = ({k: unflip(k, done[k][t]) for k in BIG} for t in range(4))
    gathered = [None] * DEPTH
    for l in reversed(range(DEPTH)):
        gathered[l] = _exchange_wait("gather", small_gathers[l], behind, f"gather_wait_small_l{l}")[0]
        behind = gathered[l]
    as_row = lambda t: dict(t, final_norm=t["final_norm"].reshape(1, D))
    for dst, src in zip((grads, delta, new_m, new_v), _adamw_small(gathered, as_row(w), as_row(mom), as_row(var), "adamw_small")):
        dst.update(src)
        dst["final_norm"] = src["final_norm"].reshape(D)

    return (loss, grad_x, *[grads[k] for k in ALL_WEIGHTS], *[delta[k] for k in ALL_WEIGHTS],
            *[new_m[k] for k in ALL_WEIGHTS], *[new_v[k] for k in ALL_WEIGHTS])
```

```python
import math

import jax
import jax.numpy as jnp
from jax import lax
from jax.experimental import pallas as pl
from jax.experimental.pallas import tpu as pltpu

F32 = jnp.float32
BF16 = jnp.bfloat16
MESH = pl.DeviceIdType.MESH

NDEV = 8
DEPTH = 2
D = 1024
FF = 2816
FFS = FF // NDEV
INW = 6912
INS = INW // NDEV
DS = D // NDEV
BLK = 128
NH = 4
HD = 64
GW = NH * HD
DILS = (1, 4, 16)
QK_SCALE = 1.0 / math.sqrt(HD)
QKV_W = 3 * len(DILS) * GW
Q_B, K_B, V_B = 0, 3, 6
UV_B = 0
C_B128 = 8
G_B = 6
BW = 512
EPS = 1e-6
NEG = -1e30
VMEM_CAP = 60 * 1024 * 1024

ADAM_LR, ADAM_B1, ADAM_B2, ADAM_EPS, ADAM_WD, ADAM_STEP = 0.001, 0.9, 0.999, 1e-08, 0.01, 10

NT = (((1,), (1,)), ((), ()))
TN = (((0,), (0,)), ((), ()))


def _bs(shape, imap):
    return pl.BlockSpec(shape, imap)


def _pc(body, *, name, grid, in_specs, out_specs, out_shape, scratch=(), alias=None, vmem=None):
    params = {}
    if vmem is not None:
        params["vmem_limit_bytes"] = min(int(vmem), VMEM_CAP)
    big = lambda s: math.prod(s.shape) * jnp.dtype(s.dtype).itemsize >= (1 << 20)
    pin = lambda s: pltpu.HBM(s.shape, s.dtype) if big(s) else s
    out_shape = [pin(s) for s in out_shape] if isinstance(out_shape, (list, tuple)) else pin(out_shape)
    call = pl.pallas_call(
        body, name=name, grid=grid, in_specs=in_specs, out_specs=out_specs, out_shape=out_shape,
        scratch_shapes=list(scratch), input_output_aliases=alias or {},
        compiler_params=pltpu.CompilerParams(**params), interpret=False)
    return lambda *args: call(*[pltpu.with_memory_space_constraint(a, pltpu.HBM) if big(a) else a for a in args])


def _sds(shape, dtype):
    return jax.ShapeDtypeStruct(shape, dtype)


def _mm(a, b, *, mode, tm, tn, tk, n, out_dtype, name, b_row0=0, scale=None, res=None, after=None, norm_gain=None):
    m, k_dim = a.shape
    nk = k_dim // tk
    assert m % tm == 0 and n % tn == 0 and k_dim % tk == 0 and (norm_gain is None or tn == n)
    b_off, rem = divmod(b_row0, tk if mode == "nn" else tn)
    assert rem == 0
    grid = (m // tm, n // tn, nk)
    in_specs = [_bs((tm, tk), lambda i, j, k: (i, k))]
    if mode == "nn":
        in_specs.append(_bs((tk, tn), lambda i, j, k: (b_off + k, j)))
    else:
        in_specs.append(_bs((tn, tk), lambda i, j, k: (b_off + j, k)))
    args = [a, b]
    if res is not None:
        in_specs.append(_bs((tm, tn), lambda i, j, k: (i, j)))
        args.append(res)
    if norm_gain is not None:
        in_specs.append(_bs((1, tn), lambda i, j, k: (0, 0)))
        args.append(norm_gain)
    if after is not None:
        in_specs.append(pl.BlockSpec(memory_space=pl.ANY))
        args.append(after)
    n_in = len(args)

    def body(*refs):
        a_ref, b_ref = refs[0], refs[1]
        r_ref = refs[2] if res is not None else None
        g_ref = refs[2 + (res is not None)] if norm_gain is not None else None
        o_ref = refs[n_in]
        av = a_ref[...].astype(BF16)
        bv = b_ref[...].astype(BF16)
        if mode == "nn":
            p = jnp.dot(av, bv, preferred_element_type=F32)
        else:
            p = lax.dot_general(av, bv, NT, preferred_element_type=F32)

        def fin(v):
            if scale is not None:
                v = v * scale
            if r_ref is not None:
                v = r_ref[...] + v
            o_ref[...] = v.astype(out_dtype)
            if g_ref is not None:
                r = lax.rsqrt(jnp.mean(v * v, axis=-1, keepdims=True) + EPS)
                refs[n_in + 1][...] = (v * r * g_ref[...]).astype(BF16)

        if nk == 1:
            fin(p)
        else:
            acc_ref = refs[-1]
            kk = pl.program_id(2)

            @pl.when(kk == 0)
            def _():
                acc_ref[...] = p

            @pl.when(kk > 0)
            def _():
                acc_ref[...] += p

            @pl.when(kk == nk - 1)
            def _():
                fin(acc_ref[...])

    scratch = [pltpu.VMEM((tm, tn), F32)] if nk > 1 else []
    est = 2 * (tm * tk * a.dtype.itemsize + tk * tn * b.dtype.itemsize + tm * tn * jnp.dtype(out_dtype).itemsize)
    est += (2 + (nk > 1)) * tm * tn * 4 + (2 * tm * tn * 4 if res is not None else 0)
    est += (tm * tk * 2 if a.dtype != BF16 else 0) + (tk * tn * 2 if b.dtype != BF16 else 0)
    tile = _bs((tm, tn), lambda i, j, k: (i, j))
    if norm_gain is None:
        return _pc(body, name=name, grid=grid, in_specs=in_specs, out_specs=tile, out_shape=_sds((m, n), out_dtype),
                   scratch=scratch, vmem=est + (8 << 20))(*args)
    return _pc(body, name=name, grid=grid, in_specs=in_specs, out_specs=[tile, tile],
               out_shape=[_sds((m, n), out_dtype), _sds((m, n), BF16)], scratch=scratch,
               vmem=est + 3 * tm * tn * 4 + (8 << 20))(*args)


def _wgrad(a, b, buf, *, piece0, rows, tm, tn, tk, name, scale=None):
    t_dim, m = a.shape
    n = b.shape[1]
    db = tm // rows
    tpp = NDEV // db
    nk = t_dim // tk
    assert tm % rows == 0 and NDEV % db == 0 and m % tm == 0 and n % tn == 0 and t_dim % tk == 0
    grid = (m // tm, n // tn, nk)

    def body(a_ref, b_ref, buf_ref, o_ref, acc_ref):
        del buf_ref
        kk = pl.program_id(2)
        p = lax.dot_general(a_ref[...].astype(BF16), b_ref[...].astype(BF16), TN, preferred_element_type=F32)

        @pl.when(kk == 0)
        def _():
            acc_ref[...] = p

        @pl.when(kk > 0)
        def _():
            acc_ref[...] += p

        @pl.when(kk == nk - 1)
        def _():
            for d in range(db):
                v = acc_ref[d * rows:(d + 1) * rows, :]
                if scale is not None:
                    v = v * scale
                o_ref[0, d] = v.astype(BF16)

    est = 2 * (tk * tm * a.dtype.itemsize + tk * tn * b.dtype.itemsize + tm * tn * 2) + 3 * tm * tn * 4 + tk * tm * 4
    return _pc(
        body, name=name, grid=grid,
        in_specs=[_bs((tk, tm), lambda i, j, k: (k, i)), _bs((tk, tn), lambda i, j, k: (k, j)),
                  pl.BlockSpec(memory_space=pl.ANY)],
        out_specs=_bs((1, db, rows, tn), lambda i, j, k: (piece0 + i // tpp, i % tpp, 0, j)),
        out_shape=_sds(buf.shape, buf.dtype), scratch=[pltpu.VMEM((tm, tn), F32)], alias={2: 0},
        vmem=est + (8 << 20))(a, b, buf)


def _mm_drms(a, b, x, gain, dres, *, tm, b_row0, after, name):
    m, k_dim = a.shape
    assert m % tm == 0 and b_row0 % k_dim == 0 and b.shape[1] == D
    k_blk = b_row0 // k_dim
    after = jnp.zeros((8, 128), F32) if after is None else after

    def body(a_ref, b_ref, x_ref, g_ref, dr_ref, after_ref, dx_ref, dg_ref):
        del after_ref
        dh_v = jnp.dot(a_ref[...].astype(BF16), b_ref[...].astype(BF16), preferred_element_type=F32)
        xv = x_ref[...]
        r = lax.rsqrt(jnp.mean(xv * xv, axis=-1, keepdims=True) + EPS)
        dyg = dh_v * g_ref[...]
        mq = jnp.mean(dyg * xv, axis=-1, keepdims=True)
        dx_ref[...] = dr_ref[...] + r * (dyg - xv * (r * r * mq))

        @pl.when(pl.program_id(0) == 0)
        def _():
            dg_ref[...] = jnp.zeros_like(dg_ref)

        dg_ref[...] += jnp.sum(dh_v * (xv * r), axis=0, keepdims=True)

    row = _bs((tm, D), lambda i: (i, 0))
    est = 2 * (tm * k_dim * a.dtype.itemsize + k_dim * D * b.dtype.itemsize + 3 * tm * D * 4) + 4 * tm * D * 4
    return _pc(body, name=name, grid=(m // tm,),
               in_specs=[_bs((tm, k_dim), lambda i: (i, 0)), _bs((k_dim, D), lambda i: (k_blk, 0)), row,
                         _bs((1, D), lambda i: (0, 0)), row, pl.BlockSpec(memory_space=pl.ANY)],
               out_specs=[row, _bs((1, D), lambda i: (0, 0))], out_shape=[_sds((m, D), F32), _sds((1, D), F32)],
               vmem=est + (8 << 20))(a, b, x, gain, dres, after)


def _dact_dswiglu(dout, b, g_pre, u_pre, *, tm, b_row0, after, name):
    t_dim = dout.shape[0]
    assert t_dim % tm == 0 and b_row0 % FF == 0
    k_blk = b_row0 // FF
    after = jnp.zeros((8, 128), F32) if after is None else after

    def body(d_ref, b_ref, g_ref, u_ref, after_ref, o_ref):
        del after_ref
        dav = 0.5 * lax.dot_general(d_ref[...].astype(BF16), b_ref[...].astype(BF16), NT, preferred_element_type=F32)
        g = g_ref[...].astype(F32)
        u = u_ref[...].astype(F32)
        s = _sigmoid(g)
        o_ref[:, :FF] = (dav * u * (s * (1.0 + g * (1.0 - s)))).astype(BF16)
        o_ref[:, FF:] = (dav * (g * s)).astype(BF16)

    est = 2 * (tm * D * 4 + FF * D * 2 + 2 * tm * 2 * FF * 2) + 5 * tm * FF * 4
    return _pc(body, name=name, grid=(t_dim // tm,),
               in_specs=[_bs((tm, D), lambda i: (i, 0)), _bs((FF, D), lambda i: (k_blk, 0)),
                         _bs((tm, FF), lambda i: (i, 0)), _bs((tm, FF), lambda i: (i, 0)), pl.BlockSpec(memory_space=pl.ANY)],
               out_specs=_bs((tm, 2 * FF), lambda i: (i, 0)), out_shape=_sds((t_dim, 2 * FF), BF16),
               vmem=est + (8 << 20))(dout, b, g_pre, u_pre, after)


def _rms_fwd(x, gain, name, tm=512):
    t_dim = x.shape[0]

    def body(x_ref, g_ref, o_ref):
        xv = x_ref[...]
        r = lax.rsqrt(jnp.mean(xv * xv, axis=-1, keepdims=True) + EPS)
        o_ref[...] = (xv * r * g_ref[...]).astype(BF16)

    return _pc(body, name=name, grid=(t_dim // tm,),
               in_specs=[_bs((tm, D), lambda i: (i, 0)), _bs((1, D), lambda i: (0, 0))],
               out_specs=_bs((tm, D), lambda i: (i, 0)), out_shape=_sds((t_dim, D), BF16))(x, gain)


def _sigmoid(v):
    return 1.0 / (1.0 + jnp.exp(-v))


def _gu_swiglu(h, b, *, tm, tn, b_row0, name):
    t_dim = h.shape[0]
    assert t_dim % tm == 0 and FF % tn == 0 and b_row0 % tn == 0
    j0 = b_row0 // tn

    def body(h_ref, bg_ref, bu_ref, g_ref, u_ref, a_ref):
        hv = h_ref[...]
        g16 = lax.dot_general(hv, bg_ref[...], NT, preferred_element_type=F32).astype(BF16)
        u16 = lax.dot_general(hv, bu_ref[...], NT, preferred_element_type=F32).astype(BF16)
        g = g16.astype(F32)
        g_ref[...] = g16
        u_ref[...] = u16
        a_ref[...] = (g * _sigmoid(g) * u16.astype(F32)).astype(BF16)

    tile = _bs((tm, tn), lambda i, j: (i, j))
    est = 2 * (tm * D * 2 + 2 * tn * D * 2 + 3 * tm * tn * 2) + 5 * tm * tn * 4
    return _pc(body, name=name, grid=(t_dim // tm, FF // tn),
               in_specs=[_bs((tm, D), lambda i, j: (i, 0)), _bs((tn, D), lambda i, j: (j0 + j, 0)),
                         _bs((tn, D), lambda i, j: (j0 + FF // tn + j, 0))],
               out_specs=[tile] * 3, out_shape=[_sds((t_dim, FF), BF16)] * 3, vmem=est + (8 << 20))(h, b, b)


def _final_loss(x, gain, tgt, name, tm=512):
    t_dim = x.shape[0]

    def body(x_ref, g_ref, t_ref, loss_ref, dx_ref, dg_ref):
        xv = x_ref[...]
        gv = g_ref[...]
        r = lax.rsqrt(jnp.mean(xv * xv, axis=-1, keepdims=True) + EPS)
        xn = xv * r
        err = xn * gv - t_ref[...]
        dy = err * (1.0 / D)
        dyg = dy * gv
        mq = jnp.mean(dyg * xv, axis=-1, keepdims=True)
        dx_ref[...] = r * (dyg - xv * (r * r * mq))

        @pl.when(pl.program_id(0) == 0)
        def _():
            dg_ref[...] = jnp.zeros_like(dg_ref)
            loss_ref[...] = jnp.zeros_like(loss_ref)

        dg_ref[...] += jnp.sum(dy * xn, axis=0, keepdims=True)
        part = 0.5 * jnp.sum(jnp.mean(err * err, axis=-1, keepdims=True), axis=0, keepdims=True)
        loss_ref[...] += jnp.broadcast_to(part, loss_ref.shape)

    return _pc(body, name=name, grid=(t_dim // tm,),
               in_specs=[_bs((tm, D), lambda i: (i, 0)), _bs((1, D), lambda i: (0, 0)), _bs((tm, D), lambda i: (i, 0))],
               out_specs=[_bs((8, 128), lambda i: (0, 0)), _bs((tm, D), lambda i: (i, 0)), _bs((1, D), lambda i: (0, 0))],
               out_shape=[_sds((8, 128), F32), _sds((t_dim, D), F32), _sds((1, D), F32)])(x, gain, tgt)


def _band_mask(n):
    ri = lax.broadcasted_iota(jnp.int32, (BLK, 2 * BLK), 0)
    cj = lax.broadcasted_iota(jnp.int32, (BLK, 2 * BLK), 1)
    dist = ri + BLK - cj
    return (dist >= 0) & (dist <= BLK) & ((cj >= BLK) | (n > 0))


def _halves(unit, imap_rows, col):
    return [_bs((unit, 128), (lambda u, j=j: (imap_rows(u), 2 * col + j))) for j in range(2)]


def _get_rows(pair, r, dil):
    return jnp.concatenate([h[pl.ds(r, BLK, stride=dil), :] for h in pair], axis=1)


def _put_rows(stage, r, dil, val):
    for j in range(2):
        stage[j, pl.ds(r, BLK, stride=dil), :] = val[:, 128 * j:128 * (j + 1)]


def _unstage(stage, out_ref):
    out_ref[:, :128] = stage[0]
    out_ref[:, 128:] = stage[1]


def _attn_fwd(z, acc, ml, *, gi, first, final, name):
    dil = DILS[gi]
    t_dim = z.shape[0]
    nsub = 4 if dil == 1 else 1
    unit = BLK * dil * nsub
    nu = t_dim // unit
    own = lambda u: u
    prev = lambda u: jnp.maximum(u - 1, 0)
    in_specs = (_halves(unit, own, Q_B + gi) + _halves(unit, own, K_B + gi) + _halves(unit, prev, K_B + gi)
                + _halves(unit, own, V_B + gi) + _halves(unit, prev, V_B + gi))
    args = [z] * 10
    st_ml = _bs((unit, 128), lambda u: (u, 0))
    if not first:
        in_specs += _halves(unit, own, 0) + [st_ml]
        args += [acc, acc, ml]

    def body(*refs):
        q2, ko2, kp2, vo2, vp2 = (refs[2 * i:2 * i + 2] for i in range(5))
        o_acc_ref, o_ml_ref, stage = refs[-3], refs[-2], refs[-1]
        lane = lax.broadcasted_iota(jnp.int32, (BLK, GW), 1)
        lane_s = lax.broadcasted_iota(jnp.int32, (BLK, 128), 1)

        def band(rows_at, prev_rows_at, store, valid):
            q = rows_at(q2) * QK_SCALE
            kb = jnp.concatenate([prev_rows_at(kp2, ko2), rows_at(ko2)], axis=0).astype(BF16)
            vb = jnp.concatenate([prev_rows_at(vp2, vo2), rows_at(vo2)], axis=0).astype(BF16)
            if first:
                acc_v = jnp.zeros((BLK, GW), F32)
                ml_v = jnp.where(lane_s < 64, NEG, 0.0).astype(F32)
            else:
                acc_v = rows_at(refs[10:12])
                ml_v = rows_at(refs[12:13])
            heads = [(lane >= h * HD) & (lane < (h + 1) * HD) for h in range(NH)]
            rows = lambda v, h: v[h * BLK:(h + 1) * BLK]
            qs = jnp.concatenate([jnp.where(hm, q, 0.0) for hm in heads], axis=0).astype(BF16)
            s = lax.dot_general(qs, kb, NT, preferred_element_type=F32)
            s = jnp.where(jnp.concatenate([valid] * NH, axis=0), s, NEG)
            m_old = jnp.concatenate([ml_v[:, 16 * h:16 * h + 1] for h in range(NH)], axis=0)
            l_old = jnp.concatenate([ml_v[:, 64 + 16 * h:64 + 16 * h + 1] for h in range(NH)], axis=0)
            m_new = jnp.maximum(m_old, jnp.max(s, axis=-1, keepdims=True))
            alpha = jnp.exp(m_old - m_new)
            p = jnp.exp(s - m_new)
            l_new = alpha * l_old + jnp.sum(p, axis=-1, keepdims=True)
            pv = jnp.dot(p.astype(BF16), vb, preferred_element_type=F32)
            stat = m_new + jnp.log(l_new) if final else m_new
            acc_o, ml_o = acc_v, ml_v
            l_b = jnp.ones((BLK, GW), F32)
            for h, hm in enumerate(heads):
                acc_o = jnp.where(hm, acc_v * rows(alpha, h) + rows(pv, h), acc_o)
                l_b = jnp.where(hm, rows(l_new, h), l_b)
                ml_o = jnp.where((lane_s >= 16 * h) & (lane_s < 16 * h + 16), rows(stat, h), ml_o)
                if not final:
                    ml_o = jnp.where((lane_s >= 64 + 16 * h) & (lane_s < 64 + 16 * h + 16), rows(l_new, h), ml_o)
            store(acc_o / l_b if final else acc_o, ml_o)

        u = pl.program_id(0)
        if dil == 1:
            def block(b):
                return lambda pair: jnp.concatenate([h[pl.ds(b * BLK, BLK), :] for h in pair], axis=1)

            for b in range(nsub):
                def store(acc_val, ml_val, b=b):
                    for j in range(2):
                        stage[j, pl.ds(b * BLK, BLK), :] = acc_val[:, 128 * j:128 * (j + 1)]
                    o_ml_ref[pl.ds(b * BLK, BLK), :] = ml_val

                before = (lambda prev_pair, own_pair, b=b: block(b - 1)(own_pair)) if b else (
                    lambda prev_pair, own_pair: block(nsub - 1)(prev_pair))
                band(block(b), before, store, _band_mask(1 if b else u))
        else:
            valid = _band_mask(u)

            def step(r, carry):
                def store(acc_val, ml_val):
                    _put_rows(stage, r, dil, acc_val)
                    o_ml_ref[pl.ds(r, BLK, stride=dil), :] = ml_val

                band(lambda pair: _get_rows(pair, r, dil), lambda prev_pair, own_pair: _get_rows(prev_pair, r, dil), store, valid)
                return carry

            lax.fori_loop(0, dil, step, 0, unroll=min(dil, 8))
        _unstage(stage, o_acc_ref)

    blocks = (len(in_specs) + 6) * unit * 128 * 4
    return _pc(body, name=name, grid=(nu,), in_specs=in_specs, out_specs=[_bs((unit, GW), lambda u: (u, 0)), st_ml],
               out_shape=[_sds((t_dim, GW), F32), _sds((t_dim, 128), F32)],
               scratch=[pltpu.VMEM((2, unit, 128), F32)], vmem=2 * blocks + (12 << 20))(*args)


def _attn_bwd(z, dy, y, lse, *, gi, name):
    dil = DILS[gi]
    t_dim = z.shape[0]
    unit = BLK * dil
    nu = t_dim // unit

    own = lambda u: jnp.minimum(u, nu - 1)
    prev = lambda u: jnp.maximum(jnp.minimum(u, nu - 1) - 1, 0)
    in_specs = (_halves(unit, own, Q_B + gi) + _halves(unit, own, K_B + gi) + _halves(unit, prev, K_B + gi)
                + _halves(unit, own, V_B + gi) + _halves(unit, prev, V_B + gi)
                + _halves(unit, own, 0) + _halves(unit, own, 0) + [_bs((unit, 128), lambda u: (own(u), 0))])
    tok = _bs((unit, GW), lambda u: (own(u), 0))
    kv_out = _bs((unit, GW), lambda u: (jnp.maximum(u - 1, 0), 0))

    def body(*refs):
        q2, ko2, kp2, vo2, vp2, dy2, y2 = (refs[2 * i:2 * i + 2] for i in range(7))
        lse_ref = refs[14]
        dq_ref, dk_ref, dv_ref, dq_st, dk_st, dv_st, dkc, dvc = refs[15:]
        u = pl.program_id(0)

        @pl.when(u == 0)
        def _():
            dkc[...] = jnp.zeros_like(dkc)
            dvc[...] = jnp.zeros_like(dvc)

        @pl.when(u < nu)
        def _():
            valid = _band_mask(u)
            lane = lax.broadcasted_iota(jnp.int32, (BLK, GW), 1)

            def band(r, carry):
                q = _get_rows(q2, r, dil) * QK_SCALE
                kb = jnp.concatenate([_get_rows(kp2, r, dil), _get_rows(ko2, r, dil)], axis=0).astype(BF16)
                vb = jnp.concatenate([_get_rows(vp2, r, dil), _get_rows(vo2, r, dil)], axis=0).astype(BF16)
                dyv = _get_rows(dy2, r, dil)
                yv = _get_rows(y2, r, dil)
                lse_v = lse_ref[pl.ds(r, BLK, stride=dil), :]
                heads = [(lane >= h * HD) & (lane < (h + 1) * HD) for h in range(NH)]
                qs = jnp.concatenate([jnp.where(hm, q, 0.0) for hm in heads], axis=0).astype(BF16)
                dys = jnp.concatenate([jnp.where(hm, dyv, 0.0) for hm in heads], axis=0)
                lse_s = jnp.concatenate([lse_v[:, 16 * h:16 * h + 1] for h in range(NH)], axis=0)
                s = lax.dot_general(qs, kb, NT, preferred_element_type=F32)
                p = jnp.where(jnp.concatenate([valid] * NH, axis=0), jnp.exp(s - lse_s), 0.0)
                delta = jnp.sum(dys * jnp.concatenate([yv] * NH, axis=0), axis=-1, keepdims=True)
                dys16 = dys.astype(BF16)
                dp = lax.dot_general(dys16, vb, NT, preferred_element_type=F32)
                ds = (p * (dp - delta)).astype(BF16)
                dq_all = jnp.dot(ds, kb, preferred_element_type=F32)
                dq = jnp.zeros((BLK, GW), F32)
                for h, hm in enumerate(heads):
                    dq = jnp.where(hm, dq_all[h * BLK:(h + 1) * BLK], dq)
                dkb = lax.dot_general(ds, qs, TN, preferred_element_type=F32)
                dvb = lax.dot_general(p.astype(BF16), dys16, TN, preferred_element_type=F32)
                _put_rows(dq_st, r, dil, dq * QK_SCALE)
                _put_rows(dk_st, r, dil, _get_rows((dkc.at[0], dkc.at[1]), r, dil) + dkb[:BLK])
                _put_rows(dv_st, r, dil, _get_rows((dvc.at[0], dvc.at[1]), r, dil) + dvb[:BLK])
                _put_rows(dkc, r, dil, dkb[BLK:])
                _put_rows(dvc, r, dil, dvb[BLK:])
                return carry

            lax.fori_loop(0, dil, band, 0, unroll=min(dil, 8))
            _unstage(dq_st, dq_ref)

            @pl.when(u > 0)
            def _():
                _unstage(dk_st, dk_ref)
                _unstage(dv_st, dv_ref)

        @pl.when(u == nu)
        def _():
            _unstage(dkc, dk_ref)
            _unstage(dvc, dv_ref)

    blocks = (15 + 6 + 5) * unit * 128 * 4 * 2
    return _pc(body, name=name, grid=(nu + 1,), in_specs=in_specs, out_specs=[tok, kv_out, kv_out],
               out_shape=[_sds((t_dim, GW), F32)] * 3, scratch=[pltpu.VMEM((2, unit, 128), F32)] * 5,
               vmem=blocks + (12 << 20))(*([z] * 10), dy, dy, y, y, lse)


_INV_SQRT2 = 1.0 / math.sqrt(2.0)
_INV_SQRT2PI = 1.0 / math.sqrt(2.0 * math.pi)


def _gelu(v):
    return 0.5 * v * (1.0 + lax.erf(v * _INV_SQRT2))


def _gelu_grad(v):
    return 0.5 * (1.0 + lax.erf(v * _INV_SQRT2)) + v * (_INV_SQRT2PI * jnp.exp(-0.5 * v * v))


def _gmlp_core(z_refs, lng, lnb, ws_ref, bs_ref):
    u_pre = jnp.concatenate([z_refs[0][...], z_refs[1][...]], axis=1).astype(F32)
    v_pre = jnp.concatenate([z_refs[2][...], z_refs[3][...]], axis=1).astype(F32)
    u = _gelu(u_pre)
    v = _gelu(v_pre)
    xc = v - jnp.mean(v, axis=-1, keepdims=True)
    rstd = lax.rsqrt(jnp.mean(xc * xc, axis=-1, keepdims=True) + EPS)
    xhat = xc * rstd
    vn = xhat * lng + lnb
    ti = lax.broadcasted_iota(jnp.int32, (BLK, BLK), 0)
    si = lax.broadcasted_iota(jnp.int32, (BLK, BLK), 1)
    causal = ti >= si
    vn16 = vn.astype(BF16)
    ws = [jnp.where(causal, ws_ref[g], 0.0).astype(BF16) for g in range(4)]
    mixed = jnp.concatenate(
        [jnp.dot(ws[g], vn16[:, g * BLK:(g + 1) * BLK], preferred_element_type=F32) + bs_ref[g] for g in range(4)], axis=1)
    return u_pre, v_pre, u, xhat, rstd, vn16, ws, causal, mixed


def _gmlp_specs():
    z_specs = [_bs((BLK, GW), (lambda i, c=c: (i, UV_B + c))) for c in range(4)]
    par_specs = [_bs((1, BW), lambda i: (0, 0)), _bs((1, BW), lambda i: (0, 0)),
                 _bs((4, BLK, BLK), lambda i: (0, 0, 0)), _bs((4, BLK, 1), lambda i: (0, 0, 0))]
    return z_specs, par_specs


def _gmlp_fwd(z, lng, lnb, ws, bs, name):
    t_dim = z.shape[0]
    z_specs, par_specs = _gmlp_specs()

    def body(z0, z1, z2, z3, lng_ref, lnb_ref, ws_ref, bs_ref, o_ref):
        core = _gmlp_core((z0, z1, z2, z3), lng_ref[...], lnb_ref[...], ws_ref, bs_ref)
        o_ref[...] = (core[2] * core[8]).astype(BF16)

    return _pc(body, name=name, grid=(t_dim // BLK,), in_specs=z_specs + par_specs,
               out_specs=_bs((BLK, BW), lambda i: (i, 0)), out_shape=_sds((t_dim, BW), BF16))(z, z, z, z, lng, lnb, ws, bs)


def _gmlp_bwd(z, dy, lng, lnb, ws, bs, name):
    t_dim = z.shape[0]
    z_specs, par_specs = _gmlp_specs()

    def body(z0, z1, z2, z3, lng_ref, lnb_ref, ws_ref, bs_ref, dy_ref, duv_ref, dws_ref, dbs_ref, dlng_ref, dlnb_ref):
        lng_v = lng_ref[...]
        u_pre, v_pre, u, xhat, rstd, vn16, wsm, causal, mixed = _gmlp_core((z0, z1, z2, z3), lng_v, lnb_ref[...], ws_ref, bs_ref)
        dyv = dy_ref[...].astype(F32)
        du = dyv * mixed
        dmixed = dyv * u

        @pl.when(pl.program_id(0) == 0)
        def _():
            dws_ref[...] = jnp.zeros_like(dws_ref)
            dbs_ref[...] = jnp.zeros_like(dbs_ref)
            dlng_ref[...] = jnp.zeros_like(dlng_ref)
            dlnb_ref[...] = jnp.zeros_like(dlnb_ref)

        dvn_parts = []
        for g in range(4):
            dm = dmixed[:, g * BLK:(g + 1) * BLK]
            dm16 = dm.astype(BF16)
            dw = lax.dot_general(dm16, vn16[:, g * BLK:(g + 1) * BLK], NT, preferred_element_type=F32)
            dws_ref[g] += jnp.where(causal, dw, 0.0)
            dbs_ref[g] += jnp.sum(dm.T, axis=0, keepdims=True)
            dvn_parts.append(lax.dot_general(wsm[g], dm16, TN, preferred_element_type=F32))
        dvn = jnp.concatenate(dvn_parts, axis=1)
        dlng_ref[...] += jnp.sum(dvn * xhat, axis=0, keepdims=True)
        dlnb_ref[...] += jnp.sum(dvn, axis=0, keepdims=True)
        dxh = dvn * lng_v
        dv = rstd * (dxh - jnp.mean(dxh, axis=-1, keepdims=True) - xhat * jnp.mean(dxh * xhat, axis=-1, keepdims=True))
        duv_ref[:, :BW] = (du * _gelu_grad(u_pre)).astype(BF16)
        duv_ref[:, BW:] = (dv * _gelu_grad(v_pre)).astype(BF16)

    return _pc(body, name=name, grid=(t_dim // BLK,),
               in_specs=z_specs + par_specs + [_bs((BLK, BW), lambda i: (i, 0))],
               out_specs=[_bs((BLK, 2 * BW), lambda i: (i, 0)), _bs((4, BLK, BLK), lambda i: (0, 0, 0)),
                          _bs((4, 1, BLK), lambda i: (0, 0, 0)), _bs((1, BW), lambda i: (0, 0)), _bs((1, BW), lambda i: (0, 0))],
               out_shape=[_sds((t_dim, 2 * BW), BF16), _sds((4, BLK, BLK), F32), _sds((4, 1, BLK), F32),
                          _sds((1, BW), F32), _sds((1, BW), F32)])(z, z, z, z, lng, lnb, ws, bs, dy)


def _pool_core(x, gi):
    t_dim = x.shape[0]
    w = jnp.left_shift(2, gi)
    row = lax.broadcasted_iota(jnp.int32, x.shape, 0)
    s = x
    for k in (1, 2, 4, 8):
        sh = jnp.where(row >= k, pltpu.roll(s, k, 0), 0.0)
        s = jnp.where(k < w, s + sh, s)
    cnt = jnp.minimum(row + 1, w).astype(F32)
    return s / cnt - x, cnt, row, w, t_dim


def _pool_fwd(z, cw, cs, name):
    t_dim = z.shape[0]

    def body(z_ref, cw_ref, cs_ref, o_ref):
        pooled = _pool_core(z_ref[...].astype(F32), pl.program_id(0))[0]
        y = jnp.dot(pooled.astype(BF16), cw_ref[0].astype(BF16), preferred_element_type=F32) * cs_ref[0]
        o_ref[...] = y.astype(BF16)

    return _pc(body, name=name, grid=(4,),
               in_specs=[_bs((t_dim, BLK), lambda g: (0, C_B128 + g)), _bs((1, BLK, BLK), lambda g: (g, 0, 0)),
                         _bs((1, 1, BLK), lambda g: (g, 0, 0))],
               out_specs=_bs((t_dim, BLK), lambda g: (0, g)), out_shape=_sds((t_dim, BW), BF16),
               vmem=40 << 20)(z, cw, cs)


def _pool_bwd(z, dy, cw, cs, name):
    t_dim = z.shape[0]

    def body(z_ref, dy_ref, cw_ref, cs_ref, dx_ref, dcw_ref, dcs_ref):
        pooled, cnt, row, w, _ = _pool_core(z_ref[...].astype(F32), pl.program_id(0))
        p16 = pooled.astype(BF16)
        cw16 = cw_ref[0].astype(BF16)
        dyv = dy_ref[...].astype(F32)
        lin = jnp.dot(p16, cw16, preferred_element_type=F32)
        dcs_ref[0] = jnp.sum(dyv * lin, axis=0, keepdims=True)
        dys = (dyv * cs_ref[0]).astype(BF16)
        dcw_ref[0] = lax.dot_general(p16, dys, TN, preferred_element_type=F32)
        dpool = lax.dot_general(dys, cw16, NT, preferred_element_type=F32)
        sb = dpool / cnt
        for k in (1, 2, 4, 8):
            sh = jnp.where(row < t_dim - k, pltpu.roll(sb, t_dim - k, 0), 0.0)
            sb = jnp.where(k < w, sb + sh, sb)
        dx_ref[...] = (sb - dpool).astype(BF16)

    return _pc(body, name=name, grid=(4,),
               in_specs=[_bs((t_dim, BLK), lambda g: (0, C_B128 + g)), _bs((t_dim, BLK), lambda g: (0, g)),
                         _bs((1, BLK, BLK), lambda g: (g, 0, 0)), _bs((1, 1, BLK), lambda g: (g, 0, 0))],
               out_specs=[_bs((t_dim, BLK), lambda g: (0, g)), _bs((1, BLK, BLK), lambda g: (g, 0, 0)),
                          _bs((1, 1, BLK), lambda g: (g, 0, 0))],
               out_shape=[_sds((t_dim, BW), BF16), _sds((4, BLK, BLK), F32), _sds((4, 1, BLK), F32)],
               vmem=48 << 20)(z, dy, cw, cs)


GATE_W = 2 * GW


def _gates_specs(tm):
    per_branch = D // GATE_W
    z_specs = [_bs((tm, GATE_W), (lambda j, i, b=b: (i, G_B * GW // GATE_W + per_branch * b + j))) for b in range(3)]
    bg_specs = [_bs((1, GATE_W), (lambda j, i, b=b: (0, per_branch * b + j))) for b in range(3)]
    tile = _bs((tm, GATE_W), lambda j, i: (i, j))
    return z_specs, bg_specs, tile


def _gates_fwd(z, bg, pa, pb, pc, name, tm=512):
    t_dim = z.shape[0]
    z_specs, bg_specs, tile = _gates_specs(tm)

    def body(za, zb, zc, ba, bb, bc, pa_ref, pb_ref, pc_ref, o_ref):
        o_ref[...] = (_sigmoid(za[...] + ba[...]) * pa_ref[...] + _sigmoid(zb[...] + bb[...]) * pb_ref[...]
                      + _sigmoid(zc[...] + bc[...]) * pc_ref[...]).astype(BF16)

    return _pc(body, name=name, grid=(D // GATE_W, t_dim // tm), in_specs=z_specs + bg_specs + [tile] * 3,
               out_specs=tile, out_shape=_sds((t_dim, D), BF16), vmem=40 << 20)(z, z, z, bg, bg, bg, pa, pb, pc)


def _gates_bwd(z, bg, pa, pb, pc, dm, name, tm=512):
    t_dim = z.shape[0]
    z_specs, bg_specs, tile = _gates_specs(tm)
    bias_out = _bs((1, GATE_W), lambda j, i: (0, j))

    def body(za, zb, zc, ba, bb, bc, pa_ref, pb_ref, pc_ref, dm_ref, dpa, dpb, dpc, dza, dzb, dzc, dba, dbb, dbc):
        dmv = dm_ref[...]
        first = pl.program_id(1) == 0
        for z_ref, b_ref, p_ref, dp_ref, dz_ref, db_ref in ((za, ba, pa_ref, dpa, dza, dba), (zb, bb, pb_ref, dpb, dzb, dbb),
                                                            (zc, bc, pc_ref, dpc, dzc, dbc)):
            gate = _sigmoid(z_ref[...] + b_ref[...])
            dp_ref[...] = (gate * dmv).astype(BF16)
            dz = dmv * p_ref[...] * (gate * (1.0 - gate))
            dz_ref[...] = dz.astype(BF16)

            @pl.when(first)
            def _():
                db_ref[...] = jnp.zeros_like(db_ref)

            db_ref[...] += jnp.sum(dz, axis=0, keepdims=True)

    return _pc(body, name=name, grid=(D // GATE_W, t_dim // tm), in_specs=z_specs + bg_specs + [tile] * 4,
               out_specs=[tile] * 6 + [bias_out] * 3,
               out_shape=[_sds((t_dim, D), BF16)] * 6 + [_sds((1, D), F32)] * 3, vmem=48 << 20)(
                   z, z, z, bg, bg, bg, pa, pb, pc, dm)


def _concat_bf16(parts, name, tm=512):
    t_dim = parts[0].shape[0]
    offs = [sum(p.shape[1] for p in parts[:i]) for i in range(len(parts) + 1)]

    def body(*refs):
        o_ref = refs[-1]
        for ref, off in zip(refs[:-1], offs):
            o_ref[:, off:off + ref.shape[1]] = ref[...].astype(BF16)

    return _pc(body, name=name, grid=(t_dim // tm,), in_specs=[_bs((tm, p.shape[1]), lambda i: (i, 0)) for p in parts],
               out_specs=_bs((tm, offs[-1]), lambda i: (i, 0)), out_shape=_sds((t_dim, offs[-1]), BF16),
               vmem=48 << 20)(*parts)


def _coords():
    return lax.axis_index("x"), lax.axis_index("y"), lax.axis_index("c")


def _peer(k, x, y, c):
    px = 1 - x if k & 4 else x
    py = 1 - y if k & 2 else y
    pc = 1 - c if k & 1 else c
    return (px, py, pc), 4 * px + 2 * py + pc


def _exchange_copies(kind, bufs, send, recv):
    x, y, c = _coords()
    me = 4 * x + 2 * y + c
    na = len(bufs) if kind == "gather" else len(bufs) // 2
    out = []
    for k in range(1, NDEV):
        to, peer = _peer(k, x, y, c)
        for a in range(na):
            sems = dict(send_sem=send.at[a * 7 + k - 1], recv_sem=recv.at[a * 7 + k - 1], device_id=to, device_id_type=MESH)
            if kind == "gather":
                mine = pltpu.make_async_remote_copy(src_ref=bufs[a].at[:, me], dst_ref=bufs[a].at[:, me], **sems)
                theirs = pltpu.make_async_remote_copy(src_ref=bufs[a].at[:, me], dst_ref=bufs[a].at[:, peer], **sems)
            else:
                src, land = bufs[a], bufs[na + a]
                mine = pltpu.make_async_remote_copy(src_ref=src.at[:, peer], dst_ref=land.at[me], **sems)
                theirs = pltpu.make_async_remote_copy(src_ref=src.at[:, peer], dst_ref=land.at[peer], **sems)
            out.append((mine, theirs))
    return out


_HBM = pl.BlockSpec(memory_space=pltpu.HBM)
_SEM = pl.BlockSpec(memory_space=pltpu.SEMAPHORE)
_EFFECT = pltpu.SideEffectType.DATAFLOW_SIDE_EFFECTING


def _exchange_start(kind, arrays, after, name):
    na = len(arrays)
    if kind == "scatter":
        arrays = list(arrays) + [lax.empty((NDEV, s.shape[0]) + s.shape[2:], s.dtype) for s in arrays]
    nb = len(arrays)

    def body(*refs):
        bufs = refs[:nb]
        send, recv = refs[nb + 1], refs[nb + 2]
        token = refs[-1]
        for mine, _ in _exchange_copies(kind, bufs, send, recv):
            mine.start()
        token[...] = jnp.zeros_like(token)

    arrays = [pltpu.with_memory_space_constraint(b, pltpu.HBM) for b in arrays]
    outs = pl.pallas_call(
        body, name=name,
        out_shape=(pltpu.SemaphoreType.DMA((7 * na,)), pltpu.SemaphoreType.DMA((7 * na,)),
                   *[pltpu.HBM(b.shape, b.dtype) for b in arrays], _sds((8, 128), F32)),
        in_specs=[_HBM] * nb + [pl.BlockSpec(memory_space=pl.ANY)],
        out_specs=(_SEM, _SEM, *([_HBM] * nb), pl.BlockSpec(memory_space=pltpu.VMEM)),
        input_output_aliases={i: 2 + i for i in range(nb)},
        compiler_params=pltpu.CompilerParams(has_side_effects=_EFFECT), interpret=False)(*arrays, after)
    return outs[0], outs[1], list(outs[2:2 + nb]), outs[-1]


def _exchange_wait(kind, started, after, name):
    send, recv, arrays, _ = started
    nb = len(arrays)

    def body(*refs):
        for mine, theirs in _exchange_copies(kind, refs[:nb], refs[nb], refs[nb + 1]):
            mine.wait_send()
            theirs.wait_recv()

    outs = pl.pallas_call(
        body, name=name,
        out_shape=tuple(pltpu.HBM(b.shape, b.dtype) for b in arrays),
        in_specs=[_HBM] * nb + [_SEM, _SEM, pl.BlockSpec(memory_space=pl.ANY)],
        out_specs=tuple([_HBM] * nb),
        input_output_aliases={i: i for i in range(nb)},
        compiler_params=pltpu.CompilerParams(has_side_effects=_EFFECT), interpret=False)(*arrays, send, recv, after)
    return list(outs)


def _adam_math(w, g, m, v):
    m2 = ADAM_B1 * m + (1.0 - ADAM_B1) * g
    v2 = ADAM_B2 * v + (1.0 - ADAM_B2) * jnp.square(g)
    m_hat = m2 / (1.0 - ADAM_B1 ** ADAM_STEP)
    v_hat = v2 / (1.0 - ADAM_B2 ** ADAM_STEP)
    return -ADAM_LR * (m_hat / (jnp.sqrt(v_hat) + ADAM_EPS) + ADAM_WD * w), m2, v2


_DIRECT = (1, 4, 2, 6)


def _gather_copies(bufs, send_d, recv_d, send_f=None, recv_f=None):
    x, y, c = _coords()
    me = 4 * x + 2 * y + c
    sibling, _ = _peer(1, x, y, c)
    direct, forward = [], []
    for a, buf in enumerate(bufs):
        for j, k in enumerate(_DIRECT):
            to, peer = _peer(k, x, y, c)
            sems = dict(send_sem=send_d.at[4 * a + j], recv_sem=recv_d.at[4 * a + j], device_id=to, device_id_type=MESH)
            direct.append((pltpu.make_async_remote_copy(src_ref=buf.at[:, me], dst_ref=buf.at[:, me], **sems),
                           pltpu.make_async_remote_copy(src_ref=buf.at[:, me], dst_ref=buf.at[:, peer], **sems)))
        if send_f is None:
            continue
        for j, k in enumerate(_DIRECT[1:]):
            _, near = _peer(k, x, y, c)
            _, far = _peer(k | 1, x, y, c)
            sems = dict(send_sem=send_f.at[3 * a + j], recv_sem=recv_f.at[3 * a + j], device_id=sibling, device_id_type=MESH)
            forward.append((pltpu.make_async_remote_copy(src_ref=buf.at[:, near], dst_ref=buf.at[:, near], **sems),
                            pltpu.make_async_remote_copy(src_ref=buf.at[:, near], dst_ref=buf.at[:, far], **sems)))
    return direct, forward


def _gather_start(lands, after, name):
    na = len(lands)

    def body(*refs):
        direct, _ = _gather_copies(refs[:na], refs[na + 1], refs[na + 2])
        for mine, _ in direct:
            mine.start()
        refs[-1][...] = jnp.zeros_like(refs[-1])

    lands = [pltpu.with_memory_space_constraint(b, pltpu.HBM) for b in lands]
    outs = pl.pallas_call(
        body, name=name,
        out_shape=(pltpu.SemaphoreType.DMA((4 * na,)), pltpu.SemaphoreType.DMA((4 * na,)),
                   *[pltpu.HBM(b.shape, b.dtype) for b in lands], _sds((8, 128), F32)),
        in_specs=[_HBM] * na + [pl.BlockSpec(memory_space=pl.ANY)],
        out_specs=(_SEM, _SEM, *([_HBM] * na), pl.BlockSpec(memory_space=pltpu.VMEM)),
        input_output_aliases={i: 2 + i for i in range(na)},
        compiler_params=pltpu.CompilerParams(has_side_effects=_EFFECT), interpret=False)(*lands, after)
    return outs[0], outs[1], list(outs[2:2 + na]), outs[-1]


def _gather_forward(started, after, name):
    send_d, recv_d, lands, _ = started
    na = len(lands)

    def body(*refs):
        direct, forward = _gather_copies(refs[:na], refs[na], refs[na + 1], refs[2 * na + 3], refs[2 * na + 4])
        for a in range(na):
            for j in range(3):
                direct[4 * a + 1 + j][1].wait_recv()
                forward[3 * a + j][0].start()
        refs[-1][...] = jnp.zeros_like(refs[-1])

    outs = pl.pallas_call(
        body, name=name,
        out_shape=(*[pltpu.HBM(b.shape, b.dtype) for b in lands], pltpu.SemaphoreType.DMA((3 * na,)),
                   pltpu.SemaphoreType.DMA((3 * na,)), _sds((8, 128), F32)),
        in_specs=[_HBM] * na + [_SEM, _SEM, pl.BlockSpec(memory_space=pl.ANY)],
        out_specs=(*([_HBM] * na), _SEM, _SEM, pl.BlockSpec(memory_space=pltpu.VMEM)),
        input_output_aliases={i: i for i in range(na)},
        compiler_params=pltpu.CompilerParams(has_side_effects=_EFFECT), interpret=False)(*lands, send_d, recv_d, after)
    return send_d, recv_d, outs[na], outs[na + 1], list(outs[:na]), outs[-1]


def _gather_wait(forwarded, after, name):
    send_d, recv_d, send_f, recv_f, lands, _ = forwarded
    na = len(lands)

    def body(*refs):
        direct, forward = _gather_copies(refs[:na], refs[na], refs[na + 1], refs[na + 2], refs[na + 3])
        for a in range(na):
            for j in range(4):
                direct[4 * a + j][0].wait_send()
            direct[4 * a][1].wait_recv()
            for j in range(3):
                forward[3 * a + j][0].wait_send()
                forward[3 * a + j][1].wait_recv()

    outs = pl.pallas_call(
        body, name=name, out_shape=tuple(pltpu.HBM(b.shape, b.dtype) for b in lands),
        in_specs=[_HBM] * na + [_SEM] * 4 + [pl.BlockSpec(memory_space=pl.ANY)], out_specs=tuple([_HBM] * na),
        input_output_aliases={i: i for i in range(na)},
        compiler_params=pltpu.CompilerParams(has_side_effects=_EFFECT), interpret=False)(
            *lands, send_d, recv_d, send_f, recv_f, after)
    return list(outs)


def _reduce_update(landed, own, me, piece, w, m, v, layer, prev, transposed, name):
    rows, cols = landed.shape[2], landed.shape[3]
    tc = min(cols, 512 if rows <= FFS else 256)
    assert cols % tc == 0 and w.shape[1:] == ((cols, rows) if transposed else (rows, cols))
    lane_chunks = [(c0, min(128, rows - c0)) for c0 in range(0, rows, 128)]

    def body(me_ref, b_ref, own_ref, w_ref, m_ref, v_ref, *rest):
        outs = rest[-4:]
        mine = me_ref[0]
        s = None
        for d in range(NDEV):
            other = jnp.where(mine == d, (d + 1) % NDEV, d)
            part = jnp.where(mine == d, own_ref[0, 0], b_ref[other, 0]).astype(F32)
            s = part if s is None else s + part
        if not transposed:
            for o, val in zip(outs, (s,) + _adam_math(w_ref[0], s, m_ref[0], v_ref[0])):
                o[0] = val
            return
        for c0, wd in lane_chunks:
            chunk = s[c0:c0 + wd]
            if wd < 128:
                chunk = jnp.concatenate([chunk, jnp.zeros((128 - wd, tc), F32)], axis=0)
            g = chunk.T[:, :wd]
            sl = (0, slice(None), slice(c0, c0 + wd))
            for o, val in zip(outs, (g,) + _adam_math(w_ref[sl], g, m_ref[sl], v_ref[sl])):
                o[sl] = val

    if transposed:
        nat = _bs((1, tc, rows), lambda j, me_ref: (layer, j, 0))
    else:
        nat = _bs((1, rows, tc), lambda j, me_ref: (layer, 0, j))
    in_specs = [_bs((NDEV, 1, rows, tc), lambda j, me_ref: (0, piece, 0, j)),
                _bs((1, 1, rows, tc), lambda j, me_ref: (piece, me_ref[0], 0, j)), nat, nat, nat]
    args = [me, landed, own, w, m, v]
    alias = {}
    if prev is not None:
        in_specs += [pl.BlockSpec(memory_space=pl.ANY)] * 4
        args += list(prev)
        alias = {6 + i: i for i in range(4)}
    grid_spec = pltpu.PrefetchScalarGridSpec(num_scalar_prefetch=1, grid=(cols // tc,), in_specs=in_specs, out_specs=[nat] * 4)
    return pl.pallas_call(body, name=name, grid_spec=grid_spec, out_shape=[_sds(w.shape, F32)] * 4,
                          input_output_aliases=alias, interpret=False)(*args)


SMALL_LAYER = ("ffn1_norm", "mix_norm", "b_gate", "b_ln_g", "b_ln_b", "b_w_s", "b_b_s", "c_w", "c_scale", "ffn2_norm")
SMALL_ROWS = {"ffn1_norm": 8, "mix_norm": 8, "b_gate": 24, "b_ln_g": 4, "b_ln_b": 4, "b_w_s": 512, "b_b_s": 4, "c_w": 512,
              "c_scale": 4, "ffn2_norm": 8, "final_norm": 8}
LAYER_ROWS = sum(SMALL_ROWS[k] for k in SMALL_LAYER)


def _small_offset(name):
    if name == "final_norm":
        return LAYER_ROWS
    return sum(SMALL_ROWS[k] for k in SMALL_LAYER[:SMALL_LAYER.index(name)])


def _pack_small_grads(layer_grads, final, me, name):
    flat, where = [], []
    for k in SMALL_LAYER:
        parts = layer_grads[k] if k == "b_gate" else [layer_grads[k]]
        for i, part in enumerate(parts):
            flat.append(part)
            where.append(_small_offset(k) + i * (D // 128))
    if final is not None:
        flat.append(final)
        where.append(_small_offset("final_norm"))
    pack_rows = LAYER_ROWS + (SMALL_ROWS["final_norm"] if final is not None else 0)

    def body(me_ref, *refs):
        del me_ref
        o_ref = refs[-1]
        for ref, off in zip(refs[:-1], where):
            if len(ref.shape) == 2:
                for j in range(ref.shape[1] // 128):
                    o_ref[0, 0, off + j:off + j + 1, :] = ref[:, 128 * j:128 * (j + 1)]
            else:
                rows = ref.shape[1]
                for g in range(4):
                    o_ref[0, 0, off + g * rows:off + (g + 1) * rows, :] = ref[g]

    grid_spec = pltpu.PrefetchScalarGridSpec(
        num_scalar_prefetch=1, grid=(1,),
        in_specs=[pl.BlockSpec(a.shape, (lambda i, me_ref, nd=a.ndim: (0,) * nd)) for a in flat],
        out_specs=_bs((1, 1, pack_rows, 128), lambda i, me_ref: (0, me_ref[0], 0, 0)))
    return pl.pallas_call(body, name=name, grid_spec=grid_spec, out_shape=_sds((1, NDEV, pack_rows, 128), F32),
                          interpret=False)(me, *flat)


def _adamw_small(gathered, w, mom, var, name):
    names = SMALL_LAYER + ("final_norm",)
    ins = list(gathered) + [t[k] for k in names for t in (w, mom, var)]

    def body(*refs):
        outs = refs[len(ins):]

        def rows(l, off, n):
            land = refs[l]
            s = land[0, 0, off:off + n, :]
            for d in range(1, NDEV):
                s = s + land[0, d, off:off + n, :]
            return s

        def as_row(l, off, n):
            s = rows(l, off, n)
            return jnp.concatenate([s[j:j + 1, :] for j in range(n)], axis=1)

        def update(i, idx, g):
            wr, mr, vr = (refs[DEPTH + 3 * i + t] for t in range(3))
            d, m2, v2 = _adam_math(wr[idx], g, mr[idx], vr[idx])
            for o, val in zip(outs[4 * i:4 * i + 4], (g, d, m2, v2)):
                o[idx] = val

        for i, k in enumerate(names):
            n = SMALL_ROWS[k]
            off = _small_offset(k)
            if k == "final_norm":
                update(i, (slice(0, 1),), as_row(DEPTH - 1, off, n))
                continue
            for l in range(DEPTH):
                if k in ("b_w_s", "c_w"):
                    for g in range(4):
                        update(i, (l, g), rows(l, off + g * BLK, BLK))
                elif k in ("b_b_s", "c_scale"):
                    update(i, (l,), rows(l, off, n))
                else:
                    update(i, (slice(l, l + 1),), as_row(l, off, n))

    out_shape = [_sds(w[k].shape, F32) for k in names for _ in range(4)]
    outs = pl.pallas_call(body, name=name, out_shape=out_shape,
                          compiler_params=pltpu.CompilerParams(vmem_limit_bytes=48 << 20), interpret=False)(*ins)
    return tuple({k: outs[4 * i + t] for i, k in enumerate(names)} for t in range(4))


PG, PD = 0, 2


def _ffn_fwd(x, h, wf, tag, mid, next_gain):
    g_pre, u_pre, act = _gu_swiglu(h, wf, tm=2048, tn=256, b_row0=PG * FF, name=f"gu_{tag}")
    token = mid(act)
    out = _mm(act, wf, mode="nn", tm=512, tn=D, tk=FF, n=D, b_row0=PD * FF, out_dtype=F32, scale=0.5, res=x, after=token,
              norm_gain=next_gain, name=f"down_{tag}")
    out, h_next = out if next_gain is not None else (out, None)
    return out, h_next, (x, h, g_pre, u_pre, act)


def _ffn_bwd(dout, saved, gain, wf, tag, after, emit):
    x, h, g_pre, u_pre, act = saved
    grf = lax.empty((3, NDEV, FFS, D), BF16)
    dgu = _dact_dswiglu(dout, wf, g_pre, u_pre, tm=256, b_row0=PD * FF, after=after, name=f"dgu_{tag}")
    grf = _wgrad(act, dout, grf, piece0=PD, rows=FFS, tm=FF // 2, tn=D, tk=1024, scale=0.5, name=f"wg_down_{tag}")
    grf = _wgrad(dgu, h, grf, piece0=PG, rows=FFS, tm=FF // 2, tn=D, tk=2048, name=f"wg_gu_{tag}")
    token = emit([grf])
    dx, dgain = _mm_drms(dgu, wf, x, gain, dout, tm=256, b_row0=PG * FF, after=token, name=f"dx_{tag}")
    return dx, dgain, token


def _mixer_fwd(x, h, sp, ww, wo, wa, wbc, tag, mid, next_gain):
    zq = _mm(h, ww, mode="nt", tm=1024, tn=1152, tk=D, n=QKV_W, out_dtype=F32, name=f"zq_{tag}")
    z = _mm(h, ww, mode="nt", tm=1024, tn=1152, tk=D, n=INW - QKV_W, b_row0=QKV_W, out_dtype=BF16, name=f"z_{tag}")
    token = mid(z)
    acc, ml = _attn_fwd(zq, None, None, gi=0, first=True, final=False, name=f"attn0_{tag}")
    acc, ml = _attn_fwd(zq, acc, ml, gi=1, first=False, final=False, name=f"attn1_{tag}")
    ya, lse = _attn_fwd(zq, acc, ml, gi=2, first=False, final=True, name=f"attn2_{tag}")
    yb = _gmlp_fwd(z, sp["b_ln_g"], sp["b_ln_b"], sp["b_w_s"], sp["b_b_s"], f"gmlp_{tag}")
    yc = _pool_fwd(z, sp["c_w"], sp["c_scale"], f"pool_{tag}")
    pa = _mm(ya, wa, mode="nt", tm=1024, tn=D, tk=GW, n=D, out_dtype=BF16, after=token, name=f"proj_a_{tag}")
    pb = _mm(yb, wbc, mode="nt", tm=1024, tn=D, tk=BW, n=D, b_row0=0, out_dtype=BF16, name=f"proj_b_{tag}")
    pc = _mm(yc, wbc, mode="nt", tm=1024, tn=D, tk=BW, n=D, b_row0=D, out_dtype=BF16, name=f"proj_c_{tag}")
    merged = _gates_fwd(z, sp["b_gate"], pa, pb, pc, f"gates_{tag}")
    out, h_next = _mm(merged, wo, mode="nn", tm=1024, tn=D, tk=D, n=D, out_dtype=F32, res=x, norm_gain=next_gain,
                      name=f"out_{tag}")
    return out, h_next, (x, h, zq, z, ya, lse, yb, yc, pa, pb, pc, merged)


def _mixer_bwd(dout, saved, sp, ww, wo, wa, wbc, tag, after, emit):
    x, h, zq, z, ya, lse, yb, yc, pa, pb, pc, merged = saved
    grw = lax.empty((1, NDEV, INS, D), BF16)
    gro = lax.empty((1, NDEV, DS, D), BF16)
    gra = lax.empty((1, NDEV, DS, GW), BF16)
    grbc = lax.empty((2, NDEV, DS, BW), BF16)
    dm = _mm(dout, wo, mode="nt", tm=1024, tn=D, tk=D, n=D, out_dtype=F32, after=after, name=f"dmerged_{tag}")
    gro = _wgrad(merged, dout, gro, piece0=0, rows=DS, tm=D, tn=D, tk=1024, name=f"wg_out_{tag}")
    dpa, dpb, dpc, dza, dzb, dzc, dba, dbb, dbc = _gates_bwd(z, sp["b_gate"], pa, pb, pc, dm, f"dgates_{tag}")
    dya = _mm(dpa, wa, mode="nn", tm=2048, tn=GW, tk=D, n=GW, out_dtype=F32, name=f"dya_{tag}")
    dyb = _mm(dpb, wbc, mode="nn", tm=2048, tn=BW, tk=D, n=BW, b_row0=0, out_dtype=F32, name=f"dyb_{tag}")
    dyc = _mm(dpc, wbc, mode="nn", tm=2048, tn=BW, tk=D, n=BW, b_row0=D, out_dtype=F32, name=f"dyc_{tag}")
    gra = _wgrad(dpa, ya, gra, piece0=0, rows=DS, tm=D, tn=GW, tk=1024, name=f"wg_pa_{tag}")
    grbc = _wgrad(dpb, yb, grbc, piece0=0, rows=DS, tm=D, tn=BW, tk=1024, name=f"wg_pb_{tag}")
    grbc = _wgrad(dpc, yc, grbc, piece0=1, rows=DS, tm=D, tn=BW, tk=1024, name=f"wg_pc_{tag}")
    dqkv = [_attn_bwd(zq, dya, ya, lse, gi=gi, name=f"dattn{gi}_{tag}") for gi in range(3)]
    duv, dws, dbs, dlng, dlnb = _gmlp_bwd(z, dyb, sp["b_ln_g"], sp["b_ln_b"], sp["b_w_s"], sp["b_b_s"], f"dgmlp_{tag}")
    dxc, dcw, dcs = _pool_bwd(z, dyc, sp["c_w"], sp["c_scale"], f"dpool_{tag}")
    dz = _concat_bf16([dqkv[gi][part] for part in range(3) for gi in range(3)] + [duv, dxc, dza, dzb, dzc], f"dz_{tag}")
    grw = _wgrad(dz, h, grw, piece0=0, rows=INS, tm=INW // 2, tn=512, tk=1024, name=f"wg_in_{tag}")
    token = emit([grw, gro, gra, grbc])
    dx, dgain = _mm_drms(dz, ww, x, sp["mix_norm"], dout, tm=256, b_row0=0, after=token, name=f"dx_mix_{tag}")
    small = {"mix_norm": dgain, "b_gate": [dba, dbb, dbc], "b_ln_g": dlng, "b_ln_b": dlnb,
             "b_w_s": dws, "b_b_s": dbs, "c_w": dcw, "c_scale": dcs}
    return dx, small, token


PARTS = ("f1", "mix", "f2")


def _forward(xs, tgt, weights_of, pass_on, small, final_gain):
    saved, views = [], []
    cur = xs
    h = _rms_fwd(xs, small[0]["ffn1_norm"], "rms_first")
    for l in range(DEPTH):
        last = l + 1 == DEPTH
        (wf1,) = weights_of(l, "f1", cur)
        cur, h, s1 = _ffn_fwd(cur, h, wf1, f"f1l{l}", lambda t: pass_on(l, "mix", t), small[l]["mix_norm"])
        wmix = weights_of(l, "mix", cur)
        cur, h, s2 = _mixer_fwd(cur, h, small[l], *wmix, f"l{l}", lambda t: pass_on(l, "f2", t), small[l]["ffn2_norm"])
        (wf2,) = weights_of(l, "f2", cur)
        cur, h, s3 = _ffn_fwd(cur, h, wf2, f"f2l{l}", lambda t: None if last else pass_on(l + 1, "f1", t),
                              None if last else small[l + 1]["ffn1_norm"])
        saved.append((s1, s2, s3))
        views.append((wf1, wmix, wf2))
    loss_part, dcur, dfinal = _final_loss(cur, final_gain, tgt, "final_loss")
    return loss_part, dcur, dfinal, saved, views


def _backward_layer(dcur, saved_l, small_l, views_l, l, emit, after):
    wf1, wmix, wf2 = views_l
    s1, s2, s3 = saved_l
    dcur, dn2, token = _ffn_bwd(dcur, s3, small_l["ffn2_norm"], wf2, f"f2l{l}", after, lambda g: emit("f2", g))
    dcur, sg, token = _mixer_bwd(dcur, s2, small_l, *wmix, f"l{l}", token, lambda g: emit("mix", g))
    dcur, dn1, token = _ffn_bwd(dcur, s1, small_l["ffn1_norm"], wf1, f"f1l{l}", token, lambda g: emit("f1", g))
    sg["ffn1_norm"] = dn1
    sg["ffn2_norm"] = dn2
    return dcur, sg, token


BIG = ("ffn1_w_gate", "ffn1_w_up", "ffn1_w_down", "w_in", "w_proj_a", "w_proj_b", "w_proj_c", "w_out",
       "ffn2_w_gate", "ffn2_w_up", "ffn2_w_down")
ALL_WEIGHTS = ("ffn1_norm", "ffn1_w_gate", "ffn1_w_up", "ffn1_w_down", "mix_norm", "w_in", "b_gate", "b_ln_g", "b_ln_b",
               "b_w_s", "b_b_s", "c_w", "c_scale", "w_proj_a", "w_proj_b", "w_proj_c", "w_out", "ffn2_norm", "ffn2_w_gate",
               "ffn2_w_up", "ffn2_w_down", "final_norm")


def _place_shards(w, l, part, me, after, name):
    tr = lambda a: a.transpose(0, 2, 1)
    if part == "mix":
        srcs = [tr(w["w_in"]), w["w_out"], tr(w["w_proj_a"]), tr(w["w_proj_b"]), tr(w["w_proj_c"])]
        groups = [[0], [1], [2], [3, 4]]
    else:
        f = "ffn1" if part == "f1" else "ffn2"
        srcs = [tr(w[f + "_w_gate"]), tr(w[f + "_w_up"]), w[f + "_w_down"]]
        groups = [[0, 1, 2]]

    def body(me_ref, *refs):
        del me_ref
        outs = refs[len(srcs) + 1:]
        for o_ref, members in zip(outs, groups):
            for p, i in enumerate(members):
                o_ref[p, 0] = refs[i][0].astype(BF16)

    out_shape = [_sds((len(g), NDEV) + srcs[g[0]].shape[1:], BF16) for g in groups]
    grid_spec = pltpu.PrefetchScalarGridSpec(
        num_scalar_prefetch=1, grid=(1,),
        in_specs=[_bs((1,) + s.shape[1:], lambda i, me_ref: (l, 0, 0)) for s in srcs] + [pl.BlockSpec(memory_space=pl.ANY)],
        out_specs=[_bs((len(g), 1) + srcs[g[0]].shape[1:], lambda i, me_ref: (0, me_ref[0], 0, 0)) for g in groups])
    return pl.pallas_call(body, name=name, grid_spec=grid_spec, out_shape=out_shape,
                          compiler_params=pltpu.CompilerParams(vmem_limit_bytes=40 << 20), interpret=False)(me, *srcs, after)


def _part_views(part, gathered):
    if part == "mix":
        pw, po, pa, pbc = gathered
        return [pw.reshape(INW, D), po.reshape(D, D), pa.reshape(D, GW), pbc.reshape(2 * D, BW)]
    return [gathered[0].reshape(3 * FF, D)]


def _part_pieces(part):
    if part == "mix":
        return [(0, 0, "w_in", True), (1, 0, "w_out", False), (2, 0, "w_proj_a", True), (3, 0, "w_proj_b", True),
                (3, 1, "w_proj_c", True)]
    f = "ffn1" if part == "f1" else "ffn2"
    return [(0, 0, f + "_w_gate", True), (0, 1, f + "_w_up", True), (0, 2, f + "_w_down", False)]


def _layer_small(w, l):
    r = lambda a, shape: a.reshape(shape)
    return {"ffn1_norm": r(w["ffn1_norm"][l], (1, D)), "mix_norm": r(w["mix_norm"][l], (1, D)),
            "b_gate": r(w["b_gate"][l], (1, 3 * D)), "b_ln_g": r(w["b_ln_g"][l], (1, BW)), "b_ln_b": r(w["b_ln_b"][l], (1, BW)),
            "b_w_s": w["b_w_s"][l], "b_b_s": r(w["b_b_s"][l], (4, BLK, 1)), "c_w": w["c_w"][l],
            "c_scale": r(w["c_scale"][l], (4, 1, BLK)), "ffn2_norm": r(w["ffn2_norm"][l], (1, D))}


def kernel(x, ffn1_norm, ffn1_w_gate, ffn1_w_up, ffn1_w_down, mix_norm, w_in, b_gate, b_ln_g, b_ln_b, b_w_s, b_b_s, c_w, c_scale, w_proj_a, w_proj_b, w_proj_c, w_out, ffn2_norm, ffn2_w_gate, ffn2_w_up, ffn2_w_down, final_norm, loss_target, m_ffn1_norm, m_ffn1_w_gate, m_ffn1_w_up, m_ffn1_w_down, m_mix_norm, m_w_in, m_b_gate, m_b_ln_g, m_b_ln_b, m_b_w_s, m_b_b_s, m_c_w, m_c_scale, m_w_proj_a, m_w_proj_b, m_w_proj_c, m_w_out, m_ffn2_norm, m_ffn2_w_gate, m_ffn2_w_up, m_ffn2_w_down, m_final_norm, v_ffn1_norm, v_ffn1_w_gate, v_ffn1_w_up, v_ffn1_w_down, v_mix_norm, v_w_in, v_b_gate, v_b_ln_g, v_b_ln_b, v_b_w_s, v_b_b_s, v_c_w, v_c_scale, v_w_proj_a, v_w_proj_b, v_w_proj_c, v_w_out, v_ffn2_norm, v_ffn2_w_gate, v_ffn2_w_up, v_ffn2_w_down, v_final_norm):
    w = dict(ffn1_norm=ffn1_norm, ffn1_w_gate=ffn1_w_gate, ffn1_w_up=ffn1_w_up, ffn1_w_down=ffn1_w_down, mix_norm=mix_norm,
             w_in=w_in, b_gate=b_gate, b_ln_g=b_ln_g, b_ln_b=b_ln_b, b_w_s=b_w_s, b_b_s=b_b_s, c_w=c_w, c_scale=c_scale,
             w_proj_a=w_proj_a, w_proj_b=w_proj_b, w_proj_c=w_proj_c, w_out=w_out, ffn2_norm=ffn2_norm, ffn2_w_gate=ffn2_w_gate,
             ffn2_w_up=ffn2_w_up, ffn2_w_down=ffn2_w_down, final_norm=final_norm)
    mom = dict(ffn1_norm=m_ffn1_norm, ffn1_w_gate=m_ffn1_w_gate, ffn1_w_up=m_ffn1_w_up, ffn1_w_down=m_ffn1_w_down,
               mix_norm=m_mix_norm, w_in=m_w_in, b_gate=m_b_gate, b_ln_g=m_b_ln_g, b_ln_b=m_b_ln_b, b_w_s=m_b_w_s, b_b_s=m_b_b_s,
               c_w=m_c_w, c_scale=m_c_scale, w_proj_a=m_w_proj_a, w_proj_b=m_w_proj_b, w_proj_c=m_w_proj_c, w_out=m_w_out,
               ffn2_norm=m_ffn2_norm, ffn2_w_gate=m_ffn2_w_gate, ffn2_w_up=m_ffn2_w_up, ffn2_w_down=m_ffn2_w_down,
               final_norm=m_final_norm)
    var = dict(ffn1_norm=v_ffn1_norm, ffn1_w_gate=v_ffn1_w_gate, ffn1_w_up=v_ffn1_w_up, ffn1_w_down=v_ffn1_w_down,
               mix_norm=v_mix_norm, w_in=v_w_in, b_gate=v_b_gate, b_ln_g=v_b_ln_g, b_ln_b=v_b_ln_b, b_w_s=v_b_w_s, b_b_s=v_b_b_s,
               c_w=v_c_w, c_scale=v_c_scale, w_proj_a=v_w_proj_a, w_proj_b=v_w_proj_b, w_proj_c=v_w_proj_c, w_out=v_w_out,
               ffn2_norm=v_ffn2_norm, ffn2_w_gate=v_ffn2_w_gate, ffn2_w_up=v_ffn2_w_up, ffn2_w_down=v_ffn2_w_down,
               final_norm=v_final_norm)

    t_dim = x.shape[1]
    xs = x.reshape(t_dim, D)
    tgt = loss_target.reshape(t_dim, D)

    me = 4 * lax.axis_index("x") + 2 * lax.axis_index("y") + lax.axis_index("c")
    me_arr = me.astype(jnp.int32).reshape(1)
    gathers = {}
    token = jnp.zeros((8, 128), F32)
    for l in range(DEPTH):
        for part in PARTS:
            lands = _place_shards(w, l, part, me_arr, token, f"place_{part}_l{l}")
            gathers[l, part] = _gather_start(lands, token, f"gather_start_{part}_l{l}")
            token = gathers[l, part][-1]

    def pass_on(l, part, after):
        gathers[l, part] = _gather_forward(gathers[l, part], after, f"gather_forward_{part}_l{l}")
        return gathers[l, part][-1]

    pass_on(0, "f1", token)

    def weights_of(l, part, after):
        return _part_views(part, _gather_wait(gathers[l, part], after, f"gather_wait_{part}_l{l}"))

    small = [_layer_small(w, l) for l in range(DEPTH)]
    loss_part, dcur, dfinal, saved, views = _forward(xs, tgt, weights_of, pass_on, small, w["final_norm"].reshape(1, D))
    loss = lax.psum(loss_part[0, 0], ("x", "y", "c"))

    scatters = []
    small_gathers = [None] * DEPTH
    token = None
    for l in reversed(range(DEPTH)):
        def emit(part, grads, l=l):
            scatters.append((l, part, _exchange_start("scatter", grads, dfinal, f"scatter_start_{part}_l{l}")))
            return scatters[-1][2][-1]
        dcur, small_grads, token = _backward_layer(dcur, saved[l], small[l], views[l], l, emit, token)
        packed = _pack_small_grads(small_grads, dfinal if l == DEPTH - 1 else None, me_arr, f"pack_small_grads_l{l}")
        small_gathers[l] = _exchange_start("gather", [packed], token, f"gather_start_small_l{l}")
        token = small_gathers[l][-1]
    grad_x = dcur.reshape(x.shape)

    done = {}
    behind = token
    for l, part, started in scatters:
        arrs = _exchange_wait("scatter", started, behind, f"scatter_wait_{part}_l{l}")
        na = len(arrs) // 2
        for a, piece, k, transposed in _part_pieces(part):
            flip = transposed and w[k].shape[2] % 128 != 0
            view = (lambda t: t.transpose(0, 2, 1)) if flip else (lambda t: t)
            done[k] = _reduce_update(arrs[na + a], arrs[a], me_arr, piece, view(w[k]), view(mom[k]), view(var[k]), l,
                                     done.get(k), transposed and not flip, f"update_{k}_l{l}")
            behind = done[k][1]
    unflip = lambda k, t: t.transpose(0, 2, 1) if t.shape != w[k].shape else t
    grads, delta, new_m, new_v = ({k: unflip(k, done[k][t]) for k in BIG} for t in range(4))
    gathered = [None] * DEPTH
    for l in reversed(range(DEPTH)):
        gathered[l] = _exchange_wait("gather", small_gathers[l], behind, f"gather_wait_small_l{l}")[0]
        behind = gathered[l]
    as_row = lambda t: dict(t, final_norm=t["final_norm"].reshape(1, D))
    for dst, src in zip((grads, delta, new_m, new_v), _adamw_small(gathered, as_row(w), as_row(mom), as_row(var), "adamw_small")):
        dst.update(src)
        dst["final_norm"] = src["final_norm"].reshape(D)

    return (loss, grad_x, *[grads[k] for k in ALL_WEIGHTS], *[delta[k] for k in ALL_WEIGHTS],
            *[new_m[k] for k in ALL_WEIGHTS], *[new_v[k] for k in ALL_WEIGHTS])
```

```python
import math

import jax
import jax.numpy as jnp
from jax import lax
from jax.experimental import pallas as pl
from jax.experimental.pallas import tpu as pltpu

F32 = jnp.float32
BF16 = jnp.bfloat16
MESH = pl.DeviceIdType.MESH

NDEV = 8
DEPTH = 2
D = 1024
FF = 2816
FFS = FF // NDEV
INW = 6912
INS = INW // NDEV
DS = D // NDEV
BLK = 128
NH = 4
HD = 64
GW = NH * HD
DILS = (1, 4, 16)
QK_SCALE = 1.0 / math.sqrt(HD)
QKV_W = 3 * len(DILS) * GW
Q_B, K_B, V_B = 0, 3, 6
UV_B = 0
C_B128 = 8
G_B = 6
BW = 512
EPS = 1e-6
NEG = -1e30
VMEM_CAP = 60 * 1024 * 1024

ADAM_LR, ADAM_B1, ADAM_B2, ADAM_EPS, ADAM_WD, ADAM_STEP = 0.001, 0.9, 0.999, 1e-08, 0.01, 10

NT = (((1,), (1,)), ((), ()))
TN = (((0,), (0,)), ((), ()))


def _bs(shape, imap):
    return pl.BlockSpec(shape, imap)


def _pc(body, *, name, grid, in_specs, out_specs, out_shape, scratch=(), alias=None, vmem=None):
    params = {}
    if vmem is not None:
        params["vmem_limit_bytes"] = min(int(vmem), VMEM_CAP)
    big = lambda s: math.prod(s.shape) * jnp.dtype(s.dtype).itemsize >= (1 << 20)
    pin = lambda s: pltpu.HBM(s.shape, s.dtype) if big(s) else s
    out_shape = [pin(s) for s in out_shape] if isinstance(out_shape, (list, tuple)) else pin(out_shape)
    call = pl.pallas_call(
        body, name=name, grid=grid, in_specs=in_specs, out_specs=out_specs, out_shape=out_shape,
        scratch_shapes=list(scratch), input_output_aliases=alias or {},
        compiler_params=pltpu.CompilerParams(**params), interpret=False)
    return lambda *args: call(*[pltpu.with_memory_space_constraint(a, pltpu.HBM) if big(a) else a for a in args])


def _sds(shape, dtype):
    return jax.ShapeDtypeStruct(shape, dtype)


def _mm(a, b, *, mode, tm, tn, tk, n, out_dtype, name, b_row0=0, scale=None, res=None, after=None, norm_gain=None):
    m, k_dim = a.shape
    nk = k_dim // tk
    assert m % tm == 0 and n % tn == 0 and k_dim % tk == 0 and (norm_gain is None or tn == n)
    b_off, rem = divmod(b_row0, tk if mode == "nn" else tn)
    assert rem == 0
    grid = (m // tm, n // tn, nk)
    in_specs = [_bs((tm, tk), lambda i, j, k: (i, k))]
    if mode == "nn":
        in_specs.append(_bs((tk, tn), lambda i, j, k: (b_off + k, j)))
    else:
        in_specs.append(_bs((tn, tk), lambda i, j, k: (b_off + j, k)))
    args = [a, b]
    if res is not None:
        in_specs.append(_bs((tm, tn), lambda i, j, k: (i, j)))
        args.append(res)
    if norm_gain is not None:
        in_specs.append(_bs((1, tn), lambda i, j, k: (0, 0)))
        args.append(norm_gain)
    if after is not None:
        in_specs.append(pl.BlockSpec(memory_space=pl.ANY))
        args.append(after)
    n_in = len(args)

    def body(*refs):
        a_ref, b_ref = refs[0], refs[1]
        r_ref = refs[2] if res is not None else None
        g_ref = refs[2 + (res is not None)] if norm_gain is not None else None
        o_ref = refs[n_in]
        av = a_ref[...].astype(BF16)
        bv = b_ref[...].astype(BF16)
        if mode == "nn":
            p = jnp.dot(av, bv, preferred_element_type=F32)
        else:
            p = lax.dot_general(av, bv, NT, preferred_element_type=F32)

        def fin(v):
            if scale is not None:
                v = v * scale
            if r_ref is not None:
                v = r_ref[...] + v
            o_ref[...] = v.astype(out_dtype)
            if g_ref is not None:
                r = lax.rsqrt(jnp.mean(v * v, axis=-1, keepdims=True) + EPS)
                refs[n_in + 1][...] = (v * r * g_ref[...]).astype(BF16)

        if nk == 1:
            fin(p)
        else:
            acc_ref = refs[-1]
            kk = pl.program_id(2)

            @pl.when(kk == 0)
            def _():
                acc_ref[...] = p

            @pl.when(kk > 0)
            def _():
                acc_ref[...] += p

            @pl.when(kk == nk - 1)
            def _():
                fin(acc_ref[...])

    scratch = [pltpu.VMEM((tm, tn), F32)] if nk > 1 else []
    est = 2 * (tm * tk * a.dtype.itemsize + tk * tn * b.dtype.itemsize + tm * tn * jnp.dtype(out_dtype).itemsize)
    est += (2 + (nk > 1)) * tm * tn * 4 + (2 * tm * tn * 4 if res is not None else 0)
    est += (tm * tk * 2 if a.dtype != BF16 else 0) + (tk * tn * 2 if b.dtype != BF16 else 0)
    tile = _bs((tm, tn), lambda i, j, k: (i, j))
    if norm_gain is None:
        return _pc(body, name=name, grid=grid, in_specs=in_specs, out_specs=tile, out_shape=_sds((m, n), out_dtype),
                   scratch=scratch, vmem=est + (8 << 20))(*args)
    return _pc(body, name=name, grid=grid, in_specs=in_specs, out_specs=[tile, tile],
               out_shape=[_sds((m, n), out_dtype), _sds((m, n), BF16)], scratch=scratch,
               vmem=est + 3 * tm * tn * 4 + (8 << 20))(*args)


def _wgrad(a, b, buf, *, piece0, rows, tm, tn, tk, name, scale=None):
    t_dim, m = a.shape
    n = b.shape[1]
    db = tm // rows
    tpp = NDEV // db
    nk = t_dim // tk
    assert tm % rows == 0 and NDEV % db == 0 and m % tm == 0 and n % tn == 0 and t_dim % tk == 0
    grid = (m // tm, n // tn, nk)

    def body(a_ref, b_ref, buf_ref, o_ref, acc_ref):
        del buf_ref
        kk = pl.program_id(2)
        p = lax.dot_general(a_ref[...].astype(BF16), b_ref[...].astype(BF16), TN, preferred_element_type=F32)

        @pl.when(kk == 0)
        def _():
            acc_ref[...] = p

        @pl.when(kk > 0)
        def _():
            acc_ref[...] += p

        @pl.when(kk == nk - 1)
        def _():
            for d in range(db):
                v = acc_ref[d * rows:(d + 1) * rows, :]
                if scale is not None:
                    v = v * scale
                o_ref[0, d] = v.astype(BF16)

    est = 2 * (tk * tm * a.dtype.itemsize + tk * tn * b.dtype.itemsize + tm * tn * 2) + 3 * tm * tn * 4 + tk * tm * 4
    return _pc(
        body, name=name, grid=grid,
        in_specs=[_bs((tk, tm), lambda i, j, k: (k, i)), _bs((tk, tn), lambda i, j, k: (k, j)),
                  pl.BlockSpec(memory_space=pl.ANY)],
        out_specs=_bs((1, db, rows, tn), lambda i, j, k: (piece0 + i // tpp, i % tpp, 0, j)),
        out_shape=_sds(buf.shape, buf.dtype), scratch=[pltpu.VMEM((tm, tn), F32)], alias={2: 0},
        vmem=est + (8 << 20))(a, b, buf)


def _mm_drms(a, b, x, gain, dres, *, tm, b_row0, after, name):
    m, k_dim = a.shape
    assert m % tm == 0 and b_row0 % k_dim == 0 and b.shape[1] == D
    k_blk = b_row0 // k_dim
    after = jnp.zeros((8, 128), F32) if after is None else after

    def body(a_ref, b_ref, x_ref, g_ref, dr_ref, after_ref, dx_ref, dg_ref):
        del after_ref
        dh_v = jnp.dot(a_ref[...].astype(BF16), b_ref[...].astype(BF16), preferred_element_type=F32)
        xv = x_ref[...]
        r = lax.rsqrt(jnp.mean(xv * xv, axis=-1, keepdims=True) + EPS)
        dyg = dh_v * g_ref[...]
        mq = jnp.mean(dyg * xv, axis=-1, keepdims=True)
        dx_ref[...] = dr_ref[...] + r * (dyg - xv * (r * r * mq))

        @pl.when(pl.program_id(0) == 0)
        def _():
            dg_ref[...] = jnp.zeros_like(dg_ref)

        dg_ref[...] += jnp.sum(dh_v * (xv * r), axis=0, keepdims=True)

    row = _bs((tm, D), lambda i: (i, 0))
    est = 2 * (tm * k_dim * a.dtype.itemsize + k_dim * D * b.dtype.itemsize + 3 * tm * D * 4) + 4 * tm * D * 4
    return _pc(body, name=name, grid=(m // tm,),
               in_specs=[_bs((tm, k_dim), lambda i: (i, 0)), _bs((k_dim, D), lambda i: (k_blk, 0)), row,
                         _bs((1, D), lambda i: (0, 0)), row, pl.BlockSpec(memory_space=pl.ANY)],
               out_specs=[row, _bs((1, D), lambda i: (0, 0))], out_shape=[_sds((m, D), F32), _sds((1, D), F32)],
               vmem=est + (8 << 20))(a, b, x, gain, dres, after)


def _dact_dswiglu(dout, b, g_pre, u_pre, *, tm, b_row0, after, name):
    t_dim = dout.shape[0]
    assert t_dim % tm == 0 and b_row0 % FF == 0
    k_blk = b_row0 // FF
    after = jnp.zeros((8, 128), F32) if after is None else after

    def body(d_ref, b_ref, g_ref, u_ref, after_ref, o_ref):
        del after_ref
        dav = 0.5 * lax.dot_general(d_ref[...].astype(BF16), b_ref[...].astype(BF16), NT, preferred_element_type=F32)
        g = g_ref[...].astype(F32)
        u = u_ref[...].astype(F32)
        s = _sigmoid(g)
        o_ref[:, :FF] = (dav * u * (s * (1.0 + g * (1.0 - s)))).astype(BF16)
        o_ref[:, FF:] = (dav * (g * s)).astype(BF16)

    est = 2 * (tm * D * 4 + FF * D * 2 + 2 * tm * 2 * FF * 2) + 5 * tm * FF * 4
    return _pc(body, name=name, grid=(t_dim // tm,),
               in_specs=[_bs((tm, D), lambda i: (i, 0)), _bs((FF, D), lambda i: (k_blk, 0)),
                         _bs((tm, FF), lambda i: (i, 0)), _bs((tm, FF), lambda i: (i, 0)), pl.BlockSpec(memory_space=pl.ANY)],
               out_specs=_bs((tm, 2 * FF), lambda i: (i, 0)), out_shape=_sds((t_dim, 2 * FF), BF16),
               vmem=est + (8 << 20))(dout, b, g_pre, u_pre, after)


def _rms_fwd(x, gain, name, tm=512):
    t_dim = x.shape[0]

    def body(x_ref, g_ref, o_ref):
        xv = x_ref[...]
        r = lax.rsqrt(jnp.mean(xv * xv, axis=-1, keepdims=True) + EPS)
        o_ref[...] = (xv * r * g_ref[...]).astype(BF16)

    return _pc(body, name=name, grid=(t_dim // tm,),
               in_specs=[_bs((tm, D), lambda i: (i, 0)), _bs((1, D), lambda i: (0, 0))],
               out_specs=_bs((tm, D), lambda i: (i, 0)), out_shape=_sds((t_dim, D), BF16))(x, gain)


def _sigmoid(v):
    return 1.0 / (1.0 + jnp.exp(-v))


def _gu_swiglu(h, b, *, tm, tn, b_row0, name):
    t_dim = h.shape[0]
    assert t_dim % tm == 0 and FF % tn == 0 and b_row0 % tn == 0
    j0 = b_row0 // tn

    def body(h_ref, bg_ref, bu_ref, g_ref, u_ref, a_ref):
        hv = h_ref[...]
        g16 = lax.dot_general(hv, bg_ref[...], NT, preferred_element_type=F32).astype(BF16)
        u16 = lax.dot_general(hv, bu_ref[...], NT, preferred_element_type=F32).astype(BF16)
        g = g16.astype(F32)
        g_ref[...] = g16
        u_ref[...] = u16
        a_ref[...] = (g * _sigmoid(g) * u16.astype(F32)).astype(BF16)

    tile = _bs((tm, tn), lambda i, j: (i, j))
    est = 2 * (tm * D * 2 + 2 * tn * D * 2 + 3 * tm * tn * 2) + 5 * tm * tn * 4
    return _pc(body, name=name, grid=(t_dim // tm, FF // tn),
               in_specs=[_bs((tm, D), lambda i, j: (i, 0)), _bs((tn, D), lambda i, j: (j0 + j, 0)),
                         _bs((tn, D), lambda i, j: (j0 + FF // tn + j, 0))],
               out_specs=[tile] * 3, out_shape=[_sds((t_dim, FF), BF16)] * 3, vmem=est + (8 << 20))(h, b, b)


def _final_loss(x, gain, tgt, name, tm=512):
    t_dim = x.shape[0]

    def body(x_ref, g_ref, t_ref, loss_ref, dx_ref, dg_ref):
        xv = x_ref[...]
        gv = g_ref[...]
        r = lax.rsqrt(jnp.mean(xv * xv, axis=-1, keepdims=True) + EPS)
        xn = xv * r
        err = xn * gv - t_ref[...]
        dy = err * (1.0 / D)
        dyg = dy * gv
        mq = jnp.mean(dyg * xv, axis=-1, keepdims=True)
        dx_ref[...] = r * (dyg - xv * (r * r * mq))

        @pl.when(pl.program_id(0) == 0)
        def _():
            dg_ref[...] = jnp.zeros_like(dg_ref)
            loss_ref[...] = jnp.zeros_like(loss_ref)

        dg_ref[...] += jnp.sum(dy * xn, axis=0, keepdims=True)
        part = 0.5 * jnp.sum(jnp.mean(err * err, axis=-1, keepdims=True), axis=0, keepdims=True)
        loss_ref[...] += jnp.broadcast_to(part, loss_ref.shape)

    return _pc(body, name=name, grid=(t_dim // tm,),
               in_specs=[_bs((tm, D), lambda i: (i, 0)), _bs((1, D), lambda i: (0, 0)), _bs((tm, D), lambda i: (i, 0))],
               out_specs=[_bs((8, 128), lambda i: (0, 0)), _bs((tm, D), lambda i: (i, 0)), _bs((1, D), lambda i: (0, 0))],
               out_shape=[_sds((8, 128), F32), _sds((t_dim, D), F32), _sds((1, D), F32)])(x, gain, tgt)


def _band_mask(n):
    ri = lax.broadcasted_iota(jnp.int32, (BLK, 2 * BLK), 0)
    cj = lax.broadcasted_iota(jnp.int32, (BLK, 2 * BLK), 1)
    dist = ri + BLK - cj
    return (dist >= 0) & (dist <= BLK) & ((cj >= BLK) | (n > 0))


def _halves(unit, imap_rows, col):
    return [_bs((unit, 128), (lambda u, j=j: (imap_rows(u), 2 * col + j))) for j in range(2)]


def _get_rows(pair, r, dil):
    return jnp.concatenate([h[pl.ds(r, BLK, stride=dil), :] for h in pair], axis=1)


def _put_rows(stage, r, dil, val):
    for j in range(2):
        stage[j, pl.ds(r, BLK, stride=dil), :] = val[:, 128 * j:128 * (j + 1)]


def _unstage(stage, out_ref):
    out_ref[:, :128] = stage[0]
    out_ref[:, 128:] = stage[1]


def _attn_fwd(z, acc, ml, *, gi, first, final, name):
    dil = DILS[gi]
    t_dim = z.shape[0]
    nsub = 4 if dil == 1 else 1
    unit = BLK * dil * nsub
    nu = t_dim // unit
    own = lambda u: u
    prev = lambda u: jnp.maximum(u - 1, 0)
    in_specs = (_halves(unit, own, Q_B + gi) + _halves(unit, own, K_B + gi) + _halves(unit, prev, K_B + gi)
                + _halves(unit, own, V_B + gi) + _halves(unit, prev, V_B + gi))
    args = [z] * 10
    st_ml = _bs((unit, 128), lambda u: (u, 0))
    if not first:
        in_specs += _halves(unit, own, 0) + [st_ml]
        args += [acc, acc, ml]

    def body(*refs):
        q2, ko2, kp2, vo2, vp2 = (refs[2 * i:2 * i + 2] for i in range(5))
        o_acc_ref, o_ml_ref, stage = refs[-3], refs[-2], refs[-1]
        lane = lax.broadcasted_iota(jnp.int32, (BLK, GW), 1)
        lane_s = lax.broadcasted_iota(jnp.int32, (BLK, 128), 1)

        def band(rows_at, prev_rows_at, store, valid):
            q = rows_at(q2) * QK_SCALE
            kb = jnp.concatenate([prev_rows_at(kp2, ko2), rows_at(ko2)], axis=0).astype(BF16)
            vb = jnp.concatenate([prev_rows_at(vp2, vo2), rows_at(vo2)], axis=0).astype(BF16)
            if first:
                acc_v = jnp.zeros((BLK, GW), F32)
                ml_v = jnp.where(lane_s < 64, NEG, 0.0).astype(F32)
            else:
                acc_v = rows_at(refs[10:12])
                ml_v = rows_at(refs[12:13])
            heads = [(lane >= h * HD) & (lane < (h + 1) * HD) for h in range(NH)]
            rows = lambda v, h: v[h * BLK:(h + 1) * BLK]
            qs = jnp.concatenate([jnp.where(hm, q, 0.0) for hm in heads], axis=0).astype(BF16)
            s = lax.dot_general(qs, kb, NT, preferred_element_type=F32)
            s = jnp.where(jnp.concatenate([valid] * NH, axis=0), s, NEG)
            m_old = jnp.concatenate([ml_v[:, 16 * h:16 * h + 1] for h in range(NH)], axis=0)
            l_old = jnp.concatenate([ml_v[:, 64 + 16 * h:64 + 16 * h + 1] for h in range(NH)], axis=0)
            m_new = jnp.maximum(m_old, jnp.max(s, axis=-1, keepdims=True))
            alpha = jnp.exp(m_old - m_new)
            p = jnp.exp(s - m_new)
            l_new = alpha * l_old + jnp.sum(p, axis=-1, keepdims=True)
            pv = jnp.dot(p.astype(BF16), vb, preferred_element_type=F32)
            stat = m_new + jnp.log(l_new) if final else m_new
            acc_o, ml_o = acc_v, ml_v
            l_b = jnp.ones((BLK, GW), F32)
            for h, hm in enumerate(heads):
                acc_o = jnp.where(hm, acc_v * rows(alpha, h) + rows(pv, h), acc_o)
                l_b = jnp.where(hm, rows(l_new, h), l_b)
                ml_o = jnp.where((lane_s >= 16 * h) & (lane_s < 16 * h + 16), rows(stat, h), ml_o)
                if not final:
                    ml_o = jnp.where((lane_s >= 64 + 16 * h) & (lane_s < 64 + 16 * h + 16), rows(l_new, h), ml_o)
            store(acc_o / l_b if final else acc_o, ml_o)

        u = pl.program_id(0)
        if dil == 1:
            def block(b):
                return lambda pair: jnp.concatenate([h[pl.ds(b * BLK, BLK), :] for h in pair], axis=1)

            for b in range(nsub):
                def store(acc_val, ml_val, b=b):
                    for j in range(2):
                        stage[j, pl.ds(b * BLK, BLK), :] = acc_val[:, 128 * j:128 * (j + 1)]
                    o_ml_ref[pl.ds(b * BLK, BLK), :] = ml_val

                before = (lambda prev_pair, own_pair, b=b: block(b - 1)(own_pair)) if b else (
                    lambda prev_pair, own_pair: block(nsub - 1)(prev_pair))
                band(block(b), before, store, _band_mask(1 if b else u))
        else:
            valid = _band_mask(u)

            def step(r, carry):
                def store(acc_val, ml_val):
                    _put_rows(stage, r, dil, acc_val)
                    o_ml_ref[pl.ds(r, BLK, stride=dil), :] = ml_val

                band(lambda pair: _get_rows(pair, r, dil), lambda prev_pair, own_pair: _get_rows(prev_pair, r, dil), store, valid)
                return carry

            lax.fori_loop(0, dil, step, 0, unroll=min(dil, 8))
        _unstage(stage, o_acc_ref)

    blocks = (len(in_specs) + 6) * unit * 128 * 4
    return _pc(body, name=name, grid=(nu,), in_specs=in_specs, out_specs=[_bs((unit, GW), lambda u: (u, 0)), st_ml],
               out_shape=[_sds((t_dim, GW), F32), _sds((t_dim, 128), F32)],
               scratch=[pltpu.VMEM((2, unit, 128), F32)], vmem=2 * blocks + (12 << 20))(*args)


def _attn_bwd(z, dy, y, lse, *, gi, name):
    dil = DILS[gi]
    t_dim = z.shape[0]
    nsub = 4 if dil == 1 else 1
    unit = BLK * dil * nsub
    nu = t_dim // unit

    own = lambda u: jnp.minimum(u, nu - 1)
    prev = lambda u: jnp.maximum(jnp.minimum(u, nu - 1) - 1, 0)
    in_specs = (_halves(unit, own, Q_B + gi) + _halves(unit, own, K_B + gi) + _halves(unit, prev, K_B + gi)
                + _halves(unit, own, V_B + gi) + _halves(unit, prev, V_B + gi)
                + _halves(unit, own, 0) + _halves(unit, own, 0) + [_bs((unit, 128), lambda u: (own(u), 0))])
    tok = _bs((unit, GW), lambda u: (own(u), 0))
    kv_out = _bs((unit, GW), lambda u: (jnp.maximum(u - 1, 0), 0))

    def body(*refs):
        q2, ko2, kp2, vo2, vp2, dy2, y2 = (refs[2 * i:2 * i + 2] for i in range(7))
        lse_ref = refs[14]
        dq_ref, dk_ref, dv_ref, dq_st, dk_st, dv_st, dkc, dvc = refs[15:]
        u = pl.program_id(0)

        @pl.when(u == 0)
        def _():
            dkc[...] = jnp.zeros_like(dkc)
            dvc[...] = jnp.zeros_like(dvc)

        @pl.when(u < nu)
        def _():
            lane = lax.broadcasted_iota(jnp.int32, (BLK, GW), 1)

            def band_grads(rows_at, prev_rows_at, valid):
                q = rows_at(q2) * QK_SCALE
                kb = jnp.concatenate([prev_rows_at(kp2, ko2), rows_at(ko2)], axis=0).astype(BF16)
                vb = jnp.concatenate([prev_rows_at(vp2, vo2), rows_at(vo2)], axis=0).astype(BF16)
                dyv = rows_at(dy2)
                yv = rows_at(y2)
                lse_v = rows_at(refs[14:15])
                heads = [(lane >= h * HD) & (lane < (h + 1) * HD) for h in range(NH)]
                qs = jnp.concatenate([jnp.where(hm, q, 0.0) for hm in heads], axis=0).astype(BF16)
                dys = jnp.concatenate([jnp.where(hm, dyv, 0.0) for hm in heads], axis=0)
                lse_s = jnp.concatenate([lse_v[:, 16 * h:16 * h + 1] for h in range(NH)], axis=0)
                s = lax.dot_general(qs, kb, NT, preferred_element_type=F32)
                p = jnp.where(jnp.concatenate([valid] * NH, axis=0), jnp.exp(s - lse_s), 0.0)
                delta = jnp.sum(dys * jnp.concatenate([yv] * NH, axis=0), axis=-1, keepdims=True)
                dys16 = dys.astype(BF16)
                dp = lax.dot_general(dys16, vb, NT, preferred_element_type=F32)
                ds = (p * (dp - delta)).astype(BF16)
                dq_all = jnp.dot(ds, kb, preferred_element_type=F32)
                dq = jnp.zeros((BLK, GW), F32)
                for h, hm in enumerate(heads):
                    dq = jnp.where(hm, dq_all[h * BLK:(h + 1) * BLK], dq)
                dkb = lax.dot_general(ds, qs, TN, preferred_element_type=F32)
                dvb = lax.dot_general(p.astype(BF16), dys16, TN, preferred_element_type=F32)
                return dq * QK_SCALE, dkb, dvb

            if dil == 1:
                def block(b):
                    return lambda pair: jnp.concatenate([h[pl.ds(b * BLK, BLK), :] for h in pair], axis=1)

                def put(stage, b, val, add=False):
                    for j in range(2):
                        part = val[:, 128 * j:128 * (j + 1)]
                        rows_j = (j, pl.ds(b * BLK, BLK), slice(None))
                        stage[rows_j] = stage[rows_j] + part if add else part

                for b in range(nsub):
                    before = (lambda prev_pair, own_pair, b=b: block(b - 1)(own_pair)) if b else (
                        lambda prev_pair, own_pair: block(nsub - 1)(prev_pair))
                    dq, dkb, dvb = band_grads(block(b), before, _band_mask(1 if b else u))
                    put(dq_st, b, dq)
                    if b == 0:
                        put(dkc, nsub - 1, dkb[:BLK], add=True)
                        put(dvc, nsub - 1, dvb[:BLK], add=True)

                        @pl.when(u > 0)
                        def _():
                            _unstage(dkc, dk_ref)
                            _unstage(dvc, dv_ref)
                    else:
                        put(dk_st, b - 1, dkb[:BLK], add=True)
                        put(dv_st, b - 1, dvb[:BLK], add=True)
                    put(dk_st, b, dkb[BLK:])
                    put(dv_st, b, dvb[BLK:])
                _unstage(dq_st, dq_ref)
                dkc[...] = dk_st[...]
                dvc[...] = dv_st[...]
            else:
                valid = _band_mask(u)

                def step(r, carry):
                    dq, dkb, dvb = band_grads(lambda pair: _get_rows(pair, r, dil),
                                              lambda prev_pair, own_pair: _get_rows(prev_pair, r, dil), valid)
                    _put_rows(dq_st, r, dil, dq)
                    _put_rows(dk_st, r, dil, _get_rows((dkc.at[0], dkc.at[1]), r, dil) + dkb[:BLK])
                    _put_rows(dv_st, r, dil, _get_rows((dvc.at[0], dvc.at[1]), r, dil) + dvb[:BLK])
                    _put_rows(dkc, r, dil, dkb[BLK:])
                    _put_rows(dvc, r, dil, dvb[BLK:])
                    return carry

                lax.fori_loop(0, dil, step, 0, unroll=min(dil, 8))
                _unstage(dq_st, dq_ref)

                @pl.when(u > 0)
                def _():
                    _unstage(dk_st, dk_ref)
                    _unstage(dv_st, dv_ref)

        @pl.when(u == nu)
        def _():
            _unstage(dkc, dk_ref)
            _unstage(dvc, dv_ref)

    blocks = (15 + 6 + 5) * unit * 128 * 4 * 2
    return _pc(body, name=name, grid=(nu + 1,), in_specs=in_specs, out_specs=[tok, kv_out, kv_out],
               out_shape=[_sds((t_dim, GW), F32)] * 3, scratch=[pltpu.VMEM((2, unit, 128), F32)] * 5,
               vmem=blocks + (12 << 20))(*([z] * 10), dy, dy, y, y, lse)


_INV_SQRT2 = 1.0 / math.sqrt(2.0)
_INV_SQRT2PI = 1.0 / math.sqrt(2.0 * math.pi)


def _gelu(v):
    return 0.5 * v * (1.0 + lax.erf(v * _INV_SQRT2))


def _gelu_grad(v):
    return 0.5 * (1.0 + lax.erf(v * _INV_SQRT2)) + v * (_INV_SQRT2PI * jnp.exp(-0.5 * v * v))


def _gmlp_core(z_refs, lng, lnb, ws_ref, bs_ref):
    u_pre = jnp.concatenate([z_refs[0][...], z_refs[1][...]], axis=1).astype(F32)
    v_pre = jnp.concatenate([z_refs[2][...], z_refs[3][...]], axis=1).astype(F32)
    u = _gelu(u_pre)
    v = _gelu(v_pre)
    xc = v - jnp.mean(v, axis=-1, keepdims=True)
    rstd = lax.rsqrt(jnp.mean(xc * xc, axis=-1, keepdims=True) + EPS)
    xhat = xc * rstd
    vn = xhat * lng + lnb
    ti = lax.broadcasted_iota(jnp.int32, (BLK, BLK), 0)
    si = lax.broadcasted_iota(jnp.int32, (BLK, BLK), 1)
    causal = ti >= si
    vn16 = vn.astype(BF16)
    ws = [jnp.where(causal, ws_ref[g], 0.0).astype(BF16) for g in range(4)]
    mixed = jnp.concatenate(
        [jnp.dot(ws[g], vn16[:, g * BLK:(g + 1) * BLK], preferred_element_type=F32) + bs_ref[g] for g in range(4)], axis=1)
    return u_pre, v_pre, u, xhat, rstd, vn16, ws, causal, mixed


def _gmlp_specs():
    z_specs = [_bs((BLK, GW), (lambda i, c=c: (i, UV_B + c))) for c in range(4)]
    par_specs = [_bs((1, BW), lambda i: (0, 0)), _bs((1, BW), lambda i: (0, 0)),
                 _bs((4, BLK, BLK), lambda i: (0, 0, 0)), _bs((4, BLK, 1), lambda i: (0, 0, 0))]
    return z_specs, par_specs


def _gmlp_fwd(z, lng, lnb, ws, bs, name):
    t_dim = z.shape[0]
    z_specs, par_specs = _gmlp_specs()

    def body(z0, z1, z2, z3, lng_ref, lnb_ref, ws_ref, bs_ref, o_ref):
        core = _gmlp_core((z0, z1, z2, z3), lng_ref[...], lnb_ref[...], ws_ref, bs_ref)
        o_ref[...] = (core[2] * core[8]).astype(BF16)

    return _pc(body, name=name, grid=(t_dim // BLK,), in_specs=z_specs + par_specs,
               out_specs=_bs((BLK, BW), lambda i: (i, 0)), out_shape=_sds((t_dim, BW), BF16))(z, z, z, z, lng, lnb, ws, bs)


def _gmlp_bwd(z, dy, lng, lnb, ws, bs, name):
    t_dim = z.shape[0]
    z_specs, par_specs = _gmlp_specs()

    def body(z0, z1, z2, z3, lng_ref, lnb_ref, ws_ref, bs_ref, dy_ref, duv_ref, dws_ref, dbs_ref, dlng_ref, dlnb_ref):
        lng_v = lng_ref[...]
        u_pre, v_pre, u, xhat, rstd, vn16, wsm, causal, mixed = _gmlp_core((z0, z1, z2, z3), lng_v, lnb_ref[...], ws_ref, bs_ref)
        dyv = dy_ref[...].astype(F32)
        du = dyv * mixed
        dmixed = dyv * u

        @pl.when(pl.program_id(0) == 0)
        def _():
            dws_ref[...] = jnp.zeros_like(dws_ref)
            dbs_ref[...] = jnp.zeros_like(dbs_ref)
            dlng_ref[...] = jnp.zeros_like(dlng_ref)
            dlnb_ref[...] = jnp.zeros_like(dlnb_ref)

        dvn_parts = []
        for g in range(4):
            dm = dmixed[:, g * BLK:(g + 1) * BLK]
            dm16 = dm.astype(BF16)
            dw = lax.dot_general(dm16, vn16[:, g * BLK:(g + 1) * BLK], NT, preferred_element_type=F32)
            dws_ref[g] += jnp.where(causal, dw, 0.0)
            dbs_ref[g] += jnp.sum(dm.T, axis=0, keepdims=True)
            dvn_parts.append(lax.dot_general(wsm[g], dm16, TN, preferred_element_type=F32))
        dvn = jnp.concatenate(dvn_parts, axis=1)
        dlng_ref[...] += jnp.sum(dvn * xhat, axis=0, keepdims=True)
        dlnb_ref[...] += jnp.sum(dvn, axis=0, keepdims=True)
        dxh = dvn * lng_v
        dv = rstd * (dxh - jnp.mean(dxh, axis=-1, keepdims=True) - xhat * jnp.mean(dxh * xhat, axis=-1, keepdims=True))
        duv_ref[:, :BW] = (du * _gelu_grad(u_pre)).astype(BF16)
        duv_ref[:, BW:] = (dv * _gelu_grad(v_pre)).astype(BF16)

    return _pc(body, name=name, grid=(t_dim // BLK,),
               in_specs=z_specs + par_specs + [_bs((BLK, BW), lambda i: (i, 0))],
               out_specs=[_bs((BLK, 2 * BW), lambda i: (i, 0)), _bs((4, BLK, BLK), lambda i: (0, 0, 0)),
                          _bs((4, 1, BLK), lambda i: (0, 0, 0)), _bs((1, BW), lambda i: (0, 0)), _bs((1, BW), lambda i: (0, 0))],
               out_shape=[_sds((t_dim, 2 * BW), BF16), _sds((4, BLK, BLK), F32), _sds((4, 1, BLK), F32),
                          _sds((1, BW), F32), _sds((1, BW), F32)])(z, z, z, z, lng, lnb, ws, bs, dy)


def _pool_core(x, gi):
    t_dim = x.shape[0]
    w = jnp.left_shift(2, gi)
    row = lax.broadcasted_iota(jnp.int32, x.shape, 0)
    s = x
    for k in (1, 2, 4, 8):
        sh = jnp.where(row >= k, pltpu.roll(s, k, 0), 0.0)
        s = jnp.where(k < w, s + sh, s)
    cnt = jnp.minimum(row + 1, w).astype(F32)
    return s / cnt - x, cnt, row, w, t_dim


def _pool_fwd(z, cw, cs, name):
    t_dim = z.shape[0]

    def body(z_ref, cw_ref, cs_ref, o_ref):
        pooled = _pool_core(z_ref[...].astype(F32), pl.program_id(0))[0]
        y = jnp.dot(pooled.astype(BF16), cw_ref[0].astype(BF16), preferred_element_type=F32) * cs_ref[0]
        o_ref[...] = y.astype(BF16)

    return _pc(body, name=name, grid=(4,),
               in_specs=[_bs((t_dim, BLK), lambda g: (0, C_B128 + g)), _bs((1, BLK, BLK), lambda g: (g, 0, 0)),
                         _bs((1, 1, BLK), lambda g: (g, 0, 0))],
               out_specs=_bs((t_dim, BLK), lambda g: (0, g)), out_shape=_sds((t_dim, BW), BF16),
               vmem=40 << 20)(z, cw, cs)


def _pool_bwd(z, dy, cw, cs, name):
    t_dim = z.shape[0]

    def body(z_ref, dy_ref, cw_ref, cs_ref, dx_ref, dcw_ref, dcs_ref):
        pooled, cnt, row, w, _ = _pool_core(z_ref[...].astype(F32), pl.program_id(0))
        p16 = pooled.astype(BF16)
        cw16 = cw_ref[0].astype(BF16)
        dyv = dy_ref[...].astype(F32)
        lin = jnp.dot(p16, cw16, preferred_element_type=F32)
        dcs_ref[0] = jnp.sum(dyv * lin, axis=0, keepdims=True)
        dys = (dyv * cs_ref[0]).astype(BF16)
        dcw_ref[0] = lax.dot_general(p16, dys, TN, preferred_element_type=F32)
        dpool = lax.dot_general(dys, cw16, NT, preferred_element_type=F32)
        sb = dpool / cnt
        for k in (1, 2, 4, 8):
            sh = jnp.where(row < t_dim - k, pltpu.roll(sb, t_dim - k, 0), 0.0)
            sb = jnp.where(k < w, sb + sh, sb)
        dx_ref[...] = (sb - dpool).astype(BF16)

    return _pc(body, name=name, grid=(4,),
               in_specs=[_bs((t_dim, BLK), lambda g: (0, C_B128 + g)), _bs((t_dim, BLK), lambda g: (0, g)),
                         _bs((1, BLK, BLK), lambda g: (g, 0, 0)), _bs((1, 1, BLK), lambda g: (g, 0, 0))],
               out_specs=[_bs((t_dim, BLK), lambda g: (0, g)), _bs((1, BLK, BLK), lambda g: (g, 0, 0)),
                          _bs((1, 1, BLK), lambda g: (g, 0, 0))],
               out_shape=[_sds((t_dim, BW), BF16), _sds((4, BLK, BLK), F32), _sds((4, 1, BLK), F32)],
               vmem=48 << 20)(z, dy, cw, cs)


GATE_W = 2 * GW


def _gates_specs(tm):
    per_branch = D // GATE_W
    z_specs = [_bs((tm, GATE_W), (lambda j, i, b=b: (i, G_B * GW // GATE_W + per_branch * b + j))) for b in range(3)]
    bg_specs = [_bs((1, GATE_W), (lambda j, i, b=b: (0, per_branch * b + j))) for b in range(3)]
    tile = _bs((tm, GATE_W), lambda j, i: (i, j))
    return z_specs, bg_specs, tile


def _gates_fwd(z, bg, pa, pb, pc, name, tm=512):
    t_dim = z.shape[0]
    z_specs, bg_specs, tile = _gates_specs(tm)

    def body(za, zb, zc, ba, bb, bc, pa_ref, pb_ref, pc_ref, o_ref):
        o_ref[...] = (_sigmoid(za[...] + ba[...]) * pa_ref[...] + _sigmoid(zb[...] + bb[...]) * pb_ref[...]
                      + _sigmoid(zc[...] + bc[...]) * pc_ref[...]).astype(BF16)

    return _pc(body, name=name, grid=(D // GATE_W, t_dim // tm), in_specs=z_specs + bg_specs + [tile] * 3,
               out_specs=tile, out_shape=_sds((t_dim, D), BF16), vmem=40 << 20)(z, z, z, bg, bg, bg, pa, pb, pc)


def _gates_bwd(z, bg, pa, pb, pc, dm, name, tm=512):
    t_dim = z.shape[0]
    z_specs, bg_specs, tile = _gates_specs(tm)
    bias_out = _bs((1, GATE_W), lambda j, i: (0, j))

    def body(za, zb, zc, ba, bb, bc, pa_ref, pb_ref, pc_ref, dm_ref, dpa, dpb, dpc, dza, dzb, dzc, dba, dbb, dbc):
        dmv = dm_ref[...]
        first = pl.program_id(1) == 0
        for z_ref, b_ref, p_ref, dp_ref, dz_ref, db_ref in ((za, ba, pa_ref, dpa, dza, dba), (zb, bb, pb_ref, dpb, dzb, dbb),
                                                            (zc, bc, pc_ref, dpc, dzc, dbc)):
            gate = _sigmoid(z_ref[...] + b_ref[...])
            dp_ref[...] = (gate * dmv).astype(BF16)
            dz = dmv * p_ref[...] * (gate * (1.0 - gate))
            dz_ref[...] = dz.astype(BF16)

            @pl.when(first)
            def _():
                db_ref[...] = jnp.zeros_like(db_ref)

            db_ref[...] += jnp.sum(dz, axis=0, keepdims=True)

    return _pc(body, name=name, grid=(D // GATE_W, t_dim // tm), in_specs=z_specs + bg_specs + [tile] * 4,
               out_specs=[tile] * 6 + [bias_out] * 3,
               out_shape=[_sds((t_dim, D), BF16)] * 6 + [_sds((1, D), F32)] * 3, vmem=48 << 20)(
                   z, z, z, bg, bg, bg, pa, pb, pc, dm)


def _concat_bf16(parts, name, tm=512):
    t_dim = parts[0].shape[0]
    offs = [sum(p.shape[1] for p in parts[:i]) for i in range(len(parts) + 1)]

    def body(*refs):
        o_ref = refs[-1]
        for ref, off in zip(refs[:-1], offs):
            o_ref[:, off:off + ref.shape[1]] = ref[...].astype(BF16)

    return _pc(body, name=name, grid=(t_dim // tm,), in_specs=[_bs((tm, p.shape[1]), lambda i: (i, 0)) for p in parts],
               out_specs=_bs((tm, offs[-1]), lambda i: (i, 0)), out_shape=_sds((t_dim, offs[-1]), BF16),
               vmem=48 << 20)(*parts)


def _coords():
    return lax.axis_index("x"), lax.axis_index("y"), lax.axis_index("c")


def _peer(k, x, y, c):
    px = 1 - x if k & 4 else x
    py = 1 - y if k & 2 else y
    pc = 1 - c if k & 1 else c
    return (px, py, pc), 4 * px + 2 * py + pc


def _exchange_copies(kind, bufs, send, recv):
    x, y, c = _coords()
    me = 4 * x + 2 * y + c
    na = len(bufs) if kind == "gather" else len(bufs) // 2
    out = []
    for k in range(1, NDEV):
        to, peer = _peer(k, x, y, c)
        for a in range(na):
            sems = dict(send_sem=send.at[a * 7 + k - 1], recv_sem=recv.at[a * 7 + k - 1], device_id=to, device_id_type=MESH)
            if kind == "gather":
                mine = pltpu.make_async_remote_copy(src_ref=bufs[a].at[:, me], dst_ref=bufs[a].at[:, me], **sems)
                theirs = pltpu.make_async_remote_copy(src_ref=bufs[a].at[:, me], dst_ref=bufs[a].at[:, peer], **sems)
            else:
                src, land = bufs[a], bufs[na + a]
                mine = pltpu.make_async_remote_copy(src_ref=src.at[:, peer], dst_ref=land.at[me], **sems)
                theirs = pltpu.make_async_remote_copy(src_ref=src.at[:, peer], dst_ref=land.at[peer], **sems)
            out.append((mine, theirs))
    return out


_HBM = pl.BlockSpec(memory_space=pltpu.HBM)
_SEM = pl.BlockSpec(memory_space=pltpu.SEMAPHORE)
_EFFECT = pltpu.SideEffectType.DATAFLOW_SIDE_EFFECTING


def _exchange_start(kind, arrays, after, name):
    na = len(arrays)
    if kind == "scatter":
        arrays = list(arrays) + [lax.empty((NDEV, s.shape[0]) + s.shape[2:], s.dtype) for s in arrays]
    nb = len(arrays)

    def body(*refs):
        bufs = refs[:nb]
        send, recv = refs[nb + 1], refs[nb + 2]
        token = refs[-1]
        for mine, _ in _exchange_copies(kind, bufs, send, recv):
            mine.start()
        token[...] = jnp.zeros_like(token)

    arrays = [pltpu.with_memory_space_constraint(b, pltpu.HBM) for b in arrays]
    outs = pl.pallas_call(
        body, name=name,
        out_shape=(pltpu.SemaphoreType.DMA((7 * na,)), pltpu.SemaphoreType.DMA((7 * na,)),
                   *[pltpu.HBM(b.shape, b.dtype) for b in arrays], _sds((8, 128), F32)),
        in_specs=[_HBM] * nb + [pl.BlockSpec(memory_space=pl.ANY)],
        out_specs=(_SEM, _SEM, *([_HBM] * nb), pl.BlockSpec(memory_space=pltpu.VMEM)),
        input_output_aliases={i: 2 + i for i in range(nb)},
        compiler_params=pltpu.CompilerParams(has_side_effects=_EFFECT), interpret=False)(*arrays, after)
    return outs[0], outs[1], list(outs[2:2 + nb]), outs[-1]


def _exchange_wait(kind, started, after, name):
    send, recv, arrays, _ = started
    nb = len(arrays)

    def body(*refs):
        for mine, theirs in _exchange_copies(kind, refs[:nb], refs[nb], refs[nb + 1]):
            mine.wait_send()
            theirs.wait_recv()

    outs = pl.pallas_call(
        body, name=name,
        out_shape=tuple(pltpu.HBM(b.shape, b.dtype) for b in arrays),
        in_specs=[_HBM] * nb + [_SEM, _SEM, pl.BlockSpec(memory_space=pl.ANY)],
        out_specs=tuple([_HBM] * nb),
        input_output_aliases={i: i for i in range(nb)},
        compiler_params=pltpu.CompilerParams(has_side_effects=_EFFECT), interpret=False)(*arrays, send, recv, after)
    return list(outs)


def _adam_math(w, g, m, v):
    m2 = ADAM_B1 * m + (1.0 - ADAM_B1) * g
    v2 = ADAM_B2 * v + (1.0 - ADAM_B2) * jnp.square(g)
    m_hat = m2 / (1.0 - ADAM_B1 ** ADAM_STEP)
    v_hat = v2 / (1.0 - ADAM_B2 ** ADAM_STEP)
    return -ADAM_LR * (m_hat / (jnp.sqrt(v_hat) + ADAM_EPS) + ADAM_WD * w), m2, v2


_DIRECT = (1, 4, 2, 6)


def _gather_copies(bufs, send_d, recv_d, send_f=None, recv_f=None):
    x, y, c = _coords()
    me = 4 * x + 2 * y + c
    sibling, _ = _peer(1, x, y, c)
    direct, forward = [], []
    for a, buf in enumerate(bufs):
        for j, k in enumerate(_DIRECT):
            to, peer = _peer(k, x, y, c)
            sems = dict(send_sem=send_d.at[4 * a + j], recv_sem=recv_d.at[4 * a + j], device_id=to, device_id_type=MESH)
            direct.append((pltpu.make_async_remote_copy(src_ref=buf.at[:, me], dst_ref=buf.at[:, me], **sems),
                           pltpu.make_async_remote_copy(src_ref=buf.at[:, me], dst_ref=buf.at[:, peer], **sems)))
        if send_f is None:
            continue
        for j, k in enumerate(_DIRECT[1:]):
            _, near = _peer(k, x, y, c)
            _, far = _peer(k | 1, x, y, c)
            sems = dict(send_sem=send_f.at[3 * a + j], recv_sem=recv_f.at[3 * a + j], device_id=sibling, device_id_type=MESH)
            forward.append((pltpu.make_async_remote_copy(src_ref=buf.at[:, near], dst_ref=buf.at[:, near], **sems),
                            pltpu.make_async_remote_copy(src_ref=buf.at[:, near], dst_ref=buf.at[:, far], **sems)))
    return direct, forward


def _gather_start(lands, after, name):
    na = len(lands)

    def body(*refs):
        direct, _ = _gather_copies(refs[:na], refs[na + 1], refs[na + 2])
        for mine, _ in direct:
            mine.start()
        refs[-1][...] = jnp.zeros_like(refs[-1])

    lands = [pltpu.with_memory_space_constraint(b, pltpu.HBM) for b in lands]
    outs = pl.pallas_call(
        body, name=name,
        out_shape=(pltpu.SemaphoreType.DMA((4 * na,)), pltpu.SemaphoreType.DMA((4 * na,)),
                   *[pltpu.HBM(b.shape, b.dtype) for b in lands], _sds((8, 128), F32)),
        in_specs=[_HBM] * na + [pl.BlockSpec(memory_space=pl.ANY)],
        out_specs=(_SEM, _SEM, *([_HBM] * na), pl.BlockSpec(memory_space=pltpu.VMEM)),
        input_output_aliases={i: 2 + i for i in range(na)},
        compiler_params=pltpu.CompilerParams(has_side_effects=_EFFECT), interpret=False)(*lands, after)
    return outs[0], outs[1], list(outs[2:2 + na]), outs[-1]


def _gather_forward(started, after, name):
    send_d, recv_d, lands, _ = started
    na = len(lands)

    def body(*refs):
        direct, forward = _gather_copies(refs[:na], refs[na], refs[na + 1], refs[2 * na + 3], refs[2 * na + 4])
        for a in range(na):
            for j in range(3):
                direct[4 * a + 1 + j][1].wait_recv()
                forward[3 * a + j][0].start()
        refs[-1][...] = jnp.zeros_like(refs[-1])

    outs = pl.pallas_call(
        body, name=name,
        out_shape=(*[pltpu.HBM(b.shape, b.dtype) for b in lands], pltpu.SemaphoreType.DMA((3 * na,)),
                   pltpu.SemaphoreType.DMA((3 * na,)), _sds((8, 128), F32)),
        in_specs=[_HBM] * na + [_SEM, _SEM, pl.BlockSpec(memory_space=pl.ANY)],
        out_specs=(*([_HBM] * na), _SEM, _SEM, pl.BlockSpec(memory_space=pltpu.VMEM)),
        input_output_aliases={i: i for i in range(na)},
        compiler_params=pltpu.CompilerParams(has_side_effects=_EFFECT), interpret=False)(*lands, send_d, recv_d, after)
    return send_d, recv_d, outs[na], outs[na + 1], list(outs[:na]), outs[-1]


def _gather_wait(forwarded, after, name):
    send_d, recv_d, send_f, recv_f, lands, _ = forwarded
    na = len(lands)

    def body(*refs):
        direct, forward = _gather_copies(refs[:na], refs[na], refs[na + 1], refs[na + 2], refs[na + 3])
        for a in range(na):
            for j in range(4):
                direct[4 * a + j][0].wait_send()
            direct[4 * a][1].wait_recv()
            for j in range(3):
                forward[3 * a + j][0].wait_send()
                forward[3 * a + j][1].wait_recv()

    outs = pl.pallas_call(
        body, name=name, out_shape=tuple(pltpu.HBM(b.shape, b.dtype) for b in lands),
        in_specs=[_HBM] * na + [_SEM] * 4 + [pl.BlockSpec(memory_space=pl.ANY)], out_specs=tuple([_HBM] * na),
        input_output_aliases={i: i for i in range(na)},
        compiler_params=pltpu.CompilerParams(has_side_effects=_EFFECT), interpret=False)(
            *lands, send_d, recv_d, send_f, recv_f, after)
    return list(outs)


def _reduce_update(landed, own, me, piece, w, m, v, layer, prev, transposed, name):
    rows, cols = landed.shape[2], landed.shape[3]
    tc = min(cols, 512 if rows <= FFS else 256)
    assert cols % tc == 0 and w.shape[1:] == ((cols, rows) if transposed else (rows, cols))
    lane_chunks = [(c0, min(128, rows - c0)) for c0 in range(0, rows, 128)]

    def body(me_ref, b_ref, own_ref, w_ref, m_ref, v_ref, *rest):
        outs = rest[-4:]
        mine = me_ref[0]
        s = None
        for d in range(NDEV):
            other = jnp.where(mine == d, (d + 1) % NDEV, d)
            part = jnp.where(mine == d, own_ref[0, 0], b_ref[other, 0]).astype(F32)
            s = part if s is None else s + part
        if not transposed:
            for o, val in zip(outs, (s,) + _adam_math(w_ref[0], s, m_ref[0], v_ref[0])):
                o[0] = val
            return
        for c0, wd in lane_chunks:
            chunk = s[c0:c0 + wd]
            if wd < 128:
                chunk = jnp.concatenate([chunk, jnp.zeros((128 - wd, tc), F32)], axis=0)
            g = chunk.T[:, :wd]
            sl = (0, slice(None), slice(c0, c0 + wd))
            for o, val in zip(outs, (g,) + _adam_math(w_ref[sl], g, m_ref[sl], v_ref[sl])):
                o[sl] = val

    if transposed:
        nat = _bs((1, tc, rows), lambda j, me_ref: (layer, j, 0))
    else:
        nat = _bs((1, rows, tc), lambda j, me_ref: (layer, 0, j))
    in_specs = [_bs((NDEV, 1, rows, tc), lambda j, me_ref: (0, piece, 0, j)),
                _bs((1, 1, rows, tc), lambda j, me_ref: (piece, me_ref[0], 0, j)), nat, nat, nat]
    args = [me, landed, own, w, m, v]
    alias = {}
    if prev is not None:
        in_specs += [pl.BlockSpec(memory_space=pl.ANY)] * 4
        args += list(prev)
        alias = {6 + i: i for i in range(4)}
    grid_spec = pltpu.PrefetchScalarGridSpec(num_scalar_prefetch=1, grid=(cols // tc,), in_specs=in_specs, out_specs=[nat] * 4)
    return pl.pallas_call(body, name=name, grid_spec=grid_spec, out_shape=[_sds(w.shape, F32)] * 4,
                          input_output_aliases=alias, interpret=False)(*args)


SMALL_LAYER = ("ffn1_norm", "mix_norm", "b_gate", "b_ln_g", "b_ln_b", "b_w_s", "b_b_s", "c_w", "c_scale", "ffn2_norm")
SMALL_ROWS = {"ffn1_norm": 8, "mix_norm": 8, "b_gate": 24, "b_ln_g": 4, "b_ln_b": 4, "b_w_s": 512, "b_b_s": 4, "c_w": 512,
              "c_scale": 4, "ffn2_norm": 8, "final_norm": 8}
LAYER_ROWS = sum(SMALL_ROWS[k] for k in SMALL_LAYER)


def _small_offset(name):
    if name == "final_norm":
        return LAYER_ROWS
    return sum(SMALL_ROWS[k] for k in SMALL_LAYER[:SMALL_LAYER.index(name)])


def _pack_small_grads(layer_grads, final, me, name):
    flat, where = [], []
    for k in SMALL_LAYER:
        parts = layer_grads[k] if k == "b_gate" else [layer_grads[k]]
        for i, part in enumerate(parts):
            flat.append(part)
            where.append(_small_offset(k) + i * (D // 128))
    if final is not None:
        flat.append(final)
        where.append(_small_offset("final_norm"))
    pack_rows = LAYER_ROWS + (SMALL_ROWS["final_norm"] if final is not None else 0)

    def body(me_ref, *refs):
        del me_ref
        o_ref = refs[-1]
        for ref, off in zip(refs[:-1], where):
            if len(ref.shape) == 2:
                for j in range(ref.shape[1] // 128):
                    o_ref[0, 0, off + j:off + j + 1, :] = ref[:, 128 * j:128 * (j + 1)]
            else:
                rows = ref.shape[1]
                for g in range(4):
                    o_ref[0, 0, off + g * rows:off + (g + 1) * rows, :] = ref[g]

    grid_spec = pltpu.PrefetchScalarGridSpec(
        num_scalar_prefetch=1, grid=(1,),
        in_specs=[pl.BlockSpec(a.shape, (lambda i, me_ref, nd=a.ndim: (0,) * nd)) for a in flat],
        out_specs=_bs((1, 1, pack_rows, 128), lambda i, me_ref: (0, me_ref[0], 0, 0)))
    return pl.pallas_call(body, name=name, grid_spec=grid_spec, out_shape=_sds((1, NDEV, pack_rows, 128), F32),
                          interpret=False)(me, *flat)


def _adamw_small(gathered, w, mom, var, name):
    names = SMALL_LAYER + ("final_norm",)
    ins = list(gathered) + [t[k] for k in names for t in (w, mom, var)]

    def body(*refs):
        outs = refs[len(ins):]

        def rows(l, off, n):
            land = refs[l]
            s = land[0, 0, off:off + n, :]
            for d in range(1, NDEV):
                s = s + land[0, d, off:off + n, :]
            return s

        def as_row(l, off, n):
            s = rows(l, off, n)
            return jnp.concatenate([s[j:j + 1, :] for j in range(n)], axis=1)

        def update(i, idx, g):
            wr, mr, vr = (refs[DEPTH + 3 * i + t] for t in range(3))
            d, m2, v2 = _adam_math(wr[idx], g, mr[idx], vr[idx])
            for o, val in zip(outs[4 * i:4 * i + 4], (g, d, m2, v2)):
                o[idx] = val

        for i, k in enumerate(names):
            n = SMALL_ROWS[k]
            off = _small_offset(k)
            if k == "final_norm":
                update(i, (slice(0, 1),), as_row(DEPTH - 1, off, n))
                continue
            for l in range(DEPTH):
                if k in ("b_w_s", "c_w"):
                    for g in range(4):
                        update(i, (l, g), rows(l, off + g * BLK, BLK))
                elif k in ("b_b_s", "c_scale"):
                    update(i, (l,), rows(l, off, n))
                else:
                    update(i, (slice(l, l + 1),), as_row(l, off, n))

    out_shape = [_sds(w[k].shape, F32) for k in names for _ in range(4)]
    outs = pl.pallas_call(body, name=name, out_shape=out_shape,
                          compiler_params=pltpu.CompilerParams(vmem_limit_bytes=48 << 20), interpret=False)(*ins)
    return tuple({k: outs[4 * i + t] for i, k in enumerate(names)} for t in range(4))


PG, PD = 0, 2


def _ffn_fwd(x, h, wf, tag, mid, next_gain):
    g_pre, u_pre, act = _gu_swiglu(h, wf, tm=2048, tn=256, b_row0=PG * FF, name=f"gu_{tag}")
    token = mid(act)
    out = _mm(act, wf, mode="nn", tm=512, tn=D, tk=FF, n=D, b_row0=PD * FF, out_dtype=F32, scale=0.5, res=x, after=token,
              norm_gain=next_gain, name=f"down_{tag}")
    out, h_next = out if next_gain is not None else (out, None)
    return out, h_next, (x, h, g_pre, u_pre, act)


def _ffn_bwd(dout, saved, gain, wf, tag, after, emit):
    x, h, g_pre, u_pre, act = saved
    grf = lax.empty((3, NDEV, FFS, D), BF16)
    dgu = _dact_dswiglu(dout, wf, g_pre, u_pre, tm=256, b_row0=PD * FF, after=after, name=f"dgu_{tag}")
    grf = _wgrad(act, dout, grf, piece0=PD, rows=FFS, tm=FF // 2, tn=D, tk=1024, scale=0.5, name=f"wg_down_{tag}")
    grf = _wgrad(dgu, h, grf, piece0=PG, rows=FFS, tm=FF // 2, tn=D, tk=2048, name=f"wg_gu_{tag}")
    token = emit([grf])
    dx, dgain = _mm_drms(dgu, wf, x, gain, dout, tm=256, b_row0=PG * FF, after=token, name=f"dx_{tag}")
    return dx, dgain, token


def _mixer_fwd(x, h, sp, ww, wo, wa, wbc, tag, mid, next_gain):
    zq = _mm(h, ww, mode="nt", tm=1024, tn=1152, tk=D, n=QKV_W, out_dtype=F32, name=f"zq_{tag}")
    z = _mm(h, ww, mode="nt", tm=1024, tn=1152, tk=D, n=INW - QKV_W, b_row0=QKV_W, out_dtype=BF16, name=f"z_{tag}")
    token = mid(z)
    acc, ml = _attn_fwd(zq, None, None, gi=0, first=True, final=False, name=f"attn0_{tag}")
    acc, ml = _attn_fwd(zq, acc, ml, gi=1, first=False, final=False, name=f"attn1_{tag}")
    ya, lse = _attn_fwd(zq, acc, ml, gi=2, first=False, final=True, name=f"attn2_{tag}")
    yb = _gmlp_fwd(z, sp["b_ln_g"], sp["b_ln_b"], sp["b_w_s"], sp["b_b_s"], f"gmlp_{tag}")
    yc = _pool_fwd(z, sp["c_w"], sp["c_scale"], f"pool_{tag}")
    pa = _mm(ya, wa, mode="nt", tm=1024, tn=D, tk=GW, n=D, out_dtype=BF16, after=token, name=f"proj_a_{tag}")
    pb = _mm(yb, wbc, mode="nt", tm=1024, tn=D, tk=BW, n=D, b_row0=0, out_dtype=BF16, name=f"proj_b_{tag}")
    pc = _mm(yc, wbc, mode="nt", tm=1024, tn=D, tk=BW, n=D, b_row0=D, out_dtype=BF16, name=f"proj_c_{tag}")
    merged = _gates_fwd(z, sp["b_gate"], pa, pb, pc, f"gates_{tag}")
    out, h_next = _mm(merged, wo, mode="nn", tm=1024, tn=D, tk=D, n=D, out_dtype=F32, res=x, norm_gain=next_gain,
                      name=f"out_{tag}")
    return out, h_next, (x, h, zq, z, ya, lse, yb, yc, pa, pb, pc, merged)


def _mixer_bwd(dout, saved, sp, ww, wo, wa, wbc, tag, after, emit):
    x, h, zq, z, ya, lse, yb, yc, pa, pb, pc, merged = saved
    grw = lax.empty((1, NDEV, INS, D), BF16)
    gro = lax.empty((1, NDEV, DS, D), BF16)
    gra = lax.empty((1, NDEV, DS, GW), BF16)
    grbc = lax.empty((2, NDEV, DS, BW), BF16)
    dm = _mm(dout, wo, mode="nt", tm=1024, tn=D, tk=D, n=D, out_dtype=F32, after=after, name=f"dmerged_{tag}")
    gro = _wgrad(merged, dout, gro, piece0=0, rows=DS, tm=D, tn=D, tk=1024, name=f"wg_out_{tag}")
    dpa, dpb, dpc, dza, dzb, dzc, dba, dbb, dbc = _gates_bwd(z, sp["b_gate"], pa, pb, pc, dm, f"dgates_{tag}")
    dya = _mm(dpa, wa, mode="nn", tm=2048, tn=GW, tk=D, n=GW, out_dtype=F32, name=f"dya_{tag}")
    dyb = _mm(dpb, wbc, mode="nn", tm=2048, tn=BW, tk=D, n=BW, b_row0=0, out_dtype=F32, name=f"dyb_{tag}")
    dyc = _mm(dpc, wbc, mode="nn", tm=2048, tn=BW, tk=D, n=BW, b_row0=D, out_dtype=F32, name=f"dyc_{tag}")
    gra = _wgrad(dpa, ya, gra, piece0=0, rows=DS, tm=D, tn=GW, tk=1024, name=f"wg_pa_{tag}")
    grbc = _wgrad(dpb, yb, grbc, piece0=0, rows=DS, tm=D, tn=BW, tk=1024, name=f"wg_pb_{tag}")
    grbc = _wgrad(dpc, yc, grbc, piece0=1, rows=DS, tm=D, tn=BW, tk=1024, name=f"wg_pc_{tag}")
    dqkv = [_attn_bwd(zq, dya, ya, lse, gi=gi, name=f"dattn{gi}_{tag}") for gi in range(3)]
    duv, dws, dbs, dlng, dlnb = _gmlp_bwd(z, dyb, sp["b_ln_g"], sp["b_ln_b"], sp["b_w_s"], sp["b_b_s"], f"dgmlp_{tag}")
    dxc, dcw, dcs = _pool_bwd(z, dyc, sp["c_w"], sp["c_scale"], f"dpool_{tag}")
    dz = _concat_bf16([dqkv[gi][part] for part in range(3) for gi in range(3)] + [duv, dxc, dza, dzb, dzc], f"dz_{tag}")
    grw = _wgrad(dz, h, grw, piece0=0, rows=INS, tm=INW // 2, tn=512, tk=1024, name=f"wg_in_{tag}")
    token = emit([grw, gro, gra, grbc])
    dx, dgain = _mm_drms(dz, ww, x, sp["mix_norm"], dout, tm=256, b_row0=0, after=token, name=f"dx_mix_{tag}")
    small = {"mix_norm": dgain, "b_gate": [dba, dbb, dbc], "b_ln_g": dlng, "b_ln_b": dlnb,
             "b_w_s": dws, "b_b_s": dbs, "c_w": dcw, "c_scale": dcs}
    return dx, small, token


PARTS = ("f1", "mix", "f2")


def _forward(xs, tgt, weights_of, pass_on, small, final_gain):
    saved, views = [], []
    cur = xs
    h = _rms_fwd(xs, small[0]["ffn1_norm"], "rms_first")
    for l in range(DEPTH):
        last = l + 1 == DEPTH
        (wf1,) = weights_of(l, "f1", cur)
        cur, h, s1 = _ffn_fwd(cur, h, wf1, f"f1l{l}", lambda t: pass_on(l, "mix", t), small[l]["mix_norm"])
        wmix = weights_of(l, "mix", cur)
        cur, h, s2 = _mixer_fwd(cur, h, small[l], *wmix, f"l{l}", lambda t: pass_on(l, "f2", t), small[l]["ffn2_norm"])
        (wf2,) = weights_of(l, "f2", cur)
        cur, h, s3 = _ffn_fwd(cur, h, wf2, f"f2l{l}", lambda t: None if last else pass_on(l + 1, "f1", t),
                              None if last else small[l + 1]["ffn1_norm"])
        saved.append((s1, s2, s3))
        views.append((wf1, wmix, wf2))
    loss_part, dcur, dfinal = _final_loss(cur, final_gain, tgt, "final_loss")
    return loss_part, dcur, dfinal, saved, views


def _backward_layer(dcur, saved_l, small_l, views_l, l, emit, after):
    wf1, wmix, wf2 = views_l
    s1, s2, s3 = saved_l
    dcur, dn2, token = _ffn_bwd(dcur, s3, small_l["ffn2_norm"], wf2, f"f2l{l}", after, lambda g: emit("f2", g))
    dcur, sg, token = _mixer_bwd(dcur, s2, small_l, *wmix, f"l{l}", token, lambda g: emit("mix", g))
    dcur, dn1, token = _ffn_bwd(dcur, s1, small_l["ffn1_norm"], wf1, f"f1l{l}", token, lambda g: emit("f1", g))
    sg["ffn1_norm"] = dn1
    sg["ffn2_norm"] = dn2
    return dcur, sg, token


BIG = ("ffn1_w_gate", "ffn1_w_up", "ffn1_w_down", "w_in", "w_proj_a", "w_proj_b", "w_proj_c", "w_out",
       "ffn2_w_gate", "ffn2_w_up", "ffn2_w_down")
ALL_WEIGHTS = ("ffn1_norm", "ffn1_w_gate", "ffn1_w_up", "ffn1_w_down", "mix_norm", "w_in", "b_gate", "b_ln_g", "b_ln_b",
               "b_w_s", "b_b_s", "c_w", "c_scale", "w_proj_a", "w_proj_b", "w_proj_c", "w_out", "ffn2_norm", "ffn2_w_gate",
               "ffn2_w_up", "ffn2_w_down", "final_norm")


def _place_shards(w, l, part, me, after, name):
    tr = lambda a: a.transpose(0, 2, 1)
    if part == "mix":
        srcs = [tr(w["w_in"]), w["w_out"], tr(w["w_proj_a"]), tr(w["w_proj_b"]), tr(w["w_proj_c"])]
        groups = [[0], [1], [2], [3, 4]]
    else:
        f = "ffn1" if part == "f1" else "ffn2"
        srcs = [tr(w[f + "_w_gate"]), tr(w[f + "_w_up"]), w[f + "_w_down"]]
        groups = [[0, 1, 2]]

    def body(me_ref, *refs):
        del me_ref
        outs = refs[len(srcs) + 1:]
        for o_ref, members in zip(outs, groups):
            for p, i in enumerate(members):
                o_ref[p, 0] = refs[i][0].astype(BF16)

    out_shape = [_sds((len(g), NDEV) + srcs[g[0]].shape[1:], BF16) for g in groups]
    grid_spec = pltpu.PrefetchScalarGridSpec(
        num_scalar_prefetch=1, grid=(1,),
        in_specs=[_bs((1,) + s.shape[1:], lambda i, me_ref: (l, 0, 0)) for s in srcs] + [pl.BlockSpec(memory_space=pl.ANY)],
        out_specs=[_bs((len(g), 1) + srcs[g[0]].shape[1:], lambda i, me_ref: (0, me_ref[0], 0, 0)) for g in groups])
    return pl.pallas_call(body, name=name, grid_spec=grid_spec, out_shape=out_shape,
                          compiler_params=pltpu.CompilerParams(vmem_limit_bytes=40 << 20), interpret=False)(me, *srcs, after)


def _part_views(part, gathered):
    if part == "mix":
        pw, po, pa, pbc = gathered
        return [pw.reshape(INW, D), po.reshape(D, D), pa.reshape(D, GW), pbc.reshape(2 * D, BW)]
    return [gathered[0].reshape(3 * FF, D)]


def _part_pieces(part):
    if part == "mix":
        return [(0, 0, "w_in", True), (1, 0, "w_out", False), (2, 0, "w_proj_a", True), (3, 0, "w_proj_b", True),
                (3, 1, "w_proj_c", True)]
    f = "ffn1" if part == "f1" else "ffn2"
    return [(0, 0, f + "_w_gate", True), (0, 1, f + "_w_up", True), (0, 2, f + "_w_down", False)]


def _layer_small(w, l):
    r = lambda a, shape: a.reshape(shape)
    return {"ffn1_norm": r(w["ffn1_norm"][l], (1, D)), "mix_norm": r(w["mix_norm"][l], (1, D)),
            "b_gate": r(w["b_gate"][l], (1, 3 * D)), "b_ln_g": r(w["b_ln_g"][l], (1, BW)), "b_ln_b": r(w["b_ln_b"][l], (1, BW)),
            "b_w_s": w["b_w_s"][l], "b_b_s": r(w["b_b_s"][l], (4, BLK, 1)), "c_w": w["c_w"][l],
            "c_scale": r(w["c_scale"][l], (4, 1, BLK)), "ffn2_norm": r(w["ffn2_norm"][l], (1, D))}


def kernel(x, ffn1_norm, ffn1_w_gate, ffn1_w_up, ffn1_w_down, mix_norm, w_in, b_gate, b_ln_g, b_ln_b, b_w_s, b_b_s, c_w, c_scale, w_proj_a, w_proj_b, w_proj_c, w_out, ffn2_norm, ffn2_w_gate, ffn2_w_up, ffn2_w_down, final_norm, loss_target, m_ffn1_norm, m_ffn1_w_gate, m_ffn1_w_up, m_ffn1_w_down, m_mix_norm, m_w_in, m_b_gate, m_b_ln_g, m_b_ln_b, m_b_w_s, m_b_b_s, m_c_w, m_c_scale, m_w_proj_a, m_w_proj_b, m_w_proj_c, m_w_out, m_ffn2_norm, m_ffn2_w_gate, m_ffn2_w_up, m_ffn2_w_down, m_final_norm, v_ffn1_norm, v_ffn1_w_gate, v_ffn1_w_up, v_ffn1_w_down, v_mix_norm, v_w_in, v_b_gate, v_b_ln_g, v_b_ln_b, v_b_w_s, v_b_b_s, v_c_w, v_c_scale, v_w_proj_a, v_w_proj_b, v_w_proj_c, v_w_out, v_ffn2_norm, v_ffn2_w_gate, v_ffn2_w_up, v_ffn2_w_down, v_final_norm):
    w = dict(ffn1_norm=ffn1_norm, ffn1_w_gate=ffn1_w_gate, ffn1_w_up=ffn1_w_up, ffn1_w_down=ffn1_w_down, mix_norm=mix_norm,
             w_in=w_in, b_gate=b_gate, b_ln_g=b_ln_g, b_ln_b=b_ln_b, b_w_s=b_w_s, b_b_s=b_b_s, c_w=c_w, c_scale=c_scale,
             w_proj_a=w_proj_a, w_proj_b=w_proj_b, w_proj_c=w_proj_c, w_out=w_out, ffn2_norm=ffn2_norm, ffn2_w_gate=ffn2_w_gate,
             ffn2_w_up=ffn2_w_up, ffn2_w_down=ffn2_w_down, final_norm=final_norm)
    mom = dict(ffn1_norm=m_ffn1_norm, ffn1_w_gate=m_ffn1_w_gate, ffn1_w_up=m_ffn1_w_up, ffn1_w_down=m_ffn1_w_down,
               mix_norm=m_mix_norm, w_in=m_w_in, b_gate=m_b_gate, b_ln_g=m_b_ln_g, b_ln_b=m_b_ln_b, b_w_s=m_b_w_s, b_b_s=m_b_b_s,
               c_w=m_c_w, c_scale=m_c_scale, w_proj_a=m_w_proj_a, w_proj_b=m_w_proj_b, w_proj_c=m_w_proj_c, w_out=m_w_out,
               ffn2_norm=m_ffn2_norm, ffn2_w_gate=m_ffn2_w_gate, ffn2_w_up=m_ffn2_w_up, ffn2_w_down=m_ffn2_w_down,
               final_norm=m_final_norm)
    var = dict(ffn1_norm=v_ffn1_norm, ffn1_w_gate=v_ffn1_w_gate, ffn1_w_up=v_ffn1_w_up, ffn1_w_down=v_ffn1_w_down,
               mix_norm=v_mix_norm, w_in=v_w_in, b_gate=v_b_gate, b_ln_g=v_b_ln_g, b_ln_b=v_b_ln_b, b_w_s=v_b_w_s, b_b_s=v_b_b_s,
               c_w=v_c_w, c_scale=v_c_scale, w_proj_a=v_w_proj_a, w_proj_b=v_w_proj_b, w_proj_c=v_w_proj_c, w_out=v_w_out,
               ffn2_norm=v_ffn2_norm, ffn2_w_gate=v_ffn2_w_gate, ffn2_w_up=v_ffn2_w_up, ffn2_w_down=v_ffn2_w_down,
               final_norm=v_final_norm)

    t_dim = x.shape[1]
    xs = x.reshape(t_dim, D)
    tgt = loss_target.reshape(t_dim, D)

    me = 4 * lax.axis_index("x") + 2 * lax.axis_index("y") + lax.axis_index("c")
    me_arr = me.astype(jnp.int32).reshape(1)
    gathers = {}
    token = jnp.zeros((8, 128), F32)
    for l in range(DEPTH):
        for part in PARTS:
            lands = _place_shards(w, l, part, me_arr, token, f"place_{part}_l{l}")
            gathers[l, part] = _gather_start(lands, token, f"gather_start_{part}_l{l}")
            token = gathers[l, part][-1]

    def pass_on(l, part, after):
        gathers[l, part] = _gather_forward(gathers[l, part], after, f"gather_forward_{part}_l{l}")
        return gathers[l, part][-1]

    pass_on(0, "f1", token)

    def weights_of(l, part, after):
        return _part_views(part, _gather_wait(gathers[l, part], after, f"gather_wait_{part}_l{l}"))

    small = [_layer_small(w, l) for l in range(DEPTH)]
    loss_part, dcur, dfinal, saved, views = _forward(xs, tgt, weights_of, pass_on, small, w["final_norm"].reshape(1, D))
    loss = lax.psum(loss_part[0, 0], ("x", "y", "c"))

    scatters = []
    small_gathers = [None] * DEPTH
    token = None
    for l in reversed(range(DEPTH)):
        def emit(part, grads, l=l):
            scatters.append((l, part, _exchange_start("scatter", grads, dfinal, f"scatter_start_{part}_l{l}")))
            return scatters[-1][2][-1]
        dcur, small_grads, token = _backward_layer(dcur, saved[l], small[l], views[l], l, emit, token)
        packed = _pack_small_grads(small_grads, dfinal if l == DEPTH - 1 else None, me_arr, f"pack_small_grads_l{l}")
        small_gathers[l] = _exchange_start("gather", [packed], token, f"gather_start_small_l{l}")
        token = small_gathers[l][-1]
    grad_x = dcur.reshape(x.shape)

    done = {}
    behind = token
    for l, part, started in scatters:
        arrs = _exchange_wait("scatter", started, behind, f"scatter_wait_{part}_l{l}")
        na = len(arrs) // 2
        for a, piece, k, transposed in _part_pieces(part):
            flip = transposed and w[k].shape[2] % 128 != 0
            view = (lambda t: t.transpose(0, 2, 1)) if flip else (lambda t: t)
            done[k] = _reduce_update(arrs[na + a], arrs[a], me_arr, piece, view(w[k]), view(mom[k]), view(var[k]), l,
                                     done.get(k), transposed and not flip, f"update_{k}_l{l}")
            behind = done[k][1]
    unflip = lambda k, t: t.transpose(0, 2, 1) if t.shape != w[k].shape else t
    grads, delta, new_m, new_v = ({k: unflip(k, done[k][t]) for k in BIG} for t in range(4))
    gathered = [None] * DEPTH
    for l in reversed(range(DEPTH)):
        gathered[l] = _exchange_wait("gather", small_gathers[l], behind, f"gather_wait_small_l{l}")[0]
        behind = gathered[l]
    as_row = lambda t: dict(t, final_norm=t["final_norm"].reshape(1, D))
    for dst, src in zip((grads, delta, new_m, new_v), _adamw_small(gathered, as_row(w), as_row(mom), as_row(var), "adamw_small")):
        dst.update(src)
        dst["final_norm"] = src["final_norm"].reshape(D)

    return (loss, grad_x, *[grads[k] for k in ALL_WEIGHTS], *[delta[k] for k in ALL_WEIGHTS],
            *[new_m[k] for k in ALL_WEIGHTS], *[new_v[k] for k in ALL_WEIGHTS])
```

```python
import math

import jax
import jax.numpy as jnp
from jax import lax
from jax.experimental import pallas as pl
from jax.experimental.pallas import tpu as pltpu

F32 = jnp.float32
BF16 = jnp.bfloat16
MESH = pl.DeviceIdType.MESH

NDEV = 8
DEPTH = 2
D = 1024
FF = 2816
FFS = FF // NDEV
INW = 6912
INS = INW // NDEV
DS = D // NDEV
BLK = 128
NH = 4
HD = 64
GW = NH * HD
DILS = (1, 4, 16)
QK_SCALE = 1.0 / math.sqrt(HD)
QKV_W = 3 * len(DILS) * GW
Q_B, K_B, V_B = 0, 3, 6
UV_B = 0
C_B128 = 8
G_B = 6
BW = 512
EPS = 1e-6
NEG = -1e30
VMEM_CAP = 60 * 1024 * 1024

ADAM_LR, ADAM_B1, ADAM_B2, ADAM_EPS, ADAM_WD, ADAM_STEP = 0.001, 0.9, 0.999, 1e-08, 0.01, 10

NT = (((1,), (1,)), ((), ()))
TN = (((0,), (0,)), ((), ()))


def _bs(shape, imap):
    return pl.BlockSpec(shape, imap)


def _pc(body, *, name, grid, in_specs, out_specs, out_shape, scratch=(), alias=None, vmem=None):
    params = {}
    if vmem is not None:
        params["vmem_limit_bytes"] = min(int(vmem), VMEM_CAP)
    big = lambda s: math.prod(s.shape) * jnp.dtype(s.dtype).itemsize >= (1 << 20)
    pin = lambda s: pltpu.HBM(s.shape, s.dtype) if big(s) else s
    out_shape = [pin(s) for s in out_shape] if isinstance(out_shape, (list, tuple)) else pin(out_shape)
    call = pl.pallas_call(
        body, name=name, grid=grid, in_specs=in_specs, out_specs=out_specs, out_shape=out_shape,
        scratch_shapes=list(scratch), input_output_aliases=alias or {},
        compiler_params=pltpu.CompilerParams(**params), interpret=False)
    return lambda *args: call(*[pltpu.with_memory_space_constraint(a, pltpu.HBM) if big(a) else a for a in args])


def _sds(shape, dtype):
    return jax.ShapeDtypeStruct(shape, dtype)


def _mm(a, b, *, mode, tm, tn, tk, n, out_dtype, name, b_row0=0, scale=None, res=None, after=None, norm_gain=None):
    m, k_dim = a.shape
    nk = k_dim // tk
    assert m % tm == 0 and n % tn == 0 and k_dim % tk == 0 and (norm_gain is None or tn == n)
    b_off, rem = divmod(b_row0, tk if mode == "nn" else tn)
    assert rem == 0
    grid = (m // tm, n // tn, nk)
    in_specs = [_bs((tm, tk), lambda i, j, k: (i, k))]
    if mode == "nn":
        in_specs.append(_bs((tk, tn), lambda i, j, k: (b_off + k, j)))
    else:
        in_specs.append(_bs((tn, tk), lambda i, j, k: (b_off + j, k)))
    args = [a, b]
    if res is not None:
        in_specs.append(_bs((tm, tn), lambda i, j, k: (i, j)))
        args.append(res)
    if norm_gain is not None:
        in_specs.append(_bs((1, tn), lambda i, j, k: (0, 0)))
        args.append(norm_gain)
    if after is not None:
        in_specs.append(pl.BlockSpec(memory_space=pl.ANY))
        args.append(after)
    n_in = len(args)

    def body(*refs):
        a_ref, b_ref = refs[0], refs[1]
        r_ref = refs[2] if res is not None else None
        g_ref = refs[2 + (res is not None)] if norm_gain is not None else None
        o_ref = refs[n_in]
        av = a_ref[...].astype(BF16)
        bv = b_ref[...].astype(BF16)
        if mode == "nn":
            p = jnp.dot(av, bv, preferred_element_type=F32)
        else:
            p = lax.dot_general(av, bv, NT, preferred_element_type=F32)

        def fin(v):
            if scale is not None:
                v = v * scale
            if r_ref is not None:
                v = r_ref[...] + v
            o_ref[...] = v.astype(out_dtype)
            if g_ref is not None:
                r = lax.rsqrt(jnp.mean(v * v, axis=-1, keepdims=True) + EPS)
                refs[n_in + 1][...] = (v * r * g_ref[...]).astype(BF16)

        if nk == 1:
            fin(p)
        else:
            acc_ref = refs[-1]
            kk = pl.program_id(2)

            @pl.when(kk == 0)
            def _():
                acc_ref[...] = p

            @pl.when(kk > 0)
            def _():
                acc_ref[...] += p

            @pl.when(kk == nk - 1)
            def _():
                fin(acc_ref[...])

    scratch = [pltpu.VMEM((tm, tn), F32)] if nk > 1 else []
    est = 2 * (tm * tk * a.dtype.itemsize + tk * tn * b.dtype.itemsize + tm * tn * jnp.dtype(out_dtype).itemsize)
    est += (2 + (nk > 1)) * tm * tn * 4 + (2 * tm * tn * 4 if res is not None else 0)
    est += (tm * tk * 2 if a.dtype != BF16 else 0) + (tk * tn * 2 if b.dtype != BF16 else 0)
    tile = _bs((tm, tn), lambda i, j, k: (i, j))
    if norm_gain is None:
        return _pc(body, name=name, grid=grid, in_specs=in_specs, out_specs=tile, out_shape=_sds((m, n), out_dtype),
                   scratch=scratch, vmem=est + (8 << 20))(*args)
    return _pc(body, name=name, grid=grid, in_specs=in_specs, out_specs=[tile, tile],
               out_shape=[_sds((m, n), out_dtype), _sds((m, n), BF16)], scratch=scratch,
               vmem=est + 3 * tm * tn * 4 + (8 << 20))(*args)


def _wgrad(a, b, buf, *, piece0, rows, tm, tn, tk, name, scale=None):
    t_dim, m = a.shape
    n = b.shape[1]
    db = tm // rows
    tpp = NDEV // db
    nk = t_dim // tk
    assert tm % rows == 0 and NDEV % db == 0 and m % tm == 0 and n % tn == 0 and t_dim % tk == 0
    grid = (m // tm, n // tn, nk)

    def body(a_ref, b_ref, buf_ref, o_ref, acc_ref):
        del buf_ref
        kk = pl.program_id(2)
        p = lax.dot_general(a_ref[...].astype(BF16), b_ref[...].astype(BF16), TN, preferred_element_type=F32)

        @pl.when(kk == 0)
        def _():
            acc_ref[...] = p

        @pl.when(kk > 0)
        def _():
            acc_ref[...] += p

        @pl.when(kk == nk - 1)
        def _():
            for d in range(db):
                v = acc_ref[d * rows:(d + 1) * rows, :]
                if scale is not None:
                    v = v * scale
                o_ref[0, d] = v.astype(BF16)

    est = 2 * (tk * tm * a.dtype.itemsize + tk * tn * b.dtype.itemsize + tm * tn * 2) + 3 * tm * tn * 4 + tk * tm * 4
    return _pc(
        body, name=name, grid=grid,
        in_specs=[_bs((tk, tm), lambda i, j, k: (k, i)), _bs((tk, tn), lambda i, j, k: (k, j)),
                  pl.BlockSpec(memory_space=pl.ANY)],
        out_specs=_bs((1, db, rows, tn), lambda i, j, k: (piece0 + i // tpp, i % tpp, 0, j)),
        out_shape=_sds(buf.shape, buf.dtype), scratch=[pltpu.VMEM((tm, tn), F32)], alias={2: 0},
        vmem=est + (8 << 20))(a, b, buf)


def _mm_drms(a, b, x, gain, dres, *, tm, b_row0, after, name):
    m, k_dim = a.shape
    assert m % tm == 0 and b_row0 % k_dim == 0 and b.shape[1] == D
    k_blk = b_row0 // k_dim
    after = jnp.zeros((8, 128), F32) if after is None else after

    def body(a_ref, b_ref, x_ref, g_ref, dr_ref, after_ref, dx_ref, dg_ref):
        del after_ref
        dh_v = jnp.dot(a_ref[...].astype(BF16), b_ref[...].astype(BF16), preferred_element_type=F32)
        xv = x_ref[...]
        r = lax.rsqrt(jnp.mean(xv * xv, axis=-1, keepdims=True) + EPS)
        dyg = dh_v * g_ref[...]
        mq = jnp.mean(dyg * xv, axis=-1, keepdims=True)
        dx_ref[...] = dr_ref[...] + r * (dyg - xv * (r * r * mq))

        @pl.when(pl.program_id(0) == 0)
        def _():
            dg_ref[...] = jnp.zeros_like(dg_ref)

        dg_ref[...] += jnp.sum(dh_v * (xv * r), axis=0, keepdims=True)

    row = _bs((tm, D), lambda i: (i, 0))
    est = 2 * (tm * k_dim * a.dtype.itemsize + k_dim * D * b.dtype.itemsize + 3 * tm * D * 4) + 4 * tm * D * 4
    return _pc(body, name=name, grid=(m // tm,),
               in_specs=[_bs((tm, k_dim), lambda i: (i, 0)),
                         pl.BlockSpec((k_dim, D), lambda i: (k_blk, 0), pipeline_mode=pl.Buffered(1)), row,
                         _bs((1, D), lambda i: (0, 0)), row, pl.BlockSpec(memory_space=pl.ANY)],
               out_specs=[row, _bs((1, D), lambda i: (0, 0))], out_shape=[_sds((m, D), F32), _sds((1, D), F32)],
               vmem=est + (8 << 20))(a, b, x, gain, dres, after)


def _dact_dswiglu(dout, b, g_pre, u_pre, *, tm, b_row0, after, name):
    t_dim = dout.shape[0]
    assert t_dim % tm == 0 and b_row0 % FF == 0
    k_blk = b_row0 // FF
    after = jnp.zeros((8, 128), F32) if after is None else after

    def body(d_ref, b_ref, g_ref, u_ref, after_ref, o_ref):
        del after_ref
        dav = 0.5 * lax.dot_general(d_ref[...].astype(BF16), b_ref[...].astype(BF16), NT, preferred_element_type=F32)
        g = g_ref[...].astype(F32)
        u = u_ref[...].astype(F32)
        s = _sigmoid(g)
        o_ref[:, :FF] = (dav * u * (s * (1.0 + g * (1.0 - s)))).astype(BF16)
        o_ref[:, FF:] = (dav * (g * s)).astype(BF16)

    est = 2 * (tm * D * 4 + FF * D * 2 + 2 * tm * 2 * FF * 2) + 5 * tm * FF * 4
    return _pc(body, name=name, grid=(t_dim // tm,),
               in_specs=[_bs((tm, D), lambda i: (i, 0)), _bs((FF, D), lambda i: (k_blk, 0)),
                         _bs((tm, FF), lambda i: (i, 0)), _bs((tm, FF), lambda i: (i, 0)), pl.BlockSpec(memory_space=pl.ANY)],
               out_specs=_bs((tm, 2 * FF), lambda i: (i, 0)), out_shape=_sds((t_dim, 2 * FF), BF16),
               vmem=est + (8 << 20))(dout, b, g_pre, u_pre, after)


def _rms_fwd(x, gain, name, tm=512):
    t_dim = x.shape[0]

    def body(x_ref, g_ref, o_ref):
        xv = x_ref[...]
        r = lax.rsqrt(jnp.mean(xv * xv, axis=-1, keepdims=True) + EPS)
        o_ref[...] = (xv * r * g_ref[...]).astype(BF16)

    return _pc(body, name=name, grid=(t_dim // tm,),
               in_specs=[_bs((tm, D), lambda i: (i, 0)), _bs((1, D), lambda i: (0, 0))],
               out_specs=_bs((tm, D), lambda i: (i, 0)), out_shape=_sds((t_dim, D), BF16))(x, gain)


def _sigmoid(v):
    return 1.0 / (1.0 + jnp.exp(-v))


def _gu_swiglu(h, b, *, tm, tn, b_row0, name):
    t_dim = h.shape[0]
    assert t_dim % tm == 0 and FF % tn == 0 and b_row0 % tn == 0
    j0 = b_row0 // tn

    def body(h_ref, bg_ref, bu_ref, g_ref, u_ref, a_ref):
        hv = h_ref[...]
        g16 = lax.dot_general(hv, bg_ref[...], NT, preferred_element_type=F32).astype(BF16)
        u16 = lax.dot_general(hv, bu_ref[...], NT, preferred_element_type=F32).astype(BF16)
        g = g16.astype(F32)
        g_ref[...] = g16
        u_ref[...] = u16
        a_ref[...] = (g * _sigmoid(g) * u16.astype(F32)).astype(BF16)

    tile = _bs((tm, tn), lambda i, j: (i, j))
    est = 2 * (tm * D * 2 + 2 * tn * D * 2 + 3 * tm * tn * 2) + 5 * tm * tn * 4
    return _pc(body, name=name, grid=(t_dim // tm, FF // tn),
               in_specs=[_bs((tm, D), lambda i, j: (i, 0)), _bs((tn, D), lambda i, j: (j0 + j, 0)),
                         _bs((tn, D), lambda i, j: (j0 + FF // tn + j, 0))],
               out_specs=[tile] * 3, out_shape=[_sds((t_dim, FF), BF16)] * 3, vmem=est + (8 << 20))(h, b, b)


def _final_loss(x, gain, tgt, name, tm=512):
    t_dim = x.shape[0]

    def body(x_ref, g_ref, t_ref, loss_ref, dx_ref, dg_ref):
        xv = x_ref[...]
        gv = g_ref[...]
        r = lax.rsqrt(jnp.mean(xv * xv, axis=-1, keepdims=True) + EPS)
        xn = xv * r
        err = xn * gv - t_ref[...]
        dy = err * (1.0 / D)
        dyg = dy * gv
        mq = jnp.mean(dyg * xv, axis=-1, keepdims=True)
        dx_ref[...] = r * (dyg - xv * (r * r * mq))

        @pl.when(pl.program_id(0) == 0)
        def _():
            dg_ref[...] = jnp.zeros_like(dg_ref)
            loss_ref[...] = jnp.zeros_like(loss_ref)

        dg_ref[...] += jnp.sum(dy * xn, axis=0, keepdims=True)
        part = 0.5 * jnp.sum(jnp.mean(err * err, axis=-1, keepdims=True), axis=0, keepdims=True)
        loss_ref[...] += jnp.broadcast_to(part, loss_ref.shape)

    return _pc(body, name=name, grid=(t_dim // tm,),
               in_specs=[_bs((tm, D), lambda i: (i, 0)), _bs((1, D), lambda i: (0, 0)), _bs((tm, D), lambda i: (i, 0))],
               out_specs=[_bs((8, 128), lambda i: (0, 0)), _bs((tm, D), lambda i: (i, 0)), _bs((1, D), lambda i: (0, 0))],
               out_shape=[_sds((8, 128), F32), _sds((t_dim, D), F32), _sds((1, D), F32)])(x, gain, tgt)


def _band_mask(n):
    ri = lax.broadcasted_iota(jnp.int32, (BLK, 2 * BLK), 0)
    cj = lax.broadcasted_iota(jnp.int32, (BLK, 2 * BLK), 1)
    dist = ri + BLK - cj
    return (dist >= 0) & (dist <= BLK) & ((cj >= BLK) | (n > 0))


def _halves(unit, imap_rows, col):
    return [_bs((unit, 128), (lambda u, j=j: (imap_rows(u), 2 * col + j))) for j in range(2)]


def _get_rows(pair, r, dil):
    return jnp.concatenate([h[pl.ds(r, BLK, stride=dil), :] for h in pair], axis=1)


def _put_rows(stage, r, dil, val):
    for j in range(2):
        stage[j, pl.ds(r, BLK, stride=dil), :] = val[:, 128 * j:128 * (j + 1)]


def _unstage(stage, out_ref):
    out_ref[:, :128] = stage[0]
    out_ref[:, 128:] = stage[1]


def _attn_fwd(z, acc, ml, *, gi, first, final, name):
    dil = DILS[gi]
    t_dim = z.shape[0]
    nsub = 4 if dil == 1 else 1
    unit = BLK * dil * nsub
    nu = t_dim // unit
    own = lambda u: u
    prev = lambda u: jnp.maximum(u - 1, 0)
    in_specs = (_halves(unit, own, Q_B + gi) + _halves(unit, own, K_B + gi) + _halves(unit, prev, K_B + gi)
                + _halves(unit, own, V_B + gi) + _halves(unit, prev, V_B + gi))
    args = [z] * 10
    st_ml = _bs((unit, 128), lambda u: (u, 0))
    if not first:
        in_specs += _halves(unit, own, 0) + [st_ml]
        args += [acc, acc, ml]

    def body(*refs):
        q2, ko2, kp2, vo2, vp2 = (refs[2 * i:2 * i + 2] for i in range(5))
        o_acc_ref, o_ml_ref, stage = refs[-3], refs[-2], refs[-1]
        lane = lax.broadcasted_iota(jnp.int32, (BLK, GW), 1)
        lane_s = lax.broadcasted_iota(jnp.int32, (BLK, 128), 1)

        def band(rows_at, prev_rows_at, store, valid):
            q = rows_at(q2) * QK_SCALE
            kb = jnp.concatenate([prev_rows_at(kp2, ko2), rows_at(ko2)], axis=0).astype(BF16)
            vb = jnp.concatenate([prev_rows_at(vp2, vo2), rows_at(vo2)], axis=0).astype(BF16)
            if first:
                acc_v = jnp.zeros((BLK, GW), F32)
                ml_v = jnp.where(lane_s < 64, NEG, 0.0).astype(F32)
            else:
                acc_v = rows_at(refs[10:12])
                ml_v = rows_at(refs[12:13])
            heads = [(lane >= h * HD) & (lane < (h + 1) * HD) for h in range(NH)]
            rows = lambda v, h: v[h * BLK:(h + 1) * BLK]
            qs = jnp.concatenate([jnp.where(hm, q, 0.0) for hm in heads], axis=0).astype(BF16)
            s = lax.dot_general(qs, kb, NT, preferred_element_type=F32)
            s = jnp.where(jnp.concatenate([valid] * NH, axis=0), s, NEG)
            m_old = jnp.concatenate([ml_v[:, 16 * h:16 * h + 1] for h in range(NH)], axis=0)
            l_old = jnp.concatenate([ml_v[:, 64 + 16 * h:64 + 16 * h + 1] for h in range(NH)], axis=0)
            m_new = jnp.maximum(m_old, jnp.max(s, axis=-1, keepdims=True))
            alpha = jnp.exp(m_old - m_new)
            p = jnp.exp(s - m_new)
            l_new = alpha * l_old + jnp.sum(p, axis=-1, keepdims=True)
            pv = jnp.dot(p.astype(BF16), vb, preferred_element_type=F32)
            stat = m_new + jnp.log(l_new) if final else m_new
            acc_o, ml_o = acc_v, ml_v
            l_b = jnp.ones((BLK, GW), F32)
            for h, hm in enumerate(heads):
                acc_o = jnp.where(hm, acc_v * rows(alpha, h) + rows(pv, h), acc_o)
                l_b = jnp.where(hm, rows(l_new, h), l_b)
                ml_o = jnp.where((lane_s >= 16 * h) & (lane_s < 16 * h + 16), rows(stat, h), ml_o)
                if not final:
                    ml_o = jnp.where((lane_s >= 64 + 16 * h) & (lane_s < 64 + 16 * h + 16), rows(l_new, h), ml_o)
            store(acc_o / l_b if final else acc_o, ml_o)

        u = pl.program_id(0)
        if dil == 1:
            def block(b):
                return lambda pair: jnp.concatenate([h[pl.ds(b * BLK, BLK), :] for h in pair], axis=1)

            for b in range(nsub):
                def store(acc_val, ml_val, b=b):
                    for j in range(2):
                        stage[j, pl.ds(b * BLK, BLK), :] = acc_val[:, 128 * j:128 * (j + 1)]
                    o_ml_ref[pl.ds(b * BLK, BLK), :] = ml_val

                before = (lambda prev_pair, own_pair, b=b: block(b - 1)(own_pair)) if b else (
                    lambda prev_pair, own_pair: block(nsub - 1)(prev_pair))
                band(block(b), before, store, _band_mask(1 if b else u))
        else:
            valid = _band_mask(u)

            def step(r, carry):
                def store(acc_val, ml_val):
                    _put_rows(stage, r, dil, acc_val)
                    o_ml_ref[pl.ds(r, BLK, stride=dil), :] = ml_val

                band(lambda pair: _get_rows(pair, r, dil), lambda prev_pair, own_pair: _get_rows(prev_pair, r, dil), store, valid)
                return carry

            lax.fori_loop(0, dil, step, 0, unroll=min(dil, 8))
        _unstage(stage, o_acc_ref)

    blocks = (len(in_specs) + 6) * unit * 128 * 4
    return _pc(body, name=name, grid=(nu,), in_specs=in_specs, out_specs=[_bs((unit, GW), lambda u: (u, 0)), st_ml],
               out_shape=[_sds((t_dim, GW), F32), _sds((t_dim, 128), F32)],
               scratch=[pltpu.VMEM((2, unit, 128), F32)], vmem=2 * blocks + (12 << 20))(*args)


def _attn_bwd(z, dy, y, lse, *, gi, name):
    dil = DILS[gi]
    t_dim = z.shape[0]
    nsub = 4 if dil == 1 else 1
    unit = BLK * dil * nsub
    nu = t_dim // unit

    own = lambda u: jnp.minimum(u, nu - 1)
    prev = lambda u: jnp.maximum(jnp.minimum(u, nu - 1) - 1, 0)
    in_specs = (_halves(unit, own, Q_B + gi) + _halves(unit, own, K_B + gi) + _halves(unit, prev, K_B + gi)
                + _halves(unit, own, V_B + gi) + _halves(unit, prev, V_B + gi)
                + _halves(unit, own, 0) + _halves(unit, own, 0) + [_bs((unit, 128), lambda u: (own(u), 0))])
    tok = _bs((unit, GW), lambda u: (own(u), 0))
    kv_out = _bs((unit, GW), lambda u: (jnp.maximum(u - 1, 0), 0))

    def body(*refs):
        q2, ko2, kp2, vo2, vp2, dy2, y2 = (refs[2 * i:2 * i + 2] for i in range(7))
        lse_ref = refs[14]
        dq_ref, dk_ref, dv_ref, dq_st, dk_st, dv_st, dkc, dvc = refs[15:]
        u = pl.program_id(0)

        @pl.when(u == 0)
        def _():
            dkc[...] = jnp.zeros_like(dkc)
            dvc[...] = jnp.zeros_like(dvc)

        @pl.when(u < nu)
        def _():
            lane = lax.broadcasted_iota(jnp.int32, (BLK, GW), 1)

            def band_grads(rows_at, prev_rows_at, valid):
                q = rows_at(q2) * QK_SCALE
                kb = jnp.concatenate([prev_rows_at(kp2, ko2), rows_at(ko2)], axis=0).astype(BF16)
                vb = jnp.concatenate([prev_rows_at(vp2, vo2), rows_at(vo2)], axis=0).astype(BF16)
                dyv = rows_at(dy2)
                yv = rows_at(y2)
                lse_v = rows_at(refs[14:15])
                heads = [(lane >= h * HD) & (lane < (h + 1) * HD) for h in range(NH)]
                qs = jnp.concatenate([jnp.where(hm, q, 0.0) for hm in heads], axis=0).astype(BF16)
                dys = jnp.concatenate([jnp.where(hm, dyv, 0.0) for hm in heads], axis=0)
                lse_s = jnp.concatenate([lse_v[:, 16 * h:16 * h + 1] for h in range(NH)], axis=0)
                s = lax.dot_general(qs, kb, NT, preferred_element_type=F32)
                p = jnp.where(jnp.concatenate([valid] * NH, axis=0), jnp.exp(s - lse_s), 0.0)
                delta = jnp.sum(dys * jnp.concatenate([yv] * NH, axis=0), axis=-1, keepdims=True)
                dys16 = dys.astype(BF16)
                dp = lax.dot_general(dys16, vb, NT, preferred_element_type=F32)
                ds = (p * (dp - delta)).astype(BF16)
                dq_all = jnp.dot(ds, kb, preferred_element_type=F32)
                dq = jnp.zeros((BLK, GW), F32)
                for h, hm in enumerate(heads):
                    dq = jnp.where(hm, dq_all[h * BLK:(h + 1) * BLK], dq)
                dkb = lax.dot_general(ds, qs, TN, preferred_element_type=F32)
                dvb = lax.dot_general(p.astype(BF16), dys16, TN, preferred_element_type=F32)
                return dq * QK_SCALE, dkb, dvb

            if dil == 1:
                def block(b):
                    return lambda pair: jnp.concatenate([h[pl.ds(b * BLK, BLK), :] for h in pair], axis=1)

                def put(stage, b, val, add=False):
                    for j in range(2):
                        part = val[:, 128 * j:128 * (j + 1)]
                        rows_j = (j, pl.ds(b * BLK, BLK), slice(None))
                        stage[rows_j] = stage[rows_j] + part if add else part

                for b in range(nsub):
                    before = (lambda prev_pair, own_pair, b=b: block(b - 1)(own_pair)) if b else (
                        lambda prev_pair, own_pair: block(nsub - 1)(prev_pair))
                    dq, dkb, dvb = band_grads(block(b), before, _band_mask(1 if b else u))
                    put(dq_st, b, dq)
                    if b == 0:
                        put(dkc, nsub - 1, dkb[:BLK], add=True)
                        put(dvc, nsub - 1, dvb[:BLK], add=True)

                        @pl.when(u > 0)
                        def _():
                            _unstage(dkc, dk_ref)
                            _unstage(dvc, dv_ref)
                    else:
                        put(dk_st, b - 1, dkb[:BLK], add=True)
                        put(dv_st, b - 1, dvb[:BLK], add=True)
                    put(dk_st, b, dkb[BLK:])
                    put(dv_st, b, dvb[BLK:])
                _unstage(dq_st, dq_ref)
                dkc[...] = dk_st[...]
                dvc[...] = dv_st[...]
            else:
                valid = _band_mask(u)

                def step(r, carry):
                    dq, dkb, dvb = band_grads(lambda pair: _get_rows(pair, r, dil),
                                              lambda prev_pair, own_pair: _get_rows(prev_pair, r, dil), valid)
                    _put_rows(dq_st, r, dil, dq)
                    _put_rows(dk_st, r, dil, _get_rows((dkc.at[0], dkc.at[1]), r, dil) + dkb[:BLK])
                    _put_rows(dv_st, r, dil, _get_rows((dvc.at[0], dvc.at[1]), r, dil) + dvb[:BLK])
                    _put_rows(dkc, r, dil, dkb[BLK:])
                    _put_rows(dvc, r, dil, dvb[BLK:])
                    return carry

                lax.fori_loop(0, dil, step, 0, unroll=min(dil, 8))
                _unstage(dq_st, dq_ref)

                @pl.when(u > 0)
                def _():
                    _unstage(dk_st, dk_ref)
                    _unstage(dv_st, dv_ref)

        @pl.when(u == nu)
        def _():
            _unstage(dkc, dk_ref)
            _unstage(dvc, dv_ref)

    blocks = (15 + 6 + 5) * unit * 128 * 4 * 2
    return _pc(body, name=name, grid=(nu + 1,), in_specs=in_specs, out_specs=[tok, kv_out, kv_out],
               out_shape=[_sds((t_dim, GW), F32)] * 3, scratch=[pltpu.VMEM((2, unit, 128), F32)] * 5,
               vmem=blocks + (12 << 20))(*([z] * 10), dy, dy, y, y, lse)


_INV_SQRT2 = 1.0 / math.sqrt(2.0)
_INV_SQRT2PI = 1.0 / math.sqrt(2.0 * math.pi)


def _gelu(v):
    return 0.5 * v * (1.0 + lax.erf(v * _INV_SQRT2))


def _gelu_grad(v):
    return 0.5 * (1.0 + lax.erf(v * _INV_SQRT2)) + v * (_INV_SQRT2PI * jnp.exp(-0.5 * v * v))


def _gmlp_core(z_refs, lng, lnb, ws_ref, bs_ref):
    u_pre = jnp.concatenate([z_refs[0][...], z_refs[1][...]], axis=1).astype(F32)
    v_pre = jnp.concatenate([z_refs[2][...], z_refs[3][...]], axis=1).astype(F32)
    u = _gelu(u_pre)
    v = _gelu(v_pre)
    xc = v - jnp.mean(v, axis=-1, keepdims=True)
    rstd = lax.rsqrt(jnp.mean(xc * xc, axis=-1, keepdims=True) + EPS)
    xhat = xc * rstd
    vn = xhat * lng + lnb
    ti = lax.broadcasted_iota(jnp.int32, (BLK, BLK), 0)
    si = lax.broadcasted_iota(jnp.int32, (BLK, BLK), 1)
    causal = ti >= si
    vn16 = vn.astype(BF16)
    ws = [jnp.where(causal, ws_ref[g], 0.0).astype(BF16) for g in range(4)]
    mixed = jnp.concatenate(
        [jnp.dot(ws[g], vn16[:, g * BLK:(g + 1) * BLK], preferred_element_type=F32) + bs_ref[g] for g in range(4)], axis=1)
    return u_pre, v_pre, u, xhat, rstd, vn16, ws, causal, mixed


def _gmlp_specs():
    z_specs = [_bs((BLK, GW), (lambda i, c=c: (i, UV_B + c))) for c in range(4)]
    par_specs = [_bs((1, BW), lambda i: (0, 0)), _bs((1, BW), lambda i: (0, 0)),
                 _bs((4, BLK, BLK), lambda i: (0, 0, 0)), _bs((4, BLK, 1), lambda i: (0, 0, 0))]
    return z_specs, par_specs


def _gmlp_fwd(z, lng, lnb, ws, bs, name):
    t_dim = z.shape[0]
    z_specs, par_specs = _gmlp_specs()

    def body(z0, z1, z2, z3, lng_ref, lnb_ref, ws_ref, bs_ref, o_ref):
        core = _gmlp_core((z0, z1, z2, z3), lng_ref[...], lnb_ref[...], ws_ref, bs_ref)
        o_ref[...] = (core[2] * core[8]).astype(BF16)

    return _pc(body, name=name, grid=(t_dim // BLK,), in_specs=z_specs + par_specs,
               out_specs=_bs((BLK, BW), lambda i: (i, 0)), out_shape=_sds((t_dim, BW), BF16))(z, z, z, z, lng, lnb, ws, bs)


def _gmlp_bwd(z, dy, lng, lnb, ws, bs, name):
    t_dim = z.shape[0]
    z_specs, par_specs = _gmlp_specs()

    def body(z0, z1, z2, z3, lng_ref, lnb_ref, ws_ref, bs_ref, dy_ref, duv_ref, dws_ref, dbs_ref, dlng_ref, dlnb_ref):
        lng_v = lng_ref[...]
        u_pre, v_pre, u, xhat, rstd, vn16, wsm, causal, mixed = _gmlp_core((z0, z1, z2, z3), lng_v, lnb_ref[...], ws_ref, bs_ref)
        dyv = dy_ref[...].astype(F32)
        du = dyv * mixed
        dmixed = dyv * u

        @pl.when(pl.program_id(0) == 0)
        def _():
            dws_ref[...] = jnp.zeros_like(dws_ref)
            dbs_ref[...] = jnp.zeros_like(dbs_ref)
            dlng_ref[...] = jnp.zeros_like(dlng_ref)
            dlnb_ref[...] = jnp.zeros_like(dlnb_ref)

        dvn_parts = []
        for g in range(4):
            dm = dmixed[:, g * BLK:(g + 1) * BLK]
            dm16 = dm.astype(BF16)
            dw = lax.dot_general(dm16, vn16[:, g * BLK:(g + 1) * BLK], NT, preferred_element_type=F32)
            dws_ref[g] += jnp.where(causal, dw, 0.0)
            dbs_ref[g] += jnp.sum(dm.T, axis=0, keepdims=True)
            dvn_parts.append(lax.dot_general(wsm[g], dm16, TN, preferred_element_type=F32))
        dvn = jnp.concatenate(dvn_parts, axis=1)
        dlng_ref[...] += jnp.sum(dvn * xhat, axis=0, keepdims=True)
        dlnb_ref[...] += jnp.sum(dvn, axis=0, keepdims=True)
        dxh = dvn * lng_v
        dv = rstd * (dxh - jnp.mean(dxh, axis=-1, keepdims=True) - xhat * jnp.mean(dxh * xhat, axis=-1, keepdims=True))
        duv_ref[:, :BW] = (du * _gelu_grad(u_pre)).astype(BF16)
        duv_ref[:, BW:] = (dv * _gelu_grad(v_pre)).astype(BF16)

    return _pc(body, name=name, grid=(t_dim // BLK,),
               in_specs=z_specs + par_specs + [_bs((BLK, BW), lambda i: (i, 0))],
               out_specs=[_bs((BLK, 2 * BW), lambda i: (i, 0)), _bs((4, BLK, BLK), lambda i: (0, 0, 0)),
                          _bs((4, 1, BLK), lambda i: (0, 0, 0)), _bs((1, BW), lambda i: (0, 0)), _bs((1, BW), lambda i: (0, 0))],
               out_shape=[_sds((t_dim, 2 * BW), BF16), _sds((4, BLK, BLK), F32), _sds((4, 1, BLK), F32),
                          _sds((1, BW), F32), _sds((1, BW), F32)])(z, z, z, z, lng, lnb, ws, bs, dy)


def _pool_core(x, gi):
    t_dim = x.shape[0]
    w = jnp.left_shift(2, gi)
    row = lax.broadcasted_iota(jnp.int32, x.shape, 0)
    s = x
    for k in (1, 2, 4, 8):
        sh = jnp.where(row >= k, pltpu.roll(s, k, 0), 0.0)
        s = jnp.where(k < w, s + sh, s)
    cnt = jnp.minimum(row + 1, w).astype(F32)
    return s / cnt - x, cnt, row, w, t_dim


def _pool_fwd(z, cw, cs, name):
    t_dim = z.shape[0]

    def body(z_ref, cw_ref, cs_ref, o_ref):
        pooled = _pool_core(z_ref[...].astype(F32), pl.program_id(0))[0]
        y = jnp.dot(pooled.astype(BF16), cw_ref[0].astype(BF16), preferred_element_type=F32) * cs_ref[0]
        o_ref[...] = y.astype(BF16)

    return _pc(body, name=name, grid=(4,),
               in_specs=[_bs((t_dim, BLK), lambda g: (0, C_B128 + g)), _bs((1, BLK, BLK), lambda g: (g, 0, 0)),
                         _bs((1, 1, BLK), lambda g: (g, 0, 0))],
               out_specs=_bs((t_dim, BLK), lambda g: (0, g)), out_shape=_sds((t_dim, BW), BF16),
               vmem=40 << 20)(z, cw, cs)


def _pool_bwd(z, dy, cw, cs, name):
    t_dim = z.shape[0]

    def body(z_ref, dy_ref, cw_ref, cs_ref, dx_ref, dcw_ref, dcs_ref):
        pooled, cnt, row, w, _ = _pool_core(z_ref[...].astype(F32), pl.program_id(0))
        p16 = pooled.astype(BF16)
        cw16 = cw_ref[0].astype(BF16)
        dyv = dy_ref[...].astype(F32)
        lin = jnp.dot(p16, cw16, preferred_element_type=F32)
        dcs_ref[0] = jnp.sum(dyv * lin, axis=0, keepdims=True)
        dys = (dyv * cs_ref[0]).astype(BF16)
        dcw_ref[0] = lax.dot_general(p16, dys, TN, preferred_element_type=F32)
        dpool = lax.dot_general(dys, cw16, NT, preferred_element_type=F32)
        sb = dpool / cnt
        for k in (1, 2, 4, 8):
            sh = jnp.where(row < t_dim - k, pltpu.roll(sb, t_dim - k, 0), 0.0)
            sb = jnp.where(k < w, sb + sh, sb)
        dx_ref[...] = (sb - dpool).astype(BF16)

    return _pc(body, name=name, grid=(4,),
               in_specs=[_bs((t_dim, BLK), lambda g: (0, C_B128 + g)), _bs((t_dim, BLK), lambda g: (0, g)),
                         _bs((1, BLK, BLK), lambda g: (g, 0, 0)), _bs((1, 1, BLK), lambda g: (g, 0, 0))],
               out_specs=[_bs((t_dim, BLK), lambda g: (0, g)), _bs((1, BLK, BLK), lambda g: (g, 0, 0)),
                          _bs((1, 1, BLK), lambda g: (g, 0, 0))],
               out_shape=[_sds((t_dim, BW), BF16), _sds((4, BLK, BLK), F32), _sds((4, 1, BLK), F32)],
               vmem=48 << 20)(z, dy, cw, cs)


GATE_W = 2 * GW


def _gates_specs(tm):
    per_branch = D // GATE_W
    z_specs = [_bs((tm, GATE_W), (lambda j, i, b=b: (i, G_B * GW // GATE_W + per_branch * b + j))) for b in range(3)]
    bg_specs = [_bs((1, GATE_W), (lambda j, i, b=b: (0, per_branch * b + j))) for b in range(3)]
    tile = _bs((tm, GATE_W), lambda j, i: (i, j))
    return z_specs, bg_specs, tile


def _gates_fwd(z, bg, pa, pb, pc, name, tm=512):
    t_dim = z.shape[0]
    z_specs, bg_specs, tile = _gates_specs(tm)

    def body(za, zb, zc, ba, bb, bc, pa_ref, pb_ref, pc_ref, o_ref):
        o_ref[...] = (_sigmoid(za[...] + ba[...]) * pa_ref[...] + _sigmoid(zb[...] + bb[...]) * pb_ref[...]
                      + _sigmoid(zc[...] + bc[...]) * pc_ref[...]).astype(BF16)

    return _pc(body, name=name, grid=(D // GATE_W, t_dim // tm), in_specs=z_specs + bg_specs + [tile] * 3,
               out_specs=tile, out_shape=_sds((t_dim, D), BF16), vmem=40 << 20)(z, z, z, bg, bg, bg, pa, pb, pc)


def _gates_bwd(z, bg, pa, pb, pc, dm, name, tm=512):
    t_dim = z.shape[0]
    z_specs, bg_specs, tile = _gates_specs(tm)
    bias_out = _bs((1, GATE_W), lambda j, i: (0, j))

    def body(za, zb, zc, ba, bb, bc, pa_ref, pb_ref, pc_ref, dm_ref, dpa, dpb, dpc, dza, dzb, dzc, dba, dbb, dbc):
        dmv = dm_ref[...]
        first = pl.program_id(1) == 0
        for z_ref, b_ref, p_ref, dp_ref, dz_ref, db_ref in ((za, ba, pa_ref, dpa, dza, dba), (zb, bb, pb_ref, dpb, dzb, dbb),
                                                            (zc, bc, pc_ref, dpc, dzc, dbc)):
            gate = _sigmoid(z_ref[...] + b_ref[...])
            dp_ref[...] = (gate * dmv).astype(BF16)
            dz = dmv * p_ref[...] * (gate * (1.0 - gate))
            dz_ref[...] = dz.astype(BF16)

            @pl.when(first)
            def _():
                db_ref[...] = jnp.zeros_like(db_ref)

            db_ref[...] += jnp.sum(dz, axis=0, keepdims=True)

    return _pc(body, name=name, grid=(D // GATE_W, t_dim // tm), in_specs=z_specs + bg_specs + [tile] * 4,
               out_specs=[tile] * 6 + [bias_out] * 3,
               out_shape=[_sds((t_dim, D), BF16)] * 6 + [_sds((1, D), F32)] * 3, vmem=48 << 20)(
                   z, z, z, bg, bg, bg, pa, pb, pc, dm)


def _concat_bf16(parts, name, tm=512):
    t_dim = parts[0].shape[0]
    offs = [sum(p.shape[1] for p in parts[:i]) for i in range(len(parts) + 1)]

    def body(*refs):
        o_ref = refs[-1]
        for ref, off in zip(refs[:-1], offs):
            o_ref[:, off:off + ref.shape[1]] = ref[...].astype(BF16)

    return _pc(body, name=name, grid=(t_dim // tm,), in_specs=[_bs((tm, p.shape[1]), lambda i: (i, 0)) for p in parts],
               out_specs=_bs((tm, offs[-1]), lambda i: (i, 0)), out_shape=_sds((t_dim, offs[-1]), BF16),
               vmem=48 << 20)(*parts)


def _coords():
    return lax.axis_index("x"), lax.axis_index("y"), lax.axis_index("c")


def _peer(k, x, y, c):
    px = 1 - x if k & 4 else x
    py = 1 - y if k & 2 else y
    pc = 1 - c if k & 1 else c
    return (px, py, pc), 4 * px + 2 * py + pc


def _exchange_copies(kind, bufs, send, recv):
    x, y, c = _coords()
    me = 4 * x + 2 * y + c
    na = len(bufs) if kind == "gather" else len(bufs) // 2
    out = []
    for k in range(1, NDEV):
        to, peer = _peer(k, x, y, c)
        for a in range(na):
            sems = dict(send_sem=send.at[a * 7 + k - 1], recv_sem=recv.at[a * 7 + k - 1], device_id=to, device_id_type=MESH)
            if kind == "gather":
                mine = pltpu.make_async_remote_copy(src_ref=bufs[a].at[:, me], dst_ref=bufs[a].at[:, me], **sems)
                theirs = pltpu.make_async_remote_copy(src_ref=bufs[a].at[:, me], dst_ref=bufs[a].at[:, peer], **sems)
            else:
                src, land = bufs[a], bufs[na + a]
                mine = pltpu.make_async_remote_copy(src_ref=src.at[:, peer], dst_ref=land.at[me], **sems)
                theirs = pltpu.make_async_remote_copy(src_ref=src.at[:, peer], dst_ref=land.at[peer], **sems)
            out.append((mine, theirs))
    return out


_HBM = pl.BlockSpec(memory_space=pltpu.HBM)
_SEM = pl.BlockSpec(memory_space=pltpu.SEMAPHORE)
_EFFECT = pltpu.SideEffectType.DATAFLOW_SIDE_EFFECTING


def _exchange_start(kind, arrays, after, name):
    na = len(arrays)
    if kind == "scatter":
        arrays = list(arrays) + [lax.empty((NDEV, s.shape[0]) + s.shape[2:], s.dtype) for s in arrays]
    nb = len(arrays)

    def body(*refs):
        bufs = refs[:nb]
        send, recv = refs[nb + 1], refs[nb + 2]
        token = refs[-1]
        for mine, _ in _exchange_copies(kind, bufs, send, recv):
            mine.start()
        token[...] = jnp.zeros_like(token)

    arrays = [pltpu.with_memory_space_constraint(b, pltpu.HBM) for b in arrays]
    outs = pl.pallas_call(
        body, name=name,
        out_shape=(pltpu.SemaphoreType.DMA((7 * na,)), pltpu.SemaphoreType.DMA((7 * na,)),
                   *[pltpu.HBM(b.shape, b.dtype) for b in arrays], _sds((8, 128), F32)),
        in_specs=[_HBM] * nb + [pl.BlockSpec(memory_space=pl.ANY)],
        out_specs=(_SEM, _SEM, *([_HBM] * nb), pl.BlockSpec(memory_space=pltpu.VMEM)),
        input_output_aliases={i: 2 + i for i in range(nb)},
        compiler_params=pltpu.CompilerParams(has_side_effects=_EFFECT), interpret=False)(*arrays, after)
    return outs[0], outs[1], list(outs[2:2 + nb]), outs[-1]


def _exchange_wait(kind, started, after, name):
    send, recv, arrays, _ = started
    nb = len(arrays)

    def body(*refs):
        for mine, theirs in _exchange_copies(kind, refs[:nb], refs[nb], refs[nb + 1]):
            mine.wait_send()
            theirs.wait_recv()

    outs = pl.pallas_call(
        body, name=name,
        out_shape=tuple(pltpu.HBM(b.shape, b.dtype) for b in arrays),
        in_specs=[_HBM] * nb + [_SEM, _SEM, pl.BlockSpec(memory_space=pl.ANY)],
        out_specs=tuple([_HBM] * nb),
        input_output_aliases={i: i for i in range(nb)},
        compiler_params=pltpu.CompilerParams(has_side_effects=_EFFECT), interpret=False)(*arrays, send, recv, after)
    return list(outs)


def _adam_math(w, g, m, v):
    m2 = ADAM_B1 * m + (1.0 - ADAM_B1) * g
    v2 = ADAM_B2 * v + (1.0 - ADAM_B2) * jnp.square(g)
    m_hat = m2 / (1.0 - ADAM_B1 ** ADAM_STEP)
    v_hat = v2 / (1.0 - ADAM_B2 ** ADAM_STEP)
    return -ADAM_LR * (m_hat / (jnp.sqrt(v_hat) + ADAM_EPS) + ADAM_WD * w), m2, v2


_DIRECT = (1, 4, 2, 6)


def _gather_copies(bufs, send_d, recv_d, send_f=None, recv_f=None):
    x, y, c = _coords()
    me = 4 * x + 2 * y + c
    sibling, _ = _peer(1, x, y, c)
    direct, forward = [], []
    for a, buf in enumerate(bufs):
        for j, k in enumerate(_DIRECT):
            to, peer = _peer(k, x, y, c)
            sems = dict(send_sem=send_d.at[4 * a + j], recv_sem=recv_d.at[4 * a + j], device_id=to, device_id_type=MESH)
            direct.append((pltpu.make_async_remote_copy(src_ref=buf.at[:, me], dst_ref=buf.at[:, me], **sems),
                           pltpu.make_async_remote_copy(src_ref=buf.at[:, me], dst_ref=buf.at[:, peer], **sems)))
        if send_f is None:
            continue
        for j, k in enumerate(_DIRECT[1:]):
            _, near = _peer(k, x, y, c)
            _, far = _peer(k | 1, x, y, c)
            sems = dict(send_sem=send_f.at[3 * a + j], recv_sem=recv_f.at[3 * a + j], device_id=sibling, device_id_type=MESH)
            forward.append((pltpu.make_async_remote_copy(src_ref=buf.at[:, near], dst_ref=buf.at[:, near], **sems),
                            pltpu.make_async_remote_copy(src_ref=buf.at[:, near], dst_ref=buf.at[:, far], **sems)))
    return direct, forward


def _gather_start(lands, after, name):
    na = len(lands)

    def body(*refs):
        direct, _ = _gather_copies(refs[:na], refs[na + 1], refs[na + 2])
        for mine, _ in direct:
            mine.start()
        refs[-1][...] = jnp.zeros_like(refs[-1])

    lands = [pltpu.with_memory_space_constraint(b, pltpu.HBM) for b in lands]
    outs = pl.pallas_call(
        body, name=name,
        out_shape=(pltpu.SemaphoreType.DMA((4 * na,)), pltpu.SemaphoreType.DMA((4 * na,)),
                   *[pltpu.HBM(b.shape, b.dtype) for b in lands], _sds((8, 128), F32)),
        in_specs=[_HBM] * na + [pl.BlockSpec(memory_space=pl.ANY)],
        out_specs=(_SEM, _SEM, *([_HBM] * na), pl.BlockSpec(memory_space=pltpu.VMEM)),
        input_output_aliases={i: 2 + i for i in range(na)},
        compiler_params=pltpu.CompilerParams(has_side_effects=_EFFECT), interpret=False)(*lands, after)
    return outs[0], outs[1], list(outs[2:2 + na]), outs[-1]


def _gather_forward(started, after, name):
    send_d, recv_d, lands, _ = started
    na = len(lands)

    def body(*refs):
        direct, forward = _gather_copies(refs[:na], refs[na], refs[na + 1], refs[2 * na + 3], refs[2 * na + 4])
        for a in range(na):
            for j in range(3):
                direct[4 * a + 1 + j][1].wait_recv()
                forward[3 * a + j][0].start()
        refs[-1][...] = jnp.zeros_like(refs[-1])

    outs = pl.pallas_call(
        body, name=name,
        out_shape=(*[pltpu.HBM(b.shape, b.dtype) for b in lands], pltpu.SemaphoreType.DMA((3 * na,)),
                   pltpu.SemaphoreType.DMA((3 * na,)), _sds((8, 128), F32)),
        in_specs=[_HBM] * na + [_SEM, _SEM, pl.BlockSpec(memory_space=pl.ANY)],
        out_specs=(*([_HBM] * na), _SEM, _SEM, pl.BlockSpec(memory_space=pltpu.VMEM)),
        input_output_aliases={i: i for i in range(na)},
        compiler_params=pltpu.CompilerParams(has_side_effects=_EFFECT), interpret=False)(*lands, send_d, recv_d, after)
    return send_d, recv_d, outs[na], outs[na + 1], list(outs[:na]), outs[-1]


def _gather_wait(forwarded, after, name):
    send_d, recv_d, send_f, recv_f, lands, _ = forwarded
    na = len(lands)

    def body(*refs):
        direct, forward = _gather_copies(refs[:na], refs[na], refs[na + 1], refs[na + 2], refs[na + 3])
        for a in range(na):
            for j in range(4):
                direct[4 * a + j][0].wait_send()
            direct[4 * a][1].wait_recv()
            for j in range(3):
                forward[3 * a + j][0].wait_send()
                forward[3 * a + j][1].wait_recv()

    outs = pl.pallas_call(
        body, name=name, out_shape=tuple(pltpu.HBM(b.shape, b.dtype) for b in lands),
        in_specs=[_HBM] * na + [_SEM] * 4 + [pl.BlockSpec(memory_space=pl.ANY)], out_specs=tuple([_HBM] * na),
        input_output_aliases={i: i for i in range(na)},
        compiler_params=pltpu.CompilerParams(has_side_effects=_EFFECT), interpret=False)(
            *lands, send_d, recv_d, send_f, recv_f, after)
    return list(outs)


def _reduce_update(landed, own, me, piece, w, m, v, layer, prev, transposed, name):
    rows, cols = landed.shape[2], landed.shape[3]
    tc = min(cols, 512 if rows <= FFS else 256)
    assert cols % tc == 0 and w.shape[1:] == ((cols, rows) if transposed else (rows, cols))
    lane_chunks = [(c0, min(128, rows - c0)) for c0 in range(0, rows, 128)]

    def body(me_ref, b_ref, own_ref, w_ref, m_ref, v_ref, *rest):
        outs = rest[-4:]
        mine = me_ref[0]
        s = None
        for d in range(NDEV):
            other = jnp.where(mine == d, (d + 1) % NDEV, d)
            part = jnp.where(mine == d, own_ref[0, 0], b_ref[other, 0]).astype(F32)
            s = part if s is None else s + part
        if not transposed:
            for o, val in zip(outs, (s,) + _adam_math(w_ref[0], s, m_ref[0], v_ref[0])):
                o[0] = val
            return
        for c0, wd in lane_chunks:
            chunk = s[c0:c0 + wd]
            if wd < 128:
                chunk = jnp.concatenate([chunk, jnp.zeros((128 - wd, tc), F32)], axis=0)
            g = chunk.T[:, :wd]
            sl = (0, slice(None), slice(c0, c0 + wd))
            for o, val in zip(outs, (g,) + _adam_math(w_ref[sl], g, m_ref[sl], v_ref[sl])):
                o[sl] = val

    if transposed:
        nat = _bs((1, tc, rows), lambda j, me_ref: (layer, j, 0))
    else:
        nat = _bs((1, rows, tc), lambda j, me_ref: (layer, 0, j))
    in_specs = [_bs((NDEV, 1, rows, tc), lambda j, me_ref: (0, piece, 0, j)),
                _bs((1, 1, rows, tc), lambda j, me_ref: (piece, me_ref[0], 0, j)), nat, nat, nat]
    args = [me, landed, own, w, m, v]
    alias = {}
    if prev is not None:
        in_specs += [pl.BlockSpec(memory_space=pl.ANY)] * 4
        args += list(prev)
        alias = {6 + i: i for i in range(4)}
    grid_spec = pltpu.PrefetchScalarGridSpec(num_scalar_prefetch=1, grid=(cols // tc,), in_specs=in_specs, out_specs=[nat] * 4)
    return pl.pallas_call(body, name=name, grid_spec=grid_spec, out_shape=[_sds(w.shape, F32)] * 4,
                          input_output_aliases=alias, interpret=False)(*args)


SMALL_LAYER = ("ffn1_norm", "mix_norm", "b_gate", "b_ln_g", "b_ln_b", "b_w_s", "b_b_s", "c_w", "c_scale", "ffn2_norm")
SMALL_ROWS = {"ffn1_norm": 8, "mix_norm": 8, "b_gate": 24, "b_ln_g": 4, "b_ln_b": 4, "b_w_s": 512, "b_b_s": 4, "c_w": 512,
              "c_scale": 4, "ffn2_norm": 8, "final_norm": 8}
LAYER_ROWS = sum(SMALL_ROWS[k] for k in SMALL_LAYER)


def _small_offset(name):
    if name == "final_norm":
        return LAYER_ROWS
    return sum(SMALL_ROWS[k] for k in SMALL_LAYER[:SMALL_LAYER.index(name)])


def _pack_small_grads(layer_grads, final, me, name):
    flat, where = [], []
    for k in SMALL_LAYER:
        parts = layer_grads[k] if k == "b_gate" else [layer_grads[k]]
        for i, part in enumerate(parts):
            flat.append(part)
            where.append(_small_offset(k) + i * (D // 128))
    if final is not None:
        flat.append(final)
        where.append(_small_offset("final_norm"))
    pack_rows = LAYER_ROWS + (SMALL_ROWS["final_norm"] if final is not None else 0)

    def body(me_ref, *refs):
        del me_ref
        o_ref = refs[-1]
        for ref, off in zip(refs[:-1], where):
            if len(ref.shape) == 2:
                for j in range(ref.shape[1] // 128):
                    o_ref[0, 0, off + j:off + j + 1, :] = ref[:, 128 * j:128 * (j + 1)]
            else:
                rows = ref.shape[1]
                for g in range(4):
                    o_ref[0, 0, off + g * rows:off + (g + 1) * rows, :] = ref[g]

    grid_spec = pltpu.PrefetchScalarGridSpec(
        num_scalar_prefetch=1, grid=(1,),
        in_specs=[pl.BlockSpec(a.shape, (lambda i, me_ref, nd=a.ndim: (0,) * nd)) for a in flat],
        out_specs=_bs((1, 1, pack_rows, 128), lambda i, me_ref: (0, me_ref[0], 0, 0)))
    return pl.pallas_call(body, name=name, grid_spec=grid_spec, out_shape=_sds((1, NDEV, pack_rows, 128), F32),
                          interpret=False)(me, *flat)


def _adamw_small(gathered, w, mom, var, name):
    names = SMALL_LAYER + ("final_norm",)
    ins = list(gathered) + [t[k] for k in names for t in (w, mom, var)]

    def body(*refs):
        outs = refs[len(ins):]

        def rows(l, off, n):
            land = refs[l]
            s = land[0, 0, off:off + n, :]
            for d in range(1, NDEV):
                s = s + land[0, d, off:off + n, :]
            return s

        def as_row(l, off, n):
            s = rows(l, off, n)
            return jnp.concatenate([s[j:j + 1, :] for j in range(n)], axis=1)

        def update(i, idx, g):
            wr, mr, vr = (refs[DEPTH + 3 * i + t] for t in range(3))
            d, m2, v2 = _adam_math(wr[idx], g, mr[idx], vr[idx])
            for o, val in zip(outs[4 * i:4 * i + 4], (g, d, m2, v2)):
                o[idx] = val

        for i, k in enumerate(names):
            n = SMALL_ROWS[k]
            off = _small_offset(k)
            if k == "final_norm":
                update(i, (slice(0, 1),), as_row(DEPTH - 1, off, n))
                continue
            for l in range(DEPTH):
                if k in ("b_w_s", "c_w"):
                    for g in range(4):
                        update(i, (l, g), rows(l, off + g * BLK, BLK))
                elif k in ("b_b_s", "c_scale"):
                    update(i, (l,), rows(l, off, n))
                else:
                    update(i, (slice(l, l + 1),), as_row(l, off, n))

    out_shape = [_sds(w[k].shape, F32) for k in names for _ in range(4)]
    outs = pl.pallas_call(body, name=name, out_shape=out_shape,
                          compiler_params=pltpu.CompilerParams(vmem_limit_bytes=48 << 20), interpret=False)(*ins)
    return tuple({k: outs[4 * i + t] for i, k in enumerate(names)} for t in range(4))


PG, PD = 0, 2


def _ffn_fwd(x, h, wf, tag, mid, next_gain):
    g_pre, u_pre, act = _gu_swiglu(h, wf, tm=2048, tn=256, b_row0=PG * FF, name=f"gu_{tag}")
    token = mid(act)
    out = _mm(act, wf, mode="nn", tm=512, tn=D, tk=FF, n=D, b_row0=PD * FF, out_dtype=F32, scale=0.5, res=x, after=token,
              norm_gain=next_gain, name=f"down_{tag}")
    out, h_next = out if next_gain is not None else (out, None)
    return out, h_next, (x, h, g_pre, u_pre, act)


def _ffn_bwd(dout, saved, gain, wf, tag, after, emit):
    x, h, g_pre, u_pre, act = saved
    grf = lax.empty((3, NDEV, FFS, D), BF16)
    dgu = _dact_dswiglu(dout, wf, g_pre, u_pre, tm=256, b_row0=PD * FF, after=after, name=f"dgu_{tag}")
    grf = _wgrad(act, dout, grf, piece0=PD, rows=FFS, tm=FF // 2, tn=D, tk=1024, scale=0.5, name=f"wg_down_{tag}")
    grf = _wgrad(dgu, h, grf, piece0=PG, rows=FFS, tm=FF // 2, tn=D, tk=2048, name=f"wg_gu_{tag}")
    token = emit([grf])
    dx, dgain = _mm_drms(dgu, wf, x, gain, dout, tm=256, b_row0=PG * FF, after=token, name=f"dx_{tag}")
    return dx, dgain, token


def _mixer_fwd(x, h, sp, ww, wo, wa, wbc, tag, mid, next_gain):
    zq = _mm(h, ww, mode="nt", tm=1024, tn=1152, tk=D, n=QKV_W, out_dtype=F32, name=f"zq_{tag}")
    z = _mm(h, ww, mode="nt", tm=1024, tn=1152, tk=D, n=INW - QKV_W, b_row0=QKV_W, out_dtype=BF16, name=f"z_{tag}")
    token = mid(z)
    acc, ml = _attn_fwd(zq, None, None, gi=0, first=True, final=False, name=f"attn0_{tag}")
    acc, ml = _attn_fwd(zq, acc, ml, gi=1, first=False, final=False, name=f"attn1_{tag}")
    ya, lse = _attn_fwd(zq, acc, ml, gi=2, first=False, final=True, name=f"attn2_{tag}")
    yb = _gmlp_fwd(z, sp["b_ln_g"], sp["b_ln_b"], sp["b_w_s"], sp["b_b_s"], f"gmlp_{tag}")
    yc = _pool_fwd(z, sp["c_w"], sp["c_scale"], f"pool_{tag}")
    pa = _mm(ya, wa, mode="nt", tm=1024, tn=D, tk=GW, n=D, out_dtype=BF16, after=token, name=f"proj_a_{tag}")
    pb = _mm(yb, wbc, mode="nt", tm=1024, tn=D, tk=BW, n=D, b_row0=0, out_dtype=BF16, name=f"proj_b_{tag}")
    pc = _mm(yc, wbc, mode="nt", tm=1024, tn=D, tk=BW, n=D, b_row0=D, out_dtype=BF16, name=f"proj_c_{tag}")
    merged = _gates_fwd(z, sp["b_gate"], pa, pb, pc, f"gates_{tag}")
    out, h_next = _mm(merged, wo, mode="nn", tm=1024, tn=D, tk=D, n=D, out_dtype=F32, res=x, norm_gain=next_gain,
                      name=f"out_{tag}")
    return out, h_next, (x, h, zq, z, ya, lse, yb, yc, pa, pb, pc, merged)


def _mixer_bwd(dout, saved, sp, ww, wo, wa, wbc, tag, after, emit):
    x, h, zq, z, ya, lse, yb, yc, pa, pb, pc, merged = saved
    grw = lax.empty((1, NDEV, INS, D), BF16)
    gro = lax.empty((1, NDEV, DS, D), BF16)
    gra = lax.empty((1, NDEV, DS, GW), BF16)
    grbc = lax.empty((2, NDEV, DS, BW), BF16)
    dm = _mm(dout, wo, mode="nt", tm=1024, tn=D, tk=D, n=D, out_dtype=F32, after=after, name=f"dmerged_{tag}")
    gro = _wgrad(merged, dout, gro, piece0=0, rows=DS, tm=D, tn=D, tk=1024, name=f"wg_out_{tag}")
    dpa, dpb, dpc, dza, dzb, dzc, dba, dbb, dbc = _gates_bwd(z, sp["b_gate"], pa, pb, pc, dm, f"dgates_{tag}")
    dya = _mm(dpa, wa, mode="nn", tm=2048, tn=GW, tk=D, n=GW, out_dtype=F32, name=f"dya_{tag}")
    dyb = _mm(dpb, wbc, mode="nn", tm=2048, tn=BW, tk=D, n=BW, b_row0=0, out_dtype=F32, name=f"dyb_{tag}")
    dyc = _mm(dpc, wbc, mode="nn", tm=2048, tn=BW, tk=D, n=BW, b_row0=D, out_dtype=F32, name=f"dyc_{tag}")
    gra = _wgrad(dpa, ya, gra, piece0=0, rows=DS, tm=D, tn=GW, tk=1024, name=f"wg_pa_{tag}")
    grbc = _wgrad(dpb, yb, grbc, piece0=0, rows=DS, tm=D, tn=BW, tk=1024, name=f"wg_pb_{tag}")
    grbc = _wgrad(dpc, yc, grbc, piece0=1, rows=DS, tm=D, tn=BW, tk=1024, name=f"wg_pc_{tag}")
    dqkv = [_attn_bwd(zq, dya, ya, lse, gi=gi, name=f"dattn{gi}_{tag}") for gi in range(3)]
    duv, dws, dbs, dlng, dlnb = _gmlp_bwd(z, dyb, sp["b_ln_g"], sp["b_ln_b"], sp["b_w_s"], sp["b_b_s"], f"dgmlp_{tag}")
    dxc, dcw, dcs = _pool_bwd(z, dyc, sp["c_w"], sp["c_scale"], f"dpool_{tag}")
    dz = _concat_bf16([dqkv[gi][part] for part in range(3) for gi in range(3)] + [duv, dxc, dza, dzb, dzc], f"dz_{tag}")
    grw = _wgrad(dz, h, grw, piece0=0, rows=INS, tm=INW // 2, tn=512, tk=1024, name=f"wg_in_{tag}")
    token = emit([grw, gro, gra, grbc])
    dx, dgain = _mm_drms(dz, ww, x, sp["mix_norm"], dout, tm=256, b_row0=0, after=token, name=f"dx_mix_{tag}")
    small = {"mix_norm": dgain, "b_gate": [dba, dbb, dbc], "b_ln_g": dlng, "b_ln_b": dlnb,
             "b_w_s": dws, "b_b_s": dbs, "c_w": dcw, "c_scale": dcs}
    return dx, small, token


PARTS = ("f1", "mix", "f2")


def _forward(xs, tgt, weights_of, pass_on, small, final_gain):
    saved, views = [], []
    cur = xs
    h = _rms_fwd(xs, small[0]["ffn1_norm"], "rms_first")
    for l in range(DEPTH):
        last = l + 1 == DEPTH
        (wf1,) = weights_of(l, "f1", cur)
        cur, h, s1 = _ffn_fwd(cur, h, wf1, f"f1l{l}", lambda t: pass_on(l, "mix", t), small[l]["mix_norm"])
        wmix = weights_of(l, "mix", cur)
        cur, h, s2 = _mixer_fwd(cur, h, small[l], *wmix, f"l{l}", lambda t: pass_on(l, "f2", t), small[l]["ffn2_norm"])
        (wf2,) = weights_of(l, "f2", cur)
        cur, h, s3 = _ffn_fwd(cur, h, wf2, f"f2l{l}", lambda t: None if last else pass_on(l + 1, "f1", t),
                              None if last else small[l + 1]["ffn1_norm"])
        saved.append((s1, s2, s3))
        views.append((wf1, wmix, wf2))
    loss_part, dcur, dfinal = _final_loss(cur, final_gain, tgt, "final_loss")
    return loss_part, dcur, dfinal, saved, views


def _backward_layer(dcur, saved_l, small_l, views_l, l, emit, after):
    wf1, wmix, wf2 = views_l
    s1, s2, s3 = saved_l
    dcur, dn2, token = _ffn_bwd(dcur, s3, small_l["ffn2_norm"], wf2, f"f2l{l}", after, lambda g: emit("f2", g))
    dcur, sg, token = _mixer_bwd(dcur, s2, small_l, *wmix, f"l{l}", token, lambda g: emit("mix", g))
    dcur, dn1, token = _ffn_bwd(dcur, s1, small_l["ffn1_norm"], wf1, f"f1l{l}", token, lambda g: emit("f1", g))
    sg["ffn1_norm"] = dn1
    sg["ffn2_norm"] = dn2
    return dcur, sg, token


BIG = ("ffn1_w_gate", "ffn1_w_up", "ffn1_w_down", "w_in", "w_proj_a", "w_proj_b", "w_proj_c", "w_out",
       "ffn2_w_gate", "ffn2_w_up", "ffn2_w_down")
ALL_WEIGHTS = ("ffn1_norm", "ffn1_w_gate", "ffn1_w_up", "ffn1_w_down", "mix_norm", "w_in", "b_gate", "b_ln_g", "b_ln_b",
               "b_w_s", "b_b_s", "c_w", "c_scale", "w_proj_a", "w_proj_b", "w_proj_c", "w_out", "ffn2_norm", "ffn2_w_gate",
               "ffn2_w_up", "ffn2_w_down", "final_norm")


def _place_shards(w, l, part, me, after, name):
    tr = lambda a: a.transpose(0, 2, 1)
    if part == "mix":
        srcs = [tr(w["w_in"]), w["w_out"], tr(w["w_proj_a"]), tr(w["w_proj_b"]), tr(w["w_proj_c"])]
        groups = [[0], [1], [2], [3, 4]]
    else:
        f = "ffn1" if part == "f1" else "ffn2"
        srcs = [tr(w[f + "_w_gate"]), tr(w[f + "_w_up"]), w[f + "_w_down"]]
        groups = [[0, 1, 2]]

    def body(me_ref, *refs):
        del me_ref
        outs = refs[len(srcs) + 1:]
        for o_ref, members in zip(outs, groups):
            for p, i in enumerate(members):
                o_ref[p, 0] = refs[i][0].astype(BF16)

    out_shape = [_sds((len(g), NDEV) + srcs[g[0]].shape[1:], BF16) for g in groups]
    grid_spec = pltpu.PrefetchScalarGridSpec(
        num_scalar_prefetch=1, grid=(1,),
        in_specs=[_bs((1,) + s.shape[1:], lambda i, me_ref: (l, 0, 0)) for s in srcs] + [pl.BlockSpec(memory_space=pl.ANY)],
        out_specs=[_bs((len(g), 1) + srcs[g[0]].shape[1:], lambda i, me_ref: (0, me_ref[0], 0, 0)) for g in groups])
    return pl.pallas_call(body, name=name, grid_spec=grid_spec, out_shape=out_shape,
                          compiler_params=pltpu.CompilerParams(vmem_limit_bytes=40 << 20), interpret=False)(me, *srcs, after)


def _part_views(part, gathered):
    if part == "mix":
        pw, po, pa, pbc = gathered
        return [pw.reshape(INW, D), po.reshape(D, D), pa.reshape(D, GW), pbc.reshape(2 * D, BW)]
    return [gathered[0].reshape(3 * FF, D)]


def _part_pieces(part):
    if part == "mix":
        return [(0, 0, "w_in", True), (1, 0, "w_out", False), (2, 0, "w_proj_a", True), (3, 0, "w_proj_b", True),
                (3, 1, "w_proj_c", True)]
    f = "ffn1" if part == "f1" else "ffn2"
    return [(0, 0, f + "_w_gate", True), (0, 1, f + "_w_up", True), (0, 2, f + "_w_down", False)]


def _layer_small(w, l):
    r = lambda a, shape: a.reshape(shape)
    return {"ffn1_norm": r(w["ffn1_norm"][l], (1, D)), "mix_norm": r(w["mix_norm"][l], (1, D)),
            "b_gate": r(w["b_gate"][l], (1, 3 * D)), "b_ln_g": r(w["b_ln_g"][l], (1, BW)), "b_ln_b": r(w["b_ln_b"][l], (1, BW)),
            "b_w_s": w["b_w_s"][l], "b_b_s": r(w["b_b_s"][l], (4, BLK, 1)), "c_w": w["c_w"][l],
            "c_scale": r(w["c_scale"][l], (4, 1, BLK)), "ffn2_norm": r(w["ffn2_norm"][l], (1, D))}


def kernel(x, ffn1_norm, ffn1_w_gate, ffn1_w_up, ffn1_w_down, mix_norm, w_in, b_gate, b_ln_g, b_ln_b, b_w_s, b_b_s, c_w, c_scale, w_proj_a, w_proj_b, w_proj_c, w_out, ffn2_norm, ffn2_w_gate, ffn2_w_up, ffn2_w_down, final_norm, loss_target, m_ffn1_norm, m_ffn1_w_gate, m_ffn1_w_up, m_ffn1_w_down, m_mix_norm, m_w_in, m_b_gate, m_b_ln_g, m_b_ln_b, m_b_w_s, m_b_b_s, m_c_w, m_c_scale, m_w_proj_a, m_w_proj_b, m_w_proj_c, m_w_out, m_ffn2_norm, m_ffn2_w_gate, m_ffn2_w_up, m_ffn2_w_down, m_final_norm, v_ffn1_norm, v_ffn1_w_gate, v_ffn1_w_up, v_ffn1_w_down, v_mix_norm, v_w_in, v_b_gate, v_b_ln_g, v_b_ln_b, v_b_w_s, v_b_b_s, v_c_w, v_c_scale, v_w_proj_a, v_w_proj_b, v_w_proj_c, v_w_out, v_ffn2_norm, v_ffn2_w_gate, v_ffn2_w_up, v_ffn2_w_down, v_final_norm):
    w = dict(ffn1_norm=ffn1_norm, ffn1_w_gate=ffn1_w_gate, ffn1_w_up=ffn1_w_up, ffn1_w_down=ffn1_w_down, mix_norm=mix_norm,
             w_in=w_in, b_gate=b_gate, b_ln_g=b_ln_g, b_ln_b=b_ln_b, b_w_s=b_w_s, b_b_s=b_b_s, c_w=c_w, c_scale=c_scale,
             w_proj_a=w_proj_a, w_proj_b=w_proj_b, w_proj_c=w_proj_c, w_out=w_out, ffn2_norm=ffn2_norm, ffn2_w_gate=ffn2_w_gate,
             ffn2_w_up=ffn2_w_up, ffn2_w_down=ffn2_w_down, final_norm=final_norm)
    mom = dict(ffn1_norm=m_ffn1_norm, ffn1_w_gate=m_ffn1_w_gate, ffn1_w_up=m_ffn1_w_up, ffn1_w_down=m_ffn1_w_down,
               mix_norm=m_mix_norm, w_in=m_w_in, b_gate=m_b_gate, b_ln_g=m_b_ln_g, b_ln_b=m_b_ln_b, b_w_s=m_b_w_s, b_b_s=m_b_b_s,
               c_w=m_c_w, c_scale=m_c_scale, w_proj_a=m_w_proj_a, w_proj_b=m_w_proj_b, w_proj_c=m_w_proj_c, w_out=m_w_out,
               ffn2_norm=m_ffn2_norm, ffn2_w_gate=m_ffn2_w_gate, ffn2_w_up=m_ffn2_w_up, ffn2_w_down=m_ffn2_w_down,
               final_norm=m_final_norm)
    var = dict(ffn1_norm=v_ffn1_norm, ffn1_w_gate=v_ffn1_w_gate, ffn1_w_up=v_ffn1_w_up, ffn1_w_down=v_ffn1_w_down,
               mix_norm=v_mix_norm, w_in=v_w_in, b_gate=v_b_gate, b_ln_g=v_b_ln_g, b_ln_b=v_b_ln_b, b_w_s=v_b_w_s, b_b_s=v_b_b_s,
               c_w=v_c_w, c_scale=v_c_scale, w_proj_a=v_w_proj_a, w_proj_b=v_w_proj_b, w_proj_c=v_w_proj_c, w_out=v_w_out,
               ffn2_norm=v_ffn2_norm, ffn2_w_gate=v_ffn2_w_gate, ffn2_w_up=v_ffn2_w_up, ffn2_w_down=v_ffn2_w_down,
               final_norm=v_final_norm)

    t_dim = x.shape[1]
    xs = x.reshape(t_dim, D)
    tgt = loss_target.reshape(t_dim, D)

    me = 4 * lax.axis_index("x") + 2 * lax.axis_index("y") + lax.axis_index("c")
    me_arr = me.astype(jnp.int32).reshape(1)
    gathers = {}
    token = jnp.zeros((8, 128), F32)
    for l in range(DEPTH):
        for part in PARTS:
            lands = _place_shards(w, l, part, me_arr, token, f"place_{part}_l{l}")
            gathers[l, part] = _gather_start(lands, token, f"gather_start_{part}_l{l}")
            token = gathers[l, part][-1]

    def pass_on(l, part, after):
        gathers[l, part] = _gather_forward(gathers[l, part], after, f"gather_forward_{part}_l{l}")
        return gathers[l, part][-1]

    pass_on(0, "f1", token)

    def weights_of(l, part, after):
        return _part_views(part, _gather_wait(gathers[l, part], after, f"gather_wait_{part}_l{l}"))

    small = [_layer_small(w, l) for l in range(DEPTH)]
    loss_part, dcur, dfinal, saved, views = _forward(xs, tgt, weights_of, pass_on, small, w["final_norm"].reshape(1, D))
    loss = lax.psum(loss_part[0, 0], ("x", "y", "c"))

    scatters = []
    small_gathers = [None] * DEPTH
    token = None
    for l in reversed(range(DEPTH)):
        def emit(part, grads, l=l):
            scatters.append((l, part, _exchange_start("scatter", grads, dfinal, f"scatter_start_{part}_l{l}")))
            return scatters[-1][2][-1]
        dcur, small_grads, token = _backward_layer(dcur, saved[l], small[l], views[l], l, emit, token)
        packed = _pack_small_grads(small_grads, dfinal if l == DEPTH - 1 else None, me_arr, f"pack_small_grads_l{l}")
        small_gathers[l] = _exchange_start("gather", [packed], token, f"gather_start_small_l{l}")
        token = small_gathers[l][-1]
    grad_x = dcur.reshape(x.shape)

    done = {}
    behind = token
    for l, part, started in scatters:
        arrs = _exchange_wait("scatter", started, behind, f"scatter_wait_{part}_l{l}")
        na = len(arrs) // 2
        for a, piece, k, transposed in _part_pieces(part):
            flip = transposed and w[k].shape[2] % 128 != 0
            view = (lambda t: t.transpose(0, 2, 1)) if flip else (lambda t: t)
            done[k] = _reduce_update(arrs[na + a], arrs[a], me_arr, piece, view(w[k]), view(mom[k]), view(var[k]), l,
                                     done.get(k), transposed and not flip, f"update_{k}_l{l}")
            behind = done[k][1]
    unflip = lambda k, t: t.transpose(0, 2, 1) if t.shape != w[k].shape else t
    grads, delta, new_m, new_v = ({k: unflip(k, done[k][t]) for k in BIG} for t in range(4))
    gathered = [None] * DEPTH
    for l in reversed(range(DEPTH)):
        gathered[l] = _exchange_wait("gather", small_gathers[l], behind, f"gather_wait_small_l{l}")[0]
        behind = gathered[l]
    as_row = lambda t: dict(t, final_norm=t["final_norm"].reshape(1, D))
    for dst, src in zip((grads, delta, new_m, new_v), _adamw_small(gathered, as_row(w), as_row(mom), as_row(var), "adamw_small")):
        dst.update(src)
        dst["final_norm"] = src["final_norm"].reshape(D)

    return (loss, grad_x, *[grads[k] for k in ALL_WEIGHTS], *[delta[k] for k in ALL_WEIGHTS],
            *[new_m[k] for k in ALL_WEIGHTS], *[new_v[k] for k in ALL_WEIGHTS])
```
